```python
import math
import jax, jax.numpy as jnp
from jax import lax
import numpy as np

D_MODEL = 1024
BATCH = 8
SEQ = 4096
DEPTH = 2

CHUNK = 64
N_META = 16
N_SB_HEADS = 8
SB_HEAD_DIM = 64
SB_WIDTH = N_SB_HEADS * SB_HEAD_DIM
CONV_CHANNELS = 512
CONV_KERNEL = 31
MIX_WIDTH = SB_WIDTH + CONV_CHANNELS
IN_COLS = 3 * SB_WIDTH + 2 * CONV_CHANNELS
D_FF = ((8 * D_MODEL // 3 + 255) // 256) * 256
QUERY_BLOCK = 128
EPS = 1e-6

kernel_name = "hybrid_stickbreak_conformer_conv_block"


def _rmsnorm(x, g):
    xf = x.astype(jnp.float32)
    y = xf * lax.rsqrt(jnp.mean(xf * xf, axis=-1, keepdims=True) + EPS)
    return (y * g.astype(jnp.float32)).astype(x.dtype)


def _layernorm(x, g, b):
    xf = x.astype(jnp.float32)
    mu = jnp.mean(xf, axis=-1, keepdims=True)
    var = jnp.mean(jnp.square(xf - mu), axis=-1, keepdims=True)
    y = (xf - mu) * lax.rsqrt(var + EPS)
    return (y * g.astype(jnp.float32) + b.astype(jnp.float32)).astype(x.dtype)


def _stick_breaking_attention(q, k, v):
    b, l, h, dh = q.shape
    n_blocks = -(-l // QUERY_BLOCK)
    lp = n_blocks * QUERY_BLOCK
    pad = [(0, 0), (0, lp - l), (0, 0), (0, 0)]
    q = jnp.pad(q, pad)
    k = jnp.pad(k, pad)
    v = jnp.pad(v, pad)
    scale = 1.0 / math.sqrt(dh)
    q_blocks = q.reshape(b, n_blocks, QUERY_BLOCK, h, dh).transpose(1, 0, 2, 3, 4)
    starts = jnp.arange(n_blocks, dtype=jnp.int32) * QUERY_BLOCK
    key_pos = jnp.arange(lp, dtype=jnp.int32)

    def one_block(args):
        qi, start = args
        z = jnp.einsum('bqhd,bkhd->bhqk', qi, k).astype(jnp.float32) * scale
        t = start + jnp.arange(QUERY_BLOCK, dtype=jnp.int32)
        mask = key_pos[None, :] < t[:, None]
        log_beta = jax.nn.log_sigmoid(z)
        log_1m_beta = jnp.where(mask, jax.nn.log_sigmoid(-z), 0.0)
        rev = lax.cumsum(log_1m_beta, axis=3, reverse=True)
        excl = jnp.concatenate([rev[..., 1:], jnp.zeros_like(rev[..., :1])], axis=-1)
        w = jnp.where(mask, jnp.exp(log_beta + excl), 0.0)
        return jnp.einsum('bhqk,bkhd->bqhd', w.astype(v.dtype), v)

    out = lax.map(one_block, (q_blocks, starts))
    out = out.transpose(1, 0, 2, 3, 4).reshape(b, lp, h, dh)
    return out[:, :l]


def _conformer_conv(a, gate, dw_w, dw_b, ln_g, ln_b):
    u = a * jax.nn.sigmoid(gate)
    c = u.shape[-1]
    u = lax.conv_general_dilated(
        u, dw_w.astype(u.dtype)[:, None, :], window_strides=(1,),
        padding=[(CONV_KERNEL - 1, 0)],
        dimension_numbers=('NWC', 'WIO', 'NWC'), feature_group_count=c)
    u = u + dw_b.astype(u.dtype)
    u = _layernorm(u, ln_g, ln_b)
    return jax.nn.silu(u)


def _fwd_setup_inputs(seed: int = 0) -> dict:
    key = jax.random.key(seed)
    ks = jax.random.split(key, 16)
    f32 = jnp.float32
    nrm = lambda k, shape, s: jax.random.normal(k, shape, f32) * s
    return {
        "x": jax.random.normal(ks[0], (BATCH, SEQ, D_MODEL), f32),
        "meta_tokens": nrm(ks[1], (N_META, D_MODEL), 1.0),
        "mix_norm_g": 1.0 + nrm(ks[2], (DEPTH, D_MODEL), 0.02),
        "w_in": nrm(ks[3], (DEPTH, D_MODEL, IN_COLS), D_MODEL ** -0.5),
        "conv_dw_w": nrm(ks[4], (DEPTH, CONV_KERNEL, CONV_CHANNELS), CONV_KERNEL ** -0.5),
        "conv_dw_b": nrm(ks[5], (DEPTH, CONV_CHANNELS), 0.02),
        "conv_ln_g": 1.0 + nrm(ks[6], (DEPTH, CONV_CHANNELS), 0.02),
        "conv_ln_b": nrm(ks[7], (DEPTH, CONV_CHANNELS), 0.02),
        "w_out": nrm(ks[8], (DEPTH, MIX_WIDTH, D_MODEL), MIX_WIDTH ** -0.5),
        "ffn_norm_g": 1.0 + nrm(ks[9], (DEPTH, D_MODEL), 0.02),
        "w_gate": nrm(ks[10], (DEPTH, D_MODEL, D_FF), D_MODEL ** -0.5),
        "w_up": nrm(ks[11], (DEPTH, D_MODEL, D_FF), D_MODEL ** -0.5),
        "w_down": nrm(ks[12], (DEPTH, D_FF, D_MODEL), D_FF ** -0.5),
        "final_norm_g": 1.0 + nrm(ks[13], (D_MODEL,), 0.02),
    }


def _fwd_reference(x, meta_tokens, mix_norm_g, w_in, conv_dw_w, conv_dw_b, conv_ln_g, conv_ln_b,
              w_out, ffn_norm_g, w_gate, w_up, w_down, final_norm_g):
    b = x.shape[0]
    meta = jnp.broadcast_to(meta_tokens.astype(x.dtype)[None], (b, N_META, D_MODEL))
    h = jnp.concatenate([meta, x], axis=1)
    l = h.shape[1]
    for i in range(DEPTH):
        hn = _rmsnorm(h, mix_norm_g[i])
        proj = jnp.einsum('bld,dc->blc', hn, w_in[i])
        q, k, v, ca, cg = jnp.split(
            proj, [SB_WIDTH, 2 * SB_WIDTH, 3 * SB_WIDTH, 3 * SB_WIDTH + CONV_CHANNELS], axis=-1)
        heads = lambda t: t.reshape(b, l, N_SB_HEADS, SB_HEAD_DIM)
        attn = _stick_breaking_attention(heads(q), heads(k), heads(v)).reshape(b, l, SB_WIDTH)
        conv = _conformer_conv(ca, cg, conv_dw_w[i], conv_dw_b[i], conv_ln_g[i], conv_ln_b[i])
        mixed = jnp.concatenate([attn, conv], axis=-1)
        h = h + jnp.einsum('blc,cd->bld', mixed, w_out[i])
        hn = _rmsnorm(h, ffn_norm_g[i])
        g = jnp.einsum('bld,df->blf', hn, w_gate[i])
        u = jnp.einsum('bld,df->blf', hn, w_up[i])
        h = h + jnp.einsum('blf,fd->bld', jax.nn.silu(g) * u, w_down[i])
    h = _rmsnorm(h, final_norm_g)
    return h[:, N_META:]


import jax as _jax
import jax.numpy as _jnp

TWIN_FORMAT = 'train_step'
FWD_PARAMS = ['x', 'meta_tokens', 'mix_norm_g', 'w_in', 'conv_dw_w', 'conv_dw_b', 'conv_ln_g', 'conv_ln_b', 'w_out', 'ffn_norm_g', 'w_gate', 'w_up', 'w_down', 'final_norm_g']
TWIN_WEIGHTS = ['meta_tokens', 'mix_norm_g', 'w_in', 'conv_dw_w', 'conv_dw_b', 'conv_ln_g', 'conv_ln_b', 'w_out', 'ffn_norm_g', 'w_gate', 'w_up', 'w_down', 'final_norm_g']
TWIN_DIFF_INPUT = 'x'
TWIN_INPUTS = ['x', 'meta_tokens', 'mix_norm_g', 'w_in', 'conv_dw_w', 'conv_dw_b', 'conv_ln_g', 'conv_ln_b', 'w_out', 'ffn_norm_g', 'w_gate', 'w_up', 'w_down', 'final_norm_g', 'loss_target', 'm_meta_tokens', 'm_mix_norm_g', 'm_w_in', 'm_conv_dw_w', 'm_conv_dw_b', 'm_conv_ln_g', 'm_conv_ln_b', 'm_w_out', 'm_ffn_norm_g', 'm_w_gate', 'm_w_up', 'm_w_down', 'm_final_norm_g', 'v_meta_tokens', 'v_mix_norm_g', 'v_w_in', 'v_conv_dw_w', 'v_conv_dw_b', 'v_conv_ln_g', 'v_conv_ln_b', 'v_w_out', 'v_ffn_norm_g', 'v_w_gate', 'v_w_up', 'v_w_down', 'v_final_norm_g']
TWIN_OUTPUTS = ['loss', 'grad_x', 'grad_meta_tokens', 'grad_mix_norm_g', 'grad_w_in', 'grad_conv_dw_w', 'grad_conv_dw_b', 'grad_conv_ln_g', 'grad_conv_ln_b', 'grad_w_out', 'grad_ffn_norm_g', 'grad_w_gate', 'grad_w_up', 'grad_w_down', 'grad_final_norm_g', 'delta_meta_tokens', 'delta_mix_norm_g', 'delta_w_in', 'delta_conv_dw_w', 'delta_conv_dw_b', 'delta_conv_ln_g', 'delta_conv_ln_b', 'delta_w_out', 'delta_ffn_norm_g', 'delta_w_gate', 'delta_w_up', 'delta_w_down', 'delta_final_norm_g', 'new_m_meta_tokens', 'new_m_mix_norm_g', 'new_m_w_in', 'new_m_conv_dw_w', 'new_m_conv_dw_b', 'new_m_conv_ln_g', 'new_m_conv_ln_b', 'new_m_w_out', 'new_m_ffn_norm_g', 'new_m_w_gate', 'new_m_w_up', 'new_m_w_down', 'new_m_final_norm_g', 'new_v_meta_tokens', 'new_v_mix_norm_g', 'new_v_w_in', 'new_v_conv_dw_w', 'new_v_conv_dw_b', 'new_v_conv_ln_g', 'new_v_conv_ln_b', 'new_v_w_out', 'new_v_ffn_norm_g', 'new_v_w_gate', 'new_v_w_up', 'new_v_w_down', 'new_v_final_norm_g']
TWIN_LEAF_KINDS = {'loss': 'loss', 'grad_x': 'grad_x', 'grad_meta_tokens': 'grad_w', 'grad_mix_norm_g': 'grad_w', 'grad_w_in': 'grad_w', 'grad_conv_dw_w': 'grad_w', 'grad_conv_dw_b': 'grad_w', 'grad_conv_ln_g': 'grad_w', 'grad_conv_ln_b': 'grad_w', 'grad_w_out': 'grad_w', 'grad_ffn_norm_g': 'grad_w', 'grad_w_gate': 'grad_w', 'grad_w_up': 'grad_w', 'grad_w_down': 'grad_w', 'grad_final_norm_g': 'grad_w', 'delta_meta_tokens': 'delta_w', 'delta_mix_norm_g': 'delta_w', 'delta_w_in': 'delta_w', 'delta_conv_dw_w': 'delta_w', 'delta_conv_dw_b': 'delta_w', 'delta_conv_ln_g': 'delta_w', 'delta_conv_ln_b': 'delta_w', 'delta_w_out': 'delta_w', 'delta_ffn_norm_g': 'delta_w', 'delta_w_gate': 'delta_w', 'delta_w_up': 'delta_w', 'delta_w_down': 'delta_w', 'delta_final_norm_g': 'delta_w', 'new_m_meta_tokens': 'new_m', 'new_m_mix_norm_g': 'new_m', 'new_m_w_in': 'new_m', 'new_m_conv_dw_w': 'new_m', 'new_m_conv_dw_b': 'new_m', 'new_m_conv_ln_g': 'new_m', 'new_m_conv_ln_b': 'new_m', 'new_m_w_out': 'new_m', 'new_m_ffn_norm_g': 'new_m', 'new_m_w_gate': 'new_m', 'new_m_w_up': 'new_m', 'new_m_w_down': 'new_m', 'new_m_final_norm_g': 'new_m', 'new_v_meta_tokens': 'new_v', 'new_v_mix_norm_g': 'new_v', 'new_v_w_in': 'new_v', 'new_v_conv_dw_w': 'new_v', 'new_v_conv_dw_b': 'new_v', 'new_v_conv_ln_g': 'new_v', 'new_v_conv_ln_b': 'new_v', 'new_v_w_out': 'new_v', 'new_v_ffn_norm_g': 'new_v', 'new_v_w_gate': 'new_v', 'new_v_w_up': 'new_v', 'new_v_w_down': 'new_v', 'new_v_final_norm_g': 'new_v'}


def _forward(args):
    return _fwd_reference(*[args[k] for k in FWD_PARAMS])


def _output_shape():
    out = _jax.eval_shape(lambda: _forward(_fwd_setup_inputs(0)))
    return out.shape, out.dtype

N_MICROBATCH = 1
ADAM_LR = 0.001
ADAM_B1 = 0.9
ADAM_B2 = 0.999
ADAM_EPS = 1e-08
ADAM_WD = 0.01
ADAM_STEP = 10
PER_EXAMPLE_BATCH_AXIS = {'x': 0, 'loss_target': 0}
SHARED_INPUTS = []
_WEIGHT_DTYPES = {'meta_tokens': _jnp.float32, 'mix_norm_g': _jnp.float32, 'w_in': _jnp.float32, 'conv_dw_w': _jnp.float32, 'conv_dw_b': _jnp.float32, 'conv_ln_g': _jnp.float32, 'conv_ln_b': _jnp.float32, 'w_out': _jnp.float32, 'ffn_norm_g': _jnp.float32, 'w_gate': _jnp.float32, 'w_up': _jnp.float32, 'w_down': _jnp.float32, 'final_norm_g': _jnp.float32}
MOMENT_SCALE = {'meta_tokens': 6.965905e-03, 'mix_norm_g': 1.201930e-01, 'w_in': 7.254007e-02, 'conv_dw_w': 9.908726e-02, 'conv_dw_b': 1.820229e-01, 'conv_ln_g': 1.125614e-01, 'conv_ln_b': 1.013209e-01, 'w_out': 1.005076e-01, 'ffn_norm_g': 1.157649e-01, 'w_gate': 4.884245e-02, 'w_up': 4.727900e-02, 'w_down': 7.833299e-02, 'final_norm_g': 3.197457e+01}


def _to_microbatches(a, axis):
    t = _jnp.moveaxis(a, axis, 0)
    t = t.reshape((N_MICROBATCH, t.shape[0] // N_MICROBATCH) + t.shape[1:])
    return _jnp.moveaxis(t, 1, axis + 1)


def setup_inputs(seed: int = 0) -> dict:
    inp = _fwd_setup_inputs(seed)
    key = _jax.random.fold_in(_jax.random.key(seed), 7919)
    shape, _ = _output_shape()
    out = dict(inp)
    out["loss_target"] = _jax.random.normal(_jax.random.fold_in(key, 0), shape, _jnp.float32)
    for i, name in enumerate(TWIN_WEIGHTS):
        w = inp[name].astype(_jnp.float32)
        if MOMENT_SCALE is None:
            s = _jnp.sqrt(_jnp.mean(_jnp.square(w)) + 1e-30)
        else:
            s = MOMENT_SCALE[name]
        km, kv = _jax.random.split(_jax.random.fold_in(key, i + 1))
        out[name] = w
        out["m_" + name] = s * _jax.random.normal(km, w.shape, _jnp.float32)
        out["v_" + name] = (s * s) * _jax.random.uniform(kv, w.shape, _jnp.float32, 0.5, 1.5)
    if N_MICROBATCH > 1:
        for name, axis in PER_EXAMPLE_BATCH_AXIS.items():
            out[name] = _to_microbatches(out[name], axis)
    return {'x': out['x'], 'meta_tokens': out['meta_tokens'], 'mix_norm_g': out['mix_norm_g'], 'w_in': out['w_in'], 'conv_dw_w': out['conv_dw_w'], 'conv_dw_b': out['conv_dw_b'], 'conv_ln_g': out['conv_ln_g'], 'conv_ln_b': out['conv_ln_b'], 'w_out': out['w_out'], 'ffn_norm_g': out['ffn_norm_g'], 'w_gate': out['w_gate'], 'w_up': out['w_up'], 'w_down': out['w_down'], 'final_norm_g': out['final_norm_g'], 'loss_target': out['loss_target'], 'm_meta_tokens': out['m_meta_tokens'], 'm_mix_norm_g': out['m_mix_norm_g'], 'm_w_in': out['m_w_in'], 'm_conv_dw_w': out['m_conv_dw_w'], 'm_conv_dw_b': out['m_conv_dw_b'], 'm_conv_ln_g': out['m_conv_ln_g'], 'm_conv_ln_b': out['m_conv_ln_b'], 'm_w_out': out['m_w_out'], 'm_ffn_norm_g': out['m_ffn_norm_g'], 'm_w_gate': out['m_w_gate'], 'm_w_up': out['m_w_up'], 'm_w_down': out['m_w_down'], 'm_final_norm_g': out['m_final_norm_g'], 'v_meta_tokens': out['v_meta_tokens'], 'v_mix_norm_g': out['v_mix_norm_g'], 'v_w_in': out['v_w_in'], 'v_conv_dw_w': out['v_conv_dw_w'], 'v_conv_dw_b': out['v_conv_dw_b'], 'v_conv_ln_g': out['v_conv_ln_g'], 'v_conv_ln_b': out['v_conv_ln_b'], 'v_w_out': out['v_w_out'], 'v_ffn_norm_g': out['v_ffn_norm_g'], 'v_w_gate': out['v_w_gate'], 'v_w_up': out['v_w_up'], 'v_w_down': out['v_w_down'], 'v_final_norm_g': out['v_final_norm_g']}


def _loss(weights, diff, rest, loss_target):
    with _jax.named_scope("forward"):
        args = {**rest, TWIN_DIFF_INPUT: diff, **{k: w.astype(_WEIGHT_DTYPES[k]) for k, w in weights.items()}}
        y = _forward(args)
    with _jax.named_scope("loss_head"):
        err = _jnp.square(y.astype(_jnp.float32) - loss_target)
        return 0.5 * _jnp.sum(_jnp.mean(err, axis=-1)) if err.ndim else 0.5 * err


def _adamw(w, g, m, v):
    m = ADAM_B1 * m + (1.0 - ADAM_B1) * g
    v = ADAM_B2 * v + (1.0 - ADAM_B2) * _jnp.square(g)
    m_hat = m / (1.0 - ADAM_B1 ** ADAM_STEP)
    v_hat = v / (1.0 - ADAM_B2 ** ADAM_STEP)
    delta = -ADAM_LR * (m_hat / (_jnp.sqrt(v_hat) + ADAM_EPS) + ADAM_WD * w)
    return delta, m, v


def reference(x, meta_tokens, mix_norm_g, w_in, conv_dw_w, conv_dw_b, conv_ln_g, conv_ln_b, w_out, ffn_norm_g, w_gate, w_up, w_down, final_norm_g, loss_target, m_meta_tokens, m_mix_norm_g, m_w_in, m_conv_dw_w, m_conv_dw_b, m_conv_ln_g, m_conv_ln_b, m_w_out, m_ffn_norm_g, m_w_gate, m_w_up, m_w_down, m_final_norm_g, v_meta_tokens, v_mix_norm_g, v_w_in, v_conv_dw_w, v_conv_dw_b, v_conv_ln_g, v_conv_ln_b, v_w_out, v_ffn_norm_g, v_w_gate, v_w_up, v_w_down, v_final_norm_g):
    given = dict(x=x, meta_tokens=meta_tokens, mix_norm_g=mix_norm_g, w_in=w_in, conv_dw_w=conv_dw_w, conv_dw_b=conv_dw_b, conv_ln_g=conv_ln_g, conv_ln_b=conv_ln_b, w_out=w_out, ffn_norm_g=ffn_norm_g, w_gate=w_gate, w_up=w_up, w_down=w_down, final_norm_g=final_norm_g, loss_target=loss_target, m_meta_tokens=m_meta_tokens, m_mix_norm_g=m_mix_norm_g, m_w_in=m_w_in, m_conv_dw_w=m_conv_dw_w, m_conv_dw_b=m_conv_dw_b, m_conv_ln_g=m_conv_ln_g, m_conv_ln_b=m_conv_ln_b, m_w_out=m_w_out, m_ffn_norm_g=m_ffn_norm_g, m_w_gate=m_w_gate, m_w_up=m_w_up, m_w_down=m_w_down, m_final_norm_g=m_final_norm_g, v_meta_tokens=v_meta_tokens, v_mix_norm_g=v_mix_norm_g, v_w_in=v_w_in, v_conv_dw_w=v_conv_dw_w, v_conv_dw_b=v_conv_dw_b, v_conv_ln_g=v_conv_ln_g, v_conv_ln_b=v_conv_ln_b, v_w_out=v_w_out, v_ffn_norm_g=v_ffn_norm_g, v_w_gate=v_w_gate, v_w_up=v_w_up, v_w_down=v_w_down, v_final_norm_g=v_final_norm_g)
    weights = {n: given[n] for n in TWIN_WEIGHTS}
    shared = {n: given[n] for n in SHARED_INPUTS}
    per_example = {n: given[n] for n in ['x']}
    grad_fn = _jax.value_and_grad(_loss, argnums=(0, 1))

    def one_microbatch(ex, loss_target):
        ex = dict(ex)
        diff = ex.pop(TWIN_DIFF_INPUT)
        return grad_fn(weights, diff, {**shared, **ex}, loss_target)

    if N_MICROBATCH == 1:
        loss, (grad_w, grad_x) = one_microbatch(per_example, given["loss_target"])
    else:
        def body(carry, xs):
            loss_sum, grad_sum = carry
            l_k, (gw_k, gx_k) = one_microbatch(xs[0], xs[1])
            with _jax.named_scope("update"):
                return (loss_sum + l_k, _jax.tree.map(_jnp.add, grad_sum, gw_k)), gx_k

        init = (_jnp.zeros((), _jnp.float32), _jax.tree.map(_jnp.zeros_like, weights))
        (loss, grad_w), grad_x = _jax.lax.scan(body, init, (per_example, given["loss_target"]))
    with _jax.named_scope("update"):
        delta_w, new_m, new_v = {}, {}, {}
        for n in TWIN_WEIGHTS:
            delta_w[n], new_m[n], new_v[n] = _adamw(weights[n], grad_w[n], given["m_" + n], given["v_" + n])
    return (loss, grad_x, *[grad_w[n] for n in TWIN_WEIGHTS], *[delta_w[n] for n in TWIN_WEIGHTS],
            *[new_m[n] for n in TWIN_WEIGHTS], *[new_v[n] for n in TWIN_WEIGHTS])
```

```python
import functools
import math

import jax
import jax.numpy as jnp
from jax import lax
from jax.experimental import pallas as pl
from jax.experimental.pallas import tpu as pltpu

F32 = jnp.float32
BF16 = jnp.bfloat16
MESH = pl.DeviceIdType.MESH

EPS = 1e-6
QUERY_BLOCK = 128
LANES = 128
HEAD_DIM = 64
CONV_HALO = 32
N_CHIPS = 4
N_DEV = 8
VMEM_LIMIT = 56 * 1024 * 1024

ADAM_LR = 0.001
ADAM_B1 = 0.9
ADAM_B2 = 0.999
ADAM_EPS = 1e-08
ADAM_WD = 0.01
ADAM_STEP = 10


def _pick(n, prefs):
    for p in prefs:
        if n % p == 0:
            return p
    return n


def _params(sem=None):
    return pltpu.CompilerParams(dimension_semantics=sem, vmem_limit_bytes=VMEM_LIMIT)


def _sigmoid(x):
    return 1.0 / (1.0 + jnp.exp(-x))


def _rmsnorm_fwd(h, g, name):
    L, D = h.shape
    T = _pick(L, (384, 128))

    def body(h_ref, g_ref, o_ref):
        x = h_ref[...]
        r = lax.rsqrt(jnp.mean(x * x, axis=-1, keepdims=True) + EPS)
        o_ref[...] = (x * r * g_ref[...]).astype(o_ref.dtype)

    return pl.pallas_call(
        body, name=name, grid=(L // T,),
        in_specs=[pl.BlockSpec((T, D), lambda i: (i, 0)), pl.BlockSpec((1, D), lambda i: (0, 0))],
        out_specs=pl.BlockSpec((T, D), lambda i: (i, 0)),
        out_shape=jax.ShapeDtypeStruct((L, D), BF16),
        compiler_params=_params(("parallel",)),
    )(h, g)


def _rmsnorm_bwd(h, g, dy, dh_in, name):
    L, D = h.shape
    T = _pick(L, (384, 128))

    def body(h_ref, g_ref, dy_ref, dhin_ref, dh_ref, dg_ref):
        x = h_ref[...]
        dyv = dy_ref[...]
        r = lax.rsqrt(jnp.mean(x * x, axis=-1, keepdims=True) + EPS)
        xh = x * r
        dxh = dyv * g_ref[...]
        dh_ref[...] = dhin_ref[...] + r * (dxh - xh * jnp.mean(dxh * xh, axis=-1, keepdims=True))

        @pl.when(pl.program_id(0) == 0)
        def _():
            dg_ref[...] = jnp.zeros_like(dg_ref)

        dg_ref[...] += jnp.sum(dyv * xh, axis=0, keepdims=True)

    row = pl.BlockSpec((T, D), lambda i: (i, 0))
    vec = pl.BlockSpec((1, D), lambda i: (0, 0))
    return pl.pallas_call(
        body, name=name, grid=(L // T,),
        in_specs=[row, vec, row, row], out_specs=[row, vec],
        out_shape=[jax.ShapeDtypeStruct((L, D), F32), jax.ShapeDtypeStruct((1, D), F32)],
        compiler_params=_params(("arbitrary",)),
    )(h, g, dy, dh_in)


def _loss_head(h, g, target, n_meta, seq, name):
    L, D = h.shape
    T = _pick(L, (384, 128))

    def body(h_ref, g_ref, t_ref, loss_ref, dh_ref, dg_ref):
        i = pl.program_id(0)
        x = h_ref[...]
        gv = g_ref[...]
        r = lax.rsqrt(jnp.mean(x * x, axis=-1, keepdims=True) + EPS)
        xh = x * r
        y = xh * gv
        rows = i * T + lax.broadcasted_iota(jnp.int32, (T, 1), 0)
        live = (rows >= n_meta) & (rows < n_meta + seq)
        diff = jnp.where(live, y - t_ref[...], 0.0)
        dyv = diff / D
        dxh = dyv * gv
        dh_ref[...] = r * (dxh - xh * jnp.mean(dxh * xh, axis=-1, keepdims=True))

        @pl.when(i == 0)
        def _():
            dg_ref[...] = jnp.zeros_like(dg_ref)
            loss_ref[...] = jnp.zeros_like(loss_ref)

        dg_ref[...] += jnp.sum(dyv * xh, axis=0, keepdims=True)
        per_row = jnp.mean(diff * diff, axis=-1, keepdims=True)
        loss_ref[...] += 0.5 * jnp.sum(per_row, axis=0, keepdims=True)

    row = pl.BlockSpec((T, D), lambda i: (i, 0))
    vec = pl.BlockSpec((1, D), lambda i: (0, 0))
    one = pl.BlockSpec((1, 1), lambda i: (0, 0))
    return pl.pallas_call(
        body, name=name, grid=(L // T,),
        in_specs=[row, vec, row], out_specs=[one, row, vec],
        out_shape=[jax.ShapeDtypeStruct((1, 1), F32), jax.ShapeDtypeStruct((L, D), F32),
                   jax.ShapeDtypeStruct((1, D), F32)],
        compiler_params=_params(("arbitrary",)),
    )(h, g, target)


def _swiglu_fwd(g, u, name):
    L, F = g.shape
    T = _pick(L, (192, 128))

    def body(g_ref, u_ref, o_ref):
        gv = g_ref[...]
        o_ref[...] = (gv * _sigmoid(gv) * u_ref[...]).astype(o_ref.dtype)

    row = pl.BlockSpec((T, F), lambda i: (i, 0))
    return pl.pallas_call(
        body, name=name, grid=(L // T,), in_specs=[row, row], out_specs=row,
        out_shape=jax.ShapeDtypeStruct((L, F), BF16), compiler_params=_params(("parallel",)),
    )(g, u)


def _swiglu_bwd(g, u, dact, name):
    L, F = g.shape
    T = _pick(L, (192, 128))

    def body(g_ref, u_ref, d_ref, dg_ref, du_ref):
        gv = g_ref[...]
        dv = d_ref[...]
        s = _sigmoid(gv)
        du_ref[...] = (dv * (gv * s)).astype(du_ref.dtype)
        dg_ref[...] = (dv * u_ref[...] * (s * (1.0 + gv * (1.0 - s)))).astype(dg_ref.dtype)

    row = pl.BlockSpec((T, F), lambda i: (i, 0))
    return pl.pallas_call(
        body, name=name, grid=(L // T,), in_specs=[row, row, row], out_specs=[row, row],
        out_shape=[jax.ShapeDtypeStruct((L, F), BF16), jax.ShapeDtypeStruct((L, F), BF16)],
        compiler_params=_params(("parallel",)),
    )(g, u, dact)


def _mm_nn(pairs, out_dtype, name, residual=None, cols=None):
    M = pairs[0][0].shape[0]
    col0, N = cols if cols is not None else (0, pairs[0][1].shape[-1])
    tm = _pick(M, (1056, 384, 128))
    tn = _pick(math.gcd(N, col0) if col0 else N, (640, 512, 256, 128))
    jb = col0 // tn
    n = len(pairs)

    def body(*refs):
        a_refs, w_refs = refs[:n], refs[n:2 * n]
        o_ref = refs[-1]
        acc = None
        for a_ref, w_ref in zip(a_refs, w_refs):
            d = jnp.dot(a_ref[...].astype(BF16), w_ref[...], preferred_element_type=F32)
            acc = d if acc is None else acc + d
        if residual is not None:
            acc = acc + refs[2 * n][...]
        o_ref[...] = acc.astype(o_ref.dtype)

    in_specs = [pl.BlockSpec((tm, a.shape[1]), lambda i, j: (i, 0)) for a, _, _, _ in pairs]
    for a, _, layer, kblk in pairs:
        in_specs.append(pl.BlockSpec((None, a.shape[1], tn), functools.partial(lambda i, j, l, kb: (l, kb, j + jb), l=layer, kb=kblk)))
    args = [p[0] for p in pairs] + [p[1] for p in pairs]
    if residual is not None:
        in_specs.append(pl.BlockSpec((tm, tn), lambda i, j: (i, j)))
        args.append(residual)
    return pl.pallas_call(
        body, name=name, grid=(M // tm, N // tn), in_specs=in_specs,
        out_specs=pl.BlockSpec((tm, tn), lambda i, j: (i, j)),
        out_shape=jax.ShapeDtypeStruct((M, N), out_dtype),
        compiler_params=_params(("parallel", "parallel")),
    )(*args)


def _mm_nt(pairs, out_dtype, name):
    M = pairs[0][0].shape[0]
    K = pairs[0][1].shape[1]
    tm = _pick(M, (1056, 384, 128))
    tk = _pick(K, (512, 1408, 256, 128))
    n = len(pairs)

    def body(*refs):
        d_refs, w_refs = refs[:n], refs[n:2 * n]
        o_ref = refs[-1]
        acc = None
        for d_ref, w_ref in zip(d_refs, w_refs):
            d = lax.dot_general(d_ref[...].astype(BF16), w_ref[...], (((1,), (1,)), ((), ())),
                                preferred_element_type=F32)
            acc = d if acc is None else acc + d
        o_ref[...] = acc.astype(o_ref.dtype)

    in_specs = [pl.BlockSpec((tm, d.shape[1]), lambda i, j: (i, 0)) for d, _, _, _ in pairs]
    for d, _, layer, cblk in pairs:
        in_specs.append(pl.BlockSpec((None, tk, d.shape[1]), functools.partial(lambda i, j, l, cb: (l, j, cb), l=layer, cb=cblk)))
    args = [p[0] for p in pairs] + [p[1] for p in pairs]
    return pl.pallas_call(
        body, name=name, grid=(M // tm, K // tk), in_specs=in_specs,
        out_specs=pl.BlockSpec((tm, tk), lambda i, j: (i, j)),
        out_shape=jax.ShapeDtypeStruct((M, K), out_dtype),
        compiler_params=_params(("parallel", "parallel")),
    )(*args)


def _mm_tn(a, b, name, col_sharded):
    M, K = a.shape
    N = b.shape[1]
    tm = _pick(M, (384, 128))
    tk = _pick(K, (512, 1408, 256, 128))
    tn = N // N_CHIPS if col_sharded else _pick(N, (512, 128))

    def body(a_ref, b_ref, o_ref):
        @pl.when(pl.program_id(2) == 0)
        def _():
            o_ref[...] = jnp.zeros_like(o_ref)

        o_ref[...] += lax.dot_general(a_ref[...].astype(BF16), b_ref[...].astype(BF16),
                                      (((0,), (0,)), ((), ())), preferred_element_type=F32)

    if col_sharded:
        out_shape = jax.ShapeDtypeStruct((N_CHIPS, K, tn), F32)
        out_spec = pl.BlockSpec((None, tk, tn), lambda k, j, m: (j, k, 0))
    else:
        out_shape = jax.ShapeDtypeStruct((K, N), F32)
        out_spec = pl.BlockSpec((tk, tn), lambda k, j, m: (k, j))
    out = pl.pallas_call(
        body, name=name, grid=(K // tk, N // tn, M // tm),
        in_specs=[pl.BlockSpec((tm, tk), lambda k, j, m: (m, k)), pl.BlockSpec((tm, tn), lambda k, j, m: (m, j))],
        out_specs=out_spec, out_shape=out_shape,
        compiler_params=_params(("parallel", "parallel", "arbitrary")),
    )(a, b)
    return out if col_sharded else out.reshape(N_CHIPS, K // N_CHIPS, N)


def _stack_heads(x, scale=None):
    lane = lax.broadcasted_iota(jnp.int32, x.shape, 1)
    zero = jnp.zeros_like(x)
    lo = jnp.where(lane < HEAD_DIM, x, zero)
    hi = jnp.where(lane < HEAD_DIM, zero, x)
    out = jnp.concatenate([lo, hi], axis=0)
    return out if scale is None else out * scale


def _unstack_heads(x2):
    qb = x2.shape[0] // 2
    lane = lax.broadcasted_iota(jnp.int32, (qb, LANES), 1)
    return jnp.where(lane < HEAD_DIM, x2[:qb], x2[qb:])


def _tri_and_ones(strict):
    r = lax.broadcasted_iota(jnp.int32, (QUERY_BLOCK, 2 * QUERY_BLOCK), 0)
    c = lax.broadcasted_iota(jnp.int32, (QUERY_BLOCK, 2 * QUERY_BLOCK), 1)
    tri = (r > c) if strict else (r >= c)
    return jnp.where(tri | (c >= QUERY_BLOCK), 1.0, 0.0).astype(BF16)


def _dot2(x, m):
    xh = x.astype(BF16)
    xl = (x - xh.astype(F32)).astype(BF16)
    return jnp.dot(xh, m, preferred_element_type=F32) + jnp.dot(xl, m, preferred_element_type=F32)


def _causal_valid():
    r = lax.broadcasted_iota(jnp.int32, (2 * QUERY_BLOCK, QUERY_BLOCK), 0) & (QUERY_BLOCK - 1)
    c = lax.broadcasted_iota(jnp.int32, (2 * QUERY_BLOCK, QUERY_BLOCK), 1)
    return c < r


def _attn_fwd(qkv, sb_width, name):
    L = qkv.shape[0]
    QB = QUERY_BLOCK
    nb = L // QB
    n_pairs = sb_width // LANES
    scale = 1.0 / math.sqrt(HEAD_DIM)

    def body(q_ref, k_ref, v_ref, o_ref, acc_ref, carry_ref):
        i = pl.program_id(1)
        q2 = _stack_heads(q_ref[...], scale)
        txo = _tri_and_ones(True)
        valid = _causal_valid()
        acc_ref[...] = jnp.zeros_like(acc_ref)
        carry_ref[...] = jnp.zeros_like(carry_ref)

        def step(j, diag):
            start = pl.multiple_of(j * QB, QB)
            kb = k_ref[pl.ds(start, QB), :]
            vb = v_ref[pl.ds(start, QB), :]
            z = lax.dot_general(q2, kb, (((1,), (1,)), ((), ())), preferred_element_type=F32)
            sp = jnp.log(1.0 + jnp.exp(-jnp.abs(z)))
            a = jnp.minimum(z, 0.0) - sp
            b = jnp.minimum(-z, 0.0) - sp
            if diag:
                b = jnp.where(valid, b, 0.0)
            res = _dot2(b, txo)
            w = jnp.exp(a + res[:, :QB] + carry_ref[...])
            if diag:
                w = jnp.where(valid, w, 0.0)
            carry_ref[...] += res[:, QB:]
            acc_ref[...] += _dot2(w, vb)

        step(i, True)

        def older(n, c):
            step(i - 1 - n, False)
            return c

        lax.fori_loop(0, i, older, 0)
        o_ref[...] = _unstack_heads(acc_ref[...])

    return pl.pallas_call(
        body, name=name, grid=(n_pairs, nb),
        in_specs=[pl.BlockSpec((QB, LANES), lambda p, i: (i, p)),
                  pl.BlockSpec((L, LANES), lambda p, i: (0, n_pairs + p)),
                  pl.BlockSpec((L, LANES), lambda p, i: (0, 2 * n_pairs + p))],
        out_specs=pl.BlockSpec((QB, LANES), lambda p, i: (i, p)),
        out_shape=jax.ShapeDtypeStruct((L, sb_width), F32),
        scratch_shapes=[pltpu.VMEM((2 * QB, LANES), F32), pltpu.VMEM((2 * QB, LANES), F32)],
        compiler_params=_params(("parallel", "arbitrary")),
    )(qkv, qkv, qkv)


def _attn_bwd(qkv, o, dmixed, sb_width, name):
    L = qkv.shape[0]
    QB = QUERY_BLOCK
    nb = L // QB
    n_pairs = sb_width // LANES
    scale = 1.0 / math.sqrt(HEAD_DIM)

    def body(q_ref, k_ref, v_ref, o_ref, do_ref, dq_ref, dk_ref, dv_ref,
             dq_acc, dk_acc, dv_acc, ce_ref, cr_ref):
        i = pl.program_id(1)

        @pl.when(i == 0)
        def _():
            dk_acc[...] = jnp.zeros_like(dk_acc)
            dv_acc[...] = jnp.zeros_like(dv_acc)

        q2 = _stack_heads(q_ref[...], scale)
        do2 = _stack_heads(do_ref[...].astype(BF16))
        ov = o_ref[...]
        o2 = jnp.concatenate([ov, ov], axis=0)
        ones = jnp.ones((LANES, LANES), BF16)
        dtot = _dot2(do2.astype(F32) * o2, ones)
        txo = _tri_and_ones(True)
        tio = _tri_and_ones(False)
        valid = _causal_valid()
        dq_acc[...] = jnp.zeros_like(dq_acc)
        ce_ref[...] = jnp.zeros_like(ce_ref)
        cr_ref[...] = jnp.zeros_like(cr_ref)

        def step(j, diag):
            start = pl.multiple_of(j * QB, QB)
            kb = k_ref[pl.ds(start, QB), :]
            vb = v_ref[pl.ds(start, QB), :]
            z = lax.dot_general(q2, kb, (((1,), (1,)), ((), ())), preferred_element_type=F32)
            e = jnp.exp(-jnp.abs(z))
            sp = jnp.log(1.0 + e)
            a = jnp.minimum(z, 0.0) - sp
            b = jnp.minimum(-z, 0.0) - sp
            rinv = 1.0 / (1.0 + e)
            pos = z >= 0.0
            beta = jnp.where(pos, rinv, e * rinv)
            one_m_beta = jnp.where(pos, e * rinv, rinv)
            if diag:
                b = jnp.where(valid, b, 0.0)
            res = _dot2(b, txo)
            w = jnp.exp(a + res[:, :QB] + ce_ref[...])
            if diag:
                w = jnp.where(valid, w, 0.0)
            ce_ref[...] += res[:, QB:]
            dw = lax.dot_general(do2, vb, (((1,), (1,)), ((), ())), preferred_element_type=F32)
            g = w * dw
            res2 = _dot2(g, tio)
            rinc = res2[:, :QB] + cr_ref[...]
            cr_ref[...] += res2[:, QB:]
            dz = g * one_m_beta - beta * (dtot - rinc)
            if diag:
                dz = jnp.where(valid, dz, 0.0)
            dzb = dz.astype(BF16)
            dq_acc[...] += jnp.dot(dzb, kb, preferred_element_type=F32)
            dk_acc[pl.ds(start, QB), :] += lax.dot_general(
                dzb, q2, (((0,), (0,)), ((), ())), preferred_element_type=F32)
            dv_acc[pl.ds(start, QB), :] += lax.dot_general(
                w.astype(BF16), do2, (((0,), (0,)), ((), ())), preferred_element_type=F32)

        step(i, True)

        def older(n, c):
            step(i - 1 - n, False)
            return c

        lax.fori_loop(0, i, older, 0)
        dq_ref[...] = (_unstack_heads(dq_acc[...]) * scale).astype(dq_ref.dtype)

        @pl.when(i == nb - 1)
        def _():
            dk_ref[...] = dk_acc[...].astype(dk_ref.dtype)
            dv_ref[...] = dv_acc[...].astype(dv_ref.dtype)

    blk = pl.BlockSpec((QB, LANES), lambda p, i: (i, p))
    col = pl.BlockSpec((L, LANES), lambda p, i: (0, p))
    return pl.pallas_call(
        body, name=name, grid=(n_pairs, nb),
        in_specs=[blk,
                  pl.BlockSpec((L, LANES), lambda p, i: (0, n_pairs + p)),
                  pl.BlockSpec((L, LANES), lambda p, i: (0, 2 * n_pairs + p)),
                  blk, blk],
        out_specs=[blk, col, col],
        out_shape=[jax.ShapeDtypeStruct((L, sb_width), BF16)] * 3,
        scratch_shapes=[pltpu.VMEM((2 * QB, LANES), F32), pltpu.VMEM((L, LANES), F32),
                        pltpu.VMEM((L, LANES), F32), pltpu.VMEM((2 * QB, LANES), F32),
                        pltpu.VMEM((2 * QB, LANES), F32)],
        compiler_params=_params(("parallel", "arbitrary")),
    )(qkv, qkv, qkv, o, dmixed)


def _conv_tile(L):
    return _pick(L, (384, 128))


def _glu(x, C):
    return x[:, :C] * _sigmoid(x[:, C:])


def _conv_fwd(cacg, dw_w, dw_b, ln_g, ln_b, name):
    L, C2 = cacg.shape
    C = C2 // 2
    T = _conv_tile(L)
    H = CONV_HALO
    K = dw_w.shape[0]
    CH = 64 if T % 64 == 0 else T

    def body(x_ref, prev_ref, w_ref, b_ref, g_ref, beta_ref, o_ref, u_ref):
        i = pl.program_id(0)
        u_ref[0:H, :] = jnp.where(i > 0, _glu(prev_ref[...], C), 0.0)
        u_ref[H:, :] = _glu(x_ref[...], C)
        for c0 in range(0, T, CH):
            y = jnp.broadcast_to(b_ref[...], (CH, C))
            for k in range(K):
                y = y + w_ref[k:k + 1, :] * u_ref[c0 + H - (K - 1) + k:c0 + H - (K - 1) + k + CH, :]
            mu = jnp.mean(y, axis=-1, keepdims=True)
            yc = y - mu
            rstd = lax.rsqrt(jnp.mean(yc * yc, axis=-1, keepdims=True) + EPS)
            ln = yc * rstd * g_ref[...] + beta_ref[...]
            o_ref[c0:c0 + CH, :] = (ln * _sigmoid(ln)).astype(o_ref.dtype)

    vec = pl.BlockSpec((1, C), lambda i: (0, 0))
    return pl.pallas_call(
        body, name=name, grid=(L // T,),
        in_specs=[pl.BlockSpec((T, C2), lambda i: (i, 0)),
                  pl.BlockSpec((H, C2), lambda i: (jnp.maximum(i * (T // H) - 1, 0), 0)),
                  pl.BlockSpec((K, C), lambda i: (0, 0)), vec, vec, vec],
        out_specs=pl.BlockSpec((T, C), lambda i: (i, 0)),
        out_shape=jax.ShapeDtypeStruct((L, C), BF16),
        scratch_shapes=[pltpu.VMEM((T + H, C), F32)],
        compiler_params=_params(("parallel",)),
    )(cacg, cacg, dw_w, dw_b, ln_g, ln_b)


def _conv_bwd(cacg, dmixed, dw_w, dw_b, ln_g, ln_b, name):
    L, C2 = cacg.shape
    C = C2 // 2
    T = _conv_tile(L)
    H = CONV_HALO
    K = dw_w.shape[0]
    nt = L // T
    TE = T + H

    def body(x_ref, prev_ref, next_ref, d_ref, dnext_ref, w_ref, b_ref, g_ref, beta_ref,
             dca_ref, dcg_ref, dwt_ref, db_ref, dg_ref, dbeta_ref, u_ref, dy_ref):
        i = pl.program_id(0)
        last = i == nt - 1
        u_ref[0:H, :] = jnp.where(i > 0, _glu(prev_ref[...], C), 0.0)
        u_ref[H:H + T, :] = _glu(x_ref[...], C)
        u_ref[H + T:, :] = _glu(next_ref[...], C)
        y = jnp.broadcast_to(b_ref[...], (TE, C))
        for k in range(K):
            y = y + w_ref[k:k + 1, :] * u_ref[H - (K - 1) + k:H - (K - 1) + k + TE, :]
        mu = jnp.mean(y, axis=-1, keepdims=True)
        yc = y - mu
        rstd = lax.rsqrt(jnp.mean(yc * yc, axis=-1, keepdims=True) + EPS)
        yh = yc * rstd
        ln = yh * g_ref[...] + beta_ref[...]
        s = _sigmoid(ln)
        dout = jnp.concatenate([d_ref[...], jnp.where(last, 0.0, dnext_ref[...])], axis=0)
        dln = dout * (s * (1.0 + ln * (1.0 - s)))
        dyh = dln * g_ref[...]
        dy = rstd * (dyh - jnp.mean(dyh, axis=-1, keepdims=True)
                     - yh * jnp.mean(dyh * yh, axis=-1, keepdims=True))
        dy_ref[...] = dy

        @pl.when(i == 0)
        def _():
            dwt_ref[...] = jnp.zeros_like(dwt_ref)
            db_ref[...] = jnp.zeros_like(db_ref)
            dg_ref[...] = jnp.zeros_like(dg_ref)
            dbeta_ref[...] = jnp.zeros_like(dbeta_ref)

        dg_ref[...] += jnp.sum((dln * yh)[:T], axis=0, keepdims=True)
        dbeta_ref[...] += jnp.sum(dln[:T], axis=0, keepdims=True)
        dy_t = dy[:T]
        db_ref[...] += jnp.sum(dy_t, axis=0, keepdims=True)
        du = jnp.zeros((T, C), F32)
        for k in range(K):
            off = H - (K - 1) + k
            dwt_ref[k:k + 1, :] += jnp.sum(dy_t * u_ref[off:off + T, :], axis=0, keepdims=True)
            du = du + w_ref[k:k + 1, :] * dy_ref[(K - 1) - k:(K - 1) - k + T, :]
        x = x_ref[...]
        sg = _sigmoid(x[:, C:])
        dca_ref[...] = (du * sg).astype(dca_ref.dtype)
        dcg_ref[...] = (du * x[:, :C] * sg * (1.0 - sg)).astype(dcg_ref.dtype)

    nh = L // H
    vec = pl.BlockSpec((1, C), lambda i: (0, 0))
    row = pl.BlockSpec((T, C), lambda i: (i, 0))
    return pl.pallas_call(
        body, name=name, grid=(nt,),
        in_specs=[pl.BlockSpec((T, C2), lambda i: (i, 0)),
                  pl.BlockSpec((H, C2), lambda i: (jnp.maximum(i * (T // H) - 1, 0), 0)),
                  pl.BlockSpec((H, C2), lambda i: (jnp.minimum((i + 1) * (T // H), nh - 1), 0)),
                  pl.BlockSpec((T, C), lambda i: (i, 1)),
                  pl.BlockSpec((H, C), lambda i: (jnp.minimum((i + 1) * (T // H), nh - 1), 1)),
                  pl.BlockSpec((K, C), lambda i: (0, 0)), vec, vec, vec],
        out_specs=[row, row, pl.BlockSpec((H, C), lambda i: (0, 0)), vec, vec, vec],
        out_shape=[jax.ShapeDtypeStruct((L, C), BF16), jax.ShapeDtypeStruct((L, C), BF16),
                   jax.ShapeDtypeStruct((H, C), F32), jax.ShapeDtypeStruct((1, C), F32),
                   jax.ShapeDtypeStruct((1, C), F32), jax.ShapeDtypeStruct((1, C), F32)],
        scratch_shapes=[pltpu.VMEM((T + 2 * H, C), F32), pltpu.VMEM((TE, C), F32)],
        compiler_params=_params(("arbitrary",)),
    )(cacg, cacg, cacg, dmixed, dmixed, dw_w, dw_b, ln_g, ln_b)


def _local_step(h0, target, n_meta, seq, norms, conv_p, wts, final_g):
    mix_g, ffn_g = norms
    dw_w, dw_b, ln_g, ln_b = conv_p
    depth = mix_g.shape[0]
    C = dw_b.shape[-1]
    sbw = (wts["w_in"].shape[-1] - 2 * C) // 3
    assert sbw == C, "the mixer halves must have equal width"
    row = lambda a, i: a[i][None, :]

    h = h0
    saved = []
    for i in range(depth):
        hn = _rmsnorm_fwd(h, row(mix_g, i), f"mix_norm_{i}")
        proj_qkv = _mm_nn([(hn, wts["w_in"], i, 0)], BF16, f"in_qkv_{i}", cols=(0, 3 * sbw))
        cacg = _mm_nn([(hn, wts["w_in"], i, 0)], F32, f"in_conv_{i}", cols=(3 * sbw, 2 * C))
        attn = _attn_fwd(proj_qkv, sbw, f"attn_fwd_{i}")
        conv = _conv_fwd(cacg, dw_w[i], row(dw_b, i), row(ln_g, i), row(ln_b, i), f"conv_fwd_{i}")
        h_mid = _mm_nn([(attn, wts["w_out"], i, 0), (conv, wts["w_out"], i, 1)], F32, f"out_proj_{i}", residual=h)
        hn2 = _rmsnorm_fwd(h_mid, row(ffn_g, i), f"ffn_norm_{i}")
        g = _mm_nt([(hn2, wts["w_gate_t"], i, 0)], F32, f"gate_{i}")
        u = _mm_nt([(hn2, wts["w_up_t"], i, 0)], F32, f"up_{i}")
        act = _swiglu_fwd(g, u, f"swiglu_{i}")
        h_out = _mm_nn([(act, wts["w_down"], i, 0)], F32, f"down_{i}", residual=h_mid)
        saved.append((h, hn, proj_qkv, cacg, attn, conv, h_mid, hn2, g, u, act))
        h = h_out

    loss, dh, d_final_g = _loss_head(h, final_g[None, :], target, n_meta, seq, "loss_head")

    grads = {k: [None] * depth for k in ("w_in", "w_out", "w_gate_t", "w_up_t", "w_down", "mix_g", "ffn_g",
                                         "dw_w", "dw_b", "ln_g", "ln_b")}
    for i in reversed(range(depth)):
        h_in, hn, proj_qkv, cacg, attn, conv, h_mid, hn2, g, u, act = saved[i]
        dact = _mm_nt([(dh, wts["w_down"], i, 0)], F32, f"d_act_{i}")
        grads["w_down"][i] = _mm_tn(act, dh, f"dw_down_{i}", col_sharded=False)
        dg, du = _swiglu_bwd(g, u, dact, f"swiglu_bwd_{i}")
        dhn2 = _mm_nn([(dg, wts["w_gate_t"], i, 0), (du, wts["w_up_t"], i, 0)], F32, f"d_hn2_{i}")
        grads["w_gate_t"][i] = _mm_tn(dg, hn2, f"dw_gate_{i}", col_sharded=False)
        grads["w_up_t"][i] = _mm_tn(du, hn2, f"dw_up_{i}", col_sharded=False)
        dh, grads["ffn_g"][i] = _rmsnorm_bwd(h_mid, row(ffn_g, i), dhn2, dh, f"ffn_norm_bwd_{i}")
        dmixed = _mm_nt([(dh, wts["w_out"], i, 0)], F32, f"d_mixed_{i}")
        mixed = jnp.concatenate([attn.astype(BF16), conv], axis=1)
        grads["w_out"][i] = _mm_tn(mixed, dh, f"dw_out_{i}", col_sharded=False)
        dq, dk, dv = _attn_bwd(proj_qkv, attn, dmixed, sbw, f"attn_bwd_{i}")
        dca, dcg, d_dw, d_b, d_lg, d_lb = _conv_bwd(
            cacg, dmixed, dw_w[i], row(dw_b, i), row(ln_g, i), row(ln_b, i), f"conv_bwd_{i}")
        grads["dw_w"][i] = d_dw[:dw_w.shape[1]]
        grads["dw_b"][i], grads["ln_g"][i], grads["ln_b"][i] = d_b[0], d_lg[0], d_lb[0]
        dproj = jnp.concatenate([dq, dk, dv, dca, dcg], axis=1)
        dhn = _mm_nt([(dproj, wts["w_in"], i, 0)], F32, f"d_hn_{i}")
        grads["w_in"][i] = _mm_tn(hn, dproj, f"dw_in_{i}", col_sharded=True)
        dh, d_mix = _rmsnorm_bwd(h_in, row(mix_g, i), dhn, dh, f"mix_norm_bwd_{i}")
        grads["mix_g"][i] = d_mix[0]
        grads["ffn_g"][i] = grads["ffn_g"][i][0]
    grads["final_g"] = d_final_g[0]
    return loss, dh, grads


ANY = pl.BlockSpec(memory_space=pl.ANY)


def _position():
    return lax.axis_index("x"), lax.axis_index("y"), lax.axis_index("c")


def _chip_at(x, y, k):
    return (1 - x if k & 2 else x), (1 - y if k & 1 else y)


def _half_rows(ref, half, rows, base=0):
    start = pl.multiple_of(base + half * rows, 8)
    lead = (slice(None),) * (len(ref.shape) - 2)
    return ref.at[(*lead, pl.ds(start, rows), slice(None))]


def _gather_weights(shards, col_sharded):
    n = len(shards)
    fulls = []
    for s, col in zip(shards, col_sharded):
        lyr, R, C = s.shape
        fulls.append(jax.ShapeDtypeStruct((lyr, R, N_CHIPS * C) if col else (lyr, N_CHIPS * R, C), s.dtype))

    def body(*refs):
        s_refs, f_refs = refs[:n], refs[n:2 * n]
        send_sems, recv_sems, local_sems = refs[2 * n:]
        x, y, c = _position()
        me = 2 * x + y

        def block(wi, chip, half):
            _, R, C = shards[wi].shape
            if col_sharded[wi]:
                cols = pl.ds(pl.multiple_of(chip * C, LANES), C)
                return f_refs[wi].at[:, pl.ds(pl.multiple_of(half * (R // 2), 8), R // 2), cols]
            return _half_rows(f_refs[wi], half, R // 2, base=chip * R)

        def own(wi):
            _, R, C = shards[wi].shape
            if col_sharded[wi]:
                return f_refs[wi].at[:, :, pl.ds(pl.multiple_of(me * C, LANES), C)]
            return f_refs[wi].at[:, pl.ds(pl.multiple_of(me * R, 8), R), :]

        def copy(wi, slot, src, dst, to):
            return pltpu.make_async_remote_copy(
                src_ref=src, dst_ref=dst, send_sem=send_sems.at[6 * wi + slot],
                recv_sem=recv_sems.at[6 * wi + slot], device_id=to, device_id_type=MESH)

        local = [pltpu.make_async_copy(s_refs[wi], own(wi), local_sems.at[wi]) for wi in range(n)]
        for cp in local:
            cp.start()
        sent = []
        for wi in range(n):
            R = shards[wi].shape[1]
            for k in (1, 2, 3):
                cp = copy(wi, k - 1, _half_rows(s_refs[wi], c, R // 2), block(wi, me, c), (*_chip_at(x, y, k), c))
                cp.start()
                sent.append(cp)
        for wi in range(n):
            for k in (1, 2, 3):
                landed = block(wi, me ^ k, c)
                copy(wi, k - 1, landed, landed, (x, y, c)).wait_recv()
                cp = copy(wi, 2 + k, landed, landed, (x, y, 1 - c))
                cp.start()
                sent.append(cp)
        for wi in range(n):
            for k in (1, 2, 3):
                passed = block(wi, me ^ k, 1 - c)
                copy(wi, 2 + k, passed, passed, (x, y, c)).wait_recv()
        for cp in sent:
            cp.wait_send()
        for cp in local:
            cp.wait()

    return pl.pallas_call(
        body, name="gather_weights", out_shape=fulls,
        in_specs=[ANY] * n, out_specs=[ANY] * n,
        scratch_shapes=[pltpu.SemaphoreType.DMA((6 * n,)), pltpu.SemaphoreType.DMA((6 * n,)),
                        pltpu.SemaphoreType.DMA((n,))],
    )(*shards)


def _to_bf16(w, name):
    lyr, R, C = w.shape
    tr = _pick(R, (256, 352, 128))

    def body(w_ref, o_ref):
        o_ref[...] = w_ref[...].astype(BF16)

    blk = pl.BlockSpec((None, tr, C), lambda l, r: (l, r, 0))
    return pl.pallas_call(
        body, name=name, grid=(lyr, R // tr), in_specs=[blk], out_specs=blk,
        out_shape=jax.ShapeDtypeStruct(w.shape, BF16), compiler_params=_params(("parallel", "parallel")),
    )(w)


def _rs_to_sibling(grads):
    n = len(grads)
    outs = [jax.ShapeDtypeStruct((g.shape[0], g.shape[1] // 2, g.shape[2]), g.dtype) for g in grads]

    def body(*refs):
        g_refs, l_refs = refs[:n], refs[n:2 * n]
        send_sems, recv_sems = refs[2 * n:]
        x, y, c = _position()
        cps = []
        for wi in range(n):
            R = grads[wi].shape[1]
            cp = pltpu.make_async_remote_copy(
                src_ref=_half_rows(g_refs[wi], 1 - c, R // 2), dst_ref=l_refs[wi],
                send_sem=send_sems.at[wi], recv_sem=recv_sems.at[wi],
                device_id=(x, y, 1 - c), device_id_type=MESH)
            cp.start()
            cps.append(cp)
        for cp in cps:
            cp.wait()

    return pl.pallas_call(
        body, name="grads_to_sibling", out_shape=outs, in_specs=[ANY] * n, out_specs=[ANY] * n,
        scratch_shapes=[pltpu.SemaphoreType.DMA((n,)), pltpu.SemaphoreType.DMA((n,))],
    )(*grads)


def _chip_sum(g, landed, core, name):
    _, R, C = g.shape
    hr = R // 2
    tr = _pick(hr, (256, 352, 128))
    nr = hr // tr

    def body(c_ref, g_ref, l_ref, o_ref):
        o_ref[...] = (g_ref[...] + l_ref[...]).astype(BF16)

    grid_spec = pltpu.PrefetchScalarGridSpec(
        num_scalar_prefetch=1, grid=(N_CHIPS, nr),
        in_specs=[pl.BlockSpec((None, tr, C), lambda j, r, c_ref: (j, c_ref[0] * nr + r, 0)),
                  pl.BlockSpec((None, tr, C), lambda j, r, c_ref: (j, r, 0))],
        out_specs=pl.BlockSpec((None, tr, C), lambda j, r, c_ref: (j, r, 0)))
    return pl.pallas_call(
        body, name=name, grid_spec=grid_spec, out_shape=jax.ShapeDtypeStruct((N_CHIPS, hr, C), BF16),
        compiler_params=_params(("parallel", "parallel")),
    )(core, g, landed)


def _rs_across_chips(parts):
    n = len(parts)
    outs = [jax.ShapeDtypeStruct(p.shape, p.dtype) for p in parts]

    def body(*refs):
        p_refs, l_refs = refs[:n], refs[n:2 * n]
        send_sems, recv_sems, local_sems = refs[2 * n:]
        x, y, c = _position()
        me = 2 * x + y
        local, sent = [], []
        for wi in range(n):
            cp = pltpu.make_async_copy(p_refs[wi].at[me], l_refs[wi].at[me], local_sems.at[wi])
            cp.start()
            local.append(cp)
            for k in (1, 2, 3):
                cp = pltpu.make_async_remote_copy(
                    src_ref=p_refs[wi].at[me ^ k], dst_ref=l_refs[wi].at[me],
                    send_sem=send_sems.at[3 * wi + k - 1], recv_sem=recv_sems.at[3 * wi + k - 1],
                    device_id=(*_chip_at(x, y, k), c), device_id_type=MESH)
                cp.start()
                sent.append(cp)
        for wi in range(n):
            for k in (1, 2, 3):
                slot = l_refs[wi].at[me ^ k]
                pltpu.make_async_remote_copy(
                    src_ref=slot, dst_ref=slot, send_sem=send_sems.at[3 * wi + k - 1],
                    recv_sem=recv_sems.at[3 * wi + k - 1], device_id=(x, y, c), device_id_type=MESH).wait_recv()
        for cp in sent:
            cp.wait_send()
        for cp in local:
            cp.wait()

    return pl.pallas_call(
        body, name="grads_across_chips", out_shape=outs, in_specs=[ANY] * n, out_specs=[ANY] * n,
        scratch_shapes=[pltpu.SemaphoreType.DMA((3 * n,)), pltpu.SemaphoreType.DMA((3 * n,)),
                        pltpu.SemaphoreType.DMA((n,))],
    )(*parts)


def _sum_chips(landed, name):
    _, hr, C = landed.shape
    tr = _pick(hr, (256, 352, 128))

    def body(l_ref, o_ref):
        o_ref[...] = ((l_ref[0].astype(F32) + l_ref[1].astype(F32)) + l_ref[2].astype(F32)) + l_ref[3].astype(F32)

    return pl.pallas_call(
        body, name=name, grid=(hr // tr,),
        in_specs=[pl.BlockSpec((N_CHIPS, tr, C), lambda r: (0, r, 0))],
        out_specs=pl.BlockSpec((tr, C), lambda r: (r, 0)),
        out_shape=jax.ShapeDtypeStruct((hr, C), F32), compiler_params=_params(("parallel",)),
    )(landed)


def _rs_join_halves(halves):
    n = len(halves)
    depth = len(halves[0])
    outs = [jax.ShapeDtypeStruct((depth, 2 * h[0].shape[0], h[0].shape[1]), F32) for h in halves]

    def body(*refs):
        h_refs = [refs[wi * depth:(wi + 1) * depth] for wi in range(n)]
        o_refs = refs[n * depth:n * depth + n]
        send_sems, recv_sems, local_sems = refs[n * depth + n:]
        x, y, c = _position()
        local, sent = [], []
        for wi in range(n):
            hr = halves[wi][0].shape[0]
            for l in range(depth):
                s = wi * depth + l
                mine = _half_rows(o_refs[wi].at[l], c, hr)
                cp = pltpu.make_async_copy(h_refs[wi][l], mine, local_sems.at[s])
                cp.start()
                local.append(cp)
                cp = pltpu.make_async_remote_copy(
                    src_ref=h_refs[wi][l], dst_ref=mine, send_sem=send_sems.at[s], recv_sem=recv_sems.at[s],
                    device_id=(x, y, 1 - c), device_id_type=MESH)
                cp.start()
                sent.append(cp)
        for wi in range(n):
            hr = halves[wi][0].shape[0]
            for l in range(depth):
                s = wi * depth + l
                theirs = _half_rows(o_refs[wi].at[l], 1 - c, hr)
                pltpu.make_async_remote_copy(
                    src_ref=theirs, dst_ref=theirs, send_sem=send_sems.at[s], recv_sem=recv_sems.at[s],
                    device_id=(x, y, c), device_id_type=MESH).wait_recv()
        for cp in sent:
            cp.wait_send()
        for cp in local:
            cp.wait()

    flat = [h for hs in halves for h in hs]
    return pl.pallas_call(
        body, name="grads_join_halves", out_shape=outs, in_specs=[ANY] * len(flat), out_specs=[ANY] * n,
        scratch_shapes=[pltpu.SemaphoreType.DMA((n * depth,)), pltpu.SemaphoreType.DMA((n * depth,)),
                        pltpu.SemaphoreType.DMA((n * depth,))],
    )(*flat)


def _small_allreduce(vec):
    rows = vec.shape[0]

    def body(v_ref, o_ref, land, send_sems, recv_sems):
        x, y, c = _position()
        me = 4 * x + 2 * y + c
        land[0] = v_ref[...]
        sent = []
        for k in range(1, N_DEV):
            to = (1 - x if k & 4 else x, 1 - y if k & 2 else y, 1 - c if k & 1 else c)
            cp = pltpu.make_async_remote_copy(
                src_ref=v_ref, dst_ref=land.at[k], send_sem=send_sems.at[k - 1], recv_sem=recv_sems.at[k - 1],
                device_id=to, device_id_type=MESH)
            cp.start()
            sent.append(cp)
        for cp in sent:
            cp.wait_recv()
        acc = land[me]
        for e in range(1, N_DEV):
            acc = acc + land[me ^ e]
        o_ref[...] = acc
        for cp in sent:
            cp.wait_send()

    vmem = pl.BlockSpec(memory_space=pltpu.VMEM)
    return pl.pallas_call(
        body, name="small_allreduce", out_shape=jax.ShapeDtypeStruct(vec.shape, F32),
        in_specs=[vmem], out_specs=vmem,
        scratch_shapes=[pltpu.VMEM((N_DEV, rows, LANES), F32), pltpu.SemaphoreType.DMA((N_DEV - 1,)),
                        pltpu.SemaphoreType.DMA((N_DEV - 1,))],
    )(vec)


def _adam_math(w, g, m, v):
    m = ADAM_B1 * m + (1.0 - ADAM_B1) * g
    v = ADAM_B2 * v + (1.0 - ADAM_B2) * jnp.square(g)
    m_hat = m / (1.0 - ADAM_B1 ** ADAM_STEP)
    v_hat = v / (1.0 - ADAM_B2 ** ADAM_STEP)
    delta = -ADAM_LR * (m_hat / (jnp.sqrt(v_hat) + ADAM_EPS) + ADAM_WD * w)
    return delta, m, v


def _adam(w, g, m, v, name):
    def body(w_ref, g_ref, m_ref, v_ref, d_ref, nm_ref, nv_ref):
        d_ref[...], nm_ref[...], nv_ref[...] = _adam_math(w_ref[...], g_ref[...], m_ref[...], v_ref[...])

    if w.ndim == 3:
        lyr, R, C = w.shape
        tr = _pick(R, (256, 352, 128))
        blk = pl.BlockSpec((None, tr, C), lambda l, r: (l, r, 0))
        grid, sem = (lyr, R // tr), ("parallel", "parallel")
    else:
        blk = pl.BlockSpec(w.shape, lambda: (0, 0))
        grid, sem = (), None
    return pl.pallas_call(
        body, name=name, grid=grid, in_specs=[blk] * 4, out_specs=[blk] * 3,
        out_shape=[jax.ShapeDtypeStruct(w.shape, F32)] * 3, compiler_params=_params(sem),
    )(w, g, m, v)


def _rows(a, pad_to=8):
    r = a.reshape(-1, LANES)
    extra = (-r.shape[0]) % pad_to
    return jnp.pad(r, ((0, extra), (0, 0))) if extra else r


def _pack(arrays):
    return jnp.concatenate([_rows(a) for a in arrays], axis=0)


def _unpack(slab, shapes):
    out, at = [], 0
    for shp in shapes:
        nrow = math.prod(shp) // LANES
        out.append(slab[at:at + nrow].reshape(shp))
        at += nrow + (-nrow) % 8
    return out


BIG = ("w_in", "w_out", "w_gate_t", "w_up_t", "w_down")
BIG_COL_SHARDED = (True, False, False, False, False)
TRANSPOSED = {"w_gate_t": "w_gate", "w_up_t": "w_up"}


def kernel(x, meta_tokens, mix_norm_g, w_in, conv_dw_w, conv_dw_b, conv_ln_g, conv_ln_b, w_out, ffn_norm_g, w_gate, w_up, w_down, final_norm_g, loss_target, m_meta_tokens, m_mix_norm_g, m_w_in, m_conv_dw_w, m_conv_dw_b, m_conv_ln_g, m_conv_ln_b, m_w_out, m_ffn_norm_g, m_w_gate, m_w_up, m_w_down, m_final_norm_g, v_meta_tokens, v_mix_norm_g, v_w_in, v_conv_dw_w, v_conv_dw_b, v_conv_ln_g, v_conv_ln_b, v_w_out, v_ffn_norm_g, v_w_gate, v_w_up, v_w_down, v_final_norm_g):
    n_meta, seq = meta_tokens.shape[0], x.shape[1]
    D = x.shape[2]
    depth, taps, c_shard = conv_dw_w.shape
    C = conv_dw_b.shape[-1]
    chip = 2 * lax.axis_index("x") + lax.axis_index("y")
    core = lax.axis_index("c").astype(jnp.int32).reshape(1)
    big_w = dict(w_in=w_in, w_out=w_out, w_gate=w_gate, w_up=w_up, w_down=w_down)
    big_m = dict(w_in=m_w_in, w_out=m_w_out, w_gate=m_w_gate, w_up=m_w_up, w_down=m_w_down)
    big_v = dict(w_in=v_w_in, w_out=v_w_out, w_gate=v_w_gate, w_up=v_w_up, w_down=v_w_down)

    small_shard = _pack([conv_dw_w, meta_tokens])[None]
    to_send = [jnp.swapaxes(big_w[TRANSPOSED[k]], 1, 2) if k in TRANSPOSED else big_w[k] for k in BIG]
    gathered = _gather_weights([_to_bf16(w, f"to_bf16_{k}") for k, w in zip(BIG, to_send)] + [small_shard],
                               BIG_COL_SHARDED + (False,))
    wts = dict(zip(BIG, gathered[:-1]))
    rows_shard = small_shard.shape[1]
    dw_full, meta_full = [], []
    for j in range(N_CHIPS):
        dwj, mj = _unpack(gathered[-1][0, j * rows_shard:(j + 1) * rows_shard],
                          [conv_dw_w.shape, meta_tokens.shape])
        dw_full.append(dwj)
        meta_full.append(mj)
    dw_w_full = jnp.concatenate(dw_full, axis=2)
    meta = jnp.concatenate(meta_full, axis=1)

    L = n_meta + seq
    Lp = -(-L // QUERY_BLOCK) * QUERY_BLOCK
    h0 = jnp.concatenate([meta, x[0], jnp.zeros((Lp - L, D), F32)], axis=0)
    target = jnp.pad(loss_target[0], ((n_meta, Lp - L), (0, 0)))
    loss, dh0, grads = _local_step(h0, target, n_meta, seq, (mix_norm_g, ffn_norm_g),
                                   (dw_w_full, conv_dw_b, conv_ln_g, conv_ln_b), wts, final_norm_g)
    loss = lax.psum(loss[0, 0], ("x", "y", "c"))
    grad_x = dh0[n_meta:L][None]

    flat = [grads[k][l] for k in BIG for l in range(depth)]
    landed = _rs_to_sibling(flat)
    parts = [_chip_sum(g, la, core, f"chip_sum_{i}") for i, (g, la) in enumerate(zip(flat, landed))]
    across = _rs_across_chips(parts)
    halves = [_sum_chips(a, f"sum_chips_{i}") for i, a in enumerate(across)]
    big_g = dict(zip(BIG, _rs_join_halves([halves[i * depth:(i + 1) * depth] for i in range(len(BIG))])))

    small_names = ("mix_g", "ffn_g", "dw_b", "ln_g", "ln_b", "dw_w")
    small = [jnp.stack(grads[k]) for k in small_names] + [grads["final_g"], dh0[:n_meta]]
    small_shapes = [a.shape for a in small]
    g_mix, g_ffn, g_dwb, g_lng, g_lnb, g_dww, g_final, g_meta = _unpack(_small_allreduce(_pack(small)), small_shapes)
    g_dww = lax.dynamic_slice_in_dim(g_dww, chip * c_shard, c_shard, axis=2)
    g_meta = lax.dynamic_slice_in_dim(g_meta, chip * meta_tokens.shape[1], meta_tokens.shape[1], axis=1)

    out_g, out_d, out_m, out_v = {}, {}, {}, {}
    for kk in BIG:
        k = TRANSPOSED.get(kk, kk)
        out_g[k] = jnp.swapaxes(big_g[kk], 1, 2) if kk in TRANSPOSED else big_g[kk]
        out_d[k], out_m[k], out_v[k] = _adam(big_w[k], out_g[k], big_m[k], big_v[k], f"adam_{k}")
    small_order = ("meta_tokens", "mix_norm_g", "conv_dw_w", "conv_dw_b", "conv_ln_g", "conv_ln_b",
                   "ffn_norm_g", "final_norm_g")
    sw = dict(meta_tokens=meta_tokens, mix_norm_g=mix_norm_g, conv_dw_w=conv_dw_w, conv_dw_b=conv_dw_b,
              conv_ln_g=conv_ln_g, conv_ln_b=conv_ln_b, ffn_norm_g=ffn_norm_g, final_norm_g=final_norm_g)
    sm = dict(meta_tokens=m_meta_tokens, mix_norm_g=m_mix_norm_g, conv_dw_w=m_conv_dw_w, conv_dw_b=m_conv_dw_b,
              conv_ln_g=m_conv_ln_g, conv_ln_b=m_conv_ln_b, ffn_norm_g=m_ffn_norm_g, final_norm_g=m_final_norm_g)
    sv = dict(meta_tokens=v_meta_tokens, mix_norm_g=v_mix_norm_g, conv_dw_w=v_conv_dw_w, conv_dw_b=v_conv_dw_b,
              conv_ln_g=v_conv_ln_g, conv_ln_b=v_conv_ln_b, ffn_norm_g=v_ffn_norm_g, final_norm_g=v_final_norm_g)
    sg = dict(meta_tokens=g_meta, mix_norm_g=g_mix, conv_dw_w=g_dww, conv_dw_b=g_dwb, conv_ln_g=g_lng,
              conv_ln_b=g_lnb, ffn_norm_g=g_ffn, final_norm_g=g_final)
    def slab(d):
        return _pack([d[k] for k in small_order])
    shapes = [sw[k].shape for k in small_order]
    deltas = _adam(slab(sw), slab(sg), slab(sm), slab(sv), "adam_small")
    for res, dst in zip(deltas, (out_d, out_m, out_v)):
        dst.update(zip(small_order, _unpack(res, shapes)))
    out_g.update(sg)

    order = ("meta_tokens", "mix_norm_g", "w_in", "conv_dw_w", "conv_dw_b", "conv_ln_g", "conv_ln_b", "w_out",
             "ffn_norm_g", "w_gate", "w_up", "w_down", "final_norm_g")
    return (loss, grad_x, *[out_g[k] for k in order], *[out_d[k] for k in order],
            *[out_m[k] for k in order], *[out_v[k] for k in order])
```

```python
import functools
import math

import jax
import jax.numpy as jnp
from jax import lax
from jax.experimental import pallas as pl
from jax.experimental.pallas import tpu as pltpu

F32 = jnp.float32
BF16 = jnp.bfloat16
MESH = pl.DeviceIdType.MESH

EPS = 1e-6
QUERY_BLOCK = 128
LANES = 128
HEAD_DIM = 64
LOG_STICK_FLOOR = -40.0
CONV_HALO = 32
N_CHIPS = 4
N_DEV = 8
VMEM_LIMIT = 56 * 1024 * 1024

ADAM_LR = 0.001
ADAM_B1 = 0.9
ADAM_B2 = 0.999
ADAM_EPS = 1e-08
ADAM_WD = 0.01
ADAM_STEP = 10


def _pick(n, prefs):
    for p in prefs:
        if n % p == 0:
            return p
    return n


def _params(sem=None):
    return pltpu.CompilerParams(dimension_semantics=sem, vmem_limit_bytes=VMEM_LIMIT)


def _sigmoid(x):
    return 1.0 / (1.0 + jnp.exp(-x))


def _rmsnorm_fwd(h, g, name):
    L, D = h.shape
    T = _pick(L, (384, 128))

    def body(h_ref, g_ref, o_ref):
        x = h_ref[...]
        r = lax.rsqrt(jnp.mean(x * x, axis=-1, keepdims=True) + EPS)
        o_ref[...] = (x * r * g_ref[...]).astype(o_ref.dtype)

    return pl.pallas_call(
        body, name=name, grid=(L // T,),
        in_specs=[pl.BlockSpec((T, D), lambda i: (i, 0)), pl.BlockSpec((1, D), lambda i: (0, 0))],
        out_specs=pl.BlockSpec((T, D), lambda i: (i, 0)),
        out_shape=jax.ShapeDtypeStruct((L, D), BF16),
        compiler_params=_params(("parallel",)),
    )(h, g)


def _rmsnorm_bwd(h, g, dy, dh_in, name):
    L, D = h.shape
    T = _pick(L, (384, 128))

    def body(h_ref, g_ref, dy_ref, dhin_ref, dh_ref, dg_ref):
        x = h_ref[...]
        dyv = dy_ref[...]
        r = lax.rsqrt(jnp.mean(x * x, axis=-1, keepdims=True) + EPS)
        xh = x * r
        dxh = dyv * g_ref[...]
        dh_ref[...] = dhin_ref[...] + r * (dxh - xh * jnp.mean(dxh * xh, axis=-1, keepdims=True))

        @pl.when(pl.program_id(0) == 0)
        def _():
            dg_ref[...] = jnp.zeros_like(dg_ref)

        dg_ref[...] += jnp.sum(dyv * xh, axis=0, keepdims=True)

    row = pl.BlockSpec((T, D), lambda i: (i, 0))
    vec = pl.BlockSpec((1, D), lambda i: (0, 0))
    return pl.pallas_call(
        body, name=name, grid=(L // T,),
        in_specs=[row, vec, row, row], out_specs=[row, vec],
        out_shape=[jax.ShapeDtypeStruct((L, D), F32), jax.ShapeDtypeStruct((1, D), F32)],
        compiler_params=_params(("arbitrary",)),
    )(h, g, dy, dh_in)


def _loss_head(h, g, target, n_meta, seq, name):
    L, D = h.shape
    T = _pick(L, (384, 128))

    def body(h_ref, g_ref, t_ref, loss_ref, dh_ref, dg_ref):
        i = pl.program_id(0)
        x = h_ref[...]
        gv = g_ref[...]
        r = lax.rsqrt(jnp.mean(x * x, axis=-1, keepdims=True) + EPS)
        xh = x * r
        y = xh * gv
        rows = i * T + lax.broadcasted_iota(jnp.int32, (T, 1), 0)
        live = (rows >= n_meta) & (rows < n_meta + seq)
        diff = jnp.where(live, y - t_ref[...], 0.0)
        dyv = diff / D
        dxh = dyv * gv
        dh_ref[...] = r * (dxh - xh * jnp.mean(dxh * xh, axis=-1, keepdims=True))

        @pl.when(i == 0)
        def _():
            dg_ref[...] = jnp.zeros_like(dg_ref)
            loss_ref[...] = jnp.zeros_like(loss_ref)

        dg_ref[...] += jnp.sum(dyv * xh, axis=0, keepdims=True)
        per_row = jnp.mean(diff * diff, axis=-1, keepdims=True)
        loss_ref[...] += 0.5 * jnp.sum(per_row, axis=0, keepdims=True)

    row = pl.BlockSpec((T, D), lambda i: (i, 0))
    vec = pl.BlockSpec((1, D), lambda i: (0, 0))
    one = pl.BlockSpec((1, 1), lambda i: (0, 0))
    return pl.pallas_call(
        body, name=name, grid=(L // T,),
        in_specs=[row, vec, row], out_specs=[one, row, vec],
        out_shape=[jax.ShapeDtypeStruct((1, 1), F32), jax.ShapeDtypeStruct((L, D), F32),
                   jax.ShapeDtypeStruct((1, D), F32)],
        compiler_params=_params(("arbitrary",)),
    )(h, g, target)


def _swiglu_fwd(g, u, name):
    L, F = g.shape
    T = _pick(L, (192, 128))

    def body(g_ref, u_ref, o_ref):
        gv = g_ref[...]
        o_ref[...] = (gv * _sigmoid(gv) * u_ref[...]).astype(o_ref.dtype)

    row = pl.BlockSpec((T, F), lambda i: (i, 0))
    return pl.pallas_call(
        body, name=name, grid=(L // T,), in_specs=[row, row], out_specs=row,
        out_shape=jax.ShapeDtypeStruct((L, F), BF16), compiler_params=_params(("parallel",)),
    )(g, u)


def _swiglu_bwd(g, u, dact, name):
    L, F = g.shape
    T = _pick(L, (192, 128))

    def body(g_ref, u_ref, d_ref, dg_ref, du_ref):
        gv = g_ref[...]
        dv = d_ref[...]
        s = _sigmoid(gv)
        du_ref[...] = (dv * (gv * s)).astype(du_ref.dtype)
        dg_ref[...] = (dv * u_ref[...] * (s * (1.0 + gv * (1.0 - s)))).astype(dg_ref.dtype)

    row = pl.BlockSpec((T, F), lambda i: (i, 0))
    return pl.pallas_call(
        body, name=name, grid=(L // T,), in_specs=[row, row, row], out_specs=[row, row],
        out_shape=[jax.ShapeDtypeStruct((L, F), BF16), jax.ShapeDtypeStruct((L, F), BF16)],
        compiler_params=_params(("parallel",)),
    )(g, u, dact)


def _mm_nn(pairs, out_dtype, name, residual=None, cols=None):
    M = pairs[0][0].shape[0]
    col0, N = cols if cols is not None else (0, pairs[0][1].shape[-1])
    tm = _pick(M, (1056, 384, 128))
    tn = _pick(math.gcd(N, col0) if col0 else N, (640, 512, 256, 128))
    jb = col0 // tn
    n = len(pairs)

    def body(*refs):
        a_refs, w_refs = refs[:n], refs[n:2 * n]
        o_ref = refs[-1]
        acc = None
        for a_ref, w_ref in zip(a_refs, w_refs):
            d = jnp.dot(a_ref[...].astype(BF16), w_ref[...], preferred_element_type=F32)
            acc = d if acc is None else acc + d
        if residual is not None:
            acc = acc + refs[2 * n][...]
        o_ref[...] = acc.astype(o_ref.dtype)

    in_specs = [pl.BlockSpec((tm, a.shape[1]), lambda i, j: (i, 0)) for a, _, _, _ in pairs]
    for a, _, layer, kblk in pairs:
        in_specs.append(pl.BlockSpec((None, a.shape[1], tn), functools.partial(lambda i, j, l, kb: (l, kb, j + jb), l=layer, kb=kblk)))
    args = [p[0] for p in pairs] + [p[1] for p in pairs]
    if residual is not None:
        in_specs.append(pl.BlockSpec((tm, tn), lambda i, j: (i, j)))
        args.append(residual)
    return pl.pallas_call(
        body, name=name, grid=(M // tm, N // tn), in_specs=in_specs,
        out_specs=pl.BlockSpec((tm, tn), lambda i, j: (i, j)),
        out_shape=jax.ShapeDtypeStruct((M, N), out_dtype),
        compiler_params=_params(("parallel", "parallel")),
    )(*args)


def _mm_nt(pairs, out_dtype, name):
    M = pairs[0][0].shape[0]
    K = pairs[0][1].shape[1]
    tm = _pick(M, (1056, 384, 128))
    tk = _pick(K, (512, 1408, 256, 128))
    n = len(pairs)

    def body(*refs):
        d_refs, w_refs = refs[:n], refs[n:2 * n]
        o_ref = refs[-1]
        acc = None
        for d_ref, w_ref in zip(d_refs, w_refs):
            d = lax.dot_general(d_ref[...].astype(BF16), w_ref[...], (((1,), (1,)), ((), ())),
                                preferred_element_type=F32)
            acc = d if acc is None else acc + d
        o_ref[...] = acc.astype(o_ref.dtype)

    in_specs = [pl.BlockSpec((tm, d.shape[1]), lambda i, j: (i, 0)) for d, _, _, _ in pairs]
    for d, _, layer, cblk in pairs:
        in_specs.append(pl.BlockSpec((None, tk, d.shape[1]), functools.partial(lambda i, j, l, cb: (l, j, cb), l=layer, cb=cblk)))
    args = [p[0] for p in pairs] + [p[1] for p in pairs]
    return pl.pallas_call(
        body, name=name, grid=(M // tm, K // tk), in_specs=in_specs,
        out_specs=pl.BlockSpec((tm, tk), lambda i, j: (i, j)),
        out_shape=jax.ShapeDtypeStruct((M, K), out_dtype),
        compiler_params=_params(("parallel", "parallel")),
    )(*args)


def _mm_tn(a, b, name, col_sharded):
    M, K = a.shape
    N = b.shape[1]
    tm = _pick(M, (384, 128))
    tk = _pick(K, (512, 1408, 256, 128))
    tn = N // N_CHIPS if col_sharded else _pick(N, (512, 128))

    def body(a_ref, b_ref, o_ref):
        @pl.when(pl.program_id(2) == 0)
        def _():
            o_ref[...] = jnp.zeros_like(o_ref)

        o_ref[...] += lax.dot_general(a_ref[...].astype(BF16), b_ref[...].astype(BF16),
                                      (((0,), (0,)), ((), ())), preferred_element_type=F32)

    if col_sharded:
        out_shape = jax.ShapeDtypeStruct((N_CHIPS, K, tn), F32)
        out_spec = pl.BlockSpec((None, tk, tn), lambda k, j, m: (j, k, 0))
    else:
        out_shape = jax.ShapeDtypeStruct((K, N), F32)
        out_spec = pl.BlockSpec((tk, tn), lambda k, j, m: (k, j))
    out = pl.pallas_call(
        body, name=name, grid=(K // tk, N // tn, M // tm),
        in_specs=[pl.BlockSpec((tm, tk), lambda k, j, m: (m, k)), pl.BlockSpec((tm, tn), lambda k, j, m: (m, j))],
        out_specs=out_spec, out_shape=out_shape,
        compiler_params=_params(("parallel", "parallel", "arbitrary")),
    )(a, b)
    return out if col_sharded else out.reshape(N_CHIPS, K // N_CHIPS, N)


def _stack_heads(x, scale=None):
    lane = lax.broadcasted_iota(jnp.int32, x.shape, 1)
    zero = jnp.zeros_like(x)
    lo = jnp.where(lane < HEAD_DIM, x, zero)
    hi = jnp.where(lane < HEAD_DIM, zero, x)
    out = jnp.concatenate([lo, hi], axis=0)
    return out if scale is None else out * scale


def _unstack_heads(x2):
    qb = x2.shape[0] // 2
    lane = lax.broadcasted_iota(jnp.int32, (qb, LANES), 1)
    return jnp.where(lane < HEAD_DIM, x2[:qb], x2[qb:])


def _tri_and_ones(strict):
    r = lax.broadcasted_iota(jnp.int32, (QUERY_BLOCK, 2 * QUERY_BLOCK), 0)
    c = lax.broadcasted_iota(jnp.int32, (QUERY_BLOCK, 2 * QUERY_BLOCK), 1)
    tri = (r > c) if strict else (r >= c)
    return jnp.where(tri | (c >= QUERY_BLOCK), 1.0, 0.0).astype(BF16)


def _dot2(x, m):
    xh = x.astype(BF16)
    xl = (x - xh.astype(F32)).astype(BF16)
    return jnp.dot(xh, m, preferred_element_type=F32) + jnp.dot(xl, m, preferred_element_type=F32)


def _causal_valid():
    r = lax.broadcasted_iota(jnp.int32, (2 * QUERY_BLOCK, QUERY_BLOCK), 0) & (QUERY_BLOCK - 1)
    c = lax.broadcasted_iota(jnp.int32, (2 * QUERY_BLOCK, QUERY_BLOCK), 1)
    return c < r


def _sweep_older(i, step, carry_ref):
    def cond(state):
        n, live = state
        return jnp.logical_and(n < i, live)

    def older(state):
        n, _ = state
        step(i - 1 - n, False)
        return n + 1, jnp.max(carry_ref[...]) > LOG_STICK_FLOOR

    lax.while_loop(cond, older, (0, jnp.max(carry_ref[...]) > LOG_STICK_FLOOR))


def _attn_fwd(qkv, sb_width, name):
    L = qkv.shape[0]
    QB = QUERY_BLOCK
    nb = L // QB
    n_pairs = sb_width // LANES
    scale = 1.0 / math.sqrt(HEAD_DIM)

    def body(q_ref, k_ref, v_ref, o_ref, acc_ref, carry_ref):
        i = pl.program_id(1)
        q2 = _stack_heads(q_ref[...], scale)
        txo = _tri_and_ones(True)
        valid = _causal_valid()
        acc_ref[...] = jnp.zeros_like(acc_ref)
        carry_ref[...] = jnp.zeros_like(carry_ref)

        def step(j, diag):
            start = pl.multiple_of(j * QB, QB)
            kb = k_ref[pl.ds(start, QB), :]
            vb = v_ref[pl.ds(start, QB), :]
            z = lax.dot_general(q2, kb, (((1,), (1,)), ((), ())), preferred_element_type=F32)
            sp = jnp.log(1.0 + jnp.exp(-jnp.abs(z)))
            a = jnp.minimum(z, 0.0) - sp
            b = jnp.minimum(-z, 0.0) - sp
            if diag:
                b = jnp.where(valid, b, 0.0)
            res = _dot2(b, txo)
            w = jnp.exp(a + res[:, :QB] + carry_ref[...])
            if diag:
                w = jnp.where(valid, w, 0.0)
            carry_ref[...] += res[:, QB:]
            acc_ref[...] += _dot2(w, vb)

        step(i, True)
        _sweep_older(i, step, carry_ref)
        o_ref[...] = _unstack_heads(acc_ref[...])

    return pl.pallas_call(
        body, name=name, grid=(n_pairs, nb),
        in_specs=[pl.BlockSpec((QB, LANES), lambda p, i: (i, p)),
                  pl.BlockSpec((L, LANES), lambda p, i: (0, n_pairs + p)),
                  pl.BlockSpec((L, LANES), lambda p, i: (0, 2 * n_pairs + p))],
        out_specs=pl.BlockSpec((QB, LANES), lambda p, i: (i, p)),
        out_shape=jax.ShapeDtypeStruct((L, sb_width), F32),
        scratch_shapes=[pltpu.VMEM((2 * QB, LANES), F32), pltpu.VMEM((2 * QB, LANES), F32)],
        compiler_params=_params(("parallel", "arbitrary")),
    )(qkv, qkv, qkv)


def _attn_bwd(qkv, o, dmixed, sb_width, name):
    L = qkv.shape[0]
    QB = QUERY_BLOCK
    nb = L // QB
    n_pairs = sb_width // LANES
    scale = 1.0 / math.sqrt(HEAD_DIM)

    def body(q_ref, k_ref, v_ref, o_ref, do_ref, dq_ref, dk_ref, dv_ref,
             dq_acc, dk_acc, dv_acc, ce_ref, cr_ref):
        i = pl.program_id(1)

        @pl.when(i == 0)
        def _():
            dk_acc[...] = jnp.zeros_like(dk_acc)
            dv_acc[...] = jnp.zeros_like(dv_acc)

        q2 = _stack_heads(q_ref[...], scale)
        do2 = _stack_heads(do_ref[...].astype(BF16))
        ov = o_ref[...]
        o2 = jnp.concatenate([ov, ov], axis=0)
        ones = jnp.ones((LANES, LANES), BF16)
        dtot = _dot2(do2.astype(F32) * o2, ones)
        txo = _tri_and_ones(True)
        tio = _tri_and_ones(False)
        valid = _causal_valid()
        dq_acc[...] = jnp.zeros_like(dq_acc)
        ce_ref[...] = jnp.zeros_like(ce_ref)
        cr_ref[...] = jnp.zeros_like(cr_ref)

        def step(j, diag):
            start = pl.multiple_of(j * QB, QB)
            kb = k_ref[pl.ds(start, QB), :]
            vb = v_ref[pl.ds(start, QB), :]
            z = lax.dot_general(q2, kb, (((1,), (1,)), ((), ())), preferred_element_type=F32)
            e = jnp.exp(-jnp.abs(z))
            sp = jnp.log(1.0 + e)
            a = jnp.minimum(z, 0.0) - sp
            b = jnp.minimum(-z, 0.0) - sp
            rinv = 1.0 / (1.0 + e)
            pos = z >= 0.0
            beta = jnp.where(pos, rinv, e * rinv)
            one_m_beta = jnp.where(pos, e * rinv, rinv)
            if diag:
                b = jnp.where(valid, b, 0.0)
            res = _dot2(b, txo)
            w = jnp.exp(a + res[:, :QB] + ce_ref[...])
            if diag:
                w = jnp.where(valid, w, 0.0)
            ce_ref[...] += res[:, QB:]
            dw = lax.dot_general(do2, vb, (((1,), (1,)), ((), ())), preferred_element_type=F32)
            g = w * dw
            res2 = _dot2(g, tio)
            rinc = res2[:, :QB] + cr_ref[...]
            cr_ref[...] += res2[:, QB:]
            dz = g * one_m_beta - beta * (dtot - rinc)
            if diag:
                dz = jnp.where(valid, dz, 0.0)
            dzb = dz.astype(BF16)
            dq_acc[...] += jnp.dot(dzb, kb, preferred_element_type=F32)
            dk_acc[pl.ds(start, QB), :] += lax.dot_general(
                dzb, q2, (((0,), (0,)), ((), ())), preferred_element_type=F32)
            dv_acc[pl.ds(start, QB), :] += lax.dot_general(
                w.astype(BF16), do2, (((0,), (0,)), ((), ())), preferred_element_type=F32)

        step(i, True)
        _sweep_older(i, step, ce_ref)
        dq_ref[...] =(_unstack_heads(dq_acc[...]) * scale).astype(dq_ref.dtype)

        @pl.when(i == nb - 1)
        def _():
            dk_ref[...] = dk_acc[...].astype(dk_ref.dtype)
            dv_ref[...] = dv_acc[...].astype(dv_ref.dtype)

    blk = pl.BlockSpec((QB, LANES), lambda p, i: (i, p))
    col = pl.BlockSpec((L, LANES), lambda p, i: (0, p))
    return pl.pallas_call(
        body, name=name, grid=(n_pairs, nb),
        in_specs=[blk,
                  pl.BlockSpec((L, LANES), lambda p, i: (0, n_pairs + p)),
                  pl.BlockSpec((L, LANES), lambda p, i: (0, 2 * n_pairs + p)),
                  blk, blk],
        out_specs=[blk, col, col],
        out_shape=[jax.ShapeDtypeStruct((L, sb_width), BF16)] * 3,
        scratch_shapes=[pltpu.VMEM((2 * QB, LANES), F32), pltpu.VMEM((L, LANES), F32),
                        pltpu.VMEM((L, LANES), F32), pltpu.VMEM((2 * QB, LANES), F32),
                        pltpu.VMEM((2 * QB, LANES), F32)],
        compiler_params=_params(("parallel", "arbitrary")),
    )(qkv, qkv, qkv, o, dmixed)


def _conv_tile(L):
    return _pick(L, (384, 128))


def _glu(x, C):
    return x[:, :C] * _sigmoid(x[:, C:])


def _conv_fwd(cacg, dw_w, dw_b, ln_g, ln_b, name):
    L, C2 = cacg.shape
    C = C2 // 2
    T = _conv_tile(L)
    H = CONV_HALO
    K = dw_w.shape[0]
    CH = 64 if T % 64 == 0 else T

    def body(x_ref, prev_ref, w_ref, b_ref, g_ref, beta_ref, o_ref, u_ref):
        i = pl.program_id(0)
        u_ref[0:H, :] = jnp.where(i > 0, _glu(prev_ref[...], C), 0.0)
        u_ref[H:, :] = _glu(x_ref[...], C)
        for c0 in range(0, T, CH):
            y = jnp.broadcast_to(b_ref[...], (CH, C))
            for k in range(K):
                y = y + w_ref[k:k + 1, :] * u_ref[c0 + H - (K - 1) + k:c0 + H - (K - 1) + k + CH, :]
            mu = jnp.mean(y, axis=-1, keepdims=True)
            yc = y - mu
            rstd = lax.rsqrt(jnp.mean(yc * yc, axis=-1, keepdims=True) + EPS)
            ln = yc * rstd * g_ref[...] + beta_ref[...]
            o_ref[c0:c0 + CH, :] = (ln * _sigmoid(ln)).astype(o_ref.dtype)

    vec = pl.BlockSpec((1, C), lambda i: (0, 0))
    return pl.pallas_call(
        body, name=name, grid=(L // T,),
        in_specs=[pl.BlockSpec((T, C2), lambda i: (i, 0)),
                  pl.BlockSpec((H, C2), lambda i: (jnp.maximum(i * (T // H) - 1, 0), 0)),
                  pl.BlockSpec((K, C), lambda i: (0, 0)), vec, vec, vec],
        out_specs=pl.BlockSpec((T, C), lambda i: (i, 0)),
        out_shape=jax.ShapeDtypeStruct((L, C), BF16),
        scratch_shapes=[pltpu.VMEM((T + H, C), F32)],
        compiler_params=_params(("parallel",)),
    )(cacg, cacg, dw_w, dw_b, ln_g, ln_b)


def _conv_bwd(cacg, dmixed, dw_w, dw_b, ln_g, ln_b, name):
    L, C2 = cacg.shape
    C = C2 // 2
    T = _conv_tile(L)
    H = CONV_HALO
    K = dw_w.shape[0]
    nt = L // T
    TE = T + H

    def body(x_ref, prev_ref, next_ref, d_ref, dnext_ref, w_ref, b_ref, g_ref, beta_ref,
             dca_ref, dcg_ref, dwt_ref, db_ref, dg_ref, dbeta_ref, u_ref, dy_ref):
        i = pl.program_id(0)
        last = i == nt - 1
        u_ref[0:H, :] = jnp.where(i > 0, _glu(prev_ref[...], C), 0.0)
        u_ref[H:H + T, :] = _glu(x_ref[...], C)
        u_ref[H + T:, :] = _glu(next_ref[...], C)
        y = jnp.broadcast_to(b_ref[...], (TE, C))
        for k in range(K):
            y = y + w_ref[k:k + 1, :] * u_ref[H - (K - 1) + k:H - (K - 1) + k + TE, :]
        mu = jnp.mean(y, axis=-1, keepdims=True)
        yc = y - mu
        rstd = lax.rsqrt(jnp.mean(yc * yc, axis=-1, keepdims=True) + EPS)
        yh = yc * rstd
        ln = yh * g_ref[...] + beta_ref[...]
        s = _sigmoid(ln)
        dout = jnp.concatenate([d_ref[...], jnp.where(last, 0.0, dnext_ref[...])], axis=0)
        dln = dout * (s * (1.0 + ln * (1.0 - s)))
        dyh = dln * g_ref[...]
        dy = rstd * (dyh - jnp.mean(dyh, axis=-1, keepdims=True)
                     - yh * jnp.mean(dyh * yh, axis=-1, keepdims=True))
        dy_ref[...] = dy

        @pl.when(i == 0)
        def _():
            dwt_ref[...] = jnp.zeros_like(dwt_ref)
            db_ref[...] = jnp.zeros_like(db_ref)
            dg_ref[...] = jnp.zeros_like(dg_ref)
            dbeta_ref[...] = jnp.zeros_like(dbeta_ref)

        dg_ref[...] += jnp.sum((dln * yh)[:T], axis=0, keepdims=True)
        dbeta_ref[...] += jnp.sum(dln[:T], axis=0, keepdims=True)
        dy_t = dy[:T]
        db_ref[...] += jnp.sum(dy_t, axis=0, keepdims=True)
        du = jnp.zeros((T, C), F32)
        for k in range(K):
            off = H - (K - 1) + k
            dwt_ref[k:k + 1, :] += jnp.sum(dy_t * u_ref[off:off + T, :], axis=0, keepdims=True)
            du = du + w_ref[k:k + 1, :] * dy_ref[(K - 1) - k:(K - 1) - k + T, :]
        x = x_ref[...]
        sg = _sigmoid(x[:, C:])
        dca_ref[...] = (du * sg).astype(dca_ref.dtype)
        dcg_ref[...] = (du * x[:, :C] * sg * (1.0 - sg)).astype(dcg_ref.dtype)

    nh = L // H
    vec = pl.BlockSpec((1, C), lambda i: (0, 0))
    row = pl.BlockSpec((T, C), lambda i: (i, 0))
    return pl.pallas_call(
        body, name=name, grid=(nt,),
        in_specs=[pl.BlockSpec((T, C2), lambda i: (i, 0)),
                  pl.BlockSpec((H, C2), lambda i: (jnp.maximum(i * (T // H) - 1, 0), 0)),
                  pl.BlockSpec((H, C2), lambda i: (jnp.minimum((i + 1) * (T // H), nh - 1), 0)),
                  pl.BlockSpec((T, C), lambda i: (i, 1)),
                  pl.BlockSpec((H, C), lambda i: (jnp.minimum((i + 1) * (T // H), nh - 1), 1)),
                  pl.BlockSpec((K, C), lambda i: (0, 0)), vec, vec, vec],
        out_specs=[row, row, pl.BlockSpec((H, C), lambda i: (0, 0)), vec, vec, vec],
        out_shape=[jax.ShapeDtypeStruct((L, C), BF16), jax.ShapeDtypeStruct((L, C), BF16),
                   jax.ShapeDtypeStruct((H, C), F32), jax.ShapeDtypeStruct((1, C), F32),
                   jax.ShapeDtypeStruct((1, C), F32), jax.ShapeDtypeStruct((1, C), F32)],
        scratch_shapes=[pltpu.VMEM((T + 2 * H, C), F32), pltpu.VMEM((TE, C), F32)],
        compiler_params=_params(("arbitrary",)),
    )(cacg, cacg, cacg, dmixed, dmixed, dw_w, dw_b, ln_g, ln_b)


def _local_step(h0, target, n_meta, seq, norms, conv_p, wts, final_g):
    mix_g, ffn_g = norms
    dw_w, dw_b, ln_g, ln_b = conv_p
    depth = mix_g.shape[0]
    C = dw_b.shape[-1]
    sbw = (wts["w_in"].shape[-1] - 2 * C) // 3
    assert sbw == C, "the mixer halves must have equal width"
    row = lambda a, i: a[i][None, :]

    h = h0
    saved = []
    for i in range(depth):
        hn = _rmsnorm_fwd(h, row(mix_g, i), f"mix_norm_{i}")
        proj_qkv = _mm_nn([(hn, wts["w_in"], i, 0)], BF16, f"in_qkv_{i}", cols=(0, 3 * sbw))
        cacg = _mm_nn([(hn, wts["w_in"], i, 0)], F32, f"in_conv_{i}", cols=(3 * sbw, 2 * C))
        attn = _attn_fwd(proj_qkv, sbw, f"attn_fwd_{i}")
        conv = _conv_fwd(cacg, dw_w[i], row(dw_b, i), row(ln_g, i), row(ln_b, i), f"conv_fwd_{i}")
        h_mid = _mm_nn([(attn, wts["w_out"], i, 0), (conv, wts["w_out"], i, 1)], F32, f"out_proj_{i}", residual=h)
        hn2 = _rmsnorm_fwd(h_mid, row(ffn_g, i), f"ffn_norm_{i}")
        g = _mm_nt([(hn2, wts["w_gate_t"], i, 0)], F32, f"gate_{i}")
        u = _mm_nt([(hn2, wts["w_up_t"], i, 0)], F32, f"up_{i}")
        act = _swiglu_fwd(g, u, f"swiglu_{i}")
        h_out = _mm_nn([(act, wts["w_down"], i, 0)], F32, f"down_{i}", residual=h_mid)
        saved.append((h, hn, proj_qkv, cacg, attn, conv, h_mid, hn2, g, u, act))
        h = h_out

    loss, dh, d_final_g = _loss_head(h, final_g[None, :], target, n_meta, seq, "loss_head")

    grads = {k: [None] * depth for k in ("w_in", "w_out", "w_gate_t", "w_up_t", "w_down", "mix_g", "ffn_g",
                                         "dw_w", "dw_b", "ln_g", "ln_b")}
    for i in reversed(range(depth)):
        h_in, hn, proj_qkv, cacg, attn, conv, h_mid, hn2, g, u, act = saved[i]
        dact = _mm_nt([(dh, wts["w_down"], i, 0)], F32, f"d_act_{i}")
        grads["w_down"][i] = _mm_tn(act, dh, f"dw_down_{i}", col_sharded=False)
        dg, du = _swiglu_bwd(g, u, dact, f"swiglu_bwd_{i}")
        dhn2 = _mm_nn([(dg, wts["w_gate_t"], i, 0), (du, wts["w_up_t"], i, 0)], F32, f"d_hn2_{i}")
        grads["w_gate_t"][i] = _mm_tn(dg, hn2, f"dw_gate_{i}", col_sharded=False)
        grads["w_up_t"][i] = _mm_tn(du, hn2, f"dw_up_{i}", col_sharded=False)
        dh, grads["ffn_g"][i] = _rmsnorm_bwd(h_mid, row(ffn_g, i), dhn2, dh, f"ffn_norm_bwd_{i}")
        dmixed = _mm_nt([(dh, wts["w_out"], i, 0)], F32, f"d_mixed_{i}")
        mixed = jnp.concatenate([attn.astype(BF16), conv], axis=1)
        grads["w_out"][i] = _mm_tn(mixed, dh, f"dw_out_{i}", col_sharded=False)
        dq, dk, dv = _attn_bwd(proj_qkv, attn, dmixed, sbw, f"attn_bwd_{i}")
        dca, dcg, d_dw, d_b, d_lg, d_lb = _conv_bwd(
            cacg, dmixed, dw_w[i], row(dw_b, i), row(ln_g, i), row(ln_b, i), f"conv_bwd_{i}")
        grads["dw_w"][i] = d_dw[:dw_w.shape[1]]
        grads["dw_b"][i], grads["ln_g"][i], grads["ln_b"][i] = d_b[0], d_lg[0], d_lb[0]
        dproj = jnp.concatenate([dq, dk, dv, dca, dcg], axis=1)
        dhn = _mm_nt([(dproj, wts["w_in"], i, 0)], F32, f"d_hn_{i}")
        grads["w_in"][i] = _mm_tn(hn, dproj, f"dw_in_{i}", col_sharded=True)
        dh, d_mix = _rmsnorm_bwd(h_in, row(mix_g, i), dhn, dh, f"mix_norm_bwd_{i}")
        grads["mix_g"][i] = d_mix[0]
        grads["ffn_g"][i] = grads["ffn_g"][i][0]
    grads["final_g"] = d_final_g[0]
    return loss, dh, grads


ANY = pl.BlockSpec(memory_space=pl.ANY)


def _position():
    return lax.axis_index("x"), lax.axis_index("y"), lax.axis_index("c")


def _chip_at(x, y, k):
    return (1 - x if k & 2 else x), (1 - y if k & 1 else y)


def _half_rows(ref, half, rows, base=0):
    start = pl.multiple_of(base + half * rows, 8)
    lead = (slice(None),) * (len(ref.shape) - 2)
    return ref.at[(*lead, pl.ds(start, rows), slice(None))]


def _gather_weights(shards, col_sharded):
    n = len(shards)
    fulls = []
    for s, col in zip(shards, col_sharded):
        lyr, R, C = s.shape
        fulls.append(jax.ShapeDtypeStruct((lyr, R, N_CHIPS * C) if col else (lyr, N_CHIPS * R, C), s.dtype))

    def body(*refs):
        s_refs, f_refs = refs[:n], refs[n:2 * n]
        send_sems, recv_sems, local_sems = refs[2 * n:]
        x, y, c = _position()
        me = 2 * x + y

        def block(wi, chip, half):
            _, R, C = shards[wi].shape
            if col_sharded[wi]:
                cols = pl.ds(pl.multiple_of(chip * C, LANES), C)
                return f_refs[wi].at[:, pl.ds(pl.multiple_of(half * (R // 2), 8), R // 2), cols]
            return _half_rows(f_refs[wi], half, R // 2, base=chip * R)

        def own(wi):
            _, R, C = shards[wi].shape
            if col_sharded[wi]:
                return f_refs[wi].at[:, :, pl.ds(pl.multiple_of(me * C, LANES), C)]
            return f_refs[wi].at[:, pl.ds(pl.multiple_of(me * R, 8), R), :]

        def copy(wi, slot, src, dst, to):
            return pltpu.make_async_remote_copy(
                src_ref=src, dst_ref=dst, send_sem=send_sems.at[6 * wi + slot],
                recv_sem=recv_sems.at[6 * wi + slot], device_id=to, device_id_type=MESH)

        local = [pltpu.make_async_copy(s_refs[wi], own(wi), local_sems.at[wi]) for wi in range(n)]
        for cp in local:
            cp.start()
        sent = []
        for wi in range(n):
            R = shards[wi].shape[1]
            for k in (1, 2, 3):
                cp = copy(wi, k - 1, _half_rows(s_refs[wi], c, R // 2), block(wi, me, c), (*_chip_at(x, y, k), c))
                cp.start()
                sent.append(cp)
        for wi in range(n):
            for k in (1, 2, 3):
                landed = block(wi, me ^ k, c)
                copy(wi, k - 1, landed, landed, (x, y, c)).wait_recv()
                cp = copy(wi, 2 + k, landed, landed, (x, y, 1 - c))
                cp.start()
                sent.append(cp)
        for wi in range(n):
            for k in (1, 2, 3):
                passed = block(wi, me ^ k, 1 - c)
                copy(wi, 2 + k, passed, passed, (x, y, c)).wait_recv()
        for cp in sent:
            cp.wait_send()
        for cp in local:
            cp.wait()

    return pl.pallas_call(
        body, name="gather_weights", out_shape=fulls,
        in_specs=[ANY] * n, out_specs=[ANY] * n,
        scratch_shapes=[pltpu.SemaphoreType.DMA((6 * n,)), pltpu.SemaphoreType.DMA((6 * n,)),
                        pltpu.SemaphoreType.DMA((n,))],
    )(*shards)


def _to_bf16(w, name):
    lyr, R, C = w.shape
    tr = _pick(R, (256, 352, 128))

    def body(w_ref, o_ref):
        o_ref[...] = w_ref[...].astype(BF16)

    blk = pl.BlockSpec((None, tr, C), lambda l, r: (l, r, 0))
    return pl.pallas_call(
        body, name=name, grid=(lyr, R // tr), in_specs=[blk], out_specs=blk,
        out_shape=jax.ShapeDtypeStruct(w.shape, BF16), compiler_params=_params(("parallel", "parallel")),
    )(w)


def _rs_to_sibling(grads):
    n = len(grads)
    outs = [jax.ShapeDtypeStruct((g.shape[0], g.shape[1] // 2, g.shape[2]), g.dtype) for g in grads]

    def body(*refs):
        g_refs, l_refs = refs[:n], refs[n:2 * n]
        send_sems, recv_sems = refs[2 * n:]
        x, y, c = _position()
        cps = []
        for wi in range(n):
            R = grads[wi].shape[1]
            cp = pltpu.make_async_remote_copy(
                src_ref=_half_rows(g_refs[wi], 1 - c, R // 2), dst_ref=l_refs[wi],
                send_sem=send_sems.at[wi], recv_sem=recv_sems.at[wi],
                device_id=(x, y, 1 - c), device_id_type=MESH)
            cp.start()
            cps.append(cp)
        for cp in cps:
            cp.wait()

    return pl.pallas_call(
        body, name="grads_to_sibling", out_shape=outs, in_specs=[ANY] * n, out_specs=[ANY] * n,
        scratch_shapes=[pltpu.SemaphoreType.DMA((n,)), pltpu.SemaphoreType.DMA((n,))],
    )(*grads)


def _chip_sum(g, landed, core, name):
    _, R, C = g.shape
    hr = R // 2
    tr = _pick(hr, (256, 352, 128))
    nr = hr // tr

    def body(c_ref, g_ref, l_ref, o_ref):
        o_ref[...] = (g_ref[...] + l_ref[...]).astype(BF16)

    grid_spec = pltpu.PrefetchScalarGridSpec(
        num_scalar_prefetch=1, grid=(N_CHIPS, nr),
        in_specs=[pl.BlockSpec((None, tr, C), lambda j, r, c_ref: (j, c_ref[0] * nr + r, 0)),
                  pl.BlockSpec((None, tr, C), lambda j, r, c_ref: (j, r, 0))],
        out_specs=pl.BlockSpec((None, tr, C), lambda j, r, c_ref: (j, r, 0)))
    return pl.pallas_call(
        body, name=name, grid_spec=grid_spec, out_shape=jax.ShapeDtypeStruct((N_CHIPS, hr, C), BF16),
        compiler_params=_params(("parallel", "parallel")),
    )(core, g, landed)


def _rs_across_chips(parts):
    n = len(parts)
    outs = [jax.ShapeDtypeStruct(p.shape, p.dtype) for p in parts]

    def body(*refs):
        p_refs, l_refs = refs[:n], refs[n:2 * n]
        send_sems, recv_sems, local_sems = refs[2 * n:]
        x, y, c = _position()
        me = 2 * x + y
        local, sent = [], []
        for wi in range(n):
            cp = pltpu.make_async_copy(p_refs[wi].at[me], l_refs[wi].at[me], local_sems.at[wi])
            cp.start()
            local.append(cp)
            for k in (1, 2, 3):
                cp = pltpu.make_async_remote_copy(
                    src_ref=p_refs[wi].at[me ^ k], dst_ref=l_refs[wi].at[me],
                    send_sem=send_sems.at[3 * wi + k - 1], recv_sem=recv_sems.at[3 * wi + k - 1],
                    device_id=(*_chip_at(x, y, k), c), device_id_type=MESH)
                cp.start()
                sent.append(cp)
        for wi in range(n):
            for k in (1, 2, 3):
                slot = l_refs[wi].at[me ^ k]
                pltpu.make_async_remote_copy(
                    src_ref=slot, dst_ref=slot, send_sem=send_sems.at[3 * wi + k - 1],
                    recv_sem=recv_sems.at[3 * wi + k - 1], device_id=(x, y, c), device_id_type=MESH).wait_recv()
        for cp in sent:
            cp.wait_send()
        for cp in local:
            cp.wait()

    return pl.pallas_call(
        body, name="grads_across_chips", out_shape=outs, in_specs=[ANY] * n, out_specs=[ANY] * n,
        scratch_shapes=[pltpu.SemaphoreType.DMA((3 * n,)), pltpu.SemaphoreType.DMA((3 * n,)),
                        pltpu.SemaphoreType.DMA((n,))],
    )(*parts)


def _sum_chips(landed, name):
    _, hr, C = landed.shape
    tr = _pick(hr, (256, 352, 128))

    def body(l_ref, o_ref):
        o_ref[...] = ((l_ref[0].astype(F32) + l_ref[1].astype(F32)) + l_ref[2].astype(F32)) + l_ref[3].astype(F32)

    return pl.pallas_call(
        body, name=name, grid=(hr // tr,),
        in_specs=[pl.BlockSpec((N_CHIPS, tr, C), lambda r: (0, r, 0))],
        out_specs=pl.BlockSpec((tr, C), lambda r: (r, 0)),
        out_shape=jax.ShapeDtypeStruct((hr, C), F32), compiler_params=_params(("parallel",)),
    )(landed)


def _rs_join_halves(halves):
    n = len(halves)
    depth = len(halves[0])
    outs = [jax.ShapeDtypeStruct((depth, 2 * h[0].shape[0], h[0].shape[1]), F32) for h in halves]

    def body(*refs):
        h_refs = [refs[wi * depth:(wi + 1) * depth] for wi in range(n)]
        o_refs = refs[n * depth:n * depth + n]
        send_sems, recv_sems, local_sems = refs[n * depth + n:]
        x, y, c = _position()
        local, sent = [], []
        for wi in range(n):
            hr = halves[wi][0].shape[0]
            for l in range(depth):
                s = wi * depth + l
                mine = _half_rows(o_refs[wi].at[l], c, hr)
                cp = pltpu.make_async_copy(h_refs[wi][l], mine, local_sems.at[s])
                cp.start()
                local.append(cp)
                cp = pltpu.make_async_remote_copy(
                    src_ref=h_refs[wi][l], dst_ref=mine, send_sem=send_sems.at[s], recv_sem=recv_sems.at[s],
                    device_id=(x, y, 1 - c), device_id_type=MESH)
                cp.start()
                sent.append(cp)
        for wi in range(n):
            hr = halves[wi][0].shape[0]
            for l in range(depth):
                s = wi * depth + l
                theirs = _half_rows(o_refs[wi].at[l], 1 - c, hr)
                pltpu.make_async_remote_copy(
                    src_ref=theirs, dst_ref=theirs, send_sem=send_sems.at[s], recv_sem=recv_sems.at[s],
                    device_id=(x, y, c), device_id_type=MESH).wait_recv()
        for cp in sent:
            cp.wait_send()
        for cp in local:
            cp.wait()

    flat = [h for hs in halves for h in hs]
    return pl.pallas_call(
        body, name="grads_join_halves", out_shape=outs, in_specs=[ANY] * len(flat), out_specs=[ANY] * n,
        scratch_shapes=[pltpu.SemaphoreType.DMA((n * depth,)), pltpu.SemaphoreType.DMA((n * depth,)),
                        pltpu.SemaphoreType.DMA((n * depth,))],
    )(*flat)


def _small_allreduce(vec):
    rows = vec.shape[0]

    def body(v_ref, o_ref, land, send_sems, recv_sems):
        x, y, c = _position()
        me = 4 * x + 2 * y + c
        land[0] = v_ref[...]
        sent = []
        for k in range(1, N_DEV):
            to = (1 - x if k & 4 else x, 1 - y if k & 2 else y, 1 - c if k & 1 else c)
            cp = pltpu.make_async_remote_copy(
                src_ref=v_ref, dst_ref=land.at[k], send_sem=send_sems.at[k - 1], recv_sem=recv_sems.at[k - 1],
                device_id=to, device_id_type=MESH)
            cp.start()
            sent.append(cp)
        for cp in sent:
            cp.wait_recv()
        acc = land[me]
        for e in range(1, N_DEV):
            acc = acc + land[me ^ e]
        o_ref[...] = acc
        for cp in sent:
            cp.wait_send()

    vmem = pl.BlockSpec(memory_space=pltpu.VMEM)
    return pl.pallas_call(
        body, name="small_allreduce", out_shape=jax.ShapeDtypeStruct(vec.shape, F32),
        in_specs=[vmem], out_specs=vmem,
        scratch_shapes=[pltpu.VMEM((N_DEV, rows, LANES), F32), pltpu.SemaphoreType.DMA((N_DEV - 1,)),
                        pltpu.SemaphoreType.DMA((N_DEV - 1,))],
    )(vec)


def _adam_math(w, g, m, v):
    m = ADAM_B1 * m + (1.0 - ADAM_B1) * g
    v = ADAM_B2 * v + (1.0 - ADAM_B2) * jnp.square(g)
    m_hat = m / (1.0 - ADAM_B1 ** ADAM_STEP)
    v_hat = v / (1.0 - ADAM_B2 ** ADAM_STEP)
    delta = -ADAM_LR * (m_hat / (jnp.sqrt(v_hat) + ADAM_EPS) + ADAM_WD * w)
    return delta, m, v


def _adam(w, g, m, v, name):
    def body(w_ref, g_ref, m_ref, v_ref, d_ref, nm_ref, nv_ref):
        d_ref[...], nm_ref[...], nv_ref[...] = _adam_math(w_ref[...], g_ref[...], m_ref[...], v_ref[...])

    if w.ndim == 3:
        lyr, R, C = w.shape
        tr = _pick(R, (256, 352, 128))
        blk = pl.BlockSpec((None, tr, C), lambda l, r: (l, r, 0))
        grid, sem = (lyr, R // tr), ("parallel", "parallel")
    else:
        blk = pl.BlockSpec(w.shape, lambda: (0, 0))
        grid, sem = (), None
    return pl.pallas_call(
        body, name=name, grid=grid, in_specs=[blk] * 4, out_specs=[blk] * 3,
        out_shape=[jax.ShapeDtypeStruct(w.shape, F32)] * 3, compiler_params=_params(sem),
    )(w, g, m, v)


def _rows(a, pad_to=8):
    r = a.reshape(-1, LANES)
    extra = (-r.shape[0]) % pad_to
    return jnp.pad(r, ((0, extra), (0, 0))) if extra else r


def _pack(arrays):
    return jnp.concatenate([_rows(a) for a in arrays], axis=0)


def _unpack(slab, shapes):
    out, at = [], 0
    for shp in shapes:
        nrow = math.prod(shp) // LANES
        out.append(slab[at:at + nrow].reshape(shp))
        at += nrow + (-nrow) % 8
    return out


BIG = ("w_in", "w_out", "w_gate_t", "w_up_t", "w_down")
BIG_COL_SHARDED = (True, False, False, False, False)
TRANSPOSED = {"w_gate_t": "w_gate", "w_up_t": "w_up"}


def kernel(x, meta_tokens, mix_norm_g, w_in, conv_dw_w, conv_dw_b, conv_ln_g, conv_ln_b, w_out, ffn_norm_g, w_gate, w_up, w_down, final_norm_g, loss_target, m_meta_tokens, m_mix_norm_g, m_w_in, m_conv_dw_w, m_conv_dw_b, m_conv_ln_g, m_conv_ln_b, m_w_out, m_ffn_norm_g, m_w_gate, m_w_up, m_w_down, m_final_norm_g, v_meta_tokens, v_mix_norm_g, v_w_in, v_conv_dw_w, v_conv_dw_b, v_conv_ln_g, v_conv_ln_b, v_w_out, v_ffn_norm_g, v_w_gate, v_w_up, v_w_down, v_final_norm_g):
    n_meta, seq = meta_tokens.shape[0], x.shape[1]
    D = x.shape[2]
    depth, taps, c_shard = conv_dw_w.shape
    C = conv_dw_b.shape[-1]
    chip = 2 * lax.axis_index("x") + lax.axis_index("y")
    core = lax.axis_index("c").astype(jnp.int32).reshape(1)
    big_w = dict(w_in=w_in, w_out=w_out, w_gate=w_gate, w_up=w_up, w_down=w_down)
    big_m = dict(w_in=m_w_in, w_out=m_w_out, w_gate=m_w_gate, w_up=m_w_up, w_down=m_w_down)
    big_v = dict(w_in=v_w_in, w_out=v_w_out, w_gate=v_w_gate, w_up=v_w_up, w_down=v_w_down)

    small_shard = _pack([conv_dw_w, meta_tokens])[None]
    to_send = [jnp.swapaxes(big_w[TRANSPOSED[k]], 1, 2) if k in TRANSPOSED else big_w[k] for k in BIG]
    gathered = _gather_weights([_to_bf16(w, f"to_bf16_{k}") for k, w in zip(BIG, to_send)] + [small_shard],
                               BIG_COL_SHARDED + (False,))
    wts = dict(zip(BIG, gathered[:-1]))
    rows_shard = small_shard.shape[1]
    dw_full, meta_full = [], []
    for j in range(N_CHIPS):
        dwj, mj = _unpack(gathered[-1][0, j * rows_shard:(j + 1) * rows_shard],
                          [conv_dw_w.shape, meta_tokens.shape])
        dw_full.append(dwj)
        meta_full.append(mj)
    dw_w_full = jnp.concatenate(dw_full, axis=2)
    meta = jnp.concatenate(meta_full, axis=1)

    L = n_meta + seq
    Lp = -(-L // QUERY_BLOCK) * QUERY_BLOCK
    h0 = jnp.concatenate([meta, x[0], jnp.zeros((Lp - L, D), F32)], axis=0)
    target = jnp.pad(loss_target[0], ((n_meta, Lp - L), (0, 0)))
    loss, dh0, grads = _local_step(h0, target, n_meta, seq, (mix_norm_g, ffn_norm_g),
                                   (dw_w_full, conv_dw_b, conv_ln_g, conv_ln_b), wts, final_norm_g)
    loss = lax.psum(loss[0, 0], ("x", "y", "c"))
    grad_x = dh0[n_meta:L][None]

    flat = [grads[k][l] for k in BIG for l in range(depth)]
    landed = _rs_to_sibling(flat)
    parts = [_chip_sum(g, la, core, f"chip_sum_{i}") for i, (g, la) in enumerate(zip(flat, landed))]
    across = _rs_across_chips(parts)
    halves = [_sum_chips(a, f"sum_chips_{i}") for i, a in enumerate(across)]
    big_g = dict(zip(BIG, _rs_join_halves([halves[i * depth:(i + 1) * depth] for i in range(len(BIG))])))

    small_names = ("mix_g", "ffn_g", "dw_b", "ln_g", "ln_b", "dw_w")
    small = [jnp.stack(grads[k]) for k in small_names] + [grads["final_g"], dh0[:n_meta]]
    small_shapes = [a.shape for a in small]
    g_mix, g_ffn, g_dwb, g_lng, g_lnb, g_dww, g_final, g_meta = _unpack(_small_allreduce(_pack(small)), small_shapes)
    g_dww = lax.dynamic_slice_in_dim(g_dww, chip * c_shard, c_shard, axis=2)
    g_meta = lax.dynamic_slice_in_dim(g_meta, chip * meta_tokens.shape[1], meta_tokens.shape[1], axis=1)

    out_g, out_d, out_m, out_v = {}, {}, {}, {}
    for kk in BIG:
        k = TRANSPOSED.get(kk, kk)
        out_g[k] = jnp.swapaxes(big_g[kk], 1, 2) if kk in TRANSPOSED else big_g[kk]
        out_d[k], out_m[k], out_v[k] = _adam(big_w[k], out_g[k], big_m[k], big_v[k], f"adam_{k}")
    small_order = ("meta_tokens", "mix_norm_g", "conv_dw_w", "conv_dw_b", "conv_ln_g", "conv_ln_b",
                   "ffn_norm_g", "final_norm_g")
    sw = dict(meta_tokens=meta_tokens, mix_norm_g=mix_norm_g, conv_dw_w=conv_dw_w, conv_dw_b=conv_dw_b,
              conv_ln_g=conv_ln_g, conv_ln_b=conv_ln_b, ffn_norm_g=ffn_norm_g, final_norm_g=final_norm_g)
    sm = dict(meta_tokens=m_meta_tokens, mix_norm_g=m_mix_norm_g, conv_dw_w=m_conv_dw_w, conv_dw_b=m_conv_dw_b,
              conv_ln_g=m_conv_ln_g, conv_ln_b=m_conv_ln_b, ffn_norm_g=m_ffn_norm_g, final_norm_g=m_final_norm_g)
    sv = dict(meta_tokens=v_meta_tokens, mix_norm_g=v_mix_norm_g, conv_dw_w=v_conv_dw_w, conv_dw_b=v_conv_dw_b,
              conv_ln_g=v_conv_ln_g, conv_ln_b=v_conv_ln_b, ffn_norm_g=v_ffn_norm_g, final_norm_g=v_final_norm_g)
    sg = dict(meta_tokens=g_meta, mix_norm_g=g_mix, conv_dw_w=g_dww, conv_dw_b=g_dwb, conv_ln_g=g_lng,
              conv_ln_b=g_lnb, ffn_norm_g=g_ffn, final_norm_g=g_final)
    def slab(d):
        return _pack([d[k] for k in small_order])
    shapes = [sw[k].shape for k in small_order]
    deltas = _adam(slab(sw), slab(sg), slab(sm), slab(sv), "adam_small")
    for res, dst in zip(deltas, (out_d, out_m, out_v)):
        dst.update(zip(small_order, _unpack(res, shapes)))
    out_g.update(sg)

    order = ("meta_tokens", "mix_norm_g", "w_in", "conv_dw_w", "conv_dw_b", "conv_ln_g", "conv_ln_b", "w_out",
             "ffn_norm_g", "w_gate", "w_up", "w_down", "final_norm_g")
    return (loss, grad_x, *[out_g[k] for k in order], *[out_d[k] for k in order],
            *[out_m[k] for k in order], *[out_v[k] for k in order])
```

```python
import functools
import math

import jax
import jax.numpy as jnp
from jax import lax
from jax.experimental import pallas as pl
from jax.experimental.pallas import tpu as pltpu

F32 = jnp.float32
BF16 = jnp.bfloat16
MESH = pl.DeviceIdType.MESH

EPS = 1e-6
QUERY_BLOCK = 128
LANES = 128
HEAD_DIM = 64
LOG_STICK_FLOOR = -40.0
CONV_HALO = 32
N_CHIPS = 4
N_DEV = 8
VMEM_LIMIT = 56 * 1024 * 1024

ADAM_LR = 0.001
ADAM_B1 = 0.9
ADAM_B2 = 0.999
ADAM_EPS = 1e-08
ADAM_WD = 0.01
ADAM_STEP = 10


def _pick(n, prefs):
    for p in prefs:
        if n % p == 0:
            return p
    return n


def _params(sem=None):
    return pltpu.CompilerParams(dimension_semantics=sem, vmem_limit_bytes=VMEM_LIMIT)


def _sigmoid(x):
    return 1.0 / (1.0 + jnp.exp(-x))


def _rmsnorm_fwd(h, g, name):
    L, D = h.shape
    T = _pick(L, (384, 128))

    def body(h_ref, g_ref, o_ref):
        x = h_ref[...]
        r = lax.rsqrt(jnp.mean(x * x, axis=-1, keepdims=True) + EPS)
        o_ref[...] = (x * r * g_ref[...]).astype(o_ref.dtype)

    return pl.pallas_call(
        body, name=name, grid=(L // T,),
        in_specs=[pl.BlockSpec((T, D), lambda i: (i, 0)), pl.BlockSpec((1, D), lambda i: (0, 0))],
        out_specs=pl.BlockSpec((T, D), lambda i: (i, 0)),
        out_shape=jax.ShapeDtypeStruct((L, D), BF16),
        compiler_params=_params(("parallel",)),
    )(h, g)


def _rmsnorm_bwd(h, g, dy, dh_in, name):
    L, D = h.shape
    T = _pick(L, (384, 128))

    def body(h_ref, g_ref, dy_ref, dhin_ref, dh_ref, dg_ref):
        x = h_ref[...]
        dyv = dy_ref[...]
        r = lax.rsqrt(jnp.mean(x * x, axis=-1, keepdims=True) + EPS)
        xh = x * r
        dxh = dyv * g_ref[...]
        dh_ref[...] = dhin_ref[...] + r * (dxh - xh * jnp.mean(dxh * xh, axis=-1, keepdims=True))

        @pl.when(pl.program_id(0) == 0)
        def _():
            dg_ref[...] = jnp.zeros_like(dg_ref)

        dg_ref[...] += jnp.sum(dyv * xh, axis=0, keepdims=True)

    row = pl.BlockSpec((T, D), lambda i: (i, 0))
    vec = pl.BlockSpec((1, D), lambda i: (0, 0))
    return pl.pallas_call(
        body, name=name, grid=(L // T,),
        in_specs=[row, vec, row, row], out_specs=[row, vec],
        out_shape=[jax.ShapeDtypeStruct((L, D), F32), jax.ShapeDtypeStruct((1, D), F32)],
        compiler_params=_params(("arbitrary",)),
    )(h, g, dy, dh_in)


def _loss_head(h, g, target, n_meta, seq, name):
    L, D = h.shape
    T = _pick(L, (384, 128))

    def body(h_ref, g_ref, t_ref, loss_ref, dh_ref, dg_ref):
        i = pl.program_id(0)
        x = h_ref[...]
        gv = g_ref[...]
        r = lax.rsqrt(jnp.mean(x * x, axis=-1, keepdims=True) + EPS)
        xh = x * r
        y = xh * gv
        rows = i * T + lax.broadcasted_iota(jnp.int32, (T, 1), 0)
        live = (rows >= n_meta) & (rows < n_meta + seq)
        diff = jnp.where(live, y - t_ref[...], 0.0)
        dyv = diff / D
        dxh = dyv * gv
        dh_ref[...] = r * (dxh - xh * jnp.mean(dxh * xh, axis=-1, keepdims=True))

        @pl.when(i == 0)
        def _():
            dg_ref[...] = jnp.zeros_like(dg_ref)
            loss_ref[...] = jnp.zeros_like(loss_ref)

        dg_ref[...] += jnp.sum(dyv * xh, axis=0, keepdims=True)
        per_row = jnp.mean(diff * diff, axis=-1, keepdims=True)
        loss_ref[...] += 0.5 * jnp.sum(per_row, axis=0, keepdims=True)

    row = pl.BlockSpec((T, D), lambda i: (i, 0))
    vec = pl.BlockSpec((1, D), lambda i: (0, 0))
    one = pl.BlockSpec((1, 1), lambda i: (0, 0))
    return pl.pallas_call(
        body, name=name, grid=(L // T,),
        in_specs=[row, vec, row], out_specs=[one, row, vec],
        out_shape=[jax.ShapeDtypeStruct((1, 1), F32), jax.ShapeDtypeStruct((L, D), F32),
                   jax.ShapeDtypeStruct((1, D), F32)],
        compiler_params=_params(("arbitrary",)),
    )(h, g, target)


def _swiglu_fwd(g, u, name):
    L, F = g.shape
    T = _pick(L, (192, 128))

    def body(g_ref, u_ref, o_ref):
        gv = g_ref[...]
        o_ref[...] = (gv * _sigmoid(gv) * u_ref[...]).astype(o_ref.dtype)

    row = pl.BlockSpec((T, F), lambda i: (i, 0))
    return pl.pallas_call(
        body, name=name, grid=(L // T,), in_specs=[row, row], out_specs=row,
        out_shape=jax.ShapeDtypeStruct((L, F), BF16), compiler_params=_params(("parallel",)),
    )(g, u)


def _swiglu_bwd(g, u, dact, name):
    L, F = g.shape
    T = _pick(L, (192, 128))

    def body(g_ref, u_ref, d_ref, dg_ref, du_ref):
        gv = g_ref[...]
        dv = d_ref[...]
        s = _sigmoid(gv)
        du_ref[...] = (dv * (gv * s)).astype(du_ref.dtype)
        dg_ref[...] = (dv * u_ref[...] * (s * (1.0 + gv * (1.0 - s)))).astype(dg_ref.dtype)

    row = pl.BlockSpec((T, F), lambda i: (i, 0))
    return pl.pallas_call(
        body, name=name, grid=(L // T,), in_specs=[row, row, row], out_specs=[row, row],
        out_shape=[jax.ShapeDtypeStruct((L, F), BF16), jax.ShapeDtypeStruct((L, F), BF16)],
        compiler_params=_params(("parallel",)),
    )(g, u, dact)


def _mm_nn(pairs, out_dtype, name, residual=None, cols=None):
    M = pairs[0][0].shape[0]
    col0, N = cols if cols is not None else (0, pairs[0][1].shape[-1])
    tm = _pick(M, (1056, 384, 128))
    tn = _pick(math.gcd(N, col0) if col0 else N, (640, 512, 256, 128))
    jb = col0 // tn
    n = len(pairs)

    def body(*refs):
        a_refs, w_refs = refs[:n], refs[n:2 * n]
        o_ref = refs[-1]
        acc = None
        for a_ref, w_ref in zip(a_refs, w_refs):
            d = jnp.dot(a_ref[...].astype(BF16), w_ref[...], preferred_element_type=F32)
            acc = d if acc is None else acc + d
        if residual is not None:
            acc = acc + refs[2 * n][...]
        o_ref[...] = acc.astype(o_ref.dtype)

    in_specs = [pl.BlockSpec((tm, a.shape[1]), lambda i, j: (i, 0)) for a, _, _, _ in pairs]
    for a, _, layer, kblk in pairs:
        in_specs.append(pl.BlockSpec((None, a.shape[1], tn), functools.partial(lambda i, j, l, kb: (l, kb, j + jb), l=layer, kb=kblk)))
    args = [p[0] for p in pairs] + [p[1] for p in pairs]
    if residual is not None:
        in_specs.append(pl.BlockSpec((tm, tn), lambda i, j: (i, j)))
        args.append(residual)
    return pl.pallas_call(
        body, name=name, grid=(M // tm, N // tn), in_specs=in_specs,
        out_specs=pl.BlockSpec((tm, tn), lambda i, j: (i, j)),
        out_shape=jax.ShapeDtypeStruct((M, N), out_dtype),
        compiler_params=_params(("parallel", "parallel")),
    )(*args)


def _mm_nt(pairs, out_dtype, name):
    M = pairs[0][0].shape[0]
    K = pairs[0][1].shape[1]
    tm = _pick(M, (1056, 384, 128))
    tk = _pick(K, (512, 1408, 256, 128))
    n = len(pairs)

    def body(*refs):
        d_refs, w_refs = refs[:n], refs[n:2 * n]
        o_ref = refs[-1]
        acc = None
        for d_ref, w_ref in zip(d_refs, w_refs):
            d = lax.dot_general(d_ref[...].astype(BF16), w_ref[...], (((1,), (1,)), ((), ())),
                                preferred_element_type=F32)
            acc = d if acc is None else acc + d
        o_ref[...] = acc.astype(o_ref.dtype)

    in_specs = [pl.BlockSpec((tm, d.shape[1]), lambda i, j: (i, 0)) for d, _, _, _ in pairs]
    for d, _, layer, cblk in pairs:
        in_specs.append(pl.BlockSpec((None, tk, d.shape[1]), functools.partial(lambda i, j, l, cb: (l, j, cb), l=layer, cb=cblk)))
    args = [p[0] for p in pairs] + [p[1] for p in pairs]
    return pl.pallas_call(
        body, name=name, grid=(M // tm, K // tk), in_specs=in_specs,
        out_specs=pl.BlockSpec((tm, tk), lambda i, j: (i, j)),
        out_shape=jax.ShapeDtypeStruct((M, K), out_dtype),
        compiler_params=_params(("parallel", "parallel")),
    )(*args)


def _mm_tn(a, b, name, col_sharded):
    M, K = a.shape
    N = b.shape[1]
    tm = _pick(M, (384, 128))
    tk = _pick(K, (512, 1408, 256, 128))
    tn = N // N_CHIPS if col_sharded else _pick(N, (512, 128))

    def body(a_ref, b_ref, o_ref):
        @pl.when(pl.program_id(2) == 0)
        def _():
            o_ref[...] = jnp.zeros_like(o_ref)

        o_ref[...] += lax.dot_general(a_ref[...].astype(BF16), b_ref[...].astype(BF16),
                                      (((0,), (0,)), ((), ())), preferred_element_type=F32)

    if col_sharded:
        out_shape = jax.ShapeDtypeStruct((N_CHIPS, K, tn), F32)
        out_spec = pl.BlockSpec((None, tk, tn), lambda k, j, m: (j, k, 0))
    else:
        out_shape = jax.ShapeDtypeStruct((K, N), F32)
        out_spec = pl.BlockSpec((tk, tn), lambda k, j, m: (k, j))
    out = pl.pallas_call(
        body, name=name, grid=(K // tk, N // tn, M // tm),
        in_specs=[pl.BlockSpec((tm, tk), lambda k, j, m: (m, k)), pl.BlockSpec((tm, tn), lambda k, j, m: (m, j))],
        out_specs=out_spec, out_shape=out_shape,
        compiler_params=_params(("parallel", "parallel", "arbitrary")),
    )(a, b)
    return out if col_sharded else out.reshape(N_CHIPS, K // N_CHIPS, N)


def _stack_heads(x, scale=None):
    lane = lax.broadcasted_iota(jnp.int32, x.shape, 1)
    zero = jnp.zeros_like(x)
    lo = jnp.where(lane < HEAD_DIM, x, zero)
    hi = jnp.where(lane < HEAD_DIM, zero, x)
    out = jnp.concatenate([lo, hi], axis=0)
    return out if scale is None else out * scale


def _unstack_heads(x2):
    qb = x2.shape[0] // 2
    lane = lax.broadcasted_iota(jnp.int32, (qb, LANES), 1)
    return jnp.where(lane < HEAD_DIM, x2[:qb], x2[qb:])


def _tri_and_ones(strict):
    r = lax.broadcasted_iota(jnp.int32, (QUERY_BLOCK, 2 * QUERY_BLOCK), 0)
    c = lax.broadcasted_iota(jnp.int32, (QUERY_BLOCK, 2 * QUERY_BLOCK), 1)
    tri = (r > c) if strict else (r >= c)
    return jnp.where(tri | (c >= QUERY_BLOCK), 1.0, 0.0).astype(BF16)


def _dot2(x, m):
    xh = x.astype(BF16)
    xl = (x - xh.astype(F32)).astype(BF16)
    return jnp.dot(xh, m, preferred_element_type=F32) + jnp.dot(xl, m, preferred_element_type=F32)


def _causal_valid():
    r = lax.broadcasted_iota(jnp.int32, (2 * QUERY_BLOCK, QUERY_BLOCK), 0) & (QUERY_BLOCK - 1)
    c = lax.broadcasted_iota(jnp.int32, (2 * QUERY_BLOCK, QUERY_BLOCK), 1)
    return c < r


def _sweep_older(i, step, carry_ref):
    def cond(state):
        n, live = state
        return jnp.logical_and(n < i, live)

    def older(state):
        n, _ = state
        step(i - 1 - n, False)
        return n + 1, jnp.max(carry_ref[...]) > LOG_STICK_FLOOR

    lax.while_loop(cond, older, (0, jnp.max(carry_ref[...]) > LOG_STICK_FLOOR))


def _attn_fwd(qkv, sb_width, name):
    L = qkv.shape[0]
    QB = QUERY_BLOCK
    nb = L // QB
    n_pairs = sb_width // LANES
    scale = 1.0 / math.sqrt(HEAD_DIM)

    def body(q_ref, k_ref, v_ref, o_ref, acc_ref, carry_ref):
        i = pl.program_id(1)
        q2 = _stack_heads(q_ref[...], scale)
        txo = _tri_and_ones(True)
        valid = _causal_valid()
        acc_ref[...] = jnp.zeros_like(acc_ref)
        carry_ref[...] = jnp.zeros_like(carry_ref)

        def step(j, diag):
            start = pl.multiple_of(j * QB, QB)
            kb = k_ref[pl.ds(start, QB), :]
            vb = v_ref[pl.ds(start, QB), :]
            z = lax.dot_general(q2, kb, (((1,), (1,)), ((), ())), preferred_element_type=F32)
            sp = jnp.log(1.0 + jnp.exp(-jnp.abs(z)))
            a = jnp.minimum(z, 0.0) - sp
            b = jnp.minimum(-z, 0.0) - sp
            if diag:
                b = jnp.where(valid, b, 0.0)
            res = _dot2(b, txo)
            w = jnp.exp(a + res[:, :QB] + carry_ref[...])
            if diag:
                w = jnp.where(valid, w, 0.0)
            carry_ref[...] += res[:, QB:]
            acc_ref[...] += _dot2(w, vb)

        step(i, True)
        _sweep_older(i, step, carry_ref)
        o_ref[...] = _unstack_heads(acc_ref[...])

    return pl.pallas_call(
        body, name=name, grid=(n_pairs, nb),
        in_specs=[pl.BlockSpec((QB, LANES), lambda p, i: (i, p)),
                  pl.BlockSpec((L, LANES), lambda p, i: (0, n_pairs + p)),
                  pl.BlockSpec((L, LANES), lambda p, i: (0, 2 * n_pairs + p))],
        out_specs=pl.BlockSpec((QB, LANES), lambda p, i: (i, p)),
        out_shape=jax.ShapeDtypeStruct((L, sb_width), F32),
        scratch_shapes=[pltpu.VMEM((2 * QB, LANES), F32), pltpu.VMEM((2 * QB, LANES), F32)],
        compiler_params=_params(("parallel", "arbitrary")),
    )(qkv, qkv, qkv)


def _attn_bwd(qkv, o, dmixed, sb_width, name):
    L = qkv.shape[0]
    QB = QUERY_BLOCK
    nb = L // QB
    n_pairs = sb_width // LANES
    scale = 1.0 / math.sqrt(HEAD_DIM)

    def body(q_ref, k_ref, v_ref, o_ref, do_ref, dq_ref, dk_ref, dv_ref,
             dq_acc, dk_acc, dv_acc, ce_ref, cr_ref):
        i = pl.program_id(1)

        @pl.when(i == 0)
        def _():
            dk_acc[...] = jnp.zeros_like(dk_acc)
            dv_acc[...] = jnp.zeros_like(dv_acc)

        q2 = _stack_heads(q_ref[...], scale)
        do2 = _stack_heads(do_ref[...].astype(BF16))
        ov = o_ref[...]
        o2 = jnp.concatenate([ov, ov], axis=0)
        ones = jnp.ones((LANES, LANES), BF16)
        dtot = _dot2(do2.astype(F32) * o2, ones)
        txo = _tri_and_ones(True)
        tio = _tri_and_ones(False)
        valid = _causal_valid()
        dq_acc[...] = jnp.zeros_like(dq_acc)
        ce_ref[...] = jnp.zeros_like(ce_ref)
        cr_ref[...] = jnp.zeros_like(cr_ref)

        def step(j, diag):
            start = pl.multiple_of(j * QB, QB)
            kb = k_ref[pl.ds(start, QB), :]
            vb = v_ref[pl.ds(start, QB), :]
            z = lax.dot_general(q2, kb, (((1,), (1,)), ((), ())), preferred_element_type=F32)
            e = jnp.exp(-jnp.abs(z))
            sp = jnp.log(1.0 + e)
            a = jnp.minimum(z, 0.0) - sp
            b = jnp.minimum(-z, 0.0) - sp
            rinv = 1.0 / (1.0 + e)
            pos = z >= 0.0
            beta = jnp.where(pos, rinv, e * rinv)
            one_m_beta = jnp.where(pos, e * rinv, rinv)
            if diag:
                b = jnp.where(valid, b, 0.0)
            res = _dot2(b, txo)
            w = jnp.exp(a + res[:, :QB] + ce_ref[...])
            if diag:
                w = jnp.where(valid, w, 0.0)
            ce_ref[...] += res[:, QB:]
            dw = lax.dot_general(do2, vb, (((1,), (1,)), ((), ())), preferred_element_type=F32)
            g = w * dw
            res2 = _dot2(g, tio)
            rinc = res2[:, :QB] + cr_ref[...]
            cr_ref[...] += res2[:, QB:]
            dz = g * one_m_beta - beta * (dtot - rinc)
            if diag:
                dz = jnp.where(valid, dz, 0.0)
            dzb = dz.astype(BF16)
            dq_acc[...] += jnp.dot(dzb, kb, preferred_element_type=F32)
            dk_acc[pl.ds(start, QB), :] += lax.dot_general(
                dzb, q2, (((0,), (0,)), ((), ())), preferred_element_type=F32)
            dv_acc[pl.ds(start, QB), :] += lax.dot_general(
                w.astype(BF16), do2, (((0,), (0,)), ((), ())), preferred_element_type=F32)

        step(i, True)
        _sweep_older(i, step, ce_ref)
        dq_ref[...] =(_unstack_heads(dq_acc[...]) * scale).astype(dq_ref.dtype)

        @pl.when(i == nb - 1)
        def _():
            dk_ref[...] = dk_acc[...].astype(dk_ref.dtype)
            dv_ref[...] = dv_acc[...].astype(dv_ref.dtype)

    blk = pl.BlockSpec((QB, LANES), lambda p, i: (i, p))
    col = pl.BlockSpec((L, LANES), lambda p, i: (0, p))
    return pl.pallas_call(
        body, name=name, grid=(n_pairs, nb),
        in_specs=[blk,
                  pl.BlockSpec((L, LANES), lambda p, i: (0, n_pairs + p)),
                  pl.BlockSpec((L, LANES), lambda p, i: (0, 2 * n_pairs + p)),
                  blk, blk],
        out_specs=[blk, col, col],
        out_shape=[jax.ShapeDtypeStruct((L, sb_width), BF16)] * 3,
        scratch_shapes=[pltpu.VMEM((2 * QB, LANES), F32), pltpu.VMEM((L, LANES), F32),
                        pltpu.VMEM((L, LANES), F32), pltpu.VMEM((2 * QB, LANES), F32),
                        pltpu.VMEM((2 * QB, LANES), F32)],
        compiler_params=_params(("parallel", "arbitrary")),
    )(qkv, qkv, qkv, o, dmixed)


def _conv_tile(L):
    return _pick(L, (384, 128))


def _glu(x, C):
    return x[:, :C] * _sigmoid(x[:, C:])


def _conv_fwd(cacg, dw_w, dw_b, ln_g, ln_b, name):
    L, C2 = cacg.shape
    C = C2 // 2
    T = _conv_tile(L)
    H = CONV_HALO
    K = dw_w.shape[0]
    CH = 64 if T % 64 == 0 else T

    def body(x_ref, prev_ref, w_ref, b_ref, g_ref, beta_ref, o_ref, u_ref):
        i = pl.program_id(0)
        u_ref[0:H, :] = jnp.where(i > 0, _glu(prev_ref[...], C), 0.0)
        u_ref[H:, :] = _glu(x_ref[...], C)
        for c0 in range(0, T, CH):
            y = jnp.broadcast_to(b_ref[...], (CH, C))
            for k in range(K):
                y = y + w_ref[k:k + 1, :] * u_ref[c0 + H - (K - 1) + k:c0 + H - (K - 1) + k + CH, :]
            mu = jnp.mean(y, axis=-1, keepdims=True)
            yc = y - mu
            rstd = lax.rsqrt(jnp.mean(yc * yc, axis=-1, keepdims=True) + EPS)
            ln = yc * rstd * g_ref[...] + beta_ref[...]
            o_ref[c0:c0 + CH, :] = (ln * _sigmoid(ln)).astype(o_ref.dtype)

    vec = pl.BlockSpec((1, C), lambda i: (0, 0))
    return pl.pallas_call(
        body, name=name, grid=(L // T,),
        in_specs=[pl.BlockSpec((T, C2), lambda i: (i, 0)),
                  pl.BlockSpec((H, C2), lambda i: (jnp.maximum(i * (T // H) - 1, 0), 0)),
                  pl.BlockSpec((K, C), lambda i: (0, 0)), vec, vec, vec],
        out_specs=pl.BlockSpec((T, C), lambda i: (i, 0)),
        out_shape=jax.ShapeDtypeStruct((L, C), BF16),
        scratch_shapes=[pltpu.VMEM((T + H, C), F32)],
        compiler_params=_params(("parallel",)),
    )(cacg, cacg, dw_w, dw_b, ln_g, ln_b)


def _conv_bwd(cacg, dmixed, dw_w, dw_b, ln_g, ln_b, name):
    L, C2 = cacg.shape
    C = C2 // 2
    T = _conv_tile(L)
    H = CONV_HALO
    K = dw_w.shape[0]
    nt = L // T
    TE = T + H

    def body(x_ref, prev_ref, next_ref, d_ref, dnext_ref, w_ref, b_ref, g_ref, beta_ref,
             dca_ref, dcg_ref, dwt_ref, db_ref, dg_ref, dbeta_ref, u_ref, dy_ref):
        i = pl.program_id(0)
        last = i == nt - 1
        u_ref[0:H, :] = jnp.where(i > 0, _glu(prev_ref[...], C), 0.0)
        u_ref[H:H + T, :] = _glu(x_ref[...], C)
        u_ref[H + T:, :] = _glu(next_ref[...], C)
        y = jnp.broadcast_to(b_ref[...], (TE, C))
        for k in range(K):
            y = y + w_ref[k:k + 1, :] * u_ref[H - (K - 1) + k:H - (K - 1) + k + TE, :]
        mu = jnp.mean(y, axis=-1, keepdims=True)
        yc = y - mu
        rstd = lax.rsqrt(jnp.mean(yc * yc, axis=-1, keepdims=True) + EPS)
        yh = yc * rstd
        ln = yh * g_ref[...] + beta_ref[...]
        s = _sigmoid(ln)
        dout = jnp.concatenate([d_ref[...], jnp.where(last, 0.0, dnext_ref[...])], axis=0)
        dln = dout * (s * (1.0 + ln * (1.0 - s)))
        dyh = dln * g_ref[...]
        dy = rstd * (dyh - jnp.mean(dyh, axis=-1, keepdims=True)
                     - yh * jnp.mean(dyh * yh, axis=-1, keepdims=True))
        dy_ref[...] = dy

        @pl.when(i == 0)
        def _():
            dwt_ref[...] = jnp.zeros_like(dwt_ref)
            db_ref[...] = jnp.zeros_like(db_ref)
            dg_ref[...] = jnp.zeros_like(dg_ref)
            dbeta_ref[...] = jnp.zeros_like(dbeta_ref)

        dg_ref[...] += jnp.sum((dln * yh)[:T], axis=0, keepdims=True)
        dbeta_ref[...] += jnp.sum(dln[:T], axis=0, keepdims=True)
        dy_t = dy[:T]
        db_ref[...] += jnp.sum(dy_t, axis=0, keepdims=True)
        du = jnp.zeros((T, C), F32)
        for k in range(K):
            off = H - (K - 1) + k
            dwt_ref[k:k + 1, :] += jnp.sum(dy_t * u_ref[off:off + T, :], axis=0, keepdims=True)
            du = du + w_ref[k:k + 1, :] * dy_ref[(K - 1) - k:(K - 1) - k + T, :]
        x = x_ref[...]
        sg = _sigmoid(x[:, C:])
        dca_ref[...] = (du * sg).astype(dca_ref.dtype)
        dcg_ref[...] = (du * x[:, :C] * sg * (1.0 - sg)).astype(dcg_ref.dtype)

    nh = L // H
    vec = pl.BlockSpec((1, C), lambda i: (0, 0))
    row = pl.BlockSpec((T, C), lambda i: (i, 0))
    return pl.pallas_call(
        body, name=name, grid=(nt,),
        in_specs=[pl.BlockSpec((T, C2), lambda i: (i, 0)),
                  pl.BlockSpec((H, C2), lambda i: (jnp.maximum(i * (T // H) - 1, 0), 0)),
                  pl.BlockSpec((H, C2), lambda i: (jnp.minimum((i + 1) * (T // H), nh - 1), 0)),
                  pl.BlockSpec((T, C), lambda i: (i, 1)),
                  pl.BlockSpec((H, C), lambda i: (jnp.minimum((i + 1) * (T // H), nh - 1), 1)),
                  pl.BlockSpec((K, C), lambda i: (0, 0)), vec, vec, vec],
        out_specs=[row, row, pl.BlockSpec((H, C), lambda i: (0, 0)), vec, vec, vec],
        out_shape=[jax.ShapeDtypeStruct((L, C), BF16), jax.ShapeDtypeStruct((L, C), BF16),
                   jax.ShapeDtypeStruct((H, C), F32), jax.ShapeDtypeStruct((1, C), F32),
                   jax.ShapeDtypeStruct((1, C), F32), jax.ShapeDtypeStruct((1, C), F32)],
        scratch_shapes=[pltpu.VMEM((T + 2 * H, C), F32), pltpu.VMEM((TE, C), F32)],
        compiler_params=_params(("arbitrary",)),
    )(cacg, cacg, cacg, dmixed, dmixed, dw_w, dw_b, ln_g, ln_b)


def _local_step(h0, target, n_meta, seq, norms, conv_p, wts, final_g):
    mix_g, ffn_g = norms
    dw_w, dw_b, ln_g, ln_b = conv_p
    depth = mix_g.shape[0]
    C = dw_b.shape[-1]
    sbw = (wts["w_in"].shape[-1] - 2 * C) // 3
    assert sbw == C, "the mixer halves must have equal width"
    row = lambda a, i: a[i][None, :]

    h = h0
    saved = []
    for i in range(depth):
        hn = _rmsnorm_fwd(h, row(mix_g, i), f"mix_norm_{i}")
        proj_qkv = _mm_nn([(hn, wts["w_in"], i, 0)], BF16, f"in_qkv_{i}", cols=(0, 3 * sbw))
        cacg = _mm_nn([(hn, wts["w_in"], i, 0)], F32, f"in_conv_{i}", cols=(3 * sbw, 2 * C))
        attn = _attn_fwd(proj_qkv, sbw, f"attn_fwd_{i}")
        conv = _conv_fwd(cacg, dw_w[i], row(dw_b, i), row(ln_g, i), row(ln_b, i), f"conv_fwd_{i}")
        h_mid = _mm_nn([(attn, wts["w_out"], i, 0), (conv, wts["w_out"], i, 1)], F32, f"out_proj_{i}", residual=h)
        hn2 = _rmsnorm_fwd(h_mid, row(ffn_g, i), f"ffn_norm_{i}")
        g = _mm_nt([(hn2, wts["w_gate_t"], i, 0)], F32, f"gate_{i}")
        u = _mm_nt([(hn2, wts["w_up_t"], i, 0)], F32, f"up_{i}")
        act = _swiglu_fwd(g, u, f"swiglu_{i}")
        h_out = _mm_nn([(act, wts["w_down"], i, 0)], F32, f"down_{i}", residual=h_mid)
        saved.append((h, hn, proj_qkv, cacg, attn, conv, h_mid, hn2, g, u, act))
        h = h_out

    loss, dh, d_final_g = _loss_head(h, final_g[None, :], target, n_meta, seq, "loss_head")

    grads = {k: [None] * depth for k in ("w_in", "w_out", "w_gate_t", "w_up_t", "w_down", "mix_g", "ffn_g",
                                         "dw_w", "dw_b", "ln_g", "ln_b")}
    for i in reversed(range(depth)):
        h_in, hn, proj_qkv, cacg, attn, conv, h_mid, hn2, g, u, act = saved[i]
        dact = _mm_nt([(dh, wts["w_down"], i, 0)], F32, f"d_act_{i}")
        grads["w_down"][i] = _mm_tn(act, dh, f"dw_down_{i}", col_sharded=False)
        dg, du = _swiglu_bwd(g, u, dact, f"swiglu_bwd_{i}")
        dhn2 = _mm_nn([(dg, wts["w_gate_t"], i, 0), (du, wts["w_up_t"], i, 0)], F32, f"d_hn2_{i}")
        grads["w_gate_t"][i] = _mm_tn(dg, hn2, f"dw_gate_{i}", col_sharded=False)
        grads["w_up_t"][i] = _mm_tn(du, hn2, f"dw_up_{i}", col_sharded=False)
        dh, grads["ffn_g"][i] = _rmsnorm_bwd(h_mid, row(ffn_g, i), dhn2, dh, f"ffn_norm_bwd_{i}")
        dmixed = _mm_nt([(dh, wts["w_out"], i, 0)], F32, f"d_mixed_{i}")
        mixed = jnp.concatenate([attn.astype(BF16), conv], axis=1)
        grads["w_out"][i] = _mm_tn(mixed, dh, f"dw_out_{i}", col_sharded=False)
        dq, dk, dv = _attn_bwd(proj_qkv, attn, dmixed, sbw, f"attn_bwd_{i}")
        dca, dcg, d_dw, d_b, d_lg, d_lb = _conv_bwd(
            cacg, dmixed, dw_w[i], row(dw_b, i), row(ln_g, i), row(ln_b, i), f"conv_bwd_{i}")
        grads["dw_w"][i] = d_dw[:dw_w.shape[1]]
        grads["dw_b"][i], grads["ln_g"][i], grads["ln_b"][i] = d_b[0], d_lg[0], d_lb[0]
        dproj = jnp.concatenate([dq, dk, dv, dca, dcg], axis=1)
        dhn = _mm_nt([(dproj, wts["w_in"], i, 0)], F32, f"d_hn_{i}")
        grads["w_in"][i] = _mm_tn(hn, dproj, f"dw_in_{i}", col_sharded=True)
        dh, d_mix = _rmsnorm_bwd(h_in, row(mix_g, i), dhn, dh, f"mix_norm_bwd_{i}")
        grads["mix_g"][i] = d_mix[0]
        grads["ffn_g"][i] = grads["ffn_g"][i][0]
    grads["final_g"] = d_final_g[0]
    return loss, dh, grads


ANY = pl.BlockSpec(memory_space=pl.ANY)


def _position():
    return lax.axis_index("x"), lax.axis_index("y"), lax.axis_index("c")


def _chip_at(x, y, k):
    return (1 - x if k & 2 else x), (1 - y if k & 1 else y)


def _half_rows(ref, half, rows, base=0):
    start = pl.multiple_of(base + half * rows, 8)
    lead = (slice(None),) * (len(ref.shape) - 2)
    return ref.at[(*lead, pl.ds(start, rows), slice(None))]


def _gather_weights(fulls, shard_shapes, col_sharded):
    n = len(fulls)

    def body(*refs):
        f_refs = refs[n:2 * n]
        send_sems, recv_sems = refs[2 * n:]
        x, y, c = _position()
        me = 2 * x + y

        def block(wi, chip, half):
            _, R, C = shard_shapes[wi]
            if col_sharded[wi]:
                cols = pl.ds(pl.multiple_of(chip * C, LANES), C)
                return f_refs[wi].at[:, pl.ds(pl.multiple_of(half * (R // 2), 8), R // 2), cols]
            return _half_rows(f_refs[wi], half, R // 2, base=chip * R)

        def copy(wi, slot, blk, to):
            return pltpu.make_async_remote_copy(
                src_ref=blk, dst_ref=blk, send_sem=send_sems.at[6 * wi + slot],
                recv_sem=recv_sems.at[6 * wi + slot], device_id=to, device_id_type=MESH)

        sent = []
        for wi in range(n):
            for k in (1, 2, 3):
                cp = copy(wi, k - 1, block(wi, me, c), (*_chip_at(x, y, k), c))
                cp.start()
                sent.append(cp)
        for wi in range(n):
            for k in (1, 2, 3):
                landed = block(wi, me ^ k, c)
                copy(wi, k - 1, landed, (x, y, c)).wait_recv()
                cp = copy(wi, 2 + k, landed, (x, y, 1 - c))
                cp.start()
                sent.append(cp)
        for wi in range(n):
            for k in (1, 2, 3):
                copy(wi, 2 + k, block(wi, me ^ k, 1 - c), (x, y, c)).wait_recv()
        for cp in sent:
            cp.wait_send()

    return pl.pallas_call(
        body, name="gather_weights", out_shape=[jax.ShapeDtypeStruct(f.shape, f.dtype) for f in fulls],
        in_specs=[ANY] * n, out_specs=[ANY] * n, input_output_aliases={i: i for i in range(n)},
        scratch_shapes=[pltpu.SemaphoreType.DMA((6 * n,)), pltpu.SemaphoreType.DMA((6 * n,))],
    )(*fulls)


def _place_shard(w, chip, col_sharded, dtype, name):
    lyr, R, C = w.shape
    tr = _pick(R, (256, 352, 128, 48))
    nr = R // tr

    def body(chip_ref, w_ref, o_ref):
        o_ref[...] = w_ref[...].astype(dtype)

    if col_sharded:
        shape = (lyr, R, N_CHIPS * C)
        out_spec = pl.BlockSpec((None, tr, C), lambda l, r, chip_ref: (l, r, chip_ref[0]))
    else:
        shape = (lyr, N_CHIPS * R, C)
        out_spec = pl.BlockSpec((None, tr, C), lambda l, r, chip_ref: (l, chip_ref[0] * nr + r, 0))
    grid_spec = pltpu.PrefetchScalarGridSpec(
        num_scalar_prefetch=1, grid=(lyr, nr),
        in_specs=[pl.BlockSpec((None, tr, C), lambda l, r, chip_ref: (l, r, 0))], out_specs=out_spec)
    return pl.pallas_call(
        body, name=name, grid_spec=grid_spec, out_shape=jax.ShapeDtypeStruct(shape, dtype),
        compiler_params=_params(("parallel", "parallel")),
    )(chip, w)


def _rs_to_sibling(grads):
    n = len(grads)
    outs = [jax.ShapeDtypeStruct((g.shape[0], g.shape[1] // 2, g.shape[2]), g.dtype) for g in grads]

    def body(*refs):
        g_refs, l_refs = refs[:n], refs[n:2 * n]
        send_sems, recv_sems = refs[2 * n:]
        x, y, c = _position()
        cps = []
        for wi in range(n):
            R = grads[wi].shape[1]
            cp = pltpu.make_async_remote_copy(
                src_ref=_half_rows(g_refs[wi], 1 - c, R // 2), dst_ref=l_refs[wi],
                send_sem=send_sems.at[wi], recv_sem=recv_sems.at[wi],
                device_id=(x, y, 1 - c), device_id_type=MESH)
            cp.start()
            cps.append(cp)
        for cp in cps:
            cp.wait()

    return pl.pallas_call(
        body, name="grads_to_sibling", out_shape=outs, in_specs=[ANY] * n, out_specs=[ANY] * n,
        scratch_shapes=[pltpu.SemaphoreType.DMA((n,)), pltpu.SemaphoreType.DMA((n,))],
    )(*grads)


def _chip_sum(g, landed, core, name):
    _, R, C = g.shape
    hr = R // 2
    tr = _pick(hr, (256, 352, 128))
    nr = hr // tr

    def body(c_ref, g_ref, l_ref, o_ref):
        o_ref[...] = (g_ref[...] + l_ref[...]).astype(BF16)

    grid_spec = pltpu.PrefetchScalarGridSpec(
        num_scalar_prefetch=1, grid=(N_CHIPS, nr),
        in_specs=[pl.BlockSpec((None, tr, C), lambda j, r, c_ref: (j, c_ref[0] * nr + r, 0)),
                  pl.BlockSpec((None, tr, C), lambda j, r, c_ref: (j, r, 0))],
        out_specs=pl.BlockSpec((None, tr, C), lambda j, r, c_ref: (j, r, 0)))
    return pl.pallas_call(
        body, name=name, grid_spec=grid_spec, out_shape=jax.ShapeDtypeStruct((N_CHIPS, hr, C), BF16),
        compiler_params=_params(("parallel", "parallel")),
    )(core, g, landed)


def _rs_across_chips(parts):
    n = len(parts)
    outs = [jax.ShapeDtypeStruct(p.shape, p.dtype) for p in parts]

    def body(*refs):
        p_refs, l_refs = refs[:n], refs[n:2 * n]
        send_sems, recv_sems = refs[2 * n:]
        x, y, c = _position()
        me = 2 * x + y
        sent = []
        for wi in range(n):
            for k in (1, 2, 3):
                cp = pltpu.make_async_remote_copy(
                    src_ref=p_refs[wi].at[me ^ k], dst_ref=l_refs[wi].at[me],
                    send_sem=send_sems.at[3 * wi + k - 1], recv_sem=recv_sems.at[3 * wi + k - 1],
                    device_id=(*_chip_at(x, y, k), c), device_id_type=MESH)
                cp.start()
                sent.append(cp)
        for wi in range(n):
            for k in (1, 2, 3):
                slot = l_refs[wi].at[me ^ k]
                pltpu.make_async_remote_copy(
                    src_ref=slot, dst_ref=slot, send_sem=send_sems.at[3 * wi + k - 1],
                    recv_sem=recv_sems.at[3 * wi + k - 1], device_id=(x, y, c), device_id_type=MESH).wait_recv()
        for cp in sent:
            cp.wait_send()

    return pl.pallas_call(
        body, name="grads_across_chips", out_shape=outs, in_specs=[ANY] * n, out_specs=[ANY] * n,
        scratch_shapes=[pltpu.SemaphoreType.DMA((3 * n,)), pltpu.SemaphoreType.DMA((3 * n,))],
    )(*parts)


def _sum_chips(parts, landed, where, layer, depth, prev, name):
    _, hr, C = landed.shape
    tr = _pick(hr, (256, 352, 128))
    nr = hr // tr

    def body(*refs):
        own_ref, slots, o_ref = refs[1], refs[2:2 + N_CHIPS], refs[-1]
        chip = refs[0][0]
        total = None
        for q in range(N_CHIPS):
            term = jnp.where(chip == q, own_ref[...], slots[q][...]).astype(F32)
            total = term if total is None else total + term
        o_ref[...] = total

    def slot_spec(q):
        return pl.BlockSpec((None, tr, C), lambda r, w: (jnp.where(w[0] == q, (q + 1) % N_CHIPS, q), r, 0))

    in_specs = [pl.BlockSpec((None, tr, C), lambda r, w: (w[0], r, 0))] + [slot_spec(q) for q in range(N_CHIPS)]
    args = [where, parts] + [landed] * N_CHIPS
    aliases = {}
    if prev is not None:
        in_specs.append(ANY)
        args.append(prev)
        aliases = {len(args) - 1: 0}
    grid_spec = pltpu.PrefetchScalarGridSpec(
        num_scalar_prefetch=1, grid=(nr,), in_specs=in_specs,
        out_specs=pl.BlockSpec((None, tr, C), lambda r, w: (layer, w[1] * nr + r, 0)))
    return pl.pallas_call(
        body, name=name, grid_spec=grid_spec, out_shape=jax.ShapeDtypeStruct((depth, 2 * hr, C), F32),
        input_output_aliases=aliases, compiler_params=_params(("parallel",)),
    )(*args)


def _rs_join_halves(reduced):
    n = len(reduced)

    def body(*refs):
        o_refs = refs[n:2 * n]
        send_sems, recv_sems = refs[2 * n:]
        x, y, c = _position()
        sent = []
        for wi in range(n):
            hr = reduced[wi].shape[1] // 2
            mine = _half_rows(o_refs[wi], c, hr)
            cp = pltpu.make_async_remote_copy(
                src_ref=mine, dst_ref=mine, send_sem=send_sems.at[wi], recv_sem=recv_sems.at[wi],
                device_id=(x, y, 1 - c), device_id_type=MESH)
            cp.start()
            sent.append(cp)
        for wi in range(n):
            hr = reduced[wi].shape[1] // 2
            theirs = _half_rows(o_refs[wi], 1 - c, hr)
            pltpu.make_async_remote_copy(
                src_ref=theirs, dst_ref=theirs, send_sem=send_sems.at[wi], recv_sem=recv_sems.at[wi],
                device_id=(x, y, c), device_id_type=MESH).wait_recv()
        for cp in sent:
            cp.wait_send()

    return pl.pallas_call(
        body, name="grads_join_halves", out_shape=[jax.ShapeDtypeStruct(r.shape, r.dtype) for r in reduced],
        in_specs=[ANY] * n, out_specs=[ANY] * n, input_output_aliases={i: i for i in range(n)},
        scratch_shapes=[pltpu.SemaphoreType.DMA((n,)), pltpu.SemaphoreType.DMA((n,))],
    )(*reduced)


def _small_allreduce(vec):
    rows = vec.shape[0]

    def body(v_ref, o_ref, land, send_sems, recv_sems):
        x, y, c = _position()
        me = 4 * x + 2 * y + c
        land[0] = v_ref[...]
        sent = []
        for k in range(1, N_DEV):
            to = (1 - x if k & 4 else x, 1 - y if k & 2 else y, 1 - c if k & 1 else c)
            cp = pltpu.make_async_remote_copy(
                src_ref=v_ref, dst_ref=land.at[k], send_sem=send_sems.at[k - 1], recv_sem=recv_sems.at[k - 1],
                device_id=to, device_id_type=MESH)
            cp.start()
            sent.append(cp)
        for cp in sent:
            cp.wait_recv()
        acc = land[me]
        for e in range(1, N_DEV):
            acc = acc + land[me ^ e]
        o_ref[...] = acc
        for cp in sent:
            cp.wait_send()

    vmem = pl.BlockSpec(memory_space=pltpu.VMEM)
    return pl.pallas_call(
        body, name="small_allreduce", out_shape=jax.ShapeDtypeStruct(vec.shape, F32),
        in_specs=[vmem], out_specs=vmem,
        scratch_shapes=[pltpu.VMEM((N_DEV, rows, LANES), F32), pltpu.SemaphoreType.DMA((N_DEV - 1,)),
                        pltpu.SemaphoreType.DMA((N_DEV - 1,))],
    )(vec)


def _adam_math(w, g, m, v):
    m = ADAM_B1 * m + (1.0 - ADAM_B1) * g
    v = ADAM_B2 * v + (1.0 - ADAM_B2) * jnp.square(g)
    m_hat = m / (1.0 - ADAM_B1 ** ADAM_STEP)
    v_hat = v / (1.0 - ADAM_B2 ** ADAM_STEP)
    delta = -ADAM_LR * (m_hat / (jnp.sqrt(v_hat) + ADAM_EPS) + ADAM_WD * w)
    return delta, m, v


def _adam(w, g, m, v, name):
    def body(w_ref, g_ref, m_ref, v_ref, d_ref, nm_ref, nv_ref):
        d_ref[...], nm_ref[...], nv_ref[...] = _adam_math(w_ref[...], g_ref[...], m_ref[...], v_ref[...])

    if w.ndim == 3:
        lyr, R, C = w.shape
        tr = _pick(R, (256, 352, 128))
        blk = pl.BlockSpec((None, tr, C), lambda l, r: (l, r, 0))
        grid, sem = (lyr, R // tr), ("parallel", "parallel")
    else:
        blk = pl.BlockSpec(w.shape, lambda: (0, 0))
        grid, sem = (), None
    return pl.pallas_call(
        body, name=name, grid=grid, in_specs=[blk] * 4, out_specs=[blk] * 3,
        out_shape=[jax.ShapeDtypeStruct(w.shape, F32)] * 3, compiler_params=_params(sem),
    )(w, g, m, v)


def _rows(a, pad_to=8):
    r = a.reshape(-1, LANES)
    extra = (-r.shape[0]) % pad_to
    return jnp.pad(r, ((0, extra), (0, 0))) if extra else r


def _pack(arrays):
    return jnp.concatenate([_rows(a) for a in arrays], axis=0)


def _unpack(slab, shapes):
    out, at = [], 0
    for shp in shapes:
        nrow = math.prod(shp) // LANES
        out.append(slab[at:at + nrow].reshape(shp))
        at += nrow + (-nrow) % 8
    return out


BIG = ("w_in", "w_out", "w_gate_t", "w_up_t", "w_down")
BIG_COL_SHARDED = (True, False, False, False, False)
TRANSPOSED = {"w_gate_t": "w_gate", "w_up_t": "w_up"}


def kernel(x, meta_tokens, mix_norm_g, w_in, conv_dw_w, conv_dw_b, conv_ln_g, conv_ln_b, w_out, ffn_norm_g, w_gate, w_up, w_down, final_norm_g, loss_target, m_meta_tokens, m_mix_norm_g, m_w_in, m_conv_dw_w, m_conv_dw_b, m_conv_ln_g, m_conv_ln_b, m_w_out, m_ffn_norm_g, m_w_gate, m_w_up, m_w_down, m_final_norm_g, v_meta_tokens, v_mix_norm_g, v_w_in, v_conv_dw_w, v_conv_dw_b, v_conv_ln_g, v_conv_ln_b, v_w_out, v_ffn_norm_g, v_w_gate, v_w_up, v_w_down, v_final_norm_g):
    n_meta, seq = meta_tokens.shape[0], x.shape[1]
    D = x.shape[2]
    depth, taps, c_shard = conv_dw_w.shape
    C = conv_dw_b.shape[-1]
    chip = (2 * lax.axis_index("x") + lax.axis_index("y")).astype(jnp.int32)
    core = lax.axis_index("c").astype(jnp.int32).reshape(1)
    chip1 = chip.reshape(1)
    where = jnp.concatenate([chip1, core])
    big_w = dict(w_in=w_in, w_out=w_out, w_gate=w_gate, w_up=w_up, w_down=w_down)
    big_m = dict(w_in=m_w_in, w_out=m_w_out, w_gate=m_w_gate, w_up=m_w_up, w_down=m_w_down)
    big_v = dict(w_in=v_w_in, w_out=v_w_out, w_gate=v_w_gate, w_up=v_w_up, w_down=v_w_down)

    small_shard = _pack([conv_dw_w, meta_tokens])[None]
    to_send = [jnp.swapaxes(big_w[TRANSPOSED[k]], 1, 2) if k in TRANSPOSED else big_w[k] for k in BIG]
    placed = [_place_shard(w, chip1, col, BF16, f"place_{k}") for k, w, col in zip(BIG, to_send, BIG_COL_SHARDED)]
    placed.append(_place_shard(small_shard, chip1, False, F32, "place_small"))
    gathered = _gather_weights(placed, [w.shape for w in to_send] + [small_shard.shape],
                               BIG_COL_SHARDED + (False,))
    wts = dict(zip(BIG, gathered[:-1]))
    rows_shard = small_shard.shape[1]
    dw_full, meta_full = [], []
    for j in range(N_CHIPS):
        dwj, mj = _unpack(gathered[-1][0, j * rows_shard:(j + 1) * rows_shard],
                          [conv_dw_w.shape, meta_tokens.shape])
        dw_full.append(dwj)
        meta_full.append(mj)
    dw_w_full = jnp.concatenate(dw_full, axis=2)
    meta = jnp.concatenate(meta_full, axis=1)

    L = n_meta + seq
    Lp = -(-L // QUERY_BLOCK) * QUERY_BLOCK
    h0 = jnp.concatenate([meta, x[0], jnp.zeros((Lp - L, D), F32)], axis=0)
    target = jnp.pad(loss_target[0], ((n_meta, Lp - L), (0, 0)))
    loss, dh0, grads = _local_step(h0, target, n_meta, seq, (mix_norm_g, ffn_norm_g),
                                   (dw_w_full, conv_dw_b, conv_ln_g, conv_ln_b), wts, final_norm_g)
    loss = lax.psum(loss[0, 0], ("x", "y", "c"))
    grad_x = dh0[n_meta:L][None]

    flat = [grads[k][l] for k in BIG for l in range(depth)]
    landed = _rs_to_sibling(flat)
    parts = [_chip_sum(g, la, core, f"chip_sum_{i}") for i, (g, la) in enumerate(zip(flat, landed))]
    across = _rs_across_chips(parts)
    reduced = []
    for wi in range(len(BIG)):
        arr = None
        for l in range(depth):
            i = wi * depth + l
            arr = _sum_chips(parts[i], across[i], where, l, depth, arr, f"sum_chips_{i}")
        reduced.append(arr)
    big_g = dict(zip(BIG, _rs_join_halves(reduced)))

    small_names = ("mix_g", "ffn_g", "dw_b", "ln_g", "ln_b", "dw_w")
    small = [jnp.stack(grads[k]) for k in small_names] + [grads["final_g"], dh0[:n_meta]]
    small_shapes = [a.shape for a in small]
    g_mix, g_ffn, g_dwb, g_lng, g_lnb, g_dww, g_final, g_meta = _unpack(_small_allreduce(_pack(small)), small_shapes)
    g_dww = lax.dynamic_slice_in_dim(g_dww, chip * c_shard, c_shard, axis=2)
    g_meta = lax.dynamic_slice_in_dim(g_meta, chip * meta_tokens.shape[1], meta_tokens.shape[1], axis=1)

    out_g, out_d, out_m, out_v = {}, {}, {}, {}
    for kk in BIG:
        k = TRANSPOSED.get(kk, kk)
        out_g[k] = jnp.swapaxes(big_g[kk], 1, 2) if kk in TRANSPOSED else big_g[kk]
        out_d[k], out_m[k], out_v[k] = _adam(big_w[k], out_g[k], big_m[k], big_v[k], f"adam_{k}")
    small_order = ("meta_tokens", "mix_norm_g", "conv_dw_w", "conv_dw_b", "conv_ln_g", "conv_ln_b",
                   "ffn_norm_g", "final_norm_g")
    sw = dict(meta_tokens=meta_tokens, mix_norm_g=mix_norm_g, conv_dw_w=conv_dw_w, conv_dw_b=conv_dw_b,
              conv_ln_g=conv_ln_g, conv_ln_b=conv_ln_b, ffn_norm_g=ffn_norm_g, final_norm_g=final_norm_g)
    sm = dict(meta_tokens=m_meta_tokens, mix_norm_g=m_mix_norm_g, conv_dw_w=m_conv_dw_w, conv_dw_b=m_conv_dw_b,
              conv_ln_g=m_conv_ln_g, conv_ln_b=m_conv_ln_b, ffn_norm_g=m_ffn_norm_g, final_norm_g=m_final_norm_g)
    sv = dict(meta_tokens=v_meta_tokens, mix_norm_g=v_mix_norm_g, conv_dw_w=v_conv_dw_w, conv_dw_b=v_conv_dw_b,
              conv_ln_g=v_conv_ln_g, conv_ln_b=v_conv_ln_b, ffn_norm_g=v_ffn_norm_g, final_norm_g=v_final_norm_g)
    sg = dict(meta_tokens=g_meta, mix_norm_g=g_mix, conv_dw_w=g_dww, conv_dw_b=g_dwb, conv_ln_g=g_lng,
              conv_ln_b=g_lnb, ffn_norm_g=g_ffn, final_norm_g=g_final)
    def slab(d):
        return _pack([d[k] for k in small_order])
    shapes = [sw[k].shape for k in small_order]
    deltas = _adam(slab(sw), slab(sg), slab(sm), slab(sv), "adam_small")
    for res, dst in zip(deltas, (out_d, out_m, out_v)):
        dst.update(zip(small_order, _unpack(res, shapes)))
    out_g.update(sg)

    order = ("meta_tokens", "mix_norm_g", "w_in", "conv_dw_w", "conv_dw_b", "conv_ln_g", "conv_ln_b", "w_out",
             "ffn_norm_g", "w_gate", "w_up", "w_down", "final_norm_g")
    return (loss, grad_x, *[out_g[k] for k in order], *[out_d[k] for k in order],
            *[out_m[k] for k in order], *[out_v[k] for k in order])
```

```python
import functools
import math

import jax
import jax.numpy as jnp
from jax import lax
from jax.experimental import pallas as pl
from jax.experimental.pallas import tpu as pltpu

F32 = jnp.float32
BF16 = jnp.bfloat16
MESH = pl.DeviceIdType.MESH

EPS = 1e-6
QUERY_BLOCK = 128
LANES = 128
HEAD_DIM = 64
LOG_STICK_FLOOR = -40.0
CONV_HALO = 32
N_CHIPS = 4
N_DEV = 8
VMEM_LIMIT = 56 * 1024 * 1024

ADAM_LR = 0.001
ADAM_B1 = 0.9
ADAM_B2 = 0.999
ADAM_EPS = 1e-08
ADAM_WD = 0.01
ADAM_STEP = 10


def _pick(n, prefs):
    for p in prefs:
        if n % p == 0:
            return p
    return n


def _params(sem=None):
    return pltpu.CompilerParams(dimension_semantics=sem, vmem_limit_bytes=VMEM_LIMIT)


def _sigmoid(x):
    return 1.0 / (1.0 + jnp.exp(-x))


def _rmsnorm_fwd(h, g, name):
    L, D = h.shape
    T = _pick(L, (384, 128))

    def body(h_ref, g_ref, o_ref):
        x = h_ref[...]
        r = lax.rsqrt(jnp.mean(x * x, axis=-1, keepdims=True) + EPS)
        o_ref[...] = (x * r * g_ref[...]).astype(o_ref.dtype)

    return pl.pallas_call(
        body, name=name, grid=(L // T,),
        in_specs=[pl.BlockSpec((T, D), lambda i: (i, 0)), pl.BlockSpec((1, D), lambda i: (0, 0))],
        out_specs=pl.BlockSpec((T, D), lambda i: (i, 0)),
        out_shape=jax.ShapeDtypeStruct((L, D), BF16),
        compiler_params=_params(("parallel",)),
    )(h, g)


def _rmsnorm_bwd(h, g, dy, dh_in, name):
    L, D = h.shape
    T = _pick(L, (384, 128))

    def body(h_ref, g_ref, dy_ref, dhin_ref, dh_ref, dg_ref):
        x = h_ref[...]
        dyv = dy_ref[...]
        r = lax.rsqrt(jnp.mean(x * x, axis=-1, keepdims=True) + EPS)
        xh = x * r
        dxh = dyv * g_ref[...]
        dh_ref[...] = dhin_ref[...] + r * (dxh - xh * jnp.mean(dxh * xh, axis=-1, keepdims=True))

        @pl.when(pl.program_id(0) == 0)
        def _():
            dg_ref[...] = jnp.zeros_like(dg_ref)

        dg_ref[...] += jnp.sum(dyv * xh, axis=0, keepdims=True)

    row = pl.BlockSpec((T, D), lambda i: (i, 0))
    vec = pl.BlockSpec((1, D), lambda i: (0, 0))
    return pl.pallas_call(
        body, name=name, grid=(L // T,),
        in_specs=[row, vec, row, row], out_specs=[row, vec],
        out_shape=[jax.ShapeDtypeStruct((L, D), F32), jax.ShapeDtypeStruct((1, D), F32)],
        compiler_params=_params(("arbitrary",)),
    )(h, g, dy, dh_in)


def _loss_head(h, g, target, n_meta, seq, name):
    L, D = h.shape
    T = _pick(L, (384, 128))

    def body(h_ref, g_ref, t_ref, loss_ref, dh_ref, dg_ref):
        i = pl.program_id(0)
        x = h_ref[...]
        gv = g_ref[...]
        r = lax.rsqrt(jnp.mean(x * x, axis=-1, keepdims=True) + EPS)
        xh = x * r
        y = xh * gv
        rows = i * T + lax.broadcasted_iota(jnp.int32, (T, 1), 0)
        live = (rows >= n_meta) & (rows < n_meta + seq)
        diff = jnp.where(live, y - t_ref[...], 0.0)
        dyv = diff / D
        dxh = dyv * gv
        dh_ref[...] = r * (dxh - xh * jnp.mean(dxh * xh, axis=-1, keepdims=True))

        @pl.when(i == 0)
        def _():
            dg_ref[...] = jnp.zeros_like(dg_ref)
            loss_ref[...] = jnp.zeros_like(loss_ref)

        dg_ref[...] += jnp.sum(dyv * xh, axis=0, keepdims=True)
        per_row = jnp.mean(diff * diff, axis=-1, keepdims=True)
        loss_ref[...] += 0.5 * jnp.sum(per_row, axis=0, keepdims=True)

    row = pl.BlockSpec((T, D), lambda i: (i, 0))
    vec = pl.BlockSpec((1, D), lambda i: (0, 0))
    one = pl.BlockSpec((1, 1), lambda i: (0, 0))
    return pl.pallas_call(
        body, name=name, grid=(L // T,),
        in_specs=[row, vec, row], out_specs=[one, row, vec],
        out_shape=[jax.ShapeDtypeStruct((1, 1), F32), jax.ShapeDtypeStruct((L, D), F32),
                   jax.ShapeDtypeStruct((1, D), F32)],
        compiler_params=_params(("arbitrary",)),
    )(h, g, target)


def _swiglu_fwd(g, u, name):
    L, F = g.shape
    T = _pick(L, (192, 128))

    def body(g_ref, u_ref, o_ref):
        gv = g_ref[...]
        o_ref[...] = (gv * _sigmoid(gv) * u_ref[...]).astype(o_ref.dtype)

    row = pl.BlockSpec((T, F), lambda i: (i, 0))
    return pl.pallas_call(
        body, name=name, grid=(L // T,), in_specs=[row, row], out_specs=row,
        out_shape=jax.ShapeDtypeStruct((L, F), BF16), compiler_params=_params(("parallel",)),
    )(g, u)


def _swiglu_bwd(g, u, dact, name):
    L, F = g.shape
    T = _pick(L, (192, 128))

    def body(g_ref, u_ref, d_ref, dg_ref, du_ref):
        gv = g_ref[...]
        dv = d_ref[...]
        s = _sigmoid(gv)
        du_ref[...] = (dv * (gv * s)).astype(du_ref.dtype)
        dg_ref[...] = (dv * u_ref[...] * (s * (1.0 + gv * (1.0 - s)))).astype(dg_ref.dtype)

    row = pl.BlockSpec((T, F), lambda i: (i, 0))
    return pl.pallas_call(
        body, name=name, grid=(L // T,), in_specs=[row, row, row], out_specs=[row, row],
        out_shape=[jax.ShapeDtypeStruct((L, F), BF16), jax.ShapeDtypeStruct((L, F), BF16)],
        compiler_params=_params(("parallel",)),
    )(g, u, dact)


def _mm_nn(pairs, out_dtype, name, residual=None, cols=None):
    M = pairs[0][0].shape[0]
    col0, N = cols if cols is not None else (0, pairs[0][1].shape[-1])
    tm = _pick(M, (1056, 384, 128))
    tn = _pick(math.gcd(N, col0) if col0 else N, (640, 512, 256, 128))
    jb = col0 // tn
    n = len(pairs)

    def body(*refs):
        a_refs, w_refs = refs[:n], refs[n:2 * n]
        o_ref = refs[-1]
        acc = None
        for a_ref, w_ref in zip(a_refs, w_refs):
            d = jnp.dot(a_ref[...].astype(BF16), w_ref[...], preferred_element_type=F32)
            acc = d if acc is None else acc + d
        if residual is not None:
            acc = acc + refs[2 * n][...]
        o_ref[...] = acc.astype(o_ref.dtype)

    in_specs = [pl.BlockSpec((tm, a.shape[1]), lambda i, j: (i, 0)) for a, _, _, _ in pairs]
    for a, _, layer, kblk in pairs:
        in_specs.append(pl.BlockSpec((None, a.shape[1], tn), functools.partial(lambda i, j, l, kb: (l, kb, j + jb), l=layer, kb=kblk)))
    args = [p[0] for p in pairs] + [p[1] for p in pairs]
    if residual is not None:
        in_specs.append(pl.BlockSpec((tm, tn), lambda i, j: (i, j)))
        args.append(residual)
    return pl.pallas_call(
        body, name=name, grid=(M // tm, N // tn), in_specs=in_specs,
        out_specs=pl.BlockSpec((tm, tn), lambda i, j: (i, j)),
        out_shape=jax.ShapeDtypeStruct((M, N), out_dtype),
        compiler_params=_params(("parallel", "parallel")),
    )(*args)


def _mm_nt(pairs, out_dtype, name):
    M = pairs[0][0].shape[0]
    K = pairs[0][1].shape[1]
    tm = _pick(M, (1056, 384, 128))
    tk = _pick(K, (512, 1408, 256, 128))
    n = len(pairs)

    def body(*refs):
        d_refs, w_refs = refs[:n], refs[n:2 * n]
        o_ref = refs[-1]
        acc = None
        for d_ref, w_ref in zip(d_refs, w_refs):
            d = lax.dot_general(d_ref[...].astype(BF16), w_ref[...], (((1,), (1,)), ((), ())),
                                preferred_element_type=F32)
            acc = d if acc is None else acc + d
        o_ref[...] = acc.astype(o_ref.dtype)

    in_specs = [pl.BlockSpec((tm, d.shape[1]), lambda i, j: (i, 0)) for d, _, _, _ in pairs]
    for d, _, layer, cblk in pairs:
        in_specs.append(pl.BlockSpec((None, tk, d.shape[1]), functools.partial(lambda i, j, l, cb: (l, j, cb), l=layer, cb=cblk)))
    args = [p[0] for p in pairs] + [p[1] for p in pairs]
    return pl.pallas_call(
        body, name=name, grid=(M // tm, K // tk), in_specs=in_specs,
        out_specs=pl.BlockSpec((tm, tk), lambda i, j: (i, j)),
        out_shape=jax.ShapeDtypeStruct((M, K), out_dtype),
        compiler_params=_params(("parallel", "parallel")),
    )(*args)


def _mm_tn(a, b, name, col_sharded):
    M, K = a.shape
    N = b.shape[1]
    tm = _pick(M, (1056, 384, 128))
    tk = _pick(K, (512, 1408, 256, 128))
    tn = N // N_CHIPS if col_sharded else _pick(N, (512, 128))

    def body(a_ref, b_ref, o_ref):
        @pl.when(pl.program_id(2) == 0)
        def _():
            o_ref[...] = jnp.zeros_like(o_ref)

        o_ref[...] += lax.dot_general(a_ref[...].astype(BF16), b_ref[...].astype(BF16),
                                      (((0,), (0,)), ((), ())), preferred_element_type=F32)

    if col_sharded:
        out_shape = jax.ShapeDtypeStruct((N_CHIPS, K, tn), F32)
        out_spec = pl.BlockSpec((None, tk, tn), lambda k, j, m: (j, k, 0))
    else:
        out_shape = jax.ShapeDtypeStruct((K, N), F32)
        out_spec = pl.BlockSpec((tk, tn), lambda k, j, m: (k, j))
    out = pl.pallas_call(
        body, name=name, grid=(K // tk, N // tn, M // tm),
        in_specs=[pl.BlockSpec((tm, tk), lambda k, j, m: (m, k)), pl.BlockSpec((tm, tn), lambda k, j, m: (m, j))],
        out_specs=out_spec, out_shape=out_shape,
        compiler_params=_params(("parallel", "parallel", "arbitrary")),
    )(a, b)
    return out if col_sharded else out.reshape(N_CHIPS, K // N_CHIPS, N)


def _stack_heads(x, scale=None):
    lane = lax.broadcasted_iota(jnp.int32, x.shape, 1)
    zero = jnp.zeros_like(x)
    lo = jnp.where(lane < HEAD_DIM, x, zero)
    hi = jnp.where(lane < HEAD_DIM, zero, x)
    out = jnp.concatenate([lo, hi], axis=0)
    return out if scale is None else out * scale


def _unstack_heads(x2):
    qb = x2.shape[0] // 2
    lane = lax.broadcasted_iota(jnp.int32, (qb, LANES), 1)
    return jnp.where(lane < HEAD_DIM, x2[:qb], x2[qb:])


def _tri_and_ones(strict):
    r = lax.broadcasted_iota(jnp.int32, (QUERY_BLOCK, 2 * QUERY_BLOCK), 0)
    c = lax.broadcasted_iota(jnp.int32, (QUERY_BLOCK, 2 * QUERY_BLOCK), 1)
    tri = (r > c) if strict else (r >= c)
    return jnp.where(tri | (c >= QUERY_BLOCK), 1.0, 0.0).astype(BF16)


def _dot2(x, m):
    xh = x.astype(BF16)
    xl = (x - xh.astype(F32)).astype(BF16)
    return jnp.dot(xh, m, preferred_element_type=F32) + jnp.dot(xl, m, preferred_element_type=F32)


def _causal_valid():
    r = lax.broadcasted_iota(jnp.int32, (2 * QUERY_BLOCK, QUERY_BLOCK), 0) & (QUERY_BLOCK - 1)
    c = lax.broadcasted_iota(jnp.int32, (2 * QUERY_BLOCK, QUERY_BLOCK), 1)
    return c < r


def _sweep_older(i, step, carry_ref, first):
    def cond(state):
        n, live = state
        return jnp.logical_and(n < i, live)

    def older(state):
        n, _ = state
        step(i - 1 - n, False)
        return n + 1, jnp.max(carry_ref[...]) > LOG_STICK_FLOOR

    lax.while_loop(cond, older, (first, jnp.max(carry_ref[...]) > LOG_STICK_FLOOR))


def _attn_fwd(qkv, sb_width, name):
    L = qkv.shape[0]
    QB = QUERY_BLOCK
    nb = L // QB
    n_pairs = sb_width // LANES
    scale = 1.0 / math.sqrt(HEAD_DIM)

    def body(q_ref, k_ref, v_ref, o_ref, acc_ref, carry_ref):
        i = pl.program_id(1)
        q2 = _stack_heads(q_ref[...], scale)
        txo = _tri_and_ones(True)
        valid = _causal_valid()
        acc_ref[...] = jnp.zeros_like(acc_ref)
        carry_ref[...] = jnp.zeros_like(carry_ref)

        def front(j, diag):
            start = pl.multiple_of(j * QB, QB)
            kb = k_ref[pl.ds(start, QB), :]
            vb = v_ref[pl.ds(start, QB), :]
            z = lax.dot_general(q2, kb, (((1,), (1,)), ((), ())), preferred_element_type=F32)
            sp = jnp.log(1.0 + jnp.exp(-jnp.abs(z)))
            a = jnp.minimum(z, 0.0) - sp
            b = jnp.minimum(-z, 0.0) - sp
            if diag:
                b = jnp.where(valid, b, 0.0)
            return a, _dot2(b, txo), vb

        def back(a, res, vb, diag):
            w = jnp.exp(a + res[:, :QB] + carry_ref[...])
            if diag:
                w = jnp.where(valid, w, 0.0)
            carry_ref[...] += res[:, QB:]
            acc_ref[...] += _dot2(w, vb)

        def step(j, diag):
            back(*front(j, diag), diag)

        @pl.when(i == 0)
        def _():
            step(0, True)

        @pl.when(i > 0)
        def _():
            f0 = front(i, True)
            f1 = front(i - 1, False)
            back(*f0, True)
            back(*f1, False)
            _sweep_older(i, step, carry_ref, 1)

        o_ref[...] = _unstack_heads(acc_ref[...])

    return pl.pallas_call(
        body, name=name, grid=(n_pairs, nb),
        in_specs=[pl.BlockSpec((QB, LANES), lambda p, i: (i, p)),
                  pl.BlockSpec((L, LANES), lambda p, i: (0, n_pairs + p)),
                  pl.BlockSpec((L, LANES), lambda p, i: (0, 2 * n_pairs + p))],
        out_specs=pl.BlockSpec((QB, LANES), lambda p, i: (i, p)),
        out_shape=jax.ShapeDtypeStruct((L, sb_width), F32),
        scratch_shapes=[pltpu.VMEM((2 * QB, LANES), F32), pltpu.VMEM((2 * QB, LANES), F32)],
        compiler_params=_params(("parallel", "arbitrary")),
    )(qkv, qkv, qkv)


def _attn_bwd(qkv, o, dmixed, sb_width, name):
    L = qkv.shape[0]
    QB = QUERY_BLOCK
    nb = L // QB
    n_pairs = sb_width // LANES
    scale = 1.0 / math.sqrt(HEAD_DIM)

    def body(q_ref, k_ref, v_ref, o_ref, do_ref, dq_ref, dk_ref, dv_ref,
             dq_acc, dk_acc, dv_acc, ce_ref, cr_ref):
        i = pl.program_id(1)

        @pl.when(i == 0)
        def _():
            dk_acc[...] = jnp.zeros_like(dk_acc)
            dv_acc[...] = jnp.zeros_like(dv_acc)

        q2 = _stack_heads(q_ref[...], scale)
        do2 = _stack_heads(do_ref[...].astype(BF16))
        ov = o_ref[...]
        o2 = jnp.concatenate([ov, ov], axis=0)
        ones = jnp.ones((LANES, LANES), BF16)
        dtot = _dot2(do2.astype(F32) * o2, ones)
        txo = _tri_and_ones(True)
        tio = _tri_and_ones(False)
        valid = _causal_valid()
        dq_acc[...] = jnp.zeros_like(dq_acc)
        ce_ref[...] = jnp.zeros_like(ce_ref)
        cr_ref[...] = jnp.zeros_like(cr_ref)

        def front(j, diag):
            start = pl.multiple_of(j * QB, QB)
            kb = k_ref[pl.ds(start, QB), :]
            vb = v_ref[pl.ds(start, QB), :]
            z = lax.dot_general(q2, kb, (((1,), (1,)), ((), ())), preferred_element_type=F32)
            e = jnp.exp(-jnp.abs(z))
            sp = jnp.log(1.0 + e)
            a = jnp.minimum(z, 0.0) - sp
            b = jnp.minimum(-z, 0.0) - sp
            rinv = 1.0 / (1.0 + e)
            pos = z >= 0.0
            beta = jnp.where(pos, rinv, e * rinv)
            one_m_beta = jnp.where(pos, e * rinv, rinv)
            if diag:
                b = jnp.where(valid, b, 0.0)
            dw = lax.dot_general(do2, vb, (((1,), (1,)), ((), ())), preferred_element_type=F32)
            return start, kb, a, beta, one_m_beta, _dot2(b, txo), dw

        def back(start, kb, a, beta, one_m_beta, res, dw, diag):
            w = jnp.exp(a + res[:, :QB] + ce_ref[...])
            if diag:
                w = jnp.where(valid, w, 0.0)
            ce_ref[...] += res[:, QB:]
            g = w * dw
            res2 = _dot2(g, tio)
            rinc = res2[:, :QB] + cr_ref[...]
            cr_ref[...] += res2[:, QB:]
            dz = g * one_m_beta - beta * (dtot - rinc)
            if diag:
                dz = jnp.where(valid, dz, 0.0)
            dzb = dz.astype(BF16)
            dq_acc[...] += jnp.dot(dzb, kb, preferred_element_type=F32)
            dk_acc[pl.ds(start, QB), :] += lax.dot_general(
                dzb, q2, (((0,), (0,)), ((), ())), preferred_element_type=F32)
            dv_acc[pl.ds(start, QB), :] += lax.dot_general(
                w.astype(BF16), do2, (((0,), (0,)), ((), ())), preferred_element_type=F32)

        def step(j, diag):
            back(*front(j, diag), diag)

        @pl.when(i == 0)
        def _():
            step(0, True)

        @pl.when(i > 0)
        def _():
            f0 = front(i, True)
            f1 = front(i - 1, False)
            back(*f0, True)
            back(*f1, False)
            _sweep_older(i, step, ce_ref, 1)

        dq_ref[...] = (_unstack_heads(dq_acc[...]) * scale).astype(dq_ref.dtype)

        @pl.when(i == nb - 1)
        def _():
            dk_ref[...] = dk_acc[...].astype(dk_ref.dtype)
            dv_ref[...] = dv_acc[...].astype(dv_ref.dtype)

    blk = pl.BlockSpec((QB, LANES), lambda p, i: (i, p))
    col = pl.BlockSpec((L, LANES), lambda p, i: (0, p))
    return pl.pallas_call(
        body, name=name, grid=(n_pairs, nb),
        in_specs=[blk,
                  pl.BlockSpec((L, LANES), lambda p, i: (0, n_pairs + p)),
                  pl.BlockSpec((L, LANES), lambda p, i: (0, 2 * n_pairs + p)),
                  blk, blk],
        out_specs=[blk, col, col],
        out_shape=[jax.ShapeDtypeStruct((L, sb_width), BF16)] * 3,
        scratch_shapes=[pltpu.VMEM((2 * QB, LANES), F32), pltpu.VMEM((L, LANES), F32),
                        pltpu.VMEM((L, LANES), F32), pltpu.VMEM((2 * QB, LANES), F32),
                        pltpu.VMEM((2 * QB, LANES), F32)],
        compiler_params=_params(("parallel", "arbitrary")),
    )(qkv, qkv, qkv, o, dmixed)


def _conv_tile(L):
    return _pick(L, (384, 128))


def _glu(x, C):
    return x[:, :C] * _sigmoid(x[:, C:])


CONV_CHUNK = 32
SHIFT_TAIL = 24


def _fill_shifted(src_ref, dst_ref):
    n = dst_ref.shape[1]
    for r in range(1, 8):
        dst_ref[r - 1] = src_ref[r:r + n, :]


def _rows_at(src_ref, shifted_ref, start, n):
    q, r = divmod(start, 8)
    if r == 0:
        return src_ref[start:start + n, :]
    return shifted_ref[r - 1, 8 * q:8 * q + n, :]


def _conv_fwd(cacg, dw_w, dw_b, ln_g, ln_b, name):
    L, C2 = cacg.shape
    C = C2 // 2
    T = _conv_tile(L)
    H = CONV_HALO
    K = dw_w.shape[0]
    CH = CONV_CHUNK

    def body(x_ref, prev_ref, w_ref, b_ref, g_ref, beta_ref, o_ref, u_ref, us_ref):
        i = pl.program_id(0)
        u_ref[0:H, :] = jnp.where(i > 0, _glu(prev_ref[...], C), 0.0)
        u_ref[H:, :] = _glu(x_ref[...], C)
        _fill_shifted(u_ref, us_ref)
        for c0 in range(0, T, CH):
            y = jnp.broadcast_to(b_ref[...], (CH, C))
            for k in range(K):
                y = y + w_ref[k:k + 1, :] * _rows_at(u_ref, us_ref, c0 + H - (K - 1) + k, CH)
            mu = jnp.mean(y, axis=-1, keepdims=True)
            yc = y - mu
            rstd = lax.rsqrt(jnp.mean(yc * yc, axis=-1, keepdims=True) + EPS)
            ln = yc * rstd * g_ref[...] + beta_ref[...]
            o_ref[c0:c0 + CH, :] = (ln * _sigmoid(ln)).astype(o_ref.dtype)

    vec = pl.BlockSpec((1, C), lambda i: (0, 0))
    return pl.pallas_call(
        body, name=name, grid=(L // T,),
        in_specs=[pl.BlockSpec((T, C2), lambda i: (i, 0)),
                  pl.BlockSpec((H, C2), lambda i: (jnp.maximum(i * (T // H) - 1, 0), 0)),
                  pl.BlockSpec((K, C), lambda i: (0, 0)), vec, vec, vec],
        out_specs=pl.BlockSpec((T, C), lambda i: (i, 0)),
        out_shape=jax.ShapeDtypeStruct((L, C), BF16),
        scratch_shapes=[pltpu.VMEM((T + H, C), F32), pltpu.VMEM((7, T + SHIFT_TAIL, C), F32)],
        compiler_params=_params(("parallel",)),
    )(cacg, cacg, dw_w, dw_b, ln_g, ln_b)


def _conv_bwd(cacg, dmixed, dw_w, dw_b, ln_g, ln_b, name):
    L, C2 = cacg.shape
    C = C2 // 2
    T = _conv_tile(L)
    H = CONV_HALO
    K = dw_w.shape[0]
    nt = L // T
    TE = T + H

    CH = CONV_CHUNK

    def body(x_ref, prev_ref, next_ref, d_ref, dnext_ref, w_ref, b_ref, g_ref, beta_ref,
             dca_ref, dcg_ref, dwt_ref, db_ref, dg_ref, dbeta_ref, u_ref, us_ref, dy_ref, dys_ref):
        i = pl.program_id(0)
        last = i == nt - 1

        @pl.when(i == 0)
        def _():
            dwt_ref[...] = jnp.zeros_like(dwt_ref)
            db_ref[...] = jnp.zeros_like(db_ref)
            dg_ref[...] = jnp.zeros_like(dg_ref)
            dbeta_ref[...] = jnp.zeros_like(dbeta_ref)

        u_ref[0:H, :] = jnp.where(i > 0, _glu(prev_ref[...], C), 0.0)
        u_ref[H:H + T, :] = _glu(x_ref[...], C)
        u_ref[H + T:, :] = _glu(next_ref[...], C)
        _fill_shifted(u_ref, us_ref)
        dg_acc = jnp.zeros((1, C), F32)
        dbeta_acc = jnp.zeros((1, C), F32)
        db_acc = jnp.zeros((1, C), F32)
        for c0 in range(0, TE, CH):
            y = jnp.broadcast_to(b_ref[...], (CH, C))
            for k in range(K):
                y = y + w_ref[k:k + 1, :] * _rows_at(u_ref, us_ref, c0 + H - (K - 1) + k, CH)
            mu = jnp.mean(y, axis=-1, keepdims=True)
            yc = y - mu
            rstd = lax.rsqrt(jnp.mean(yc * yc, axis=-1, keepdims=True) + EPS)
            yh = yc * rstd
            ln = yh * g_ref[...] + beta_ref[...]
            s = _sigmoid(ln)
            dout = d_ref[c0:c0 + CH, :] if c0 < T else jnp.where(last, 0.0, dnext_ref[c0 - T:c0 - T + CH, :])
            dln = dout * (s * (1.0 + ln * (1.0 - s)))
            dyh = dln * g_ref[...]
            dy = rstd * (dyh - jnp.mean(dyh, axis=-1, keepdims=True)
                         - yh * jnp.mean(dyh * yh, axis=-1, keepdims=True))
            dy_ref[c0:c0 + CH, :] = dy
            if c0 < T:
                dg_acc = dg_acc + jnp.sum(dln * yh, axis=0, keepdims=True)
                dbeta_acc = dbeta_acc + jnp.sum(dln, axis=0, keepdims=True)
                db_acc = db_acc + jnp.sum(dy, axis=0, keepdims=True)
        dg_ref[...] += dg_acc
        dbeta_ref[...] += dbeta_acc
        db_ref[...] += db_acc
        _fill_shifted(dy_ref, dys_ref)
        for k in range(K):
            dwt_ref[k:k + 1, :] += jnp.sum(
                dy_ref[0:T, :] * _rows_at(u_ref, us_ref, H - (K - 1) + k, T), axis=0, keepdims=True)
        for c0 in range(0, T, CH):
            du = jnp.zeros((CH, C), F32)
            for k in range(K):
                du = du + w_ref[k:k + 1, :] * _rows_at(dy_ref, dys_ref, c0 + (K - 1) - k, CH)
            x = x_ref[c0:c0 + CH, :]
            sg = _sigmoid(x[:, C:])
            dca_ref[c0:c0 + CH, :] = (du * sg).astype(dca_ref.dtype)
            dcg_ref[c0:c0 + CH, :] = (du * x[:, :C] * sg * (1.0 - sg)).astype(dcg_ref.dtype)

    nh = L // H
    vec = pl.BlockSpec((1, C), lambda i: (0, 0))
    row = pl.BlockSpec((T, C), lambda i: (i, 0))
    return pl.pallas_call(
        body, name=name, grid=(nt,),
        in_specs=[pl.BlockSpec((T, C2), lambda i: (i, 0)),
                  pl.BlockSpec((H, C2), lambda i: (jnp.maximum(i * (T // H) - 1, 0), 0)),
                  pl.BlockSpec((H, C2), lambda i: (jnp.minimum((i + 1) * (T // H), nh - 1), 0)),
                  pl.BlockSpec((T, C), lambda i: (i, 1)),
                  pl.BlockSpec((H, C), lambda i: (jnp.minimum((i + 1) * (T // H), nh - 1), 1)),
                  pl.BlockSpec((K, C), lambda i: (0, 0)), vec, vec, vec],
        out_specs=[row, row, pl.BlockSpec((H, C), lambda i: (0, 0)), vec, vec, vec],
        out_shape=[jax.ShapeDtypeStruct((L, C), BF16), jax.ShapeDtypeStruct((L, C), BF16),
                   jax.ShapeDtypeStruct((H, C), F32), jax.ShapeDtypeStruct((1, C), F32),
                   jax.ShapeDtypeStruct((1, C), F32), jax.ShapeDtypeStruct((1, C), F32)],
        scratch_shapes=[pltpu.VMEM((T + 2 * H, C), F32), pltpu.VMEM((7, TE + SHIFT_TAIL, C), F32),
                        pltpu.VMEM((TE, C), F32), pltpu.VMEM((7, T + SHIFT_TAIL, C), F32)],
        compiler_params=_params(("arbitrary",)),
    )(cacg, cacg, cacg, dmixed, dmixed, dw_w, dw_b, ln_g, ln_b)


def _local_step(h0, target, n_meta, seq, norms, conv_p, wts, final_g):
    mix_g, ffn_g = norms
    dw_w, dw_b, ln_g, ln_b = conv_p
    depth = mix_g.shape[0]
    C = dw_b.shape[-1]
    sbw = (wts["w_in"].shape[-1] - 2 * C) // 3
    assert sbw == C, "the mixer halves must have equal width"
    row = lambda a, i: a[i][None, :]

    h = h0
    saved = []
    for i in range(depth):
        hn = _rmsnorm_fwd(h, row(mix_g, i), f"mix_norm_{i}")
        proj_qkv = _mm_nn([(hn, wts["w_in"], i, 0)], BF16, f"in_qkv_{i}", cols=(0, 3 * sbw))
        cacg = _mm_nn([(hn, wts["w_in"], i, 0)], F32, f"in_conv_{i}", cols=(3 * sbw, 2 * C))
        attn = _attn_fwd(proj_qkv, sbw, f"attn_fwd_{i}")
        conv = _conv_fwd(cacg, dw_w[i], row(dw_b, i), row(ln_g, i), row(ln_b, i), f"conv_fwd_{i}")
        h_mid = _mm_nn([(attn, wts["w_out"], i, 0), (conv, wts["w_out"], i, 1)], F32, f"out_proj_{i}", residual=h)
        hn2 = _rmsnorm_fwd(h_mid, row(ffn_g, i), f"ffn_norm_{i}")
        g = _mm_nt([(hn2, wts["w_gate_t"], i, 0)], F32, f"gate_{i}")
        u = _mm_nt([(hn2, wts["w_up_t"], i, 0)], F32, f"up_{i}")
        act = _swiglu_fwd(g, u, f"swiglu_{i}")
        h_out = _mm_nn([(act, wts["w_down"], i, 0)], F32, f"down_{i}", residual=h_mid)
        saved.append((h, hn, proj_qkv, cacg, attn, conv, h_mid, hn2, g, u, act))
        h = h_out

    loss, dh, d_final_g = _loss_head(h, final_g[None, :], target, n_meta, seq, "loss_head")

    grads = {k: [None] * depth for k in ("w_in", "w_out", "w_gate_t", "w_up_t", "w_down", "mix_g", "ffn_g",
                                         "dw_w", "dw_b", "ln_g", "ln_b")}
    for i in reversed(range(depth)):
        h_in, hn, proj_qkv, cacg, attn, conv, h_mid, hn2, g, u, act = saved[i]
        dact = _mm_nt([(dh, wts["w_down"], i, 0)], F32, f"d_act_{i}")
        grads["w_down"][i] = _mm_tn(act, dh, f"dw_down_{i}", col_sharded=False)
        dg, du = _swiglu_bwd(g, u, dact, f"swiglu_bwd_{i}")
        dhn2 = _mm_nn([(dg, wts["w_gate_t"], i, 0), (du, wts["w_up_t"], i, 0)], F32, f"d_hn2_{i}")
        grads["w_gate_t"][i] = _mm_tn(dg, hn2, f"dw_gate_{i}", col_sharded=False)
        grads["w_up_t"][i] = _mm_tn(du, hn2, f"dw_up_{i}", col_sharded=False)
        dh, grads["ffn_g"][i] = _rmsnorm_bwd(h_mid, row(ffn_g, i), dhn2, dh, f"ffn_norm_bwd_{i}")
        dmixed = _mm_nt([(dh, wts["w_out"], i, 0)], F32, f"d_mixed_{i}")
        mixed = jnp.concatenate([attn.astype(BF16), conv], axis=1)
        grads["w_out"][i] = _mm_tn(mixed, dh, f"dw_out_{i}", col_sharded=False)
        dq, dk, dv = _attn_bwd(proj_qkv, attn, dmixed, sbw, f"attn_bwd_{i}")
        dca, dcg, d_dw, d_b, d_lg, d_lb = _conv_bwd(
            cacg, dmixed, dw_w[i], row(dw_b, i), row(ln_g, i), row(ln_b, i), f"conv_bwd_{i}")
        grads["dw_w"][i] = d_dw[:dw_w.shape[1]]
        grads["dw_b"][i], grads["ln_g"][i], grads["ln_b"][i] = d_b[0], d_lg[0], d_lb[0]
        dproj = jnp.concatenate([dq, dk, dv, dca, dcg], axis=1)
        dhn = _mm_nt([(dproj, wts["w_in"], i, 0)], F32, f"d_hn_{i}")
        grads["w_in"][i] = _mm_tn(hn, dproj, f"dw_in_{i}", col_sharded=True)
        dh, d_mix = _rmsnorm_bwd(h_in, row(mix_g, i), dhn, dh, f"mix_norm_bwd_{i}")
        grads["mix_g"][i] = d_mix[0]
        grads["ffn_g"][i] = grads["ffn_g"][i][0]
    grads["final_g"] = d_final_g[0]
    return loss, dh, grads


ANY = pl.BlockSpec(memory_space=pl.ANY)


def _position():
    return lax.axis_index("x"), lax.axis_index("y"), lax.axis_index("c")


def _chip_at(x, y, k):
    return (1 - x if k & 2 else x), (1 - y if k & 1 else y)


def _half_rows(ref, half, rows, base=0):
    start = pl.multiple_of(base + half * rows, 8)
    lead = (slice(None),) * (len(ref.shape) - 2)
    return ref.at[(*lead, pl.ds(start, rows), slice(None))]


def _gather_weights(fulls, shard_shapes, col_sharded):
    n = len(fulls)

    def body(*refs):
        f_refs = refs[n:2 * n]
        send_sems, recv_sems = refs[2 * n:]
        x, y, c = _position()
        me = 2 * x + y

        def block(wi, chip, half):
            _, R, C = shard_shapes[wi]
            if col_sharded[wi]:
                cols = pl.ds(pl.multiple_of(chip * C, LANES), C)
                return f_refs[wi].at[:, pl.ds(pl.multiple_of(half * (R // 2), 8), R // 2), cols]
            return _half_rows(f_refs[wi], half, R // 2, base=chip * R)

        def copy(wi, slot, blk, to):
            return pltpu.make_async_remote_copy(
                src_ref=blk, dst_ref=blk, send_sem=send_sems.at[6 * wi + slot],
                recv_sem=recv_sems.at[6 * wi + slot], device_id=to, device_id_type=MESH)

        sent = []
        for wi in range(n):
            for k in (1, 2, 3):
                cp = copy(wi, k - 1, block(wi, me, c), (*_chip_at(x, y, k), c))
                cp.start()
                sent.append(cp)
        for wi in range(n):
            for k in (1, 2, 3):
                landed = block(wi, me ^ k, c)
                copy(wi, k - 1, landed, (x, y, c)).wait_recv()
                cp = copy(wi, 2 + k, landed, (x, y, 1 - c))
                cp.start()
                sent.append(cp)
        for wi in range(n):
            for k in (1, 2, 3):
                copy(wi, 2 + k, block(wi, me ^ k, 1 - c), (x, y, c)).wait_recv()
        for cp in sent:
            cp.wait_send()

    return pl.pallas_call(
        body, name="gather_weights", out_shape=[jax.ShapeDtypeStruct(f.shape, f.dtype) for f in fulls],
        in_specs=[ANY] * n, out_specs=[ANY] * n, input_output_aliases={i: i for i in range(n)},
        scratch_shapes=[pltpu.SemaphoreType.DMA((6 * n,)), pltpu.SemaphoreType.DMA((6 * n,))],
    )(*fulls)


def _place_shard(w, chip, col_sharded, dtype, name):
    lyr, R, C = w.shape
    tr = _pick(R, (256, 352, 128, 48))
    nr = R // tr

    def body(chip_ref, w_ref, o_ref):
        o_ref[...] = w_ref[...].astype(dtype)

    if col_sharded:
        shape = (lyr, R, N_CHIPS * C)
        out_spec = pl.BlockSpec((None, tr, C), lambda l, r, chip_ref: (l, r, chip_ref[0]))
    else:
        shape = (lyr, N_CHIPS * R, C)
        out_spec = pl.BlockSpec((None, tr, C), lambda l, r, chip_ref: (l, chip_ref[0] * nr + r, 0))
    grid_spec = pltpu.PrefetchScalarGridSpec(
        num_scalar_prefetch=1, grid=(lyr, nr),
        in_specs=[pl.BlockSpec((None, tr, C), lambda l, r, chip_ref: (l, r, 0))], out_specs=out_spec)
    return pl.pallas_call(
        body, name=name, grid_spec=grid_spec, out_shape=jax.ShapeDtypeStruct(shape, dtype),
        compiler_params=_params(("parallel", "parallel")),
    )(chip, w)


def _rs_to_sibling(grads):
    n = len(grads)
    outs = [jax.ShapeDtypeStruct((g.shape[0], g.shape[1] // 2, g.shape[2]), g.dtype) for g in grads]

    def body(*refs):
        g_refs, l_refs = refs[:n], refs[n:2 * n]
        send_sems, recv_sems = refs[2 * n:]
        x, y, c = _position()
        cps = []
        for wi in range(n):
            R = grads[wi].shape[1]
            cp = pltpu.make_async_remote_copy(
                src_ref=_half_rows(g_refs[wi], 1 - c, R // 2), dst_ref=l_refs[wi],
                send_sem=send_sems.at[wi], recv_sem=recv_sems.at[wi],
                device_id=(x, y, 1 - c), device_id_type=MESH)
            cp.start()
            cps.append(cp)
        for cp in cps:
            cp.wait()

    return pl.pallas_call(
        body, name="grads_to_sibling", out_shape=outs, in_specs=[ANY] * n, out_specs=[ANY] * n,
        scratch_shapes=[pltpu.SemaphoreType.DMA((n,)), pltpu.SemaphoreType.DMA((n,))],
    )(*grads)


def _chip_sum(g, landed, core, name):
    _, R, C = g.shape
    hr = R // 2
    tr = _pick(hr, (256, 352, 128))
    nr = hr // tr

    def body(c_ref, g_ref, l_ref, o_ref):
        o_ref[...] = (g_ref[...] + l_ref[...]).astype(BF16)

    grid_spec = pltpu.PrefetchScalarGridSpec(
        num_scalar_prefetch=1, grid=(N_CHIPS, nr),
        in_specs=[pl.BlockSpec((None, tr, C), lambda j, r, c_ref: (j, c_ref[0] * nr + r, 0)),
                  pl.BlockSpec((None, tr, C), lambda j, r, c_ref: (j, r, 0))],
        out_specs=pl.BlockSpec((None, tr, C), lambda j, r, c_ref: (j, r, 0)))
    return pl.pallas_call(
        body, name=name, grid_spec=grid_spec, out_shape=jax.ShapeDtypeStruct((N_CHIPS, hr, C), BF16),
        compiler_params=_params(("parallel", "parallel")),
    )(core, g, landed)


def _rs_across_chips(parts):
    n = len(parts)
    outs = [jax.ShapeDtypeStruct(p.shape, p.dtype) for p in parts]

    def body(*refs):
        p_refs, l_refs = refs[:n], refs[n:2 * n]
        send_sems, recv_sems = refs[2 * n:]
        x, y, c = _position()
        me = 2 * x + y
        sent = []
        for wi in range(n):
            for k in (1, 2, 3):
                cp = pltpu.make_async_remote_copy(
                    src_ref=p_refs[wi].at[me ^ k], dst_ref=l_refs[wi].at[me],
                    send_sem=send_sems.at[3 * wi + k - 1], recv_sem=recv_sems.at[3 * wi + k - 1],
                    device_id=(*_chip_at(x, y, k), c), device_id_type=MESH)
                cp.start()
                sent.append(cp)
        for wi in range(n):
            for k in (1, 2, 3):
                slot = l_refs[wi].at[me ^ k]
                pltpu.make_async_remote_copy(
                    src_ref=slot, dst_ref=slot, send_sem=send_sems.at[3 * wi + k - 1],
                    recv_sem=recv_sems.at[3 * wi + k - 1], device_id=(x, y, c), device_id_type=MESH).wait_recv()
        for cp in sent:
            cp.wait_send()

    return pl.pallas_call(
        body, name="grads_across_chips", out_shape=outs, in_specs=[ANY] * n, out_specs=[ANY] * n,
        scratch_shapes=[pltpu.SemaphoreType.DMA((3 * n,)), pltpu.SemaphoreType.DMA((3 * n,))],
    )(*parts)


def _sum_chips(parts, landed, where, layer, depth, prev, name):
    _, hr, C = landed.shape
    tr = _pick(hr, (256, 352, 128))
    nr = hr // tr

    def body(*refs):
        own_ref, slots, o_ref = refs[1], refs[2:2 + N_CHIPS], refs[-1]
        chip = refs[0][0]
        total = None
        for q in range(N_CHIPS):
            term = jnp.where(chip == q, own_ref[...], slots[q][...]).astype(F32)
            total = term if total is None else total + term
        o_ref[...] = total

    def slot_spec(q):
        return pl.BlockSpec((None, tr, C), lambda r, w: (jnp.where(w[0] == q, (q + 1) % N_CHIPS, q), r, 0))

    in_specs = [pl.BlockSpec((None, tr, C), lambda r, w: (w[0], r, 0))] + [slot_spec(q) for q in range(N_CHIPS)]
    args = [where, parts] + [landed] * N_CHIPS
    aliases = {}
    if prev is not None:
        in_specs.append(ANY)
        args.append(prev)
        aliases = {len(args) - 1: 0}
    grid_spec = pltpu.PrefetchScalarGridSpec(
        num_scalar_prefetch=1, grid=(nr,), in_specs=in_specs,
        out_specs=pl.BlockSpec((None, tr, C), lambda r, w: (layer, w[1] * nr + r, 0)))
    return pl.pallas_call(
        body, name=name, grid_spec=grid_spec, out_shape=jax.ShapeDtypeStruct((depth, 2 * hr, C), F32),
        input_output_aliases=aliases, compiler_params=_params(("parallel",)),
    )(*args)


def _rs_join_halves(reduced):
    n = len(reduced)

    def body(*refs):
        o_refs = refs[n:2 * n]
        send_sems, recv_sems = refs[2 * n:]
        x, y, c = _position()
        sent = []
        for wi in range(n):
            hr = reduced[wi].shape[1] // 2
            mine = _half_rows(o_refs[wi], c, hr)
            cp = pltpu.make_async_remote_copy(
                src_ref=mine, dst_ref=mine, send_sem=send_sems.at[wi], recv_sem=recv_sems.at[wi],
                device_id=(x, y, 1 - c), device_id_type=MESH)
            cp.start()
            sent.append(cp)
        for wi in range(n):
            hr = reduced[wi].shape[1] // 2
            theirs = _half_rows(o_refs[wi], 1 - c, hr)
            pltpu.make_async_remote_copy(
                src_ref=theirs, dst_ref=theirs, send_sem=send_sems.at[wi], recv_sem=recv_sems.at[wi],
                device_id=(x, y, c), device_id_type=MESH).wait_recv()
        for cp in sent:
            cp.wait_send()

    return pl.pallas_call(
        body, name="grads_join_halves", out_shape=[jax.ShapeDtypeStruct(r.shape, r.dtype) for r in reduced],
        in_specs=[ANY] * n, out_specs=[ANY] * n, input_output_aliases={i: i for i in range(n)},
        scratch_shapes=[pltpu.SemaphoreType.DMA((n,)), pltpu.SemaphoreType.DMA((n,))],
    )(*reduced)


def _small_allreduce(vec):
    rows = vec.shape[0]

    def body(v_ref, o_ref, land, send_sems, recv_sems):
        x, y, c = _position()
        me = 4 * x + 2 * y + c
        land[0] = v_ref[...]
        sent = []
        for k in range(1, N_DEV):
            to = (1 - x if k & 4 else x, 1 - y if k & 2 else y, 1 - c if k & 1 else c)
            cp = pltpu.make_async_remote_copy(
                src_ref=v_ref, dst_ref=land.at[k], send_sem=send_sems.at[k - 1], recv_sem=recv_sems.at[k - 1],
                device_id=to, device_id_type=MESH)
            cp.start()
            sent.append(cp)
        for cp in sent:
            cp.wait_recv()
        acc = land[me]
        for e in range(1, N_DEV):
            acc = acc + land[me ^ e]
        o_ref[...] = acc
        for cp in sent:
            cp.wait_send()

    vmem = pl.BlockSpec(memory_space=pltpu.VMEM)
    return pl.pallas_call(
        body, name="small_allreduce", out_shape=jax.ShapeDtypeStruct(vec.shape, F32),
        in_specs=[vmem], out_specs=vmem,
        scratch_shapes=[pltpu.VMEM((N_DEV, rows, LANES), F32), pltpu.SemaphoreType.DMA((N_DEV - 1,)),
                        pltpu.SemaphoreType.DMA((N_DEV - 1,))],
    )(vec)


def _adam_math(w, g, m, v):
    m = ADAM_B1 * m + (1.0 - ADAM_B1) * g
    v = ADAM_B2 * v + (1.0 - ADAM_B2) * jnp.square(g)
    m_hat = m / (1.0 - ADAM_B1 ** ADAM_STEP)
    v_hat = v / (1.0 - ADAM_B2 ** ADAM_STEP)
    delta = -ADAM_LR * (m_hat / (jnp.sqrt(v_hat) + ADAM_EPS) + ADAM_WD * w)
    return delta, m, v


def _adam(w, g, m, v, name):
    def body(w_ref, g_ref, m_ref, v_ref, d_ref, nm_ref, nv_ref):
        d_ref[...], nm_ref[...], nv_ref[...] = _adam_math(w_ref[...], g_ref[...], m_ref[...], v_ref[...])

    if w.ndim == 3:
        lyr, R, C = w.shape
        tr = _pick(R, (256, 352, 128))
        blk = pl.BlockSpec((None, tr, C), lambda l, r: (l, r, 0))
        grid, sem = (lyr, R // tr), ("parallel", "parallel")
    else:
        blk = pl.BlockSpec(w.shape, lambda: (0, 0))
        grid, sem = (), None
    return pl.pallas_call(
        body, name=name, grid=grid, in_specs=[blk] * 4, out_specs=[blk] * 3,
        out_shape=[jax.ShapeDtypeStruct(w.shape, F32)] * 3, compiler_params=_params(sem),
    )(w, g, m, v)


def _rows(a, pad_to=8):
    r = a.reshape(-1, LANES)
    extra = (-r.shape[0]) % pad_to
    return jnp.pad(r, ((0, extra), (0, 0))) if extra else r


def _pack(arrays):
    return jnp.concatenate([_rows(a) for a in arrays], axis=0)


def _unpack(slab, shapes):
    out, at = [], 0
    for shp in shapes:
        nrow = math.prod(shp) // LANES
        out.append(slab[at:at + nrow].reshape(shp))
        at += nrow + (-nrow) % 8
    return out


BIG = ("w_in", "w_out", "w_gate_t", "w_up_t", "w_down")
BIG_COL_SHARDED = (True, False, False, False, False)
TRANSPOSED = {"w_gate_t": "w_gate", "w_up_t": "w_up"}


def kernel(x, meta_tokens, mix_norm_g, w_in, conv_dw_w, conv_dw_b, conv_ln_g, conv_ln_b, w_out, ffn_norm_g, w_gate, w_up, w_down, final_norm_g, loss_target, m_meta_tokens, m_mix_norm_g, m_w_in, m_conv_dw_w, m_conv_dw_b, m_conv_ln_g, m_conv_ln_b, m_w_out, m_ffn_norm_g, m_w_gate, m_w_up, m_w_down, m_final_norm_g, v_meta_tokens, v_mix_norm_g, v_w_in, v_conv_dw_w, v_conv_dw_b, v_conv_ln_g, v_conv_ln_b, v_w_out, v_ffn_norm_g, v_w_gate, v_w_up, v_w_down, v_final_norm_g):
    n_meta, seq = meta_tokens.shape[0], x.shape[1]
    D = x.shape[2]
    depth, taps, c_shard = conv_dw_w.shape
    C = conv_dw_b.shape[-1]
    chip = (2 * lax.axis_index("x") + lax.axis_index("y")).astype(jnp.int32)
    core = lax.axis_index("c").astype(jnp.int32).reshape(1)
    chip1 = chip.reshape(1)
    where = jnp.concatenate([chip1, core])
    big_w = dict(w_in=w_in, w_out=w_out, w_gate=w_gate, w_up=w_up, w_down=w_down)
    big_m = dict(w_in=m_w_in, w_out=m_w_out, w_gate=m_w_gate, w_up=m_w_up, w_down=m_w_down)
    big_v = dict(w_in=v_w_in, w_out=v_w_out, w_gate=v_w_gate, w_up=v_w_up, w_down=v_w_down)

    small_shard = _pack([conv_dw_w, meta_tokens])[None]
    to_send = [jnp.swapaxes(big_w[TRANSPOSED[k]], 1, 2) if k in TRANSPOSED else big_w[k] for k in BIG]
    placed = [_place_shard(w, chip1, col, BF16, f"place_{k}") for k, w, col in zip(BIG, to_send, BIG_COL_SHARDED)]
    placed.append(_place_shard(small_shard, chip1, False, F32, "place_small"))
    gathered = _gather_weights(placed, [w.shape for w in to_send] + [small_shard.shape],
                               BIG_COL_SHARDED + (False,))
    wts = dict(zip(BIG, gathered[:-1]))
    rows_shard = small_shard.shape[1]
    dw_full, meta_full = [], []
    for j in range(N_CHIPS):
        dwj, mj = _unpack(gathered[-1][0, j * rows_shard:(j + 1) * rows_shard],
                          [conv_dw_w.shape, meta_tokens.shape])
        dw_full.append(dwj)
        meta_full.append(mj)
    dw_w_full = jnp.concatenate(dw_full, axis=2)
    meta = jnp.concatenate(meta_full, axis=1)

    L = n_meta + seq
    Lp = -(-L // QUERY_BLOCK) * QUERY_BLOCK
    h0 = jnp.concatenate([meta, x[0], jnp.zeros((Lp - L, D), F32)], axis=0)
    target = jnp.pad(loss_target[0], ((n_meta, Lp - L), (0, 0)))
    loss, dh0, grads = _local_step(h0, target, n_meta, seq, (mix_norm_g, ffn_norm_g),
                                   (dw_w_full, conv_dw_b, conv_ln_g, conv_ln_b), wts, final_norm_g)
    loss = lax.psum(loss[0, 0], ("x", "y", "c"))
    grad_x = dh0[n_meta:L][None]

    flat = [grads[k][l] for k in BIG for l in range(depth)]
    landed = _rs_to_sibling(flat)
    parts = [_chip_sum(g, la, core, f"chip_sum_{i}") for i, (g, la) in enumerate(zip(flat, landed))]
    across = _rs_across_chips(parts)
    reduced = []
    for wi in range(len(BIG)):
        arr = None
        for l in range(depth):
            i = wi * depth + l
            arr = _sum_chips(parts[i], across[i], where, l, depth, arr, f"sum_chips_{i}")
        reduced.append(arr)
    big_g = dict(zip(BIG, _rs_join_halves(reduced)))

    small_names = ("mix_g", "ffn_g", "dw_b", "ln_g", "ln_b", "dw_w")
    small = [jnp.stack(grads[k]) for k in small_names] + [grads["final_g"], dh0[:n_meta]]
    small_shapes = [a.shape for a in small]
    g_mix, g_ffn, g_dwb, g_lng, g_lnb, g_dww, g_final, g_meta = _unpack(_small_allreduce(_pack(small)), small_shapes)
    g_dww = lax.dynamic_slice_in_dim(g_dww, chip * c_shard, c_shard, axis=2)
    g_meta = lax.dynamic_slice_in_dim(g_meta, chip * meta_tokens.shape[1], meta_tokens.shape[1], axis=1)

    out_g, out_d, out_m, out_v = {}, {}, {}, {}
    for kk in BIG:
        k = TRANSPOSED.get(kk, kk)
        out_g[k] = jnp.swapaxes(big_g[kk], 1, 2) if kk in TRANSPOSED else big_g[kk]
        out_d[k], out_m[k], out_v[k] = _adam(big_w[k], out_g[k], big_m[k], big_v[k], f"adam_{k}")
    small_order = ("meta_tokens", "mix_norm_g", "conv_dw_w", "conv_dw_b", "conv_ln_g", "conv_ln_b",
                   "ffn_norm_g", "final_norm_g")
    sw = dict(meta_tokens=meta_tokens, mix_norm_g=mix_norm_g, conv_dw_w=conv_dw_w, conv_dw_b=conv_dw_b,
              conv_ln_g=conv_ln_g, conv_ln_b=conv_ln_b, ffn_norm_g=ffn_norm_g, final_norm_g=final_norm_g)
    sm = dict(meta_tokens=m_meta_tokens, mix_norm_g=m_mix_norm_g, conv_dw_w=m_conv_dw_w, conv_dw_b=m_conv_dw_b,
              conv_ln_g=m_conv_ln_g, conv_ln_b=m_conv_ln_b, ffn_norm_g=m_ffn_norm_g, final_norm_g=m_final_norm_g)
    sv = dict(meta_tokens=v_meta_tokens, mix_norm_g=v_mix_norm_g, conv_dw_w=v_conv_dw_w, conv_dw_b=v_conv_dw_b,
              conv_ln_g=v_conv_ln_g, conv_ln_b=v_conv_ln_b, ffn_norm_g=v_ffn_norm_g, final_norm_g=v_final_norm_g)
    sg = dict(meta_tokens=g_meta, mix_norm_g=g_mix, conv_dw_w=g_dww, conv_dw_b=g_dwb, conv_ln_g=g_lng,
              conv_ln_b=g_lnb, ffn_norm_g=g_ffn, final_norm_g=g_final)
    def slab(d):
        return _pack([d[k] for k in small_order])
    shapes = [sw[k].shape for k in small_order]
    deltas = _adam(slab(sw), slab(sg), slab(sm), slab(sv), "adam_small")
    for res, dst in zip(deltas, (out_d, out_m, out_v)):
        dst.update(zip(small_order, _unpack(res, shapes)))
    out_g.update(sg)

    order = ("meta_tokens", "mix_norm_g", "w_in", "conv_dw_w", "conv_dw_b", "conv_ln_g", "conv_ln_b", "w_out",
             "ffn_norm_g", "w_gate", "w_up", "w_down", "final_norm_g")
    return (loss, grad_x, *[out_g[k] for k in order], *[out_d[k] for k in order],
            *[out_m[k] for k in order], *[out_v[k] for k in order])
```

```python
import functools
import math

import jax
import jax.numpy as jnp
from jax import lax
from jax.experimental import pallas as pl
from jax.experimental.pallas import tpu as pltpu

F32 = jnp.float32
BF16 = jnp.bfloat16
MESH = pl.DeviceIdType.MESH

EPS = 1e-6
QUERY_BLOCK = 128
LANES = 128
HEAD_DIM = 64
LOG_STICK_FLOOR = -40.0
CONV_HALO = 32
N_CHIPS = 4
N_DEV = 8
VMEM_LIMIT = 56 * 1024 * 1024

ADAM_LR = 0.001
ADAM_B1 = 0.9
ADAM_B2 = 0.999
ADAM_EPS = 1e-08
ADAM_WD = 0.01
ADAM_STEP = 10


def _pick(n, prefs):
    for p in prefs:
        if n % p == 0:
            return p
    return n


def _params(sem=None):
    return pltpu.CompilerParams(dimension_semantics=sem, vmem_limit_bytes=VMEM_LIMIT)


def _sigmoid(x):
    return 1.0 / (1.0 + jnp.exp(-x))


def _rmsnorm_fwd(h, g, name):
    L, D = h.shape
    T = _pick(L, (384, 128))

    def body(h_ref, g_ref, o_ref):
        x = h_ref[...]
        r = lax.rsqrt(jnp.mean(x * x, axis=-1, keepdims=True) + EPS)
        o_ref[...] = (x * r * g_ref[...]).astype(o_ref.dtype)

    return pl.pallas_call(
        body, name=name, grid=(L // T,),
        in_specs=[pl.BlockSpec((T, D), lambda i: (i, 0)), pl.BlockSpec((1, D), lambda i: (0, 0))],
        out_specs=pl.BlockSpec((T, D), lambda i: (i, 0)),
        out_shape=jax.ShapeDtypeStruct((L, D), BF16),
        compiler_params=_params(("parallel",)),
    )(h, g)


def _rmsnorm_bwd(h, g, dy, dh_in, name):
    L, D = h.shape
    T = _pick(L, (384, 128))

    def body(h_ref, g_ref, dy_ref, dhin_ref, dh_ref, dg_ref):
        x = h_ref[...]
        dyv = dy_ref[...]
        r = lax.rsqrt(jnp.mean(x * x, axis=-1, keepdims=True) + EPS)
        xh = x * r
        dxh = dyv * g_ref[...]
        dh_ref[...] = dhin_ref[...] + r * (dxh - xh * jnp.mean(dxh * xh, axis=-1, keepdims=True))

        @pl.when(pl.program_id(0) == 0)
        def _():
            dg_ref[...] = jnp.zeros_like(dg_ref)

        dg_ref[...] += jnp.sum(dyv * xh, axis=0, keepdims=True)

    row = pl.BlockSpec((T, D), lambda i: (i, 0))
    vec = pl.BlockSpec((1, D), lambda i: (0, 0))
    return pl.pallas_call(
        body, name=name, grid=(L // T,),
        in_specs=[row, vec, row, row], out_specs=[row, vec],
        out_shape=[jax.ShapeDtypeStruct((L, D), F32), jax.ShapeDtypeStruct((1, D), F32)],
        compiler_params=_params(("arbitrary",)),
    )(h, g, dy, dh_in)


def _loss_head(h, g, target, n_meta, seq, name):
    L, D = h.shape
    T = _pick(L, (384, 128))

    def body(h_ref, g_ref, t_ref, loss_ref, dh_ref, dg_ref):
        i = pl.program_id(0)
        x = h_ref[...]
        gv = g_ref[...]
        r = lax.rsqrt(jnp.mean(x * x, axis=-1, keepdims=True) + EPS)
        xh = x * r
        y = xh * gv
        rows = i * T + lax.broadcasted_iota(jnp.int32, (T, 1), 0)
        live = (rows >= n_meta) & (rows < n_meta + seq)
        diff = jnp.where(live, y - t_ref[...], 0.0)
        dyv = diff / D
        dxh = dyv * gv
        dh_ref[...] = r * (dxh - xh * jnp.mean(dxh * xh, axis=-1, keepdims=True))

        @pl.when(i == 0)
        def _():
            dg_ref[...] = jnp.zeros_like(dg_ref)
            loss_ref[...] = jnp.zeros_like(loss_ref)

        dg_ref[...] += jnp.sum(dyv * xh, axis=0, keepdims=True)
        per_row = jnp.mean(diff * diff, axis=-1, keepdims=True)
        loss_ref[...] += 0.5 * jnp.sum(per_row, axis=0, keepdims=True)

    row = pl.BlockSpec((T, D), lambda i: (i, 0))
    vec = pl.BlockSpec((1, D), lambda i: (0, 0))
    one = pl.BlockSpec((1, 1), lambda i: (0, 0))
    return pl.pallas_call(
        body, name=name, grid=(L // T,),
        in_specs=[row, vec, row], out_specs=[one, row, vec],
        out_shape=[jax.ShapeDtypeStruct((1, 1), F32), jax.ShapeDtypeStruct((L, D), F32),
                   jax.ShapeDtypeStruct((1, D), F32)],
        compiler_params=_params(("arbitrary",)),
    )(h, g, target)


def _swiglu_fwd(g, u, name):
    L, F = g.shape
    T = _pick(L, (192, 128))

    def body(g_ref, u_ref, o_ref):
        gv = g_ref[...]
        o_ref[...] = (gv * _sigmoid(gv) * u_ref[...]).astype(o_ref.dtype)

    row = pl.BlockSpec((T, F), lambda i: (i, 0))
    return pl.pallas_call(
        body, name=name, grid=(L // T,), in_specs=[row, row], out_specs=row,
        out_shape=jax.ShapeDtypeStruct((L, F), BF16), compiler_params=_params(("parallel",)),
    )(g, u)


def _swiglu_bwd(g, u, dact, name):
    L, F = g.shape
    T = _pick(L, (192, 128))

    def body(g_ref, u_ref, d_ref, dg_ref, du_ref):
        gv = g_ref[...]
        dv = d_ref[...]
        s = _sigmoid(gv)
        du_ref[...] = (dv * (gv * s)).astype(du_ref.dtype)
        dg_ref[...] = (dv * u_ref[...] * (s * (1.0 + gv * (1.0 - s)))).astype(dg_ref.dtype)

    row = pl.BlockSpec((T, F), lambda i: (i, 0))
    return pl.pallas_call(
        body, name=name, grid=(L // T,), in_specs=[row, row, row], out_specs=[row, row],
        out_shape=[jax.ShapeDtypeStruct((L, F), BF16), jax.ShapeDtypeStruct((L, F), BF16)],
        compiler_params=_params(("parallel",)),
    )(g, u, dact)


def _mm_nn(pairs, out_dtype, name, residual=None, cols=None):
    M = pairs[0][0].shape[0]
    col0, N = cols if cols is not None else (0, pairs[0][1].shape[-1])
    tm = _pick(M, (1056, 384, 128))
    tn = _pick(math.gcd(N, col0) if col0 else N, (640, 512, 256, 128))
    jb = col0 // tn
    n = len(pairs)

    def body(*refs):
        a_refs, w_refs = refs[:n], refs[n:2 * n]
        o_ref = refs[-1]
        acc = None
        for a_ref, w_ref in zip(a_refs, w_refs):
            d = jnp.dot(a_ref[...].astype(BF16), w_ref[...], preferred_element_type=F32)
            acc = d if acc is None else acc + d
        if residual is not None:
            acc = acc + refs[2 * n][...]
        o_ref[...] = acc.astype(o_ref.dtype)

    in_specs = [pl.BlockSpec((tm, a.shape[1]), lambda i, j: (i, 0)) for a, _, _, _ in pairs]
    for a, _, layer, kblk in pairs:
        in_specs.append(pl.BlockSpec((None, a.shape[1], tn), functools.partial(lambda i, j, l, kb: (l, kb, j + jb), l=layer, kb=kblk)))
    args = [p[0] for p in pairs] + [p[1] for p in pairs]
    if residual is not None:
        in_specs.append(pl.BlockSpec((tm, tn), lambda i, j: (i, j)))
        args.append(residual)
    return pl.pallas_call(
        body, name=name, grid=(M // tm, N // tn), in_specs=in_specs,
        out_specs=pl.BlockSpec((tm, tn), lambda i, j: (i, j)),
        out_shape=jax.ShapeDtypeStruct((M, N), out_dtype),
        compiler_params=_params(("parallel", "parallel")),
    )(*args)


def _mm_nt(pairs, out_dtype, name):
    M = pairs[0][0].shape[0]
    K = pairs[0][1].shape[1]
    tm = _pick(M, (1056, 384, 128))
    tk = _pick(K, (512, 1408, 256, 128))
    n = len(pairs)

    def body(*refs):
        d_refs, w_refs = refs[:n], refs[n:2 * n]
        o_ref = refs[-1]
        acc = None
        for d_ref, w_ref in zip(d_refs, w_refs):
            d = lax.dot_general(d_ref[...].astype(BF16), w_ref[...], (((1,), (1,)), ((), ())),
                                preferred_element_type=F32)
            acc = d if acc is None else acc + d
        o_ref[...] = acc.astype(o_ref.dtype)

    in_specs = [pl.BlockSpec((tm, d.shape[1]), lambda i, j: (i, 0)) for d, _, _, _ in pairs]
    for d, _, layer, cblk in pairs:
        in_specs.append(pl.BlockSpec((None, tk, d.shape[1]), functools.partial(lambda i, j, l, cb: (l, j, cb), l=layer, cb=cblk)))
    args = [p[0] for p in pairs] + [p[1] for p in pairs]
    return pl.pallas_call(
        body, name=name, grid=(M // tm, K // tk), in_specs=in_specs,
        out_specs=pl.BlockSpec((tm, tk), lambda i, j: (i, j)),
        out_shape=jax.ShapeDtypeStruct((M, K), out_dtype),
        compiler_params=_params(("parallel", "parallel")),
    )(*args)


def _mm_tn(a, b, name, col_sharded):
    M, K = a.shape
    N = b.shape[1]
    tm = _pick(M, (1056, 384, 128))
    tk = _pick(K, (512, 1408, 256, 128))
    tn = N // N_CHIPS if col_sharded else _pick(N, (512, 128))

    def body(a_ref, b_ref, o_ref):
        @pl.when(pl.program_id(2) == 0)
        def _():
            o_ref[...] = jnp.zeros_like(o_ref)

        o_ref[...] += lax.dot_general(a_ref[...].astype(BF16), b_ref[...].astype(BF16),
                                      (((0,), (0,)), ((), ())), preferred_element_type=F32)

    if col_sharded:
        out_shape = jax.ShapeDtypeStruct((N_CHIPS, K, tn), F32)
        out_spec = pl.BlockSpec((None, tk, tn), lambda k, j, m: (j, k, 0))
    else:
        out_shape = jax.ShapeDtypeStruct((K, N), F32)
        out_spec = pl.BlockSpec((tk, tn), lambda k, j, m: (k, j))
    out = pl.pallas_call(
        body, name=name, grid=(K // tk, N // tn, M // tm),
        in_specs=[pl.BlockSpec((tm, tk), lambda k, j, m: (m, k)), pl.BlockSpec((tm, tn), lambda k, j, m: (m, j))],
        out_specs=out_spec, out_shape=out_shape,
        compiler_params=_params(("parallel", "parallel", "arbitrary")),
    )(a, b)
    return out if col_sharded else out.reshape(N_CHIPS, K // N_CHIPS, N)


def _stack_heads(x, scale=None):
    lane = lax.broadcasted_iota(jnp.int32, x.shape, 1)
    zero = jnp.zeros_like(x)
    lo = jnp.where(lane < HEAD_DIM, x, zero)
    hi = jnp.where(lane < HEAD_DIM, zero, x)
    out = jnp.concatenate([lo, hi], axis=0)
    return out if scale is None else out * scale


def _unstack_heads(x2):
    qb = x2.shape[0] // 2
    lane = lax.broadcasted_iota(jnp.int32, (qb, LANES), 1)
    return jnp.where(lane < HEAD_DIM, x2[:qb], x2[qb:])


def _tri_and_ones(strict):
    r = lax.broadcasted_iota(jnp.int32, (QUERY_BLOCK, 2 * QUERY_BLOCK), 0)
    c = lax.broadcasted_iota(jnp.int32, (QUERY_BLOCK, 2 * QUERY_BLOCK), 1)
    tri = (r > c) if strict else (r >= c)
    return jnp.where(tri | (c >= QUERY_BLOCK), 1.0, 0.0).astype(BF16)


def _dot2(x, m):
    xh = x.astype(BF16)
    xl = (x - xh.astype(F32)).astype(BF16)
    return jnp.dot(xh, m, preferred_element_type=F32) + jnp.dot(xl, m, preferred_element_type=F32)


def _causal_valid():
    r = lax.broadcasted_iota(jnp.int32, (2 * QUERY_BLOCK, QUERY_BLOCK), 0) & (QUERY_BLOCK - 1)
    c = lax.broadcasted_iota(jnp.int32, (2 * QUERY_BLOCK, QUERY_BLOCK), 1)
    return c < r


def _sweep_older(i, step, carry_ref, first):
    def cond(state):
        n, live = state
        return jnp.logical_and(n < i, live)

    def older(state):
        n, _ = state
        step(i - 1 - n, False)
        return n + 1, jnp.max(carry_ref[...]) > LOG_STICK_FLOOR

    lax.while_loop(cond, older, (first, jnp.max(carry_ref[...]) > LOG_STICK_FLOOR))


class _CommJob:
    def __init__(self, inputs, out_shapes, aliases, n_sems, start, finish):
        self.inputs, self.out_shapes, self.aliases, self.n_sems = list(inputs), list(out_shapes), aliases, n_sems
        self.start, self.finish = start, finish


def _call_with_job(core_body, job, *, name, grid, in_specs, out_specs, out_shape, scratch_shapes, args):
    sem = ("arbitrary",) * len(grid)
    if job is None:
        res = pl.pallas_call(core_body, name=name, grid=grid, in_specs=in_specs, out_specs=out_specs,
                             out_shape=out_shape, scratch_shapes=scratch_shapes, compiler_params=_params(sem))(*args)
        return list(res), []
    n_in, n_out, n_scr = len(in_specs), len(out_specs), len(scratch_shapes)
    m_in, m_out = len(job.inputs), len(job.out_shapes)

    def body(*refs):
        at = 0
        parts = []
        for count in (n_in, m_in, n_out, m_out, n_scr, 2):
            parts.append(refs[at:at + count])
            at += count
        ins, job_in, outs, job_outs, scratch, (send_sems, recv_sems) = parts
        first = functools.reduce(jnp.logical_and, [pl.program_id(a) == 0 for a in range(len(grid))])
        last = functools.reduce(jnp.logical_and, [pl.program_id(a) == grid[a] - 1 for a in range(len(grid))])

        @pl.when(first)
        def _():
            job.start(job_in, job_outs, send_sems, recv_sems)

        core_body(*ins, *outs, *scratch)

        @pl.when(last)
        def _():
            job.finish(job_in, job_outs, send_sems, recv_sems)

    res = pl.pallas_call(
        body, name=name, grid=grid, in_specs=list(in_specs) + [ANY] * m_in, out_specs=list(out_specs) + [ANY] * m_out,
        out_shape=list(out_shape) + job.out_shapes,
        input_output_aliases={n_in + a: n_out + b for a, b in job.aliases.items()},
        scratch_shapes=list(scratch_shapes) + [pltpu.SemaphoreType.DMA((job.n_sems,)), pltpu.SemaphoreType.DMA((job.n_sems,))],
        compiler_params=_params(sem),
    )(*args, *job.inputs)
    return list(res[:n_out]), list(res[n_out:])


def _run_job(job, name):
    m_in, m_out = len(job.inputs), len(job.out_shapes)

    def body(*refs):
        job_in, job_outs = refs[:m_in], refs[m_in:m_in + m_out]
        send_sems, recv_sems = refs[m_in + m_out:]
        job.start(job_in, job_outs, send_sems, recv_sems)
        job.finish(job_in, job_outs, send_sems, recv_sems)

    return list(pl.pallas_call(
        body, name=name, in_specs=[ANY] * m_in, out_specs=[ANY] * m_out, out_shape=job.out_shapes,
        input_output_aliases=dict(job.aliases),
        scratch_shapes=[pltpu.SemaphoreType.DMA((job.n_sems,)), pltpu.SemaphoreType.DMA((job.n_sems,))],
    )(*job.inputs))


def _attn_fwd(qkv, sb_width, name, job=None):
    L = qkv.shape[0]
    QB = QUERY_BLOCK
    nb = L // QB
    n_pairs = sb_width // LANES
    scale = 1.0 / math.sqrt(HEAD_DIM)

    def body(q_ref, k_ref, v_ref, o_ref, acc_ref, carry_ref):
        i = pl.program_id(1)
        q2 = _stack_heads(q_ref[...], scale)
        txo = _tri_and_ones(True)
        valid = _causal_valid()
        acc_ref[...] = jnp.zeros_like(acc_ref)
        carry_ref[...] = jnp.zeros_like(carry_ref)

        def front(j, diag):
            start = pl.multiple_of(j * QB, QB)
            kb = k_ref[pl.ds(start, QB), :]
            vb = v_ref[pl.ds(start, QB), :]
            z = lax.dot_general(q2, kb, (((1,), (1,)), ((), ())), preferred_element_type=F32)
            sp = jnp.log(1.0 + jnp.exp(-jnp.abs(z)))
            a = jnp.minimum(z, 0.0) - sp
            b = jnp.minimum(-z, 0.0) - sp
            if diag:
                b = jnp.where(valid, b, 0.0)
            return a, _dot2(b, txo), vb

        def back(a, res, vb, diag):
            w = jnp.exp(a + res[:, :QB] + carry_ref[...])
            if diag:
                w = jnp.where(valid, w, 0.0)
            carry_ref[...] += res[:, QB:]
            acc_ref[...] += _dot2(w, vb)

        def step(j, diag):
            back(*front(j, diag), diag)

        @pl.when(i == 0)
        def _():
            step(0, True)

        @pl.when(i > 0)
        def _():
            f0 = front(i, True)
            f1 = front(i - 1, False)
            back(*f0, True)
            back(*f1, False)
            _sweep_older(i, step, carry_ref, 1)

        o_ref[...] = _unstack_heads(acc_ref[...])

    (out,), job_out = _call_with_job(
        body, job, name=name, grid=(n_pairs, nb),
        in_specs=[pl.BlockSpec((QB, LANES), lambda p, i: (i, p)),
                  pl.BlockSpec((L, LANES), lambda p, i: (0, n_pairs + p)),
                  pl.BlockSpec((L, LANES), lambda p, i: (0, 2 * n_pairs + p))],
        out_specs=[pl.BlockSpec((QB, LANES), lambda p, i: (i, p))],
        out_shape=[jax.ShapeDtypeStruct((L, sb_width), F32)],
        scratch_shapes=[pltpu.VMEM((2 * QB, LANES), F32), pltpu.VMEM((2 * QB, LANES), F32)],
        args=(qkv, qkv, qkv))
    return out, job_out


def _attn_bwd(qkv, o, dmixed, sb_width, name, job=None):
    L = qkv.shape[0]
    QB = QUERY_BLOCK
    nb = L // QB
    n_pairs = sb_width // LANES
    scale = 1.0 / math.sqrt(HEAD_DIM)

    def body(q_ref, k_ref, v_ref, o_ref, do_ref, dq_ref, dk_ref, dv_ref,
             dq_acc, dk_acc, dv_acc, ce_ref, cr_ref):
        i = pl.program_id(1)

        @pl.when(i == 0)
        def _():
            dk_acc[...] = jnp.zeros_like(dk_acc)
            dv_acc[...] = jnp.zeros_like(dv_acc)

        q2 = _stack_heads(q_ref[...], scale)
        do2 = _stack_heads(do_ref[...].astype(BF16))
        ov = o_ref[...]
        o2 = jnp.concatenate([ov, ov], axis=0)
        ones = jnp.ones((LANES, LANES), BF16)
        dtot = _dot2(do2.astype(F32) * o2, ones)
        txo = _tri_and_ones(True)
        tio = _tri_and_ones(False)
        valid = _causal_valid()
        dq_acc[...] = jnp.zeros_like(dq_acc)
        ce_ref[...] = jnp.zeros_like(ce_ref)
        cr_ref[...] = jnp.zeros_like(cr_ref)

        def front(j, diag):
            start = pl.multiple_of(j * QB, QB)
            kb = k_ref[pl.ds(start, QB), :]
            vb = v_ref[pl.ds(start, QB), :]
            z = lax.dot_general(q2, kb, (((1,), (1,)), ((), ())), preferred_element_type=F32)
            e = jnp.exp(-jnp.abs(z))
            sp = jnp.log(1.0 + e)
            a = jnp.minimum(z, 0.0) - sp
            b = jnp.minimum(-z, 0.0) - sp
            rinv = 1.0 / (1.0 + e)
            pos = z >= 0.0
            beta = jnp.where(pos, rinv, e * rinv)
            one_m_beta = jnp.where(pos, e * rinv, rinv)
            if diag:
                b = jnp.where(valid, b, 0.0)
            dw = lax.dot_general(do2, vb, (((1,), (1,)), ((), ())), preferred_element_type=F32)
            return start, kb, a, beta, one_m_beta, _dot2(b, txo), dw

        def back(start, kb, a, beta, one_m_beta, res, dw, diag):
            w = jnp.exp(a + res[:, :QB] + ce_ref[...])
            if diag:
                w = jnp.where(valid, w, 0.0)
            ce_ref[...] += res[:, QB:]
            g = w * dw
            res2 = _dot2(g, tio)
            rinc = res2[:, :QB] + cr_ref[...]
            cr_ref[...] += res2[:, QB:]
            dz = g * one_m_beta - beta * (dtot - rinc)
            if diag:
                dz = jnp.where(valid, dz, 0.0)
            dzb = dz.astype(BF16)
            dq_acc[...] += jnp.dot(dzb, kb, preferred_element_type=F32)
            dk_acc[pl.ds(start, QB), :] += lax.dot_general(
                dzb, q2, (((0,), (0,)), ((), ())), preferred_element_type=F32)
            dv_acc[pl.ds(start, QB), :] += lax.dot_general(
                w.astype(BF16), do2, (((0,), (0,)), ((), ())), preferred_element_type=F32)

        def step(j, diag):
            back(*front(j, diag), diag)

        @pl.when(i == 0)
        def _():
            step(0, True)

        @pl.when(i > 0)
        def _():
            f0 = front(i, True)
            f1 = front(i - 1, False)
            back(*f0, True)
            back(*f1, False)
            _sweep_older(i, step, ce_ref, 1)

        dq_ref[...] = (_unstack_heads(dq_acc[...]) * scale).astype(dq_ref.dtype)

        @pl.when(i == nb - 1)
        def _():
            dk_ref[...] = dk_acc[...].astype(dk_ref.dtype)
            dv_ref[...] = dv_acc[...].astype(dv_ref.dtype)

    blk = pl.BlockSpec((QB, LANES), lambda p, i: (i, p))
    col = pl.BlockSpec((L, LANES), lambda p, i: (0, p))
    return _call_with_job(
        body, job, name=name, grid=(n_pairs, nb),
        in_specs=[blk,
                  pl.BlockSpec((L, LANES), lambda p, i: (0, n_pairs + p)),
                  pl.BlockSpec((L, LANES), lambda p, i: (0, 2 * n_pairs + p)),
                  blk, blk],
        out_specs=[blk, col, col],
        out_shape=[jax.ShapeDtypeStruct((L, sb_width), BF16)] * 3,
        scratch_shapes=[pltpu.VMEM((2 * QB, LANES), F32), pltpu.VMEM((L, LANES), F32),
                        pltpu.VMEM((L, LANES), F32), pltpu.VMEM((2 * QB, LANES), F32),
                        pltpu.VMEM((2 * QB, LANES), F32)],
        args=(qkv, qkv, qkv, o, dmixed))


def _conv_tile(L):
    return _pick(L, (384, 128))


def _glu(x, C):
    return x[:, :C] * _sigmoid(x[:, C:])


CONV_CHUNK = 32
SHIFT_TAIL = 24


def _fill_shifted(src_ref, dst_ref):
    n = dst_ref.shape[1]
    for r in range(1, 8):
        dst_ref[r - 1] = src_ref[r:r + n, :]


def _rows_at(src_ref, shifted_ref, start, n):
    q, r = divmod(start, 8)
    if r == 0:
        return src_ref[start:start + n, :]
    return shifted_ref[r - 1, 8 * q:8 * q + n, :]


def _conv_fwd(cacg, dw_w, dw_b, ln_g, ln_b, name):
    L, C2 = cacg.shape
    C = C2 // 2
    T = _conv_tile(L)
    H = CONV_HALO
    K = dw_w.shape[0]
    CH = CONV_CHUNK

    def body(x_ref, prev_ref, w_ref, b_ref, g_ref, beta_ref, o_ref, u_ref, us_ref):
        i = pl.program_id(0)
        u_ref[0:H, :] = jnp.where(i > 0, _glu(prev_ref[...], C), 0.0)
        u_ref[H:, :] = _glu(x_ref[...], C)
        _fill_shifted(u_ref, us_ref)
        for c0 in range(0, T, CH):
            y = jnp.broadcast_to(b_ref[...], (CH, C))
            for k in range(K):
                y = y + w_ref[k:k + 1, :] * _rows_at(u_ref, us_ref, c0 + H - (K - 1) + k, CH)
            mu = jnp.mean(y, axis=-1, keepdims=True)
            yc = y - mu
            rstd = lax.rsqrt(jnp.mean(yc * yc, axis=-1, keepdims=True) + EPS)
            ln = yc * rstd * g_ref[...] + beta_ref[...]
            o_ref[c0:c0 + CH, :] = (ln * _sigmoid(ln)).astype(o_ref.dtype)

    vec = pl.BlockSpec((1, C), lambda i: (0, 0))
    return pl.pallas_call(
        body, name=name, grid=(L // T,),
        in_specs=[pl.BlockSpec((T, C2), lambda i: (i, 0)),
                  pl.BlockSpec((H, C2), lambda i: (jnp.maximum(i * (T // H) - 1, 0), 0)),
                  pl.BlockSpec((K, C), lambda i: (0, 0)), vec, vec, vec],
        out_specs=pl.BlockSpec((T, C), lambda i: (i, 0)),
        out_shape=jax.ShapeDtypeStruct((L, C), BF16),
        scratch_shapes=[pltpu.VMEM((T + H, C), F32), pltpu.VMEM((7, T + SHIFT_TAIL, C), F32)],
        compiler_params=_params(("parallel",)),
    )(cacg, cacg, dw_w, dw_b, ln_g, ln_b)


def _conv_bwd(cacg, dmixed, dw_w, dw_b, ln_g, ln_b, name):
    L, C2 = cacg.shape
    C = C2 // 2
    T = _conv_tile(L)
    H = CONV_HALO
    K = dw_w.shape[0]
    nt = L // T
    TE = T + H

    CH = CONV_CHUNK

    def body(x_ref, prev_ref, next_ref, d_ref, dnext_ref, w_ref, b_ref, g_ref, beta_ref,
             dca_ref, dcg_ref, dwt_ref, db_ref, dg_ref, dbeta_ref, u_ref, us_ref, dy_ref, dys_ref):
        i = pl.program_id(0)
        last = i == nt - 1

        @pl.when(i == 0)
        def _():
            dwt_ref[...] = jnp.zeros_like(dwt_ref)
            db_ref[...] = jnp.zeros_like(db_ref)
            dg_ref[...] = jnp.zeros_like(dg_ref)
            dbeta_ref[...] = jnp.zeros_like(dbeta_ref)

        u_ref[0:H, :] = jnp.where(i > 0, _glu(prev_ref[...], C), 0.0)
        u_ref[H:H + T, :] = _glu(x_ref[...], C)
        u_ref[H + T:, :] = _glu(next_ref[...], C)
        _fill_shifted(u_ref, us_ref)
        dg_acc = jnp.zeros((1, C), F32)
        dbeta_acc = jnp.zeros((1, C), F32)
        db_acc = jnp.zeros((1, C), F32)
        for c0 in range(0, TE, CH):
            y = jnp.broadcast_to(b_ref[...], (CH, C))
            for k in range(K):
                y = y + w_ref[k:k + 1, :] * _rows_at(u_ref, us_ref, c0 + H - (K - 1) + k, CH)
            mu = jnp.mean(y, axis=-1, keepdims=True)
            yc = y - mu
            rstd = lax.rsqrt(jnp.mean(yc * yc, axis=-1, keepdims=True) + EPS)
            yh = yc * rstd
            ln = yh * g_ref[...] + beta_ref[...]
            s = _sigmoid(ln)
            dout = d_ref[c0:c0 + CH, :] if c0 < T else jnp.where(last, 0.0, dnext_ref[c0 - T:c0 - T + CH, :])
            dln = dout * (s * (1.0 + ln * (1.0 - s)))
            dyh = dln * g_ref[...]
            dy = rstd * (dyh - jnp.mean(dyh, axis=-1, keepdims=True)
                         - yh * jnp.mean(dyh * yh, axis=-1, keepdims=True))
            dy_ref[c0:c0 + CH, :] = dy
            if c0 < T:
                dg_acc = dg_acc + jnp.sum(dln * yh, axis=0, keepdims=True)
                dbeta_acc = dbeta_acc + jnp.sum(dln, axis=0, keepdims=True)
                db_acc = db_acc + jnp.sum(dy, axis=0, keepdims=True)
        dg_ref[...] += dg_acc
        dbeta_ref[...] += dbeta_acc
        db_ref[...] += db_acc
        _fill_shifted(dy_ref, dys_ref)
        for k in range(K):
            dwt_ref[k:k + 1, :] += jnp.sum(
                dy_ref[0:T, :] * _rows_at(u_ref, us_ref, H - (K - 1) + k, T), axis=0, keepdims=True)
        for c0 in range(0, T, CH):
            du = jnp.zeros((CH, C), F32)
            for k in range(K):
                du = du + w_ref[k:k + 1, :] * _rows_at(dy_ref, dys_ref, c0 + (K - 1) - k, CH)
            x = x_ref[c0:c0 + CH, :]
            sg = _sigmoid(x[:, C:])
            dca_ref[c0:c0 + CH, :] = (du * sg).astype(dca_ref.dtype)
            dcg_ref[c0:c0 + CH, :] = (du * x[:, :C] * sg * (1.0 - sg)).astype(dcg_ref.dtype)

    nh = L // H
    vec = pl.BlockSpec((1, C), lambda i: (0, 0))
    row = pl.BlockSpec((T, C), lambda i: (i, 0))
    return pl.pallas_call(
        body, name=name, grid=(nt,),
        in_specs=[pl.BlockSpec((T, C2), lambda i: (i, 0)),
                  pl.BlockSpec((H, C2), lambda i: (jnp.maximum(i * (T // H) - 1, 0), 0)),
                  pl.BlockSpec((H, C2), lambda i: (jnp.minimum((i + 1) * (T // H), nh - 1), 0)),
                  pl.BlockSpec((T, C), lambda i: (i, 1)),
                  pl.BlockSpec((H, C), lambda i: (jnp.minimum((i + 1) * (T // H), nh - 1), 1)),
                  pl.BlockSpec((K, C), lambda i: (0, 0)), vec, vec, vec],
        out_specs=[row, row, pl.BlockSpec((H, C), lambda i: (0, 0)), vec, vec, vec],
        out_shape=[jax.ShapeDtypeStruct((L, C), BF16), jax.ShapeDtypeStruct((L, C), BF16),
                   jax.ShapeDtypeStruct((H, C), F32), jax.ShapeDtypeStruct((1, C), F32),
                   jax.ShapeDtypeStruct((1, C), F32), jax.ShapeDtypeStruct((1, C), F32)],
        scratch_shapes=[pltpu.VMEM((T + 2 * H, C), F32), pltpu.VMEM((7, TE + SHIFT_TAIL, C), F32),
                        pltpu.VMEM((TE, C), F32), pltpu.VMEM((7, T + SHIFT_TAIL, C), F32)],
        compiler_params=_params(("arbitrary",)),
    )(cacg, cacg, cacg, dmixed, dmixed, dw_w, dw_b, ln_g, ln_b)


def _local_step(h0, target, n_meta, seq, norms, conv_p, wts, final_g, gather_behind, reduce_layer):
    mix_g, ffn_g = norms
    dw_w, dw_b, ln_g, ln_b = conv_p
    depth = mix_g.shape[0]
    C = dw_b.shape[-1]
    sbw = (wts["w_in"][0].shape[-1] - 2 * C) // 3
    assert sbw == C, "the mixer halves must have equal width"
    row = lambda a, i: a[i][None, :]

    h = h0
    saved = []
    for i in range(depth):
        hn = _rmsnorm_fwd(h, row(mix_g, i), f"mix_norm_{i}")
        proj_qkv = _mm_nn([(hn, wts["w_in"][i], 0, 0)], BF16, f"in_qkv_{i}", cols=(0, 3 * sbw))
        cacg = _mm_nn([(hn, wts["w_in"][i], 0, 0)], F32, f"in_conv_{i}", cols=(3 * sbw, 2 * C))
        job, keys = gather_behind.get(i, (None, ()))
        attn, arrived = _attn_fwd(proj_qkv, sbw, f"attn_fwd_{i}", job)
        for (wname, wl), arr in zip(keys, arrived):
            wts[wname][wl] = arr
        conv = _conv_fwd(cacg, dw_w[i], row(dw_b, i), row(ln_g, i), row(ln_b, i), f"conv_fwd_{i}")
        h_mid = _mm_nn([(attn, wts["w_out"][i], 0, 0), (conv, wts["w_out"][i], 0, 1)], F32, f"out_proj_{i}",
                       residual=h)
        hn2 = _rmsnorm_fwd(h_mid, row(ffn_g, i), f"ffn_norm_{i}")
        g = _mm_nt([(hn2, wts["w_gate_t"][i], 0, 0)], F32, f"gate_{i}")
        u = _mm_nt([(hn2, wts["w_up_t"][i], 0, 0)], F32, f"up_{i}")
        act = _swiglu_fwd(g, u, f"swiglu_{i}")
        h_out = _mm_nn([(act, wts["w_down"][i], 0, 0)], F32, f"down_{i}", residual=h_mid)
        saved.append((h, hn, proj_qkv, cacg, attn, conv, h_mid, hn2, g, u, act))
        h = h_out

    loss, dh, d_final_g = _loss_head(h, final_g[None, :], target, n_meta, seq, "loss_head")

    grads = {k: [None] * depth for k in ("mix_g", "ffn_g", "dw_w", "dw_b", "ln_g", "ln_b")}
    behind = None
    for i in reversed(range(depth)):
        h_in, hn, proj_qkv, cacg, attn, conv, h_mid, hn2, g, u, act = saved[i]
        big = {}
        dact = _mm_nt([(dh, wts["w_down"][i], 0, 0)], F32, f"d_act_{i}")
        big["w_down"] = _mm_tn(act, dh, f"dw_down_{i}", col_sharded=False)
        dg, du = _swiglu_bwd(g, u, dact, f"swiglu_bwd_{i}")
        dhn2 = _mm_nn([(dg, wts["w_gate_t"][i], 0, 0), (du, wts["w_up_t"][i], 0, 0)], F32, f"d_hn2_{i}")
        big["w_gate_t"] = _mm_tn(dg, hn2, f"dw_gate_{i}", col_sharded=False)
        big["w_up_t"] = _mm_tn(du, hn2, f"dw_up_{i}", col_sharded=False)
        dh, d_ffn = _rmsnorm_bwd(h_mid, row(ffn_g, i), dhn2, dh, f"ffn_norm_bwd_{i}")
        dmixed = _mm_nt([(dh, wts["w_out"][i], 0, 0)], F32, f"d_mixed_{i}")
        mixed = jnp.concatenate([attn.astype(BF16), conv], axis=1)
        big["w_out"] = _mm_tn(mixed, dh, f"dw_out_{i}", col_sharded=False)
        job, sink = behind if behind is not None else (None, None)
        (dq, dk, dv), arrived = _attn_bwd(proj_qkv, attn, dmixed, sbw, f"attn_bwd_{i}", job)
        if sink is not None:
            sink(arrived)
        dca, dcg, d_dw, d_b, d_lg, d_lb = _conv_bwd(
            cacg, dmixed, dw_w[i], row(dw_b, i), row(ln_g, i), row(ln_b, i), f"conv_bwd_{i}")
        grads["dw_w"][i] = d_dw[:dw_w.shape[1]]
        grads["dw_b"][i], grads["ln_g"][i], grads["ln_b"][i] = d_b[0], d_lg[0], d_lb[0]
        dproj = jnp.concatenate([dq, dk, dv, dca, dcg], axis=1)
        dhn = _mm_nt([(dproj, wts["w_in"][i], 0, 0)], F32, f"d_hn_{i}")
        big["w_in"] = _mm_tn(hn, dproj, f"dw_in_{i}", col_sharded=True)
        dh, d_mix = _rmsnorm_bwd(h_in, row(mix_g, i), dhn, dh, f"mix_norm_bwd_{i}")
        grads["mix_g"][i], grads["ffn_g"][i] = d_mix[0], d_ffn[0]
        behind = reduce_layer(i, big)
    grads["final_g"] = d_final_g[0]
    return loss, dh, grads


ANY = pl.BlockSpec(memory_space=pl.ANY)


def _position():
    return lax.axis_index("x"), lax.axis_index("y"), lax.axis_index("c")


def _chip_at(x, y, k):
    return (1 - x if k & 2 else x), (1 - y if k & 1 else y)


def _half_rows(ref, half, rows, base=0):
    start = pl.multiple_of(base + half * rows, 8)
    lead = (slice(None),) * (len(ref.shape) - 2)
    return ref.at[(*lead, pl.ds(start, rows), slice(None))]


def _gather_job(fulls, shard_shapes, col_sharded):
    n = len(fulls)

    def tools(f_refs, send_sems, recv_sems):
        def block(wi, chip, half):
            _, R, C = shard_shapes[wi]
            if col_sharded[wi]:
                cols = pl.ds(pl.multiple_of(chip * C, LANES), C)
                return f_refs[wi].at[:, pl.ds(pl.multiple_of(half * (R // 2), 8), R // 2), cols]
            return _half_rows(f_refs[wi], half, R // 2, base=chip * R)

        def copy(wi, slot, blk, to):
            return pltpu.make_async_remote_copy(
                src_ref=blk, dst_ref=blk, send_sem=send_sems.at[6 * wi + slot],
                recv_sem=recv_sems.at[6 * wi + slot], device_id=to, device_id_type=MESH)

        return block, copy

    def start(_, f_refs, send_sems, recv_sems):
        block, copy = tools(f_refs, send_sems, recv_sems)
        x, y, c = _position()
        me = 2 * x + y
        for wi in range(n):
            for k in (1, 2, 3):
                copy(wi, k - 1, block(wi, me, c), (*_chip_at(x, y, k), c)).start()

    def finish(_, f_refs, send_sems, recv_sems):
        block, copy = tools(f_refs, send_sems, recv_sems)
        x, y, c = _position()
        me = 2 * x + y
        for wi in range(n):
            for k in (1, 2, 3):
                landed = block(wi, me ^ k, c)
                copy(wi, k - 1, landed, (x, y, c)).wait_recv()
                copy(wi, 2 + k, landed, (x, y, 1 - c)).start()
        for wi in range(n):
            for k in (1, 2, 3):
                copy(wi, 2 + k, block(wi, me ^ k, 1 - c), (x, y, c)).wait_recv()
        for wi in range(n):
            for k in (1, 2, 3):
                copy(wi, k - 1, block(wi, me, c), (x, y, c)).wait_send()
                copy(wi, 2 + k, block(wi, me ^ k, c), (x, y, c)).wait_send()

    return _CommJob(fulls, [jax.ShapeDtypeStruct(f.shape, f.dtype) for f in fulls], {i: i for i in range(n)},
                    6 * n, start, finish)


def _place_shard(w, layer, chip, col_sharded, dtype, name):
    _, R, C = w.shape
    tr = _pick(R, (256, 352, 128, 48))
    nr = R // tr

    def body(chip_ref, w_ref, o_ref):
        o_ref[...] = w_ref[...].astype(dtype)

    if col_sharded:
        shape = (1, R, N_CHIPS * C)
        out_spec = pl.BlockSpec((None, tr, C), lambda r, chip_ref: (0, r, chip_ref[0]))
    else:
        shape = (1, N_CHIPS * R, C)
        out_spec = pl.BlockSpec((None, tr, C), lambda r, chip_ref: (0, chip_ref[0] * nr + r, 0))
    grid_spec = pltpu.PrefetchScalarGridSpec(
        num_scalar_prefetch=1, grid=(nr,),
        in_specs=[pl.BlockSpec((None, tr, C), lambda r, chip_ref: (layer, r, 0))], out_specs=out_spec)
    return pl.pallas_call(
        body, name=name, grid_spec=grid_spec, out_shape=jax.ShapeDtypeStruct(shape, dtype),
        compiler_params=_params(("parallel",)),
    )(chip, w)


def _rs_to_sibling(grads, name):
    n = len(grads)
    outs = [jax.ShapeDtypeStruct((g.shape[0], g.shape[1] // 2, g.shape[2]), g.dtype) for g in grads]

    def body(*refs):
        g_refs, l_refs = refs[:n], refs[n:2 * n]
        send_sems, recv_sems = refs[2 * n:]
        x, y, c = _position()
        cps = []
        for wi in range(n):
            R = grads[wi].shape[1]
            cp = pltpu.make_async_remote_copy(
                src_ref=_half_rows(g_refs[wi], 1 - c, R // 2), dst_ref=l_refs[wi],
                send_sem=send_sems.at[wi], recv_sem=recv_sems.at[wi],
                device_id=(x, y, 1 - c), device_id_type=MESH)
            cp.start()
            cps.append(cp)
        for cp in cps:
            cp.wait()

    return pl.pallas_call(
        body, name=name, out_shape=outs, in_specs=[ANY] * n, out_specs=[ANY] * n,
        scratch_shapes=[pltpu.SemaphoreType.DMA((n,)), pltpu.SemaphoreType.DMA((n,))],
    )(*grads)


def _chip_sum(g, landed, core, name):
    _, R, C = g.shape
    hr = R // 2
    tr = _pick(hr, (256, 352, 128))
    nr = hr // tr

    def body(c_ref, g_ref, l_ref, o_ref):
        o_ref[...] = (g_ref[...] + l_ref[...]).astype(BF16)

    grid_spec = pltpu.PrefetchScalarGridSpec(
        num_scalar_prefetch=1, grid=(N_CHIPS, nr),
        in_specs=[pl.BlockSpec((None, tr, C), lambda j, r, c_ref: (j, c_ref[0] * nr + r, 0)),
                  pl.BlockSpec((None, tr, C), lambda j, r, c_ref: (j, r, 0))],
        out_specs=pl.BlockSpec((None, tr, C), lambda j, r, c_ref: (j, r, 0)))
    return pl.pallas_call(
        body, name=name, grid_spec=grid_spec, out_shape=jax.ShapeDtypeStruct((N_CHIPS, hr, C), BF16),
        compiler_params=_params(("parallel", "parallel")),
    )(core, g, landed)


def _across_job(parts):
    n = len(parts)

    def copy(p_refs, l_refs, send_sems, recv_sems, wi, k, to):
        x, y, _ = _position()
        me = 2 * x + y
        return pltpu.make_async_remote_copy(
            src_ref=p_refs[wi].at[me ^ k], dst_ref=l_refs[wi].at[me],
            send_sem=send_sems.at[3 * wi + k - 1], recv_sem=recv_sems.at[3 * wi + k - 1],
            device_id=to, device_id_type=MESH)

    def start(p_refs, l_refs, send_sems, recv_sems):
        x, y, c = _position()
        for wi in range(n):
            for k in (1, 2, 3):
                copy(p_refs, l_refs, send_sems, recv_sems, wi, k, (*_chip_at(x, y, k), c)).start()

    def finish(p_refs, l_refs, send_sems, recv_sems):
        x, y, c = _position()
        me = 2 * x + y
        for wi in range(n):
            for k in (1, 2, 3):
                slot = l_refs[wi].at[me ^ k]
                pltpu.make_async_remote_copy(
                    src_ref=slot, dst_ref=slot, send_sem=send_sems.at[3 * wi + k - 1],
                    recv_sem=recv_sems.at[3 * wi + k - 1], device_id=(x, y, c), device_id_type=MESH).wait_recv()
        for wi in range(n):
            for k in (1, 2, 3):
                copy(p_refs, l_refs, send_sems, recv_sems, wi, k, (x, y, c)).wait_send()

    return _CommJob(parts, [jax.ShapeDtypeStruct(p.shape, p.dtype) for p in parts], {}, 3 * n, start, finish)


def _sum_chips(parts, landed, where, layer, depth, prev, name):
    _, hr, C = landed.shape
    tr = _pick(hr, (256, 352, 128))
    nr = hr // tr

    def body(*refs):
        own_ref, slots, o_ref = refs[1], refs[2:2 + N_CHIPS], refs[-1]
        chip = refs[0][0]
        total = None
        for q in range(N_CHIPS):
            term = jnp.where(chip == q, own_ref[...], slots[q][...]).astype(F32)
            total = term if total is None else total + term
        o_ref[...] = total

    def slot_spec(q):
        return pl.BlockSpec((None, tr, C), lambda r, w: (jnp.where(w[0] == q, (q + 1) % N_CHIPS, q), r, 0))

    in_specs = [pl.BlockSpec((None, tr, C), lambda r, w: (w[0], r, 0))] + [slot_spec(q) for q in range(N_CHIPS)]
    args = [where, parts] + [landed] * N_CHIPS
    aliases = {}
    if prev is not None:
        in_specs.append(ANY)
        args.append(prev)
        aliases = {len(args) - 1: 0}
    grid_spec = pltpu.PrefetchScalarGridSpec(
        num_scalar_prefetch=1, grid=(nr,), in_specs=in_specs,
        out_specs=pl.BlockSpec((None, tr, C), lambda r, w: (layer, w[1] * nr + r, 0)))
    return pl.pallas_call(
        body, name=name, grid_spec=grid_spec, out_shape=jax.ShapeDtypeStruct((depth, 2 * hr, C), F32),
        input_output_aliases=aliases, compiler_params=_params(("parallel",)),
    )(*args)


def _rs_join_halves(reduced):
    n = len(reduced)

    def body(*refs):
        o_refs = refs[n:2 * n]
        send_sems, recv_sems = refs[2 * n:]
        x, y, c = _position()
        sent = []
        for wi in range(n):
            hr = reduced[wi].shape[1] // 2
            mine = _half_rows(o_refs[wi], c, hr)
            cp = pltpu.make_async_remote_copy(
                src_ref=mine, dst_ref=mine, send_sem=send_sems.at[wi], recv_sem=recv_sems.at[wi],
                device_id=(x, y, 1 - c), device_id_type=MESH)
            cp.start()
            sent.append(cp)
        for wi in range(n):
            hr = reduced[wi].shape[1] // 2
            theirs = _half_rows(o_refs[wi], 1 - c, hr)
            pltpu.make_async_remote_copy(
                src_ref=theirs, dst_ref=theirs, send_sem=send_sems.at[wi], recv_sem=recv_sems.at[wi],
                device_id=(x, y, c), device_id_type=MESH).wait_recv()
        for cp in sent:
            cp.wait_send()

    return pl.pallas_call(
        body, name="grads_join_halves", out_shape=[jax.ShapeDtypeStruct(r.shape, r.dtype) for r in reduced],
        in_specs=[ANY] * n, out_specs=[ANY] * n, input_output_aliases={i: i for i in range(n)},
        scratch_shapes=[pltpu.SemaphoreType.DMA((n,)), pltpu.SemaphoreType.DMA((n,))],
    )(*reduced)


def _small_allreduce(vec):
    rows = vec.shape[0]

    def body(v_ref, o_ref, land, send_sems, recv_sems):
        x, y, c = _position()
        me = 4 * x + 2 * y + c
        land[0] = v_ref[...]
        sent = []
        for k in range(1, N_DEV):
            to = (1 - x if k & 4 else x, 1 - y if k & 2 else y, 1 - c if k & 1 else c)
            cp = pltpu.make_async_remote_copy(
                src_ref=v_ref, dst_ref=land.at[k], send_sem=send_sems.at[k - 1], recv_sem=recv_sems.at[k - 1],
                device_id=to, device_id_type=MESH)
            cp.start()
            sent.append(cp)
        for cp in sent:
            cp.wait_recv()
        acc = land[me]
        for e in range(1, N_DEV):
            acc = acc + land[me ^ e]
        o_ref[...] = acc
        for cp in sent:
            cp.wait_send()

    vmem = pl.BlockSpec(memory_space=pltpu.VMEM)
    return pl.pallas_call(
        body, name="small_allreduce", out_shape=jax.ShapeDtypeStruct(vec.shape, F32),
        in_specs=[vmem], out_specs=vmem,
        scratch_shapes=[pltpu.VMEM((N_DEV, rows, LANES), F32), pltpu.SemaphoreType.DMA((N_DEV - 1,)),
                        pltpu.SemaphoreType.DMA((N_DEV - 1,))],
    )(vec)


def _adam_math(w, g, m, v):
    m = ADAM_B1 * m + (1.0 - ADAM_B1) * g
    v = ADAM_B2 * v + (1.0 - ADAM_B2) * jnp.square(g)
    m_hat = m / (1.0 - ADAM_B1 ** ADAM_STEP)
    v_hat = v / (1.0 - ADAM_B2 ** ADAM_STEP)
    delta = -ADAM_LR * (m_hat / (jnp.sqrt(v_hat) + ADAM_EPS) + ADAM_WD * w)
    return delta, m, v


def _adam(w, g, m, v, name):
    def body(w_ref, g_ref, m_ref, v_ref, d_ref, nm_ref, nv_ref):
        d_ref[...], nm_ref[...], nv_ref[...] = _adam_math(w_ref[...], g_ref[...], m_ref[...], v_ref[...])

    if w.ndim == 3:
        lyr, R, C = w.shape
        tr = _pick(R, (256, 352, 128))
        blk = pl.BlockSpec((None, tr, C), lambda l, r: (l, r, 0))
        grid, sem = (lyr, R // tr), ("parallel", "parallel")
    else:
        blk = pl.BlockSpec(w.shape, lambda: (0, 0))
        grid, sem = (), None
    return pl.pallas_call(
        body, name=name, grid=grid, in_specs=[blk] * 4, out_specs=[blk] * 3,
        out_shape=[jax.ShapeDtypeStruct(w.shape, F32)] * 3, compiler_params=_params(sem),
    )(w, g, m, v)


def _rows(a, pad_to=8):
    r = a.reshape(-1, LANES)
    extra = (-r.shape[0]) % pad_to
    return jnp.pad(r, ((0, extra), (0, 0))) if extra else r


def _pack(arrays):
    return jnp.concatenate([_rows(a) for a in arrays], axis=0)


def _unpack(slab, shapes):
    out, at = [], 0
    for shp in shapes:
        nrow = math.prod(shp) // LANES
        out.append(slab[at:at + nrow].reshape(shp))
        at += nrow + (-nrow) % 8
    return out


BIG = ("w_in", "w_out", "w_gate_t", "w_up_t", "w_down")
BIG_COL_SHARDED = (True, False, False, False, False)
TRANSPOSED = {"w_gate_t": "w_gate", "w_up_t": "w_up"}


def kernel(x, meta_tokens, mix_norm_g, w_in, conv_dw_w, conv_dw_b, conv_ln_g, conv_ln_b, w_out, ffn_norm_g, w_gate, w_up, w_down, final_norm_g, loss_target, m_meta_tokens, m_mix_norm_g, m_w_in, m_conv_dw_w, m_conv_dw_b, m_conv_ln_g, m_conv_ln_b, m_w_out, m_ffn_norm_g, m_w_gate, m_w_up, m_w_down, m_final_norm_g, v_meta_tokens, v_mix_norm_g, v_w_in, v_conv_dw_w, v_conv_dw_b, v_conv_ln_g, v_conv_ln_b, v_w_out, v_ffn_norm_g, v_w_gate, v_w_up, v_w_down, v_final_norm_g):
    n_meta, seq = meta_tokens.shape[0], x.shape[1]
    D = x.shape[2]
    depth, taps, c_shard = conv_dw_w.shape
    C = conv_dw_b.shape[-1]
    chip = (2 * lax.axis_index("x") + lax.axis_index("y")).astype(jnp.int32)
    core = lax.axis_index("c").astype(jnp.int32).reshape(1)
    chip1 = chip.reshape(1)
    where = jnp.concatenate([chip1, core])
    big_w = dict(w_in=w_in, w_out=w_out, w_gate=w_gate, w_up=w_up, w_down=w_down)
    big_m = dict(w_in=m_w_in, w_out=m_w_out, w_gate=m_w_gate, w_up=m_w_up, w_down=m_w_down)
    big_v = dict(w_in=v_w_in, w_out=v_w_out, w_gate=v_w_gate, w_up=v_w_up, w_down=v_w_down)

    small_shard = _pack([conv_dw_w, meta_tokens])[None]
    to_send = {k: jnp.swapaxes(big_w[TRANSPOSED[k]], 1, 2) if k in TRANSPOSED else big_w[k] for k in BIG}
    col = dict(zip(BIG, BIG_COL_SHARDED))
    wts = {k: [_place_shard(to_send[k], l, chip1, col[k], BF16, f"place_{k}_{l}") for l in range(depth)] for k in BIG}
    small_placed = _place_shard(small_shard, 0, chip1, False, F32, "place_small")

    def gather_job(keys, extra=()):
        arrays = [wts[k][l] for k, l in keys] + list(extra)
        shapes = [(1,) + to_send[k].shape[1:] for k, _ in keys] + [(1,) + small_shard.shape[1:]] * len(extra)
        return _gather_job(arrays, shapes, [col[k] for k, _ in keys] + [False] * len(extra))

    first_keys = [("w_in", 0), ("w_out", 0)]
    *first, small_full = _run_job(gather_job(first_keys, [small_placed]), "gather_first")
    for (k, l), arr in zip(first_keys, first):
        wts[k][l] = arr
    ffn = lambda l: [("w_gate_t", l), ("w_up_t", l), ("w_down", l)]
    behind_keys = {0: ffn(0) + [("w_in", l) for l in range(1, depth)] + [("w_out", l) for l in range(1, depth)]}
    for l in range(1, depth):
        behind_keys[l] = ffn(l)
    gather_behind = {l: (gather_job(keys), keys) for l, keys in behind_keys.items()}

    rows_shard = small_shard.shape[1]
    dw_full, meta_full = [], []
    for j in range(N_CHIPS):
        dwj, mj = _unpack(small_full[0, j * rows_shard:(j + 1) * rows_shard],
                          [conv_dw_w.shape, meta_tokens.shape])
        dw_full.append(dwj)
        meta_full.append(mj)
    dw_w_full = jnp.concatenate(dw_full, axis=2)
    meta = jnp.concatenate(meta_full, axis=1)

    L = n_meta + seq
    Lp = -(-L // QUERY_BLOCK) * QUERY_BLOCK
    h0 = jnp.concatenate([meta, x[0], jnp.zeros((Lp - L, D), F32)], axis=0)
    target = jnp.pad(loss_target[0], ((n_meta, Lp - L), (0, 0)))
    parts, across = {}, {}

    def reduce_layer(l, big):
        flat = [big[k] for k in BIG]
        landed = _rs_to_sibling(flat, f"grads_to_sibling_{l}")
        parts[l] = [_chip_sum(g, la, core, f"chip_sum_{k}_{l}") for k, g, la in zip(BIG, flat, landed)]
        job = _across_job(parts[l])
        if l == 0:
            across[l] = _run_job(job, "grads_across_chips_0")
            return None
        return job, functools.partial(across.__setitem__, l)

    loss, dh0, grads = _local_step(h0, target, n_meta, seq, (mix_norm_g, ffn_norm_g),
                                   (dw_w_full, conv_dw_b, conv_ln_g, conv_ln_b), wts, final_norm_g,
                                   gather_behind, reduce_layer)
    loss = lax.psum(loss[0, 0], ("x", "y", "c"))
    grad_x = dh0[n_meta:L][None]

    reduced = []
    for wi, k in enumerate(BIG):
        arr = None
        for l in range(depth):
            arr = _sum_chips(parts[l][wi], across[l][wi], where, l, depth, arr, f"sum_chips_{k}_{l}")
        reduced.append(arr)
    big_g = dict(zip(BIG, _rs_join_halves(reduced)))

    small_names = ("mix_g", "ffn_g", "dw_b", "ln_g", "ln_b", "dw_w")
    small = [jnp.stack(grads[k]) for k in small_names] + [grads["final_g"], dh0[:n_meta]]
    small_shapes = [a.shape for a in small]
    g_mix, g_ffn, g_dwb, g_lng, g_lnb, g_dww, g_final, g_meta = _unpack(_small_allreduce(_pack(small)), small_shapes)
    g_dww = lax.dynamic_slice_in_dim(g_dww, chip * c_shard, c_shard, axis=2)
    g_meta = lax.dynamic_slice_in_dim(g_meta, chip * meta_tokens.shape[1], meta_tokens.shape[1], axis=1)

    out_g, out_d, out_m, out_v = {}, {}, {}, {}
    for kk in BIG:
        k = TRANSPOSED.get(kk, kk)
        out_g[k] = jnp.swapaxes(big_g[kk], 1, 2) if kk in TRANSPOSED else big_g[kk]
        out_d[k], out_m[k], out_v[k] = _adam(big_w[k], out_g[k], big_m[k], big_v[k], f"adam_{k}")
    small_order = ("meta_tokens", "mix_norm_g", "conv_dw_w", "conv_dw_b", "conv_ln_g", "conv_ln_b",
                   "ffn_norm_g", "final_norm_g")
    sw = dict(meta_tokens=meta_tokens, mix_norm_g=mix_norm_g, conv_dw_w=conv_dw_w, conv_dw_b=conv_dw_b,
              conv_ln_g=conv_ln_g, conv_ln_b=conv_ln_b, ffn_norm_g=ffn_norm_g, final_norm_g=final_norm_g)
    sm = dict(meta_tokens=m_meta_tokens, mix_norm_g=m_mix_norm_g, conv_dw_w=m_conv_dw_w, conv_dw_b=m_conv_dw_b,
              conv_ln_g=m_conv_ln_g, conv_ln_b=m_conv_ln_b, ffn_norm_g=m_ffn_norm_g, final_norm_g=m_final_norm_g)
    sv = dict(meta_tokens=v_meta_tokens, mix_norm_g=v_mix_norm_g, conv_dw_w=v_conv_dw_w, conv_dw_b=v_conv_dw_b,
              conv_ln_g=v_conv_ln_g, conv_ln_b=v_conv_ln_b, ffn_norm_g=v_ffn_norm_g, final_norm_g=v_final_norm_g)
    sg = dict(meta_tokens=g_meta, mix_norm_g=g_mix, conv_dw_w=g_dww, conv_dw_b=g_dwb, conv_ln_g=g_lng,
              conv_ln_b=g_lnb, ffn_norm_g=g_ffn, final_norm_g=g_final)
    def slab(d):
        return _pack([d[k] for k in small_order])
    shapes = [sw[k].shape for k in small_order]
    deltas = _adam(slab(sw), slab(sg), slab(sm), slab(sv), "adam_small")
    for res, dst in zip(deltas, (out_d, out_m, out_v)):
        dst.update(zip(small_order, _unpack(res, shapes)))
    out_g.update(sg)

    order = ("meta_tokens", "mix_norm_g", "w_in", "conv_dw_w", "conv_dw_b", "conv_ln_g", "conv_ln_b", "w_out",
             "ffn_norm_g", "w_gate", "w_up", "w_down", "final_norm_g")
    return (loss, grad_x, *[out_g[k] for k in order], *[out_d[k] for k in order],
            *[out_m[k] for k in order], *[out_v[k] for k in order])
```

```python
import functools
import math

import jax
import jax.numpy as jnp
from jax import lax
from jax.experimental import pallas as pl
from jax.experimental.pallas import tpu as pltpu

F32 = jnp.float32
BF16 = jnp.bfloat16
MESH = pl.DeviceIdType.MESH

EPS = 1e-6
QUERY_BLOCK = 128
LANES = 128
HEAD_DIM = 64
LOG_STICK_FLOOR = -40.0
CONV_HALO = 32
N_CHIPS = 4
N_DEV = 8
VMEM_LIMIT = 56 * 1024 * 1024

ADAM_LR = 0.001
ADAM_B1 = 0.9
ADAM_B2 = 0.999
ADAM_EPS = 1e-08
ADAM_WD = 0.01
ADAM_STEP = 10


def _pick(n, prefs):
    for p in prefs:
        if n % p == 0:
            return p
    return n


def _params(sem=None):
    return pltpu.CompilerParams(dimension_semantics=sem, vmem_limit_bytes=VMEM_LIMIT)


def _sigmoid(x):
    return 1.0 / (1.0 + jnp.exp(-x))


def _rmsnorm_fwd(h, g, name):
    L, D = h.shape
    T = _pick(L, (384, 128))

    def body(h_ref, g_ref, o_ref):
        x = h_ref[...]
        r = lax.rsqrt(jnp.mean(x * x, axis=-1, keepdims=True) + EPS)
        o_ref[...] = (x * r * g_ref[...]).astype(o_ref.dtype)

    return pl.pallas_call(
        body, name=name, grid=(L // T,),
        in_specs=[pl.BlockSpec((T, D), lambda i: (i, 0)), pl.BlockSpec((1, D), lambda i: (0, 0))],
        out_specs=pl.BlockSpec((T, D), lambda i: (i, 0)),
        out_shape=jax.ShapeDtypeStruct((L, D), BF16),
        compiler_params=_params(("parallel",)),
    )(h, g)


def _rmsnorm_bwd(h, g, dy, dh_in, name):
    L, D = h.shape
    T = _pick(L, (384, 128))

    def body(h_ref, g_ref, dy_ref, dhin_ref, dh_ref, dg_ref):
        x = h_ref[...]
        dyv = dy_ref[...]
        r = lax.rsqrt(jnp.mean(x * x, axis=-1, keepdims=True) + EPS)
        xh = x * r
        dxh = dyv * g_ref[...]
        dh_ref[...] = dhin_ref[...] + r * (dxh - xh * jnp.mean(dxh * xh, axis=-1, keepdims=True))

        @pl.when(pl.program_id(0) == 0)
        def _():
            dg_ref[...] = jnp.zeros_like(dg_ref)

        dg_ref[...] += jnp.sum(dyv * xh, axis=0, keepdims=True)

    row = pl.BlockSpec((T, D), lambda i: (i, 0))
    vec = pl.BlockSpec((1, D), lambda i: (0, 0))
    return pl.pallas_call(
        body, name=name, grid=(L // T,),
        in_specs=[row, vec, row, row], out_specs=[row, vec],
        out_shape=[jax.ShapeDtypeStruct((L, D), F32), jax.ShapeDtypeStruct((1, D), F32)],
        compiler_params=_params(("arbitrary",)),
    )(h, g, dy, dh_in)


def _loss_head(h, g, target, n_meta, seq, name):
    L, D = h.shape
    T = _pick(L, (384, 128))

    def body(h_ref, g_ref, t_ref, loss_ref, dh_ref, dg_ref):
        i = pl.program_id(0)
        x = h_ref[...]
        gv = g_ref[...]
        r = lax.rsqrt(jnp.mean(x * x, axis=-1, keepdims=True) + EPS)
        xh = x * r
        y = xh * gv
        rows = i * T + lax.broadcasted_iota(jnp.int32, (T, 1), 0)
        live = (rows >= n_meta) & (rows < n_meta + seq)
        diff = jnp.where(live, y - t_ref[...], 0.0)
        dyv = diff / D
        dxh = dyv * gv
        dh_ref[...] = r * (dxh - xh * jnp.mean(dxh * xh, axis=-1, keepdims=True))

        @pl.when(i == 0)
        def _():
            dg_ref[...] = jnp.zeros_like(dg_ref)
            loss_ref[...] = jnp.zeros_like(loss_ref)

        dg_ref[...] += jnp.sum(dyv * xh, axis=0, keepdims=True)
        per_row = jnp.mean(diff * diff, axis=-1, keepdims=True)
        loss_ref[...] += 0.5 * jnp.sum(per_row, axis=0, keepdims=True)

    row = pl.BlockSpec((T, D), lambda i: (i, 0))
    vec = pl.BlockSpec((1, D), lambda i: (0, 0))
    one = pl.BlockSpec((1, 1), lambda i: (0, 0))
    return pl.pallas_call(
        body, name=name, grid=(L // T,),
        in_specs=[row, vec, row], out_specs=[one, row, vec],
        out_shape=[jax.ShapeDtypeStruct((1, 1), F32), jax.ShapeDtypeStruct((L, D), F32),
                   jax.ShapeDtypeStruct((1, D), F32)],
        compiler_params=_params(("arbitrary",)),
    )(h, g, target)


def _ffn_tiles(M, F):
    return _pick(M, (704, 384, 128)), _pick(F, (1408, 512, 256, 128))


def _ffn_up(hn, w_gate_t, w_up_t, name):
    M, D = hn.shape
    F = w_gate_t.shape[1]
    tm, tf = _ffn_tiles(M, F)
    nt = (((1,), (1,)), ((), ()))

    def body(h_ref, wg_ref, wu_ref, g_ref, u_ref, a_ref):
        hv = h_ref[...]
        gv = lax.dot_general(hv, wg_ref[...], nt, preferred_element_type=F32)
        uv = lax.dot_general(hv, wu_ref[...], nt, preferred_element_type=F32)
        g_ref[...] = gv
        u_ref[...] = uv
        a_ref[...] = (gv * _sigmoid(gv) * uv).astype(a_ref.dtype)

    w_spec = pl.BlockSpec((None, tf, D), lambda i, j: (0, j, 0))
    out = pl.BlockSpec((tm, tf), lambda i, j: (i, j))
    return pl.pallas_call(
        body, name=name, grid=(M // tm, F // tf),
        in_specs=[pl.BlockSpec((tm, D), lambda i, j: (i, 0)), w_spec, w_spec], out_specs=[out, out, out],
        out_shape=[jax.ShapeDtypeStruct((M, F), F32), jax.ShapeDtypeStruct((M, F), F32),
                   jax.ShapeDtypeStruct((M, F), BF16)],
        compiler_params=_params(("parallel", "parallel")),
    )(hn, w_gate_t, w_up_t)


def _ffn_down_bwd(dh, w_down, g, u, name):
    M, D = dh.shape
    F = g.shape[1]
    tm, tf = _ffn_tiles(M, F)

    def body(d_ref, w_ref, g_ref, u_ref, dg_ref, du_ref):
        dv = lax.dot_general(d_ref[...].astype(BF16), w_ref[...], (((1,), (1,)), ((), ())),
                             preferred_element_type=F32)
        gv = g_ref[...]
        s = _sigmoid(gv)
        du_ref[...] = (dv * (gv * s)).astype(du_ref.dtype)
        dg_ref[...] = (dv * u_ref[...] * (s * (1.0 + gv * (1.0 - s)))).astype(dg_ref.dtype)

    tile = pl.BlockSpec((tm, tf), lambda i, j: (i, j))
    return pl.pallas_call(
        body, name=name, grid=(M // tm, F // tf),
        in_specs=[pl.BlockSpec((tm, D), lambda i, j: (i, 0)), pl.BlockSpec((None, tf, D), lambda i, j: (0, j, 0)),
                  tile, tile],
        out_specs=[tile, tile],
        out_shape=[jax.ShapeDtypeStruct((M, F), BF16), jax.ShapeDtypeStruct((M, F), BF16)],
        compiler_params=_params(("parallel", "parallel")),
    )(dh, w_down, g, u)


def _mm_nn(pairs, out_dtype, name, residual=None, cols=None):
    M = pairs[0][0].shape[0]
    col0, N = cols if cols is not None else (0, pairs[0][1].shape[-1])
    tm = _pick(M, (1056, 384, 128))
    tn = _pick(math.gcd(N, col0) if col0 else N, (640, 512, 256, 128))
    jb = col0 // tn
    n = len(pairs)

    def body(*refs):
        a_refs, w_refs = refs[:n], refs[n:2 * n]
        o_ref = refs[-1]
        acc = None
        for a_ref, w_ref in zip(a_refs, w_refs):
            d = jnp.dot(a_ref[...].astype(BF16), w_ref[...], preferred_element_type=F32)
            acc = d if acc is None else acc + d
        if residual is not None:
            acc = acc + refs[2 * n][...]
        o_ref[...] = acc.astype(o_ref.dtype)

    in_specs = [pl.BlockSpec((tm, a.shape[1]), lambda i, j: (i, 0)) for a, _, _, _ in pairs]
    for a, _, layer, kblk in pairs:
        in_specs.append(pl.BlockSpec((None, a.shape[1], tn), functools.partial(lambda i, j, l, kb: (l, kb, j + jb), l=layer, kb=kblk)))
    args = [p[0] for p in pairs] + [p[1] for p in pairs]
    if residual is not None:
        in_specs.append(pl.BlockSpec((tm, tn), lambda i, j: (i, j)))
        args.append(residual)
    return pl.pallas_call(
        body, name=name, grid=(M // tm, N // tn), in_specs=in_specs,
        out_specs=pl.BlockSpec((tm, tn), lambda i, j: (i, j)),
        out_shape=jax.ShapeDtypeStruct((M, N), out_dtype),
        compiler_params=_params(("parallel", "parallel")),
    )(*args)


def _mm_nt(pairs, out_dtype, name):
    M = pairs[0][0].shape[0]
    K = pairs[0][1].shape[1]
    tm = _pick(M, (1056, 384, 128))
    tk = _pick(K, (512, 1408, 256, 128))
    n = len(pairs)

    def body(*refs):
        d_refs, w_refs = refs[:n], refs[n:2 * n]
        o_ref = refs[-1]
        acc = None
        for d_ref, w_ref in zip(d_refs, w_refs):
            d = lax.dot_general(d_ref[...].astype(BF16), w_ref[...], (((1,), (1,)), ((), ())),
                                preferred_element_type=F32)
            acc = d if acc is None else acc + d
        o_ref[...] = acc.astype(o_ref.dtype)

    in_specs = [pl.BlockSpec((tm, d.shape[1]), lambda i, j: (i, 0)) for d, _, _, _ in pairs]
    for d, _, layer, cblk in pairs:
        in_specs.append(pl.BlockSpec((None, tk, d.shape[1]), functools.partial(lambda i, j, l, cb: (l, j, cb), l=layer, cb=cblk)))
    args = [p[0] for p in pairs] + [p[1] for p in pairs]
    return pl.pallas_call(
        body, name=name, grid=(M // tm, K // tk), in_specs=in_specs,
        out_specs=pl.BlockSpec((tm, tk), lambda i, j: (i, j)),
        out_shape=jax.ShapeDtypeStruct((M, K), out_dtype),
        compiler_params=_params(("parallel", "parallel")),
    )(*args)


def _mm_tn(a, b, name, col_sharded):
    M, K = a.shape
    N = b.shape[1]
    tm = _pick(M, (1056, 384, 128))
    tk = _pick(K, (512, 1408, 256, 128))
    tn = N // N_CHIPS if col_sharded else _pick(N, (512, 128))

    def body(a_ref, b_ref, o_ref):
        @pl.when(pl.program_id(2) == 0)
        def _():
            o_ref[...] = jnp.zeros_like(o_ref)

        o_ref[...] += lax.dot_general(a_ref[...].astype(BF16), b_ref[...].astype(BF16),
                                      (((0,), (0,)), ((), ())), preferred_element_type=F32)

    if col_sharded:
        out_shape = jax.ShapeDtypeStruct((N_CHIPS, K, tn), F32)
        out_spec = pl.BlockSpec((None, tk, tn), lambda k, j, m: (j, k, 0))
    else:
        out_shape = jax.ShapeDtypeStruct((K, N), F32)
        out_spec = pl.BlockSpec((tk, tn), lambda k, j, m: (k, j))
    out = pl.pallas_call(
        body, name=name, grid=(K // tk, N // tn, M // tm),
        in_specs=[pl.BlockSpec((tm, tk), lambda k, j, m: (m, k)), pl.BlockSpec((tm, tn), lambda k, j, m: (m, j))],
        out_specs=out_spec, out_shape=out_shape,
        compiler_params=_params(("parallel", "parallel", "arbitrary")),
    )(a, b)
    return out if col_sharded else out.reshape(N_CHIPS, K // N_CHIPS, N)


def _stack_heads(x, scale=None):
    lane = lax.broadcasted_iota(jnp.int32, x.shape, 1)
    zero = jnp.zeros_like(x)
    lo = jnp.where(lane < HEAD_DIM, x, zero)
    hi = jnp.where(lane < HEAD_DIM, zero, x)
    out = jnp.concatenate([lo, hi], axis=0)
    return out if scale is None else out * scale


def _unstack_heads(x2):
    qb = x2.shape[0] // 2
    lane = lax.broadcasted_iota(jnp.int32, (qb, LANES), 1)
    return jnp.where(lane < HEAD_DIM, x2[:qb], x2[qb:])


def _tri_and_ones(strict):
    r = lax.broadcasted_iota(jnp.int32, (QUERY_BLOCK, 2 * QUERY_BLOCK), 0)
    c = lax.broadcasted_iota(jnp.int32, (QUERY_BLOCK, 2 * QUERY_BLOCK), 1)
    tri = (r > c) if strict else (r >= c)
    return jnp.where(tri | (c >= QUERY_BLOCK), 1.0, 0.0).astype(BF16)


def _dot2(x, m):
    xh = x.astype(BF16)
    xl = (x - xh.astype(F32)).astype(BF16)
    return jnp.dot(xh, m, preferred_element_type=F32) + jnp.dot(xl, m, preferred_element_type=F32)


def _causal_valid():
    r = lax.broadcasted_iota(jnp.int32, (2 * QUERY_BLOCK, QUERY_BLOCK), 0) & (QUERY_BLOCK - 1)
    c = lax.broadcasted_iota(jnp.int32, (2 * QUERY_BLOCK, QUERY_BLOCK), 1)
    return c < r


def _sweep_older(i, step, carry_ref, first):
    def cond(state):
        n, live = state
        return jnp.logical_and(n < i, live)

    def older(state):
        n, _ = state
        step(i - 1 - n, False)
        return n + 1, jnp.max(carry_ref[...]) > LOG_STICK_FLOOR

    lax.while_loop(cond, older, (first, jnp.max(carry_ref[...]) > LOG_STICK_FLOOR))


class _CommJob:
    def __init__(self, inputs, out_shapes, aliases, n_sems, start, finish):
        self.inputs, self.out_shapes, self.aliases, self.n_sems = list(inputs), list(out_shapes), aliases, n_sems
        self.start, self.finish = start, finish


def _call_with_job(core_body, job, *, name, grid, in_specs, out_specs, out_shape, scratch_shapes, args):
    sem = ("arbitrary",) * len(grid)
    if job is None:
        res = pl.pallas_call(core_body, name=name, grid=grid, in_specs=in_specs, out_specs=out_specs,
                             out_shape=out_shape, scratch_shapes=scratch_shapes, compiler_params=_params(sem))(*args)
        return list(res), []
    n_in, n_out, n_scr = len(in_specs), len(out_specs), len(scratch_shapes)
    m_in, m_out = len(job.inputs), len(job.out_shapes)

    def body(*refs):
        at = 0
        parts = []
        for count in (n_in, m_in, n_out, m_out, n_scr, 2):
            parts.append(refs[at:at + count])
            at += count
        ins, job_in, outs, job_outs, scratch, (send_sems, recv_sems) = parts
        first = functools.reduce(jnp.logical_and, [pl.program_id(a) == 0 for a in range(len(grid))])
        last = functools.reduce(jnp.logical_and, [pl.program_id(a) == grid[a] - 1 for a in range(len(grid))])

        @pl.when(first)
        def _():
            job.start(job_in, job_outs, send_sems, recv_sems)

        core_body(*ins, *outs, *scratch)

        @pl.when(last)
        def _():
            job.finish(job_in, job_outs, send_sems, recv_sems)

    res = pl.pallas_call(
        body, name=name, grid=grid, in_specs=list(in_specs) + [ANY] * m_in, out_specs=list(out_specs) + [ANY] * m_out,
        out_shape=list(out_shape) + job.out_shapes,
        input_output_aliases={n_in + a: n_out + b for a, b in job.aliases.items()},
        scratch_shapes=list(scratch_shapes) + [pltpu.SemaphoreType.DMA((job.n_sems,)), pltpu.SemaphoreType.DMA((job.n_sems,))],
        compiler_params=_params(sem),
    )(*args, *job.inputs)
    return list(res[:n_out]), list(res[n_out:])


def _run_job(job, name):
    m_in, m_out = len(job.inputs), len(job.out_shapes)

    def body(*refs):
        job_in, job_outs = refs[:m_in], refs[m_in:m_in + m_out]
        send_sems, recv_sems = refs[m_in + m_out:]
        job.start(job_in, job_outs, send_sems, recv_sems)
        job.finish(job_in, job_outs, send_sems, recv_sems)

    return list(pl.pallas_call(
        body, name=name, in_specs=[ANY] * m_in, out_specs=[ANY] * m_out, out_shape=job.out_shapes,
        input_output_aliases=dict(job.aliases),
        scratch_shapes=[pltpu.SemaphoreType.DMA((job.n_sems,)), pltpu.SemaphoreType.DMA((job.n_sems,))],
    )(*job.inputs))


def _attn_fwd(qkv, sb_width, name, job=None):
    L = qkv.shape[0]
    QB = QUERY_BLOCK
    nb = L // QB
    n_pairs = sb_width // LANES
    scale = 1.0 / math.sqrt(HEAD_DIM)

    def body(q_ref, k_ref, v_ref, o_ref, acc_ref, carry_ref):
        i = pl.program_id(1)
        q2 = _stack_heads(q_ref[...], scale)
        txo = _tri_and_ones(True)
        valid = _causal_valid()
        acc_ref[...] = jnp.zeros_like(acc_ref)
        carry_ref[...] = jnp.zeros_like(carry_ref)

        def front(j, diag):
            start = pl.multiple_of(j * QB, QB)
            kb = k_ref[pl.ds(start, QB), :]
            vb = v_ref[pl.ds(start, QB), :]
            z = lax.dot_general(q2, kb, (((1,), (1,)), ((), ())), preferred_element_type=F32)
            sp = jnp.log(1.0 + jnp.exp(-jnp.abs(z)))
            a = jnp.minimum(z, 0.0) - sp
            b = jnp.minimum(-z, 0.0) - sp
            if diag:
                b = jnp.where(valid, b, 0.0)
            return a, _dot2(b, txo), vb

        def back(a, res, vb, diag):
            w = jnp.exp(a + res[:, :QB] + carry_ref[...])
            if diag:
                w = jnp.where(valid, w, 0.0)
            carry_ref[...] += res[:, QB:]
            acc_ref[...] += _dot2(w, vb)

        def step(j, diag):
            back(*front(j, diag), diag)

        @pl.when(i == 0)
        def _():
            step(0, True)

        @pl.when(i > 0)
        def _():
            f0 = front(i, True)
            f1 = front(i - 1, False)
            back(*f0, True)
            back(*f1, False)
            _sweep_older(i, step, carry_ref, 1)

        o_ref[...] = _unstack_heads(acc_ref[...])

    (out,), job_out = _call_with_job(
        body, job, name=name, grid=(n_pairs, nb),
        in_specs=[pl.BlockSpec((QB, LANES), lambda p, i: (i, p)),
                  pl.BlockSpec((L, LANES), lambda p, i: (0, n_pairs + p)),
                  pl.BlockSpec((L, LANES), lambda p, i: (0, 2 * n_pairs + p))],
        out_specs=[pl.BlockSpec((QB, LANES), lambda p, i: (i, p))],
        out_shape=[jax.ShapeDtypeStruct((L, sb_width), F32)],
        scratch_shapes=[pltpu.VMEM((2 * QB, LANES), F32), pltpu.VMEM((2 * QB, LANES), F32)],
        args=(qkv, qkv, qkv))
    return out, job_out


def _attn_bwd(qkv, o, dmixed, sb_width, name, job=None):
    L = qkv.shape[0]
    QB = QUERY_BLOCK
    nb = L // QB
    n_pairs = sb_width // LANES
    scale = 1.0 / math.sqrt(HEAD_DIM)

    def body(q_ref, k_ref, v_ref, o_ref, do_ref, dq_ref, dk_ref, dv_ref,
             dq_acc, dk_acc, dv_acc, ce_ref, cr_ref):
        i = pl.program_id(1)

        @pl.when(i == 0)
        def _():
            dk_acc[...] = jnp.zeros_like(dk_acc)
            dv_acc[...] = jnp.zeros_like(dv_acc)

        q2 = _stack_heads(q_ref[...], scale)
        do2 = _stack_heads(do_ref[...].astype(BF16))
        ov = o_ref[...]
        o2 = jnp.concatenate([ov, ov], axis=0)
        ones = jnp.ones((LANES, LANES), BF16)
        dtot = _dot2(do2.astype(F32) * o2, ones)
        txo = _tri_and_ones(True)
        tio = _tri_and_ones(False)
        valid = _causal_valid()
        dq_acc[...] = jnp.zeros_like(dq_acc)
        ce_ref[...] = jnp.zeros_like(ce_ref)
        cr_ref[...] = jnp.zeros_like(cr_ref)

        def front(j, diag):
            start = pl.multiple_of(j * QB, QB)
            kb = k_ref[pl.ds(start, QB), :]
            vb = v_ref[pl.ds(start, QB), :]
            z = lax.dot_general(q2, kb, (((1,), (1,)), ((), ())), preferred_element_type=F32)
            e = jnp.exp(-jnp.abs(z))
            sp = jnp.log(1.0 + e)
            a = jnp.minimum(z, 0.0) - sp
            b = jnp.minimum(-z, 0.0) - sp
            rinv = 1.0 / (1.0 + e)
            pos = z >= 0.0
            beta = jnp.where(pos, rinv, e * rinv)
            one_m_beta = jnp.where(pos, e * rinv, rinv)
            if diag:
                b = jnp.where(valid, b, 0.0)
            dw = lax.dot_general(do2, vb, (((1,), (1,)), ((), ())), preferred_element_type=F32)
            return start, kb, a, beta, one_m_beta, _dot2(b, txo), dw

        def back(start, kb, a, beta, one_m_beta, res, dw, diag):
            w = jnp.exp(a + res[:, :QB] + ce_ref[...])
            if diag:
                w = jnp.where(valid, w, 0.0)
            ce_ref[...] += res[:, QB:]
            g = w * dw
            res2 = _dot2(g, tio)
            rinc = res2[:, :QB] + cr_ref[...]
            cr_ref[...] += res2[:, QB:]
            dz = g * one_m_beta - beta * (dtot - rinc)
            if diag:
                dz = jnp.where(valid, dz, 0.0)
            dzb = dz.astype(BF16)
            dq_acc[...] += jnp.dot(dzb, kb, preferred_element_type=F32)
            dk_acc[pl.ds(start, QB), :] += lax.dot_general(
                dzb, q2, (((0,), (0,)), ((), ())), preferred_element_type=F32)
            dv_acc[pl.ds(start, QB), :] += lax.dot_general(
                w.astype(BF16), do2, (((0,), (0,)), ((), ())), preferred_element_type=F32)

        def step(j, diag):
            back(*front(j, diag), diag)

        @pl.when(i == 0)
        def _():
            step(0, True)

        @pl.when(i > 0)
        def _():
            f0 = front(i, True)
            f1 = front(i - 1, False)
            back(*f0, True)
            back(*f1, False)
            _sweep_older(i, step, ce_ref, 1)

        dq_ref[...] = (_unstack_heads(dq_acc[...]) * scale).astype(dq_ref.dtype)

        @pl.when(i == nb - 1)
        def _():
            dk_ref[...] = dk_acc[...].astype(dk_ref.dtype)
            dv_ref[...] = dv_acc[...].astype(dv_ref.dtype)

    blk = pl.BlockSpec((QB, LANES), lambda p, i: (i, p))
    col = pl.BlockSpec((L, LANES), lambda p, i: (0, p))
    return _call_with_job(
        body, job, name=name, grid=(n_pairs, nb),
        in_specs=[blk,
                  pl.BlockSpec((L, LANES), lambda p, i: (0, n_pairs + p)),
                  pl.BlockSpec((L, LANES), lambda p, i: (0, 2 * n_pairs + p)),
                  blk, blk],
        out_specs=[blk, col, col],
        out_shape=[jax.ShapeDtypeStruct((L, sb_width), BF16)] * 3,
        scratch_shapes=[pltpu.VMEM((2 * QB, LANES), F32), pltpu.VMEM((L, LANES), F32),
                        pltpu.VMEM((L, LANES), F32), pltpu.VMEM((2 * QB, LANES), F32),
                        pltpu.VMEM((2 * QB, LANES), F32)],
        args=(qkv, qkv, qkv, o, dmixed))


def _conv_tile(L):
    return _pick(L, (384, 128))


def _glu(x, C):
    return x[:, :C] * _sigmoid(x[:, C:])


CONV_CHUNK = 32
SHIFT_TAIL = 24


def _fill_shifted(src_ref, dst_ref):
    n = dst_ref.shape[1]
    for r in range(1, 8):
        dst_ref[r - 1] = src_ref[r:r + n, :]


def _rows_at(src_ref, shifted_ref, start, n):
    q, r = divmod(start, 8)
    if r == 0:
        return src_ref[start:start + n, :]
    return shifted_ref[r - 1, 8 * q:8 * q + n, :]


def _conv_fwd(cacg, dw_w, dw_b, ln_g, ln_b, name):
    L, C2 = cacg.shape
    C = C2 // 2
    T = _conv_tile(L)
    H = CONV_HALO
    K = dw_w.shape[0]
    CH = CONV_CHUNK

    def body(x_ref, prev_ref, w_ref, b_ref, g_ref, beta_ref, o_ref, u_ref, us_ref):
        i = pl.program_id(0)
        u_ref[0:H, :] = jnp.where(i > 0, _glu(prev_ref[...], C), 0.0)
        u_ref[H:, :] = _glu(x_ref[...], C)
        _fill_shifted(u_ref, us_ref)
        for c0 in range(0, T, CH):
            y = jnp.broadcast_to(b_ref[...], (CH, C))
            for k in range(K):
                y = y + w_ref[k:k + 1, :] * _rows_at(u_ref, us_ref, c0 + H - (K - 1) + k, CH)
            mu = jnp.mean(y, axis=-1, keepdims=True)
            yc = y - mu
            rstd = lax.rsqrt(jnp.mean(yc * yc, axis=-1, keepdims=True) + EPS)
            ln = yc * rstd * g_ref[...] + beta_ref[...]
            o_ref[c0:c0 + CH, :] = (ln * _sigmoid(ln)).astype(o_ref.dtype)

    vec = pl.BlockSpec((1, C), lambda i: (0, 0))
    return pl.pallas_call(
        body, name=name, grid=(L // T,),
        in_specs=[pl.BlockSpec((T, C2), lambda i: (i, 0)),
                  pl.BlockSpec((H, C2), lambda i: (jnp.maximum(i * (T // H) - 1, 0), 0)),
                  pl.BlockSpec((K, C), lambda i: (0, 0)), vec, vec, vec],
        out_specs=pl.BlockSpec((T, C), lambda i: (i, 0)),
        out_shape=jax.ShapeDtypeStruct((L, C), BF16),
        scratch_shapes=[pltpu.VMEM((T + H, C), F32), pltpu.VMEM((7, T + SHIFT_TAIL, C), F32)],
        compiler_params=_params(("parallel",)),
    )(cacg, cacg, dw_w, dw_b, ln_g, ln_b)


def _conv_bwd(cacg, dmixed, dw_w, dw_b, ln_g, ln_b, name):
    L, C2 = cacg.shape
    C = C2 // 2
    T = _conv_tile(L)
    H = CONV_HALO
    K = dw_w.shape[0]
    nt = L // T
    TE = T + H

    CH = CONV_CHUNK

    def body(x_ref, prev_ref, next_ref, d_ref, dnext_ref, w_ref, b_ref, g_ref, beta_ref,
             dca_ref, dcg_ref, dwt_ref, db_ref, dg_ref, dbeta_ref, u_ref, us_ref, dy_ref, dys_ref):
        i = pl.program_id(0)
        last = i == nt - 1

        @pl.when(i == 0)
        def _():
            dwt_ref[...] = jnp.zeros_like(dwt_ref)
            db_ref[...] = jnp.zeros_like(db_ref)
            dg_ref[...] = jnp.zeros_like(dg_ref)
            dbeta_ref[...] = jnp.zeros_like(dbeta_ref)

        u_ref[0:H, :] = jnp.where(i > 0, _glu(prev_ref[...], C), 0.0)
        u_ref[H:H + T, :] = _glu(x_ref[...], C)
        u_ref[H + T:, :] = _glu(next_ref[...], C)
        _fill_shifted(u_ref, us_ref)
        dg_acc = jnp.zeros((1, C), F32)
        dbeta_acc = jnp.zeros((1, C), F32)
        db_acc = jnp.zeros((1, C), F32)
        for c0 in range(0, TE, CH):
            y = jnp.broadcast_to(b_ref[...], (CH, C))
            for k in range(K):
                y = y + w_ref[k:k + 1, :] * _rows_at(u_ref, us_ref, c0 + H - (K - 1) + k, CH)
            mu = jnp.mean(y, axis=-1, keepdims=True)
            yc = y - mu
            rstd = lax.rsqrt(jnp.mean(yc * yc, axis=-1, keepdims=True) + EPS)
            yh = yc * rstd
            ln = yh * g_ref[...] + beta_ref[...]
            s = _sigmoid(ln)
            dout = d_ref[c0:c0 + CH, :] if c0 < T else jnp.where(last, 0.0, dnext_ref[c0 - T:c0 - T + CH, :])
            dln = dout * (s * (1.0 + ln * (1.0 - s)))
            dyh = dln * g_ref[...]
            dy = rstd * (dyh - jnp.mean(dyh, axis=-1, keepdims=True)
                         - yh * jnp.mean(dyh * yh, axis=-1, keepdims=True))
            dy_ref[c0:c0 + CH, :] = dy
            if c0 < T:
                dg_acc = dg_acc + jnp.sum(dln * yh, axis=0, keepdims=True)
                dbeta_acc = dbeta_acc + jnp.sum(dln, axis=0, keepdims=True)
                db_acc = db_acc + jnp.sum(dy, axis=0, keepdims=True)
        dg_ref[...] += dg_acc
        dbeta_ref[...] += dbeta_acc
        db_ref[...] += db_acc
        _fill_shifted(dy_ref, dys_ref)
        for k in range(K):
            dwt_ref[k:k + 1, :] += jnp.sum(
                dy_ref[0:T, :] * _rows_at(u_ref, us_ref, H - (K - 1) + k, T), axis=0, keepdims=True)
        for c0 in range(0, T, CH):
            du = jnp.zeros((CH, C), F32)
            for k in range(K):
                du = du + w_ref[k:k + 1, :] * _rows_at(dy_ref, dys_ref, c0 + (K - 1) - k, CH)
            x = x_ref[c0:c0 + CH, :]
            sg = _sigmoid(x[:, C:])
            dca_ref[c0:c0 + CH, :] = (du * sg).astype(dca_ref.dtype)
            dcg_ref[c0:c0 + CH, :] = (du * x[:, :C] * sg * (1.0 - sg)).astype(dcg_ref.dtype)

    nh = L // H
    vec = pl.BlockSpec((1, C), lambda i: (0, 0))
    row = pl.BlockSpec((T, C), lambda i: (i, 0))
    return pl.pallas_call(
        body, name=name, grid=(nt,),
        in_specs=[pl.BlockSpec((T, C2), lambda i: (i, 0)),
                  pl.BlockSpec((H, C2), lambda i: (jnp.maximum(i * (T // H) - 1, 0), 0)),
                  pl.BlockSpec((H, C2), lambda i: (jnp.minimum((i + 1) * (T // H), nh - 1), 0)),
                  pl.BlockSpec((T, C), lambda i: (i, 1)),
                  pl.BlockSpec((H, C), lambda i: (jnp.minimum((i + 1) * (T // H), nh - 1), 1)),
                  pl.BlockSpec((K, C), lambda i: (0, 0)), vec, vec, vec],
        out_specs=[row, row, pl.BlockSpec((H, C), lambda i: (0, 0)), vec, vec, vec],
        out_shape=[jax.ShapeDtypeStruct((L, C), BF16), jax.ShapeDtypeStruct((L, C), BF16),
                   jax.ShapeDtypeStruct((H, C), F32), jax.ShapeDtypeStruct((1, C), F32),
                   jax.ShapeDtypeStruct((1, C), F32), jax.ShapeDtypeStruct((1, C), F32)],
        scratch_shapes=[pltpu.VMEM((T + 2 * H, C), F32), pltpu.VMEM((7, TE + SHIFT_TAIL, C), F32),
                        pltpu.VMEM((TE, C), F32), pltpu.VMEM((7, T + SHIFT_TAIL, C), F32)],
        compiler_params=_params(("arbitrary",)),
    )(cacg, cacg, cacg, dmixed, dmixed, dw_w, dw_b, ln_g, ln_b)


def _local_step(h0, target, n_meta, seq, norms, conv_p, wts, final_g, gather_behind, reducer):
    mix_g, ffn_g = norms
    dw_w, dw_b, ln_g, ln_b = conv_p
    depth = mix_g.shape[0]
    C = dw_b.shape[-1]
    sbw = (wts["w_in"][0].shape[-1] - 2 * C) // 3
    assert sbw == C, "the mixer halves must have equal width"
    row = lambda a, i: a[i][None, :]

    h = h0
    saved = []
    for i in range(depth):
        hn = _rmsnorm_fwd(h, row(mix_g, i), f"mix_norm_{i}")
        proj_qkv = _mm_nn([(hn, wts["w_in"][i], 0, 0)], BF16, f"in_qkv_{i}", cols=(0, 3 * sbw))
        cacg = _mm_nn([(hn, wts["w_in"][i], 0, 0)], F32, f"in_conv_{i}", cols=(3 * sbw, 2 * C))
        job, keys = gather_behind.get(i, (None, ()))
        attn, arrived = _attn_fwd(proj_qkv, sbw, f"attn_fwd_{i}", job)
        for (wname, wl), arr in zip(keys, arrived):
            wts[wname][wl] = arr
        conv = _conv_fwd(cacg, dw_w[i], row(dw_b, i), row(ln_g, i), row(ln_b, i), f"conv_fwd_{i}")
        h_mid = _mm_nn([(attn, wts["w_out"][i], 0, 0), (conv, wts["w_out"][i], 0, 1)], F32, f"out_proj_{i}",
                       residual=h)
        hn2 = _rmsnorm_fwd(h_mid, row(ffn_g, i), f"ffn_norm_{i}")
        g, u, act = _ffn_up(hn2, wts["w_gate_t"][i], wts["w_up_t"][i], f"ffn_up_{i}")
        h_out = _mm_nn([(act, wts["w_down"][i], 0, 0)], F32, f"down_{i}", residual=h_mid)
        saved.append((h, hn, proj_qkv, cacg, attn, conv, h_mid, hn2, g, u, act))
        h = h_out

    loss, dh, d_final_g = _loss_head(h, final_g[None, :], target, n_meta, seq, "loss_head")

    grads = {k: [None] * depth for k in ("mix_g", "ffn_g", "dw_w", "dw_b", "ln_g", "ln_b")}
    for i in reversed(range(depth)):
        h_in, hn, proj_qkv, cacg, attn, conv, h_mid, hn2, g, u, act = saved[i]
        big = {}
        dg, du = _ffn_down_bwd(dh, wts["w_down"][i], g, u, f"ffn_down_bwd_{i}")
        big["w_down"] = _mm_tn(act, dh, f"dw_down_{i}", col_sharded=False)
        dhn2 = _mm_nn([(dg, wts["w_gate_t"][i], 0, 0), (du, wts["w_up_t"][i], 0, 0)], F32, f"d_hn2_{i}")
        big["w_gate_t"] = _mm_tn(dg, hn2, f"dw_gate_{i}", col_sharded=False)
        big["w_up_t"] = _mm_tn(du, hn2, f"dw_up_{i}", col_sharded=False)
        dh, d_ffn = _rmsnorm_bwd(h_mid, row(ffn_g, i), dhn2, dh, f"ffn_norm_bwd_{i}")
        dmixed = _mm_nt([(dh, wts["w_out"][i], 0, 0)], F32, f"d_mixed_{i}")
        mixed = jnp.concatenate([attn.astype(BF16), conv], axis=1)
        big["w_out"] = _mm_tn(mixed, dh, f"dw_out_{i}", col_sharded=False)
        reducer.ready(i, big)
        job, sink = reducer.take()
        (dq, dk, dv), arrived = _attn_bwd(proj_qkv, attn, dmixed, sbw, f"attn_bwd_{i}", job)
        sink(arrived)
        dca, dcg, d_dw, d_b, d_lg, d_lb = _conv_bwd(
            cacg, dmixed, dw_w[i], row(dw_b, i), row(ln_g, i), row(ln_b, i), f"conv_bwd_{i}")
        grads["dw_w"][i] = d_dw[:dw_w.shape[1]]
        grads["dw_b"][i], grads["ln_g"][i], grads["ln_b"][i] = d_b[0], d_lg[0], d_lb[0]
        dproj = jnp.concatenate([dq, dk, dv, dca, dcg], axis=1)
        dhn = _mm_nt([(dproj, wts["w_in"][i], 0, 0)], F32, f"d_hn_{i}")
        reducer.ready(i, {"w_in": _mm_tn(hn, dproj, f"dw_in_{i}", col_sharded=True)})
        dh, d_mix = _rmsnorm_bwd(h_in, row(mix_g, i), dhn, dh, f"mix_norm_bwd_{i}")
        grads["mix_g"][i], grads["ffn_g"][i] = d_mix[0], d_ffn[0]
    grads["final_g"] = d_final_g[0]
    return loss, dh, grads


ANY = pl.BlockSpec(memory_space=pl.ANY)


def _position():
    return lax.axis_index("x"), lax.axis_index("y"), lax.axis_index("c")


def _chip_at(x, y, k):
    return (1 - x if k & 2 else x), (1 - y if k & 1 else y)


def _half_rows(ref, half, rows, base=0):
    start = pl.multiple_of(base + half * rows, 8)
    lead = (slice(None),) * (len(ref.shape) - 2)
    return ref.at[(*lead, pl.ds(start, rows), slice(None))]


def _gather_job(fulls, shard_shapes, col_sharded):
    n = len(fulls)

    def tools(f_refs, send_sems, recv_sems):
        def block(wi, chip, half):
            _, R, C = shard_shapes[wi]
            if col_sharded[wi]:
                cols = pl.ds(pl.multiple_of(chip * C, LANES), C)
                return f_refs[wi].at[:, pl.ds(pl.multiple_of(half * (R // 2), 8), R // 2), cols]
            return _half_rows(f_refs[wi], half, R // 2, base=chip * R)

        def copy(wi, slot, blk, to):
            return pltpu.make_async_remote_copy(
                src_ref=blk, dst_ref=blk, send_sem=send_sems.at[6 * wi + slot],
                recv_sem=recv_sems.at[6 * wi + slot], device_id=to, device_id_type=MESH)

        return block, copy

    def start(_, f_refs, send_sems, recv_sems):
        block, copy = tools(f_refs, send_sems, recv_sems)
        x, y, c = _position()
        me = 2 * x + y
        for wi in range(n):
            for k in (1, 2, 3):
                copy(wi, k - 1, block(wi, me, c), (*_chip_at(x, y, k), c)).start()

    def finish(_, f_refs, send_sems, recv_sems):
        block, copy = tools(f_refs, send_sems, recv_sems)
        x, y, c = _position()
        me = 2 * x + y
        for wi in range(n):
            for k in (1, 2, 3):
                landed = block(wi, me ^ k, c)
                copy(wi, k - 1, landed, (x, y, c)).wait_recv()
                copy(wi, 2 + k, landed, (x, y, 1 - c)).start()
        for wi in range(n):
            for k in (1, 2, 3):
                copy(wi, 2 + k, block(wi, me ^ k, 1 - c), (x, y, c)).wait_recv()
        for wi in range(n):
            for k in (1, 2, 3):
                copy(wi, k - 1, block(wi, me, c), (x, y, c)).wait_send()
                copy(wi, 2 + k, block(wi, me ^ k, c), (x, y, c)).wait_send()

    return _CommJob(fulls, [jax.ShapeDtypeStruct(f.shape, f.dtype) for f in fulls], {i: i for i in range(n)},
                    6 * n, start, finish)


def _place_shard(w, layer, chip, col_sharded, dtype, name):
    _, R, C = w.shape
    tr = _pick(R, (256, 352, 128, 48))
    nr = R // tr

    def body(chip_ref, w_ref, o_ref):
        o_ref[...] = w_ref[...].astype(dtype)

    if col_sharded:
        shape = (1, R, N_CHIPS * C)
        out_spec = pl.BlockSpec((None, tr, C), lambda r, chip_ref: (0, r, chip_ref[0]))
    else:
        shape = (1, N_CHIPS * R, C)
        out_spec = pl.BlockSpec((None, tr, C), lambda r, chip_ref: (0, chip_ref[0] * nr + r, 0))
    grid_spec = pltpu.PrefetchScalarGridSpec(
        num_scalar_prefetch=1, grid=(nr,),
        in_specs=[pl.BlockSpec((None, tr, C), lambda r, chip_ref: (layer, r, 0))], out_specs=out_spec)
    return pl.pallas_call(
        body, name=name, grid_spec=grid_spec, out_shape=jax.ShapeDtypeStruct(shape, dtype),
        compiler_params=_params(("parallel",)),
    )(chip, w)


def _rs_to_sibling(grads, name):
    n = len(grads)
    outs = [jax.ShapeDtypeStruct((g.shape[0], g.shape[1] // 2, g.shape[2]), g.dtype) for g in grads]

    def body(*refs):
        g_refs, l_refs = refs[:n], refs[n:2 * n]
        send_sems, recv_sems = refs[2 * n:]
        x, y, c = _position()
        cps = []
        for wi in range(n):
            R = grads[wi].shape[1]
            cp = pltpu.make_async_remote_copy(
                src_ref=_half_rows(g_refs[wi], 1 - c, R // 2), dst_ref=l_refs[wi],
                send_sem=send_sems.at[wi], recv_sem=recv_sems.at[wi],
                device_id=(x, y, 1 - c), device_id_type=MESH)
            cp.start()
            cps.append(cp)
        for cp in cps:
            cp.wait()

    return pl.pallas_call(
        body, name=name, out_shape=outs, in_specs=[ANY] * n, out_specs=[ANY] * n,
        scratch_shapes=[pltpu.SemaphoreType.DMA((n,)), pltpu.SemaphoreType.DMA((n,))],
    )(*grads)


def _chip_sum(g, landed, core, name):
    _, R, C = g.shape
    hr = R // 2
    tr = _pick(hr, (256, 352, 128))
    nr = hr // tr

    def body(c_ref, g_ref, l_ref, o_ref):
        o_ref[...] = (g_ref[...] + l_ref[...]).astype(BF16)

    grid_spec = pltpu.PrefetchScalarGridSpec(
        num_scalar_prefetch=1, grid=(N_CHIPS, nr),
        in_specs=[pl.BlockSpec((None, tr, C), lambda j, r, c_ref: (j, c_ref[0] * nr + r, 0)),
                  pl.BlockSpec((None, tr, C), lambda j, r, c_ref: (j, r, 0))],
        out_specs=pl.BlockSpec((None, tr, C), lambda j, r, c_ref: (j, r, 0)))
    return pl.pallas_call(
        body, name=name, grid_spec=grid_spec, out_shape=jax.ShapeDtypeStruct((N_CHIPS, hr, C), BF16),
        compiler_params=_params(("parallel", "parallel")),
    )(core, g, landed)


def _across_job(parts):
    n = len(parts)

    def copy(p_refs, l_refs, send_sems, recv_sems, wi, k, to):
        x, y, _ = _position()
        me = 2 * x + y
        return pltpu.make_async_remote_copy(
            src_ref=p_refs[wi].at[me ^ k], dst_ref=l_refs[wi].at[me],
            send_sem=send_sems.at[3 * wi + k - 1], recv_sem=recv_sems.at[3 * wi + k - 1],
            device_id=to, device_id_type=MESH)

    def start(p_refs, l_refs, send_sems, recv_sems):
        x, y, c = _position()
        for wi in range(n):
            for k in (1, 2, 3):
                copy(p_refs, l_refs, send_sems, recv_sems, wi, k, (*_chip_at(x, y, k), c)).start()

    def finish(p_refs, l_refs, send_sems, recv_sems):
        x, y, c = _position()
        me = 2 * x + y
        for wi in range(n):
            for k in (1, 2, 3):
                slot = l_refs[wi].at[me ^ k]
                pltpu.make_async_remote_copy(
                    src_ref=slot, dst_ref=slot, send_sem=send_sems.at[3 * wi + k - 1],
                    recv_sem=recv_sems.at[3 * wi + k - 1], device_id=(x, y, c), device_id_type=MESH).wait_recv()
        for wi in range(n):
            for k in (1, 2, 3):
                copy(p_refs, l_refs, send_sems, recv_sems, wi, k, (x, y, c)).wait_send()

    return _CommJob(parts, [jax.ShapeDtypeStruct(p.shape, p.dtype) for p in parts], {}, 3 * n, start, finish)


class _Reducer:
    def __init__(self, core):
        self.core, self.parts, self.across, self.pending = core, {}, {}, []

    def ready(self, layer, big):
        names = list(big)
        flat = [big[k] for k in names]
        landed = _rs_to_sibling(flat, f"grads_to_sibling_{names[0]}_{layer}")
        for k, g, la in zip(names, flat, landed):
            self.parts[k, layer] = _chip_sum(g, la, self.core, f"chip_sum_{k}_{layer}")
            self.pending.append((k, layer))

    def take(self):
        keys, self.pending = self.pending, []

        def sink(results):
            self.across.update(zip(keys, results))

        return _across_job([self.parts[key] for key in keys]), sink


def _sum_chips(parts, landed, where, layer, depth, prev, name):
    _, hr, C = landed.shape
    tr = _pick(hr, (256, 352, 128))
    nr = hr // tr

    def body(*refs):
        own_ref, slots, o_ref = refs[1], refs[2:2 + N_CHIPS], refs[-1]
        chip = refs[0][0]
        total = None
        for q in range(N_CHIPS):
            term = jnp.where(chip == q, own_ref[...], slots[q][...]).astype(F32)
            total = term if total is None else total + term
        o_ref[...] = total

    def slot_spec(q):
        return pl.BlockSpec((None, tr, C), lambda r, w: (jnp.where(w[0] == q, (q + 1) % N_CHIPS, q), r, 0))

    in_specs = [pl.BlockSpec((None, tr, C), lambda r, w: (w[0], r, 0))] + [slot_spec(q) for q in range(N_CHIPS)]
    args = [where, parts] + [landed] * N_CHIPS
    aliases = {}
    if prev is not None:
        in_specs.append(ANY)
        args.append(prev)
        aliases = {len(args) - 1: 0}
    grid_spec = pltpu.PrefetchScalarGridSpec(
        num_scalar_prefetch=1, grid=(nr,), in_specs=in_specs,
        out_specs=pl.BlockSpec((None, tr, C), lambda r, w: (layer, w[1] * nr + r, 0)))
    return pl.pallas_call(
        body, name=name, grid_spec=grid_spec, out_shape=jax.ShapeDtypeStruct((depth, 2 * hr, C), F32),
        input_output_aliases=aliases, compiler_params=_params(("parallel",)),
    )(*args)


def _rs_join_halves(reduced):
    n = len(reduced)

    def body(*refs):
        o_refs = refs[n:2 * n]
        send_sems, recv_sems = refs[2 * n:]
        x, y, c = _position()
        sent = []
        for wi in range(n):
            hr = reduced[wi].shape[1] // 2
            mine = _half_rows(o_refs[wi], c, hr)
            cp = pltpu.make_async_remote_copy(
                src_ref=mine, dst_ref=mine, send_sem=send_sems.at[wi], recv_sem=recv_sems.at[wi],
                device_id=(x, y, 1 - c), device_id_type=MESH)
            cp.start()
            sent.append(cp)
        for wi in range(n):
            hr = reduced[wi].shape[1] // 2
            theirs = _half_rows(o_refs[wi], 1 - c, hr)
            pltpu.make_async_remote_copy(
                src_ref=theirs, dst_ref=theirs, send_sem=send_sems.at[wi], recv_sem=recv_sems.at[wi],
                device_id=(x, y, c), device_id_type=MESH).wait_recv()
        for cp in sent:
            cp.wait_send()

    return pl.pallas_call(
        body, name="grads_join_halves", out_shape=[jax.ShapeDtypeStruct(r.shape, r.dtype) for r in reduced],
        in_specs=[ANY] * n, out_specs=[ANY] * n, input_output_aliases={i: i for i in range(n)},
        scratch_shapes=[pltpu.SemaphoreType.DMA((n,)), pltpu.SemaphoreType.DMA((n,))],
    )(*reduced)


def _small_allreduce(vec):
    rows = vec.shape[0]

    def body(v_ref, o_ref, land, send_sems, recv_sems):
        x, y, c = _position()
        me = 4 * x + 2 * y + c
        land[0] = v_ref[...]
        sent = []
        for k in range(1, N_DEV):
            to = (1 - x if k & 4 else x, 1 - y if k & 2 else y, 1 - c if k & 1 else c)
            cp = pltpu.make_async_remote_copy(
                src_ref=v_ref, dst_ref=land.at[k], send_sem=send_sems.at[k - 1], recv_sem=recv_sems.at[k - 1],
                device_id=to, device_id_type=MESH)
            cp.start()
            sent.append(cp)
        for cp in sent:
            cp.wait_recv()
        acc = land[me]
        for e in range(1, N_DEV):
            acc = acc + land[me ^ e]
        o_ref[...] = acc
        for cp in sent:
            cp.wait_send()

    vmem = pl.BlockSpec(memory_space=pltpu.VMEM)
    return pl.pallas_call(
        body, name="small_allreduce", out_shape=jax.ShapeDtypeStruct(vec.shape, F32),
        in_specs=[vmem], out_specs=vmem,
        scratch_shapes=[pltpu.VMEM((N_DEV, rows, LANES), F32), pltpu.SemaphoreType.DMA((N_DEV - 1,)),
                        pltpu.SemaphoreType.DMA((N_DEV - 1,))],
    )(vec)


def _adam_math(w, g, m, v):
    m = ADAM_B1 * m + (1.0 - ADAM_B1) * g
    v = ADAM_B2 * v + (1.0 - ADAM_B2) * jnp.square(g)
    m_hat = m / (1.0 - ADAM_B1 ** ADAM_STEP)
    v_hat = v / (1.0 - ADAM_B2 ** ADAM_STEP)
    delta = -ADAM_LR * (m_hat / (jnp.sqrt(v_hat) + ADAM_EPS) + ADAM_WD * w)
    return delta, m, v


def _adam(w, g, m, v, name):
    def body(w_ref, g_ref, m_ref, v_ref, d_ref, nm_ref, nv_ref):
        d_ref[...], nm_ref[...], nv_ref[...] = _adam_math(w_ref[...], g_ref[...], m_ref[...], v_ref[...])

    if w.ndim == 3:
        lyr, R, C = w.shape
        tr = _pick(R, (256, 352, 128))
        blk = pl.BlockSpec((None, tr, C), lambda l, r: (l, r, 0))
        grid, sem = (lyr, R // tr), ("parallel", "parallel")
    else:
        blk = pl.BlockSpec(w.shape, lambda: (0, 0))
        grid, sem = (), None
    return pl.pallas_call(
        body, name=name, grid=grid, in_specs=[blk] * 4, out_specs=[blk] * 3,
        out_shape=[jax.ShapeDtypeStruct(w.shape, F32)] * 3, compiler_params=_params(sem),
    )(w, g, m, v)


def _rows(a, pad_to=8):
    r = a.reshape(-1, LANES)
    extra = (-r.shape[0]) % pad_to
    return jnp.pad(r, ((0, extra), (0, 0))) if extra else r


def _pack(arrays):
    return jnp.concatenate([_rows(a) for a in arrays], axis=0)


def _unpack(slab, shapes):
    out, at = [], 0
    for shp in shapes:
        nrow = math.prod(shp) // LANES
        out.append(slab[at:at + nrow].reshape(shp))
        at += nrow + (-nrow) % 8
    return out


BIG = ("w_in", "w_out", "w_gate_t", "w_up_t", "w_down")
BIG_COL_SHARDED = (True, False, False, False, False)
TRANSPOSED = {"w_gate_t": "w_gate", "w_up_t": "w_up"}


def kernel(x, meta_tokens, mix_norm_g, w_in, conv_dw_w, conv_dw_b, conv_ln_g, conv_ln_b, w_out, ffn_norm_g, w_gate, w_up, w_down, final_norm_g, loss_target, m_meta_tokens, m_mix_norm_g, m_w_in, m_conv_dw_w, m_conv_dw_b, m_conv_ln_g, m_conv_ln_b, m_w_out, m_ffn_norm_g, m_w_gate, m_w_up, m_w_down, m_final_norm_g, v_meta_tokens, v_mix_norm_g, v_w_in, v_conv_dw_w, v_conv_dw_b, v_conv_ln_g, v_conv_ln_b, v_w_out, v_ffn_norm_g, v_w_gate, v_w_up, v_w_down, v_final_norm_g):
    n_meta, seq = meta_tokens.shape[0], x.shape[1]
    D = x.shape[2]
    depth, taps, c_shard = conv_dw_w.shape
    C = conv_dw_b.shape[-1]
    chip = (2 * lax.axis_index("x") + lax.axis_index("y")).astype(jnp.int32)
    core = lax.axis_index("c").astype(jnp.int32).reshape(1)
    chip1 = chip.reshape(1)
    where = jnp.concatenate([chip1, core])
    big_w = dict(w_in=w_in, w_out=w_out, w_gate=w_gate, w_up=w_up, w_down=w_down)
    big_m = dict(w_in=m_w_in, w_out=m_w_out, w_gate=m_w_gate, w_up=m_w_up, w_down=m_w_down)
    big_v = dict(w_in=v_w_in, w_out=v_w_out, w_gate=v_w_gate, w_up=v_w_up, w_down=v_w_down)

    small_shard = _pack([conv_dw_w, meta_tokens])[None]
    to_send = {k: jnp.swapaxes(big_w[TRANSPOSED[k]], 1, 2) if k in TRANSPOSED else big_w[k] for k in BIG}
    col = dict(zip(BIG, BIG_COL_SHARDED))
    wts = {k: [_place_shard(to_send[k], l, chip1, col[k], BF16, f"place_{k}_{l}") for l in range(depth)] for k in BIG}
    small_placed = _place_shard(small_shard, 0, chip1, False, F32, "place_small")

    def gather_job(keys, extra=()):
        arrays = [wts[k][l] for k, l in keys] + list(extra)
        shapes = [(1,) + to_send[k].shape[1:] for k, _ in keys] + [(1,) + small_shard.shape[1:]] * len(extra)
        return _gather_job(arrays, shapes, [col[k] for k, _ in keys] + [False] * len(extra))

    first_keys = [("w_in", 0), ("w_out", 0)]
    *first, small_full = _run_job(gather_job(first_keys, [small_placed]), "gather_first")
    for (k, l), arr in zip(first_keys, first):
        wts[k][l] = arr
    ffn = lambda l: [("w_gate_t", l), ("w_up_t", l), ("w_down", l)]
    behind_keys = {0: ffn(0) + [("w_in", l) for l in range(1, depth)] + [("w_out", l) for l in range(1, depth)]}
    for l in range(1, depth):
        behind_keys[l] = ffn(l)
    gather_behind = {l: (gather_job(keys), keys) for l, keys in behind_keys.items()}

    rows_shard = small_shard.shape[1]
    dw_full, meta_full = [], []
    for j in range(N_CHIPS):
        dwj, mj = _unpack(small_full[0, j * rows_shard:(j + 1) * rows_shard],
                          [conv_dw_w.shape, meta_tokens.shape])
        dw_full.append(dwj)
        meta_full.append(mj)
    dw_w_full = jnp.concatenate(dw_full, axis=2)
    meta = jnp.concatenate(meta_full, axis=1)

    L = n_meta + seq
    Lp = -(-L // QUERY_BLOCK) * QUERY_BLOCK
    h0 = jnp.concatenate([meta, x[0], jnp.zeros((Lp - L, D), F32)], axis=0)
    target = jnp.pad(loss_target[0], ((n_meta, Lp - L), (0, 0)))
    reducer = _Reducer(core)
    loss, dh0, grads = _local_step(h0, target, n_meta, seq, (mix_norm_g, ffn_norm_g),
                                   (dw_w_full, conv_dw_b, conv_ln_g, conv_ln_b), wts, final_norm_g,
                                   gather_behind, reducer)
    loss = lax.psum(loss[0, 0], ("x", "y", "c"))
    grad_x = dh0[n_meta:L][None]
    job, sink = reducer.take()
    sink(_run_job(job, "grads_across_last"))

    reduced = []
    for k in BIG:
        arr = None
        for l in range(depth):
            arr = _sum_chips(reducer.parts[k, l], reducer.across[k, l], where, l, depth, arr, f"sum_chips_{k}_{l}")
        reduced.append(arr)
    big_g = dict(zip(BIG, _rs_join_halves(reduced)))

    small_names = ("mix_g", "ffn_g", "dw_b", "ln_g", "ln_b", "dw_w")
    small = [jnp.stack(grads[k]) for k in small_names] + [grads["final_g"], dh0[:n_meta]]
    small_shapes = [a.shape for a in small]
    g_mix, g_ffn, g_dwb, g_lng, g_lnb, g_dww, g_final, g_meta = _unpack(_small_allreduce(_pack(small)), small_shapes)
    g_dww = lax.dynamic_slice_in_dim(g_dww, chip * c_shard, c_shard, axis=2)
    g_meta = lax.dynamic_slice_in_dim(g_meta, chip * meta_tokens.shape[1], meta_tokens.shape[1], axis=1)

    out_g, out_d, out_m, out_v = {}, {}, {}, {}
    for kk in BIG:
        k = TRANSPOSED.get(kk, kk)
        view = (lambda a: jnp.swapaxes(a, 1, 2)) if kk in TRANSPOSED else (lambda a: a)
        res = _adam(view(big_w[k]), big_g[kk], view(big_m[k]), view(big_v[k]), f"adam_{k}")
        out_g[k] = view(big_g[kk])
        out_d[k], out_m[k], out_v[k] = (view(a) for a in res)
    small_order = ("meta_tokens", "mix_norm_g", "conv_dw_w", "conv_dw_b", "conv_ln_g", "conv_ln_b",
                   "ffn_norm_g", "final_norm_g")
    sw = dict(meta_tokens=meta_tokens, mix_norm_g=mix_norm_g, conv_dw_w=conv_dw_w, conv_dw_b=conv_dw_b,
              conv_ln_g=conv_ln_g, conv_ln_b=conv_ln_b, ffn_norm_g=ffn_norm_g, final_norm_g=final_norm_g)
    sm = dict(meta_tokens=m_meta_tokens, mix_norm_g=m_mix_norm_g, conv_dw_w=m_conv_dw_w, conv_dw_b=m_conv_dw_b,
              conv_ln_g=m_conv_ln_g, conv_ln_b=m_conv_ln_b, ffn_norm_g=m_ffn_norm_g, final_norm_g=m_final_norm_g)
    sv = dict(meta_tokens=v_meta_tokens, mix_norm_g=v_mix_norm_g, conv_dw_w=v_conv_dw_w, conv_dw_b=v_conv_dw_b,
              conv_ln_g=v_conv_ln_g, conv_ln_b=v_conv_ln_b, ffn_norm_g=v_ffn_norm_g, final_norm_g=v_final_norm_g)
    sg = dict(meta_tokens=g_meta, mix_norm_g=g_mix, conv_dw_w=g_dww, conv_dw_b=g_dwb, conv_ln_g=g_lng,
              conv_ln_b=g_lnb, ffn_norm_g=g_ffn, final_norm_g=g_final)
    def slab(d):
        return _pack([d[k] for k in small_order])
    shapes = [sw[k].shape for k in small_order]
    deltas = _adam(slab(sw), slab(sg), slab(sm), slab(sv), "adam_small")
    for res, dst in zip(deltas, (out_d, out_m, out_v)):
        dst.update(zip(small_order, _unpack(res, shapes)))
    out_g.update(sg)

    order = ("meta_tokens", "mix_norm_g", "w_in", "conv_dw_w", "conv_dw_b", "conv_ln_g", "conv_ln_b", "w_out",
             "ffn_norm_g", "w_gate", "w_up", "w_down", "final_norm_g")
    return (loss, grad_x, *[out_g[k] for k in order], *[out_d[k] for k in order],
            *[out_m[k] for k in order], *[out_v[k] for k in order])
```

```python
import functools
import math

import jax
import jax.numpy as jnp
from jax import lax
from jax.experimental import pallas as pl
from jax.experimental.pallas import tpu as pltpu

F32 = jnp.float32
BF16 = jnp.bfloat16
MESH = pl.DeviceIdType.MESH

EPS = 1e-6
QUERY_BLOCK = 128
LANES = 128
HEAD_DIM = 64
LOG_STICK_FLOOR = -40.0
CONV_HALO = 32
N_CHIPS = 4
N_DEV = 8
VMEM_LIMIT = 56 * 1024 * 1024

ADAM_LR = 0.001
ADAM_B1 = 0.9
ADAM_B2 = 0.999
ADAM_EPS = 1e-08
ADAM_WD = 0.01
ADAM_STEP = 10


def _pick(n, prefs):
    for p in prefs:
        if n % p == 0:
            return p
    return n


def _params(sem=None):
    return pltpu.CompilerParams(dimension_semantics=sem, vmem_limit_bytes=VMEM_LIMIT)


def _sigmoid(x):
    return 1.0 / (1.0 + jnp.exp(-x))


def _rmsnorm_fwd(h, g, name):
    L, D = h.shape
    T = _pick(L, (384, 128))

    def body(h_ref, g_ref, o_ref):
        x = h_ref[...]
        r = lax.rsqrt(jnp.mean(x * x, axis=-1, keepdims=True) + EPS)
        o_ref[...] = (x * r * g_ref[...]).astype(o_ref.dtype)

    return pl.pallas_call(
        body, name=name, grid=(L // T,),
        in_specs=[pl.BlockSpec((T, D), lambda i: (i, 0)), pl.BlockSpec((1, D), lambda i: (0, 0))],
        out_specs=pl.BlockSpec((T, D), lambda i: (i, 0)),
        out_shape=jax.ShapeDtypeStruct((L, D), BF16),
        compiler_params=_params(("parallel",)),
    )(h, g)


def _rmsnorm_bwd(h, g, dy, dh_in, name):
    L, D = h.shape
    T = _pick(L, (384, 128))

    def body(h_ref, g_ref, dy_ref, dhin_ref, dh_ref, dg_ref):
        x = h_ref[...]
        dyv = dy_ref[...]
        r = lax.rsqrt(jnp.mean(x * x, axis=-1, keepdims=True) + EPS)
        xh = x * r
        dxh = dyv * g_ref[...]
        dh_ref[...] = dhin_ref[...] + r * (dxh - xh * jnp.mean(dxh * xh, axis=-1, keepdims=True))

        @pl.when(pl.program_id(0) == 0)
        def _():
            dg_ref[...] = jnp.zeros_like(dg_ref)

        dg_ref[...] += jnp.sum(dyv * xh, axis=0, keepdims=True)

    row = pl.BlockSpec((T, D), lambda i: (i, 0))
    vec = pl.BlockSpec((1, D), lambda i: (0, 0))
    return pl.pallas_call(
        body, name=name, grid=(L // T,),
        in_specs=[row, vec, row, row], out_specs=[row, vec],
        out_shape=[jax.ShapeDtypeStruct((L, D), F32), jax.ShapeDtypeStruct((1, D), F32)],
        compiler_params=_params(("arbitrary",)),
    )(h, g, dy, dh_in)


def _loss_head(h, g, target, n_meta, seq, name):
    L, D = h.shape
    T = _pick(L, (384, 128))

    def body(h_ref, g_ref, t_ref, loss_ref, dh_ref, dg_ref):
        i = pl.program_id(0)
        x = h_ref[...]
        gv = g_ref[...]
        r = lax.rsqrt(jnp.mean(x * x, axis=-1, keepdims=True) + EPS)
        xh = x * r
        y = xh * gv
        rows = i * T + lax.broadcasted_iota(jnp.int32, (T, 1), 0)
        live = (rows >= n_meta) & (rows < n_meta + seq)
        diff = jnp.where(live, y - t_ref[...], 0.0)
        dyv = diff / D
        dxh = dyv * gv
        dh_ref[...] = r * (dxh - xh * jnp.mean(dxh * xh, axis=-1, keepdims=True))

        @pl.when(i == 0)
        def _():
            dg_ref[...] = jnp.zeros_like(dg_ref)
            loss_ref[...] = jnp.zeros_like(loss_ref)

        dg_ref[...] += jnp.sum(dyv * xh, axis=0, keepdims=True)
        per_row = jnp.mean(diff * diff, axis=-1, keepdims=True)
        loss_ref[...] += 0.5 * jnp.sum(per_row, axis=0, keepdims=True)

    row = pl.BlockSpec((T, D), lambda i: (i, 0))
    vec = pl.BlockSpec((1, D), lambda i: (0, 0))
    one = pl.BlockSpec((1, 1), lambda i: (0, 0))
    return pl.pallas_call(
        body, name=name, grid=(L // T,),
        in_specs=[row, vec, row], out_specs=[one, row, vec],
        out_shape=[jax.ShapeDtypeStruct((1, 1), F32), jax.ShapeDtypeStruct((L, D), F32),
                   jax.ShapeDtypeStruct((1, D), F32)],
        compiler_params=_params(("arbitrary",)),
    )(h, g, target)


def _ffn_tiles(M, F):
    return _pick(M, (704, 384, 128)), _pick(F, (1408, 512, 256, 128))


def _ffn_up(hn, w_gate_t, w_up_t, name):
    M, D = hn.shape
    F = w_gate_t.shape[1]
    tm, tf = _ffn_tiles(M, F)
    nt = (((1,), (1,)), ((), ()))

    def body(h_ref, wg_ref, wu_ref, g_ref, u_ref, a_ref):
        hv = h_ref[...]
        gv = lax.dot_general(hv, wg_ref[...], nt, preferred_element_type=F32)
        uv = lax.dot_general(hv, wu_ref[...], nt, preferred_element_type=F32)
        g_ref[...] = gv
        u_ref[...] = uv
        a_ref[...] = (gv * _sigmoid(gv) * uv).astype(a_ref.dtype)

    w_spec = pl.BlockSpec((None, tf, D), lambda i, j: (0, j, 0))
    out = pl.BlockSpec((tm, tf), lambda i, j: (i, j))
    return pl.pallas_call(
        body, name=name, grid=(M // tm, F // tf),
        in_specs=[pl.BlockSpec((tm, D), lambda i, j: (i, 0)), w_spec, w_spec], out_specs=[out, out, out],
        out_shape=[jax.ShapeDtypeStruct((M, F), F32), jax.ShapeDtypeStruct((M, F), F32),
                   jax.ShapeDtypeStruct((M, F), BF16)],
        compiler_params=_params(("parallel", "parallel")),
    )(hn, w_gate_t, w_up_t)


def _ffn_down_bwd(dh, w_down, g, u, name):
    M, D = dh.shape
    F = g.shape[1]
    tm, tf = _ffn_tiles(M, F)

    def body(d_ref, w_ref, g_ref, u_ref, dg_ref, du_ref):
        dv = lax.dot_general(d_ref[...].astype(BF16), w_ref[...], (((1,), (1,)), ((), ())),
                             preferred_element_type=F32)
        gv = g_ref[...]
        s = _sigmoid(gv)
        du_ref[...] = (dv * (gv * s)).astype(du_ref.dtype)
        dg_ref[...] = (dv * u_ref[...] * (s * (1.0 + gv * (1.0 - s)))).astype(dg_ref.dtype)

    tile = pl.BlockSpec((tm, tf), lambda i, j: (i, j))
    return pl.pallas_call(
        body, name=name, grid=(M // tm, F // tf),
        in_specs=[pl.BlockSpec((tm, D), lambda i, j: (i, 0)), pl.BlockSpec((None, tf, D), lambda i, j: (0, j, 0)),
                  tile, tile],
        out_specs=[tile, tile],
        out_shape=[jax.ShapeDtypeStruct((M, F), BF16), jax.ShapeDtypeStruct((M, F), BF16)],
        compiler_params=_params(("parallel", "parallel")),
    )(dh, w_down, g, u)


def _mm_nn(pairs, out_dtype, name, residual=None, cols=None):
    M = pairs[0][0].shape[0]
    col0, N = cols if cols is not None else (0, pairs[0][1].shape[-1])
    tm = _pick(M, (1056, 384, 128))
    tn = _pick(math.gcd(N, col0) if col0 else N, (640, 512, 256, 128))
    jb = col0 // tn
    n = len(pairs)

    def body(*refs):
        a_refs, w_refs = refs[:n], refs[n:2 * n]
        o_ref = refs[-1]
        acc = None
        for a_ref, w_ref in zip(a_refs, w_refs):
            d = jnp.dot(a_ref[...].astype(BF16), w_ref[...], preferred_element_type=F32)
            acc = d if acc is None else acc + d
        if residual is not None:
            acc = acc + refs[2 * n][...]
        o_ref[...] = acc.astype(o_ref.dtype)

    in_specs = [pl.BlockSpec((tm, a.shape[1]), lambda i, j: (i, 0)) for a, _, _, _ in pairs]
    for a, _, layer, kblk in pairs:
        in_specs.append(pl.BlockSpec((None, a.shape[1], tn), functools.partial(lambda i, j, l, kb: (l, kb, j + jb), l=layer, kb=kblk)))
    args = [p[0] for p in pairs] + [p[1] for p in pairs]
    if residual is not None:
        in_specs.append(pl.BlockSpec((tm, tn), lambda i, j: (i, j)))
        args.append(residual)
    return pl.pallas_call(
        body, name=name, grid=(M // tm, N // tn), in_specs=in_specs,
        out_specs=pl.BlockSpec((tm, tn), lambda i, j: (i, j)),
        out_shape=jax.ShapeDtypeStruct((M, N), out_dtype),
        compiler_params=_params(("parallel", "parallel")),
    )(*args)


def _mm_nt(pairs, out_dtype, name):
    M = pairs[0][0].shape[0]
    K = pairs[0][1].shape[1]
    tm = _pick(M, (1056, 384, 128))
    tk = _pick(K, (512, 1408, 256, 128))
    n = len(pairs)

    def body(*refs):
        d_refs, w_refs = refs[:n], refs[n:2 * n]
        o_ref = refs[-1]
        acc = None
        for d_ref, w_ref in zip(d_refs, w_refs):
            d = lax.dot_general(d_ref[...].astype(BF16), w_ref[...], (((1,), (1,)), ((), ())),
                                preferred_element_type=F32)
            acc = d if acc is None else acc + d
        o_ref[...] = acc.astype(o_ref.dtype)

    in_specs = [pl.BlockSpec((tm, d.shape[1]), lambda i, j: (i, 0)) for d, _, _, _ in pairs]
    for d, _, layer, cblk in pairs:
        in_specs.append(pl.BlockSpec((None, tk, d.shape[1]), functools.partial(lambda i, j, l, cb: (l, j, cb), l=layer, cb=cblk)))
    args = [p[0] for p in pairs] + [p[1] for p in pairs]
    return pl.pallas_call(
        body, name=name, grid=(M // tm, K // tk), in_specs=in_specs,
        out_specs=pl.BlockSpec((tm, tk), lambda i, j: (i, j)),
        out_shape=jax.ShapeDtypeStruct((M, K), out_dtype),
        compiler_params=_params(("parallel", "parallel")),
    )(*args)


def _mm_tn(a, b, name, col_sharded):
    M, K = a.shape
    N = b.shape[1]
    tm = _pick(M, (1056, 384, 128))
    tk = _pick(K, (512, 1408, 256, 128))
    tn = N // N_CHIPS if col_sharded else _pick(N, (512, 128))

    def body(a_ref, b_ref, o_ref):
        @pl.when(pl.program_id(2) == 0)
        def _():
            o_ref[...] = jnp.zeros_like(o_ref)

        o_ref[...] += lax.dot_general(a_ref[...].astype(BF16), b_ref[...].astype(BF16),
                                      (((0,), (0,)), ((), ())), preferred_element_type=F32)

    if col_sharded:
        out_shape = jax.ShapeDtypeStruct((N_CHIPS, K, tn), F32)
        out_spec = pl.BlockSpec((None, tk, tn), lambda k, j, m: (j, k, 0))
    else:
        out_shape = jax.ShapeDtypeStruct((K, N), F32)
        out_spec = pl.BlockSpec((tk, tn), lambda k, j, m: (k, j))
    out = pl.pallas_call(
        body, name=name, grid=(K // tk, N // tn, M // tm),
        in_specs=[pl.BlockSpec((tm, tk), lambda k, j, m: (m, k)), pl.BlockSpec((tm, tn), lambda k, j, m: (m, j))],
        out_specs=out_spec, out_shape=out_shape,
        compiler_params=_params(("parallel", "parallel", "arbitrary")),
    )(a, b)
    return out if col_sharded else out.reshape(N_CHIPS, K // N_CHIPS, N)


def _stack_heads(x, scale=None):
    lane = lax.broadcasted_iota(jnp.int32, x.shape, 1)
    zero = jnp.zeros_like(x)
    lo = jnp.where(lane < HEAD_DIM, x, zero)
    hi = jnp.where(lane < HEAD_DIM, zero, x)
    out = jnp.concatenate([lo, hi], axis=0)
    return out if scale is None else out * scale


def _unstack_heads(x2):
    qb = x2.shape[0] // 2
    lane = lax.broadcasted_iota(jnp.int32, (qb, LANES), 1)
    return jnp.where(lane < HEAD_DIM, x2[:qb], x2[qb:])


def _tri_and_ones(strict, keys=QUERY_BLOCK):
    r = lax.broadcasted_iota(jnp.int32, (keys, keys + LANES), 0)
    c = lax.broadcasted_iota(jnp.int32, (keys, keys + LANES), 1)
    tri = (r > c) if strict else (r >= c)
    return jnp.where(tri | (c >= keys), 1.0, 0.0).astype(BF16)


def _dot2(x, m):
    xh = x.astype(BF16)
    xl = (x - xh.astype(F32)).astype(BF16)
    return jnp.dot(xh, m, preferred_element_type=F32) + jnp.dot(xl, m, preferred_element_type=F32)


def _causal_valid(older=0):
    r = lax.broadcasted_iota(jnp.int32, (2 * QUERY_BLOCK, older + QUERY_BLOCK), 0) & (QUERY_BLOCK - 1)
    c = lax.broadcasted_iota(jnp.int32, (2 * QUERY_BLOCK, older + QUERY_BLOCK), 1)
    return c < r + older


def _sweep_older(i, step, carry_ref, first):
    def cond(state):
        n, live = state
        return jnp.logical_and(n < i, live)

    def older(state):
        n, _ = state
        step(i - 1 - n, False)
        return n + 1, jnp.max(carry_ref[...]) > LOG_STICK_FLOOR

    lax.while_loop(cond, older, (first, jnp.max(carry_ref[...]) > LOG_STICK_FLOOR))


class _CommJob:
    def __init__(self, inputs, out_shapes, aliases, n_sems, start, finish):
        self.inputs, self.out_shapes, self.aliases, self.n_sems = list(inputs), list(out_shapes), aliases, n_sems
        self.start, self.finish = start, finish


def _call_with_job(core_body, job, *, name, grid, in_specs, out_specs, out_shape, scratch_shapes, args):
    sem = ("arbitrary",) * len(grid)
    if job is None:
        res = pl.pallas_call(core_body, name=name, grid=grid, in_specs=in_specs, out_specs=out_specs,
                             out_shape=out_shape, scratch_shapes=scratch_shapes, compiler_params=_params(sem))(*args)
        return list(res), []
    n_in, n_out, n_scr = len(in_specs), len(out_specs), len(scratch_shapes)
    m_in, m_out = len(job.inputs), len(job.out_shapes)

    def body(*refs):
        at = 0
        parts = []
        for count in (n_in, m_in, n_out, m_out, n_scr, 2):
            parts.append(refs[at:at + count])
            at += count
        ins, job_in, outs, job_outs, scratch, (send_sems, recv_sems) = parts
        first = functools.reduce(jnp.logical_and, [pl.program_id(a) == 0 for a in range(len(grid))])
        last = functools.reduce(jnp.logical_and, [pl.program_id(a) == grid[a] - 1 for a in range(len(grid))])

        @pl.when(first)
        def _():
            job.start(job_in, job_outs, send_sems, recv_sems)

        core_body(*ins, *outs, *scratch)

        @pl.when(last)
        def _():
            job.finish(job_in, job_outs, send_sems, recv_sems)

    res = pl.pallas_call(
        body, name=name, grid=grid, in_specs=list(in_specs) + [ANY] * m_in, out_specs=list(out_specs) + [ANY] * m_out,
        out_shape=list(out_shape) + job.out_shapes,
        input_output_aliases={n_in + a: n_out + b for a, b in job.aliases.items()},
        scratch_shapes=list(scratch_shapes) + [pltpu.SemaphoreType.DMA((job.n_sems,)), pltpu.SemaphoreType.DMA((job.n_sems,))],
        compiler_params=_params(sem),
    )(*args, *job.inputs)
    return list(res[:n_out]), list(res[n_out:])


def _run_job(job, name):
    m_in, m_out = len(job.inputs), len(job.out_shapes)

    def body(*refs):
        job_in, job_outs = refs[:m_in], refs[m_in:m_in + m_out]
        send_sems, recv_sems = refs[m_in + m_out:]
        job.start(job_in, job_outs, send_sems, recv_sems)
        job.finish(job_in, job_outs, send_sems, recv_sems)

    return list(pl.pallas_call(
        body, name=name, in_specs=[ANY] * m_in, out_specs=[ANY] * m_out, out_shape=job.out_shapes,
        input_output_aliases=dict(job.aliases),
        scratch_shapes=[pltpu.SemaphoreType.DMA((job.n_sems,)), pltpu.SemaphoreType.DMA((job.n_sems,))],
    )(*job.inputs))


def _attn_fwd(qkv, sb_width, name, job=None):
    L = qkv.shape[0]
    QB = QUERY_BLOCK
    nb = L // QB
    n_pairs = sb_width // LANES
    scale = 1.0 / math.sqrt(HEAD_DIM)

    def body(q_ref, k_ref, v_ref, o_ref, acc_ref, carry_ref):
        i = pl.program_id(1)
        q2 = _stack_heads(q_ref[...], scale)
        acc_ref[...] = jnp.zeros_like(acc_ref)
        carry_ref[...] = jnp.zeros_like(carry_ref)

        def step(j, older, mask):
            n = older + QB
            start = pl.multiple_of(j * QB, QB)
            kb = k_ref[pl.ds(start, n), :]
            vb = v_ref[pl.ds(start, n), :]
            z = lax.dot_general(q2, kb, (((1,), (1,)), ((), ())), preferred_element_type=F32)
            sp = jnp.log(1.0 + jnp.exp(-jnp.abs(z)))
            a = jnp.minimum(z, 0.0) - sp
            b = jnp.minimum(-z, 0.0) - sp
            if mask is not None:
                b = jnp.where(mask, b, 0.0)
            res = _dot2(b, _tri_and_ones(True, n))
            excl = res[:, :n]
            if mask is None:
                excl = excl + carry_ref[...]
            w = jnp.exp(a + excl)
            if mask is not None:
                w = jnp.where(mask, w, 0.0)
            carry_ref[...] += res[:, n:]
            acc_ref[...] += _dot2(w, vb)

        @pl.when(i == 0)
        def _():
            step(0, 0, _causal_valid())

        @pl.when(i > 0)
        def _():
            step(i - 1, QB, _causal_valid(QB))
            _sweep_older(i, lambda j, _: step(j, 0, None), carry_ref, 1)

        o_ref[...] = _unstack_heads(acc_ref[...])

    (out,), job_out = _call_with_job(
        body, job, name=name, grid=(n_pairs, nb),
        in_specs=[pl.BlockSpec((QB, LANES), lambda p, i: (i, p)),
                  pl.BlockSpec((L, LANES), lambda p, i: (0, n_pairs + p)),
                  pl.BlockSpec((L, LANES), lambda p, i: (0, 2 * n_pairs + p))],
        out_specs=[pl.BlockSpec((QB, LANES), lambda p, i: (i, p))],
        out_shape=[jax.ShapeDtypeStruct((L, sb_width), F32)],
        scratch_shapes=[pltpu.VMEM((2 * QB, LANES), F32), pltpu.VMEM((2 * QB, LANES), F32)],
        args=(qkv, qkv, qkv))
    return out, job_out


def _attn_bwd(qkv, o, dmixed, sb_width, name, job=None):
    L = qkv.shape[0]
    QB = QUERY_BLOCK
    nb = L // QB
    n_pairs = sb_width // LANES
    scale = 1.0 / math.sqrt(HEAD_DIM)

    def body(q_ref, k_ref, v_ref, o_ref, do_ref, dq_ref, dk_ref, dv_ref,
             dq_acc, dk_acc, dv_acc, ce_ref, cr_ref):
        i = pl.program_id(1)

        @pl.when(i == 0)
        def _():
            dk_acc[...] = jnp.zeros_like(dk_acc)
            dv_acc[...] = jnp.zeros_like(dv_acc)

        q2 = _stack_heads(q_ref[...], scale)
        do2 = _stack_heads(do_ref[...].astype(BF16))
        ov = o_ref[...]
        o2 = jnp.concatenate([ov, ov], axis=0)
        ones = jnp.ones((LANES, 2 * QB), BF16)
        dtot2 = _dot2(do2.astype(F32) * o2, ones)
        dq_acc[...] = jnp.zeros_like(dq_acc)
        ce_ref[...] = jnp.zeros_like(ce_ref)
        cr_ref[...] = jnp.zeros_like(cr_ref)

        def step(j, older, mask):
            n = older + QB
            start = pl.multiple_of(j * QB, QB)
            kb = k_ref[pl.ds(start, n), :]
            vb = v_ref[pl.ds(start, n), :]
            z = lax.dot_general(q2, kb, (((1,), (1,)), ((), ())), preferred_element_type=F32)
            e = jnp.exp(-jnp.abs(z))
            sp = jnp.log(1.0 + e)
            a = jnp.minimum(z, 0.0) - sp
            b = jnp.minimum(-z, 0.0) - sp
            rinv = 1.0 / (1.0 + e)
            pos = z >= 0.0
            beta = jnp.where(pos, rinv, e * rinv)
            one_m_beta = jnp.where(pos, e * rinv, rinv)
            if mask is not None:
                b = jnp.where(mask, b, 0.0)
            dw = lax.dot_general(do2, vb, (((1,), (1,)), ((), ())), preferred_element_type=F32)
            res = _dot2(b, _tri_and_ones(True, n))
            excl = res[:, :n]
            if mask is None:
                excl = excl + ce_ref[...]
            w = jnp.exp(a + excl)
            if mask is not None:
                w = jnp.where(mask, w, 0.0)
            ce_ref[...] += res[:, n:]
            g = w * dw
            res2 = _dot2(g, _tri_and_ones(False, n))
            rinc = res2[:, :n]
            if mask is None:
                rinc = rinc + cr_ref[...]
            cr_ref[...] += res2[:, n:]
            dz = g * one_m_beta - beta * (dtot2[:, :n] - rinc)
            if mask is not None:
                dz = jnp.where(mask, dz, 0.0)
            dzb = dz.astype(BF16)
            dq_acc[...] += jnp.dot(dzb, kb, preferred_element_type=F32)
            dk_acc[pl.ds(start, n), :] += lax.dot_general(
                dzb, q2, (((0,), (0,)), ((), ())), preferred_element_type=F32)
            dv_acc[pl.ds(start, n), :] += lax.dot_general(
                w.astype(BF16), do2, (((0,), (0,)), ((), ())), preferred_element_type=F32)

        @pl.when(i == 0)
        def _():
            step(0, 0, _causal_valid())

        @pl.when(i > 0)
        def _():
            step(i - 1, QB, _causal_valid(QB))
            _sweep_older(i, lambda j, _: step(j, 0, None), ce_ref, 1)

        dq_ref[...] = (_unstack_heads(dq_acc[...]) * scale).astype(dq_ref.dtype)

        @pl.when(i == nb - 1)
        def _():
            dk_ref[...] = dk_acc[...].astype(dk_ref.dtype)
            dv_ref[...] = dv_acc[...].astype(dv_ref.dtype)

    blk = pl.BlockSpec((QB, LANES), lambda p, i: (i, p))
    col = pl.BlockSpec((L, LANES), lambda p, i: (0, p))
    return _call_with_job(
        body, job, name=name, grid=(n_pairs, nb),
        in_specs=[blk,
                  pl.BlockSpec((L, LANES), lambda p, i: (0, n_pairs + p)),
                  pl.BlockSpec((L, LANES), lambda p, i: (0, 2 * n_pairs + p)),
                  blk, blk],
        out_specs=[blk, col, col],
        out_shape=[jax.ShapeDtypeStruct((L, sb_width), BF16)] * 3,
        scratch_shapes=[pltpu.VMEM((2 * QB, LANES), F32), pltpu.VMEM((L, LANES), F32),
                        pltpu.VMEM((L, LANES), F32), pltpu.VMEM((2 * QB, LANES), F32),
                        pltpu.VMEM((2 * QB, LANES), F32)],
        args=(qkv, qkv, qkv, o, dmixed))


def _conv_tile(L):
    return _pick(L, (384, 128))


def _glu(x, C):
    return x[:, :C] * _sigmoid(x[:, C:])


CONV_CHUNK = 32
SHIFT_TAIL = 24


def _fill_shifted(src_ref, dst_ref):
    n = dst_ref.shape[1]
    for r in range(1, 8):
        dst_ref[r - 1] = src_ref[r:r + n, :]


def _rows_at(src_ref, shifted_ref, start, n):
    q, r = divmod(start, 8)
    if r == 0:
        return src_ref[start:start + n, :]
    return shifted_ref[r - 1, 8 * q:8 * q + n, :]


def _conv_fwd(cacg, dw_w, dw_b, ln_g, ln_b, name):
    L, C2 = cacg.shape
    C = C2 // 2
    T = _conv_tile(L)
    H = CONV_HALO
    K = dw_w.shape[0]
    CH = CONV_CHUNK

    def body(x_ref, prev_ref, w_ref, b_ref, g_ref, beta_ref, o_ref, u_ref, us_ref):
        i = pl.program_id(0)
        u_ref[0:H, :] = jnp.where(i > 0, _glu(prev_ref[...], C), 0.0)
        u_ref[H:, :] = _glu(x_ref[...], C)
        _fill_shifted(u_ref, us_ref)
        for c0 in range(0, T, CH):
            y = jnp.broadcast_to(b_ref[...], (CH, C))
            for k in range(K):
                y = y + w_ref[k:k + 1, :] * _rows_at(u_ref, us_ref, c0 + H - (K - 1) + k, CH)
            mu = jnp.mean(y, axis=-1, keepdims=True)
            yc = y - mu
            rstd = lax.rsqrt(jnp.mean(yc * yc, axis=-1, keepdims=True) + EPS)
            ln = yc * rstd * g_ref[...] + beta_ref[...]
            o_ref[c0:c0 + CH, :] = (ln * _sigmoid(ln)).astype(o_ref.dtype)

    vec = pl.BlockSpec((1, C), lambda i: (0, 0))
    return pl.pallas_call(
        body, name=name, grid=(L // T,),
        in_specs=[pl.BlockSpec((T, C2), lambda i: (i, 0)),
                  pl.BlockSpec((H, C2), lambda i: (jnp.maximum(i * (T // H) - 1, 0), 0)),
                  pl.BlockSpec((K, C), lambda i: (0, 0)), vec, vec, vec],
        out_specs=pl.BlockSpec((T, C), lambda i: (i, 0)),
        out_shape=jax.ShapeDtypeStruct((L, C), BF16),
        scratch_shapes=[pltpu.VMEM((T + H, C), F32), pltpu.VMEM((7, T + SHIFT_TAIL, C), F32)],
        compiler_params=_params(("parallel",)),
    )(cacg, cacg, dw_w, dw_b, ln_g, ln_b)


def _conv_bwd(cacg, dmixed, dw_w, dw_b, ln_g, ln_b, name):
    L, C2 = cacg.shape
    C = C2 // 2
    T = _conv_tile(L)
    H = CONV_HALO
    K = dw_w.shape[0]
    nt = L // T
    TE = T + H

    CH = CONV_CHUNK

    def body(x_ref, prev_ref, next_ref, d_ref, dnext_ref, w_ref, b_ref, g_ref, beta_ref,
             dca_ref, dcg_ref, dwt_ref, db_ref, dg_ref, dbeta_ref, u_ref, us_ref, dy_ref, dys_ref):
        i = pl.program_id(0)
        last = i == nt - 1

        @pl.when(i == 0)
        def _():
            dwt_ref[...] = jnp.zeros_like(dwt_ref)
            db_ref[...] = jnp.zeros_like(db_ref)
            dg_ref[...] = jnp.zeros_like(dg_ref)
            dbeta_ref[...] = jnp.zeros_like(dbeta_ref)

        u_ref[0:H, :] = jnp.where(i > 0, _glu(prev_ref[...], C), 0.0)
        u_ref[H:H + T, :] = _glu(x_ref[...], C)
        u_ref[H + T:, :] = _glu(next_ref[...], C)
        _fill_shifted(u_ref, us_ref)
        dg_acc = jnp.zeros((1, C), F32)
        dbeta_acc = jnp.zeros((1, C), F32)
        db_acc = jnp.zeros((1, C), F32)
        for c0 in range(0, TE, CH):
            y = jnp.broadcast_to(b_ref[...], (CH, C))
            for k in range(K):
                y = y + w_ref[k:k + 1, :] * _rows_at(u_ref, us_ref, c0 + H - (K - 1) + k, CH)
            mu = jnp.mean(y, axis=-1, keepdims=True)
            yc = y - mu
            rstd = lax.rsqrt(jnp.mean(yc * yc, axis=-1, keepdims=True) + EPS)
            yh = yc * rstd
            ln = yh * g_ref[...] + beta_ref[...]
            s = _sigmoid(ln)
            dout = d_ref[c0:c0 + CH, :] if c0 < T else jnp.where(last, 0.0, dnext_ref[c0 - T:c0 - T + CH, :])
            dln = dout * (s * (1.0 + ln * (1.0 - s)))
            dyh = dln * g_ref[...]
            dy = rstd * (dyh - jnp.mean(dyh, axis=-1, keepdims=True)
                         - yh * jnp.mean(dyh * yh, axis=-1, keepdims=True))
            dy_ref[c0:c0 + CH, :] = dy
            if c0 < T:
                dg_acc = dg_acc + jnp.sum(dln * yh, axis=0, keepdims=True)
                dbeta_acc = dbeta_acc + jnp.sum(dln, axis=0, keepdims=True)
                db_acc = db_acc + jnp.sum(dy, axis=0, keepdims=True)
        dg_ref[...] += dg_acc
        dbeta_ref[...] += dbeta_acc
        db_ref[...] += db_acc
        _fill_shifted(dy_ref, dys_ref)
        for k in range(K):
            dwt_ref[k:k + 1, :] += jnp.sum(
                dy_ref[0:T, :] * _rows_at(u_ref, us_ref, H - (K - 1) + k, T), axis=0, keepdims=True)
        for c0 in range(0, T, CH):
            du = jnp.zeros((CH, C), F32)
            for k in range(K):
                du = du + w_ref[k:k + 1, :] * _rows_at(dy_ref, dys_ref, c0 + (K - 1) - k, CH)
            x = x_ref[c0:c0 + CH, :]
            sg = _sigmoid(x[:, C:])
            dca_ref[c0:c0 + CH, :] = (du * sg).astype(dca_ref.dtype)
            dcg_ref[c0:c0 + CH, :] = (du * x[:, :C] * sg * (1.0 - sg)).astype(dcg_ref.dtype)

    nh = L // H
    vec = pl.BlockSpec((1, C), lambda i: (0, 0))
    row = pl.BlockSpec((T, C), lambda i: (i, 0))
    return pl.pallas_call(
        body, name=name, grid=(nt,),
        in_specs=[pl.BlockSpec((T, C2), lambda i: (i, 0)),
                  pl.BlockSpec((H, C2), lambda i: (jnp.maximum(i * (T // H) - 1, 0), 0)),
                  pl.BlockSpec((H, C2), lambda i: (jnp.minimum((i + 1) * (T // H), nh - 1), 0)),
                  pl.BlockSpec((T, C), lambda i: (i, 1)),
                  pl.BlockSpec((H, C), lambda i: (jnp.minimum((i + 1) * (T // H), nh - 1), 1)),
                  pl.BlockSpec((K, C), lambda i: (0, 0)), vec, vec, vec],
        out_specs=[row, row, pl.BlockSpec((H, C), lambda i: (0, 0)), vec, vec, vec],
        out_shape=[jax.ShapeDtypeStruct((L, C), BF16), jax.ShapeDtypeStruct((L, C), BF16),
                   jax.ShapeDtypeStruct((H, C), F32), jax.ShapeDtypeStruct((1, C), F32),
                   jax.ShapeDtypeStruct((1, C), F32), jax.ShapeDtypeStruct((1, C), F32)],
        scratch_shapes=[pltpu.VMEM((T + 2 * H, C), F32), pltpu.VMEM((7, TE + SHIFT_TAIL, C), F32),
                        pltpu.VMEM((TE, C), F32), pltpu.VMEM((7, T + SHIFT_TAIL, C), F32)],
        compiler_params=_params(("arbitrary",)),
    )(cacg, cacg, cacg, dmixed, dmixed, dw_w, dw_b, ln_g, ln_b)


def _local_step(h0, target, n_meta, seq, norms, conv_p, wts, final_g, gather_behind, reducer):
    mix_g, ffn_g = norms
    dw_w, dw_b, ln_g, ln_b = conv_p
    depth = mix_g.shape[0]
    C = dw_b.shape[-1]
    sbw = (wts["w_in"][0].shape[-1] - 2 * C) // 3
    assert sbw == C, "the mixer halves must have equal width"
    row = lambda a, i: a[i][None, :]

    h = h0
    saved = []
    for i in range(depth):
        hn = _rmsnorm_fwd(h, row(mix_g, i), f"mix_norm_{i}")
        proj_qkv = _mm_nn([(hn, wts["w_in"][i], 0, 0)], BF16, f"in_qkv_{i}", cols=(0, 3 * sbw))
        cacg = _mm_nn([(hn, wts["w_in"][i], 0, 0)], F32, f"in_conv_{i}", cols=(3 * sbw, 2 * C))
        job, keys = gather_behind.get(i, (None, ()))
        attn, arrived = _attn_fwd(proj_qkv, sbw, f"attn_fwd_{i}", job)
        for (wname, wl), arr in zip(keys, arrived):
            wts[wname][wl] = arr
        conv = _conv_fwd(cacg, dw_w[i], row(dw_b, i), row(ln_g, i), row(ln_b, i), f"conv_fwd_{i}")
        h_mid = _mm_nn([(attn, wts["w_out"][i], 0, 0), (conv, wts["w_out"][i], 0, 1)], F32, f"out_proj_{i}",
                       residual=h)
        hn2 = _rmsnorm_fwd(h_mid, row(ffn_g, i), f"ffn_norm_{i}")
        g, u, act = _ffn_up(hn2, wts["w_gate_t"][i], wts["w_up_t"][i], f"ffn_up_{i}")
        h_out = _mm_nn([(act, wts["w_down"][i], 0, 0)], F32, f"down_{i}", residual=h_mid)
        saved.append((h, hn, proj_qkv, cacg, attn, conv, h_mid, hn2, g, u, act))
        h = h_out

    loss, dh, d_final_g = _loss_head(h, final_g[None, :], target, n_meta, seq, "loss_head")

    grads = {k: [None] * depth for k in ("mix_g", "ffn_g", "dw_w", "dw_b", "ln_g", "ln_b")}
    for i in reversed(range(depth)):
        h_in, hn, proj_qkv, cacg, attn, conv, h_mid, hn2, g, u, act = saved[i]
        big = {}
        dg, du = _ffn_down_bwd(dh, wts["w_down"][i], g, u, f"ffn_down_bwd_{i}")
        big["w_down"] = _mm_tn(act, dh, f"dw_down_{i}", col_sharded=False)
        dhn2 = _mm_nn([(dg, wts["w_gate_t"][i], 0, 0), (du, wts["w_up_t"][i], 0, 0)], F32, f"d_hn2_{i}")
        big["w_gate_t"] = _mm_tn(dg, hn2, f"dw_gate_{i}", col_sharded=False)
        big["w_up_t"] = _mm_tn(du, hn2, f"dw_up_{i}", col_sharded=False)
        dh, d_ffn = _rmsnorm_bwd(h_mid, row(ffn_g, i), dhn2, dh, f"ffn_norm_bwd_{i}")
        dmixed = _mm_nt([(dh, wts["w_out"][i], 0, 0)], F32, f"d_mixed_{i}")
        mixed = jnp.concatenate([attn.astype(BF16), conv], axis=1)
        big["w_out"] = _mm_tn(mixed, dh, f"dw_out_{i}", col_sharded=False)
        reducer.ready(i, big)
        job, sink = reducer.take()
        (dq, dk, dv), arrived = _attn_bwd(proj_qkv, attn, dmixed, sbw, f"attn_bwd_{i}", job)
        sink(arrived)
        dca, dcg, d_dw, d_b, d_lg, d_lb = _conv_bwd(
            cacg, dmixed, dw_w[i], row(dw_b, i), row(ln_g, i), row(ln_b, i), f"conv_bwd_{i}")
        grads["dw_w"][i] = d_dw[:dw_w.shape[1]]
        grads["dw_b"][i], grads["ln_g"][i], grads["ln_b"][i] = d_b[0], d_lg[0], d_lb[0]
        dproj = jnp.concatenate([dq, dk, dv, dca, dcg], axis=1)
        dhn = _mm_nt([(dproj, wts["w_in"][i], 0, 0)], F32, f"d_hn_{i}")
        reducer.ready(i, {"w_in": _mm_tn(hn, dproj, f"dw_in_{i}", col_sharded=True)})
        dh, d_mix = _rmsnorm_bwd(h_in, row(mix_g, i), dhn, dh, f"mix_norm_bwd_{i}")
        grads["mix_g"][i], grads["ffn_g"][i] = d_mix[0], d_ffn[0]
    grads["final_g"] = d_final_g[0]
    return loss, dh, grads


ANY = pl.BlockSpec(memory_space=pl.ANY)


def _position():
    return lax.axis_index("x"), lax.axis_index("y"), lax.axis_index("c")


def _chip_at(x, y, k):
    return (1 - x if k & 2 else x), (1 - y if k & 1 else y)


def _half_rows(ref, half, rows, base=0):
    start = pl.multiple_of(base + half * rows, 8)
    lead = (slice(None),) * (len(ref.shape) - 2)
    return ref.at[(*lead, pl.ds(start, rows), slice(None))]


def _gather_job(fulls, shard_shapes, col_sharded):
    n = len(fulls)

    def tools(f_refs, send_sems, recv_sems):
        def block(wi, chip, half):
            _, R, C = shard_shapes[wi]
            if col_sharded[wi]:
                cols = pl.ds(pl.multiple_of(chip * C, LANES), C)
                return f_refs[wi].at[:, pl.ds(pl.multiple_of(half * (R // 2), 8), R // 2), cols]
            return _half_rows(f_refs[wi], half, R // 2, base=chip * R)

        def copy(wi, slot, blk, to):
            return pltpu.make_async_remote_copy(
                src_ref=blk, dst_ref=blk, send_sem=send_sems.at[6 * wi + slot],
                recv_sem=recv_sems.at[6 * wi + slot], device_id=to, device_id_type=MESH)

        return block, copy

    def start(_, f_refs, send_sems, recv_sems):
        block, copy = tools(f_refs, send_sems, recv_sems)
        x, y, c = _position()
        me = 2 * x + y
        for wi in range(n):
            for k in (1, 2, 3):
                copy(wi, k - 1, block(wi, me, c), (*_chip_at(x, y, k), c)).start()

    def finish(_, f_refs, send_sems, recv_sems):
        block, copy = tools(f_refs, send_sems, recv_sems)
        x, y, c = _position()
        me = 2 * x + y
        for wi in range(n):
            for k in (1, 2, 3):
                landed = block(wi, me ^ k, c)
                copy(wi, k - 1, landed, (x, y, c)).wait_recv()
                copy(wi, 2 + k, landed, (x, y, 1 - c)).start()
        for wi in range(n):
            for k in (1, 2, 3):
                copy(wi, 2 + k, block(wi, me ^ k, 1 - c), (x, y, c)).wait_recv()
        for wi in range(n):
            for k in (1, 2, 3):
                copy(wi, k - 1, block(wi, me, c), (x, y, c)).wait_send()
                copy(wi, 2 + k, block(wi, me ^ k, c), (x, y, c)).wait_send()

    return _CommJob(fulls, [jax.ShapeDtypeStruct(f.shape, f.dtype) for f in fulls], {i: i for i in range(n)},
                    6 * n, start, finish)


def _place_shard(w, layer, chip, col_sharded, dtype, name):
    _, R, C = w.shape
    tr = _pick(R, (256, 352, 128, 48))
    nr = R // tr

    def body(chip_ref, w_ref, o_ref):
        o_ref[...] = w_ref[...].astype(dtype)

    if col_sharded:
        shape = (1, R, N_CHIPS * C)
        out_spec = pl.BlockSpec((None, tr, C), lambda r, chip_ref: (0, r, chip_ref[0]))
    else:
        shape = (1, N_CHIPS * R, C)
        out_spec = pl.BlockSpec((None, tr, C), lambda r, chip_ref: (0, chip_ref[0] * nr + r, 0))
    grid_spec = pltpu.PrefetchScalarGridSpec(
        num_scalar_prefetch=1, grid=(nr,),
        in_specs=[pl.BlockSpec((None, tr, C), lambda r, chip_ref: (layer, r, 0))], out_specs=out_spec)
    return pl.pallas_call(
        body, name=name, grid_spec=grid_spec, out_shape=jax.ShapeDtypeStruct(shape, dtype),
        compiler_params=_params(("parallel",)),
    )(chip, w)


def _rs_to_sibling(grads, name):
    n = len(grads)
    outs = [jax.ShapeDtypeStruct((g.shape[0], g.shape[1] // 2, g.shape[2]), g.dtype) for g in grads]

    def body(*refs):
        g_refs, l_refs = refs[:n], refs[n:2 * n]
        send_sems, recv_sems = refs[2 * n:]
        x, y, c = _position()
        cps = []
        for wi in range(n):
            R = grads[wi].shape[1]
            cp = pltpu.make_async_remote_copy(
                src_ref=_half_rows(g_refs[wi], 1 - c, R // 2), dst_ref=l_refs[wi],
                send_sem=send_sems.at[wi], recv_sem=recv_sems.at[wi],
                device_id=(x, y, 1 - c), device_id_type=MESH)
            cp.start()
            cps.append(cp)
        for cp in cps:
            cp.wait()

    return pl.pallas_call(
        body, name=name, out_shape=outs, in_specs=[ANY] * n, out_specs=[ANY] * n,
        scratch_shapes=[pltpu.SemaphoreType.DMA((n,)), pltpu.SemaphoreType.DMA((n,))],
    )(*grads)


def _chip_sum(g, landed, core, name):
    _, R, C = g.shape
    hr = R // 2
    tr = _pick(hr, (256, 352, 128))
    nr = hr // tr

    def body(c_ref, g_ref, l_ref, o_ref):
        o_ref[...] = (g_ref[...] + l_ref[...]).astype(BF16)

    grid_spec = pltpu.PrefetchScalarGridSpec(
        num_scalar_prefetch=1, grid=(N_CHIPS, nr),
        in_specs=[pl.BlockSpec((None, tr, C), lambda j, r, c_ref: (j, c_ref[0] * nr + r, 0)),
                  pl.BlockSpec((None, tr, C), lambda j, r, c_ref: (j, r, 0))],
        out_specs=pl.BlockSpec((None, tr, C), lambda j, r, c_ref: (j, r, 0)))
    return pl.pallas_call(
        body, name=name, grid_spec=grid_spec, out_shape=jax.ShapeDtypeStruct((N_CHIPS, hr, C), BF16),
        compiler_params=_params(("parallel", "parallel")),
    )(core, g, landed)


def _across_job(parts):
    n = len(parts)

    def copy(p_refs, l_refs, send_sems, recv_sems, wi, k, to):
        x, y, _ = _position()
        me = 2 * x + y
        return pltpu.make_async_remote_copy(
            src_ref=p_refs[wi].at[me ^ k], dst_ref=l_refs[wi].at[me],
            send_sem=send_sems.at[3 * wi + k - 1], recv_sem=recv_sems.at[3 * wi + k - 1],
            device_id=to, device_id_type=MESH)

    def start(p_refs, l_refs, send_sems, recv_sems):
        x, y, c = _position()
        for wi in range(n):
            for k in (1, 2, 3):
                copy(p_refs, l_refs, send_sems, recv_sems, wi, k, (*_chip_at(x, y, k), c)).start()

    def finish(p_refs, l_refs, send_sems, recv_sems):
        x, y, c = _position()
        me = 2 * x + y
        for wi in range(n):
            for k in (1, 2, 3):
                slot = l_refs[wi].at[me ^ k]
                pltpu.make_async_remote_copy(
                    src_ref=slot, dst_ref=slot, send_sem=send_sems.at[3 * wi + k - 1],
                    recv_sem=recv_sems.at[3 * wi + k - 1], device_id=(x, y, c), device_id_type=MESH).wait_recv()
        for wi in range(n):
            for k in (1, 2, 3):
                copy(p_refs, l_refs, send_sems, recv_sems, wi, k, (x, y, c)).wait_send()

    return _CommJob(parts, [jax.ShapeDtypeStruct(p.shape, p.dtype) for p in parts], {}, 3 * n, start, finish)


class _Reducer:
    def __init__(self, core):
        self.core, self.parts, self.across, self.pending = core, {}, {}, []

    def ready(self, layer, big):
        names = list(big)
        flat = [big[k] for k in names]
        landed = _rs_to_sibling(flat, f"grads_to_sibling_{names[0]}_{layer}")
        for k, g, la in zip(names, flat, landed):
            self.parts[k, layer] = _chip_sum(g, la, self.core, f"chip_sum_{k}_{layer}")
            self.pending.append((k, layer))

    def take(self):
        keys, self.pending = self.pending, []

        def sink(results):
            self.across.update(zip(keys, results))

        return _across_job([self.parts[key] for key in keys]), sink


def _sum_chips(parts, landed, where, layer, depth, prev, name):
    _, hr, C = landed.shape
    tr = _pick(hr, (256, 352, 128))
    nr = hr // tr

    def body(*refs):
        own_ref, slots, o_ref = refs[1], refs[2:2 + N_CHIPS], refs[-1]
        chip = refs[0][0]
        total = None
        for q in range(N_CHIPS):
            term = jnp.where(chip == q, own_ref[...], slots[q][...]).astype(F32)
            total = term if total is None else total + term
        o_ref[...] = total

    def slot_spec(q):
        return pl.BlockSpec((None, tr, C), lambda r, w: (jnp.where(w[0] == q, (q + 1) % N_CHIPS, q), r, 0))

    in_specs = [pl.BlockSpec((None, tr, C), lambda r, w: (w[0], r, 0))] + [slot_spec(q) for q in range(N_CHIPS)]
    args = [where, parts] + [landed] * N_CHIPS
    aliases = {}
    if prev is not None:
        in_specs.append(ANY)
        args.append(prev)
        aliases = {len(args) - 1: 0}
    grid_spec = pltpu.PrefetchScalarGridSpec(
        num_scalar_prefetch=1, grid=(nr,), in_specs=in_specs,
        out_specs=pl.BlockSpec((None, tr, C), lambda r, w: (layer, w[1] * nr + r, 0)))
    return pl.pallas_call(
        body, name=name, grid_spec=grid_spec, out_shape=jax.ShapeDtypeStruct((depth, 2 * hr, C), F32),
        input_output_aliases=aliases, compiler_params=_params(("parallel",)),
    )(*args)


def _rs_join_halves(reduced):
    n = len(reduced)

    def body(*refs):
        o_refs = refs[n:2 * n]
        send_sems, recv_sems = refs[2 * n:]
        x, y, c = _position()
        sent = []
        for wi in range(n):
            hr = reduced[wi].shape[1] // 2
            mine = _half_rows(o_refs[wi], c, hr)
            cp = pltpu.make_async_remote_copy(
                src_ref=mine, dst_ref=mine, send_sem=send_sems.at[wi], recv_sem=recv_sems.at[wi],
                device_id=(x, y, 1 - c), device_id_type=MESH)
            cp.start()
            sent.append(cp)
        for wi in range(n):
            hr = reduced[wi].shape[1] // 2
            theirs = _half_rows(o_refs[wi], 1 - c, hr)
            pltpu.make_async_remote_copy(
                src_ref=theirs, dst_ref=theirs, send_sem=send_sems.at[wi], recv_sem=recv_sems.at[wi],
                device_id=(x, y, c), device_id_type=MESH).wait_recv()
        for cp in sent:
            cp.wait_send()

    return pl.pallas_call(
        body, name="grads_join_halves", out_shape=[jax.ShapeDtypeStruct(r.shape, r.dtype) for r in reduced],
        in_specs=[ANY] * n, out_specs=[ANY] * n, input_output_aliases={i: i for i in range(n)},
        scratch_shapes=[pltpu.SemaphoreType.DMA((n,)), pltpu.SemaphoreType.DMA((n,))],
    )(*reduced)


def _small_allreduce(vec):
    rows = vec.shape[0]

    def body(v_ref, o_ref, land, send_sems, recv_sems):
        x, y, c = _position()
        me = 4 * x + 2 * y + c
        land[0] = v_ref[...]
        sent = []
        for k in range(1, N_DEV):
            to = (1 - x if k & 4 else x, 1 - y if k & 2 else y, 1 - c if k & 1 else c)
            cp = pltpu.make_async_remote_copy(
                src_ref=v_ref, dst_ref=land.at[k], send_sem=send_sems.at[k - 1], recv_sem=recv_sems.at[k - 1],
                device_id=to, device_id_type=MESH)
            cp.start()
            sent.append(cp)
        for cp in sent:
            cp.wait_recv()
        acc = land[me]
        for e in range(1, N_DEV):
            acc = acc + land[me ^ e]
        o_ref[...] = acc
        for cp in sent:
            cp.wait_send()

    vmem = pl.BlockSpec(memory_space=pltpu.VMEM)
    return pl.pallas_call(
        body, name="small_allreduce", out_shape=jax.ShapeDtypeStruct(vec.shape, F32),
        in_specs=[vmem], out_specs=vmem,
        scratch_shapes=[pltpu.VMEM((N_DEV, rows, LANES), F32), pltpu.SemaphoreType.DMA((N_DEV - 1,)),
                        pltpu.SemaphoreType.DMA((N_DEV - 1,))],
    )(vec)


def _adam_math(w, g, m, v):
    m = ADAM_B1 * m + (1.0 - ADAM_B1) * g
    v = ADAM_B2 * v + (1.0 - ADAM_B2) * jnp.square(g)
    m_hat = m / (1.0 - ADAM_B1 ** ADAM_STEP)
    v_hat = v / (1.0 - ADAM_B2 ** ADAM_STEP)
    delta = -ADAM_LR * (m_hat / (jnp.sqrt(v_hat) + ADAM_EPS) + ADAM_WD * w)
    return delta, m, v


def _adam(w, g, m, v, name):
    def body(w_ref, g_ref, m_ref, v_ref, d_ref, nm_ref, nv_ref):
        d_ref[...], nm_ref[...], nv_ref[...] = _adam_math(w_ref[...], g_ref[...], m_ref[...], v_ref[...])

    if w.ndim == 3:
        lyr, R, C = w.shape
        tr = _pick(R, (256, 352, 128))
        blk = pl.BlockSpec((None, tr, C), lambda l, r: (l, r, 0))
        grid, sem = (lyr, R // tr), ("parallel", "parallel")
    else:
        blk = pl.BlockSpec(w.shape, lambda: (0, 0))
        grid, sem = (), None
    return pl.pallas_call(
        body, name=name, grid=grid, in_specs=[blk] * 4, out_specs=[blk] * 3,
        out_shape=[jax.ShapeDtypeStruct(w.shape, F32)] * 3, compiler_params=_params(sem),
    )(w, g, m, v)


def _rows(a, pad_to=8):
    r = a.reshape(-1, LANES)
    extra = (-r.shape[0]) % pad_to
    return jnp.pad(r, ((0, extra), (0, 0))) if extra else r


def _pack(arrays):
    return jnp.concatenate([_rows(a) for a in arrays], axis=0)


def _unpack(slab, shapes):
    out, at = [], 0
    for shp in shapes:
        nrow = math.prod(shp) // LANES
        out.append(slab[at:at + nrow].reshape(shp))
        at += nrow + (-nrow) % 8
    return out


BIG = ("w_in", "w_out", "w_gate_t", "w_up_t", "w_down")
BIG_COL_SHARDED = (True, False, False, False, False)
TRANSPOSED = {"w_gate_t": "w_gate", "w_up_t": "w_up"}


def kernel(x, meta_tokens, mix_norm_g, w_in, conv_dw_w, conv_dw_b, conv_ln_g, conv_ln_b, w_out, ffn_norm_g, w_gate, w_up, w_down, final_norm_g, loss_target, m_meta_tokens, m_mix_norm_g, m_w_in, m_conv_dw_w, m_conv_dw_b, m_conv_ln_g, m_conv_ln_b, m_w_out, m_ffn_norm_g, m_w_gate, m_w_up, m_w_down, m_final_norm_g, v_meta_tokens, v_mix_norm_g, v_w_in, v_conv_dw_w, v_conv_dw_b, v_conv_ln_g, v_conv_ln_b, v_w_out, v_ffn_norm_g, v_w_gate, v_w_up, v_w_down, v_final_norm_g):
    n_meta, seq = meta_tokens.shape[0], x.shape[1]
    D = x.shape[2]
    depth, taps, c_shard = conv_dw_w.shape
    C = conv_dw_b.shape[-1]
    chip = (2 * lax.axis_index("x") + lax.axis_index("y")).astype(jnp.int32)
    core = lax.axis_index("c").astype(jnp.int32).reshape(1)
    chip1 = chip.reshape(1)
    where = jnp.concatenate([chip1, core])
    big_w = dict(w_in=w_in, w_out=w_out, w_gate=w_gate, w_up=w_up, w_down=w_down)
    big_m = dict(w_in=m_w_in, w_out=m_w_out, w_gate=m_w_gate, w_up=m_w_up, w_down=m_w_down)
    big_v = dict(w_in=v_w_in, w_out=v_w_out, w_gate=v_w_gate, w_up=v_w_up, w_down=v_w_down)

    small_shard = _pack([conv_dw_w, meta_tokens])[None]
    to_send = {k: jnp.swapaxes(big_w[TRANSPOSED[k]], 1, 2) if k in TRANSPOSED else big_w[k] for k in BIG}
    col = dict(zip(BIG, BIG_COL_SHARDED))
    wts = {k: [_place_shard(to_send[k], l, chip1, col[k], BF16, f"place_{k}_{l}") for l in range(depth)] for k in BIG}
    small_placed = _place_shard(small_shard, 0, chip1, False, F32, "place_small")

    def gather_job(keys, extra=()):
        arrays = [wts[k][l] for k, l in keys] + list(extra)
        shapes = [(1,) + to_send[k].shape[1:] for k, _ in keys] + [(1,) + small_shard.shape[1:]] * len(extra)
        return _gather_job(arrays, shapes, [col[k] for k, _ in keys] + [False] * len(extra))

    first_keys = [("w_in", 0), ("w_out", 0)]
    *first, small_full = _run_job(gather_job(first_keys, [small_placed]), "gather_first")
    for (k, l), arr in zip(first_keys, first):
        wts[k][l] = arr
    ffn = lambda l: [("w_gate_t", l), ("w_up_t", l), ("w_down", l)]
    behind_keys = {0: ffn(0) + [("w_in", l) for l in range(1, depth)] + [("w_out", l) for l in range(1, depth)]}
    for l in range(1, depth):
        behind_keys[l] = ffn(l)
    gather_behind = {l: (gather_job(keys), keys) for l, keys in behind_keys.items()}

    rows_shard = small_shard.shape[1]
    dw_full, meta_full = [], []
    for j in range(N_CHIPS):
        dwj, mj = _unpack(small_full[0, j * rows_shard:(j + 1) * rows_shard],
                          [conv_dw_w.shape, meta_tokens.shape])
        dw_full.append(dwj)
        meta_full.append(mj)
    dw_w_full = jnp.concatenate(dw_full, axis=2)
    meta = jnp.concatenate(meta_full, axis=1)

    L = n_meta + seq
    Lp = -(-L // QUERY_BLOCK) * QUERY_BLOCK
    h0 = jnp.concatenate([meta, x[0], jnp.zeros((Lp - L, D), F32)], axis=0)
    target = jnp.pad(loss_target[0], ((n_meta, Lp - L), (0, 0)))
    reducer = _Reducer(core)
    loss, dh0, grads = _local_step(h0, target, n_meta, seq, (mix_norm_g, ffn_norm_g),
                                   (dw_w_full, conv_dw_b, conv_ln_g, conv_ln_b), wts, final_norm_g,
                                   gather_behind, reducer)
    loss = lax.psum(loss[0, 0], ("x", "y", "c"))
    grad_x = dh0[n_meta:L][None]
    job, sink = reducer.take()
    sink(_run_job(job, "grads_across_last"))

    reduced = []
    for k in BIG:
        arr = None
        for l in range(depth):
            arr = _sum_chips(reducer.parts[k, l], reducer.across[k, l], where, l, depth, arr, f"sum_chips_{k}_{l}")
        reduced.append(arr)
    big_g = dict(zip(BIG, _rs_join_halves(reduced)))

    small_names = ("mix_g", "ffn_g", "dw_b", "ln_g", "ln_b", "dw_w")
    small = [jnp.stack(grads[k]) for k in small_names] + [grads["final_g"], dh0[:n_meta]]
    small_shapes = [a.shape for a in small]
    g_mix, g_ffn, g_dwb, g_lng, g_lnb, g_dww, g_final, g_meta = _unpack(_small_allreduce(_pack(small)), small_shapes)
    g_dww = lax.dynamic_slice_in_dim(g_dww, chip * c_shard, c_shard, axis=2)
    g_meta = lax.dynamic_slice_in_dim(g_meta, chip * meta_tokens.shape[1], meta_tokens.shape[1], axis=1)

    out_g, out_d, out_m, out_v = {}, {}, {}, {}
    for kk in BIG:
        k = TRANSPOSED.get(kk, kk)
        view = (lambda a: jnp.swapaxes(a, 1, 2)) if kk in TRANSPOSED else (lambda a: a)
        res = _adam(view(big_w[k]), big_g[kk], view(big_m[k]), view(big_v[k]), f"adam_{k}")
        out_g[k] = view(big_g[kk])
        out_d[k], out_m[k], out_v[k] = (view(a) for a in res)
    small_order = ("meta_tokens", "mix_norm_g", "conv_dw_w", "conv_dw_b", "conv_ln_g", "conv_ln_b",
                   "ffn_norm_g", "final_norm_g")
    sw = dict(meta_tokens=meta_tokens, mix_norm_g=mix_norm_g, conv_dw_w=conv_dw_w, conv_dw_b=conv_dw_b,
              conv_ln_g=conv_ln_g, conv_ln_b=conv_ln_b, ffn_norm_g=ffn_norm_g, final_norm_g=final_norm_g)
    sm = dict(meta_tokens=m_meta_tokens, mix_norm_g=m_mix_norm_g, conv_dw_w=m_conv_dw_w, conv_dw_b=m_conv_dw_b,
              conv_ln_g=m_conv_ln_g, conv_ln_b=m_conv_ln_b, ffn_norm_g=m_ffn_norm_g, final_norm_g=m_final_norm_g)
    sv = dict(meta_tokens=v_meta_tokens, mix_norm_g=v_mix_norm_g, conv_dw_w=v_conv_dw_w, conv_dw_b=v_conv_dw_b,
              conv_ln_g=v_conv_ln_g, conv_ln_b=v_conv_ln_b, ffn_norm_g=v_ffn_norm_g, final_norm_g=v_final_norm_g)
    sg = dict(meta_tokens=g_meta, mix_norm_g=g_mix, conv_dw_w=g_dww, conv_dw_b=g_dwb, conv_ln_g=g_lng,
              conv_ln_b=g_lnb, ffn_norm_g=g_ffn, final_norm_g=g_final)
    def slab(d):
        return _pack([d[k] for k in small_order])
    shapes = [sw[k].shape for k in small_order]
    deltas = _adam(slab(sw), slab(sg), slab(sm), slab(sv), "adam_small")
    for res, dst in zip(deltas, (out_d, out_m, out_v)):
        dst.update(zip(small_order, _unpack(res, shapes)))
    out_g.update(sg)

    order = ("meta_tokens", "mix_norm_g", "w_in", "conv_dw_w", "conv_dw_b", "conv_ln_g", "conv_ln_b", "w_out",
             "ffn_norm_g", "w_gate", "w_up", "w_down", "final_norm_g")
    return (loss, grad_x, *[out_g[k] for k in order], *[out_d[k] for k in order],
            *[out_m[k] for k in order], *[out_v[k] for k in order])
```

```python
import functools
import math

import jax
import jax.numpy as jnp
from jax import lax
from jax.experimental import pallas as pl
from jax.experimental.pallas import tpu as pltpu

F32 = jnp.float32
BF16 = jnp.bfloat16
MESH = pl.DeviceIdType.MESH

EPS = 1e-6
QUERY_BLOCK = 128
LANES = 128
HEAD_DIM = 64
LOG_STICK_FLOOR = -40.0
CONV_HALO = 32
N_CHIPS = 4
N_DEV = 8
VMEM_LIMIT = 56 * 1024 * 1024

ADAM_LR = 0.001
ADAM_B1 = 0.9
ADAM_B2 = 0.999
ADAM_EPS = 1e-08
ADAM_WD = 0.01
ADAM_STEP = 10


def _pick(n, prefs):
    for p in prefs:
        if n % p == 0:
            return p
    return n


def _params(sem=None):
    return pltpu.CompilerParams(dimension_semantics=sem, vmem_limit_bytes=VMEM_LIMIT)


def _sigmoid(x):
    return 1.0 / (1.0 + jnp.exp(-x))


def _rmsnorm_bwd(h, g, dy, dh_in, name):
    L, D = h.shape
    T = _pick(L, (384, 128))

    def body(h_ref, g_ref, dy_ref, dhin_ref, dh_ref, dg_ref):
        x = h_ref[...]
        dyv = dy_ref[...]
        r = lax.rsqrt(jnp.mean(x * x, axis=-1, keepdims=True) + EPS)
        xh = x * r
        dxh = dyv * g_ref[...]
        dh_ref[...] = dhin_ref[...] + r * (dxh - xh * jnp.mean(dxh * xh, axis=-1, keepdims=True))

        @pl.when(pl.program_id(0) == 0)
        def _():
            dg_ref[...] = jnp.zeros_like(dg_ref)

        dg_ref[...] += jnp.sum(dyv * xh, axis=0, keepdims=True)

    row = pl.BlockSpec((T, D), lambda i: (i, 0))
    vec = pl.BlockSpec((1, D), lambda i: (0, 0))
    return pl.pallas_call(
        body, name=name, grid=(L // T,),
        in_specs=[row, vec, row, row], out_specs=[row, vec],
        out_shape=[jax.ShapeDtypeStruct((L, D), F32), jax.ShapeDtypeStruct((1, D), F32)],
        compiler_params=_params(("arbitrary",)),
    )(h, g, dy, dh_in)


def _loss_head(h, g, target, n_meta, seq, name):
    L, D = h.shape
    T = _pick(L, (384, 128))

    def body(h_ref, g_ref, t_ref, loss_ref, dh_ref, dg_ref):
        i = pl.program_id(0)
        x = h_ref[...]
        gv = g_ref[...]
        r = lax.rsqrt(jnp.mean(x * x, axis=-1, keepdims=True) + EPS)
        xh = x * r
        y = xh * gv
        rows = i * T + lax.broadcasted_iota(jnp.int32, (T, 1), 0)
        live = (rows >= n_meta) & (rows < n_meta + seq)
        diff = jnp.where(live, y - t_ref[...], 0.0)
        dyv = diff / D
        dxh = dyv * gv
        dh_ref[...] = r * (dxh - xh * jnp.mean(dxh * xh, axis=-1, keepdims=True))

        @pl.when(i == 0)
        def _():
            dg_ref[...] = jnp.zeros_like(dg_ref)
            loss_ref[...] = jnp.zeros_like(loss_ref)

        dg_ref[...] += jnp.sum(dyv * xh, axis=0, keepdims=True)
        per_row = jnp.mean(diff * diff, axis=-1, keepdims=True)
        loss_ref[...] += 0.5 * jnp.sum(per_row, axis=0, keepdims=True)

    row = pl.BlockSpec((T, D), lambda i: (i, 0))
    vec = pl.BlockSpec((1, D), lambda i: (0, 0))
    one = pl.BlockSpec((1, 1), lambda i: (0, 0))
    return pl.pallas_call(
        body, name=name, grid=(L // T,),
        in_specs=[row, vec, row], out_specs=[one, row, vec],
        out_shape=[jax.ShapeDtypeStruct((1, 1), F32), jax.ShapeDtypeStruct((L, D), F32),
                   jax.ShapeDtypeStruct((1, D), F32)],
        compiler_params=_params(("arbitrary",)),
    )(h, g, target)


def _ffn_tiles(M, F):
    return _pick(M, (704, 384, 128)), _pick(F, (1408, 512, 256, 128))


def _normed_rows(h_ref, g_ref, hn_ref, keep_ref):
    @pl.when(pl.program_id(1) == 0)
    def _():
        x = h_ref[...]
        r = lax.rsqrt(jnp.mean(x * x, axis=-1, keepdims=True) + EPS)
        keep_ref[...] = (x * r * g_ref[...]).astype(BF16)
        hn_ref[...] = keep_ref[...]


def _norm_in_proj(h, g, w_in, n_cols, name):
    M, D = h.shape
    tm = _pick(M, (1056, 384, 128))
    tn = _pick(n_cols, (512, 256, 128))

    def body(h_ref, g_ref, w_ref, hn_ref, o_ref, keep_ref):
        _normed_rows(h_ref, g_ref, hn_ref, keep_ref)
        o_ref[...] = jnp.dot(keep_ref[...], w_ref[...], preferred_element_type=F32).astype(o_ref.dtype)

    rows = pl.BlockSpec((tm, D), lambda i, j: (i, 0))
    return pl.pallas_call(
        body, name=name, grid=(M // tm, n_cols // tn),
        in_specs=[rows, pl.BlockSpec((1, D), lambda i, j: (0, 0)), pl.BlockSpec((None, D, tn), lambda i, j: (0, 0, j))],
        out_specs=[rows, pl.BlockSpec((tm, tn), lambda i, j: (i, j))],
        out_shape=[jax.ShapeDtypeStruct((M, D), BF16), jax.ShapeDtypeStruct((M, n_cols), BF16)],
        scratch_shapes=[pltpu.VMEM((tm, D), BF16)],
        compiler_params=_params(("parallel", "arbitrary")),
    )(h, g, w_in)


def _ffn_up(h, norm_g, w_gate_t, w_up_t, name):
    M, D = h.shape
    F = w_gate_t.shape[1]
    tm, tf = _ffn_tiles(M, F)
    nt = (((1,), (1,)), ((), ()))

    def body(h_ref, ng_ref, wg_ref, wu_ref, hn_ref, g_ref, u_ref, a_ref, keep_ref):
        _normed_rows(h_ref, ng_ref, hn_ref, keep_ref)
        hv = keep_ref[...]
        gv = lax.dot_general(hv, wg_ref[...], nt, preferred_element_type=F32)
        uv = lax.dot_general(hv, wu_ref[...], nt, preferred_element_type=F32)
        g_ref[...] = gv
        u_ref[...] = uv
        a_ref[...] = (gv * _sigmoid(gv) * uv).astype(a_ref.dtype)

    rows = pl.BlockSpec((tm, D), lambda i, j: (i, 0))
    w_spec = pl.BlockSpec((None, tf, D), lambda i, j: (0, j, 0))
    out = pl.BlockSpec((tm, tf), lambda i, j: (i, j))
    return pl.pallas_call(
        body, name=name, grid=(M // tm, F // tf),
        in_specs=[rows, pl.BlockSpec((1, D), lambda i, j: (0, 0)), w_spec, w_spec], out_specs=[rows, out, out, out],
        out_shape=[jax.ShapeDtypeStruct((M, D), BF16), jax.ShapeDtypeStruct((M, F), F32),
                   jax.ShapeDtypeStruct((M, F), F32), jax.ShapeDtypeStruct((M, F), BF16)],
        scratch_shapes=[pltpu.VMEM((tm, D), BF16)],
        compiler_params=_params(("parallel", "arbitrary")),
    )(h, norm_g, w_gate_t, w_up_t)


def _ffn_down_bwd(dh, w_down, g, u, name):
    M, D = dh.shape
    F = g.shape[1]
    tm, tf = _ffn_tiles(M, F)

    def body(d_ref, w_ref, g_ref, u_ref, dg_ref, du_ref):
        dv = lax.dot_general(d_ref[...].astype(BF16), w_ref[...], (((1,), (1,)), ((), ())),
                             preferred_element_type=F32)
        gv = g_ref[...]
        s = _sigmoid(gv)
        du_ref[...] = (dv * (gv * s)).astype(du_ref.dtype)
        dg_ref[...] = (dv * u_ref[...] * (s * (1.0 + gv * (1.0 - s)))).astype(dg_ref.dtype)

    tile = pl.BlockSpec((tm, tf), lambda i, j: (i, j))
    return pl.pallas_call(
        body, name=name, grid=(M // tm, F // tf),
        in_specs=[pl.BlockSpec((tm, D), lambda i, j: (i, 0)), pl.BlockSpec((None, tf, D), lambda i, j: (0, j, 0)),
                  tile, tile],
        out_specs=[tile, tile],
        out_shape=[jax.ShapeDtypeStruct((M, F), BF16), jax.ShapeDtypeStruct((M, F), BF16)],
        compiler_params=_params(("parallel", "parallel")),
    )(dh, w_down, g, u)


def _mm_nn(pairs, out_dtype, name, residual=None, cols=None):
    M = pairs[0][0].shape[0]
    col0, N = cols if cols is not None else (0, pairs[0][1].shape[-1])
    tm = _pick(M, (1056, 384, 128))
    shallow = sum(p[0].shape[1] for p in pairs) <= 1024
    tn = _pick(math.gcd(N, col0) if col0 else N, ((1024,) if shallow else ()) + (640, 512, 256, 128))
    jb = col0 // tn
    n = len(pairs)

    def body(*refs):
        a_refs, w_refs = refs[:n], refs[n:2 * n]
        o_ref = refs[-1]
        acc = None
        for a_ref, w_ref in zip(a_refs, w_refs):
            d = jnp.dot(a_ref[...].astype(BF16), w_ref[...], preferred_element_type=F32)
            acc = d if acc is None else acc + d
        if residual is not None:
            acc = acc + refs[2 * n][...]
        o_ref[...] = acc.astype(o_ref.dtype)

    in_specs = [pl.BlockSpec((tm, a.shape[1]), lambda i, j: (i, 0)) for a, _, _, _ in pairs]
    for a, _, layer, kblk in pairs:
        in_specs.append(pl.BlockSpec((None, a.shape[1], tn), functools.partial(lambda i, j, l, kb: (l, kb, j + jb), l=layer, kb=kblk)))
    args = [p[0] for p in pairs] + [p[1] for p in pairs]
    if residual is not None:
        in_specs.append(pl.BlockSpec((tm, tn), lambda i, j: (i, j)))
        args.append(residual)
    return pl.pallas_call(
        body, name=name, grid=(M // tm, N // tn), in_specs=in_specs,
        out_specs=pl.BlockSpec((tm, tn), lambda i, j: (i, j)),
        out_shape=jax.ShapeDtypeStruct((M, N), out_dtype),
        compiler_params=_params(("parallel", "parallel")),
    )(*args)


def _mm_nt(pairs, out_dtype, name):
    M = pairs[0][0].shape[0]
    K = pairs[0][1].shape[1]
    tm = _pick(M, (1056, 384, 128))
    shallow = sum(p[0].shape[1] for p in pairs) <= 1024
    tk = _pick(K, ((1024,) if shallow else ()) + (512, 1408, 256, 128))
    n = len(pairs)

    def body(*refs):
        d_refs, w_refs = refs[:n], refs[n:2 * n]
        o_ref = refs[-1]
        acc = None
        for d_ref, w_ref in zip(d_refs, w_refs):
            d = lax.dot_general(d_ref[...].astype(BF16), w_ref[...], (((1,), (1,)), ((), ())),
                                preferred_element_type=F32)
            acc = d if acc is None else acc + d
        o_ref[...] = acc.astype(o_ref.dtype)

    in_specs = [pl.BlockSpec((tm, d.shape[1]), lambda i, j: (i, 0)) for d, _, _, _ in pairs]
    for d, _, layer, cblk in pairs:
        in_specs.append(pl.BlockSpec((None, tk, d.shape[1]), functools.partial(lambda i, j, l, cb: (l, j, cb), l=layer, cb=cblk)))
    args = [p[0] for p in pairs] + [p[1] for p in pairs]
    return pl.pallas_call(
        body, name=name, grid=(M // tm, K // tk), in_specs=in_specs,
        out_specs=pl.BlockSpec((tm, tk), lambda i, j: (i, j)),
        out_shape=jax.ShapeDtypeStruct((M, K), out_dtype),
        compiler_params=_params(("parallel", "parallel")),
    )(*args)


def _mm_tn(a, b, name, col_sharded, chips=N_CHIPS):
    M, K = a.shape
    N = b.shape[1]
    tm = _pick(M, (1056, 384, 128))
    tk = _pick(K, (1024, 1408, 512, 256, 128))
    tn = N // N_CHIPS if col_sharded else _pick(N, (512, 128))

    def body(a_ref, b_ref, o_ref):
        @pl.when(pl.program_id(2) == 0)
        def _():
            o_ref[...] = jnp.zeros_like(o_ref)

        o_ref[...] += lax.dot_general(a_ref[...].astype(BF16), b_ref[...].astype(BF16),
                                      (((0,), (0,)), ((), ())), preferred_element_type=F32)

    if col_sharded:
        out_shape = jax.ShapeDtypeStruct((N_CHIPS, K, tn), F32)
        out_spec = pl.BlockSpec((None, tk, tn), lambda k, j, m: (j, k, 0))
    else:
        out_shape = jax.ShapeDtypeStruct((K, N), F32)
        out_spec = pl.BlockSpec((tk, tn), lambda k, j, m: (k, j))
    out = pl.pallas_call(
        body, name=name, grid=(K // tk, N // tn, M // tm),
        in_specs=[pl.BlockSpec((tm, tk), lambda k, j, m: (m, k)), pl.BlockSpec((tm, tn), lambda k, j, m: (m, j))],
        out_specs=out_spec, out_shape=out_shape,
        compiler_params=_params(("parallel", "parallel", "arbitrary")),
    )(a, b)
    return out if col_sharded else out.reshape(chips, K // chips, N)


def _stack_heads(x, scale=None):
    lane = lax.broadcasted_iota(jnp.int32, x.shape, 1)
    zero = jnp.zeros_like(x)
    lo = jnp.where(lane < HEAD_DIM, x, zero)
    hi = jnp.where(lane < HEAD_DIM, zero, x)
    out = jnp.concatenate([lo, hi], axis=0)
    return out if scale is None else out * scale


def _unstack_heads(x2):
    qb = x2.shape[0] // 2
    lane = lax.broadcasted_iota(jnp.int32, (qb, LANES), 1)
    return jnp.where(lane < HEAD_DIM, x2[:qb], x2[qb:])


def _tri_and_ones(strict, keys=QUERY_BLOCK):
    r = lax.broadcasted_iota(jnp.int32, (keys, keys + LANES), 0)
    c = lax.broadcasted_iota(jnp.int32, (keys, keys + LANES), 1)
    tri = (r > c) if strict else (r >= c)
    return jnp.where(tri | (c >= keys), 1.0, 0.0).astype(BF16)


def _dot2(x, m):
    xh = x.astype(BF16)
    xl = (x - xh.astype(F32)).astype(BF16)
    return jnp.dot(xh, m, preferred_element_type=F32) + jnp.dot(xl, m, preferred_element_type=F32)


def _causal_valid(older=0):
    r = lax.broadcasted_iota(jnp.int32, (2 * QUERY_BLOCK, older + QUERY_BLOCK), 0) & (QUERY_BLOCK - 1)
    c = lax.broadcasted_iota(jnp.int32, (2 * QUERY_BLOCK, older + QUERY_BLOCK), 1)
    return c < r + older


def _pairs_per_step(n_pairs):
    return 2 if n_pairs % 2 == 0 else 1


def _lanes(g):
    return slice(g * LANES, (g + 1) * LANES)


def _sweep_older(i, step, carry_ref, first):
    def cond(state):
        n, live = state
        return jnp.logical_and(n < i, live)

    def older(state):
        n, _ = state
        step(i - 1 - n, False)
        return n + 1, jnp.max(carry_ref[...]) > LOG_STICK_FLOOR

    lax.while_loop(cond, older, (first, jnp.max(carry_ref[...]) > LOG_STICK_FLOOR))


class _CommJob:
    def __init__(self, inputs, out_shapes, aliases, n_sems, start, finish):
        self.inputs, self.out_shapes, self.aliases, self.n_sems = list(inputs), list(out_shapes), aliases, n_sems
        self.start, self.finish = start, finish


def _call_with_job(core_body, job, *, name, grid, in_specs, out_specs, out_shape, scratch_shapes, args):
    sem = ("arbitrary",) * len(grid)
    if job is None:
        res = pl.pallas_call(core_body, name=name, grid=grid, in_specs=in_specs, out_specs=out_specs,
                             out_shape=out_shape, scratch_shapes=scratch_shapes, compiler_params=_params(sem))(*args)
        return list(res), []
    n_in, n_out, n_scr = len(in_specs), len(out_specs), len(scratch_shapes)
    m_in, m_out = len(job.inputs), len(job.out_shapes)

    def body(*refs):
        at = 0
        parts = []
        for count in (n_in, m_in, n_out, m_out, n_scr, 2):
            parts.append(refs[at:at + count])
            at += count
        ins, job_in, outs, job_outs, scratch, (send_sems, recv_sems) = parts
        first = functools.reduce(jnp.logical_and, [pl.program_id(a) == 0 for a in range(len(grid))])
        last = functools.reduce(jnp.logical_and, [pl.program_id(a) == grid[a] - 1 for a in range(len(grid))])

        @pl.when(first)
        def _():
            job.start(job_in, job_outs, send_sems, recv_sems)

        core_body(*ins, *outs, *scratch)

        @pl.when(last)
        def _():
            job.finish(job_in, job_outs, send_sems, recv_sems)

    res = pl.pallas_call(
        body, name=name, grid=grid, in_specs=list(in_specs) + [ANY] * m_in, out_specs=list(out_specs) + [ANY] * m_out,
        out_shape=list(out_shape) + job.out_shapes,
        input_output_aliases={n_in + a: n_out + b for a, b in job.aliases.items()},
        scratch_shapes=list(scratch_shapes) + [pltpu.SemaphoreType.DMA((job.n_sems,)), pltpu.SemaphoreType.DMA((job.n_sems,))],
        compiler_params=_params(sem),
    )(*args, *job.inputs)
    return list(res[:n_out]), list(res[n_out:])


def _run_job(job, name):
    m_in, m_out = len(job.inputs), len(job.out_shapes)

    def body(*refs):
        job_in, job_outs = refs[:m_in], refs[m_in:m_in + m_out]
        send_sems, recv_sems = refs[m_in + m_out:]
        job.start(job_in, job_outs, send_sems, recv_sems)
        job.finish(job_in, job_outs, send_sems, recv_sems)

    return list(pl.pallas_call(
        body, name=name, in_specs=[ANY] * m_in, out_specs=[ANY] * m_out, out_shape=job.out_shapes,
        input_output_aliases=dict(job.aliases),
        scratch_shapes=[pltpu.SemaphoreType.DMA((job.n_sems,)), pltpu.SemaphoreType.DMA((job.n_sems,))],
    )(*job.inputs))


def _attn_fwd(qkv, sb_width, name, job=None):
    L = qkv.shape[0]
    QB = QUERY_BLOCK
    nb = L // QB
    n_pairs = sb_width // LANES
    G = _pairs_per_step(n_pairs)
    W = G * LANES
    scale = 1.0 / math.sqrt(HEAD_DIM)

    def body(q_ref, k_ref, v_ref, o_ref, acc_ref, carry_ref):
        i = pl.program_id(1)
        q2 = [_stack_heads(q_ref[:, _lanes(g)], scale) for g in range(G)]
        acc_ref[...] = jnp.zeros_like(acc_ref)
        carry_ref[...] = jnp.zeros_like(carry_ref)

        def step(g, j, older, mask):
            n = older + QB
            start = pl.multiple_of(j * QB, QB)
            kb = k_ref[pl.ds(start, n), _lanes(g)]
            vb = v_ref[pl.ds(start, n), _lanes(g)]
            z = lax.dot_general(q2[g], kb, (((1,), (1,)), ((), ())), preferred_element_type=F32)
            sp = jnp.log(1.0 + jnp.exp(-jnp.abs(z)))
            a = jnp.minimum(z, 0.0) - sp
            b = jnp.minimum(-z, 0.0) - sp
            if mask is not None:
                b = jnp.where(mask, b, 0.0)
            res = _dot2(b, _tri_and_ones(True, n))
            excl = res[:, :n]
            if mask is None:
                excl = excl + carry_ref[g]
            w = jnp.exp(a + excl)
            if mask is not None:
                w = jnp.where(mask, w, 0.0)
            carry_ref[g] += res[:, n:]
            acc_ref[g] += _dot2(w, vb)

        @pl.when(i == 0)
        def _():
            for g in range(G):
                step(g, 0, 0, _causal_valid())

        @pl.when(i > 0)
        def _():
            for g in range(G):
                step(g, i - 1, QB, _causal_valid(QB))
            for g in range(G):
                _sweep_older(i, lambda j, _, g=g: step(g, j, 0, None), carry_ref.at[g], 1)

        for g in range(G):
            o_ref[:, _lanes(g)] = _unstack_heads(acc_ref[g])

    n_steps = n_pairs // G
    (out,), job_out = _call_with_job(
        body, job, name=name, grid=(n_steps, nb),
        in_specs=[pl.BlockSpec((QB, W), lambda p, i: (i, p)),
                  pl.BlockSpec((L, W), lambda p, i: (0, n_steps + p)),
                  pl.BlockSpec((L, W), lambda p, i: (0, 2 * n_steps + p))],
        out_specs=[pl.BlockSpec((QB, W), lambda p, i: (i, p))],
        out_shape=[jax.ShapeDtypeStruct((L, sb_width), F32)],
        scratch_shapes=[pltpu.VMEM((G, 2 * QB, LANES), F32), pltpu.VMEM((G, 2 * QB, LANES), F32)],
        args=(qkv, qkv, qkv))
    return out, job_out


def _attn_bwd(qkv, o, dmixed, sb_width, name, job=None):
    L = qkv.shape[0]
    QB = QUERY_BLOCK
    nb = L // QB
    n_pairs = sb_width // LANES
    G = _pairs_per_step(n_pairs)
    W = G * LANES
    scale = 1.0 / math.sqrt(HEAD_DIM)

    def body(q_ref, k_ref, v_ref, o_ref, do_ref, dq_ref, dk_ref, dv_ref,
             dq_acc, dk_acc, dv_acc, ce_ref, cr_ref):
        i = pl.program_id(1)

        @pl.when(i == 0)
        def _():
            dk_acc[...] = jnp.zeros_like(dk_acc)
            dv_acc[...] = jnp.zeros_like(dv_acc)

        q2s = [_stack_heads(q_ref[:, _lanes(g)], scale) for g in range(G)]
        do2s = [_stack_heads(do_ref[:, _lanes(g)].astype(BF16)) for g in range(G)]
        ones = jnp.ones((LANES, 2 * QB), BF16)
        dtot2s = []
        for g in range(G):
            ov = o_ref[:, _lanes(g)]
            dtot2s.append(_dot2(do2s[g].astype(F32) * jnp.concatenate([ov, ov], axis=0), ones))
        dq_acc[...] = jnp.zeros_like(dq_acc)
        ce_ref[...] = jnp.zeros_like(ce_ref)
        cr_ref[...] = jnp.zeros_like(cr_ref)

        def step(g, j, older, mask):
            n = older + QB
            q2, do2, dtot2 = q2s[g], do2s[g], dtot2s[g]
            start = pl.multiple_of(j * QB, QB)
            kb = k_ref[pl.ds(start, n), _lanes(g)]
            vb = v_ref[pl.ds(start, n), _lanes(g)]
            z = lax.dot_general(q2, kb, (((1,), (1,)), ((), ())), preferred_element_type=F32)
            e = jnp.exp(-jnp.abs(z))
            sp = jnp.log(1.0 + e)
            a = jnp.minimum(z, 0.0) - sp
            b = jnp.minimum(-z, 0.0) - sp
            rinv = 1.0 / (1.0 + e)
            pos = z >= 0.0
            beta = jnp.where(pos, rinv, e * rinv)
            one_m_beta = jnp.where(pos, e * rinv, rinv)
            if mask is not None:
                b = jnp.where(mask, b, 0.0)
            dw = lax.dot_general(do2, vb, (((1,), (1,)), ((), ())), preferred_element_type=F32)
            res = _dot2(b, _tri_and_ones(True, n))
            excl = res[:, :n]
            if mask is None:
                excl = excl + ce_ref[g]
            w = jnp.exp(a + excl)
            if mask is not None:
                w = jnp.where(mask, w, 0.0)
            ce_ref[g] += res[:, n:]
            gw = w * dw
            res2 = _dot2(gw, _tri_and_ones(False, n))
            rinc = res2[:, :n]
            if mask is None:
                rinc = rinc + cr_ref[g]
            cr_ref[g] += res2[:, n:]
            dz = gw * one_m_beta - beta * (dtot2[:, :n] - rinc)
            if mask is not None:
                dz = jnp.where(mask, dz, 0.0)
            dzb = dz.astype(BF16)
            dq_acc[g] += jnp.dot(dzb, kb, preferred_element_type=F32)
            dk_acc[pl.ds(start, n), _lanes(g)] += lax.dot_general(
                dzb, q2, (((0,), (0,)), ((), ())), preferred_element_type=F32)
            dv_acc[pl.ds(start, n), _lanes(g)] += lax.dot_general(
                w.astype(BF16), do2, (((0,), (0,)), ((), ())), preferred_element_type=F32)

        @pl.when(i == 0)
        def _():
            for g in range(G):
                step(g, 0, 0, _causal_valid())

        @pl.when(i > 0)
        def _():
            for g in range(G):
                step(g, i - 1, QB, _causal_valid(QB))
            for g in range(G):
                _sweep_older(i, lambda j, _, g=g: step(g, j, 0, None), ce_ref.at[g], 1)

        for g in range(G):
            dq_ref[:, _lanes(g)] = (_unstack_heads(dq_acc[g]) * scale).astype(dq_ref.dtype)

        @pl.when(i == nb - 1)
        def _():
            dk_ref[...] = dk_acc[...].astype(dk_ref.dtype)
            dv_ref[...] = dv_acc[...].astype(dv_ref.dtype)

    n_steps = n_pairs // G
    blk = pl.BlockSpec((QB, W), lambda p, i: (i, p))
    col = pl.BlockSpec((L, W), lambda p, i: (0, p))
    return _call_with_job(
        body, job, name=name, grid=(n_steps, nb),
        in_specs=[blk,
                  pl.BlockSpec((L, W), lambda p, i: (0, n_steps + p)),
                  pl.BlockSpec((L, W), lambda p, i: (0, 2 * n_steps + p)),
                  blk, blk],
        out_specs=[blk, col, col],
        out_shape=[jax.ShapeDtypeStruct((L, sb_width), BF16)] * 3,
        scratch_shapes=[pltpu.VMEM((G, 2 * QB, LANES), F32), pltpu.VMEM((L, W), F32),
                        pltpu.VMEM((L, W), F32), pltpu.VMEM((G, 2 * QB, LANES), F32),
                        pltpu.VMEM((G, 2 * QB, LANES), F32)],
        args=(qkv, qkv, qkv, o, dmixed))


def _conv_tile(L):
    return _pick(L, (384, 128))


def _glu(x, C):
    return x[:, :C] * _sigmoid(x[:, C:])


CONV_CHUNK = 32
SHIFT_TAIL = 24


def _fill_shifted(src_ref, dst_ref):
    n = dst_ref.shape[1]
    for r in range(1, 8):
        dst_ref[r - 1] = src_ref[r:r + n, :]


def _rows_at(src_ref, shifted_ref, start, n):
    q, r = divmod(start, 8)
    if r == 0:
        return src_ref[start:start + n, :]
    return shifted_ref[r - 1, 8 * q:8 * q + n, :]


def _conv_fwd(cacg, dw_w, dw_b, ln_g, ln_b, name):
    L, C2 = cacg.shape
    C = C2 // 2
    T = _conv_tile(L)
    H = CONV_HALO
    K = dw_w.shape[0]
    CH = CONV_CHUNK

    def body(x_ref, prev_ref, w_ref, b_ref, g_ref, beta_ref, o_ref, u_ref, us_ref):
        i = pl.program_id(0)
        u_ref[0:H, :] = jnp.where(i > 0, _glu(prev_ref[...], C), 0.0)
        u_ref[H:, :] = _glu(x_ref[...], C)
        _fill_shifted(u_ref, us_ref)
        for c0 in range(0, T, CH):
            y = jnp.broadcast_to(b_ref[...], (CH, C))
            for k in range(K):
                y = y + w_ref[k:k + 1, :] * _rows_at(u_ref, us_ref, c0 + H - (K - 1) + k, CH)
            mu = jnp.mean(y, axis=-1, keepdims=True)
            yc = y - mu
            rstd = lax.rsqrt(jnp.mean(yc * yc, axis=-1, keepdims=True) + EPS)
            ln = yc * rstd * g_ref[...] + beta_ref[...]
            o_ref[c0:c0 + CH, :] = (ln * _sigmoid(ln)).astype(o_ref.dtype)

    vec = pl.BlockSpec((1, C), lambda i: (0, 0))
    return pl.pallas_call(
        body, name=name, grid=(L // T,),
        in_specs=[pl.BlockSpec((T, C2), lambda i: (i, 0)),
                  pl.BlockSpec((H, C2), lambda i: (jnp.maximum(i * (T // H) - 1, 0), 0)),
                  pl.BlockSpec((K, C), lambda i: (0, 0)), vec, vec, vec],
        out_specs=pl.BlockSpec((T, C), lambda i: (i, 0)),
        out_shape=jax.ShapeDtypeStruct((L, C), BF16),
        scratch_shapes=[pltpu.VMEM((T + H, C), F32), pltpu.VMEM((7, T + SHIFT_TAIL, C), F32)],
        compiler_params=_params(("parallel",)),
    )(cacg, cacg, dw_w, dw_b, ln_g, ln_b)


def _conv_bwd(cacg, dmixed, dw_w, dw_b, ln_g, ln_b, name):
    L, C2 = cacg.shape
    C = C2 // 2
    T = _conv_tile(L)
    H = CONV_HALO
    K = dw_w.shape[0]
    nt = L // T
    TE = T + H

    CH = CONV_CHUNK

    def body(x_ref, prev_ref, next_ref, d_ref, dnext_ref, w_ref, b_ref, g_ref, beta_ref,
             dca_ref, dcg_ref, dwt_ref, db_ref, dg_ref, dbeta_ref, u_ref, us_ref, dy_ref, dys_ref):
        i = pl.program_id(0)
        last = i == nt - 1

        @pl.when(i == 0)
        def _():
            dwt_ref[...] = jnp.zeros_like(dwt_ref)
            db_ref[...] = jnp.zeros_like(db_ref)
            dg_ref[...] = jnp.zeros_like(dg_ref)
            dbeta_ref[...] = jnp.zeros_like(dbeta_ref)

        u_ref[0:H, :] = jnp.where(i > 0, _glu(prev_ref[...], C), 0.0)
        u_ref[H:H + T, :] = _glu(x_ref[...], C)
        u_ref[H + T:, :] = _glu(next_ref[...], C)
        _fill_shifted(u_ref, us_ref)
        dg_acc = jnp.zeros((1, C), F32)
        dbeta_acc = jnp.zeros((1, C), F32)
        db_acc = jnp.zeros((1, C), F32)
        for c0 in range(0, TE, CH):
            y = jnp.broadcast_to(b_ref[...], (CH, C))
            for k in range(K):
                y = y + w_ref[k:k + 1, :] * _rows_at(u_ref, us_ref, c0 + H - (K - 1) + k, CH)
            mu = jnp.mean(y, axis=-1, keepdims=True)
            yc = y - mu
            rstd = lax.rsqrt(jnp.mean(yc * yc, axis=-1, keepdims=True) + EPS)
            yh = yc * rstd
            ln = yh * g_ref[...] + beta_ref[...]
            s = _sigmoid(ln)
            dout = d_ref[c0:c0 + CH, :] if c0 < T else jnp.where(last, 0.0, dnext_ref[c0 - T:c0 - T + CH, :])
            dln = dout * (s * (1.0 + ln * (1.0 - s)))
            dyh = dln * g_ref[...]
            dy = rstd * (dyh - jnp.mean(dyh, axis=-1, keepdims=True)
                         - yh * jnp.mean(dyh * yh, axis=-1, keepdims=True))
            dy_ref[c0:c0 + CH, :] = dy
            if c0 < T:
                dg_acc = dg_acc + jnp.sum(dln * yh, axis=0, keepdims=True)
                dbeta_acc = dbeta_acc + jnp.sum(dln, axis=0, keepdims=True)
                db_acc = db_acc + jnp.sum(dy, axis=0, keepdims=True)
        dg_ref[...] += dg_acc
        dbeta_ref[...] += dbeta_acc
        db_ref[...] += db_acc
        _fill_shifted(dy_ref, dys_ref)
        for k in range(K):
            dwt_ref[k:k + 1, :] += jnp.sum(
                dy_ref[0:T, :] * _rows_at(u_ref, us_ref, H - (K - 1) + k, T), axis=0, keepdims=True)
        for c0 in range(0, T, CH):
            du = jnp.zeros((CH, C), F32)
            for k in range(K):
                du = du + w_ref[k:k + 1, :] * _rows_at(dy_ref, dys_ref, c0 + (K - 1) - k, CH)
            x = x_ref[c0:c0 + CH, :]
            sg = _sigmoid(x[:, C:])
            dca_ref[c0:c0 + CH, :] = (du * sg).astype(dca_ref.dtype)
            dcg_ref[c0:c0 + CH, :] = (du * x[:, :C] * sg * (1.0 - sg)).astype(dcg_ref.dtype)

    nh = L // H
    vec = pl.BlockSpec((1, C), lambda i: (0, 0))
    row = pl.BlockSpec((T, C), lambda i: (i, 0))
    return pl.pallas_call(
        body, name=name, grid=(nt,),
        in_specs=[pl.BlockSpec((T, C2), lambda i: (i, 0)),
                  pl.BlockSpec((H, C2), lambda i: (jnp.maximum(i * (T // H) - 1, 0), 0)),
                  pl.BlockSpec((H, C2), lambda i: (jnp.minimum((i + 1) * (T // H), nh - 1), 0)),
                  pl.BlockSpec((T, C), lambda i: (i, 1)),
                  pl.BlockSpec((H, C), lambda i: (jnp.minimum((i + 1) * (T // H), nh - 1), 1)),
                  pl.BlockSpec((K, C), lambda i: (0, 0)), vec, vec, vec],
        out_specs=[row, row, pl.BlockSpec((H, C), lambda i: (0, 0)), vec, vec, vec],
        out_shape=[jax.ShapeDtypeStruct((L, C), BF16), jax.ShapeDtypeStruct((L, C), BF16),
                   jax.ShapeDtypeStruct((H, C), F32), jax.ShapeDtypeStruct((1, C), F32),
                   jax.ShapeDtypeStruct((1, C), F32), jax.ShapeDtypeStruct((1, C), F32)],
        scratch_shapes=[pltpu.VMEM((T + 2 * H, C), F32), pltpu.VMEM((7, TE + SHIFT_TAIL, C), F32),
                        pltpu.VMEM((TE, C), F32), pltpu.VMEM((7, T + SHIFT_TAIL, C), F32)],
        compiler_params=_params(("arbitrary",)),
    )(cacg, cacg, cacg, dmixed, dmixed, dw_w, dw_b, ln_g, ln_b)


def _local_step(h0, target, n_meta, seq, norms, conv_p, wts, final_g, gather_behind, reducer):
    mix_g, ffn_g = norms
    dw_w, dw_b, ln_g, ln_b = conv_p
    depth = mix_g.shape[0]
    C = dw_b.shape[-1]
    sbw = (wts["w_in"][0].shape[-1] - 2 * C) // 3
    assert sbw == C, "the mixer halves must have equal width"
    row = lambda a, i: a[i][None, :]

    h = h0
    saved = []
    for i in range(depth):
        hn, proj_qkv = _norm_in_proj(h, row(mix_g, i), wts["w_in"][i], 3 * sbw, f"in_qkv_{i}")
        cacg = _mm_nn([(hn, wts["w_in"][i], 0, 0)], F32, f"in_conv_{i}", cols=(3 * sbw, 2 * C))
        job, keys = gather_behind.get(i, (None, ()))
        attn, arrived = _attn_fwd(proj_qkv, sbw, f"attn_fwd_{i}", job)
        for (wname, wl), arr in zip(keys, arrived):
            wts[wname][wl] = arr
        conv = _conv_fwd(cacg, dw_w[i], row(dw_b, i), row(ln_g, i), row(ln_b, i), f"conv_fwd_{i}")
        h_mid = _mm_nn([(attn, wts["w_out"][i], 0, 0), (conv, wts["w_out"][i], 0, 1)], F32, f"out_proj_{i}",
                       residual=h)
        hn2, g, u, act = _ffn_up(h_mid, row(ffn_g, i), wts["w_gate_t"][i], wts["w_up_t"][i], f"ffn_up_{i}")
        h_out = _mm_nn([(act, wts["w_down"][i], 0, 0)], F32, f"down_{i}", residual=h_mid)
        saved.append((h, hn, proj_qkv, cacg, attn, conv, h_mid, hn2, g, u, act))
        h = h_out

    loss, dh, d_final_g = _loss_head(h, final_g[None, :], target, n_meta, seq, "loss_head")

    grads = {k: [None] * depth for k in ("mix_g", "ffn_g", "dw_w", "dw_b", "ln_g", "ln_b")}
    for i in reversed(range(depth)):
        h_in, hn, proj_qkv, cacg, attn, conv, h_mid, hn2, g, u, act = saved[i]
        big = {}
        dg, du = _ffn_down_bwd(dh, wts["w_down"][i], g, u, f"ffn_down_bwd_{i}")
        big["w_down"] = _mm_tn(act, dh, f"dw_down_{i}", col_sharded=False)
        dhn2 = _mm_nn([(dg, wts["w_gate_t"][i], 0, 0), (du, wts["w_up_t"][i], 0, 0)], F32, f"d_hn2_{i}")
        big["w_gate_t"] = _mm_tn(dg, hn2, f"dw_gate_{i}", col_sharded=False)
        big["w_up_t"] = _mm_tn(du, hn2, f"dw_up_{i}", col_sharded=False)
        dh, d_ffn = _rmsnorm_bwd(h_mid, row(ffn_g, i), dhn2, dh, f"ffn_norm_bwd_{i}")
        dmixed = _mm_nt([(dh, wts["w_out"][i], 0, 0)], F32, f"d_mixed_{i}")
        big["w_out"] = jnp.concatenate([_mm_tn(attn, dh, f"dw_out_attn_{i}", col_sharded=False, chips=2),
                                        _mm_tn(conv, dh, f"dw_out_conv_{i}", col_sharded=False, chips=2)], axis=0)
        reducer.ready(i, big)
        job, sink = reducer.take()
        (dq, dk, dv), arrived = _attn_bwd(proj_qkv, attn, dmixed, sbw, f"attn_bwd_{i}", job)
        sink(arrived)
        dca, dcg, d_dw, d_b, d_lg, d_lb = _conv_bwd(
            cacg, dmixed, dw_w[i], row(dw_b, i), row(ln_g, i), row(ln_b, i), f"conv_bwd_{i}")
        grads["dw_w"][i] = d_dw[:dw_w.shape[1]]
        grads["dw_b"][i], grads["ln_g"][i], grads["ln_b"][i] = d_b[0], d_lg[0], d_lb[0]
        dproj = jnp.concatenate([dq, dk, dv, dca, dcg], axis=1)
        dhn = _mm_nt([(dproj, wts["w_in"][i], 0, 0)], F32, f"d_hn_{i}")
        reducer.ready(i, {"w_in": _mm_tn(hn, dproj, f"dw_in_{i}", col_sharded=True)})
        dh, d_mix = _rmsnorm_bwd(h_in, row(mix_g, i), dhn, dh, f"mix_norm_bwd_{i}")
        grads["mix_g"][i], grads["ffn_g"][i] = d_mix[0], d_ffn[0]
    grads["final_g"] = d_final_g[0]
    return loss, dh, grads


ANY = pl.BlockSpec(memory_space=pl.ANY)


def _position():
    return lax.axis_index("x"), lax.axis_index("y"), lax.axis_index("c")


def _chip_at(x, y, k):
    return (1 - x if k & 2 else x), (1 - y if k & 1 else y)


def _half_rows(ref, half, rows, base=0):
    start = pl.multiple_of(base + half * rows, 8)
    lead = (slice(None),) * (len(ref.shape) - 2)
    return ref.at[(*lead, pl.ds(start, rows), slice(None))]


def _gather_job(fulls, shard_shapes, col_sharded):
    n = len(fulls)

    def tools(f_refs, send_sems, recv_sems):
        def block(wi, chip, half):
            _, R, C = shard_shapes[wi]
            if col_sharded[wi]:
                cols = pl.ds(pl.multiple_of(chip * C, LANES), C)
                return f_refs[wi].at[:, pl.ds(pl.multiple_of(half * (R // 2), 8), R // 2), cols]
            return _half_rows(f_refs[wi], half, R // 2, base=chip * R)

        def copy(wi, slot, blk, to):
            return pltpu.make_async_remote_copy(
                src_ref=blk, dst_ref=blk, send_sem=send_sems.at[6 * wi + slot],
                recv_sem=recv_sems.at[6 * wi + slot], device_id=to, device_id_type=MESH)

        return block, copy

    def start(_, f_refs, send_sems, recv_sems):
        block, copy = tools(f_refs, send_sems, recv_sems)
        x, y, c = _position()
        me = 2 * x + y
        for wi in range(n):
            for k in (1, 2, 3):
                copy(wi, k - 1, block(wi, me, c), (*_chip_at(x, y, k), c)).start()

    def finish(_, f_refs, send_sems, recv_sems):
        block, copy = tools(f_refs, send_sems, recv_sems)
        x, y, c = _position()
        me = 2 * x + y
        for wi in range(n):
            for k in (1, 2, 3):
                landed = block(wi, me ^ k, c)
                copy(wi, k - 1, landed, (x, y, c)).wait_recv()
                copy(wi, 2 + k, landed, (x, y, 1 - c)).start()
        for wi in range(n):
            for k in (1, 2, 3):
                copy(wi, 2 + k, block(wi, me ^ k, 1 - c), (x, y, c)).wait_recv()
        for wi in range(n):
            for k in (1, 2, 3):
                copy(wi, k - 1, block(wi, me, c), (x, y, c)).wait_send()
                copy(wi, 2 + k, block(wi, me ^ k, c), (x, y, c)).wait_send()

    return _CommJob(fulls, [jax.ShapeDtypeStruct(f.shape, f.dtype) for f in fulls], {i: i for i in range(n)},
                    6 * n, start, finish)


def _place_shard(w, layer, chip, col_sharded, dtype, name):
    _, R, C = w.shape
    tr = _pick(R, (256, 352, 128, 48))
    nr = R // tr

    def body(chip_ref, w_ref, o_ref):
        o_ref[...] = w_ref[...].astype(dtype)

    if col_sharded:
        shape = (1, R, N_CHIPS * C)
        out_spec = pl.BlockSpec((None, tr, C), lambda r, chip_ref: (0, r, chip_ref[0]))
    else:
        shape = (1, N_CHIPS * R, C)
        out_spec = pl.BlockSpec((None, tr, C), lambda r, chip_ref: (0, chip_ref[0] * nr + r, 0))
    grid_spec = pltpu.PrefetchScalarGridSpec(
        num_scalar_prefetch=1, grid=(nr,),
        in_specs=[pl.BlockSpec((None, tr, C), lambda r, chip_ref: (layer, r, 0))], out_specs=out_spec)
    return pl.pallas_call(
        body, name=name, grid_spec=grid_spec, out_shape=jax.ShapeDtypeStruct(shape, dtype),
        compiler_params=_params(("parallel",)),
    )(chip, w)


def _rs_to_sibling(grads, name):
    n = len(grads)
    outs = [jax.ShapeDtypeStruct((g.shape[0], g.shape[1] // 2, g.shape[2]), g.dtype) for g in grads]

    def body(*refs):
        g_refs, l_refs = refs[:n], refs[n:2 * n]
        send_sems, recv_sems = refs[2 * n:]
        x, y, c = _position()
        cps = []
        for wi in range(n):
            R = grads[wi].shape[1]
            cp = pltpu.make_async_remote_copy(
                src_ref=_half_rows(g_refs[wi], 1 - c, R // 2), dst_ref=l_refs[wi],
                send_sem=send_sems.at[wi], recv_sem=recv_sems.at[wi],
                device_id=(x, y, 1 - c), device_id_type=MESH)
            cp.start()
            cps.append(cp)
        for cp in cps:
            cp.wait()

    return pl.pallas_call(
        body, name=name, out_shape=outs, in_specs=[ANY] * n, out_specs=[ANY] * n,
        scratch_shapes=[pltpu.SemaphoreType.DMA((n,)), pltpu.SemaphoreType.DMA((n,))],
    )(*grads)


def _chip_sum(g, landed, core, name):
    _, R, C = g.shape
    hr = R // 2
    tr = _pick(hr, (256, 352, 128))
    nr = hr // tr

    def body(c_ref, g_ref, l_ref, o_ref):
        o_ref[...] = (g_ref[...] + l_ref[...]).astype(BF16)

    grid_spec = pltpu.PrefetchScalarGridSpec(
        num_scalar_prefetch=1, grid=(N_CHIPS, nr),
        in_specs=[pl.BlockSpec((None, tr, C), lambda j, r, c_ref: (j, c_ref[0] * nr + r, 0)),
                  pl.BlockSpec((None, tr, C), lambda j, r, c_ref: (j, r, 0))],
        out_specs=pl.BlockSpec((None, tr, C), lambda j, r, c_ref: (j, r, 0)))
    return pl.pallas_call(
        body, name=name, grid_spec=grid_spec, out_shape=jax.ShapeDtypeStruct((N_CHIPS, hr, C), BF16),
        compiler_params=_params(("parallel", "parallel")),
    )(core, g, landed)


def _across_job(parts):
    n = len(parts)

    def copy(p_refs, l_refs, send_sems, recv_sems, wi, k, to):
        x, y, _ = _position()
        me = 2 * x + y
        return pltpu.make_async_remote_copy(
            src_ref=p_refs[wi].at[me ^ k], dst_ref=l_refs[wi].at[me],
            send_sem=send_sems.at[3 * wi + k - 1], recv_sem=recv_sems.at[3 * wi + k - 1],
            device_id=to, device_id_type=MESH)

    def start(p_refs, l_refs, send_sems, recv_sems):
        x, y, c = _position()
        for wi in range(n):
            for k in (1, 2, 3):
                copy(p_refs, l_refs, send_sems, recv_sems, wi, k, (*_chip_at(x, y, k), c)).start()

    def finish(p_refs, l_refs, send_sems, recv_sems):
        x, y, c = _position()
        me = 2 * x + y
        for wi in range(n):
            for k in (1, 2, 3):
                slot = l_refs[wi].at[me ^ k]
                pltpu.make_async_remote_copy(
                    src_ref=slot, dst_ref=slot, send_sem=send_sems.at[3 * wi + k - 1],
                    recv_sem=recv_sems.at[3 * wi + k - 1], device_id=(x, y, c), device_id_type=MESH).wait_recv()
        for wi in range(n):
            for k in (1, 2, 3):
                copy(p_refs, l_refs, send_sems, recv_sems, wi, k, (x, y, c)).wait_send()

    return _CommJob(parts, [jax.ShapeDtypeStruct(p.shape, p.dtype) for p in parts], {}, 3 * n, start, finish)


class _Reducer:
    def __init__(self, core):
        self.core, self.parts, self.across, self.pending = core, {}, {}, []

    def ready(self, layer, big):
        names = list(big)
        flat = [big[k] for k in names]
        landed = _rs_to_sibling(flat, f"grads_to_sibling_{names[0]}_{layer}")
        for k, g, la in zip(names, flat, landed):
            self.parts[k, layer] = _chip_sum(g, la, self.core, f"chip_sum_{k}_{layer}")
            self.pending.append((k, layer))

    def take(self):
        keys, self.pending = self.pending, []

        def sink(results):
            self.across.update(zip(keys, results))

        return _across_job([self.parts[key] for key in keys]), sink


def _sum_chips(parts, landed, where, layer, depth, prev, name):
    _, hr, C = landed.shape
    tr = _pick(hr, (256, 352, 128))
    nr = hr // tr

    def body(*refs):
        own_ref, slots, o_ref = refs[1], refs[2:2 + N_CHIPS], refs[-1]
        chip = refs[0][0]
        total = None
        for q in range(N_CHIPS):
            term = jnp.where(chip == q, own_ref[...], slots[q][...]).astype(F32)
            total = term if total is None else total + term
        o_ref[...] = total

    def slot_spec(q):
        return pl.BlockSpec((None, tr, C), lambda r, w: (jnp.where(w[0] == q, (q + 1) % N_CHIPS, q), r, 0))

    in_specs = [pl.BlockSpec((None, tr, C), lambda r, w: (w[0], r, 0))] + [slot_spec(q) for q in range(N_CHIPS)]
    args = [where, parts] + [landed] * N_CHIPS
    aliases = {}
    if prev is not None:
        in_specs.append(ANY)
        args.append(prev)
        aliases = {len(args) - 1: 0}
    grid_spec = pltpu.PrefetchScalarGridSpec(
        num_scalar_prefetch=1, grid=(nr,), in_specs=in_specs,
        out_specs=pl.BlockSpec((None, tr, C), lambda r, w: (layer, w[1] * nr + r, 0)))
    return pl.pallas_call(
        body, name=name, grid_spec=grid_spec, out_shape=jax.ShapeDtypeStruct((depth, 2 * hr, C), F32),
        input_output_aliases=aliases, compiler_params=_params(("parallel",)),
    )(*args)


def _rs_join_halves(reduced):
    n = len(reduced)

    def body(*refs):
        o_refs = refs[n:2 * n]
        send_sems, recv_sems = refs[2 * n:]
        x, y, c = _position()
        sent = []
        for wi in range(n):
            hr = reduced[wi].shape[1] // 2
            mine = _half_rows(o_refs[wi], c, hr)
            cp = pltpu.make_async_remote_copy(
                src_ref=mine, dst_ref=mine, send_sem=send_sems.at[wi], recv_sem=recv_sems.at[wi],
                device_id=(x, y, 1 - c), device_id_type=MESH)
            cp.start()
            sent.append(cp)
        for wi in range(n):
            hr = reduced[wi].shape[1] // 2
            theirs = _half_rows(o_refs[wi], 1 - c, hr)
            pltpu.make_async_remote_copy(
                src_ref=theirs, dst_ref=theirs, send_sem=send_sems.at[wi], recv_sem=recv_sems.at[wi],
                device_id=(x, y, c), device_id_type=MESH).wait_recv()
        for cp in sent:
            cp.wait_send()

    return pl.pallas_call(
        body, name="grads_join_halves", out_shape=[jax.ShapeDtypeStruct(r.shape, r.dtype) for r in reduced],
        in_specs=[ANY] * n, out_specs=[ANY] * n, input_output_aliases={i: i for i in range(n)},
        scratch_shapes=[pltpu.SemaphoreType.DMA((n,)), pltpu.SemaphoreType.DMA((n,))],
    )(*reduced)


def _small_allreduce(vec):
    rows = vec.shape[0]

    def body(v_ref, o_ref, land, send_sems, recv_sems):
        x, y, c = _position()
        me = 4 * x + 2 * y + c
        land[0] = v_ref[...]
        sent = []
        for k in range(1, N_DEV):
            to = (1 - x if k & 4 else x, 1 - y if k & 2 else y, 1 - c if k & 1 else c)
            cp = pltpu.make_async_remote_copy(
                src_ref=v_ref, dst_ref=land.at[k], send_sem=send_sems.at[k - 1], recv_sem=recv_sems.at[k - 1],
                device_id=to, device_id_type=MESH)
            cp.start()
            sent.append(cp)
        for cp in sent:
            cp.wait_recv()
        acc = land[me]
        for e in range(1, N_DEV):
            acc = acc + land[me ^ e]
        o_ref[...] = acc
        for cp in sent:
            cp.wait_send()

    vmem = pl.BlockSpec(memory_space=pltpu.VMEM)
    return pl.pallas_call(
        body, name="small_allreduce", out_shape=jax.ShapeDtypeStruct(vec.shape, F32),
        in_specs=[vmem], out_specs=vmem,
        scratch_shapes=[pltpu.VMEM((N_DEV, rows, LANES), F32), pltpu.SemaphoreType.DMA((N_DEV - 1,)),
                        pltpu.SemaphoreType.DMA((N_DEV - 1,))],
    )(vec)


def _adam_math(w, g, m, v):
    m = ADAM_B1 * m + (1.0 - ADAM_B1) * g
    v = ADAM_B2 * v + (1.0 - ADAM_B2) * jnp.square(g)
    m_hat = m / (1.0 - ADAM_B1 ** ADAM_STEP)
    v_hat = v / (1.0 - ADAM_B2 ** ADAM_STEP)
    delta = -ADAM_LR * (m_hat / (jnp.sqrt(v_hat) + ADAM_EPS) + ADAM_WD * w)
    return delta, m, v


def _adam(w, g, m, v, name):
    def body(w_ref, g_ref, m_ref, v_ref, d_ref, nm_ref, nv_ref):
        d_ref[...], nm_ref[...], nv_ref[...] = _adam_math(w_ref[...], g_ref[...], m_ref[...], v_ref[...])

    if w.ndim == 3:
        lyr, R, C = w.shape
        tr = _pick(R, (256, 352, 128))
        blk = pl.BlockSpec((None, tr, C), lambda l, r: (l, r, 0))
        grid, sem = (lyr, R // tr), ("parallel", "parallel")
    else:
        blk = pl.BlockSpec(w.shape, lambda: (0, 0))
        grid, sem = (), None
    return pl.pallas_call(
        body, name=name, grid=grid, in_specs=[blk] * 4, out_specs=[blk] * 3,
        out_shape=[jax.ShapeDtypeStruct(w.shape, F32)] * 3, compiler_params=_params(sem),
    )(w, g, m, v)


def _rows(a, pad_to=8):
    r = a.reshape(-1, LANES)
    extra = (-r.shape[0]) % pad_to
    return jnp.pad(r, ((0, extra), (0, 0))) if extra else r


def _pack(arrays):
    return jnp.concatenate([_rows(a) for a in arrays], axis=0)


def _unpack(slab, shapes):
    out, at = [], 0
    for shp in shapes:
        nrow = math.prod(shp) // LANES
        out.append(slab[at:at + nrow].reshape(shp))
        at += nrow + (-nrow) % 8
    return out


BIG = ("w_in", "w_out", "w_gate_t", "w_up_t", "w_down")
BIG_COL_SHARDED = (True, False, False, False, False)
TRANSPOSED = {"w_gate_t": "w_gate", "w_up_t": "w_up"}


def kernel(x, meta_tokens, mix_norm_g, w_in, conv_dw_w, conv_dw_b, conv_ln_g, conv_ln_b, w_out, ffn_norm_g, w_gate, w_up, w_down, final_norm_g, loss_target, m_meta_tokens, m_mix_norm_g, m_w_in, m_conv_dw_w, m_conv_dw_b, m_conv_ln_g, m_conv_ln_b, m_w_out, m_ffn_norm_g, m_w_gate, m_w_up, m_w_down, m_final_norm_g, v_meta_tokens, v_mix_norm_g, v_w_in, v_conv_dw_w, v_conv_dw_b, v_conv_ln_g, v_conv_ln_b, v_w_out, v_ffn_norm_g, v_w_gate, v_w_up, v_w_down, v_final_norm_g):
    n_meta, seq = meta_tokens.shape[0], x.shape[1]
    D = x.shape[2]
    depth, taps, c_shard = conv_dw_w.shape
    C = conv_dw_b.shape[-1]
    chip = (2 * lax.axis_index("x") + lax.axis_index("y")).astype(jnp.int32)
    core = lax.axis_index("c").astype(jnp.int32).reshape(1)
    chip1 = chip.reshape(1)
    where = jnp.concatenate([chip1, core])
    big_w = dict(w_in=w_in, w_out=w_out, w_gate=w_gate, w_up=w_up, w_down=w_down)
    big_m = dict(w_in=m_w_in, w_out=m_w_out, w_gate=m_w_gate, w_up=m_w_up, w_down=m_w_down)
    big_v = dict(w_in=v_w_in, w_out=v_w_out, w_gate=v_w_gate, w_up=v_w_up, w_down=v_w_down)

    small_shard = _pack([conv_dw_w, meta_tokens])[None]
    to_send = {k: jnp.swapaxes(big_w[TRANSPOSED[k]], 1, 2) if k in TRANSPOSED else big_w[k] for k in BIG}
    col = dict(zip(BIG, BIG_COL_SHARDED))
    wts = {k: [_place_shard(to_send[k], l, chip1, col[k], BF16, f"place_{k}_{l}") for l in range(depth)] for k in BIG}
    small_placed = _place_shard(small_shard, 0, chip1, False, F32, "place_small")

    def gather_job(keys, extra=()):
        arrays = [wts[k][l] for k, l in keys] + list(extra)
        shapes = [(1,) + to_send[k].shape[1:] for k, _ in keys] + [(1,) + small_shard.shape[1:]] * len(extra)
        return _gather_job(arrays, shapes, [col[k] for k, _ in keys] + [False] * len(extra))

    first_keys = [("w_in", 0), ("w_out", 0)]
    *first, small_full = _run_job(gather_job(first_keys, [small_placed]), "gather_first")
    for (k, l), arr in zip(first_keys, first):
        wts[k][l] = arr
    ffn = lambda l: [("w_gate_t", l), ("w_up_t", l), ("w_down", l)]
    behind_keys = {0: ffn(0) + [("w_in", l) for l in range(1, depth)] + [("w_out", l) for l in range(1, depth)]}
    for l in range(1, depth):
        behind_keys[l] = ffn(l)
    gather_behind = {l: (gather_job(keys), keys) for l, keys in behind_keys.items()}

    rows_shard = small_shard.shape[1]
    dw_full, meta_full = [], []
    for j in range(N_CHIPS):
        dwj, mj = _unpack(small_full[0, j * rows_shard:(j + 1) * rows_shard],
                          [conv_dw_w.shape, meta_tokens.shape])
        dw_full.append(dwj)
        meta_full.append(mj)
    dw_w_full = jnp.concatenate(dw_full, axis=2)
    meta = jnp.concatenate(meta_full, axis=1)

    L = n_meta + seq
    Lp = -(-L // QUERY_BLOCK) * QUERY_BLOCK
    h0 = jnp.concatenate([meta, x[0], jnp.zeros((Lp - L, D), F32)], axis=0)
    target = jnp.pad(loss_target[0], ((n_meta, Lp - L), (0, 0)))
    reducer = _Reducer(core)
    loss, dh0, grads = _local_step(h0, target, n_meta, seq, (mix_norm_g, ffn_norm_g),
                                   (dw_w_full, conv_dw_b, conv_ln_g, conv_ln_b), wts, final_norm_g,
                                   gather_behind, reducer)
    loss = lax.psum(loss[0, 0], ("x", "y", "c"))
    grad_x = dh0[n_meta:L][None]
    job, sink = reducer.take()
    sink(_run_job(job, "grads_across_last"))

    reduced = []
    for k in BIG:
        arr = None
        for l in range(depth):
            arr = _sum_chips(reducer.parts[k, l], reducer.across[k, l], where, l, depth, arr, f"sum_chips_{k}_{l}")
        reduced.append(arr)
    big_g = dict(zip(BIG, _rs_join_halves(reduced)))

    small_names = ("mix_g", "ffn_g", "dw_b", "ln_g", "ln_b", "dw_w")
    small = [jnp.stack(grads[k]) for k in small_names] + [grads["final_g"], dh0[:n_meta]]
    small_shapes = [a.shape for a in small]
    g_mix, g_ffn, g_dwb, g_lng, g_lnb, g_dww, g_final, g_meta = _unpack(_small_allreduce(_pack(small)), small_shapes)
    g_dww = lax.dynamic_slice_in_dim(g_dww, chip * c_shard, c_shard, axis=2)
    g_meta = lax.dynamic_slice_in_dim(g_meta, chip * meta_tokens.shape[1], meta_tokens.shape[1], axis=1)

    out_g, out_d, out_m, out_v = {}, {}, {}, {}
    for kk in BIG:
        k = TRANSPOSED.get(kk, kk)
        view = (lambda a: jnp.swapaxes(a, 1, 2)) if kk in TRANSPOSED else (lambda a: a)
        res = _adam(view(big_w[k]), big_g[kk], view(big_m[k]), view(big_v[k]), f"adam_{k}")
        out_g[k] = view(big_g[kk])
        out_d[k], out_m[k], out_v[k] = (view(a) for a in res)
    small_order = ("meta_tokens", "mix_norm_g", "conv_dw_w", "conv_dw_b", "conv_ln_g", "conv_ln_b",
                   "ffn_norm_g", "final_norm_g")
    sw = dict(meta_tokens=meta_tokens, mix_norm_g=mix_norm_g, conv_dw_w=conv_dw_w, conv_dw_b=conv_dw_b,
              conv_ln_g=conv_ln_g, conv_ln_b=conv_ln_b, ffn_norm_g=ffn_norm_g, final_norm_g=final_norm_g)
    sm = dict(meta_tokens=m_meta_tokens, mix_norm_g=m_mix_norm_g, conv_dw_w=m_conv_dw_w, conv_dw_b=m_conv_dw_b,
              conv_ln_g=m_conv_ln_g, conv_ln_b=m_conv_ln_b, ffn_norm_g=m_ffn_norm_g, final_norm_g=m_final_norm_g)
    sv = dict(meta_tokens=v_meta_tokens, mix_norm_g=v_mix_norm_g, conv_dw_w=v_conv_dw_w, conv_dw_b=v_conv_dw_b,
              conv_ln_g=v_conv_ln_g, conv_ln_b=v_conv_ln_b, ffn_norm_g=v_ffn_norm_g, final_norm_g=v_final_norm_g)
    sg = dict(meta_tokens=g_meta, mix_norm_g=g_mix, conv_dw_w=g_dww, conv_dw_b=g_dwb, conv_ln_g=g_lng,
              conv_ln_b=g_lnb, ffn_norm_g=g_ffn, final_norm_g=g_final)
    def slab(d):
        return _pack([d[k] for k in small_order])
    shapes = [sw[k].shape for k in small_order]
    deltas = _adam(slab(sw), slab(sg), slab(sm), slab(sv), "adam_small")
    for res, dst in zip(deltas, (out_d, out_m, out_v)):
        dst.update(zip(small_order, _unpack(res, shapes)))
    out_g.update(sg)

    order = ("meta_tokens", "mix_norm_g", "w_in", "conv_dw_w", "conv_dw_b", "conv_ln_g", "conv_ln_b", "w_out",
             "ffn_norm_g", "w_gate", "w_up", "w_down", "final_norm_g")
    return (loss, grad_x, *[out_g[k] for k in order], *[out_d[k] for k in order],
            *[out_m[k] for k in order], *[out_v[k] for k in order])
```

```python
import functools
import math

import jax
import jax.numpy as jnp
from jax import lax
from jax.experimental import pallas as pl
from jax.experimental.pallas import tpu as pltpu

F32 = jnp.float32
BF16 = jnp.bfloat16
MESH = pl.DeviceIdType.MESH

EPS = 1e-6
QUERY_BLOCK = 128
LANES = 128
HEAD_DIM = 64
LOG_STICK_FLOOR = -40.0
CONV_HALO = 32
N_CHIPS = 4
N_DEV = 8
VMEM_LIMIT = 56 * 1024 * 1024

ADAM_LR = 0.001
ADAM_B1 = 0.9
ADAM_B2 = 0.999
ADAM_EPS = 1e-08
ADAM_WD = 0.01
ADAM_STEP = 10


def _pick(n, prefs):
    for p in prefs:
        if n % p == 0:
            return p
    return n


def _params(sem=None):
    return pltpu.CompilerParams(dimension_semantics=sem, vmem_limit_bytes=VMEM_LIMIT)


def _sigmoid(x):
    return 1.0 / (1.0 + jnp.exp(-x))


def _rmsnorm_bwd(h, g, dy, dh_in, name):
    L, D = h.shape
    T = _pick(L, (384, 128))

    def body(h_ref, g_ref, dy_ref, dhin_ref, dh_ref, dg_ref):
        x = h_ref[...]
        dyv = dy_ref[...]
        r = lax.rsqrt(jnp.mean(x * x, axis=-1, keepdims=True) + EPS)
        xh = x * r
        dxh = dyv * g_ref[...]
        dh_ref[...] = dhin_ref[...] + r * (dxh - xh * jnp.mean(dxh * xh, axis=-1, keepdims=True))

        @pl.when(pl.program_id(0) == 0)
        def _():
            dg_ref[...] = jnp.zeros_like(dg_ref)

        dg_ref[...] += jnp.sum(dyv * xh, axis=0, keepdims=True)

    row = pl.BlockSpec((T, D), lambda i: (i, 0))
    vec = pl.BlockSpec((1, D), lambda i: (0, 0))
    return pl.pallas_call(
        body, name=name, grid=(L // T,),
        in_specs=[row, vec, row, row], out_specs=[row, vec],
        out_shape=[jax.ShapeDtypeStruct((L, D), F32), jax.ShapeDtypeStruct((1, D), F32)],
        compiler_params=_params(("arbitrary",)),
    )(h, g, dy, dh_in)


def _loss_head(h, g, target, n_meta, seq, name):
    L, D = h.shape
    T = _pick(L, (384, 128))

    def body(h_ref, g_ref, t_ref, loss_ref, dh_ref, dg_ref):
        i = pl.program_id(0)
        x = h_ref[...]
        gv = g_ref[...]
        r = lax.rsqrt(jnp.mean(x * x, axis=-1, keepdims=True) + EPS)
        xh = x * r
        y = xh * gv
        rows = i * T + lax.broadcasted_iota(jnp.int32, (T, 1), 0)
        live = (rows >= n_meta) & (rows < n_meta + seq)
        diff = jnp.where(live, y - t_ref[...], 0.0)
        dyv = diff / D
        dxh = dyv * gv
        dh_ref[...] = r * (dxh - xh * jnp.mean(dxh * xh, axis=-1, keepdims=True))

        @pl.when(i == 0)
        def _():
            dg_ref[...] = jnp.zeros_like(dg_ref)
            loss_ref[...] = jnp.zeros_like(loss_ref)

        dg_ref[...] += jnp.sum(dyv * xh, axis=0, keepdims=True)
        per_row = jnp.mean(diff * diff, axis=-1, keepdims=True)
        loss_ref[...] += 0.5 * jnp.sum(per_row, axis=0, keepdims=True)

    row = pl.BlockSpec((T, D), lambda i: (i, 0))
    vec = pl.BlockSpec((1, D), lambda i: (0, 0))
    one = pl.BlockSpec((1, 1), lambda i: (0, 0))
    return pl.pallas_call(
        body, name=name, grid=(L // T,),
        in_specs=[row, vec, row], out_specs=[one, row, vec],
        out_shape=[jax.ShapeDtypeStruct((1, 1), F32), jax.ShapeDtypeStruct((L, D), F32),
                   jax.ShapeDtypeStruct((1, D), F32)],
        compiler_params=_params(("arbitrary",)),
    )(h, g, target)


def _ffn_tiles(M, F):
    return _pick(M, (704, 384, 128)), _pick(F, (1408, 512, 256, 128))


def _normed_rows(h_ref, g_ref, hn_ref, keep_ref):
    @pl.when(pl.program_id(1) == 0)
    def _():
        x = h_ref[...]
        r = lax.rsqrt(jnp.mean(x * x, axis=-1, keepdims=True) + EPS)
        keep_ref[...] = (x * r * g_ref[...]).astype(BF16)
        hn_ref[...] = keep_ref[...]


def _norm_in_proj(h, g, w_in, n_cols, name):
    M, D = h.shape
    tm = _pick(M, (1056, 384, 128))
    tn = _pick(n_cols, (512, 256, 128))

    def body(h_ref, g_ref, w_ref, hn_ref, o_ref, keep_ref):
        _normed_rows(h_ref, g_ref, hn_ref, keep_ref)
        o_ref[...] = jnp.dot(keep_ref[...], w_ref[...], preferred_element_type=F32).astype(o_ref.dtype)

    rows = pl.BlockSpec((tm, D), lambda i, j: (i, 0))
    return pl.pallas_call(
        body, name=name, grid=(M // tm, n_cols // tn),
        in_specs=[rows, pl.BlockSpec((1, D), lambda i, j: (0, 0)), pl.BlockSpec((None, D, tn), lambda i, j: (0, 0, j))],
        out_specs=[rows, pl.BlockSpec((tm, tn), lambda i, j: (i, j))],
        out_shape=[jax.ShapeDtypeStruct((M, D), BF16), jax.ShapeDtypeStruct((M, n_cols), BF16)],
        scratch_shapes=[pltpu.VMEM((tm, D), BF16)],
        compiler_params=_params(("parallel", "arbitrary")),
    )(h, g, w_in)


def _ffn_up(h, norm_g, w_gate_t, w_up_t, name, job=None):
    M, D = h.shape
    F = w_gate_t.shape[1]
    tm, tf = _ffn_tiles(M, F)
    nt = (((1,), (1,)), ((), ()))

    def body(h_ref, ng_ref, wg_ref, wu_ref, hn_ref, g_ref, u_ref, a_ref, keep_ref):
        _normed_rows(h_ref, ng_ref, hn_ref, keep_ref)
        hv = keep_ref[...]
        gv = lax.dot_general(hv, wg_ref[...], nt, preferred_element_type=F32)
        uv = lax.dot_general(hv, wu_ref[...], nt, preferred_element_type=F32)
        g_ref[...] = gv
        u_ref[...] = uv
        a_ref[...] = (gv * _sigmoid(gv) * uv).astype(a_ref.dtype)

    rows = pl.BlockSpec((tm, D), lambda i, j: (i, 0))
    w_spec = pl.BlockSpec((None, tf, D), lambda i, j: (0, j, 0))
    out = pl.BlockSpec((tm, tf), lambda i, j: (i, j))
    return _call_with_job(
        body, job, name=name, grid=(M // tm, F // tf),
        in_specs=[rows, pl.BlockSpec((1, D), lambda i, j: (0, 0)), w_spec, w_spec], out_specs=[rows, out, out, out],
        out_shape=[jax.ShapeDtypeStruct((M, D), BF16), jax.ShapeDtypeStruct((M, F), F32),
                   jax.ShapeDtypeStruct((M, F), F32), jax.ShapeDtypeStruct((M, F), BF16)],
        scratch_shapes=[pltpu.VMEM((tm, D), BF16)],
        args=(h, norm_g, w_gate_t, w_up_t))


def _ffn_down_bwd(dh, w_down, g, u, name):
    M, D = dh.shape
    F = g.shape[1]
    tm, tf = _ffn_tiles(M, F)

    def body(d_ref, w_ref, g_ref, u_ref, dg_ref, du_ref):
        dv = lax.dot_general(d_ref[...].astype(BF16), w_ref[...], (((1,), (1,)), ((), ())),
                             preferred_element_type=F32)
        gv = g_ref[...]
        s = _sigmoid(gv)
        du_ref[...] = (dv * (gv * s)).astype(du_ref.dtype)
        dg_ref[...] = (dv * u_ref[...] * (s * (1.0 + gv * (1.0 - s)))).astype(dg_ref.dtype)

    tile = pl.BlockSpec((tm, tf), lambda i, j: (i, j))
    return pl.pallas_call(
        body, name=name, grid=(M // tm, F // tf),
        in_specs=[pl.BlockSpec((tm, D), lambda i, j: (i, 0)), pl.BlockSpec((None, tf, D), lambda i, j: (0, j, 0)),
                  tile, tile],
        out_specs=[tile, tile],
        out_shape=[jax.ShapeDtypeStruct((M, F), BF16), jax.ShapeDtypeStruct((M, F), BF16)],
        compiler_params=_params(("parallel", "parallel")),
    )(dh, w_down, g, u)


def _mm_nn(pairs, out_dtype, name, residual=None, cols=None):
    M = pairs[0][0].shape[0]
    col0, N = cols if cols is not None else (0, pairs[0][1].shape[-1])
    tm = _pick(M, (1056, 384, 128))
    shallow = sum(p[0].shape[1] for p in pairs) <= 1024
    tn = _pick(math.gcd(N, col0) if col0 else N, ((1024,) if shallow else ()) + (640, 512, 256, 128))
    jb = col0 // tn
    n = len(pairs)

    def body(*refs):
        a_refs, w_refs = refs[:n], refs[n:2 * n]
        o_ref = refs[-1]
        acc = None
        for a_ref, w_ref in zip(a_refs, w_refs):
            d = jnp.dot(a_ref[...].astype(BF16), w_ref[...], preferred_element_type=F32)
            acc = d if acc is None else acc + d
        if residual is not None:
            acc = acc + refs[2 * n][...]
        o_ref[...] = acc.astype(o_ref.dtype)

    in_specs = [pl.BlockSpec((tm, a.shape[1]), lambda i, j: (i, 0)) for a, _, _, _ in pairs]
    for a, _, layer, kblk in pairs:
        in_specs.append(pl.BlockSpec((None, a.shape[1], tn), functools.partial(lambda i, j, l, kb: (l, kb, j + jb), l=layer, kb=kblk)))
    args = [p[0] for p in pairs] + [p[1] for p in pairs]
    if residual is not None:
        in_specs.append(pl.BlockSpec((tm, tn), lambda i, j: (i, j)))
        args.append(residual)
    return pl.pallas_call(
        body, name=name, grid=(M // tm, N // tn), in_specs=in_specs,
        out_specs=pl.BlockSpec((tm, tn), lambda i, j: (i, j)),
        out_shape=jax.ShapeDtypeStruct((M, N), out_dtype),
        compiler_params=_params(("parallel", "parallel")),
    )(*args)


def _mm_nt(pairs, out_dtype, name):
    M = pairs[0][0].shape[0]
    K = pairs[0][1].shape[1]
    tm = _pick(M, (1056, 384, 128))
    shallow = sum(p[0].shape[1] for p in pairs) <= 1024
    tk = _pick(K, ((1024,) if shallow else ()) + (512, 1408, 256, 128))
    n = len(pairs)

    def body(*refs):
        d_refs, w_refs = refs[:n], refs[n:2 * n]
        o_ref = refs[-1]
        acc = None
        for d_ref, w_ref in zip(d_refs, w_refs):
            d = lax.dot_general(d_ref[...].astype(BF16), w_ref[...], (((1,), (1,)), ((), ())),
                                preferred_element_type=F32)
            acc = d if acc is None else acc + d
        o_ref[...] = acc.astype(o_ref.dtype)

    in_specs = [pl.BlockSpec((tm, d.shape[1]), lambda i, j: (i, 0)) for d, _, _, _ in pairs]
    for d, _, layer, cblk in pairs:
        in_specs.append(pl.BlockSpec((None, tk, d.shape[1]), functools.partial(lambda i, j, l, cb: (l, j, cb), l=layer, cb=cblk)))
    args = [p[0] for p in pairs] + [p[1] for p in pairs]
    return pl.pallas_call(
        body, name=name, grid=(M // tm, K // tk), in_specs=in_specs,
        out_specs=pl.BlockSpec((tm, tk), lambda i, j: (i, j)),
        out_shape=jax.ShapeDtypeStruct((M, K), out_dtype),
        compiler_params=_params(("parallel", "parallel")),
    )(*args)


def _mm_tn(a, b, name, col_sharded, chips=N_CHIPS):
    M, K = a.shape
    N = b.shape[1]
    tm = _pick(M, (1056, 384, 128))
    tk = _pick(K, (1024, 1408, 512, 256, 128))
    tn = N // N_CHIPS if col_sharded else _pick(N, (512, 128))

    def body(a_ref, b_ref, o_ref):
        @pl.when(pl.program_id(2) == 0)
        def _():
            o_ref[...] = jnp.zeros_like(o_ref)

        o_ref[...] += lax.dot_general(a_ref[...].astype(BF16), b_ref[...].astype(BF16),
                                      (((0,), (0,)), ((), ())), preferred_element_type=F32)

    if col_sharded:
        out_shape = jax.ShapeDtypeStruct((N_CHIPS, K, tn), F32)
        out_spec = pl.BlockSpec((None, tk, tn), lambda k, j, m: (j, k, 0))
    else:
        out_shape = jax.ShapeDtypeStruct((K, N), F32)
        out_spec = pl.BlockSpec((tk, tn), lambda k, j, m: (k, j))
    out = pl.pallas_call(
        body, name=name, grid=(K // tk, N // tn, M // tm),
        in_specs=[pl.BlockSpec((tm, tk), lambda k, j, m: (m, k)), pl.BlockSpec((tm, tn), lambda k, j, m: (m, j))],
        out_specs=out_spec, out_shape=out_shape,
        compiler_params=_params(("parallel", "parallel", "arbitrary")),
    )(a, b)
    return out if col_sharded else out.reshape(chips, K // chips, N)


def _stack_heads(x, scale=None):
    lane = lax.broadcasted_iota(jnp.int32, x.shape, 1)
    zero = jnp.zeros_like(x)
    lo = jnp.where(lane < HEAD_DIM, x, zero)
    hi = jnp.where(lane < HEAD_DIM, zero, x)
    out = jnp.concatenate([lo, hi], axis=0)
    return out if scale is None else out * scale


def _unstack_heads(x2):
    qb = x2.shape[0] // 2
    lane = lax.broadcasted_iota(jnp.int32, (qb, LANES), 1)
    return jnp.where(lane < HEAD_DIM, x2[:qb], x2[qb:])


def _tri_and_ones(strict, keys=QUERY_BLOCK):
    r = lax.broadcasted_iota(jnp.int32, (keys, keys + LANES), 0)
    c = lax.broadcasted_iota(jnp.int32, (keys, keys + LANES), 1)
    tri = (r > c) if strict else (r >= c)
    return jnp.where(tri | (c >= keys), 1.0, 0.0).astype(BF16)


def _dot2(x, m):
    xh = x.astype(BF16)
    xl = (x - xh.astype(F32)).astype(BF16)
    return jnp.dot(xh, m, preferred_element_type=F32) + jnp.dot(xl, m, preferred_element_type=F32)


def _causal_valid(older=0):
    r = lax.broadcasted_iota(jnp.int32, (2 * QUERY_BLOCK, older + QUERY_BLOCK), 0) & (QUERY_BLOCK - 1)
    c = lax.broadcasted_iota(jnp.int32, (2 * QUERY_BLOCK, older + QUERY_BLOCK), 1)
    return c < r + older


def _pairs_per_step(n_pairs):
    return 2 if n_pairs % 2 == 0 else 1


def _lanes(g):
    return slice(g * LANES, (g + 1) * LANES)


def _sweep_older(i, step, carry_ref, first):
    def cond(state):
        n, live = state
        return jnp.logical_and(n < i, live)

    def older(state):
        n, _ = state
        step(i - 1 - n, False)
        return n + 1, jnp.max(carry_ref[...]) > LOG_STICK_FLOOR

    lax.while_loop(cond, older, (first, jnp.max(carry_ref[...]) > LOG_STICK_FLOOR))


class _CommJob:
    def __init__(self, inputs, out_shapes, aliases, n_sems, start, finish):
        self.inputs, self.out_shapes, self.aliases, self.n_sems = list(inputs), list(out_shapes), aliases, n_sems
        self.start, self.finish = start, finish


def _merge_jobs(jobs):
    jobs = [j for j in jobs if j is not None]
    if len(jobs) <= 1:
        return jobs[0] if jobs else None
    spans, aliases = [], {}
    i0 = o0 = s0 = 0
    for j in jobs:
        spans.append((i0, o0, s0))
        aliases.update({i0 + a: o0 + b for a, b in j.aliases.items()})
        i0, o0, s0 = i0 + len(j.inputs), o0 + len(j.out_shapes), s0 + j.n_sems

    def run(which):
        def go(ins, outs, send_sems, recv_sems):
            for j, (i, o, s) in zip(jobs, spans):
                getattr(j, which)(ins[i:i + len(j.inputs)], outs[o:o + len(j.out_shapes)],
                                  send_sems.at[pl.ds(s, j.n_sems)], recv_sems.at[pl.ds(s, j.n_sems)])
        return go

    return _CommJob([a for j in jobs for a in j.inputs], [s for j in jobs for s in j.out_shapes], aliases, s0,
                    run("start"), run("finish"))


def _call_with_job(core_body, job, *, name, grid, in_specs, out_specs, out_shape, scratch_shapes, args):
    sem = ("arbitrary",) * len(grid)
    if job is None:
        res = pl.pallas_call(core_body, name=name, grid=grid, in_specs=in_specs, out_specs=out_specs,
                             out_shape=out_shape, scratch_shapes=scratch_shapes, compiler_params=_params(sem))(*args)
        return list(res), []
    n_in, n_out, n_scr = len(in_specs), len(out_specs), len(scratch_shapes)
    m_in, m_out = len(job.inputs), len(job.out_shapes)

    def body(*refs):
        at = 0
        parts = []
        for count in (n_in, m_in, n_out, m_out, n_scr, 2):
            parts.append(refs[at:at + count])
            at += count
        ins, job_in, outs, job_outs, scratch, (send_sems, recv_sems) = parts
        first = functools.reduce(jnp.logical_and, [pl.program_id(a) == 0 for a in range(len(grid))])
        last = functools.reduce(jnp.logical_and, [pl.program_id(a) == grid[a] - 1 for a in range(len(grid))])

        @pl.when(first)
        def _():
            job.start(job_in, job_outs, send_sems, recv_sems)

        core_body(*ins, *outs, *scratch)

        @pl.when(last)
        def _():
            job.finish(job_in, job_outs, send_sems, recv_sems)

    res = pl.pallas_call(
        body, name=name, grid=grid, in_specs=list(in_specs) + [ANY] * m_in, out_specs=list(out_specs) + [ANY] * m_out,
        out_shape=list(out_shape) + job.out_shapes,
        input_output_aliases={n_in + a: n_out + b for a, b in job.aliases.items()},
        scratch_shapes=list(scratch_shapes) + [pltpu.SemaphoreType.DMA((job.n_sems,)), pltpu.SemaphoreType.DMA((job.n_sems,))],
        compiler_params=_params(sem),
    )(*args, *job.inputs)
    return list(res[:n_out]), list(res[n_out:])


def _run_job(job, name):
    m_in, m_out = len(job.inputs), len(job.out_shapes)

    def body(*refs):
        job_in, job_outs = refs[:m_in], refs[m_in:m_in + m_out]
        send_sems, recv_sems = refs[m_in + m_out:]
        job.start(job_in, job_outs, send_sems, recv_sems)
        job.finish(job_in, job_outs, send_sems, recv_sems)

    return list(pl.pallas_call(
        body, name=name, in_specs=[ANY] * m_in, out_specs=[ANY] * m_out, out_shape=job.out_shapes,
        input_output_aliases=dict(job.aliases),
        scratch_shapes=[pltpu.SemaphoreType.DMA((job.n_sems,)), pltpu.SemaphoreType.DMA((job.n_sems,))],
    )(*job.inputs))


def _attn_fwd(qkv, sb_width, name, job=None):
    L = qkv.shape[0]
    QB = QUERY_BLOCK
    nb = L // QB
    n_pairs = sb_width // LANES
    G = _pairs_per_step(n_pairs)
    W = G * LANES
    scale = 1.0 / math.sqrt(HEAD_DIM)

    def body(q_ref, k_ref, v_ref, o_ref, acc_ref, carry_ref):
        i = pl.program_id(1)
        q2 = [_stack_heads(q_ref[:, _lanes(g)], scale) for g in range(G)]
        acc_ref[...] = jnp.zeros_like(acc_ref)
        carry_ref[...] = jnp.zeros_like(carry_ref)

        def step(g, j, older, mask):
            n = older + QB
            start = pl.multiple_of(j * QB, QB)
            kb = k_ref[pl.ds(start, n), _lanes(g)]
            vb = v_ref[pl.ds(start, n), _lanes(g)]
            z = lax.dot_general(q2[g], kb, (((1,), (1,)), ((), ())), preferred_element_type=F32)
            sp = jnp.log(1.0 + jnp.exp(-jnp.abs(z)))
            a = jnp.minimum(z, 0.0) - sp
            b = jnp.minimum(-z, 0.0) - sp
            if mask is not None:
                b = jnp.where(mask, b, 0.0)
            res = _dot2(b, _tri_and_ones(True, n))
            excl = res[:, :n]
            if mask is None:
                excl = excl + carry_ref[g]
            w = jnp.exp(a + excl)
            if mask is not None:
                w = jnp.where(mask, w, 0.0)
            carry_ref[g] += res[:, n:]
            acc_ref[g] += _dot2(w, vb)

        @pl.when(i == 0)
        def _():
            for g in range(G):
                step(g, 0, 0, _causal_valid())

        @pl.when(i > 0)
        def _():
            for g in range(G):
                step(g, i - 1, QB, _causal_valid(QB))
            for g in range(G):
                _sweep_older(i, lambda j, _, g=g: step(g, j, 0, None), carry_ref.at[g], 1)

        for g in range(G):
            o_ref[:, _lanes(g)] = _unstack_heads(acc_ref[g])

    n_steps = n_pairs // G
    (out,), job_out = _call_with_job(
        body, job, name=name, grid=(n_steps, nb),
        in_specs=[pl.BlockSpec((QB, W), lambda p, i: (i, p)),
                  pl.BlockSpec((L, W), lambda p, i: (0, n_steps + p)),
                  pl.BlockSpec((L, W), lambda p, i: (0, 2 * n_steps + p))],
        out_specs=[pl.BlockSpec((QB, W), lambda p, i: (i, p))],
        out_shape=[jax.ShapeDtypeStruct((L, sb_width), F32)],
        scratch_shapes=[pltpu.VMEM((G, 2 * QB, LANES), F32), pltpu.VMEM((G, 2 * QB, LANES), F32)],
        args=(qkv, qkv, qkv))
    return out, job_out


def _attn_bwd(qkv, o, dmixed, sb_width, name, job=None):
    L = qkv.shape[0]
    QB = QUERY_BLOCK
    nb = L // QB
    n_pairs = sb_width // LANES
    G = _pairs_per_step(n_pairs)
    W = G * LANES
    scale = 1.0 / math.sqrt(HEAD_DIM)

    def body(q_ref, k_ref, v_ref, o_ref, do_ref, dq_ref, dk_ref, dv_ref,
             dq_acc, dk_acc, dv_acc, ce_ref, cr_ref):
        i = pl.program_id(1)

        @pl.when(i == 0)
        def _():
            dk_acc[...] = jnp.zeros_like(dk_acc)
            dv_acc[...] = jnp.zeros_like(dv_acc)

        q2s = [_stack_heads(q_ref[:, _lanes(g)], scale) for g in range(G)]
        do2s = [_stack_heads(do_ref[:, _lanes(g)].astype(BF16)) for g in range(G)]
        ones = jnp.ones((LANES, 2 * QB), BF16)
        dtot2s = []
        for g in range(G):
            ov = o_ref[:, _lanes(g)]
            dtot2s.append(_dot2(do2s[g].astype(F32) * jnp.concatenate([ov, ov], axis=0), ones))
        dq_acc[...] = jnp.zeros_like(dq_acc)
        ce_ref[...] = jnp.zeros_like(ce_ref)
        cr_ref[...] = jnp.zeros_like(cr_ref)

        def step(g, j, older, mask):
            n = older + QB
            q2, do2, dtot2 = q2s[g], do2s[g], dtot2s[g]
            start = pl.multiple_of(j * QB, QB)
            kb = k_ref[pl.ds(start, n), _lanes(g)]
            vb = v_ref[pl.ds(start, n), _lanes(g)]
            z = lax.dot_general(q2, kb, (((1,), (1,)), ((), ())), preferred_element_type=F32)
            e = jnp.exp(-jnp.abs(z))
            sp = jnp.log(1.0 + e)
            a = jnp.minimum(z, 0.0) - sp
            b = jnp.minimum(-z, 0.0) - sp
            rinv = 1.0 / (1.0 + e)
            pos = z >= 0.0
            beta = jnp.where(pos, rinv, e * rinv)
            one_m_beta = jnp.where(pos, e * rinv, rinv)
            if mask is not None:
                b = jnp.where(mask, b, 0.0)
            dw = lax.dot_general(do2, vb, (((1,), (1,)), ((), ())), preferred_element_type=F32)
            res = _dot2(b, _tri_and_ones(True, n))
            excl = res[:, :n]
            if mask is None:
                excl = excl + ce_ref[g]
            w = jnp.exp(a + excl)
            if mask is not None:
                w = jnp.where(mask, w, 0.0)
            ce_ref[g] += res[:, n:]
            gw = w * dw
            res2 = _dot2(gw, _tri_and_ones(False, n))
            rinc = res2[:, :n]
            if mask is None:
                rinc = rinc + cr_ref[g]
            cr_ref[g] += res2[:, n:]
            dz = gw * one_m_beta - beta * (dtot2[:, :n] - rinc)
            if mask is not None:
                dz = jnp.where(mask, dz, 0.0)
            dzb = dz.astype(BF16)
            dq_acc[g] += jnp.dot(dzb, kb, preferred_element_type=F32)
            dk_acc[pl.ds(start, n), _lanes(g)] += lax.dot_general(
                dzb, q2, (((0,), (0,)), ((), ())), preferred_element_type=F32)
            dv_acc[pl.ds(start, n), _lanes(g)] += lax.dot_general(
                w.astype(BF16), do2, (((0,), (0,)), ((), ())), preferred_element_type=F32)

        @pl.when(i == 0)
        def _():
            for g in range(G):
                step(g, 0, 0, _causal_valid())

        @pl.when(i > 0)
        def _():
            for g in range(G):
                step(g, i - 1, QB, _causal_valid(QB))
            for g in range(G):
                _sweep_older(i, lambda j, _, g=g: step(g, j, 0, None), ce_ref.at[g], 1)

        for g in range(G):
            dq_ref[:, _lanes(g)] = (_unstack_heads(dq_acc[g]) * scale).astype(dq_ref.dtype)

        @pl.when(i == nb - 1)
        def _():
            dk_ref[...] = dk_acc[...].astype(dk_ref.dtype)
            dv_ref[...] = dv_acc[...].astype(dv_ref.dtype)

    n_steps = n_pairs // G
    blk = pl.BlockSpec((QB, W), lambda p, i: (i, p))
    col = pl.BlockSpec((L, W), lambda p, i: (0, p))
    return _call_with_job(
        body, job, name=name, grid=(n_steps, nb),
        in_specs=[blk,
                  pl.BlockSpec((L, W), lambda p, i: (0, n_steps + p)),
                  pl.BlockSpec((L, W), lambda p, i: (0, 2 * n_steps + p)),
                  blk, blk],
        out_specs=[blk, col, col],
        out_shape=[jax.ShapeDtypeStruct((L, sb_width), BF16)] * 3,
        scratch_shapes=[pltpu.VMEM((G, 2 * QB, LANES), F32), pltpu.VMEM((L, W), F32),
                        pltpu.VMEM((L, W), F32), pltpu.VMEM((G, 2 * QB, LANES), F32),
                        pltpu.VMEM((G, 2 * QB, LANES), F32)],
        args=(qkv, qkv, qkv, o, dmixed))


def _conv_tile(L):
    return _pick(L, (384, 128))


def _glu(x, C):
    return x[:, :C] * _sigmoid(x[:, C:])


CONV_CHUNK = 32
SHIFT_TAIL = 24


def _fill_shifted(src_ref, dst_ref):
    n = dst_ref.shape[1]
    for r in range(1, 8):
        dst_ref[r - 1] = src_ref[r:r + n, :]


def _rows_at(src_ref, shifted_ref, start, n):
    q, r = divmod(start, 8)
    if r == 0:
        return src_ref[start:start + n, :]
    return shifted_ref[r - 1, 8 * q:8 * q + n, :]


def _conv_fwd(cacg, dw_w, dw_b, ln_g, ln_b, name):
    L, C2 = cacg.shape
    C = C2 // 2
    T = _conv_tile(L)
    H = CONV_HALO
    K = dw_w.shape[0]
    CH = CONV_CHUNK

    def body(x_ref, prev_ref, w_ref, b_ref, g_ref, beta_ref, o_ref, u_ref, us_ref):
        i = pl.program_id(0)
        u_ref[0:H, :] = jnp.where(i > 0, _glu(prev_ref[...], C), 0.0)
        u_ref[H:, :] = _glu(x_ref[...], C)
        _fill_shifted(u_ref, us_ref)
        for c0 in range(0, T, CH):
            y = jnp.broadcast_to(b_ref[...], (CH, C))
            for k in range(K):
                y = y + w_ref[k:k + 1, :] * _rows_at(u_ref, us_ref, c0 + H - (K - 1) + k, CH)
            mu = jnp.mean(y, axis=-1, keepdims=True)
            yc = y - mu
            rstd = lax.rsqrt(jnp.mean(yc * yc, axis=-1, keepdims=True) + EPS)
            ln = yc * rstd * g_ref[...] + beta_ref[...]
            o_ref[c0:c0 + CH, :] = (ln * _sigmoid(ln)).astype(o_ref.dtype)

    vec = pl.BlockSpec((1, C), lambda i: (0, 0))
    return pl.pallas_call(
        body, name=name, grid=(L // T,),
        in_specs=[pl.BlockSpec((T, C2), lambda i: (i, 0)),
                  pl.BlockSpec((H, C2), lambda i: (jnp.maximum(i * (T // H) - 1, 0), 0)),
                  pl.BlockSpec((K, C), lambda i: (0, 0)), vec, vec, vec],
        out_specs=pl.BlockSpec((T, C), lambda i: (i, 0)),
        out_shape=jax.ShapeDtypeStruct((L, C), BF16),
        scratch_shapes=[pltpu.VMEM((T + H, C), F32), pltpu.VMEM((7, T + SHIFT_TAIL, C), F32)],
        compiler_params=_params(("parallel",)),
    )(cacg, cacg, dw_w, dw_b, ln_g, ln_b)


def _conv_bwd(cacg, dmixed, dw_w, dw_b, ln_g, ln_b, name, job=None):
    L, C2 = cacg.shape
    C = C2 // 2
    T = _conv_tile(L)
    H = CONV_HALO
    K = dw_w.shape[0]
    nt = L // T
    TE = T + H

    CH = CONV_CHUNK

    def body(x_ref, prev_ref, next_ref, d_ref, dnext_ref, w_ref, b_ref, g_ref, beta_ref,
             dca_ref, dcg_ref, dwt_ref, db_ref, dg_ref, dbeta_ref, u_ref, us_ref, dy_ref, dys_ref):
        i = pl.program_id(0)
        last = i == nt - 1

        @pl.when(i == 0)
        def _():
            dwt_ref[...] = jnp.zeros_like(dwt_ref)
            db_ref[...] = jnp.zeros_like(db_ref)
            dg_ref[...] = jnp.zeros_like(dg_ref)
            dbeta_ref[...] = jnp.zeros_like(dbeta_ref)

        u_ref[0:H, :] = jnp.where(i > 0, _glu(prev_ref[...], C), 0.0)
        u_ref[H:H + T, :] = _glu(x_ref[...], C)
        u_ref[H + T:, :] = _glu(next_ref[...], C)
        _fill_shifted(u_ref, us_ref)
        dg_acc = jnp.zeros((1, C), F32)
        dbeta_acc = jnp.zeros((1, C), F32)
        db_acc = jnp.zeros((1, C), F32)
        for c0 in range(0, TE, CH):
            y = jnp.broadcast_to(b_ref[...], (CH, C))
            for k in range(K):
                y = y + w_ref[k:k + 1, :] * _rows_at(u_ref, us_ref, c0 + H - (K - 1) + k, CH)
            mu = jnp.mean(y, axis=-1, keepdims=True)
            yc = y - mu
            rstd = lax.rsqrt(jnp.mean(yc * yc, axis=-1, keepdims=True) + EPS)
            yh = yc * rstd
            ln = yh * g_ref[...] + beta_ref[...]
            s = _sigmoid(ln)
            dout = d_ref[c0:c0 + CH, :] if c0 < T else jnp.where(last, 0.0, dnext_ref[c0 - T:c0 - T + CH, :])
            dln = dout * (s * (1.0 + ln * (1.0 - s)))
            dyh = dln * g_ref[...]
            dy = rstd * (dyh - jnp.mean(dyh, axis=-1, keepdims=True)
                         - yh * jnp.mean(dyh * yh, axis=-1, keepdims=True))
            dy_ref[c0:c0 + CH, :] = dy
            if c0 < T:
                dg_acc = dg_acc + jnp.sum(dln * yh, axis=0, keepdims=True)
                dbeta_acc = dbeta_acc + jnp.sum(dln, axis=0, keepdims=True)
                db_acc = db_acc + jnp.sum(dy, axis=0, keepdims=True)
        dg_ref[...] += dg_acc
        dbeta_ref[...] += dbeta_acc
        db_ref[...] += db_acc
        _fill_shifted(dy_ref, dys_ref)
        for k in range(K):
            dwt_ref[k:k + 1, :] += jnp.sum(
                dy_ref[0:T, :] * _rows_at(u_ref, us_ref, H - (K - 1) + k, T), axis=0, keepdims=True)
        for c0 in range(0, T, CH):
            du = jnp.zeros((CH, C), F32)
            for k in range(K):
                du = du + w_ref[k:k + 1, :] * _rows_at(dy_ref, dys_ref, c0 + (K - 1) - k, CH)
            x = x_ref[c0:c0 + CH, :]
            sg = _sigmoid(x[:, C:])
            dca_ref[c0:c0 + CH, :] = (du * sg).astype(dca_ref.dtype)
            dcg_ref[c0:c0 + CH, :] = (du * x[:, :C] * sg * (1.0 - sg)).astype(dcg_ref.dtype)

    nh = L // H
    vec = pl.BlockSpec((1, C), lambda i: (0, 0))
    row = pl.BlockSpec((T, C), lambda i: (i, 0))
    return _call_with_job(
        body, job, name=name, grid=(nt,),
        in_specs=[pl.BlockSpec((T, C2), lambda i: (i, 0)),
                  pl.BlockSpec((H, C2), lambda i: (jnp.maximum(i * (T // H) - 1, 0), 0)),
                  pl.BlockSpec((H, C2), lambda i: (jnp.minimum((i + 1) * (T // H), nh - 1), 0)),
                  pl.BlockSpec((T, C), lambda i: (i, 1)),
                  pl.BlockSpec((H, C), lambda i: (jnp.minimum((i + 1) * (T // H), nh - 1), 1)),
                  pl.BlockSpec((K, C), lambda i: (0, 0)), vec, vec, vec],
        out_specs=[row, row, pl.BlockSpec((H, C), lambda i: (0, 0)), vec, vec, vec],
        out_shape=[jax.ShapeDtypeStruct((L, C), BF16), jax.ShapeDtypeStruct((L, C), BF16),
                   jax.ShapeDtypeStruct((H, C), F32), jax.ShapeDtypeStruct((1, C), F32),
                   jax.ShapeDtypeStruct((1, C), F32), jax.ShapeDtypeStruct((1, C), F32)],
        scratch_shapes=[pltpu.VMEM((T + 2 * H, C), F32), pltpu.VMEM((7, TE + SHIFT_TAIL, C), F32),
                        pltpu.VMEM((TE, C), F32), pltpu.VMEM((7, T + SHIFT_TAIL, C), F32)],
        args=(cacg, cacg, cacg, dmixed, dmixed, dw_w, dw_b, ln_g, ln_b))


def _local_step(h0, target, n_meta, seq, norms, conv_p, wts, final_g, gather_behind, reducer):
    mix_g, ffn_g = norms
    dw_w, dw_b, ln_g, ln_b = conv_p
    depth = mix_g.shape[0]
    C = dw_b.shape[-1]
    sbw = (wts["w_in"][0].shape[-1] - 2 * C) // 3
    assert sbw == C, "the mixer halves must have equal width"
    row = lambda a, i: a[i][None, :]

    h = h0
    saved = []
    for i in range(depth):
        hn, proj_qkv = _norm_in_proj(h, row(mix_g, i), wts["w_in"][i], 3 * sbw, f"in_qkv_{i}")
        cacg = _mm_nn([(hn, wts["w_in"][i], 0, 0)], F32, f"in_conv_{i}", cols=(3 * sbw, 2 * C))
        job, keys = gather_behind.get(("attn", i), (None, ()))
        attn, arrived = _attn_fwd(proj_qkv, sbw, f"attn_fwd_{i}", job)
        for (wname, wl), arr in zip(keys, arrived):
            wts[wname][wl] = arr
        conv = _conv_fwd(cacg, dw_w[i], row(dw_b, i), row(ln_g, i), row(ln_b, i), f"conv_fwd_{i}")
        h_mid = _mm_nn([(attn, wts["w_out"][i], 0, 0), (conv, wts["w_out"][i], 0, 1)], F32, f"out_proj_{i}",
                       residual=h)
        job, keys = gather_behind.get(("ffn", i), (None, ()))
        (hn2, g, u, act), arrived = _ffn_up(h_mid, row(ffn_g, i), wts["w_gate_t"][i], wts["w_up_t"][i],
                                            f"ffn_up_{i}", job)
        for (wname, wl), arr in zip(keys, arrived):
            wts[wname][wl] = arr
        h_out = _mm_nn([(act, wts["w_down"][i], 0, 0)], F32, f"down_{i}", residual=h_mid)
        saved.append((h, hn, proj_qkv, cacg, attn, conv, h_mid, hn2, g, u, act))
        h = h_out

    loss, dh, d_final_g = _loss_head(h, final_g[None, :], target, n_meta, seq, "loss_head")

    grads = {k: [None] * depth for k in ("mix_g", "ffn_g", "dw_w", "dw_b", "ln_g", "ln_b")}
    for i in reversed(range(depth)):
        h_in, hn, proj_qkv, cacg, attn, conv, h_mid, hn2, g, u, act = saved[i]
        big = {}
        dg, du = _ffn_down_bwd(dh, wts["w_down"][i], g, u, f"ffn_down_bwd_{i}")
        big["w_down"] = _mm_tn(act, dh, f"dw_down_{i}", col_sharded=False)
        dhn2 = _mm_nn([(dg, wts["w_gate_t"][i], 0, 0), (du, wts["w_up_t"][i], 0, 0)], F32, f"d_hn2_{i}")
        big["w_gate_t"] = _mm_tn(dg, hn2, f"dw_gate_{i}", col_sharded=False)
        big["w_up_t"] = _mm_tn(du, hn2, f"dw_up_{i}", col_sharded=False)
        dh, d_ffn = _rmsnorm_bwd(h_mid, row(ffn_g, i), dhn2, dh, f"ffn_norm_bwd_{i}")
        dmixed = _mm_nt([(dh, wts["w_out"][i], 0, 0)], F32, f"d_mixed_{i}")
        big["w_out"] = jnp.concatenate([_mm_tn(attn, dh, f"dw_out_attn_{i}", col_sharded=False, chips=2),
                                        _mm_tn(conv, dh, f"dw_out_conv_{i}", col_sharded=False, chips=2)], axis=0)
        sib_job, sib_sink = reducer.to_sibling(i, big)
        x_job, x_sink = reducer.take()
        (dq, dk, dv), arrived = _attn_bwd(proj_qkv, attn, dmixed, sbw, f"attn_bwd_{i}",
                                          _merge_jobs([sib_job, x_job]))
        sib_sink(arrived[:len(sib_job.out_shapes)])
        x_sink(arrived[len(sib_job.out_shapes):])
        x_job, x_sink = reducer.take()
        (dca, dcg, d_dw, d_b, d_lg, d_lb), arrived = _conv_bwd(
            cacg, dmixed, dw_w[i], row(dw_b, i), row(ln_g, i), row(ln_b, i), f"conv_bwd_{i}", x_job)
        x_sink(arrived)
        grads["dw_w"][i] = d_dw[:dw_w.shape[1]]
        grads["dw_b"][i], grads["ln_g"][i], grads["ln_b"][i] = d_b[0], d_lg[0], d_lb[0]
        dproj = jnp.concatenate([dq, dk, dv, dca, dcg], axis=1)
        dhn = _mm_nt([(dproj, wts["w_in"][i], 0, 0)], F32, f"d_hn_{i}")
        reducer.ready(i, {"w_in": _mm_tn(hn, dproj, f"dw_in_{i}", col_sharded=True)})
        dh, d_mix = _rmsnorm_bwd(h_in, row(mix_g, i), dhn, dh, f"mix_norm_bwd_{i}")
        grads["mix_g"][i], grads["ffn_g"][i] = d_mix[0], d_ffn[0]
    grads["final_g"] = d_final_g[0]
    return loss, dh, grads


ANY = pl.BlockSpec(memory_space=pl.ANY)


def _position():
    return lax.axis_index("x"), lax.axis_index("y"), lax.axis_index("c")


def _chip_at(x, y, k):
    return (1 - x if k & 2 else x), (1 - y if k & 1 else y)


def _half_rows(ref, half, rows, base=0):
    start = pl.multiple_of(base + half * rows, 8)
    lead = (slice(None),) * (len(ref.shape) - 2)
    return ref.at[(*lead, pl.ds(start, rows), slice(None))]


def _gather_job(fulls, shard_shapes, col_sharded):
    n = len(fulls)

    def tools(f_refs, send_sems, recv_sems):
        def block(wi, chip, half):
            _, R, C = shard_shapes[wi]
            if col_sharded[wi]:
                cols = pl.ds(pl.multiple_of(chip * C, LANES), C)
                return f_refs[wi].at[:, pl.ds(pl.multiple_of(half * (R // 2), 8), R // 2), cols]
            return _half_rows(f_refs[wi], half, R // 2, base=chip * R)

        def copy(wi, slot, blk, to):
            return pltpu.make_async_remote_copy(
                src_ref=blk, dst_ref=blk, send_sem=send_sems.at[6 * wi + slot],
                recv_sem=recv_sems.at[6 * wi + slot], device_id=to, device_id_type=MESH)

        return block, copy

    def start(_, f_refs, send_sems, recv_sems):
        block, copy = tools(f_refs, send_sems, recv_sems)
        x, y, c = _position()
        me = 2 * x + y
        for wi in range(n):
            for k in (1, 2, 3):
                copy(wi, k - 1, block(wi, me, c), (*_chip_at(x, y, k), c)).start()

    def finish(_, f_refs, send_sems, recv_sems):
        block, copy = tools(f_refs, send_sems, recv_sems)
        x, y, c = _position()
        me = 2 * x + y
        for wi in range(n):
            for k in (1, 2, 3):
                landed = block(wi, me ^ k, c)
                copy(wi, k - 1, landed, (x, y, c)).wait_recv()
                copy(wi, 2 + k, landed, (x, y, 1 - c)).start()
        for wi in range(n):
            for k in (1, 2, 3):
                copy(wi, 2 + k, block(wi, me ^ k, 1 - c), (x, y, c)).wait_recv()
        for wi in range(n):
            for k in (1, 2, 3):
                copy(wi, k - 1, block(wi, me, c), (x, y, c)).wait_send()
                copy(wi, 2 + k, block(wi, me ^ k, c), (x, y, c)).wait_send()

    return _CommJob(fulls, [jax.ShapeDtypeStruct(f.shape, f.dtype) for f in fulls], {i: i for i in range(n)},
                    6 * n, start, finish)


def _place_shard(w, layer, chip, col_sharded, dtype, name):
    _, R, C = w.shape
    tr = _pick(R, (256, 352, 128, 48))
    nr = R // tr

    def body(chip_ref, w_ref, o_ref):
        o_ref[...] = w_ref[...].astype(dtype)

    if col_sharded:
        shape = (1, R, N_CHIPS * C)
        out_spec = pl.BlockSpec((None, tr, C), lambda r, chip_ref: (0, r, chip_ref[0]))
    else:
        shape = (1, N_CHIPS * R, C)
        out_spec = pl.BlockSpec((None, tr, C), lambda r, chip_ref: (0, chip_ref[0] * nr + r, 0))
    grid_spec = pltpu.PrefetchScalarGridSpec(
        num_scalar_prefetch=1, grid=(nr,),
        in_specs=[pl.BlockSpec((None, tr, C), lambda r, chip_ref: (layer, r, 0))], out_specs=out_spec)
    return pl.pallas_call(
        body, name=name, grid_spec=grid_spec, out_shape=jax.ShapeDtypeStruct(shape, dtype),
        compiler_params=_params(("parallel",)),
    )(chip, w)


def _sibling_job(grads):
    n = len(grads)

    def copies(g_refs, l_refs, send_sems, recv_sems):
        x, y, c = _position()
        return [pltpu.make_async_remote_copy(
            src_ref=_half_rows(g_refs[wi], 1 - c, grads[wi].shape[1] // 2), dst_ref=l_refs[wi],
            send_sem=send_sems.at[wi], recv_sem=recv_sems.at[wi],
            device_id=(x, y, 1 - c), device_id_type=MESH) for wi in range(n)]

    def start(*refs):
        for cp in copies(*refs):
            cp.start()

    def finish(*refs):
        for cp in copies(*refs):
            cp.wait()

    outs = [jax.ShapeDtypeStruct((g.shape[0], g.shape[1] // 2, g.shape[2]), g.dtype) for g in grads]
    return _CommJob(grads, outs, {}, n, start, finish)


def _chip_sum(g, landed, core, name):
    _, R, C = g.shape
    hr = R // 2
    tr = _pick(hr, (256, 352, 128))
    nr = hr // tr

    def body(c_ref, g_ref, l_ref, o_ref):
        o_ref[...] = (g_ref[...] + l_ref[...]).astype(BF16)

    grid_spec = pltpu.PrefetchScalarGridSpec(
        num_scalar_prefetch=1, grid=(N_CHIPS, nr),
        in_specs=[pl.BlockSpec((None, tr, C), lambda j, r, c_ref: (j, c_ref[0] * nr + r, 0)),
                  pl.BlockSpec((None, tr, C), lambda j, r, c_ref: (j, r, 0))],
        out_specs=pl.BlockSpec((None, tr, C), lambda j, r, c_ref: (j, r, 0)))
    return pl.pallas_call(
        body, name=name, grid_spec=grid_spec, out_shape=jax.ShapeDtypeStruct((N_CHIPS, hr, C), BF16),
        compiler_params=_params(("parallel", "parallel")),
    )(core, g, landed)


def _across_job(parts):
    n = len(parts)

    def copy(p_refs, l_refs, send_sems, recv_sems, wi, k, to):
        x, y, _ = _position()
        me = 2 * x + y
        return pltpu.make_async_remote_copy(
            src_ref=p_refs[wi].at[me ^ k], dst_ref=l_refs[wi].at[me],
            send_sem=send_sems.at[3 * wi + k - 1], recv_sem=recv_sems.at[3 * wi + k - 1],
            device_id=to, device_id_type=MESH)

    def start(p_refs, l_refs, send_sems, recv_sems):
        x, y, c = _position()
        for wi in range(n):
            for k in (1, 2, 3):
                copy(p_refs, l_refs, send_sems, recv_sems, wi, k, (*_chip_at(x, y, k), c)).start()

    def finish(p_refs, l_refs, send_sems, recv_sems):
        x, y, c = _position()
        me = 2 * x + y
        for wi in range(n):
            for k in (1, 2, 3):
                slot = l_refs[wi].at[me ^ k]
                pltpu.make_async_remote_copy(
                    src_ref=slot, dst_ref=slot, send_sem=send_sems.at[3 * wi + k - 1],
                    recv_sem=recv_sems.at[3 * wi + k - 1], device_id=(x, y, c), device_id_type=MESH).wait_recv()
        for wi in range(n):
            for k in (1, 2, 3):
                copy(p_refs, l_refs, send_sems, recv_sems, wi, k, (x, y, c)).wait_send()

    return _CommJob(parts, [jax.ShapeDtypeStruct(p.shape, p.dtype) for p in parts], {}, 3 * n, start, finish)


class _Reducer:
    def __init__(self, core):
        self.core, self.parts, self.across, self.pending = core, {}, {}, []

    def to_sibling(self, layer, big):
        names = list(big)
        flat = [big[k] for k in names]

        def sink(landed):
            for k, g, la in zip(names, flat, landed):
                self.parts[k, layer] = _chip_sum(g, la, self.core, f"chip_sum_{k}_{layer}")
                self.pending.append((k, layer))

        return _sibling_job(flat), sink

    def ready(self, layer, big):
        job, sink = self.to_sibling(layer, big)
        sink(_run_job(job, f"grads_to_sibling_{next(iter(big))}_{layer}"))

    def take(self):
        keys, self.pending = self.pending, []
        if not keys:
            return None, lambda results: None

        def sink(results):
            self.across.update(zip(keys, results))

        return _across_job([self.parts[key] for key in keys]), sink


def _sum_chips(parts, landed, where, layer, depth, prev, name):
    _, hr, C = landed.shape
    tr = _pick(hr, (256, 352, 128))
    nr = hr // tr

    def body(*refs):
        own_ref, slots, o_ref = refs[1], refs[2:2 + N_CHIPS], refs[-1]
        chip = refs[0][0]
        total = None
        for q in range(N_CHIPS):
            term = jnp.where(chip == q, own_ref[...], slots[q][...]).astype(F32)
            total = term if total is None else total + term
        o_ref[...] = total

    def slot_spec(q):
        return pl.BlockSpec((None, tr, C), lambda r, w: (jnp.where(w[0] == q, (q + 1) % N_CHIPS, q), r, 0))

    in_specs = [pl.BlockSpec((None, tr, C), lambda r, w: (w[0], r, 0))] + [slot_spec(q) for q in range(N_CHIPS)]
    args = [where, parts] + [landed] * N_CHIPS
    aliases = {}
    if prev is not None:
        in_specs.append(ANY)
        args.append(prev)
        aliases = {len(args) - 1: 0}
    grid_spec = pltpu.PrefetchScalarGridSpec(
        num_scalar_prefetch=1, grid=(nr,), in_specs=in_specs,
        out_specs=pl.BlockSpec((None, tr, C), lambda r, w: (layer, w[1] * nr + r, 0)))
    return pl.pallas_call(
        body, name=name, grid_spec=grid_spec, out_shape=jax.ShapeDtypeStruct((depth, 2 * hr, C), F32),
        input_output_aliases=aliases, compiler_params=_params(("parallel",)),
    )(*args)


def _rs_join_halves(reduced):
    n = len(reduced)

    def body(*refs):
        o_refs = refs[n:2 * n]
        send_sems, recv_sems = refs[2 * n:]
        x, y, c = _position()
        sent = []
        for wi in range(n):
            hr = reduced[wi].shape[1] // 2
            mine = _half_rows(o_refs[wi], c, hr)
            cp = pltpu.make_async_remote_copy(
                src_ref=mine, dst_ref=mine, send_sem=send_sems.at[wi], recv_sem=recv_sems.at[wi],
                device_id=(x, y, 1 - c), device_id_type=MESH)
            cp.start()
            sent.append(cp)
        for wi in range(n):
            hr = reduced[wi].shape[1] // 2
            theirs = _half_rows(o_refs[wi], 1 - c, hr)
            pltpu.make_async_remote_copy(
                src_ref=theirs, dst_ref=theirs, send_sem=send_sems.at[wi], recv_sem=recv_sems.at[wi],
                device_id=(x, y, c), device_id_type=MESH).wait_recv()
        for cp in sent:
            cp.wait_send()

    return pl.pallas_call(
        body, name="grads_join_halves", out_shape=[jax.ShapeDtypeStruct(r.shape, r.dtype) for r in reduced],
        in_specs=[ANY] * n, out_specs=[ANY] * n, input_output_aliases={i: i for i in range(n)},
        scratch_shapes=[pltpu.SemaphoreType.DMA((n,)), pltpu.SemaphoreType.DMA((n,))],
    )(*reduced)


def _small_allreduce(vec):
    rows = vec.shape[0]

    def body(v_ref, o_ref, land, send_sems, recv_sems):
        x, y, c = _position()
        me = 4 * x + 2 * y + c
        land[0] = v_ref[...]
        sent = []
        for k in range(1, N_DEV):
            to = (1 - x if k & 4 else x, 1 - y if k & 2 else y, 1 - c if k & 1 else c)
            cp = pltpu.make_async_remote_copy(
                src_ref=v_ref, dst_ref=land.at[k], send_sem=send_sems.at[k - 1], recv_sem=recv_sems.at[k - 1],
                device_id=to, device_id_type=MESH)
            cp.start()
            sent.append(cp)
        for cp in sent:
            cp.wait_recv()
        acc = land[me]
        for e in range(1, N_DEV):
            acc = acc + land[me ^ e]
        o_ref[...] = acc
        for cp in sent:
            cp.wait_send()

    vmem = pl.BlockSpec(memory_space=pltpu.VMEM)
    return pl.pallas_call(
        body, name="small_allreduce", out_shape=jax.ShapeDtypeStruct(vec.shape, F32),
        in_specs=[vmem], out_specs=vmem,
        scratch_shapes=[pltpu.VMEM((N_DEV, rows, LANES), F32), pltpu.SemaphoreType.DMA((N_DEV - 1,)),
                        pltpu.SemaphoreType.DMA((N_DEV - 1,))],
    )(vec)


def _adam_math(w, g, m, v):
    m = ADAM_B1 * m + (1.0 - ADAM_B1) * g
    v = ADAM_B2 * v + (1.0 - ADAM_B2) * jnp.square(g)
    m_hat = m / (1.0 - ADAM_B1 ** ADAM_STEP)
    v_hat = v / (1.0 - ADAM_B2 ** ADAM_STEP)
    delta = -ADAM_LR * (m_hat / (jnp.sqrt(v_hat) + ADAM_EPS) + ADAM_WD * w)
    return delta, m, v


def _adam(w, g, m, v, name):
    def body(w_ref, g_ref, m_ref, v_ref, d_ref, nm_ref, nv_ref):
        d_ref[...], nm_ref[...], nv_ref[...] = _adam_math(w_ref[...], g_ref[...], m_ref[...], v_ref[...])

    if w.ndim == 3:
        lyr, R, C = w.shape
        tr = _pick(R, (256, 352, 128))
        blk = pl.BlockSpec((None, tr, C), lambda l, r: (l, r, 0))
        grid, sem = (lyr, R // tr), ("parallel", "parallel")
    else:
        blk = pl.BlockSpec(w.shape, lambda: (0, 0))
        grid, sem = (), None
    return pl.pallas_call(
        body, name=name, grid=grid, in_specs=[blk] * 4, out_specs=[blk] * 3,
        out_shape=[jax.ShapeDtypeStruct(w.shape, F32)] * 3, compiler_params=_params(sem),
    )(w, g, m, v)


def _rows(a, pad_to=8):
    r = a.reshape(-1, LANES)
    extra = (-r.shape[0]) % pad_to
    return jnp.pad(r, ((0, extra), (0, 0))) if extra else r


def _pack(arrays):
    return jnp.concatenate([_rows(a) for a in arrays], axis=0)


def _unpack(slab, shapes):
    out, at = [], 0
    for shp in shapes:
        nrow = math.prod(shp) // LANES
        out.append(slab[at:at + nrow].reshape(shp))
        at += nrow + (-nrow) % 8
    return out


BIG = ("w_in", "w_out", "w_gate_t", "w_up_t", "w_down")
BIG_COL_SHARDED = (True, False, False, False, False)
TRANSPOSED = {"w_gate_t": "w_gate", "w_up_t": "w_up"}


def kernel(x, meta_tokens, mix_norm_g, w_in, conv_dw_w, conv_dw_b, conv_ln_g, conv_ln_b, w_out, ffn_norm_g, w_gate, w_up, w_down, final_norm_g, loss_target, m_meta_tokens, m_mix_norm_g, m_w_in, m_conv_dw_w, m_conv_dw_b, m_conv_ln_g, m_conv_ln_b, m_w_out, m_ffn_norm_g, m_w_gate, m_w_up, m_w_down, m_final_norm_g, v_meta_tokens, v_mix_norm_g, v_w_in, v_conv_dw_w, v_conv_dw_b, v_conv_ln_g, v_conv_ln_b, v_w_out, v_ffn_norm_g, v_w_gate, v_w_up, v_w_down, v_final_norm_g):
    n_meta, seq = meta_tokens.shape[0], x.shape[1]
    D = x.shape[2]
    depth, taps, c_shard = conv_dw_w.shape
    C = conv_dw_b.shape[-1]
    chip = (2 * lax.axis_index("x") + lax.axis_index("y")).astype(jnp.int32)
    core = lax.axis_index("c").astype(jnp.int32).reshape(1)
    chip1 = chip.reshape(1)
    where = jnp.concatenate([chip1, core])
    big_w = dict(w_in=w_in, w_out=w_out, w_gate=w_gate, w_up=w_up, w_down=w_down)
    big_m = dict(w_in=m_w_in, w_out=m_w_out, w_gate=m_w_gate, w_up=m_w_up, w_down=m_w_down)
    big_v = dict(w_in=v_w_in, w_out=v_w_out, w_gate=v_w_gate, w_up=v_w_up, w_down=v_w_down)

    small_shard = _pack([conv_dw_w, meta_tokens])[None]
    to_send = {k: jnp.swapaxes(big_w[TRANSPOSED[k]], 1, 2) if k in TRANSPOSED else big_w[k] for k in BIG}
    col = dict(zip(BIG, BIG_COL_SHARDED))
    wts = {k: [_place_shard(to_send[k], l, chip1, col[k], BF16, f"place_{k}_{l}") for l in range(depth)] for k in BIG}
    small_placed = _place_shard(small_shard, 0, chip1, False, F32, "place_small")

    def gather_job(keys, extra=()):
        arrays = [wts[k][l] for k, l in keys] + list(extra)
        shapes = [(1,) + to_send[k].shape[1:] for k, _ in keys] + [(1,) + small_shard.shape[1:]] * len(extra)
        return _gather_job(arrays, shapes, [col[k] for k, _ in keys] + [False] * len(extra))

    first_keys = [("w_in", 0)]
    *first, small_full = _run_job(gather_job(first_keys, [small_placed]), "gather_first")
    for (k, l), arr in zip(first_keys, first):
        wts[k][l] = arr
    behind_keys = {}
    for l in range(depth):
        behind_keys["attn", l] = [("w_out", l), ("w_gate_t", l), ("w_up_t", l), ("w_down", l)]
        if l + 1 < depth:
            behind_keys["ffn", l] = [("w_in", l + 1)]
    gather_behind = {host: (gather_job(keys), keys) for host, keys in behind_keys.items()}

    rows_shard = small_shard.shape[1]
    dw_full, meta_full = [], []
    for j in range(N_CHIPS):
        dwj, mj = _unpack(small_full[0, j * rows_shard:(j + 1) * rows_shard],
                          [conv_dw_w.shape, meta_tokens.shape])
        dw_full.append(dwj)
        meta_full.append(mj)
    dw_w_full = jnp.concatenate(dw_full, axis=2)
    meta = jnp.concatenate(meta_full, axis=1)

    L = n_meta + seq
    Lp = -(-L // QUERY_BLOCK) * QUERY_BLOCK
    h0 = jnp.concatenate([meta, x[0], jnp.zeros((Lp - L, D), F32)], axis=0)
    target = jnp.pad(loss_target[0], ((n_meta, Lp - L), (0, 0)))
    reducer = _Reducer(core)
    loss, dh0, grads = _local_step(h0, target, n_meta, seq, (mix_norm_g, ffn_norm_g),
                                   (dw_w_full, conv_dw_b, conv_ln_g, conv_ln_b), wts, final_norm_g,
                                   gather_behind, reducer)
    loss = lax.psum(loss[0, 0], ("x", "y", "c"))
    grad_x = dh0[n_meta:L][None]
    job, sink = reducer.take()
    sink(_run_job(job, "grads_across_last"))

    reduced = []
    for k in BIG:
        arr = None
        for l in range(depth):
            arr = _sum_chips(reducer.parts[k, l], reducer.across[k, l], where, l, depth, arr, f"sum_chips_{k}_{l}")
        reduced.append(arr)
    big_g = dict(zip(BIG, _rs_join_halves(reduced)))

    small_names = ("mix_g", "ffn_g", "dw_b", "ln_g", "ln_b", "dw_w")
    small = [jnp.stack(grads[k]) for k in small_names] + [grads["final_g"], dh0[:n_meta]]
    small_shapes = [a.shape for a in small]
    g_mix, g_ffn, g_dwb, g_lng, g_lnb, g_dww, g_final, g_meta = _unpack(_small_allreduce(_pack(small)), small_shapes)
    g_dww = lax.dynamic_slice_in_dim(g_dww, chip * c_shard, c_shard, axis=2)
    g_meta = lax.dynamic_slice_in_dim(g_meta, chip * meta_tokens.shape[1], meta_tokens.shape[1], axis=1)

    out_g, out_d, out_m, out_v = {}, {}, {}, {}
    for kk in BIG:
        k = TRANSPOSED.get(kk, kk)
        view = (lambda a: jnp.swapaxes(a, 1, 2)) if kk in TRANSPOSED else (lambda a: a)
        res = _adam(view(big_w[k]), big_g[kk], view(big_m[k]), view(big_v[k]), f"adam_{k}")
        out_g[k] = view(big_g[kk])
        out_d[k], out_m[k], out_v[k] = (view(a) for a in res)
    small_order = ("meta_tokens", "mix_norm_g", "conv_dw_w", "conv_dw_b", "conv_ln_g", "conv_ln_b",
                   "ffn_norm_g", "final_norm_g")
    sw = dict(meta_tokens=meta_tokens, mix_norm_g=mix_norm_g, conv_dw_w=conv_dw_w, conv_dw_b=conv_dw_b,
              conv_ln_g=conv_ln_g, conv_ln_b=conv_ln_b, ffn_norm_g=ffn_norm_g, final_norm_g=final_norm_g)
    sm = dict(meta_tokens=m_meta_tokens, mix_norm_g=m_mix_norm_g, conv_dw_w=m_conv_dw_w, conv_dw_b=m_conv_dw_b,
              conv_ln_g=m_conv_ln_g, conv_ln_b=m_conv_ln_b, ffn_norm_g=m_ffn_norm_g, final_norm_g=m_final_norm_g)
    sv = dict(meta_tokens=v_meta_tokens, mix_norm_g=v_mix_norm_g, conv_dw_w=v_conv_dw_w, conv_dw_b=v_conv_dw_b,
              conv_ln_g=v_conv_ln_g, conv_ln_b=v_conv_ln_b, ffn_norm_g=v_ffn_norm_g, final_norm_g=v_final_norm_g)
    sg = dict(meta_tokens=g_meta, mix_norm_g=g_mix, conv_dw_w=g_dww, conv_dw_b=g_dwb, conv_ln_g=g_lng,
              conv_ln_b=g_lnb, ffn_norm_g=g_ffn, final_norm_g=g_final)
    def slab(d):
        return _pack([d[k] for k in small_order])
    shapes = [sw[k].shape for k in small_order]
    deltas = _adam(slab(sw), slab(sg), slab(sm), slab(sv), "adam_small")
    for res, dst in zip(deltas, (out_d, out_m, out_v)):
        dst.update(zip(small_order, _unpack(res, shapes)))
    out_g.update(sg)

    order = ("meta_tokens", "mix_norm_g", "w_in", "conv_dw_w", "conv_dw_b", "conv_ln_g", "conv_ln_b", "w_out",
             "ffn_norm_g", "w_gate", "w_up", "w_down", "final_norm_g")
    return (loss, grad_x, *[out_g[k] for k in order], *[out_d[k] for k in order],
            *[out_m[k] for k in order], *[out_v[k] for k in order])
```

```python
import functools
import math

import jax
import jax.numpy as jnp
from jax import lax
from jax.experimental import pallas as pl
from jax.experimental.pallas import tpu as pltpu

F32 = jnp.float32
BF16 = jnp.bfloat16
MESH = pl.DeviceIdType.MESH

EPS = 1e-6
QUERY_BLOCK = 128
LANES = 128
HEAD_DIM = 64
LOG_STICK_FLOOR = -40.0
CONV_HALO = 32
N_CHIPS = 4
N_DEV = 8
VMEM_LIMIT = 56 * 1024 * 1024

ADAM_LR = 0.001
ADAM_B1 = 0.9
ADAM_B2 = 0.999
ADAM_EPS = 1e-08
ADAM_WD = 0.01
ADAM_STEP = 10


def _pick(n, prefs):
    for p in prefs:
        if n % p == 0:
            return p
    return n


def _params(sem=None):
    return pltpu.CompilerParams(dimension_semantics=sem, vmem_limit_bytes=VMEM_LIMIT)


def _sigmoid(x):
    return 1.0 / (1.0 + jnp.exp(-x))


def _rmsnorm_bwd(h, g, dy, dh_in, name):
    L, D = h.shape
    T = _pick(L, (384, 128))

    def body(h_ref, g_ref, dy_ref, dhin_ref, dh_ref, dg_ref):
        x = h_ref[...]
        dyv = dy_ref[...]
        r = lax.rsqrt(jnp.mean(x * x, axis=-1, keepdims=True) + EPS)
        xh = x * r
        dxh = dyv * g_ref[...]
        dh_ref[...] = dhin_ref[...] + r * (dxh - xh * jnp.mean(dxh * xh, axis=-1, keepdims=True))

        @pl.when(pl.program_id(0) == 0)
        def _():
            dg_ref[...] = jnp.zeros_like(dg_ref)

        dg_ref[...] += jnp.sum(dyv * xh, axis=0, keepdims=True)

    row = pl.BlockSpec((T, D), lambda i: (i, 0))
    vec = pl.BlockSpec((1, D), lambda i: (0, 0))
    return pl.pallas_call(
        body, name=name, grid=(L // T,),
        in_specs=[row, vec, row, row], out_specs=[row, vec],
        out_shape=[jax.ShapeDtypeStruct((L, D), F32), jax.ShapeDtypeStruct((1, D), F32)],
        compiler_params=_params(("arbitrary",)),
    )(h, g, dy, dh_in)


def _loss_head(h, g, target, n_meta, seq, name):
    L, D = h.shape
    T = _pick(L, (384, 128))

    def body(h_ref, g_ref, t_ref, loss_ref, dh_ref, dg_ref):
        i = pl.program_id(0)
        x = h_ref[...]
        gv = g_ref[...]
        r = lax.rsqrt(jnp.mean(x * x, axis=-1, keepdims=True) + EPS)
        xh = x * r
        y = xh * gv
        rows = i * T + lax.broadcasted_iota(jnp.int32, (T, 1), 0)
        live = (rows >= n_meta) & (rows < n_meta + seq)
        diff = jnp.where(live, y - t_ref[...], 0.0)
        dyv = diff / D
        dxh = dyv * gv
        dh_ref[...] = r * (dxh - xh * jnp.mean(dxh * xh, axis=-1, keepdims=True))

        @pl.when(i == 0)
        def _():
            dg_ref[...] = jnp.zeros_like(dg_ref)
            loss_ref[...] = jnp.zeros_like(loss_ref)

        dg_ref[...] += jnp.sum(dyv * xh, axis=0, keepdims=True)
        per_row = jnp.mean(diff * diff, axis=-1, keepdims=True)
        loss_ref[...] += 0.5 * jnp.sum(per_row, axis=0, keepdims=True)

    row = pl.BlockSpec((T, D), lambda i: (i, 0))
    vec = pl.BlockSpec((1, D), lambda i: (0, 0))
    one = pl.BlockSpec((1, 1), lambda i: (0, 0))
    return pl.pallas_call(
        body, name=name, grid=(L // T,),
        in_specs=[row, vec, row], out_specs=[one, row, vec],
        out_shape=[jax.ShapeDtypeStruct((1, 1), F32), jax.ShapeDtypeStruct((L, D), F32),
                   jax.ShapeDtypeStruct((1, D), F32)],
        compiler_params=_params(("arbitrary",)),
    )(h, g, target)


def _ffn_tiles(M, F):
    return _pick(M, (352, 384, 128)), _pick(F, (1408, 512, 256, 128))


def _resident(shape):
    return pl.BlockSpec((None,) + tuple(shape[1:]), lambda *_: (0,) * len(shape), pipeline_mode=pl.Buffered(1))


def _normed_rows(h_ref, g_ref, hn_ref, keep_ref):
    @pl.when(pl.program_id(1) == 0)
    def _():
        x = h_ref[...]
        r = lax.rsqrt(jnp.mean(x * x, axis=-1, keepdims=True) + EPS)
        keep_ref[...] = (x * r * g_ref[...]).astype(BF16)
        hn_ref[...] = keep_ref[...]


def _norm_in_proj(h, g, w_in, n_cols, name):
    M, D = h.shape
    tm = _pick(M, (1056, 384, 128))
    tn = _pick(n_cols, (512, 256, 128))

    def body(h_ref, g_ref, w_ref, hn_ref, o_ref, keep_ref):
        _normed_rows(h_ref, g_ref, hn_ref, keep_ref)
        o_ref[...] = jnp.dot(keep_ref[...], w_ref[...], preferred_element_type=F32).astype(o_ref.dtype)

    rows = pl.BlockSpec((tm, D), lambda i, j: (i, 0))
    return pl.pallas_call(
        body, name=name, grid=(M // tm, n_cols // tn),
        in_specs=[rows, pl.BlockSpec((1, D), lambda i, j: (0, 0)), pl.BlockSpec((None, D, tn), lambda i, j: (0, 0, j))],
        out_specs=[rows, pl.BlockSpec((tm, tn), lambda i, j: (i, j))],
        out_shape=[jax.ShapeDtypeStruct((M, D), BF16), jax.ShapeDtypeStruct((M, n_cols), BF16)],
        scratch_shapes=[pltpu.VMEM((tm, D), BF16)],
        compiler_params=_params(("parallel", "arbitrary")),
    )(h, g, w_in)


def _ffn_up(h, norm_g, w_gate_t, w_up_t, name, job=None):
    M, D = h.shape
    F = w_gate_t.shape[1]
    tm, tf = _ffn_tiles(M, F)
    nt = (((1,), (1,)), ((), ()))

    def body(h_ref, ng_ref, wg_ref, wu_ref, hn_ref, g_ref, u_ref, a_ref):
        x = h_ref[...]
        r = lax.rsqrt(jnp.mean(x * x, axis=-1, keepdims=True) + EPS)
        hv = (x * r * ng_ref[...]).astype(BF16)
        hn_ref[...] = hv
        for c in range(0, F, tf):
            gv = lax.dot_general(hv, wg_ref[c:c + tf, :], nt, preferred_element_type=F32)
            uv = lax.dot_general(hv, wu_ref[c:c + tf, :], nt, preferred_element_type=F32)
            g_ref[:, c:c + tf] = gv
            u_ref[:, c:c + tf] = uv
            a_ref[:, c:c + tf] = (gv * _sigmoid(gv) * uv).astype(a_ref.dtype)

    rows = pl.BlockSpec((tm, D), lambda i: (i, 0))
    wide = pl.BlockSpec((tm, F), lambda i: (i, 0))
    return _call_with_job(
        body, job, name=name, grid=(M // tm,),
        in_specs=[rows, pl.BlockSpec((1, D), lambda i: (0, 0)), _resident(w_gate_t.shape), _resident(w_up_t.shape)],
        out_specs=[rows, wide, wide, wide],
        out_shape=[jax.ShapeDtypeStruct((M, D), BF16), jax.ShapeDtypeStruct((M, F), F32),
                   jax.ShapeDtypeStruct((M, F), F32), jax.ShapeDtypeStruct((M, F), BF16)],
        scratch_shapes=[], args=(h, norm_g, w_gate_t, w_up_t))


def _ffn_down_bwd(dh, w_down, g, u, name):
    M, D = dh.shape
    F = g.shape[1]
    tm, tf = _ffn_tiles(M, F)

    def body(d_ref, w_ref, g_ref, u_ref, dg_ref, du_ref):
        dhv = d_ref[...].astype(BF16)
        for c in range(0, F, tf):
            dv = lax.dot_general(dhv, w_ref[c:c + tf, :], (((1,), (1,)), ((), ())), preferred_element_type=F32)
            gv = g_ref[:, c:c + tf]
            s = _sigmoid(gv)
            du_ref[:, c:c + tf] = (dv * (gv * s)).astype(du_ref.dtype)
            dg_ref[:, c:c + tf] = (dv * u_ref[:, c:c + tf] * (s * (1.0 + gv * (1.0 - s)))).astype(dg_ref.dtype)

    wide = pl.BlockSpec((tm, F), lambda i: (i, 0))
    return pl.pallas_call(
        body, name=name, grid=(M // tm,),
        in_specs=[pl.BlockSpec((tm, D), lambda i: (i, 0)), _resident(w_down.shape), wide, wide],
        out_specs=[wide, wide],
        out_shape=[jax.ShapeDtypeStruct((M, F), BF16), jax.ShapeDtypeStruct((M, F), BF16)],
        compiler_params=_params(("parallel",)),
    )(dh, w_down, g, u)


def _mm_nn(pairs, out_dtype, name, residual=None, cols=None):
    M = pairs[0][0].shape[0]
    col0, N = cols if cols is not None else (0, pairs[0][1].shape[-1])
    tm = _pick(M, (1056, 384, 128))
    shallow = sum(p[0].shape[1] for p in pairs) <= 1024
    tn = _pick(math.gcd(N, col0) if col0 else N, ((1024,) if shallow else ()) + (640, 512, 256, 128))
    jb = col0 // tn
    n = len(pairs)

    def body(*refs):
        a_refs, w_refs = refs[:n], refs[n:2 * n]
        o_ref = refs[-1]
        acc = None
        for a_ref, w_ref in zip(a_refs, w_refs):
            d = jnp.dot(a_ref[...].astype(BF16), w_ref[...], preferred_element_type=F32)
            acc = d if acc is None else acc + d
        if residual is not None:
            acc = acc + refs[2 * n][...]
        o_ref[...] = acc.astype(o_ref.dtype)

    in_specs = [pl.BlockSpec((tm, a.shape[1]), lambda i, j: (i, 0)) for a, _, _, _ in pairs]
    for a, _, layer, kblk in pairs:
        in_specs.append(pl.BlockSpec((None, a.shape[1], tn), functools.partial(lambda i, j, l, kb: (l, kb, j + jb), l=layer, kb=kblk)))
    args = [p[0] for p in pairs] + [p[1] for p in pairs]
    if residual is not None:
        in_specs.append(pl.BlockSpec((tm, tn), lambda i, j: (i, j)))
        args.append(residual)
    return pl.pallas_call(
        body, name=name, grid=(M // tm, N // tn), in_specs=in_specs,
        out_specs=pl.BlockSpec((tm, tn), lambda i, j: (i, j)),
        out_shape=jax.ShapeDtypeStruct((M, N), out_dtype),
        compiler_params=_params(("parallel", "parallel")),
    )(*args)


def _mm_nt(pairs, out_dtype, name):
    M = pairs[0][0].shape[0]
    K = pairs[0][1].shape[1]
    tm = _pick(M, (1056, 384, 128))
    shallow = sum(p[0].shape[1] for p in pairs) <= 1024
    tk = _pick(K, ((1024,) if shallow else ()) + (512, 1408, 256, 128))
    n = len(pairs)

    def body(*refs):
        d_refs, w_refs = refs[:n], refs[n:2 * n]
        o_ref = refs[-1]
        acc = None
        for d_ref, w_ref in zip(d_refs, w_refs):
            d = lax.dot_general(d_ref[...].astype(BF16), w_ref[...], (((1,), (1,)), ((), ())),
                                preferred_element_type=F32)
            acc = d if acc is None else acc + d
        o_ref[...] = acc.astype(o_ref.dtype)

    in_specs = [pl.BlockSpec((tm, d.shape[1]), lambda i, j: (i, 0)) for d, _, _, _ in pairs]
    for d, _, layer, cblk in pairs:
        in_specs.append(pl.BlockSpec((None, tk, d.shape[1]), functools.partial(lambda i, j, l, cb: (l, j, cb), l=layer, cb=cblk)))
    args = [p[0] for p in pairs] + [p[1] for p in pairs]
    return pl.pallas_call(
        body, name=name, grid=(M // tm, K // tk), in_specs=in_specs,
        out_specs=pl.BlockSpec((tm, tk), lambda i, j: (i, j)),
        out_shape=jax.ShapeDtypeStruct((M, K), out_dtype),
        compiler_params=_params(("parallel", "parallel")),
    )(*args)


def _mm_tn(a, b, name, col_sharded, chips=N_CHIPS):
    M, K = a.shape
    N = b.shape[1]
    tm = _pick(M, (1056, 384, 128))
    tk = _pick(K, (1024, 1408, 512, 256, 128))
    tn = N // N_CHIPS if col_sharded else _pick(N, (512, 128))

    def body(a_ref, b_ref, o_ref):
        @pl.when(pl.program_id(2) == 0)
        def _():
            o_ref[...] = jnp.zeros_like(o_ref)

        o_ref[...] += lax.dot_general(a_ref[...].astype(BF16), b_ref[...].astype(BF16),
                                      (((0,), (0,)), ((), ())), preferred_element_type=F32)

    if col_sharded:
        out_shape = jax.ShapeDtypeStruct((N_CHIPS, K, tn), F32)
        out_spec = pl.BlockSpec((None, tk, tn), lambda k, j, m: (j, k, 0))
    else:
        out_shape = jax.ShapeDtypeStruct((K, N), F32)
        out_spec = pl.BlockSpec((tk, tn), lambda k, j, m: (k, j))
    out = pl.pallas_call(
        body, name=name, grid=(K // tk, N // tn, M // tm),
        in_specs=[pl.BlockSpec((tm, tk), lambda k, j, m: (m, k)), pl.BlockSpec((tm, tn), lambda k, j, m: (m, j))],
        out_specs=out_spec, out_shape=out_shape,
        compiler_params=_params(("parallel", "parallel", "arbitrary")),
    )(a, b)
    return out if col_sharded else out.reshape(chips, K // chips, N)


def _stack_heads(x, scale=None):
    lane = lax.broadcasted_iota(jnp.int32, x.shape, 1)
    zero = jnp.zeros_like(x)
    lo = jnp.where(lane < HEAD_DIM, x, zero)
    hi = jnp.where(lane < HEAD_DIM, zero, x)
    out = jnp.concatenate([lo, hi], axis=0)
    return out if scale is None else out * scale


def _unstack_heads(x2):
    qb = x2.shape[0] // 2
    lane = lax.broadcasted_iota(jnp.int32, (qb, LANES), 1)
    return jnp.where(lane < HEAD_DIM, x2[:qb], x2[qb:])


def _dot2(x, m):
    xh = x.astype(BF16)
    xl = (x - xh.astype(F32)).astype(BF16)
    return jnp.dot(xh, m, preferred_element_type=F32) + jnp.dot(xl, m, preferred_element_type=F32)


def _rev_cumsum(x, strict):
    rows, keys = x.shape
    r = lax.broadcasted_iota(jnp.int32, (keys, keys), 0)
    c = lax.broadcasted_iota(jnp.int32, (keys, keys), 1)
    tri = jnp.where((r > c) if strict else (r >= c), 1.0, 0.0).astype(BF16)
    sums = _dot2(x, tri)
    first = sums[:, :LANES] + x[:, :LANES] if strict else sums[:, :LANES]
    lane = lax.broadcasted_iota(jnp.int32, (rows, LANES), 1)
    total = jnp.sum(jnp.where(lane == 0, first, 0.0), axis=1, keepdims=True)
    return sums, jnp.broadcast_to(total, (rows, LANES))


def _causal_valid(older=0):
    r = lax.broadcasted_iota(jnp.int32, (2 * QUERY_BLOCK, older + QUERY_BLOCK), 0) & (QUERY_BLOCK - 1)
    c = lax.broadcasted_iota(jnp.int32, (2 * QUERY_BLOCK, older + QUERY_BLOCK), 1)
    return c < r + older


def _pairs_per_step(n_pairs):
    return 2 if n_pairs % 2 == 0 else 1


def _lanes(g):
    return slice(g * LANES, (g + 1) * LANES)


def _sweep_older(i, step, carry_ref, first):
    def cond(state):
        n, live = state
        return jnp.logical_and(n < i, live)

    def older(state):
        n, _ = state
        step(i - 1 - n, False)
        return n + 1, jnp.max(carry_ref[...]) > LOG_STICK_FLOOR

    lax.while_loop(cond, older, (first, jnp.max(carry_ref[...]) > LOG_STICK_FLOOR))


class _CommJob:
    def __init__(self, inputs, out_shapes, aliases, n_sems, start, finish):
        self.inputs, self.out_shapes, self.aliases, self.n_sems = list(inputs), list(out_shapes), aliases, n_sems
        self.start, self.finish = start, finish


def _merge_jobs(jobs):
    jobs = [j for j in jobs if j is not None]
    if len(jobs) <= 1:
        return jobs[0] if jobs else None
    spans, aliases = [], {}
    i0 = o0 = s0 = 0
    for j in jobs:
        spans.append((i0, o0, s0))
        aliases.update({i0 + a: o0 + b for a, b in j.aliases.items()})
        i0, o0, s0 = i0 + len(j.inputs), o0 + len(j.out_shapes), s0 + j.n_sems

    def run(which):
        def go(ins, outs, send_sems, recv_sems):
            for j, (i, o, s) in zip(jobs, spans):
                getattr(j, which)(ins[i:i + len(j.inputs)], outs[o:o + len(j.out_shapes)],
                                  send_sems.at[pl.ds(s, j.n_sems)], recv_sems.at[pl.ds(s, j.n_sems)])
        return go

    return _CommJob([a for j in jobs for a in j.inputs], [s for j in jobs for s in j.out_shapes], aliases, s0,
                    run("start"), run("finish"))


def _call_with_job(core_body, job, *, name, grid, in_specs, out_specs, out_shape, scratch_shapes, args):
    sem = ("arbitrary",) * len(grid)
    if job is None:
        res = pl.pallas_call(core_body, name=name, grid=grid, in_specs=in_specs, out_specs=out_specs,
                             out_shape=out_shape, scratch_shapes=scratch_shapes, compiler_params=_params(sem))(*args)
        return list(res), []
    n_in, n_out, n_scr = len(in_specs), len(out_specs), len(scratch_shapes)
    m_in, m_out = len(job.inputs), len(job.out_shapes)

    def body(*refs):
        at = 0
        parts = []
        for count in (n_in, m_in, n_out, m_out, n_scr, 2):
            parts.append(refs[at:at + count])
            at += count
        ins, job_in, outs, job_outs, scratch, (send_sems, recv_sems) = parts
        first = functools.reduce(jnp.logical_and, [pl.program_id(a) == 0 for a in range(len(grid))])
        last = functools.reduce(jnp.logical_and, [pl.program_id(a) == grid[a] - 1 for a in range(len(grid))])

        @pl.when(first)
        def _():
            job.start(job_in, job_outs, send_sems, recv_sems)

        core_body(*ins, *outs, *scratch)

        @pl.when(last)
        def _():
            job.finish(job_in, job_outs, send_sems, recv_sems)

    res = pl.pallas_call(
        body, name=name, grid=grid, in_specs=list(in_specs) + [ANY] * m_in, out_specs=list(out_specs) + [ANY] * m_out,
        out_shape=list(out_shape) + job.out_shapes,
        input_output_aliases={n_in + a: n_out + b for a, b in job.aliases.items()},
        scratch_shapes=list(scratch_shapes) + [pltpu.SemaphoreType.DMA((job.n_sems,)), pltpu.SemaphoreType.DMA((job.n_sems,))],
        compiler_params=_params(sem),
    )(*args, *job.inputs)
    return list(res[:n_out]), list(res[n_out:])


def _run_job(job, name):
    m_in, m_out = len(job.inputs), len(job.out_shapes)

    def body(*refs):
        job_in, job_outs = refs[:m_in], refs[m_in:m_in + m_out]
        send_sems, recv_sems = refs[m_in + m_out:]
        job.start(job_in, job_outs, send_sems, recv_sems)
        job.finish(job_in, job_outs, send_sems, recv_sems)

    return list(pl.pallas_call(
        body, name=name, in_specs=[ANY] * m_in, out_specs=[ANY] * m_out, out_shape=job.out_shapes,
        input_output_aliases=dict(job.aliases),
        scratch_shapes=[pltpu.SemaphoreType.DMA((job.n_sems,)), pltpu.SemaphoreType.DMA((job.n_sems,))],
    )(*job.inputs))


def _attn_fwd(qkv, sb_width, name, job=None):
    L = qkv.shape[0]
    QB = QUERY_BLOCK
    nb = L // QB
    n_pairs = sb_width // LANES
    G = _pairs_per_step(n_pairs)
    W = G * LANES
    scale = 1.0 / math.sqrt(HEAD_DIM)

    def body(q_ref, k_ref, v_ref, o_ref, acc_ref, carry_ref):
        i = pl.program_id(1)
        q2 = [_stack_heads(q_ref[:, _lanes(g)], scale) for g in range(G)]
        acc_ref[...] = jnp.zeros_like(acc_ref)
        carry_ref[...] = jnp.zeros_like(carry_ref)

        def step(g, j, older, mask):
            n = older + QB
            start = pl.multiple_of(j * QB, QB)
            kb = k_ref[pl.ds(start, n), _lanes(g)]
            vb = v_ref[pl.ds(start, n), _lanes(g)]
            z = lax.dot_general(q2[g], kb, (((1,), (1,)), ((), ())), preferred_element_type=F32)
            sp = jnp.log(1.0 + jnp.exp(-jnp.abs(z)))
            a = jnp.minimum(z, 0.0) - sp
            b = jnp.minimum(-z, 0.0) - sp
            if mask is not None:
                b = jnp.where(mask, b, 0.0)
            excl, total = _rev_cumsum(b, True)
            if mask is None:
                excl = excl + jnp.tile(carry_ref[g], (1, n // LANES))
            w = jnp.exp(a + excl)
            if mask is not None:
                w = jnp.where(mask, w, 0.0)
            carry_ref[g] += total
            acc_ref[g] += _dot2(w, vb)

        @pl.when(i == 0)
        def _():
            for g in range(G):
                step(g, 0, 0, _causal_valid())

        @pl.when(i > 0)
        def _():
            for g in range(G):
                step(g, i - 1, QB, _causal_valid(QB))
            for g in range(G):
                _sweep_older(i, lambda j, _, g=g: step(g, j, 0, None), carry_ref.at[g], 1)

        for g in range(G):
            o_ref[:, _lanes(g)] = _unstack_heads(acc_ref[g])

    n_steps = n_pairs // G
    (out,), job_out = _call_with_job(
        body, job, name=name, grid=(n_steps, nb),
        in_specs=[pl.BlockSpec((QB, W), lambda p, i: (i, p)),
                  pl.BlockSpec((L, W), lambda p, i: (0, n_steps + p)),
                  pl.BlockSpec((L, W), lambda p, i: (0, 2 * n_steps + p))],
        out_specs=[pl.BlockSpec((QB, W), lambda p, i: (i, p))],
        out_shape=[jax.ShapeDtypeStruct((L, sb_width), F32)],
        scratch_shapes=[pltpu.VMEM((G, 2 * QB, LANES), F32), pltpu.VMEM((G, 2 * QB, LANES), F32)],
        args=(qkv, qkv, qkv))
    return out, job_out


def _attn_bwd(qkv, o, dmixed, sb_width, name, job=None):
    L = qkv.shape[0]
    QB = QUERY_BLOCK
    nb = L // QB
    n_pairs = sb_width // LANES
    G = _pairs_per_step(n_pairs)
    W = G * LANES
    scale = 1.0 / math.sqrt(HEAD_DIM)

    def body(q_ref, k_ref, v_ref, o_ref, do_ref, dq_ref, dk_ref, dv_ref,
             dq_acc, dk_acc, dv_acc, ce_ref, cr_ref):
        i = pl.program_id(1)

        @pl.when(i == 0)
        def _():
            dk_acc[...] = jnp.zeros_like(dk_acc)
            dv_acc[...] = jnp.zeros_like(dv_acc)

        q2s = [_stack_heads(q_ref[:, _lanes(g)], scale) for g in range(G)]
        do2s = [_stack_heads(do_ref[:, _lanes(g)].astype(BF16)) for g in range(G)]
        ones = jnp.ones((LANES, 2 * QB), BF16)
        dtot2s = []
        for g in range(G):
            ov = o_ref[:, _lanes(g)]
            dtot2s.append(_dot2(do2s[g].astype(F32) * jnp.concatenate([ov, ov], axis=0), ones))
        dq_acc[...] = jnp.zeros_like(dq_acc)
        ce_ref[...] = jnp.zeros_like(ce_ref)
        cr_ref[...] = jnp.zeros_like(cr_ref)

        def step(g, j, older, mask):
            n = older + QB
            q2, do2, dtot2 = q2s[g], do2s[g], dtot2s[g]
            start = pl.multiple_of(j * QB, QB)
            kb = k_ref[pl.ds(start, n), _lanes(g)]
            vb = v_ref[pl.ds(start, n), _lanes(g)]
            z = lax.dot_general(q2, kb, (((1,), (1,)), ((), ())), preferred_element_type=F32)
            e = jnp.exp(-jnp.abs(z))
            sp = jnp.log(1.0 + e)
            a = jnp.minimum(z, 0.0) - sp
            b = jnp.minimum(-z, 0.0) - sp
            rinv = 1.0 / (1.0 + e)
            pos = z >= 0.0
            beta = jnp.where(pos, rinv, e * rinv)
            one_m_beta = jnp.where(pos, e * rinv, rinv)
            if mask is not None:
                b = jnp.where(mask, b, 0.0)
            dw = lax.dot_general(do2, vb, (((1,), (1,)), ((), ())), preferred_element_type=F32)
            excl, total = _rev_cumsum(b, True)
            if mask is None:
                excl = excl + jnp.tile(ce_ref[g], (1, n // LANES))
            w = jnp.exp(a + excl)
            if mask is not None:
                w = jnp.where(mask, w, 0.0)
            ce_ref[g] += total
            gw = w * dw
            rinc, total = _rev_cumsum(gw, False)
            if mask is None:
                rinc = rinc + jnp.tile(cr_ref[g], (1, n // LANES))
            cr_ref[g] += total
            dz = gw * one_m_beta - beta * (dtot2[:, :n] - rinc)
            if mask is not None:
                dz = jnp.where(mask, dz, 0.0)
            dzb = dz.astype(BF16)
            dq_acc[g] += jnp.dot(dzb, kb, preferred_element_type=F32)
            dk_acc[pl.ds(start, n), _lanes(g)] += lax.dot_general(
                dzb, q2, (((0,), (0,)), ((), ())), preferred_element_type=F32)
            dv_acc[pl.ds(start, n), _lanes(g)] += lax.dot_general(
                w.astype(BF16), do2, (((0,), (0,)), ((), ())), preferred_element_type=F32)

        @pl.when(i == 0)
        def _():
            for g in range(G):
                step(g, 0, 0, _causal_valid())

        @pl.when(i > 0)
        def _():
            for g in range(G):
                step(g, i - 1, QB, _causal_valid(QB))
            for g in range(G):
                _sweep_older(i, lambda j, _, g=g: step(g, j, 0, None), ce_ref.at[g], 1)

        for g in range(G):
            dq_ref[:, _lanes(g)] = (_unstack_heads(dq_acc[g]) * scale).astype(dq_ref.dtype)

        @pl.when(i == nb - 1)
        def _():
            dk_ref[...] = dk_acc[...].astype(dk_ref.dtype)
            dv_ref[...] = dv_acc[...].astype(dv_ref.dtype)

    n_steps = n_pairs // G
    blk = pl.BlockSpec((QB, W), lambda p, i: (i, p))
    col = pl.BlockSpec((L, W), lambda p, i: (0, p))
    return _call_with_job(
        body, job, name=name, grid=(n_steps, nb),
        in_specs=[blk,
                  pl.BlockSpec((L, W), lambda p, i: (0, n_steps + p)),
                  pl.BlockSpec((L, W), lambda p, i: (0, 2 * n_steps + p)),
                  blk, blk],
        out_specs=[blk, col, col],
        out_shape=[jax.ShapeDtypeStruct((L, sb_width), BF16)] * 3,
        scratch_shapes=[pltpu.VMEM((G, 2 * QB, LANES), F32), pltpu.VMEM((L, W), F32),
                        pltpu.VMEM((L, W), F32), pltpu.VMEM((G, 2 * QB, LANES), F32),
                        pltpu.VMEM((G, 2 * QB, LANES), F32)],
        args=(qkv, qkv, qkv, o, dmixed))


def _conv_tile(L):
    return _pick(L, (384, 128))


def _glu(x, C):
    return x[:, :C] * _sigmoid(x[:, C:])


CONV_CHUNK = 32
SHIFT_TAIL = 24


def _fill_shifted(src_ref, dst_ref):
    n = dst_ref.shape[1]
    for r in range(1, 8):
        dst_ref[r - 1] = src_ref[r:r + n, :]


def _rows_at(src_ref, shifted_ref, start, n):
    q, r = divmod(start, 8)
    if r == 0:
        return src_ref[start:start + n, :]
    return shifted_ref[r - 1, 8 * q:8 * q + n, :]


def _conv_fwd(cacg, dw_w, dw_b, ln_g, ln_b, name):
    L, C2 = cacg.shape
    C = C2 // 2
    T = _conv_tile(L)
    H = CONV_HALO
    K = dw_w.shape[0]
    CH = CONV_CHUNK

    def body(x_ref, prev_ref, w_ref, b_ref, g_ref, beta_ref, o_ref, u_ref, us_ref):
        i = pl.program_id(0)
        u_ref[0:H, :] = jnp.where(i > 0, _glu(prev_ref[...], C), 0.0)
        u_ref[H:, :] = _glu(x_ref[...], C)
        _fill_shifted(u_ref, us_ref)
        for c0 in range(0, T, CH):
            y = jnp.broadcast_to(b_ref[...], (CH, C))
            for k in range(K):
                y = y + w_ref[k:k + 1, :] * _rows_at(u_ref, us_ref, c0 + H - (K - 1) + k, CH)
            mu = jnp.mean(y, axis=-1, keepdims=True)
            yc = y - mu
            rstd = lax.rsqrt(jnp.mean(yc * yc, axis=-1, keepdims=True) + EPS)
            ln = yc * rstd * g_ref[...] + beta_ref[...]
            o_ref[c0:c0 + CH, :] = (ln * _sigmoid(ln)).astype(o_ref.dtype)

    vec = pl.BlockSpec((1, C), lambda i: (0, 0))
    return pl.pallas_call(
        body, name=name, grid=(L // T,),
        in_specs=[pl.BlockSpec((T, C2), lambda i: (i, 0)),
                  pl.BlockSpec((H, C2), lambda i: (jnp.maximum(i * (T // H) - 1, 0), 0)),
                  pl.BlockSpec((K, C), lambda i: (0, 0)), vec, vec, vec],
        out_specs=pl.BlockSpec((T, C), lambda i: (i, 0)),
        out_shape=jax.ShapeDtypeStruct((L, C), BF16),
        scratch_shapes=[pltpu.VMEM((T + H, C), F32), pltpu.VMEM((7, T + SHIFT_TAIL, C), F32)],
        compiler_params=_params(("parallel",)),
    )(cacg, cacg, dw_w, dw_b, ln_g, ln_b)


def _conv_bwd(cacg, dmixed, dw_w, dw_b, ln_g, ln_b, name, job=None):
    L, C2 = cacg.shape
    C = C2 // 2
    T = _conv_tile(L)
    H = CONV_HALO
    K = dw_w.shape[0]
    nt = L // T
    TE = T + H

    CH = CONV_CHUNK

    def body(x_ref, prev_ref, next_ref, d_ref, dnext_ref, w_ref, b_ref, g_ref, beta_ref,
             dca_ref, dcg_ref, dwt_ref, db_ref, dg_ref, dbeta_ref, u_ref, us_ref, dy_ref, dys_ref):
        i = pl.program_id(0)
        last = i == nt - 1

        @pl.when(i == 0)
        def _():
            dwt_ref[...] = jnp.zeros_like(dwt_ref)
            db_ref[...] = jnp.zeros_like(db_ref)
            dg_ref[...] = jnp.zeros_like(dg_ref)
            dbeta_ref[...] = jnp.zeros_like(dbeta_ref)

        u_ref[0:H, :] = jnp.where(i > 0, _glu(prev_ref[...], C), 0.0)
        u_ref[H:H + T, :] = _glu(x_ref[...], C)
        u_ref[H + T:, :] = _glu(next_ref[...], C)
        _fill_shifted(u_ref, us_ref)
        dg_acc = jnp.zeros((1, C), F32)
        dbeta_acc = jnp.zeros((1, C), F32)
        db_acc = jnp.zeros((1, C), F32)
        for c0 in range(0, TE, CH):
            y = jnp.broadcast_to(b_ref[...], (CH, C))
            for k in range(K):
                y = y + w_ref[k:k + 1, :] * _rows_at(u_ref, us_ref, c0 + H - (K - 1) + k, CH)
            mu = jnp.mean(y, axis=-1, keepdims=True)
            yc = y - mu
            rstd = lax.rsqrt(jnp.mean(yc * yc, axis=-1, keepdims=True) + EPS)
            yh = yc * rstd
            ln = yh * g_ref[...] + beta_ref[...]
            s = _sigmoid(ln)
            dout = d_ref[c0:c0 + CH, :] if c0 < T else jnp.where(last, 0.0, dnext_ref[c0 - T:c0 - T + CH, :])
            dln = dout * (s * (1.0 + ln * (1.0 - s)))
            dyh = dln * g_ref[...]
            dy = rstd * (dyh - jnp.mean(dyh, axis=-1, keepdims=True)
                         - yh * jnp.mean(dyh * yh, axis=-1, keepdims=True))
            dy_ref[c0:c0 + CH, :] = dy
            if c0 < T:
                dg_acc = dg_acc + jnp.sum(dln * yh, axis=0, keepdims=True)
                dbeta_acc = dbeta_acc + jnp.sum(dln, axis=0, keepdims=True)
                db_acc = db_acc + jnp.sum(dy, axis=0, keepdims=True)
        dg_ref[...] += dg_acc
        dbeta_ref[...] += dbeta_acc
        db_ref[...] += db_acc
        _fill_shifted(dy_ref, dys_ref)
        for k in range(K):
            dwt_ref[k:k + 1, :] += jnp.sum(
                dy_ref[0:T, :] * _rows_at(u_ref, us_ref, H - (K - 1) + k, T), axis=0, keepdims=True)
        for c0 in range(0, T, CH):
            du = jnp.zeros((CH, C), F32)
            for k in range(K):
                du = du + w_ref[k:k + 1, :] * _rows_at(dy_ref, dys_ref, c0 + (K - 1) - k, CH)
            x = x_ref[c0:c0 + CH, :]
            sg = _sigmoid(x[:, C:])
            dca_ref[c0:c0 + CH, :] = (du * sg).astype(dca_ref.dtype)
            dcg_ref[c0:c0 + CH, :] = (du * x[:, :C] * sg * (1.0 - sg)).astype(dcg_ref.dtype)

    nh = L // H
    vec = pl.BlockSpec((1, C), lambda i: (0, 0))
    row = pl.BlockSpec((T, C), lambda i: (i, 0))
    return _call_with_job(
        body, job, name=name, grid=(nt,),
        in_specs=[pl.BlockSpec((T, C2), lambda i: (i, 0)),
                  pl.BlockSpec((H, C2), lambda i: (jnp.maximum(i * (T // H) - 1, 0), 0)),
                  pl.BlockSpec((H, C2), lambda i: (jnp.minimum((i + 1) * (T // H), nh - 1), 0)),
                  pl.BlockSpec((T, C), lambda i: (i, 1)),
                  pl.BlockSpec((H, C), lambda i: (jnp.minimum((i + 1) * (T // H), nh - 1), 1)),
                  pl.BlockSpec((K, C), lambda i: (0, 0)), vec, vec, vec],
        out_specs=[row, row, pl.BlockSpec((H, C), lambda i: (0, 0)), vec, vec, vec],
        out_shape=[jax.ShapeDtypeStruct((L, C), BF16), jax.ShapeDtypeStruct((L, C), BF16),
                   jax.ShapeDtypeStruct((H, C), F32), jax.ShapeDtypeStruct((1, C), F32),
                   jax.ShapeDtypeStruct((1, C), F32), jax.ShapeDtypeStruct((1, C), F32)],
        scratch_shapes=[pltpu.VMEM((T + 2 * H, C), F32), pltpu.VMEM((7, TE + SHIFT_TAIL, C), F32),
                        pltpu.VMEM((TE, C), F32), pltpu.VMEM((7, T + SHIFT_TAIL, C), F32)],
        args=(cacg, cacg, cacg, dmixed, dmixed, dw_w, dw_b, ln_g, ln_b))


def _local_step(h0, target, n_meta, seq, norms, conv_p, wts, final_g, gather_behind, reducer):
    mix_g, ffn_g = norms
    dw_w, dw_b, ln_g, ln_b = conv_p
    depth = mix_g.shape[0]
    C = dw_b.shape[-1]
    sbw = (wts["w_in"][0].shape[-1] - 2 * C) // 3
    assert sbw == C, "the mixer halves must have equal width"
    row = lambda a, i: a[i][None, :]

    h = h0
    saved = []
    for i in range(depth):
        hn, proj_qkv = _norm_in_proj(h, row(mix_g, i), wts["w_in"][i], 3 * sbw, f"in_qkv_{i}")
        cacg = _mm_nn([(hn, wts["w_in"][i], 0, 0)], F32, f"in_conv_{i}", cols=(3 * sbw, 2 * C))
        job, keys = gather_behind.get(("attn", i), (None, ()))
        attn, arrived = _attn_fwd(proj_qkv, sbw, f"attn_fwd_{i}", job)
        for (wname, wl), arr in zip(keys, arrived):
            wts[wname][wl] = arr
        conv = _conv_fwd(cacg, dw_w[i], row(dw_b, i), row(ln_g, i), row(ln_b, i), f"conv_fwd_{i}")
        h_mid = _mm_nn([(attn, wts["w_out"][i], 0, 0), (conv, wts["w_out"][i], 0, 1)], F32, f"out_proj_{i}",
                       residual=h)
        job, keys = gather_behind.get(("ffn", i), (None, ()))
        (hn2, g, u, act), arrived = _ffn_up(h_mid, row(ffn_g, i), wts["w_gate_t"][i], wts["w_up_t"][i],
                                            f"ffn_up_{i}", job)
        for (wname, wl), arr in zip(keys, arrived):
            wts[wname][wl] = arr
        h_out = _mm_nn([(act, wts["w_down"][i], 0, 0)], F32, f"down_{i}", residual=h_mid)
        saved.append((h, hn, proj_qkv, cacg, attn, conv, h_mid, hn2, g, u, act))
        h = h_out

    loss, dh, d_final_g = _loss_head(h, final_g[None, :], target, n_meta, seq, "loss_head")

    grads = {k: [None] * depth for k in ("mix_g", "ffn_g", "dw_w", "dw_b", "ln_g", "ln_b")}
    for i in reversed(range(depth)):
        h_in, hn, proj_qkv, cacg, attn, conv, h_mid, hn2, g, u, act = saved[i]
        big = {}
        dg, du = _ffn_down_bwd(dh, wts["w_down"][i], g, u, f"ffn_down_bwd_{i}")
        big["w_down"] = _mm_tn(act, dh, f"dw_down_{i}", col_sharded=False)
        dhn2 = _mm_nn([(dg, wts["w_gate_t"][i], 0, 0), (du, wts["w_up_t"][i], 0, 0)], F32, f"d_hn2_{i}")
        big["w_gate_t"] = _mm_tn(dg, hn2, f"dw_gate_{i}", col_sharded=False)
        big["w_up_t"] = _mm_tn(du, hn2, f"dw_up_{i}", col_sharded=False)
        dh, d_ffn = _rmsnorm_bwd(h_mid, row(ffn_g, i), dhn2, dh, f"ffn_norm_bwd_{i}")
        dmixed = _mm_nt([(dh, wts["w_out"][i], 0, 0)], F32, f"d_mixed_{i}")
        big["w_out"] = jnp.concatenate([_mm_tn(attn, dh, f"dw_out_attn_{i}", col_sharded=False, chips=2),
                                        _mm_tn(conv, dh, f"dw_out_conv_{i}", col_sharded=False, chips=2)], axis=0)
        sib_job, sib_sink = reducer.to_sibling(i, big)
        x_job, x_sink = reducer.take()
        (dq, dk, dv), arrived = _attn_bwd(proj_qkv, attn, dmixed, sbw, f"attn_bwd_{i}",
                                          _merge_jobs([sib_job, x_job]))
        sib_sink(arrived[:len(sib_job.out_shapes)])
        x_sink(arrived[len(sib_job.out_shapes):])
        x_job, x_sink = reducer.take()
        (dca, dcg, d_dw, d_b, d_lg, d_lb), arrived = _conv_bwd(
            cacg, dmixed, dw_w[i], row(dw_b, i), row(ln_g, i), row(ln_b, i), f"conv_bwd_{i}", x_job)
        x_sink(arrived)
        grads["dw_w"][i] = d_dw[:dw_w.shape[1]]
        grads["dw_b"][i], grads["ln_g"][i], grads["ln_b"][i] = d_b[0], d_lg[0], d_lb[0]
        dproj = jnp.concatenate([dq, dk, dv, dca, dcg], axis=1)
        dhn = _mm_nt([(dproj, wts["w_in"][i], 0, 0)], F32, f"d_hn_{i}")
        reducer.ready(i, {"w_in": _mm_tn(hn, dproj, f"dw_in_{i}", col_sharded=True)})
        dh, d_mix = _rmsnorm_bwd(h_in, row(mix_g, i), dhn, dh, f"mix_norm_bwd_{i}")
        grads["mix_g"][i], grads["ffn_g"][i] = d_mix[0], d_ffn[0]
    grads["final_g"] = d_final_g[0]
    return loss, dh, grads


ANY = pl.BlockSpec(memory_space=pl.ANY)


def _position():
    return lax.axis_index("x"), lax.axis_index("y"), lax.axis_index("c")


def _chip_at(x, y, k):
    return (1 - x if k & 2 else x), (1 - y if k & 1 else y)


def _half_rows(ref, half, rows, base=0):
    start = pl.multiple_of(base + half * rows, 8)
    lead = (slice(None),) * (len(ref.shape) - 2)
    return ref.at[(*lead, pl.ds(start, rows), slice(None))]


def _gather_job(fulls, shard_shapes, col_sharded):
    n = len(fulls)

    def tools(f_refs, send_sems, recv_sems):
        def block(wi, chip, half):
            _, R, C = shard_shapes[wi]
            if col_sharded[wi]:
                cols = pl.ds(pl.multiple_of(chip * C, LANES), C)
                return f_refs[wi].at[:, pl.ds(pl.multiple_of(half * (R // 2), 8), R // 2), cols]
            return _half_rows(f_refs[wi], half, R // 2, base=chip * R)

        def copy(wi, slot, blk, to):
            return pltpu.make_async_remote_copy(
                src_ref=blk, dst_ref=blk, send_sem=send_sems.at[6 * wi + slot],
                recv_sem=recv_sems.at[6 * wi + slot], device_id=to, device_id_type=MESH)

        return block, copy

    def start(_, f_refs, send_sems, recv_sems):
        block, copy = tools(f_refs, send_sems, recv_sems)
        x, y, c = _position()
        me = 2 * x + y
        for wi in range(n):
            for k in (1, 2, 3):
                copy(wi, k - 1, block(wi, me, c), (*_chip_at(x, y, k), c)).start()

    def finish(_, f_refs, send_sems, recv_sems):
        block, copy = tools(f_refs, send_sems, recv_sems)
        x, y, c = _position()
        me = 2 * x + y
        for wi in range(n):
            for k in (1, 2, 3):
                landed = block(wi, me ^ k, c)
                copy(wi, k - 1, landed, (x, y, c)).wait_recv()
                copy(wi, 2 + k, landed, (x, y, 1 - c)).start()
        for wi in range(n):
            for k in (1, 2, 3):
                copy(wi, 2 + k, block(wi, me ^ k, 1 - c), (x, y, c)).wait_recv()
        for wi in range(n):
            for k in (1, 2, 3):
                copy(wi, k - 1, block(wi, me, c), (x, y, c)).wait_send()
                copy(wi, 2 + k, block(wi, me ^ k, c), (x, y, c)).wait_send()

    return _CommJob(fulls, [jax.ShapeDtypeStruct(f.shape, f.dtype) for f in fulls], {i: i for i in range(n)},
                    6 * n, start, finish)


def _place_shard(w, layer, chip, col_sharded, dtype, name):
    _, R, C = w.shape
    tr = _pick(R, (256, 352, 128, 48))
    nr = R // tr

    def body(chip_ref, w_ref, o_ref):
        o_ref[...] = w_ref[...].astype(dtype)

    if col_sharded:
        shape = (1, R, N_CHIPS * C)
        out_spec = pl.BlockSpec((None, tr, C), lambda r, chip_ref: (0, r, chip_ref[0]))
    else:
        shape = (1, N_CHIPS * R, C)
        out_spec = pl.BlockSpec((None, tr, C), lambda r, chip_ref: (0, chip_ref[0] * nr + r, 0))
    grid_spec = pltpu.PrefetchScalarGridSpec(
        num_scalar_prefetch=1, grid=(nr,),
        in_specs=[pl.BlockSpec((None, tr, C), lambda r, chip_ref: (layer, r, 0))], out_specs=out_spec)
    return pl.pallas_call(
        body, name=name, grid_spec=grid_spec, out_shape=jax.ShapeDtypeStruct(shape, dtype),
        compiler_params=_params(("parallel",)),
    )(chip, w)


def _sibling_job(grads):
    n = len(grads)

    def copies(g_refs, l_refs, send_sems, recv_sems):
        x, y, c = _position()
        return [pltpu.make_async_remote_copy(
            src_ref=_half_rows(g_refs[wi], 1 - c, grads[wi].shape[1] // 2), dst_ref=l_refs[wi],
            send_sem=send_sems.at[wi], recv_sem=recv_sems.at[wi],
            device_id=(x, y, 1 - c), device_id_type=MESH) for wi in range(n)]

    def start(*refs):
        for cp in copies(*refs):
            cp.start()

    def finish(*refs):
        for cp in copies(*refs):
            cp.wait()

    outs = [jax.ShapeDtypeStruct((g.shape[0], g.shape[1] // 2, g.shape[2]), g.dtype) for g in grads]
    return _CommJob(grads, outs, {}, n, start, finish)


def _chip_sum(g, landed, core, name):
    _, R, C = g.shape
    hr = R // 2
    tr = _pick(hr, (256, 352, 128))
    nr = hr // tr

    def body(c_ref, g_ref, l_ref, o_ref):
        o_ref[...] = (g_ref[...] + l_ref[...]).astype(BF16)

    grid_spec = pltpu.PrefetchScalarGridSpec(
        num_scalar_prefetch=1, grid=(N_CHIPS, nr),
        in_specs=[pl.BlockSpec((None, tr, C), lambda j, r, c_ref: (j, c_ref[0] * nr + r, 0)),
                  pl.BlockSpec((None, tr, C), lambda j, r, c_ref: (j, r, 0))],
        out_specs=pl.BlockSpec((None, tr, C), lambda j, r, c_ref: (j, r, 0)))
    return pl.pallas_call(
        body, name=name, grid_spec=grid_spec, out_shape=jax.ShapeDtypeStruct((N_CHIPS, hr, C), BF16),
        compiler_params=_params(("parallel", "parallel")),
    )(core, g, landed)


def _across_job(parts):
    n = len(parts)

    def copy(p_refs, l_refs, send_sems, recv_sems, wi, k, to):
        x, y, _ = _position()
        me = 2 * x + y
        return pltpu.make_async_remote_copy(
            src_ref=p_refs[wi].at[me ^ k], dst_ref=l_refs[wi].at[me],
            send_sem=send_sems.at[3 * wi + k - 1], recv_sem=recv_sems.at[3 * wi + k - 1],
            device_id=to, device_id_type=MESH)

    def start(p_refs, l_refs, send_sems, recv_sems):
        x, y, c = _position()
        for wi in range(n):
            for k in (1, 2, 3):
                copy(p_refs, l_refs, send_sems, recv_sems, wi, k, (*_chip_at(x, y, k), c)).start()

    def finish(p_refs, l_refs, send_sems, recv_sems):
        x, y, c = _position()
        me = 2 * x + y
        for wi in range(n):
            for k in (1, 2, 3):
                slot = l_refs[wi].at[me ^ k]
                pltpu.make_async_remote_copy(
                    src_ref=slot, dst_ref=slot, send_sem=send_sems.at[3 * wi + k - 1],
                    recv_sem=recv_sems.at[3 * wi + k - 1], device_id=(x, y, c), device_id_type=MESH).wait_recv()
        for wi in range(n):
            for k in (1, 2, 3):
                copy(p_refs, l_refs, send_sems, recv_sems, wi, k, (x, y, c)).wait_send()

    return _CommJob(parts, [jax.ShapeDtypeStruct(p.shape, p.dtype) for p in parts], {}, 3 * n, start, finish)


class _Reducer:
    def __init__(self, core):
        self.core, self.parts, self.across, self.pending = core, {}, {}, []

    def to_sibling(self, layer, big):
        names = list(big)
        flat = [big[k] for k in names]

        def sink(landed):
            for k, g, la in zip(names, flat, landed):
                self.parts[k, layer] = _chip_sum(g, la, self.core, f"chip_sum_{k}_{layer}")
                self.pending.append((k, layer))

        return _sibling_job(flat), sink

    def ready(self, layer, big):
        job, sink = self.to_sibling(layer, big)
        sink(_run_job(job, f"grads_to_sibling_{next(iter(big))}_{layer}"))

    def take(self):
        keys, self.pending = self.pending, []
        if not keys:
            return None, lambda results: None

        def sink(results):
            self.across.update(zip(keys, results))

        return _across_job([self.parts[key] for key in keys]), sink


def _sum_chips(parts, landed, where, layer, depth, prev, name):
    _, hr, C = landed.shape
    tr = _pick(hr, (256, 352, 128))
    nr = hr // tr

    def body(*refs):
        own_ref, slots, o_ref = refs[1], refs[2:2 + N_CHIPS], refs[-1]
        chip = refs[0][0]
        total = None
        for q in range(N_CHIPS):
            term = jnp.where(chip == q, own_ref[...], slots[q][...]).astype(F32)
            total = term if total is None else total + term
        o_ref[...] = total

    def slot_spec(q):
        return pl.BlockSpec((None, tr, C), lambda r, w: (jnp.where(w[0] == q, (q + 1) % N_CHIPS, q), r, 0))

    in_specs = [pl.BlockSpec((None, tr, C), lambda r, w: (w[0], r, 0))] + [slot_spec(q) for q in range(N_CHIPS)]
    args = [where, parts] + [landed] * N_CHIPS
    aliases = {}
    if prev is not None:
        in_specs.append(ANY)
        args.append(prev)
        aliases = {len(args) - 1: 0}
    grid_spec = pltpu.PrefetchScalarGridSpec(
        num_scalar_prefetch=1, grid=(nr,), in_specs=in_specs,
        out_specs=pl.BlockSpec((None, tr, C), lambda r, w: (layer, w[1] * nr + r, 0)))
    return pl.pallas_call(
        body, name=name, grid_spec=grid_spec, out_shape=jax.ShapeDtypeStruct((depth, 2 * hr, C), F32),
        input_output_aliases=aliases, compiler_params=_params(("parallel",)),
    )(*args)


def _rs_join_halves(reduced):
    n = len(reduced)

    def body(*refs):
        o_refs = refs[n:2 * n]
        send_sems, recv_sems = refs[2 * n:]
        x, y, c = _position()
        sent = []
        for wi in range(n):
            hr = reduced[wi].shape[1] // 2
            mine = _half_rows(o_refs[wi], c, hr)
            cp = pltpu.make_async_remote_copy(
                src_ref=mine, dst_ref=mine, send_sem=send_sems.at[wi], recv_sem=recv_sems.at[wi],
                device_id=(x, y, 1 - c), device_id_type=MESH)
            cp.start()
            sent.append(cp)
        for wi in range(n):
            hr = reduced[wi].shape[1] // 2
            theirs = _half_rows(o_refs[wi], 1 - c, hr)
            pltpu.make_async_remote_copy(
                src_ref=theirs, dst_ref=theirs, send_sem=send_sems.at[wi], recv_sem=recv_sems.at[wi],
                device_id=(x, y, c), device_id_type=MESH).wait_recv()
        for cp in sent:
            cp.wait_send()

    return pl.pallas_call(
        body, name="grads_join_halves", out_shape=[jax.ShapeDtypeStruct(r.shape, r.dtype) for r in reduced],
        in_specs=[ANY] * n, out_specs=[ANY] * n, input_output_aliases={i: i for i in range(n)},
        scratch_shapes=[pltpu.SemaphoreType.DMA((n,)), pltpu.SemaphoreType.DMA((n,))],
    )(*reduced)


def _small_allreduce(vec):
    rows = vec.shape[0]

    def body(v_ref, o_ref, land, send_sems, recv_sems):
        x, y, c = _position()
        me = 4 * x + 2 * y + c
        land[0] = v_ref[...]
        sent = []
        for k in range(1, N_DEV):
            to = (1 - x if k & 4 else x, 1 - y if k & 2 else y, 1 - c if k & 1 else c)
            cp = pltpu.make_async_remote_copy(
                src_ref=v_ref, dst_ref=land.at[k], send_sem=send_sems.at[k - 1], recv_sem=recv_sems.at[k - 1],
                device_id=to, device_id_type=MESH)
            cp.start()
            sent.append(cp)
        for cp in sent:
            cp.wait_recv()
        acc = land[me]
        for e in range(1, N_DEV):
            acc = acc + land[me ^ e]
        o_ref[...] = acc
        for cp in sent:
            cp.wait_send()

    vmem = pl.BlockSpec(memory_space=pltpu.VMEM)
    return pl.pallas_call(
        body, name="small_allreduce", out_shape=jax.ShapeDtypeStruct(vec.shape, F32),
        in_specs=[vmem], out_specs=vmem,
        scratch_shapes=[pltpu.VMEM((N_DEV, rows, LANES), F32), pltpu.SemaphoreType.DMA((N_DEV - 1,)),
                        pltpu.SemaphoreType.DMA((N_DEV - 1,))],
    )(vec)


def _adam_math(w, g, m, v):
    m = ADAM_B1 * m + (1.0 - ADAM_B1) * g
    v = ADAM_B2 * v + (1.0 - ADAM_B2) * jnp.square(g)
    m_hat = m / (1.0 - ADAM_B1 ** ADAM_STEP)
    v_hat = v / (1.0 - ADAM_B2 ** ADAM_STEP)
    delta = -ADAM_LR * (m_hat / (jnp.sqrt(v_hat) + ADAM_EPS) + ADAM_WD * w)
    return delta, m, v


def _adam(w, g, m, v, name):
    def body(w_ref, g_ref, m_ref, v_ref, d_ref, nm_ref, nv_ref):
        d_ref[...], nm_ref[...], nv_ref[...] = _adam_math(w_ref[...], g_ref[...], m_ref[...], v_ref[...])

    if w.ndim == 3:
        lyr, R, C = w.shape
        tr = _pick(R, (256, 352, 128))
        blk = pl.BlockSpec((None, tr, C), lambda l, r: (l, r, 0))
        grid, sem = (lyr, R // tr), ("parallel", "parallel")
    else:
        blk = pl.BlockSpec(w.shape, lambda: (0, 0))
        grid, sem = (), None
    return pl.pallas_call(
        body, name=name, grid=grid, in_specs=[blk] * 4, out_specs=[blk] * 3,
        out_shape=[jax.ShapeDtypeStruct(w.shape, F32)] * 3, compiler_params=_params(sem),
    )(w, g, m, v)


def _rows(a, pad_to=8):
    r = a.reshape(-1, LANES)
    extra = (-r.shape[0]) % pad_to
    return jnp.pad(r, ((0, extra), (0, 0))) if extra else r


def _pack(arrays):
    return jnp.concatenate([_rows(a) for a in arrays], axis=0)


def _unpack(slab, shapes):
    out, at = [], 0
    for shp in shapes:
        nrow = math.prod(shp) // LANES
        out.append(slab[at:at + nrow].reshape(shp))
        at += nrow + (-nrow) % 8
    return out


BIG = ("w_in", "w_out", "w_gate_t", "w_up_t", "w_down")
BIG_COL_SHARDED = (True, False, False, False, False)
TRANSPOSED = {"w_gate_t": "w_gate", "w_up_t": "w_up"}


def kernel(x, meta_tokens, mix_norm_g, w_in, conv_dw_w, conv_dw_b, conv_ln_g, conv_ln_b, w_out, ffn_norm_g, w_gate, w_up, w_down, final_norm_g, loss_target, m_meta_tokens, m_mix_norm_g, m_w_in, m_conv_dw_w, m_conv_dw_b, m_conv_ln_g, m_conv_ln_b, m_w_out, m_ffn_norm_g, m_w_gate, m_w_up, m_w_down, m_final_norm_g, v_meta_tokens, v_mix_norm_g, v_w_in, v_conv_dw_w, v_conv_dw_b, v_conv_ln_g, v_conv_ln_b, v_w_out, v_ffn_norm_g, v_w_gate, v_w_up, v_w_down, v_final_norm_g):
    n_meta, seq = meta_tokens.shape[0], x.shape[1]
    D = x.shape[2]
    depth, taps, c_shard = conv_dw_w.shape
    C = conv_dw_b.shape[-1]
    chip = (2 * lax.axis_index("x") + lax.axis_index("y")).astype(jnp.int32)
    core = lax.axis_index("c").astype(jnp.int32).reshape(1)
    chip1 = chip.reshape(1)
    where = jnp.concatenate([chip1, core])
    big_w = dict(w_in=w_in, w_out=w_out, w_gate=w_gate, w_up=w_up, w_down=w_down)
    big_m = dict(w_in=m_w_in, w_out=m_w_out, w_gate=m_w_gate, w_up=m_w_up, w_down=m_w_down)
    big_v = dict(w_in=v_w_in, w_out=v_w_out, w_gate=v_w_gate, w_up=v_w_up, w_down=v_w_down)

    small_shard = _pack([conv_dw_w, meta_tokens])[None]
    to_send = {k: jnp.swapaxes(big_w[TRANSPOSED[k]], 1, 2) if k in TRANSPOSED else big_w[k] for k in BIG}
    col = dict(zip(BIG, BIG_COL_SHARDED))
    wts = {k: [_place_shard(to_send[k], l, chip1, col[k], BF16, f"place_{k}_{l}") for l in range(depth)] for k in BIG}
    small_placed = _place_shard(small_shard, 0, chip1, False, F32, "place_small")

    def gather_job(keys, extra=()):
        arrays = [wts[k][l] for k, l in keys] + list(extra)
        shapes = [(1,) + to_send[k].shape[1:] for k, _ in keys] + [(1,) + small_shard.shape[1:]] * len(extra)
        return _gather_job(arrays, shapes, [col[k] for k, _ in keys] + [False] * len(extra))

    first_keys = [("w_in", 0)]
    *first, small_full = _run_job(gather_job(first_keys, [small_placed]), "gather_first")
    for (k, l), arr in zip(first_keys, first):
        wts[k][l] = arr
    behind_keys = {}
    for l in range(depth):
        behind_keys["attn", l] = [("w_out", l), ("w_gate_t", l), ("w_up_t", l), ("w_down", l)]
        if l + 1 < depth:
            behind_keys["ffn", l] = [("w_in", l + 1)]
    gather_behind = {host: (gather_job(keys), keys) for host, keys in behind_keys.items()}

    rows_shard = small_shard.shape[1]
    dw_full, meta_full = [], []
    for j in range(N_CHIPS):
        dwj, mj = _unpack(small_full[0, j * rows_shard:(j + 1) * rows_shard],
                          [conv_dw_w.shape, meta_tokens.shape])
        dw_full.append(dwj)
        meta_full.append(mj)
    dw_w_full = jnp.concatenate(dw_full, axis=2)
    meta = jnp.concatenate(meta_full, axis=1)

    L = n_meta + seq
    Lp = -(-L // QUERY_BLOCK) * QUERY_BLOCK
    h0 = jnp.concatenate([meta, x[0], jnp.zeros((Lp - L, D), F32)], axis=0)
    target = jnp.pad(loss_target[0], ((n_meta, Lp - L), (0, 0)))
    reducer = _Reducer(core)
    loss, dh0, grads = _local_step(h0, target, n_meta, seq, (mix_norm_g, ffn_norm_g),
                                   (dw_w_full, conv_dw_b, conv_ln_g, conv_ln_b), wts, final_norm_g,
                                   gather_behind, reducer)
    loss = lax.psum(loss[0, 0], ("x", "y", "c"))
    grad_x = dh0[n_meta:L][None]
    job, sink = reducer.take()
    sink(_run_job(job, "grads_across_last"))

    reduced = []
    for k in BIG:
        arr = None
        for l in range(depth):
            arr = _sum_chips(reducer.parts[k, l], reducer.across[k, l], where, l, depth, arr, f"sum_chips_{k}_{l}")
        reduced.append(arr)
    big_g = dict(zip(BIG, _rs_join_halves(reduced)))

    small_names = ("mix_g", "ffn_g", "dw_b", "ln_g", "ln_b", "dw_w")
    small = [jnp.stack(grads[k]) for k in small_names] + [grads["final_g"], dh0[:n_meta]]
    small_shapes = [a.shape for a in small]
    g_mix, g_ffn, g_dwb, g_lng, g_lnb, g_dww, g_final, g_meta = _unpack(_small_allreduce(_pack(small)), small_shapes)
    g_dww = lax.dynamic_slice_in_dim(g_dww, chip * c_shard, c_shard, axis=2)
    g_meta = lax.dynamic_slice_in_dim(g_meta, chip * meta_tokens.shape[1], meta_tokens.shape[1], axis=1)

    out_g, out_d, out_m, out_v = {}, {}, {}, {}
    for kk in BIG:
        k = TRANSPOSED.get(kk, kk)
        view = (lambda a: jnp.swapaxes(a, 1, 2)) if kk in TRANSPOSED else (lambda a: a)
        res = _adam(view(big_w[k]), big_g[kk], view(big_m[k]), view(big_v[k]), f"adam_{k}")
        out_g[k] = view(big_g[kk])
        out_d[k], out_m[k], out_v[k] = (view(a) for a in res)
    small_order = ("meta_tokens", "mix_norm_g", "conv_dw_w", "conv_dw_b", "conv_ln_g", "conv_ln_b",
                   "ffn_norm_g", "final_norm_g")
    sw = dict(meta_tokens=meta_tokens, mix_norm_g=mix_norm_g, conv_dw_w=conv_dw_w, conv_dw_b=conv_dw_b,
              conv_ln_g=conv_ln_g, conv_ln_b=conv_ln_b, ffn_norm_g=ffn_norm_g, final_norm_g=final_norm_g)
    sm = dict(meta_tokens=m_meta_tokens, mix_norm_g=m_mix_norm_g, conv_dw_w=m_conv_dw_w, conv_dw_b=m_conv_dw_b,
              conv_ln_g=m_conv_ln_g, conv_ln_b=m_conv_ln_b, ffn_norm_g=m_ffn_norm_g, final_norm_g=m_final_norm_g)
    sv = dict(meta_tokens=v_meta_tokens, mix_norm_g=v_mix_norm_g, conv_dw_w=v_conv_dw_w, conv_dw_b=v_conv_dw_b,
              conv_ln_g=v_conv_ln_g, conv_ln_b=v_conv_ln_b, ffn_norm_g=v_ffn_norm_g, final_norm_g=v_final_norm_g)
    sg = dict(meta_tokens=g_meta, mix_norm_g=g_mix, conv_dw_w=g_dww, conv_dw_b=g_dwb, conv_ln_g=g_lng,
              conv_ln_b=g_lnb, ffn_norm_g=g_ffn, final_norm_g=g_final)
    def slab(d):
        return _pack([d[k] for k in small_order])
    shapes = [sw[k].shape for k in small_order]
    deltas = _adam(slab(sw), slab(sg), slab(sm), slab(sv), "adam_small")
    for res, dst in zip(deltas, (out_d, out_m, out_v)):
        dst.update(zip(small_order, _unpack(res, shapes)))
    out_g.update(sg)

    order = ("meta_tokens", "mix_norm_g", "w_in", "conv_dw_w", "conv_dw_b", "conv_ln_g", "conv_ln_b", "w_out",
             "ffn_norm_g", "w_gate", "w_up", "w_down", "final_norm_g")
    return (loss, grad_x, *[out_g[k] for k in order], *[out_d[k] for k in order],
            *[out_m[k] for k in order], *[out_v[k] for k in order])
```

```python
import functools
import math

import jax
import jax.numpy as jnp
from jax import lax
from jax.experimental import pallas as pl
from jax.experimental.pallas import tpu as pltpu

F32 = jnp.float32
BF16 = jnp.bfloat16
MESH = pl.DeviceIdType.MESH

EPS = 1e-6
QUERY_BLOCK = 128
LANES = 128
HEAD_DIM = 64
LOG_STICK_FLOOR = -40.0
CONV_HALO = 32
N_CHIPS = 4
N_DEV = 8
VMEM_LIMIT = 56 * 1024 * 1024

ADAM_LR = 0.001
ADAM_B1 = 0.9
ADAM_B2 = 0.999
ADAM_EPS = 1e-08
ADAM_WD = 0.01
ADAM_STEP = 10


def _pick(n, prefs):
    for p in prefs:
        if n % p == 0:
            return p
    return n


def _params(sem=None):
    return pltpu.CompilerParams(dimension_semantics=sem, vmem_limit_bytes=VMEM_LIMIT)


def _sigmoid(x):
    return 1.0 / (1.0 + jnp.exp(-x))


def _rmsnorm_bwd(h, g, dy, dh_in, name):
    L, D = h.shape
    T = _pick(L, (384, 128))

    def body(h_ref, g_ref, dy_ref, dhin_ref, dh_ref, dg_ref):
        x = h_ref[...]
        dyv = dy_ref[...]
        r = lax.rsqrt(jnp.mean(x * x, axis=-1, keepdims=True) + EPS)
        xh = x * r
        dxh = dyv * g_ref[...]
        dh_ref[...] = dhin_ref[...] + r * (dxh - xh * jnp.mean(dxh * xh, axis=-1, keepdims=True))

        @pl.when(pl.program_id(0) == 0)
        def _():
            dg_ref[...] = jnp.zeros_like(dg_ref)

        dg_ref[...] += jnp.sum(dyv * xh, axis=0, keepdims=True)

    row = pl.BlockSpec((T, D), lambda i: (i, 0))
    vec = pl.BlockSpec((1, D), lambda i: (0, 0))
    return pl.pallas_call(
        body, name=name, grid=(L // T,),
        in_specs=[row, vec, row, row], out_specs=[row, vec],
        out_shape=[jax.ShapeDtypeStruct((L, D), F32), jax.ShapeDtypeStruct((1, D), F32)],
        compiler_params=_params(("arbitrary",)),
    )(h, g, dy, dh_in)


def _loss_head(h, g, target, n_meta, seq, name):
    L, D = h.shape
    T = _pick(L, (384, 128))

    def body(h_ref, g_ref, t_ref, loss_ref, dh_ref, dg_ref):
        i = pl.program_id(0)
        x = h_ref[...]
        gv = g_ref[...]
        r = lax.rsqrt(jnp.mean(x * x, axis=-1, keepdims=True) + EPS)
        xh = x * r
        y = xh * gv
        rows = i * T + lax.broadcasted_iota(jnp.int32, (T, 1), 0)
        live = (rows >= n_meta) & (rows < n_meta + seq)
        diff = jnp.where(live, y - t_ref[...], 0.0)
        dyv = diff / D
        dxh = dyv * gv
        dh_ref[...] = r * (dxh - xh * jnp.mean(dxh * xh, axis=-1, keepdims=True))

        @pl.when(i == 0)
        def _():
            dg_ref[...] = jnp.zeros_like(dg_ref)
            loss_ref[...] = jnp.zeros_like(loss_ref)

        dg_ref[...] += jnp.sum(dyv * xh, axis=0, keepdims=True)
        per_row = jnp.mean(diff * diff, axis=-1, keepdims=True)
        loss_ref[...] += 0.5 * jnp.sum(per_row, axis=0, keepdims=True)

    row = pl.BlockSpec((T, D), lambda i: (i, 0))
    vec = pl.BlockSpec((1, D), lambda i: (0, 0))
    one = pl.BlockSpec((1, 1), lambda i: (0, 0))
    return pl.pallas_call(
        body, name=name, grid=(L // T,),
        in_specs=[row, vec, row], out_specs=[one, row, vec],
        out_shape=[jax.ShapeDtypeStruct((1, 1), F32), jax.ShapeDtypeStruct((L, D), F32),
                   jax.ShapeDtypeStruct((1, D), F32)],
        compiler_params=_params(("arbitrary",)),
    )(h, g, target)


def _ffn_tiles(M, F):
    return _pick(M, (352, 384, 128)), _pick(F, (1408, 512, 256, 128))


def _resident(shape):
    return pl.BlockSpec((None,) + tuple(shape[1:]), lambda *_: (0,) * len(shape), pipeline_mode=pl.Buffered(1))


def _normed_rows(h_ref, g_ref, hn_ref, keep_ref):
    @pl.when(pl.program_id(1) == 0)
    def _():
        x = h_ref[...]
        r = lax.rsqrt(jnp.mean(x * x, axis=-1, keepdims=True) + EPS)
        keep_ref[...] = (x * r * g_ref[...]).astype(BF16)
        hn_ref[...] = keep_ref[...]


def _norm_in_proj(h, g, w_in, n_cols, name):
    M, D = h.shape
    tm = _pick(M, (1056, 384, 128))
    tn = _pick(n_cols, (512, 256, 128))

    def body(h_ref, g_ref, w_ref, hn_ref, o_ref, keep_ref):
        _normed_rows(h_ref, g_ref, hn_ref, keep_ref)
        o_ref[...] = jnp.dot(keep_ref[...], w_ref[...], preferred_element_type=F32).astype(o_ref.dtype)

    rows = pl.BlockSpec((tm, D), lambda i, j: (i, 0))
    return pl.pallas_call(
        body, name=name, grid=(M // tm, n_cols // tn),
        in_specs=[rows, pl.BlockSpec((1, D), lambda i, j: (0, 0)), pl.BlockSpec((None, D, tn), lambda i, j: (0, 0, j))],
        out_specs=[rows, pl.BlockSpec((tm, tn), lambda i, j: (i, j))],
        out_shape=[jax.ShapeDtypeStruct((M, D), BF16), jax.ShapeDtypeStruct((M, n_cols), BF16)],
        scratch_shapes=[pltpu.VMEM((tm, D), BF16)],
        compiler_params=_params(("parallel", "arbitrary")),
    )(h, g, w_in)


def _ffn_up(h, norm_g, w_gate_t, w_up_t, name, job=None):
    M, D = h.shape
    F = w_gate_t.shape[1]
    tm, tf = _ffn_tiles(M, F)
    nt = (((1,), (1,)), ((), ()))

    def body(h_ref, ng_ref, wg_ref, wu_ref, hn_ref, g_ref, u_ref, a_ref):
        x = h_ref[...]
        r = lax.rsqrt(jnp.mean(x * x, axis=-1, keepdims=True) + EPS)
        hv = (x * r * ng_ref[...]).astype(BF16)
        hn_ref[...] = hv
        for c in range(0, F, tf):
            gv = lax.dot_general(hv, wg_ref[c:c + tf, :], nt, preferred_element_type=F32)
            uv = lax.dot_general(hv, wu_ref[c:c + tf, :], nt, preferred_element_type=F32)
            g_ref[:, c:c + tf] = gv
            u_ref[:, c:c + tf] = uv
            a_ref[:, c:c + tf] = (gv * _sigmoid(gv) * uv).astype(a_ref.dtype)

    rows = pl.BlockSpec((tm, D), lambda i: (i, 0))
    wide = pl.BlockSpec((tm, F), lambda i: (i, 0))
    return _call_with_job(
        body, job, name=name, grid=(M // tm,),
        in_specs=[rows, pl.BlockSpec((1, D), lambda i: (0, 0)), _resident(w_gate_t.shape), _resident(w_up_t.shape)],
        out_specs=[rows, wide, wide, wide],
        out_shape=[jax.ShapeDtypeStruct((M, D), BF16), jax.ShapeDtypeStruct((M, F), F32),
                   jax.ShapeDtypeStruct((M, F), F32), jax.ShapeDtypeStruct((M, F), BF16)],
        scratch_shapes=[], args=(h, norm_g, w_gate_t, w_up_t))


def _ffn_down_bwd(dh, w_down, g, u, name):
    M, D = dh.shape
    F = g.shape[1]
    tm, tf = _ffn_tiles(M, F)

    def body(d_ref, w_ref, g_ref, u_ref, dg_ref, du_ref):
        dhv = d_ref[...].astype(BF16)
        for c in range(0, F, tf):
            dv = lax.dot_general(dhv, w_ref[c:c + tf, :], (((1,), (1,)), ((), ())), preferred_element_type=F32)
            gv = g_ref[:, c:c + tf]
            s = _sigmoid(gv)
            du_ref[:, c:c + tf] = (dv * (gv * s)).astype(du_ref.dtype)
            dg_ref[:, c:c + tf] = (dv * u_ref[:, c:c + tf] * (s * (1.0 + gv * (1.0 - s)))).astype(dg_ref.dtype)

    wide = pl.BlockSpec((tm, F), lambda i: (i, 0))
    return pl.pallas_call(
        body, name=name, grid=(M // tm,),
        in_specs=[pl.BlockSpec((tm, D), lambda i: (i, 0)), _resident(w_down.shape), wide, wide],
        out_specs=[wide, wide],
        out_shape=[jax.ShapeDtypeStruct((M, F), BF16), jax.ShapeDtypeStruct((M, F), BF16)],
        compiler_params=_params(("parallel",)),
    )(dh, w_down, g, u)


def _mm_nn(pairs, out_dtype, name, residual=None, cols=None):
    M = pairs[0][0].shape[0]
    col0, N = cols if cols is not None else (0, pairs[0][1].shape[-1])
    tm = _pick(M, (1056, 384, 128))
    shallow = sum(p[0].shape[1] for p in pairs) <= 1024
    tn = _pick(math.gcd(N, col0) if col0 else N, ((1024,) if shallow else ()) + (640, 512, 256, 128))
    jb = col0 // tn
    n = len(pairs)

    def body(*refs):
        a_refs, w_refs = refs[:n], refs[n:2 * n]
        o_ref = refs[-1]
        acc = None
        for a_ref, w_ref in zip(a_refs, w_refs):
            d = jnp.dot(a_ref[...].astype(BF16), w_ref[...], preferred_element_type=F32)
            acc = d if acc is None else acc + d
        if residual is not None:
            acc = acc + refs[2 * n][...]
        o_ref[...] = acc.astype(o_ref.dtype)

    in_specs = [pl.BlockSpec((tm, a.shape[1]), lambda i, j: (i, 0)) for a, _, _, _ in pairs]
    for a, _, layer, kblk in pairs:
        in_specs.append(pl.BlockSpec((None, a.shape[1], tn), functools.partial(lambda i, j, l, kb: (l, kb, j + jb), l=layer, kb=kblk)))
    args = [p[0] for p in pairs] + [p[1] for p in pairs]
    if residual is not None:
        in_specs.append(pl.BlockSpec((tm, tn), lambda i, j: (i, j)))
        args.append(residual)
    return pl.pallas_call(
        body, name=name, grid=(M // tm, N // tn), in_specs=in_specs,
        out_specs=pl.BlockSpec((tm, tn), lambda i, j: (i, j)),
        out_shape=jax.ShapeDtypeStruct((M, N), out_dtype),
        compiler_params=_params(("parallel", "parallel")),
    )(*args)


def _mm_nt(pairs, out_dtype, name):
    M = pairs[0][0].shape[0]
    K = pairs[0][1].shape[1]
    tm = _pick(M, (1056, 384, 128))
    shallow = sum(p[0].shape[1] for p in pairs) <= 1024
    tk = _pick(K, ((1024,) if shallow else ()) + (512, 1408, 256, 128))
    n = len(pairs)

    def body(*refs):
        d_refs, w_refs = refs[:n], refs[n:2 * n]
        o_ref = refs[-1]
        acc = None
        for d_ref, w_ref in zip(d_refs, w_refs):
            d = lax.dot_general(d_ref[...].astype(BF16), w_ref[...], (((1,), (1,)), ((), ())),
                                preferred_element_type=F32)
            acc = d if acc is None else acc + d
        o_ref[...] = acc.astype(o_ref.dtype)

    in_specs = [pl.BlockSpec((tm, d.shape[1]), lambda i, j: (i, 0)) for d, _, _, _ in pairs]
    for d, _, layer, cblk in pairs:
        in_specs.append(pl.BlockSpec((None, tk, d.shape[1]), functools.partial(lambda i, j, l, cb: (l, j, cb), l=layer, cb=cblk)))
    args = [p[0] for p in pairs] + [p[1] for p in pairs]
    return pl.pallas_call(
        body, name=name, grid=(M // tm, K // tk), in_specs=in_specs,
        out_specs=pl.BlockSpec((tm, tk), lambda i, j: (i, j)),
        out_shape=jax.ShapeDtypeStruct((M, K), out_dtype),
        compiler_params=_params(("parallel", "parallel")),
    )(*args)


def _mm_tn(a, b, name, col_sharded, chips=N_CHIPS):
    M, K = a.shape
    N = b.shape[1]
    tm = _pick(M, (1056, 384, 128))
    tk = _pick(K, (1024, 1408, 512, 256, 128))
    tn = N // N_CHIPS if col_sharded else _pick(N, (512, 128))

    def body(a_ref, b_ref, o_ref):
        @pl.when(pl.program_id(2) == 0)
        def _():
            o_ref[...] = jnp.zeros_like(o_ref)

        o_ref[...] += lax.dot_general(a_ref[...].astype(BF16), b_ref[...].astype(BF16),
                                      (((0,), (0,)), ((), ())), preferred_element_type=F32)

    if col_sharded:
        out_shape = jax.ShapeDtypeStruct((N_CHIPS, K, tn), F32)
        out_spec = pl.BlockSpec((None, tk, tn), lambda k, j, m: (j, k, 0))
    else:
        out_shape = jax.ShapeDtypeStruct((K, N), F32)
        out_spec = pl.BlockSpec((tk, tn), lambda k, j, m: (k, j))
    out = pl.pallas_call(
        body, name=name, grid=(K // tk, N // tn, M // tm),
        in_specs=[pl.BlockSpec((tm, tk), lambda k, j, m: (m, k)), pl.BlockSpec((tm, tn), lambda k, j, m: (m, j))],
        out_specs=out_spec, out_shape=out_shape,
        compiler_params=_params(("parallel", "parallel", "arbitrary")),
    )(a, b)
    return out if col_sharded else out.reshape(chips, K // chips, N)


def _stack_heads(x, scale=None):
    lane = lax.broadcasted_iota(jnp.int32, x.shape, 1)
    zero = jnp.zeros_like(x)
    lo = jnp.where(lane < HEAD_DIM, x, zero)
    hi = jnp.where(lane < HEAD_DIM, zero, x)
    out = jnp.concatenate([lo, hi], axis=0)
    return out if scale is None else out * scale


def _unstack_heads(x2):
    qb = x2.shape[0] // 2
    lane = lax.broadcasted_iota(jnp.int32, (qb, LANES), 1)
    return jnp.where(lane < HEAD_DIM, x2[:qb], x2[qb:])


def _dot2(x, m):
    xh = x.astype(BF16)
    xl = (x - xh.astype(F32)).astype(BF16)
    return jnp.dot(xh, m, preferred_element_type=F32) + jnp.dot(xl, m, preferred_element_type=F32)


ATTN_CHUNK = 32


def _in_lockstep(staged):
    waiting, live = list(staged), []
    while waiting or live:
        if waiting:
            live.append(waiting.pop(0))
        for gen in list(live):
            if next(gen, StopIteration) is StopIteration:
                live.remove(gen)


def _row_chunks():
    return [slice(r, r + ATTN_CHUNK) for r in range(0, 2 * QUERY_BLOCK, ATTN_CHUNK)]


def _chunk_valid(rows, older):
    shape = (rows.stop - rows.start, older + QUERY_BLOCK)
    r = (rows.start + lax.broadcasted_iota(jnp.int32, shape, 0)) & (QUERY_BLOCK - 1)
    return lax.broadcasted_iota(jnp.int32, shape, 1) < r + older


def _split_to(x, hi_ref, lo_ref, rows):
    xh = x.astype(BF16)
    hi_ref[rows, :] = xh
    lo_ref[rows, :] = (x - xh.astype(F32)).astype(BF16)


def _triangle(keys, strict):
    r = lax.broadcasted_iota(jnp.int32, (keys, keys), 0)
    c = lax.broadcasted_iota(jnp.int32, (keys, keys), 1)
    return jnp.where((r > c) if strict else (r >= c), 1.0, 0.0).astype(BF16)


def _row_total(first):
    lane = lax.broadcasted_iota(jnp.int32, first.shape, 1)
    total = jnp.sum(jnp.where(lane == 0, first, 0.0), axis=1, keepdims=True)
    return jnp.broadcast_to(total, first.shape)


def _pairs_per_step(n_pairs):
    return 2 if n_pairs % 2 == 0 else 1


def _lanes(g):
    return slice(g * LANES, (g + 1) * LANES)


def _sweep_older(i, step, carry_ref, first):
    def cond(state):
        n, live = state
        return jnp.logical_and(n < i, live)

    def older(state):
        n, _ = state
        step(i - 1 - n, False)
        return n + 1, jnp.max(carry_ref[...]) > LOG_STICK_FLOOR

    lax.while_loop(cond, older, (first, jnp.max(carry_ref[...]) > LOG_STICK_FLOOR))


class _CommJob:
    def __init__(self, inputs, out_shapes, aliases, n_sems, start, finish):
        self.inputs, self.out_shapes, self.aliases, self.n_sems = list(inputs), list(out_shapes), aliases, n_sems
        self.start, self.finish = start, finish


def _merge_jobs(jobs):
    jobs = [j for j in jobs if j is not None]
    if len(jobs) <= 1:
        return jobs[0] if jobs else None
    spans, aliases = [], {}
    i0 = o0 = s0 = 0
    for j in jobs:
        spans.append((i0, o0, s0))
        aliases.update({i0 + a: o0 + b for a, b in j.aliases.items()})
        i0, o0, s0 = i0 + len(j.inputs), o0 + len(j.out_shapes), s0 + j.n_sems

    def run(which):
        def go(ins, outs, send_sems, recv_sems):
            for j, (i, o, s) in zip(jobs, spans):
                getattr(j, which)(ins[i:i + len(j.inputs)], outs[o:o + len(j.out_shapes)],
                                  send_sems.at[pl.ds(s, j.n_sems)], recv_sems.at[pl.ds(s, j.n_sems)])
        return go

    return _CommJob([a for j in jobs for a in j.inputs], [s for j in jobs for s in j.out_shapes], aliases, s0,
                    run("start"), run("finish"))


def _call_with_job(core_body, job, *, name, grid, in_specs, out_specs, out_shape, scratch_shapes, args):
    sem = ("arbitrary",) * len(grid)
    if job is None:
        res = pl.pallas_call(core_body, name=name, grid=grid, in_specs=in_specs, out_specs=out_specs,
                             out_shape=out_shape, scratch_shapes=scratch_shapes, compiler_params=_params(sem))(*args)
        return list(res), []
    n_in, n_out, n_scr = len(in_specs), len(out_specs), len(scratch_shapes)
    m_in, m_out = len(job.inputs), len(job.out_shapes)

    def body(*refs):
        at = 0
        parts = []
        for count in (n_in, m_in, n_out, m_out, n_scr, 2):
            parts.append(refs[at:at + count])
            at += count
        ins, job_in, outs, job_outs, scratch, (send_sems, recv_sems) = parts
        first = functools.reduce(jnp.logical_and, [pl.program_id(a) == 0 for a in range(len(grid))])
        last = functools.reduce(jnp.logical_and, [pl.program_id(a) == grid[a] - 1 for a in range(len(grid))])

        @pl.when(first)
        def _():
            job.start(job_in, job_outs, send_sems, recv_sems)

        core_body(*ins, *outs, *scratch)

        @pl.when(last)
        def _():
            job.finish(job_in, job_outs, send_sems, recv_sems)

    res = pl.pallas_call(
        body, name=name, grid=grid, in_specs=list(in_specs) + [ANY] * m_in, out_specs=list(out_specs) + [ANY] * m_out,
        out_shape=list(out_shape) + job.out_shapes,
        input_output_aliases={n_in + a: n_out + b for a, b in job.aliases.items()},
        scratch_shapes=list(scratch_shapes) + [pltpu.SemaphoreType.DMA((job.n_sems,)), pltpu.SemaphoreType.DMA((job.n_sems,))],
        compiler_params=_params(sem),
    )(*args, *job.inputs)
    return list(res[:n_out]), list(res[n_out:])


def _run_job(job, name):
    m_in, m_out = len(job.inputs), len(job.out_shapes)

    def body(*refs):
        job_in, job_outs = refs[:m_in], refs[m_in:m_in + m_out]
        send_sems, recv_sems = refs[m_in + m_out:]
        job.start(job_in, job_outs, send_sems, recv_sems)
        job.finish(job_in, job_outs, send_sems, recv_sems)

    return list(pl.pallas_call(
        body, name=name, in_specs=[ANY] * m_in, out_specs=[ANY] * m_out, out_shape=job.out_shapes,
        input_output_aliases=dict(job.aliases),
        scratch_shapes=[pltpu.SemaphoreType.DMA((job.n_sems,)), pltpu.SemaphoreType.DMA((job.n_sems,))],
    )(*job.inputs))


def _attn_fwd(qkv, sb_width, name, job=None):
    L = qkv.shape[0]
    QB = QUERY_BLOCK
    nb = L // QB
    n_pairs = sb_width // LANES
    G = _pairs_per_step(n_pairs)
    W = G * LANES
    scale = 1.0 / math.sqrt(HEAD_DIM)

    def body(q_ref, k_ref, v_ref, o_ref, acc_ref, carry_ref, f32_ref, bf16_ref):
        i = pl.program_id(1)
        q2 = [_stack_heads(q_ref[:, _lanes(g)], scale) for g in range(G)]
        acc_ref[...] = jnp.zeros_like(acc_ref)
        carry_ref[...] = jnp.zeros_like(carry_ref)

        def step(g, j, older, masked):
            n = older + QB
            start = pl.multiple_of(j * QB, QB)
            kb = k_ref[pl.ds(start, n), _lanes(g)]
            vb = v_ref[pl.ds(start, n), _lanes(g)]
            zs, as_, bs = (f32_ref.at[g, t, :, :n] for t in range(3))
            hi, lo = (bf16_ref.at[g, t, :, :n] for t in range(2))
            zs[...] = lax.dot_general(q2[g], kb, (((1,), (1,)), ((), ())), preferred_element_type=F32)
            yield
            for rows in _row_chunks():
                z = zs[rows, :]
                sp = jnp.log(1.0 + jnp.exp(-jnp.abs(z)))
                b = jnp.minimum(-z, 0.0) - sp
                if masked:
                    b = jnp.where(_chunk_valid(rows, older), b, 0.0)
                as_[rows, :] = jnp.minimum(z, 0.0) - sp
                bs[rows, :] = b
                _split_to(b, hi, lo, rows)
            yield
            tri = _triangle(n, True)
            zs[...] = (jnp.dot(hi[...], tri, preferred_element_type=F32)
                       + jnp.dot(lo[...], tri, preferred_element_type=F32))
            yield
            for rows in _row_chunks():
                excl = zs[rows, :]
                total = _row_total(excl[:, :LANES] + bs[rows, :LANES])
                if not masked:
                    excl = excl + jnp.tile(carry_ref[g, rows, :], (1, n // LANES))
                carry_ref[g, rows, :] += total
                w = jnp.exp(as_[rows, :] + excl)
                if masked:
                    w = jnp.where(_chunk_valid(rows, older), w, 0.0)
                _split_to(w, hi, lo, rows)
            yield
            acc_ref[g] += (jnp.dot(hi[...], vb, preferred_element_type=F32)
                           + jnp.dot(lo[...], vb, preferred_element_type=F32))

        @pl.when(i == 0)
        def _():
            _in_lockstep([step(g, 0, 0, True) for g in range(G)])

        @pl.when(i > 0)
        def _():
            _in_lockstep([step(g, i - 1, QB, True) for g in range(G)])
            for g in range(G):
                _sweep_older(i, lambda j, _, g=g: _in_lockstep([step(g, j, 0, False)]), carry_ref.at[g], 1)

        for g in range(G):
            o_ref[:, _lanes(g)] = _unstack_heads(acc_ref[g])

    n_steps = n_pairs // G
    (out,), job_out = _call_with_job(
        body, job, name=name, grid=(n_steps, nb),
        in_specs=[pl.BlockSpec((QB, W), lambda p, i: (i, p)),
                  pl.BlockSpec((L, W), lambda p, i: (0, n_steps + p)),
                  pl.BlockSpec((L, W), lambda p, i: (0, 2 * n_steps + p))],
        out_specs=[pl.BlockSpec((QB, W), lambda p, i: (i, p))],
        out_shape=[jax.ShapeDtypeStruct((L, sb_width), F32)],
        scratch_shapes=[pltpu.VMEM((G, 2 * QB, LANES), F32), pltpu.VMEM((G, 2 * QB, LANES), F32),
                        pltpu.VMEM((G, 3, 2 * QB, 2 * QB), F32), pltpu.VMEM((G, 2, 2 * QB, 2 * QB), BF16)],
        args=(qkv, qkv, qkv))
    return out, job_out


def _attn_bwd(qkv, o, dmixed, sb_width, name, job=None):
    L = qkv.shape[0]
    QB = QUERY_BLOCK
    nb = L // QB
    n_pairs = sb_width // LANES
    G = _pairs_per_step(n_pairs)
    W = G * LANES
    scale = 1.0 / math.sqrt(HEAD_DIM)

    def body(q_ref, k_ref, v_ref, o_ref, do_ref, dq_ref, dk_ref, dv_ref,
             dq_acc, dk_acc, dv_acc, ce_ref, cr_ref, dtot_ref, f32_ref, bf16_ref):
        i = pl.program_id(1)

        @pl.when(i == 0)
        def _():
            dk_acc[...] = jnp.zeros_like(dk_acc)
            dv_acc[...] = jnp.zeros_like(dv_acc)

        q2s = [_stack_heads(q_ref[:, _lanes(g)], scale) for g in range(G)]
        do2s = [_stack_heads(do_ref[:, _lanes(g)].astype(BF16)) for g in range(G)]
        ones = jnp.ones((LANES, LANES), BF16)
        for g in range(G):
            ov = o_ref[:, _lanes(g)]
            dtot_ref[g] = _dot2(do2s[g].astype(F32) * jnp.concatenate([ov, ov], axis=0), ones)
        dq_acc[...] = jnp.zeros_like(dq_acc)
        ce_ref[...] = jnp.zeros_like(ce_ref)
        cr_ref[...] = jnp.zeros_like(cr_ref)

        def step(g, j, older, masked):
            n = older + QB
            wide = n // LANES
            q2, do2 = q2s[g], do2s[g]
            start = pl.multiple_of(j * QB, QB)
            kb = k_ref[pl.ds(start, n), _lanes(g)]
            vb = v_ref[pl.ds(start, n), _lanes(g)]
            zs, as_, bs, betas, gs = (f32_ref.at[g, t, :, :n] for t in range(5))
            hi, lo, wb = (bf16_ref.at[g, t, :, :n] for t in range(3))
            nt = (((1,), (1,)), ((), ()))
            zs[...] = lax.dot_general(q2, kb, nt, preferred_element_type=F32)
            gs[...] = lax.dot_general(do2, vb, nt, preferred_element_type=F32)
            yield
            for rows in _row_chunks():
                z = zs[rows, :]
                e = jnp.exp(-jnp.abs(z))
                sp = jnp.log(1.0 + e)
                b = jnp.minimum(-z, 0.0) - sp
                if masked:
                    b = jnp.where(_chunk_valid(rows, older), b, 0.0)
                rinv = 1.0 / (1.0 + e)
                as_[rows, :] = jnp.minimum(z, 0.0) - sp
                bs[rows, :] = b
                betas[rows, :] = jnp.where(z >= 0.0, rinv, e * rinv)
                _split_to(b, hi, lo, rows)
            yield
            tri = _triangle(n, True)
            zs[...] = (jnp.dot(hi[...], tri, preferred_element_type=F32)
                       + jnp.dot(lo[...], tri, preferred_element_type=F32))
            yield
            for rows in _row_chunks():
                excl = zs[rows, :]
                total = _row_total(excl[:, :LANES] + bs[rows, :LANES])
                if not masked:
                    excl = excl + jnp.tile(ce_ref[g, rows, :], (1, wide))
                ce_ref[g, rows, :] += total
                w = jnp.exp(as_[rows, :] + excl)
                if masked:
                    w = jnp.where(_chunk_valid(rows, older), w, 0.0)
                wb[rows, :] = w.astype(BF16)
                gw = w * gs[rows, :]
                gs[rows, :] = gw
                _split_to(gw, hi, lo, rows)
            yield
            tri = _triangle(n, False)
            zs[...] = (jnp.dot(hi[...], tri, preferred_element_type=F32)
                       + jnp.dot(lo[...], tri, preferred_element_type=F32))
            yield
            for rows in _row_chunks():
                rinc = zs[rows, :]
                total = _row_total(rinc[:, :LANES])
                if not masked:
                    rinc = rinc + jnp.tile(cr_ref[g, rows, :], (1, wide))
                cr_ref[g, rows, :] += total
                beta = betas[rows, :]
                dz = gs[rows, :] * (1.0 - beta) - beta * (jnp.tile(dtot_ref[g, rows, :], (1, wide)) - rinc)
                if masked:
                    dz = jnp.where(_chunk_valid(rows, older), dz, 0.0)
                hi[rows, :] = dz.astype(BF16)
            yield
            dzb = hi[...]
            dq_acc[g] += jnp.dot(dzb, kb, preferred_element_type=F32)
            dk_acc[pl.ds(start, n), _lanes(g)] += lax.dot_general(
                dzb, q2, (((0,), (0,)), ((), ())), preferred_element_type=F32)
            dv_acc[pl.ds(start, n), _lanes(g)] += lax.dot_general(
                wb[...], do2, (((0,), (0,)), ((), ())), preferred_element_type=F32)

        @pl.when(i == 0)
        def _():
            _in_lockstep([step(g, 0, 0, True) for g in range(G)])

        @pl.when(i > 0)
        def _():
            _in_lockstep([step(g, i - 1, QB, True) for g in range(G)])
            for g in range(G):
                _sweep_older(i, lambda j, _, g=g: _in_lockstep([step(g, j, 0, False)]), ce_ref.at[g], 1)

        for g in range(G):
            dq_ref[:, _lanes(g)] = (_unstack_heads(dq_acc[g]) * scale).astype(dq_ref.dtype)

        @pl.when(i == nb - 1)
        def _():
            dk_ref[...] = dk_acc[...].astype(dk_ref.dtype)
            dv_ref[...] = dv_acc[...].astype(dv_ref.dtype)

    n_steps = n_pairs // G
    blk = pl.BlockSpec((QB, W), lambda p, i: (i, p))
    col = pl.BlockSpec((L, W), lambda p, i: (0, p))
    return _call_with_job(
        body, job, name=name, grid=(n_steps, nb),
        in_specs=[blk,
                  pl.BlockSpec((L, W), lambda p, i: (0, n_steps + p)),
                  pl.BlockSpec((L, W), lambda p, i: (0, 2 * n_steps + p)),
                  blk, blk],
        out_specs=[blk, col, col],
        out_shape=[jax.ShapeDtypeStruct((L, sb_width), BF16)] * 3,
        scratch_shapes=[pltpu.VMEM((G, 2 * QB, LANES), F32), pltpu.VMEM((L, W), F32),
                        pltpu.VMEM((L, W), F32), pltpu.VMEM((G, 2 * QB, LANES), F32),
                        pltpu.VMEM((G, 2 * QB, LANES), F32), pltpu.VMEM((G, 2 * QB, LANES), F32),
                        pltpu.VMEM((G, 5, 2 * QB, 2 * QB), F32), pltpu.VMEM((G, 3, 2 * QB, 2 * QB), BF16)],
        args=(qkv, qkv, qkv, o, dmixed))


def _conv_tile(L):
    return _pick(L, (384, 128))


def _glu(x, C):
    return x[:, :C] * _sigmoid(x[:, C:])


CONV_CHUNK = 32
SHIFT_TAIL = 24


def _fill_shifted(src_ref, dst_ref):
    n = dst_ref.shape[1]
    for r in range(1, 8):
        dst_ref[r - 1] = src_ref[r:r + n, :]


def _rows_at(src_ref, shifted_ref, start, n):
    q, r = divmod(start, 8)
    if r == 0:
        return src_ref[start:start + n, :]
    return shifted_ref[r - 1, 8 * q:8 * q + n, :]


def _conv_fwd(cacg, dw_w, dw_b, ln_g, ln_b, name):
    L, C2 = cacg.shape
    C = C2 // 2
    T = _conv_tile(L)
    H = CONV_HALO
    K = dw_w.shape[0]
    CH = CONV_CHUNK

    def body(x_ref, prev_ref, w_ref, b_ref, g_ref, beta_ref, o_ref, u_ref, us_ref):
        i = pl.program_id(0)
        u_ref[0:H, :] = jnp.where(i > 0, _glu(prev_ref[...], C), 0.0)
        u_ref[H:, :] = _glu(x_ref[...], C)
        _fill_shifted(u_ref, us_ref)
        for c0 in range(0, T, CH):
            y = jnp.broadcast_to(b_ref[...], (CH, C))
            for k in range(K):
                y = y + w_ref[k:k + 1, :] * _rows_at(u_ref, us_ref, c0 + H - (K - 1) + k, CH)
            mu = jnp.mean(y, axis=-1, keepdims=True)
            yc = y - mu
            rstd = lax.rsqrt(jnp.mean(yc * yc, axis=-1, keepdims=True) + EPS)
            ln = yc * rstd * g_ref[...] + beta_ref[...]
            o_ref[c0:c0 + CH, :] = (ln * _sigmoid(ln)).astype(o_ref.dtype)

    vec = pl.BlockSpec((1, C), lambda i: (0, 0))
    return pl.pallas_call(
        body, name=name, grid=(L // T,),
        in_specs=[pl.BlockSpec((T, C2), lambda i: (i, 0)),
                  pl.BlockSpec((H, C2), lambda i: (jnp.maximum(i * (T // H) - 1, 0), 0)),
                  pl.BlockSpec((K, C), lambda i: (0, 0)), vec, vec, vec],
        out_specs=pl.BlockSpec((T, C), lambda i: (i, 0)),
        out_shape=jax.ShapeDtypeStruct((L, C), BF16),
        scratch_shapes=[pltpu.VMEM((T + H, C), F32), pltpu.VMEM((7, T + SHIFT_TAIL, C), F32)],
        compiler_params=_params(("parallel",)),
    )(cacg, cacg, dw_w, dw_b, ln_g, ln_b)


def _conv_bwd(cacg, dmixed, dw_w, dw_b, ln_g, ln_b, name, job=None):
    L, C2 = cacg.shape
    C = C2 // 2
    T = _conv_tile(L)
    H = CONV_HALO
    K = dw_w.shape[0]
    nt = L // T
    TE = T + H

    CH = CONV_CHUNK

    def body(x_ref, prev_ref, next_ref, d_ref, dnext_ref, w_ref, b_ref, g_ref, beta_ref,
             dca_ref, dcg_ref, dwt_ref, db_ref, dg_ref, dbeta_ref, u_ref, us_ref, dy_ref, dys_ref):
        i = pl.program_id(0)
        last = i == nt - 1

        @pl.when(i == 0)
        def _():
            dwt_ref[...] = jnp.zeros_like(dwt_ref)
            db_ref[...] = jnp.zeros_like(db_ref)
            dg_ref[...] = jnp.zeros_like(dg_ref)
            dbeta_ref[...] = jnp.zeros_like(dbeta_ref)

        u_ref[0:H, :] = jnp.where(i > 0, _glu(prev_ref[...], C), 0.0)
        u_ref[H:H + T, :] = _glu(x_ref[...], C)
        u_ref[H + T:, :] = _glu(next_ref[...], C)
        _fill_shifted(u_ref, us_ref)
        dg_acc = jnp.zeros((1, C), F32)
        dbeta_acc = jnp.zeros((1, C), F32)
        db_acc = jnp.zeros((1, C), F32)
        for c0 in range(0, TE, CH):
            y = jnp.broadcast_to(b_ref[...], (CH, C))
            for k in range(K):
                y = y + w_ref[k:k + 1, :] * _rows_at(u_ref, us_ref, c0 + H - (K - 1) + k, CH)
            mu = jnp.mean(y, axis=-1, keepdims=True)
            yc = y - mu
            rstd = lax.rsqrt(jnp.mean(yc * yc, axis=-1, keepdims=True) + EPS)
            yh = yc * rstd
            ln = yh * g_ref[...] + beta_ref[...]
            s = _sigmoid(ln)
            dout = d_ref[c0:c0 + CH, :] if c0 < T else jnp.where(last, 0.0, dnext_ref[c0 - T:c0 - T + CH, :])
            dln = dout * (s * (1.0 + ln * (1.0 - s)))
            dyh = dln * g_ref[...]
            dy = rstd * (dyh - jnp.mean(dyh, axis=-1, keepdims=True)
                         - yh * jnp.mean(dyh * yh, axis=-1, keepdims=True))
            dy_ref[c0:c0 + CH, :] = dy
            if c0 < T:
                dg_acc = dg_acc + jnp.sum(dln * yh, axis=0, keepdims=True)
                dbeta_acc = dbeta_acc + jnp.sum(dln, axis=0, keepdims=True)
                db_acc = db_acc + jnp.sum(dy, axis=0, keepdims=True)
        dg_ref[...] += dg_acc
        dbeta_ref[...] += dbeta_acc
        db_ref[...] += db_acc
        _fill_shifted(dy_ref, dys_ref)
        for k in range(K):
            dwt_ref[k:k + 1, :] += jnp.sum(
                dy_ref[0:T, :] * _rows_at(u_ref, us_ref, H - (K - 1) + k, T), axis=0, keepdims=True)
        for c0 in range(0, T, CH):
            du = jnp.zeros((CH, C), F32)
            for k in range(K):
                du = du + w_ref[k:k + 1, :] * _rows_at(dy_ref, dys_ref, c0 + (K - 1) - k, CH)
            x = x_ref[c0:c0 + CH, :]
            sg = _sigmoid(x[:, C:])
            dca_ref[c0:c0 + CH, :] = (du * sg).astype(dca_ref.dtype)
            dcg_ref[c0:c0 + CH, :] = (du * x[:, :C] * sg * (1.0 - sg)).astype(dcg_ref.dtype)

    nh = L // H
    vec = pl.BlockSpec((1, C), lambda i: (0, 0))
    row = pl.BlockSpec((T, C), lambda i: (i, 0))
    return _call_with_job(
        body, job, name=name, grid=(nt,),
        in_specs=[pl.BlockSpec((T, C2), lambda i: (i, 0)),
                  pl.BlockSpec((H, C2), lambda i: (jnp.maximum(i * (T // H) - 1, 0), 0)),
                  pl.BlockSpec((H, C2), lambda i: (jnp.minimum((i + 1) * (T // H), nh - 1), 0)),
                  pl.BlockSpec((T, C), lambda i: (i, 1)),
                  pl.BlockSpec((H, C), lambda i: (jnp.minimum((i + 1) * (T // H), nh - 1), 1)),
                  pl.BlockSpec((K, C), lambda i: (0, 0)), vec, vec, vec],
        out_specs=[row, row, pl.BlockSpec((H, C), lambda i: (0, 0)), vec, vec, vec],
        out_shape=[jax.ShapeDtypeStruct((L, C), BF16), jax.ShapeDtypeStruct((L, C), BF16),
                   jax.ShapeDtypeStruct((H, C), F32), jax.ShapeDtypeStruct((1, C), F32),
                   jax.ShapeDtypeStruct((1, C), F32), jax.ShapeDtypeStruct((1, C), F32)],
        scratch_shapes=[pltpu.VMEM((T + 2 * H, C), F32), pltpu.VMEM((7, TE + SHIFT_TAIL, C), F32),
                        pltpu.VMEM((TE, C), F32), pltpu.VMEM((7, T + SHIFT_TAIL, C), F32)],
        args=(cacg, cacg, cacg, dmixed, dmixed, dw_w, dw_b, ln_g, ln_b))


def _local_step(h0, target, n_meta, seq, norms, conv_p, wts, final_g, gather_behind, reducer):
    mix_g, ffn_g = norms
    dw_w, dw_b, ln_g, ln_b = conv_p
    depth = mix_g.shape[0]
    C = dw_b.shape[-1]
    sbw = (wts["w_in"][0].shape[-1] - 2 * C) // 3
    assert sbw == C, "the mixer halves must have equal width"
    row = lambda a, i: a[i][None, :]

    h = h0
    saved = []
    for i in range(depth):
        hn, proj_qkv = _norm_in_proj(h, row(mix_g, i), wts["w_in"][i], 3 * sbw, f"in_qkv_{i}")
        cacg = _mm_nn([(hn, wts["w_in"][i], 0, 0)], F32, f"in_conv_{i}", cols=(3 * sbw, 2 * C))
        job, keys = gather_behind.get(("attn", i), (None, ()))
        attn, arrived = _attn_fwd(proj_qkv, sbw, f"attn_fwd_{i}", job)
        for (wname, wl), arr in zip(keys, arrived):
            wts[wname][wl] = arr
        conv = _conv_fwd(cacg, dw_w[i], row(dw_b, i), row(ln_g, i), row(ln_b, i), f"conv_fwd_{i}")
        h_mid = _mm_nn([(attn, wts["w_out"][i], 0, 0), (conv, wts["w_out"][i], 0, 1)], F32, f"out_proj_{i}",
                       residual=h)
        job, keys = gather_behind.get(("ffn", i), (None, ()))
        (hn2, g, u, act), arrived = _ffn_up(h_mid, row(ffn_g, i), wts["w_gate_t"][i], wts["w_up_t"][i],
                                            f"ffn_up_{i}", job)
        for (wname, wl), arr in zip(keys, arrived):
            wts[wname][wl] = arr
        h_out = _mm_nn([(act, wts["w_down"][i], 0, 0)], F32, f"down_{i}", residual=h_mid)
        saved.append((h, hn, proj_qkv, cacg, attn, conv, h_mid, hn2, g, u, act))
        h = h_out

    loss, dh, d_final_g = _loss_head(h, final_g[None, :], target, n_meta, seq, "loss_head")

    grads = {k: [None] * depth for k in ("mix_g", "ffn_g", "dw_w", "dw_b", "ln_g", "ln_b")}
    for i in reversed(range(depth)):
        h_in, hn, proj_qkv, cacg, attn, conv, h_mid, hn2, g, u, act = saved[i]
        big = {}
        dg, du = _ffn_down_bwd(dh, wts["w_down"][i], g, u, f"ffn_down_bwd_{i}")
        big["w_down"] = _mm_tn(act, dh, f"dw_down_{i}", col_sharded=False)
        dhn2 = _mm_nn([(dg, wts["w_gate_t"][i], 0, 0), (du, wts["w_up_t"][i], 0, 0)], F32, f"d_hn2_{i}")
        big["w_gate_t"] = _mm_tn(dg, hn2, f"dw_gate_{i}", col_sharded=False)
        big["w_up_t"] = _mm_tn(du, hn2, f"dw_up_{i}", col_sharded=False)
        dh, d_ffn = _rmsnorm_bwd(h_mid, row(ffn_g, i), dhn2, dh, f"ffn_norm_bwd_{i}")
        dmixed = _mm_nt([(dh, wts["w_out"][i], 0, 0)], F32, f"d_mixed_{i}")
        big["w_out"] = jnp.concatenate([_mm_tn(attn, dh, f"dw_out_attn_{i}", col_sharded=False, chips=2),
                                        _mm_tn(conv, dh, f"dw_out_conv_{i}", col_sharded=False, chips=2)], axis=0)
        sib_job, sib_sink = reducer.to_sibling(i, big)
        x_job, x_sink = reducer.take()
        (dq, dk, dv), arrived = _attn_bwd(proj_qkv, attn, dmixed, sbw, f"attn_bwd_{i}",
                                          _merge_jobs([sib_job, x_job]))
        sib_sink(arrived[:len(sib_job.out_shapes)])
        x_sink(arrived[len(sib_job.out_shapes):])
        x_job, x_sink = reducer.take()
        (dca, dcg, d_dw, d_b, d_lg, d_lb), arrived = _conv_bwd(
            cacg, dmixed, dw_w[i], row(dw_b, i), row(ln_g, i), row(ln_b, i), f"conv_bwd_{i}", x_job)
        x_sink(arrived)
        grads["dw_w"][i] = d_dw[:dw_w.shape[1]]
        grads["dw_b"][i], grads["ln_g"][i], grads["ln_b"][i] = d_b[0], d_lg[0], d_lb[0]
        dproj = jnp.concatenate([dq, dk, dv, dca, dcg], axis=1)
        dhn = _mm_nt([(dproj, wts["w_in"][i], 0, 0)], F32, f"d_hn_{i}")
        reducer.ready(i, {"w_in": _mm_tn(hn, dproj, f"dw_in_{i}", col_sharded=True)})
        dh, d_mix = _rmsnorm_bwd(h_in, row(mix_g, i), dhn, dh, f"mix_norm_bwd_{i}")
        grads["mix_g"][i], grads["ffn_g"][i] = d_mix[0], d_ffn[0]
    grads["final_g"] = d_final_g[0]
    return loss, dh, grads


ANY = pl.BlockSpec(memory_space=pl.ANY)


def _position():
    return lax.axis_index("x"), lax.axis_index("y"), lax.axis_index("c")


def _chip_at(x, y, k):
    return (1 - x if k & 2 else x), (1 - y if k & 1 else y)


def _half_rows(ref, half, rows, base=0):
    start = pl.multiple_of(base + half * rows, 8)
    lead = (slice(None),) * (len(ref.shape) - 2)
    return ref.at[(*lead, pl.ds(start, rows), slice(None))]


def _gather_job(fulls, shard_shapes, col_sharded):
    n = len(fulls)

    def tools(f_refs, send_sems, recv_sems):
        def block(wi, chip, half):
            _, R, C = shard_shapes[wi]
            if col_sharded[wi]:
                cols = pl.ds(pl.multiple_of(chip * C, LANES), C)
                return f_refs[wi].at[:, pl.ds(pl.multiple_of(half * (R // 2), 8), R // 2), cols]
            return _half_rows(f_refs[wi], half, R // 2, base=chip * R)

        def copy(wi, slot, blk, to):
            return pltpu.make_async_remote_copy(
                src_ref=blk, dst_ref=blk, send_sem=send_sems.at[6 * wi + slot],
                recv_sem=recv_sems.at[6 * wi + slot], device_id=to, device_id_type=MESH)

        return block, copy

    def start(_, f_refs, send_sems, recv_sems):
        block, copy = tools(f_refs, send_sems, recv_sems)
        x, y, c = _position()
        me = 2 * x + y
        for wi in range(n):
            for k in (1, 2, 3):
                copy(wi, k - 1, block(wi, me, c), (*_chip_at(x, y, k), c)).start()

    def finish(_, f_refs, send_sems, recv_sems):
        block, copy = tools(f_refs, send_sems, recv_sems)
        x, y, c = _position()
        me = 2 * x + y
        for wi in range(n):
            for k in (1, 2, 3):
                landed = block(wi, me ^ k, c)
                copy(wi, k - 1, landed, (x, y, c)).wait_recv()
                copy(wi, 2 + k, landed, (x, y, 1 - c)).start()
        for wi in range(n):
            for k in (1, 2, 3):
                copy(wi, 2 + k, block(wi, me ^ k, 1 - c), (x, y, c)).wait_recv()
        for wi in range(n):
            for k in (1, 2, 3):
                copy(wi, k - 1, block(wi, me, c), (x, y, c)).wait_send()
                copy(wi, 2 + k, block(wi, me ^ k, c), (x, y, c)).wait_send()

    return _CommJob(fulls, [jax.ShapeDtypeStruct(f.shape, f.dtype) for f in fulls], {i: i for i in range(n)},
                    6 * n, start, finish)


def _place_shard(w, layer, chip, col_sharded, dtype, name):
    _, R, C = w.shape
    tr = _pick(R, (256, 352, 128, 48))
    nr = R // tr

    def body(chip_ref, w_ref, o_ref):
        o_ref[...] = w_ref[...].astype(dtype)

    if col_sharded:
        shape = (1, R, N_CHIPS * C)
        out_spec = pl.BlockSpec((None, tr, C), lambda r, chip_ref: (0, r, chip_ref[0]))
    else:
        shape = (1, N_CHIPS * R, C)
        out_spec = pl.BlockSpec((None, tr, C), lambda r, chip_ref: (0, chip_ref[0] * nr + r, 0))
    grid_spec = pltpu.PrefetchScalarGridSpec(
        num_scalar_prefetch=1, grid=(nr,),
        in_specs=[pl.BlockSpec((None, tr, C), lambda r, chip_ref: (layer, r, 0))], out_specs=out_spec)
    return pl.pallas_call(
        body, name=name, grid_spec=grid_spec, out_shape=jax.ShapeDtypeStruct(shape, dtype),
        compiler_params=_params(("parallel",)),
    )(chip, w)


def _sibling_job(grads):
    n = len(grads)

    def copies(g_refs, l_refs, send_sems, recv_sems):
        x, y, c = _position()
        return [pltpu.make_async_remote_copy(
            src_ref=_half_rows(g_refs[wi], 1 - c, grads[wi].shape[1] // 2), dst_ref=l_refs[wi],
            send_sem=send_sems.at[wi], recv_sem=recv_sems.at[wi],
            device_id=(x, y, 1 - c), device_id_type=MESH) for wi in range(n)]

    def start(*refs):
        for cp in copies(*refs):
            cp.start()

    def finish(*refs):
        for cp in copies(*refs):
            cp.wait()

    outs = [jax.ShapeDtypeStruct((g.shape[0], g.shape[1] // 2, g.shape[2]), g.dtype) for g in grads]
    return _CommJob(grads, outs, {}, n, start, finish)


def _chip_sum(g, landed, core, name):
    _, R, C = g.shape
    hr = R // 2
    tr = _pick(hr, (256, 352, 128))
    nr = hr // tr

    def body(c_ref, g_ref, l_ref, o_ref):
        o_ref[...] = (g_ref[...] + l_ref[...]).astype(BF16)

    grid_spec = pltpu.PrefetchScalarGridSpec(
        num_scalar_prefetch=1, grid=(N_CHIPS, nr),
        in_specs=[pl.BlockSpec((None, tr, C), lambda j, r, c_ref: (j, c_ref[0] * nr + r, 0)),
                  pl.BlockSpec((None, tr, C), lambda j, r, c_ref: (j, r, 0))],
        out_specs=pl.BlockSpec((None, tr, C), lambda j, r, c_ref: (j, r, 0)))
    return pl.pallas_call(
        body, name=name, grid_spec=grid_spec, out_shape=jax.ShapeDtypeStruct((N_CHIPS, hr, C), BF16),
        compiler_params=_params(("parallel", "parallel")),
    )(core, g, landed)


def _across_job(parts):
    n = len(parts)

    def copy(p_refs, l_refs, send_sems, recv_sems, wi, k, to):
        x, y, _ = _position()
        me = 2 * x + y
        return pltpu.make_async_remote_copy(
            src_ref=p_refs[wi].at[me ^ k], dst_ref=l_refs[wi].at[me],
            send_sem=send_sems.at[3 * wi + k - 1], recv_sem=recv_sems.at[3 * wi + k - 1],
            device_id=to, device_id_type=MESH)

    def start(p_refs, l_refs, send_sems, recv_sems):
        x, y, c = _position()
        for wi in range(n):
            for k in (1, 2, 3):
                copy(p_refs, l_refs, send_sems, recv_sems, wi, k, (*_chip_at(x, y, k), c)).start()

    def finish(p_refs, l_refs, send_sems, recv_sems):
        x, y, c = _position()
        me = 2 * x + y
        for wi in range(n):
            for k in (1, 2, 3):
                slot = l_refs[wi].at[me ^ k]
                pltpu.make_async_remote_copy(
                    src_ref=slot, dst_ref=slot, send_sem=send_sems.at[3 * wi + k - 1],
                    recv_sem=recv_sems.at[3 * wi + k - 1], device_id=(x, y, c), device_id_type=MESH).wait_recv()
        for wi in range(n):
            for k in (1, 2, 3):
                copy(p_refs, l_refs, send_sems, recv_sems, wi, k, (x, y, c)).wait_send()

    return _CommJob(parts, [jax.ShapeDtypeStruct(p.shape, p.dtype) for p in parts], {}, 3 * n, start, finish)


class _Reducer:
    def __init__(self, core):
        self.core, self.parts, self.across, self.pending = core, {}, {}, []

    def to_sibling(self, layer, big):
        names = list(big)
        flat = [big[k] for k in names]

        def sink(landed):
            for k, g, la in zip(names, flat, landed):
                self.parts[k, layer] = _chip_sum(g, la, self.core, f"chip_sum_{k}_{layer}")
                self.pending.append((k, layer))

        return _sibling_job(flat), sink

    def ready(self, layer, big):
        job, sink = self.to_sibling(layer, big)
        sink(_run_job(job, f"grads_to_sibling_{next(iter(big))}_{layer}"))

    def take(self):
        keys, self.pending = self.pending, []
        if not keys:
            return None, lambda results: None

        def sink(results):
            self.across.update(zip(keys, results))

        return _across_job([self.parts[key] for key in keys]), sink


def _sum_chips(parts, landed, where, layer, depth, prev, name):
    _, hr, C = landed.shape
    tr = _pick(hr, (256, 352, 128))
    nr = hr // tr

    def body(*refs):
        own_ref, slots, o_ref = refs[1], refs[2:2 + N_CHIPS], refs[-1]
        chip = refs[0][0]
        total = None
        for q in range(N_CHIPS):
            term = jnp.where(chip == q, own_ref[...], slots[q][...]).astype(F32)
            total = term if total is None else total + term
        o_ref[...] = total

    def slot_spec(q):
        return pl.BlockSpec((None, tr, C), lambda r, w: (jnp.where(w[0] == q, (q + 1) % N_CHIPS, q), r, 0))

    in_specs = [pl.BlockSpec((None, tr, C), lambda r, w: (w[0], r, 0))] + [slot_spec(q) for q in range(N_CHIPS)]
    args = [where, parts] + [landed] * N_CHIPS
    aliases = {}
    if prev is not None:
        in_specs.append(ANY)
        args.append(prev)
        aliases = {len(args) - 1: 0}
    grid_spec = pltpu.PrefetchScalarGridSpec(
        num_scalar_prefetch=1, grid=(nr,), in_specs=in_specs,
        out_specs=pl.BlockSpec((None, tr, C), lambda r, w: (layer, w[1] * nr + r, 0)))
    return pl.pallas_call(
        body, name=name, grid_spec=grid_spec, out_shape=jax.ShapeDtypeStruct((depth, 2 * hr, C), F32),
        input_output_aliases=aliases, compiler_params=_params(("parallel",)),
    )(*args)


def _rs_join_halves(reduced):
    n = len(reduced)

    def body(*refs):
        o_refs = refs[n:2 * n]
        send_sems, recv_sems = refs[2 * n:]
        x, y, c = _position()
        sent = []
        for wi in range(n):
            hr = reduced[wi].shape[1] // 2
            mine = _half_rows(o_refs[wi], c, hr)
            cp = pltpu.make_async_remote_copy(
                src_ref=mine, dst_ref=mine, send_sem=send_sems.at[wi], recv_sem=recv_sems.at[wi],
                device_id=(x, y, 1 - c), device_id_type=MESH)
            cp.start()
            sent.append(cp)
        for wi in range(n):
            hr = reduced[wi].shape[1] // 2
            theirs = _half_rows(o_refs[wi], 1 - c, hr)
            pltpu.make_async_remote_copy(
                src_ref=theirs, dst_ref=theirs, send_sem=send_sems.at[wi], recv_sem=recv_sems.at[wi],
                device_id=(x, y, c), device_id_type=MESH).wait_recv()
        for cp in sent:
            cp.wait_send()

    return pl.pallas_call(
        body, name="grads_join_halves", out_shape=[jax.ShapeDtypeStruct(r.shape, r.dtype) for r in reduced],
        in_specs=[ANY] * n, out_specs=[ANY] * n, input_output_aliases={i: i for i in range(n)},
        scratch_shapes=[pltpu.SemaphoreType.DMA((n,)), pltpu.SemaphoreType.DMA((n,))],
    )(*reduced)


def _small_allreduce(vec):
    rows = vec.shape[0]

    def body(v_ref, o_ref, land, send_sems, recv_sems):
        x, y, c = _position()
        me = 4 * x + 2 * y + c
        land[0] = v_ref[...]
        sent = []
        for k in range(1, N_DEV):
            to = (1 - x if k & 4 else x, 1 - y if k & 2 else y, 1 - c if k & 1 else c)
            cp = pltpu.make_async_remote_copy(
                src_ref=v_ref, dst_ref=land.at[k], send_sem=send_sems.at[k - 1], recv_sem=recv_sems.at[k - 1],
                device_id=to, device_id_type=MESH)
            cp.start()
            sent.append(cp)
        for cp in sent:
            cp.wait_recv()
        acc = land[me]
        for e in range(1, N_DEV):
            acc = acc + land[me ^ e]
        o_ref[...] = acc
        for cp in sent:
            cp.wait_send()

    vmem = pl.BlockSpec(memory_space=pltpu.VMEM)
    return pl.pallas_call(
        body, name="small_allreduce", out_shape=jax.ShapeDtypeStruct(vec.shape, F32),
        in_specs=[vmem], out_specs=vmem,
        scratch_shapes=[pltpu.VMEM((N_DEV, rows, LANES), F32), pltpu.SemaphoreType.DMA((N_DEV - 1,)),
                        pltpu.SemaphoreType.DMA((N_DEV - 1,))],
    )(vec)


def _adam_math(w, g, m, v):
    m = ADAM_B1 * m + (1.0 - ADAM_B1) * g
    v = ADAM_B2 * v + (1.0 - ADAM_B2) * jnp.square(g)
    m_hat = m / (1.0 - ADAM_B1 ** ADAM_STEP)
    v_hat = v / (1.0 - ADAM_B2 ** ADAM_STEP)
    delta = -ADAM_LR * (m_hat / (jnp.sqrt(v_hat) + ADAM_EPS) + ADAM_WD * w)
    return delta, m, v


def _adam(w, g, m, v, name):
    def body(w_ref, g_ref, m_ref, v_ref, d_ref, nm_ref, nv_ref):
        d_ref[...], nm_ref[...], nv_ref[...] = _adam_math(w_ref[...], g_ref[...], m_ref[...], v_ref[...])

    if w.ndim == 3:
        lyr, R, C = w.shape
        tr = _pick(R, (256, 352, 128))
        blk = pl.BlockSpec((None, tr, C), lambda l, r: (l, r, 0))
        grid, sem = (lyr, R // tr), ("parallel", "parallel")
    else:
        blk = pl.BlockSpec(w.shape, lambda: (0, 0))
        grid, sem = (), None
    return pl.pallas_call(
        body, name=name, grid=grid, in_specs=[blk] * 4, out_specs=[blk] * 3,
        out_shape=[jax.ShapeDtypeStruct(w.shape, F32)] * 3, compiler_params=_params(sem),
    )(w, g, m, v)


def _rows(a, pad_to=8):
    r = a.reshape(-1, LANES)
    extra = (-r.shape[0]) % pad_to
    return jnp.pad(r, ((0, extra), (0, 0))) if extra else r


def _pack(arrays):
    return jnp.concatenate([_rows(a) for a in arrays], axis=0)


def _unpack(slab, shapes):
    out, at = [], 0
    for shp in shapes:
        nrow = math.prod(shp) // LANES
        out.append(slab[at:at + nrow].reshape(shp))
        at += nrow + (-nrow) % 8
    return out


BIG = ("w_in", "w_out", "w_gate_t", "w_up_t", "w_down")
BIG_COL_SHARDED = (True, False, False, False, False)
TRANSPOSED = {"w_gate_t": "w_gate", "w_up_t": "w_up"}


def kernel(x, meta_tokens, mix_norm_g, w_in, conv_dw_w, conv_dw_b, conv_ln_g, conv_ln_b, w_out, ffn_norm_g, w_gate, w_up, w_down, final_norm_g, loss_target, m_meta_tokens, m_mix_norm_g, m_w_in, m_conv_dw_w, m_conv_dw_b, m_conv_ln_g, m_conv_ln_b, m_w_out, m_ffn_norm_g, m_w_gate, m_w_up, m_w_down, m_final_norm_g, v_meta_tokens, v_mix_norm_g, v_w_in, v_conv_dw_w, v_conv_dw_b, v_conv_ln_g, v_conv_ln_b, v_w_out, v_ffn_norm_g, v_w_gate, v_w_up, v_w_down, v_final_norm_g):
    n_meta, seq = meta_tokens.shape[0], x.shape[1]
    D = x.shape[2]
    depth, taps, c_shard = conv_dw_w.shape
    C = conv_dw_b.shape[-1]
    chip = (2 * lax.axis_index("x") + lax.axis_index("y")).astype(jnp.int32)
    core = lax.axis_index("c").astype(jnp.int32).reshape(1)
    chip1 = chip.reshape(1)
    where = jnp.concatenate([chip1, core])
    big_w = dict(w_in=w_in, w_out=w_out, w_gate=w_gate, w_up=w_up, w_down=w_down)
    big_m = dict(w_in=m_w_in, w_out=m_w_out, w_gate=m_w_gate, w_up=m_w_up, w_down=m_w_down)
    big_v = dict(w_in=v_w_in, w_out=v_w_out, w_gate=v_w_gate, w_up=v_w_up, w_down=v_w_down)

    small_shard = _pack([conv_dw_w, meta_tokens])[None]
    to_send = {k: jnp.swapaxes(big_w[TRANSPOSED[k]], 1, 2) if k in TRANSPOSED else big_w[k] for k in BIG}
    col = dict(zip(BIG, BIG_COL_SHARDED))
    wts = {k: [_place_shard(to_send[k], l, chip1, col[k], BF16, f"place_{k}_{l}") for l in range(depth)] for k in BIG}
    small_placed = _place_shard(small_shard, 0, chip1, False, F32, "place_small")

    def gather_job(keys, extra=()):
        arrays = [wts[k][l] for k, l in keys] + list(extra)
        shapes = [(1,) + to_send[k].shape[1:] for k, _ in keys] + [(1,) + small_shard.shape[1:]] * len(extra)
        return _gather_job(arrays, shapes, [col[k] for k, _ in keys] + [False] * len(extra))

    first_keys = [("w_in", 0)]
    *first, small_full = _run_job(gather_job(first_keys, [small_placed]), "gather_first")
    for (k, l), arr in zip(first_keys, first):
        wts[k][l] = arr
    behind_keys = {}
    for l in range(depth):
        behind_keys["attn", l] = [("w_out", l), ("w_gate_t", l), ("w_up_t", l), ("w_down", l)]
        if l + 1 < depth:
            behind_keys["ffn", l] = [("w_in", l + 1)]
    gather_behind = {host: (gather_job(keys), keys) for host, keys in behind_keys.items()}

    rows_shard = small_shard.shape[1]
    dw_full, meta_full = [], []
    for j in range(N_CHIPS):
        dwj, mj = _unpack(small_full[0, j * rows_shard:(j + 1) * rows_shard],
                          [conv_dw_w.shape, meta_tokens.shape])
        dw_full.append(dwj)
        meta_full.append(mj)
    dw_w_full = jnp.concatenate(dw_full, axis=2)
    meta = jnp.concatenate(meta_full, axis=1)

    L = n_meta + seq
    Lp = -(-L // QUERY_BLOCK) * QUERY_BLOCK
    h0 = jnp.concatenate([meta, x[0], jnp.zeros((Lp - L, D), F32)], axis=0)
    target = jnp.pad(loss_target[0], ((n_meta, Lp - L), (0, 0)))
    reducer = _Reducer(core)
    loss, dh0, grads = _local_step(h0, target, n_meta, seq, (mix_norm_g, ffn_norm_g),
                                   (dw_w_full, conv_dw_b, conv_ln_g, conv_ln_b), wts, final_norm_g,
                                   gather_behind, reducer)
    loss = lax.psum(loss[0, 0], ("x", "y", "c"))
    grad_x = dh0[n_meta:L][None]
    job, sink = reducer.take()
    sink(_run_job(job, "grads_across_last"))

    reduced = []
    for k in BIG:
        arr = None
        for l in range(depth):
            arr = _sum_chips(reducer.parts[k, l], reducer.across[k, l], where, l, depth, arr, f"sum_chips_{k}_{l}")
        reduced.append(arr)
    big_g = dict(zip(BIG, _rs_join_halves(reduced)))

    small_names = ("mix_g", "ffn_g", "dw_b", "ln_g", "ln_b", "dw_w")
    small = [jnp.stack(grads[k]) for k in small_names] + [grads["final_g"], dh0[:n_meta]]
    small_shapes = [a.shape for a in small]
    g_mix, g_ffn, g_dwb, g_lng, g_lnb, g_dww, g_final, g_meta = _unpack(_small_allreduce(_pack(small)), small_shapes)
    g_dww = lax.dynamic_slice_in_dim(g_dww, chip * c_shard, c_shard, axis=2)
    g_meta = lax.dynamic_slice_in_dim(g_meta, chip * meta_tokens.shape[1], meta_tokens.shape[1], axis=1)

    out_g, out_d, out_m, out_v = {}, {}, {}, {}
    for kk in BIG:
        k = TRANSPOSED.get(kk, kk)
        view = (lambda a: jnp.swapaxes(a, 1, 2)) if kk in TRANSPOSED else (lambda a: a)
        res = _adam(view(big_w[k]), big_g[kk], view(big_m[k]), view(big_v[k]), f"adam_{k}")
        out_g[k] = view(big_g[kk])
        out_d[k], out_m[k], out_v[k] = (view(a) for a in res)
    small_order = ("meta_tokens", "mix_norm_g", "conv_dw_w", "conv_dw_b", "conv_ln_g", "conv_ln_b",
                   "ffn_norm_g", "final_norm_g")
    sw = dict(meta_tokens=meta_tokens, mix_norm_g=mix_norm_g, conv_dw_w=conv_dw_w, conv_dw_b=conv_dw_b,
              conv_ln_g=conv_ln_g, conv_ln_b=conv_ln_b, ffn_norm_g=ffn_norm_g, final_norm_g=final_norm_g)
    sm = dict(meta_tokens=m_meta_tokens, mix_norm_g=m_mix_norm_g, conv_dw_w=m_conv_dw_w, conv_dw_b=m_conv_dw_b,
              conv_ln_g=m_conv_ln_g, conv_ln_b=m_conv_ln_b, ffn_norm_g=m_ffn_norm_g, final_norm_g=m_final_norm_g)
    sv = dict(meta_tokens=v_meta_tokens, mix_norm_g=v_mix_norm_g, conv_dw_w=v_conv_dw_w, conv_dw_b=v_conv_dw_b,
              conv_ln_g=v_conv_ln_g, conv_ln_b=v_conv_ln_b, ffn_norm_g=v_ffn_norm_g, final_norm_g=v_final_norm_g)
    sg = dict(meta_tokens=g_meta, mix_norm_g=g_mix, conv_dw_w=g_dww, conv_dw_b=g_dwb, conv_ln_g=g_lng,
              conv_ln_b=g_lnb, ffn_norm_g=g_ffn, final_norm_g=g_final)
    def slab(d):
        return _pack([d[k] for k in small_order])
    shapes = [sw[k].shape for k in small_order]
    deltas = _adam(slab(sw), slab(sg), slab(sm), slab(sv), "adam_small")
    for res, dst in zip(deltas, (out_d, out_m, out_v)):
        dst.update(zip(small_order, _unpack(res, shapes)))
    out_g.update(sg)

    order = ("meta_tokens", "mix_norm_g", "w_in", "conv_dw_w", "conv_dw_b", "conv_ln_g", "conv_ln_b", "w_out",
             "ffn_norm_g", "w_gate", "w_up", "w_down", "final_norm_g")
    return (loss, grad_x, *[out_g[k] for k in order], *[out_d[k] for k in order],
            *[out_m[k] for k in order], *[out_v[k] for k in order])
```

```python
import functools
import math

import jax
import jax.numpy as jnp
from jax import lax
from jax.experimental import pallas as pl
from jax.experimental.pallas import tpu as pltpu

F32 = jnp.float32
BF16 = jnp.bfloat16
MESH = pl.DeviceIdType.MESH

EPS = 1e-6
QUERY_BLOCK = 128
LANES = 128
HEAD_DIM = 64
LOG_STICK_FLOOR = -40.0
CONV_HALO = 32
N_CHIPS = 4
N_DEV = 8
VMEM_LIMIT = 56 * 1024 * 1024

ADAM_LR = 0.001
ADAM_B1 = 0.9
ADAM_B2 = 0.999
ADAM_EPS = 1e-08
ADAM_WD = 0.01
ADAM_STEP = 10


def _pick(n, prefs):
    for p in prefs:
        if n % p == 0:
            return p
    return n


def _params(sem=None):
    return pltpu.CompilerParams(dimension_semantics=sem, vmem_limit_bytes=VMEM_LIMIT)


def _sigmoid(x):
    return 1.0 / (1.0 + jnp.exp(-x))


def _rmsnorm_bwd(h, g, dy, dh_in, name):
    L, D = h.shape
    T = _pick(L, (384, 128))

    def body(h_ref, g_ref, dy_ref, dhin_ref, dh_ref, dg_ref):
        x = h_ref[...]
        dyv = dy_ref[...]
        r = lax.rsqrt(jnp.mean(x * x, axis=-1, keepdims=True) + EPS)
        xh = x * r
        dxh = dyv * g_ref[...]
        dh_ref[...] = dhin_ref[...] + r * (dxh - xh * jnp.mean(dxh * xh, axis=-1, keepdims=True))

        @pl.when(pl.program_id(0) == 0)
        def _():
            dg_ref[...] = jnp.zeros_like(dg_ref)

        dg_ref[...] += jnp.sum(dyv * xh, axis=0, keepdims=True)

    row = pl.BlockSpec((T, D), lambda i: (i, 0))
    vec = pl.BlockSpec((1, D), lambda i: (0, 0))
    return pl.pallas_call(
        body, name=name, grid=(L // T,),
        in_specs=[row, vec, row, row], out_specs=[row, vec],
        out_shape=[jax.ShapeDtypeStruct((L, D), F32), jax.ShapeDtypeStruct((1, D), F32)],
        compiler_params=_params(("arbitrary",)),
    )(h, g, dy, dh_in)


def _loss_head(h, g, target, n_meta, seq, name):
    L, D = h.shape
    T = _pick(L, (384, 128))

    def body(h_ref, g_ref, t_ref, loss_ref, dh_ref, dg_ref):
        i = pl.program_id(0)
        x = h_ref[...]
        gv = g_ref[...]
        r = lax.rsqrt(jnp.mean(x * x, axis=-1, keepdims=True) + EPS)
        xh = x * r
        y = xh * gv
        rows = i * T + lax.broadcasted_iota(jnp.int32, (T, 1), 0)
        live = (rows >= n_meta) & (rows < n_meta + seq)
        diff = jnp.where(live, y - t_ref[...], 0.0)
        dyv = diff / D
        dxh = dyv * gv
        dh_ref[...] = r * (dxh - xh * jnp.mean(dxh * xh, axis=-1, keepdims=True))

        @pl.when(i == 0)
        def _():
            dg_ref[...] = jnp.zeros_like(dg_ref)
            loss_ref[...] = jnp.zeros_like(loss_ref)

        dg_ref[...] += jnp.sum(dyv * xh, axis=0, keepdims=True)
        per_row = jnp.mean(diff * diff, axis=-1, keepdims=True)
        loss_ref[...] += 0.5 * jnp.sum(per_row, axis=0, keepdims=True)

    row = pl.BlockSpec((T, D), lambda i: (i, 0))
    vec = pl.BlockSpec((1, D), lambda i: (0, 0))
    one = pl.BlockSpec((1, 1), lambda i: (0, 0))
    return pl.pallas_call(
        body, name=name, grid=(L // T,),
        in_specs=[row, vec, row], out_specs=[one, row, vec],
        out_shape=[jax.ShapeDtypeStruct((1, 1), F32), jax.ShapeDtypeStruct((L, D), F32),
                   jax.ShapeDtypeStruct((1, D), F32)],
        compiler_params=_params(("arbitrary",)),
    )(h, g, target)


def _ffn_tiles(M, F):
    return _pick(M, (352, 384, 128)), _pick(F, (1408, 512, 256, 128))


def _resident(shape):
    return pl.BlockSpec((None,) + tuple(shape[1:]), lambda *_: (0,) * len(shape), pipeline_mode=pl.Buffered(1))


def _normed_rows(h_ref, g_ref, hn_ref, keep_ref):
    @pl.when(pl.program_id(1) == 0)
    def _():
        x = h_ref[...]
        r = lax.rsqrt(jnp.mean(x * x, axis=-1, keepdims=True) + EPS)
        keep_ref[...] = (x * r * g_ref[...]).astype(BF16)
        hn_ref[...] = keep_ref[...]


def _norm_in_proj(h, g, w_in, n_cols, name):
    M, D = h.shape
    tm = _pick(M, (1056, 384, 128))
    tn = _pick(n_cols, (512, 256, 128))

    def body(h_ref, g_ref, w_ref, hn_ref, o_ref, keep_ref):
        _normed_rows(h_ref, g_ref, hn_ref, keep_ref)
        o_ref[...] = jnp.dot(keep_ref[...], w_ref[...], preferred_element_type=F32).astype(o_ref.dtype)

    rows = pl.BlockSpec((tm, D), lambda i, j: (i, 0))
    return pl.pallas_call(
        body, name=name, grid=(M // tm, n_cols // tn),
        in_specs=[rows, pl.BlockSpec((1, D), lambda i, j: (0, 0)), pl.BlockSpec((None, D, tn), lambda i, j: (0, 0, j))],
        out_specs=[rows, pl.BlockSpec((tm, tn), lambda i, j: (i, j))],
        out_shape=[jax.ShapeDtypeStruct((M, D), BF16), jax.ShapeDtypeStruct((M, n_cols), BF16)],
        scratch_shapes=[pltpu.VMEM((tm, D), BF16)],
        compiler_params=_params(("parallel", "arbitrary")),
    )(h, g, w_in)


def _ffn_up(h, norm_g, w_gate_t, w_up_t, name, job=None):
    M, D = h.shape
    F = w_gate_t.shape[1]
    tm, tf = _ffn_tiles(M, F)
    nt = (((1,), (1,)), ((), ()))

    def body(h_ref, ng_ref, wg_ref, wu_ref, hn_ref, g_ref, u_ref, a_ref):
        x = h_ref[...]
        r = lax.rsqrt(jnp.mean(x * x, axis=-1, keepdims=True) + EPS)
        hv = (x * r * ng_ref[...]).astype(BF16)
        hn_ref[...] = hv
        for c in range(0, F, tf):
            gv = lax.dot_general(hv, wg_ref[c:c + tf, :], nt, preferred_element_type=F32)
            uv = lax.dot_general(hv, wu_ref[c:c + tf, :], nt, preferred_element_type=F32)
            g_ref[:, c:c + tf] = gv
            u_ref[:, c:c + tf] = uv
            a_ref[:, c:c + tf] = (gv * _sigmoid(gv) * uv).astype(a_ref.dtype)

    rows = pl.BlockSpec((tm, D), lambda i: (i, 0))
    wide = pl.BlockSpec((tm, F), lambda i: (i, 0))
    return _call_with_job(
        body, job, name=name, grid=(M // tm,),
        in_specs=[rows, pl.BlockSpec((1, D), lambda i: (0, 0)), _resident(w_gate_t.shape), _resident(w_up_t.shape)],
        out_specs=[rows, wide, wide, wide],
        out_shape=[jax.ShapeDtypeStruct((M, D), BF16), jax.ShapeDtypeStruct((M, F), F32),
                   jax.ShapeDtypeStruct((M, F), F32), jax.ShapeDtypeStruct((M, F), BF16)],
        scratch_shapes=[], args=(h, norm_g, w_gate_t, w_up_t))


def _ffn_down_bwd(dh, w_down, g, u, name):
    M, D = dh.shape
    F = g.shape[1]
    tm, tf = _ffn_tiles(M, F)

    def body(d_ref, w_ref, g_ref, u_ref, dg_ref, du_ref):
        dhv = d_ref[...].astype(BF16)
        for c in range(0, F, tf):
            dv = lax.dot_general(dhv, w_ref[c:c + tf, :], (((1,), (1,)), ((), ())), preferred_element_type=F32)
            gv = g_ref[:, c:c + tf]
            s = _sigmoid(gv)
            du_ref[:, c:c + tf] = (dv * (gv * s)).astype(du_ref.dtype)
            dg_ref[:, c:c + tf] = (dv * u_ref[:, c:c + tf] * (s * (1.0 + gv * (1.0 - s)))).astype(dg_ref.dtype)

    wide = pl.BlockSpec((tm, F), lambda i: (i, 0))
    return pl.pallas_call(
        body, name=name, grid=(M // tm,),
        in_specs=[pl.BlockSpec((tm, D), lambda i: (i, 0)), _resident(w_down.shape), wide, wide],
        out_specs=[wide, wide],
        out_shape=[jax.ShapeDtypeStruct((M, F), BF16), jax.ShapeDtypeStruct((M, F), BF16)],
        compiler_params=_params(("parallel",)),
    )(dh, w_down, g, u)


def _mm_nn(pairs, out_dtype, name, residual=None, cols=None, job=None):
    M = pairs[0][0].shape[0]
    col0, N = cols if cols is not None else (0, pairs[0][1].shape[-1])
    tm = _pick(M, (1056, 384, 128))
    shallow = sum(p[0].shape[1] for p in pairs) <= 1024
    tn = _pick(math.gcd(N, col0) if col0 else N, ((1024,) if shallow else ()) + (640, 512, 256, 128))
    jb = col0 // tn
    n = len(pairs)

    def body(*refs):
        a_refs, w_refs = refs[:n], refs[n:2 * n]
        o_ref = refs[-1]
        acc = None
        for a_ref, w_ref in zip(a_refs, w_refs):
            d = jnp.dot(a_ref[...].astype(BF16), w_ref[...], preferred_element_type=F32)
            acc = d if acc is None else acc + d
        if residual is not None:
            acc = acc + refs[2 * n][...]
        o_ref[...] = acc.astype(o_ref.dtype)

    in_specs = [pl.BlockSpec((tm, a.shape[1]), lambda i, j: (i, 0)) for a, _, _, _ in pairs]
    for a, _, layer, kblk in pairs:
        in_specs.append(pl.BlockSpec((None, a.shape[1], tn), functools.partial(lambda i, j, l, kb: (l, kb, j + jb), l=layer, kb=kblk)))
    args = [p[0] for p in pairs] + [p[1] for p in pairs]
    if residual is not None:
        in_specs.append(pl.BlockSpec((tm, tn), lambda i, j: (i, j)))
        args.append(residual)
    (out,), arrived = _call_with_job(
        body, job, name=name, grid=(M // tm, N // tn), in_specs=in_specs,
        out_specs=[pl.BlockSpec((tm, tn), lambda i, j: (i, j))],
        out_shape=[jax.ShapeDtypeStruct((M, N), out_dtype)], scratch_shapes=[], args=args)
    return out if job is None else (out, arrived)


def _mm_nt(pairs, out_dtype, name):
    M = pairs[0][0].shape[0]
    K = pairs[0][1].shape[1]
    tm = _pick(M, (1056, 384, 128))
    shallow = sum(p[0].shape[1] for p in pairs) <= 1024
    tk = _pick(K, ((1024,) if shallow else ()) + (512, 1408, 256, 128))
    n = len(pairs)

    def body(*refs):
        d_refs, w_refs = refs[:n], refs[n:2 * n]
        o_ref = refs[-1]
        acc = None
        for d_ref, w_ref in zip(d_refs, w_refs):
            d = lax.dot_general(d_ref[...].astype(BF16), w_ref[...], (((1,), (1,)), ((), ())),
                                preferred_element_type=F32)
            acc = d if acc is None else acc + d
        o_ref[...] = acc.astype(o_ref.dtype)

    in_specs = [pl.BlockSpec((tm, d.shape[1]), lambda i, j: (i, 0)) for d, _, _, _ in pairs]
    for d, _, layer, cblk in pairs:
        in_specs.append(pl.BlockSpec((None, tk, d.shape[1]), functools.partial(lambda i, j, l, cb: (l, j, cb), l=layer, cb=cblk)))
    args = [p[0] for p in pairs] + [p[1] for p in pairs]
    return pl.pallas_call(
        body, name=name, grid=(M // tm, K // tk), in_specs=in_specs,
        out_specs=pl.BlockSpec((tm, tk), lambda i, j: (i, j)),
        out_shape=jax.ShapeDtypeStruct((M, K), out_dtype),
        compiler_params=_params(("parallel", "parallel")),
    )(*args)


def _mm_tn(a, b, name, col_sharded, chips=N_CHIPS):
    M, K = a.shape
    N = b.shape[1]
    tm = _pick(M, (1056, 384, 128))
    tk = _pick(K, (1024, 1408, 512, 256, 128))
    tn = N // N_CHIPS if col_sharded else _pick(N, (512, 128))

    def body(a_ref, b_ref, o_ref):
        @pl.when(pl.program_id(2) == 0)
        def _():
            o_ref[...] = jnp.zeros_like(o_ref)

        o_ref[...] += lax.dot_general(a_ref[...].astype(BF16), b_ref[...].astype(BF16),
                                      (((0,), (0,)), ((), ())), preferred_element_type=F32)

    if col_sharded:
        out_shape = jax.ShapeDtypeStruct((N_CHIPS, K, tn), F32)
        out_spec = pl.BlockSpec((None, tk, tn), lambda k, j, m: (j, k, 0))
    else:
        out_shape = jax.ShapeDtypeStruct((K, N), F32)
        out_spec = pl.BlockSpec((tk, tn), lambda k, j, m: (k, j))
    out = pl.pallas_call(
        body, name=name, grid=(K // tk, N // tn, M // tm),
        in_specs=[pl.BlockSpec((tm, tk), lambda k, j, m: (m, k)), pl.BlockSpec((tm, tn), lambda k, j, m: (m, j))],
        out_specs=out_spec, out_shape=out_shape,
        compiler_params=_params(("parallel", "parallel", "arbitrary")),
    )(a, b)
    return out if col_sharded else out.reshape(chips, K // chips, N)


def _stack_heads(x, scale=None):
    lane = lax.broadcasted_iota(jnp.int32, x.shape, 1)
    zero = jnp.zeros_like(x)
    lo = jnp.where(lane < HEAD_DIM, x, zero)
    hi = jnp.where(lane < HEAD_DIM, zero, x)
    out = jnp.concatenate([lo, hi], axis=0)
    return out if scale is None else out * scale


def _unstack_heads(x2):
    qb = x2.shape[0] // 2
    lane = lax.broadcasted_iota(jnp.int32, (qb, LANES), 1)
    return jnp.where(lane < HEAD_DIM, x2[:qb], x2[qb:])


def _dot2(x, m):
    xh = x.astype(BF16)
    xl = (x - xh.astype(F32)).astype(BF16)
    return jnp.dot(xh, m, preferred_element_type=F32) + jnp.dot(xl, m, preferred_element_type=F32)


ATTN_CHUNK = 32


def _in_lockstep(staged):
    waiting, live = list(staged), []
    while waiting or live:
        if waiting:
            live.append(waiting.pop(0))
        for gen in list(live):
            if next(gen, StopIteration) is StopIteration:
                live.remove(gen)


def _row_chunks():
    return [slice(r, r + ATTN_CHUNK) for r in range(0, 2 * QUERY_BLOCK, ATTN_CHUNK)]


def _chunk_valid(rows, older):
    shape = (rows.stop - rows.start, older + QUERY_BLOCK)
    r = (rows.start + lax.broadcasted_iota(jnp.int32, shape, 0)) & (QUERY_BLOCK - 1)
    return lax.broadcasted_iota(jnp.int32, shape, 1) < r + older


def _split_to(x, hi_ref, lo_ref, rows):
    xh = x.astype(BF16)
    hi_ref[rows, :] = xh
    lo_ref[rows, :] = (x - xh.astype(F32)).astype(BF16)


def _triangle(keys, strict):
    r = lax.broadcasted_iota(jnp.int32, (keys, keys), 0)
    c = lax.broadcasted_iota(jnp.int32, (keys, keys), 1)
    return jnp.where((r > c) if strict else (r >= c), 1.0, 0.0).astype(BF16)


def _row_total(first):
    lane = lax.broadcasted_iota(jnp.int32, first.shape, 1)
    total = jnp.sum(jnp.where(lane == 0, first, 0.0), axis=1, keepdims=True)
    return jnp.broadcast_to(total, first.shape)


def _pairs_per_step(n_pairs):
    return 2 if n_pairs % 2 == 0 else 1


def _lanes(g):
    return slice(g * LANES, (g + 1) * LANES)


def _sweep_older(i, step, carry_ref, first):
    def cond(state):
        n, live = state
        return jnp.logical_and(n < i, live)

    def older(state):
        n, _ = state
        step(i - 1 - n, False)
        return n + 1, jnp.max(carry_ref[...]) > LOG_STICK_FLOOR

    lax.while_loop(cond, older, (first, jnp.max(carry_ref[...]) > LOG_STICK_FLOOR))


class _CommJob:
    def __init__(self, inputs, out_shapes, aliases, n_sems, start, finish):
        self.inputs, self.out_shapes, self.aliases, self.n_sems = list(inputs), list(out_shapes), aliases, n_sems
        self.start, self.finish = start, finish


def _merge_jobs(jobs):
    jobs = [j for j in jobs if j is not None]
    if len(jobs) <= 1:
        return jobs[0] if jobs else None
    spans, aliases = [], {}
    i0 = o0 = s0 = 0
    for j in jobs:
        spans.append((i0, o0, s0))
        aliases.update({i0 + a: o0 + b for a, b in j.aliases.items()})
        i0, o0, s0 = i0 + len(j.inputs), o0 + len(j.out_shapes), s0 + j.n_sems

    def run(which):
        def go(ins, outs, send_sems, recv_sems):
            for j, (i, o, s) in zip(jobs, spans):
                getattr(j, which)(ins[i:i + len(j.inputs)], outs[o:o + len(j.out_shapes)],
                                  send_sems.at[pl.ds(s, j.n_sems)], recv_sems.at[pl.ds(s, j.n_sems)])
        return go

    return _CommJob([a for j in jobs for a in j.inputs], [s for j in jobs for s in j.out_shapes], aliases, s0,
                    run("start"), run("finish"))


def _call_with_job(core_body, job, *, name, grid, in_specs, out_specs, out_shape, scratch_shapes, args):
    sem = ("arbitrary",) * len(grid)
    if job is None:
        res = pl.pallas_call(core_body, name=name, grid=grid, in_specs=in_specs, out_specs=out_specs,
                             out_shape=out_shape, scratch_shapes=scratch_shapes, compiler_params=_params(sem))(*args)
        return list(res), []
    n_in, n_out, n_scr = len(in_specs), len(out_specs), len(scratch_shapes)
    m_in, m_out = len(job.inputs), len(job.out_shapes)

    def body(*refs):
        at = 0
        parts = []
        for count in (n_in, m_in, n_out, m_out, n_scr, 2):
            parts.append(refs[at:at + count])
            at += count
        ins, job_in, outs, job_outs, scratch, (send_sems, recv_sems) = parts
        first = functools.reduce(jnp.logical_and, [pl.program_id(a) == 0 for a in range(len(grid))])
        last = functools.reduce(jnp.logical_and, [pl.program_id(a) == grid[a] - 1 for a in range(len(grid))])

        @pl.when(first)
        def _():
            job.start(job_in, job_outs, send_sems, recv_sems)

        core_body(*ins, *outs, *scratch)

        @pl.when(last)
        def _():
            job.finish(job_in, job_outs, send_sems, recv_sems)

    res = pl.pallas_call(
        body, name=name, grid=grid, in_specs=list(in_specs) + [ANY] * m_in, out_specs=list(out_specs) + [ANY] * m_out,
        out_shape=list(out_shape) + job.out_shapes,
        input_output_aliases={n_in + a: n_out + b for a, b in job.aliases.items()},
        scratch_shapes=list(scratch_shapes) + [pltpu.SemaphoreType.DMA((job.n_sems,)), pltpu.SemaphoreType.DMA((job.n_sems,))],
        compiler_params=_params(sem),
    )(*args, *job.inputs)
    return list(res[:n_out]), list(res[n_out:])


def _run_job(job, name):
    m_in, m_out = len(job.inputs), len(job.out_shapes)

    def body(*refs):
        job_in, job_outs = refs[:m_in], refs[m_in:m_in + m_out]
        send_sems, recv_sems = refs[m_in + m_out:]
        job.start(job_in, job_outs, send_sems, recv_sems)
        job.finish(job_in, job_outs, send_sems, recv_sems)

    return list(pl.pallas_call(
        body, name=name, in_specs=[ANY] * m_in, out_specs=[ANY] * m_out, out_shape=job.out_shapes,
        input_output_aliases=dict(job.aliases),
        scratch_shapes=[pltpu.SemaphoreType.DMA((job.n_sems,)), pltpu.SemaphoreType.DMA((job.n_sems,))],
    )(*job.inputs))


def _attn_fwd(qkv, sb_width, name, job=None):
    L = qkv.shape[0]
    QB = QUERY_BLOCK
    nb = L // QB
    n_pairs = sb_width // LANES
    G = _pairs_per_step(n_pairs)
    W = G * LANES
    scale = 1.0 / math.sqrt(HEAD_DIM)

    def body(q_ref, k_ref, v_ref, o_ref, acc_ref, carry_ref, f32_ref, bf16_ref):
        i = pl.program_id(1)
        q2 = [_stack_heads(q_ref[:, _lanes(g)], scale) for g in range(G)]
        acc_ref[...] = jnp.zeros_like(acc_ref)
        carry_ref[...] = jnp.zeros_like(carry_ref)

        def step(g, j, older, masked):
            n = older + QB
            start = pl.multiple_of(j * QB, QB)
            kb = k_ref[pl.ds(start, n), _lanes(g)]
            vb = v_ref[pl.ds(start, n), _lanes(g)]
            zs, as_, bs = (f32_ref.at[g, t, :, :n] for t in range(3))
            hi, lo = (bf16_ref.at[g, t, :, :n] for t in range(2))
            zs[...] = lax.dot_general(q2[g], kb, (((1,), (1,)), ((), ())), preferred_element_type=F32)
            yield
            for rows in _row_chunks():
                z = zs[rows, :]
                sp = jnp.log(1.0 + jnp.exp(-jnp.abs(z)))
                b = jnp.minimum(-z, 0.0) - sp
                if masked:
                    b = jnp.where(_chunk_valid(rows, older), b, 0.0)
                as_[rows, :] = jnp.minimum(z, 0.0) - sp
                bs[rows, :] = b
                _split_to(b, hi, lo, rows)
            yield
            tri = _triangle(n, True)
            zs[...] = (jnp.dot(hi[...], tri, preferred_element_type=F32)
                       + jnp.dot(lo[...], tri, preferred_element_type=F32))
            yield
            for rows in _row_chunks():
                excl = zs[rows, :]
                total = _row_total(excl[:, :LANES] + bs[rows, :LANES])
                if not masked:
                    excl = excl + jnp.tile(carry_ref[g, rows, :], (1, n // LANES))
                carry_ref[g, rows, :] += total
                w = jnp.exp(as_[rows, :] + excl)
                if masked:
                    w = jnp.where(_chunk_valid(rows, older), w, 0.0)
                _split_to(w, hi, lo, rows)
            yield
            acc_ref[g] += (jnp.dot(hi[...], vb, preferred_element_type=F32)
                           + jnp.dot(lo[...], vb, preferred_element_type=F32))

        @pl.when(i == 0)
        def _():
            _in_lockstep([step(g, 0, 0, True) for g in range(G)])

        @pl.when(i > 0)
        def _():
            _in_lockstep([step(g, i - 1, QB, True) for g in range(G)])
            for g in range(G):
                _sweep_older(i, lambda j, _, g=g: _in_lockstep([step(g, j, 0, False)]), carry_ref.at[g], 1)

        for g in range(G):
            o_ref[:, _lanes(g)] = _unstack_heads(acc_ref[g])

    n_steps = n_pairs // G
    (out,), job_out = _call_with_job(
        body, job, name=name, grid=(n_steps, nb),
        in_specs=[pl.BlockSpec((QB, W), lambda p, i: (i, p)),
                  pl.BlockSpec((L, W), lambda p, i: (0, n_steps + p)),
                  pl.BlockSpec((L, W), lambda p, i: (0, 2 * n_steps + p))],
        out_specs=[pl.BlockSpec((QB, W), lambda p, i: (i, p))],
        out_shape=[jax.ShapeDtypeStruct((L, sb_width), F32)],
        scratch_shapes=[pltpu.VMEM((G, 2 * QB, LANES), F32), pltpu.VMEM((G, 2 * QB, LANES), F32),
                        pltpu.VMEM((G, 3, 2 * QB, 2 * QB), F32), pltpu.VMEM((G, 2, 2 * QB, 2 * QB), BF16)],
        args=(qkv, qkv, qkv))
    return out, job_out


def _attn_bwd(qkv, o, dmixed, sb_width, name, job=None):
    L = qkv.shape[0]
    QB = QUERY_BLOCK
    nb = L // QB
    n_pairs = sb_width // LANES
    G = _pairs_per_step(n_pairs)
    W = G * LANES
    scale = 1.0 / math.sqrt(HEAD_DIM)

    def body(q_ref, k_ref, v_ref, o_ref, do_ref, dq_ref, dk_ref, dv_ref,
             dq_acc, dk_acc, dv_acc, ce_ref, cr_ref, dtot_ref, f32_ref, bf16_ref):
        i = pl.program_id(1)

        @pl.when(i == 0)
        def _():
            dk_acc[...] = jnp.zeros_like(dk_acc)
            dv_acc[...] = jnp.zeros_like(dv_acc)

        q2s = [_stack_heads(q_ref[:, _lanes(g)], scale) for g in range(G)]
        do2s = [_stack_heads(do_ref[:, _lanes(g)].astype(BF16)) for g in range(G)]
        ones = jnp.ones((LANES, LANES), BF16)
        for g in range(G):
            ov = o_ref[:, _lanes(g)]
            dtot_ref[g] = _dot2(do2s[g].astype(F32) * jnp.concatenate([ov, ov], axis=0), ones)
        dq_acc[...] = jnp.zeros_like(dq_acc)
        ce_ref[...] = jnp.zeros_like(ce_ref)
        cr_ref[...] = jnp.zeros_like(cr_ref)

        def step(g, j, older, masked):
            n = older + QB
            wide = n // LANES
            q2, do2 = q2s[g], do2s[g]
            start = pl.multiple_of(j * QB, QB)
            kb = k_ref[pl.ds(start, n), _lanes(g)]
            vb = v_ref[pl.ds(start, n), _lanes(g)]
            zs, as_, bs, betas, gs = (f32_ref.at[g, t, :, :n] for t in range(5))
            hi, lo, wb = (bf16_ref.at[g, t, :, :n] for t in range(3))
            nt = (((1,), (1,)), ((), ()))
            zs[...] = lax.dot_general(q2, kb, nt, preferred_element_type=F32)
            gs[...] = lax.dot_general(do2, vb, nt, preferred_element_type=F32)
            yield
            for rows in _row_chunks():
                z = zs[rows, :]
                e = jnp.exp(-jnp.abs(z))
                sp = jnp.log(1.0 + e)
                b = jnp.minimum(-z, 0.0) - sp
                if masked:
                    b = jnp.where(_chunk_valid(rows, older), b, 0.0)
                rinv = 1.0 / (1.0 + e)
                as_[rows, :] = jnp.minimum(z, 0.0) - sp
                bs[rows, :] = b
                betas[rows, :] = jnp.where(z >= 0.0, rinv, e * rinv)
                _split_to(b, hi, lo, rows)
            yield
            tri = _triangle(n, True)
            zs[...] = (jnp.dot(hi[...], tri, preferred_element_type=F32)
                       + jnp.dot(lo[...], tri, preferred_element_type=F32))
            yield
            for rows in _row_chunks():
                excl = zs[rows, :]
                total = _row_total(excl[:, :LANES] + bs[rows, :LANES])
                if not masked:
                    excl = excl + jnp.tile(ce_ref[g, rows, :], (1, wide))
                ce_ref[g, rows, :] += total
                w = jnp.exp(as_[rows, :] + excl)
                if masked:
                    w = jnp.where(_chunk_valid(rows, older), w, 0.0)
                wb[rows, :] = w.astype(BF16)
                gw = w * gs[rows, :]
                gs[rows, :] = gw
                _split_to(gw, hi, lo, rows)
            yield
            tri = _triangle(n, False)
            zs[...] = (jnp.dot(hi[...], tri, preferred_element_type=F32)
                       + jnp.dot(lo[...], tri, preferred_element_type=F32))
            yield
            for rows in _row_chunks():
                rinc = zs[rows, :]
                total = _row_total(rinc[:, :LANES])
                if not masked:
                    rinc = rinc + jnp.tile(cr_ref[g, rows, :], (1, wide))
                cr_ref[g, rows, :] += total
                beta = betas[rows, :]
                dz = gs[rows, :] * (1.0 - beta) - beta * (jnp.tile(dtot_ref[g, rows, :], (1, wide)) - rinc)
                if masked:
                    dz = jnp.where(_chunk_valid(rows, older), dz, 0.0)
                hi[rows, :] = dz.astype(BF16)
            yield
            dzb = hi[...]
            dq_acc[g] += jnp.dot(dzb, kb, preferred_element_type=F32)
            dk_acc[pl.ds(start, n), _lanes(g)] += lax.dot_general(
                dzb, q2, (((0,), (0,)), ((), ())), preferred_element_type=F32)
            dv_acc[pl.ds(start, n), _lanes(g)] += lax.dot_general(
                wb[...], do2, (((0,), (0,)), ((), ())), preferred_element_type=F32)

        @pl.when(i == 0)
        def _():
            _in_lockstep([step(g, 0, 0, True) for g in range(G)])

        @pl.when(i > 0)
        def _():
            _in_lockstep([step(g, i - 1, QB, True) for g in range(G)])
            for g in range(G):
                _sweep_older(i, lambda j, _, g=g: _in_lockstep([step(g, j, 0, False)]), ce_ref.at[g], 1)

        for g in range(G):
            dq_ref[:, _lanes(g)] = (_unstack_heads(dq_acc[g]) * scale).astype(dq_ref.dtype)

        @pl.when(i == nb - 1)
        def _():
            dk_ref[...] = dk_acc[...].astype(dk_ref.dtype)
            dv_ref[...] = dv_acc[...].astype(dv_ref.dtype)

    n_steps = n_pairs // G
    blk = pl.BlockSpec((QB, W), lambda p, i: (i, p))
    col = pl.BlockSpec((L, W), lambda p, i: (0, p))
    return _call_with_job(
        body, job, name=name, grid=(n_steps, nb),
        in_specs=[blk,
                  pl.BlockSpec((L, W), lambda p, i: (0, n_steps + p)),
                  pl.BlockSpec((L, W), lambda p, i: (0, 2 * n_steps + p)),
                  blk, blk],
        out_specs=[blk, col, col],
        out_shape=[jax.ShapeDtypeStruct((L, sb_width), BF16)] * 3,
        scratch_shapes=[pltpu.VMEM((G, 2 * QB, LANES), F32), pltpu.VMEM((L, W), F32),
                        pltpu.VMEM((L, W), F32), pltpu.VMEM((G, 2 * QB, LANES), F32),
                        pltpu.VMEM((G, 2 * QB, LANES), F32), pltpu.VMEM((G, 2 * QB, LANES), F32),
                        pltpu.VMEM((G, 5, 2 * QB, 2 * QB), F32), pltpu.VMEM((G, 3, 2 * QB, 2 * QB), BF16)],
        args=(qkv, qkv, qkv, o, dmixed))


def _conv_tile(L):
    return _pick(L, (384, 128))


def _glu(x, C):
    return x[:, :C] * _sigmoid(x[:, C:])


CONV_CHUNK = 32
SHIFT_TAIL = 24


def _fill_shifted(src_ref, dst_ref):
    n = dst_ref.shape[1]
    for r in range(1, 8):
        dst_ref[r - 1] = src_ref[r:r + n, :]


def _rows_at(src_ref, shifted_ref, start, n):
    q, r = divmod(start, 8)
    if r == 0:
        return src_ref[start:start + n, :]
    return shifted_ref[r - 1, 8 * q:8 * q + n, :]


def _conv_fwd(cacg, dw_w, dw_b, ln_g, ln_b, name, job=None):
    L, C2 = cacg.shape
    C = C2 // 2
    T = _conv_tile(L)
    H = CONV_HALO
    K = dw_w.shape[0]
    CH = CONV_CHUNK

    def body(x_ref, prev_ref, w_ref, b_ref, g_ref, beta_ref, o_ref, u_ref, us_ref):
        i = pl.program_id(0)
        u_ref[0:H, :] = jnp.where(i > 0, _glu(prev_ref[...], C), 0.0)
        u_ref[H:, :] = _glu(x_ref[...], C)
        _fill_shifted(u_ref, us_ref)
        for c0 in range(0, T, CH):
            y = jnp.broadcast_to(b_ref[...], (CH, C))
            for k in range(K):
                y = y + w_ref[k:k + 1, :] * _rows_at(u_ref, us_ref, c0 + H - (K - 1) + k, CH)
            mu = jnp.mean(y, axis=-1, keepdims=True)
            yc = y - mu
            rstd = lax.rsqrt(jnp.mean(yc * yc, axis=-1, keepdims=True) + EPS)
            ln = yc * rstd * g_ref[...] + beta_ref[...]
            o_ref[c0:c0 + CH, :] = (ln * _sigmoid(ln)).astype(o_ref.dtype)

    vec = pl.BlockSpec((1, C), lambda i: (0, 0))
    (out,), arrived = _call_with_job(
        body, job, name=name, grid=(L // T,),
        in_specs=[pl.BlockSpec((T, C2), lambda i: (i, 0)),
                  pl.BlockSpec((H, C2), lambda i: (jnp.maximum(i * (T // H) - 1, 0), 0)),
                  pl.BlockSpec((K, C), lambda i: (0, 0)), vec, vec, vec],
        out_specs=[pl.BlockSpec((T, C), lambda i: (i, 0))],
        out_shape=[jax.ShapeDtypeStruct((L, C), BF16)],
        scratch_shapes=[pltpu.VMEM((T + H, C), F32), pltpu.VMEM((7, T + SHIFT_TAIL, C), F32)],
        args=(cacg, cacg, dw_w, dw_b, ln_g, ln_b))
    return out, arrived


def _conv_bwd(cacg, dmixed, dw_w, dw_b, ln_g, ln_b, name, job=None):
    L, C2 = cacg.shape
    C = C2 // 2
    T = _conv_tile(L)
    H = CONV_HALO
    K = dw_w.shape[0]
    nt = L // T
    TE = T + H

    CH = CONV_CHUNK

    def body(x_ref, prev_ref, next_ref, d_ref, dnext_ref, w_ref, b_ref, g_ref, beta_ref,
             dca_ref, dcg_ref, dwt_ref, db_ref, dg_ref, dbeta_ref, u_ref, us_ref, dy_ref, dys_ref):
        i = pl.program_id(0)
        last = i == nt - 1

        @pl.when(i == 0)
        def _():
            dwt_ref[...] = jnp.zeros_like(dwt_ref)
            db_ref[...] = jnp.zeros_like(db_ref)
            dg_ref[...] = jnp.zeros_like(dg_ref)
            dbeta_ref[...] = jnp.zeros_like(dbeta_ref)

        u_ref[0:H, :] = jnp.where(i > 0, _glu(prev_ref[...], C), 0.0)
        u_ref[H:H + T, :] = _glu(x_ref[...], C)
        u_ref[H + T:, :] = _glu(next_ref[...], C)
        _fill_shifted(u_ref, us_ref)
        dg_acc = jnp.zeros((1, C), F32)
        dbeta_acc = jnp.zeros((1, C), F32)
        db_acc = jnp.zeros((1, C), F32)
        for c0 in range(0, TE, CH):
            y = jnp.broadcast_to(b_ref[...], (CH, C))
            for k in range(K):
                y = y + w_ref[k:k + 1, :] * _rows_at(u_ref, us_ref, c0 + H - (K - 1) + k, CH)
            mu = jnp.mean(y, axis=-1, keepdims=True)
            yc = y - mu
            rstd = lax.rsqrt(jnp.mean(yc * yc, axis=-1, keepdims=True) + EPS)
            yh = yc * rstd
            ln = yh * g_ref[...] + beta_ref[...]
            s = _sigmoid(ln)
            dout = d_ref[c0:c0 + CH, :] if c0 < T else jnp.where(last, 0.0, dnext_ref[c0 - T:c0 - T + CH, :])
            dln = dout * (s * (1.0 + ln * (1.0 - s)))
            dyh = dln * g_ref[...]
            dy = rstd * (dyh - jnp.mean(dyh, axis=-1, keepdims=True)
                         - yh * jnp.mean(dyh * yh, axis=-1, keepdims=True))
            dy_ref[c0:c0 + CH, :] = dy
            if c0 < T:
                dg_acc = dg_acc + jnp.sum(dln * yh, axis=0, keepdims=True)
                dbeta_acc = dbeta_acc + jnp.sum(dln, axis=0, keepdims=True)
                db_acc = db_acc + jnp.sum(dy, axis=0, keepdims=True)
        dg_ref[...] += dg_acc
        dbeta_ref[...] += dbeta_acc
        db_ref[...] += db_acc
        _fill_shifted(dy_ref, dys_ref)
        for k in range(K):
            dwt_ref[k:k + 1, :] += jnp.sum(
                dy_ref[0:T, :] * _rows_at(u_ref, us_ref, H - (K - 1) + k, T), axis=0, keepdims=True)
        for c0 in range(0, T, CH):
            du = jnp.zeros((CH, C), F32)
            for k in range(K):
                du = du + w_ref[k:k + 1, :] * _rows_at(dy_ref, dys_ref, c0 + (K - 1) - k, CH)
            x = x_ref[c0:c0 + CH, :]
            sg = _sigmoid(x[:, C:])
            dca_ref[c0:c0 + CH, :] = (du * sg).astype(dca_ref.dtype)
            dcg_ref[c0:c0 + CH, :] = (du * x[:, :C] * sg * (1.0 - sg)).astype(dcg_ref.dtype)

    nh = L // H
    vec = pl.BlockSpec((1, C), lambda i: (0, 0))
    row = pl.BlockSpec((T, C), lambda i: (i, 0))
    return _call_with_job(
        body, job, name=name, grid=(nt,),
        in_specs=[pl.BlockSpec((T, C2), lambda i: (i, 0)),
                  pl.BlockSpec((H, C2), lambda i: (jnp.maximum(i * (T // H) - 1, 0), 0)),
                  pl.BlockSpec((H, C2), lambda i: (jnp.minimum((i + 1) * (T // H), nh - 1), 0)),
                  pl.BlockSpec((T, C), lambda i: (i, 1)),
                  pl.BlockSpec((H, C), lambda i: (jnp.minimum((i + 1) * (T // H), nh - 1), 1)),
                  pl.BlockSpec((K, C), lambda i: (0, 0)), vec, vec, vec],
        out_specs=[row, row, pl.BlockSpec((H, C), lambda i: (0, 0)), vec, vec, vec],
        out_shape=[jax.ShapeDtypeStruct((L, C), BF16), jax.ShapeDtypeStruct((L, C), BF16),
                   jax.ShapeDtypeStruct((H, C), F32), jax.ShapeDtypeStruct((1, C), F32),
                   jax.ShapeDtypeStruct((1, C), F32), jax.ShapeDtypeStruct((1, C), F32)],
        scratch_shapes=[pltpu.VMEM((T + 2 * H, C), F32), pltpu.VMEM((7, TE + SHIFT_TAIL, C), F32),
                        pltpu.VMEM((TE, C), F32), pltpu.VMEM((7, T + SHIFT_TAIL, C), F32)],
        args=(cacg, cacg, cacg, dmixed, dmixed, dw_w, dw_b, ln_g, ln_b))


def _local_step(h0, target, n_meta, seq, norms, conv_p, wts, final_g, gather_behind, reducer):
    mix_g, ffn_g = norms
    dw_w, dw_b, ln_g, ln_b = conv_p
    depth = mix_g.shape[0]
    C = dw_b.shape[-1]
    sbw = (wts["w_in"][0].shape[-1] - 2 * C) // 3
    assert sbw == C, "the mixer halves must have equal width"
    row = lambda a, i: a[i][None, :]

    h = h0
    saved = []
    for i in range(depth):
        hn, proj_qkv = _norm_in_proj(h, row(mix_g, i), wts["w_in"][i], 3 * sbw, f"in_qkv_{i}")
        cacg = _mm_nn([(hn, wts["w_in"][i], 0, 0)], F32, f"in_conv_{i}", cols=(3 * sbw, 2 * C))
        def hosting(kind):
            job, keys = gather_behind.get((kind, i), (None, ()))

            def sink(arrived):
                for (wname, wl), arr in zip(keys, arrived):
                    wts[wname][wl] = arr

            return job, sink

        job, sink = hosting("attn")
        attn, arrived = _attn_fwd(proj_qkv, sbw, f"attn_fwd_{i}", job)
        sink(arrived)
        job, sink = hosting("conv")
        conv, arrived = _conv_fwd(cacg, dw_w[i], row(dw_b, i), row(ln_g, i), row(ln_b, i), f"conv_fwd_{i}", job)
        sink(arrived)
        h_mid = _mm_nn([(attn, wts["w_out"][i], 0, 0), (conv, wts["w_out"][i], 0, 1)], F32, f"out_proj_{i}",
                       residual=h)
        job, sink = hosting("ffn")
        (hn2, g, u, act), arrived = _ffn_up(h_mid, row(ffn_g, i), wts["w_gate_t"][i], wts["w_up_t"][i],
                                            f"ffn_up_{i}", job)
        sink(arrived)
        job, sink = hosting("down")
        h_out = _mm_nn([(act, wts["w_down"][i], 0, 0)], F32, f"down_{i}", residual=h_mid, job=job)
        if job is not None:
            h_out, arrived = h_out
            sink(arrived)
        saved.append((h, hn, proj_qkv, cacg, attn, conv, h_mid, hn2, g, u, act))
        h = h_out

    loss, dh, d_final_g = _loss_head(h, final_g[None, :], target, n_meta, seq, "loss_head")

    grads = {k: [None] * depth for k in ("mix_g", "ffn_g", "dw_w", "dw_b", "ln_g", "ln_b")}
    for i in reversed(range(depth)):
        h_in, hn, proj_qkv, cacg, attn, conv, h_mid, hn2, g, u, act = saved[i]
        big = {}
        dg, du = _ffn_down_bwd(dh, wts["w_down"][i], g, u, f"ffn_down_bwd_{i}")
        big["w_down"] = _mm_tn(act, dh, f"dw_down_{i}", col_sharded=False)
        dhn2 = _mm_nn([(dg, wts["w_gate_t"][i], 0, 0), (du, wts["w_up_t"][i], 0, 0)], F32, f"d_hn2_{i}")
        big["w_gate_t"] = _mm_tn(dg, hn2, f"dw_gate_{i}", col_sharded=False)
        big["w_up_t"] = _mm_tn(du, hn2, f"dw_up_{i}", col_sharded=False)
        dh, d_ffn = _rmsnorm_bwd(h_mid, row(ffn_g, i), dhn2, dh, f"ffn_norm_bwd_{i}")
        dmixed = _mm_nt([(dh, wts["w_out"][i], 0, 0)], F32, f"d_mixed_{i}")
        big["w_out"] = jnp.concatenate([_mm_tn(attn, dh, f"dw_out_attn_{i}", col_sharded=False, chips=2),
                                        _mm_tn(conv, dh, f"dw_out_conv_{i}", col_sharded=False, chips=2)], axis=0)
        sib_job, sib_sink = reducer.to_sibling(i, big)
        x_job, x_sink = reducer.take()
        (dq, dk, dv), arrived = _attn_bwd(proj_qkv, attn, dmixed, sbw, f"attn_bwd_{i}",
                                          _merge_jobs([sib_job, x_job]))
        sib_sink(arrived[:len(sib_job.out_shapes)])
        x_sink(arrived[len(sib_job.out_shapes):])
        x_job, x_sink = reducer.take()
        (dca, dcg, d_dw, d_b, d_lg, d_lb), arrived = _conv_bwd(
            cacg, dmixed, dw_w[i], row(dw_b, i), row(ln_g, i), row(ln_b, i), f"conv_bwd_{i}", x_job)
        x_sink(arrived)
        grads["dw_w"][i] = d_dw[:dw_w.shape[1]]
        grads["dw_b"][i], grads["ln_g"][i], grads["ln_b"][i] = d_b[0], d_lg[0], d_lb[0]
        dproj = jnp.concatenate([dq, dk, dv, dca, dcg], axis=1)
        dhn = _mm_nt([(dproj, wts["w_in"][i], 0, 0)], F32, f"d_hn_{i}")
        reducer.ready(i, {"w_in": _mm_tn(hn, dproj, f"dw_in_{i}", col_sharded=True)})
        dh, d_mix = _rmsnorm_bwd(h_in, row(mix_g, i), dhn, dh, f"mix_norm_bwd_{i}")
        grads["mix_g"][i], grads["ffn_g"][i] = d_mix[0], d_ffn[0]
    grads["final_g"] = d_final_g[0]
    return loss, dh, grads


ANY = pl.BlockSpec(memory_space=pl.ANY)


def _position():
    return lax.axis_index("x"), lax.axis_index("y"), lax.axis_index("c")


def _chip_at(x, y, k):
    return (1 - x if k & 2 else x), (1 - y if k & 1 else y)


def _half_rows(ref, half, rows, base=0):
    start = pl.multiple_of(base + half * rows, 8)
    lead = (slice(None),) * (len(ref.shape) - 2)
    return ref.at[(*lead, pl.ds(start, rows), slice(None))]


def _gather_job(fulls, shard_shapes, col_sharded):
    n = len(fulls)

    def tools(f_refs, send_sems, recv_sems):
        def block(wi, chip, half):
            _, R, C = shard_shapes[wi]
            if col_sharded[wi]:
                cols = pl.ds(pl.multiple_of(chip * C, LANES), C)
                return f_refs[wi].at[:, pl.ds(pl.multiple_of(half * (R // 2), 8), R // 2), cols]
            return _half_rows(f_refs[wi], half, R // 2, base=chip * R)

        def copy(wi, slot, blk, to):
            return pltpu.make_async_remote_copy(
                src_ref=blk, dst_ref=blk, send_sem=send_sems.at[6 * wi + slot],
                recv_sem=recv_sems.at[6 * wi + slot], device_id=to, device_id_type=MESH)

        return block, copy

    def start(_, f_refs, send_sems, recv_sems):
        block, copy = tools(f_refs, send_sems, recv_sems)
        x, y, c = _position()
        me = 2 * x + y
        for wi in range(n):
            for k in (1, 2, 3):
                copy(wi, k - 1, block(wi, me, c), (*_chip_at(x, y, k), c)).start()

    def finish(_, f_refs, send_sems, recv_sems):
        block, copy = tools(f_refs, send_sems, recv_sems)
        x, y, c = _position()
        me = 2 * x + y
        for wi in range(n):
            for k in (1, 2, 3):
                landed = block(wi, me ^ k, c)
                copy(wi, k - 1, landed, (x, y, c)).wait_recv()
                copy(wi, 2 + k, landed, (x, y, 1 - c)).start()
        for wi in range(n):
            for k in (1, 2, 3):
                copy(wi, 2 + k, block(wi, me ^ k, 1 - c), (x, y, c)).wait_recv()
        for wi in range(n):
            for k in (1, 2, 3):
                copy(wi, k - 1, block(wi, me, c), (x, y, c)).wait_send()
                copy(wi, 2 + k, block(wi, me ^ k, c), (x, y, c)).wait_send()

    return _CommJob(fulls, [jax.ShapeDtypeStruct(f.shape, f.dtype) for f in fulls], {i: i for i in range(n)},
                    6 * n, start, finish)


def _place_shard(w, layer, chip, col_sharded, dtype, name):
    _, R, C = w.shape
    tr = _pick(R, (256, 352, 128, 48))
    nr = R // tr

    def body(chip_ref, w_ref, o_ref):
        o_ref[...] = w_ref[...].astype(dtype)

    if col_sharded:
        shape = (1, R, N_CHIPS * C)
        out_spec = pl.BlockSpec((None, tr, C), lambda r, chip_ref: (0, r, chip_ref[0]))
    else:
        shape = (1, N_CHIPS * R, C)
        out_spec = pl.BlockSpec((None, tr, C), lambda r, chip_ref: (0, chip_ref[0] * nr + r, 0))
    grid_spec = pltpu.PrefetchScalarGridSpec(
        num_scalar_prefetch=1, grid=(nr,),
        in_specs=[pl.BlockSpec((None, tr, C), lambda r, chip_ref: (layer, r, 0))], out_specs=out_spec)
    return pl.pallas_call(
        body, name=name, grid_spec=grid_spec, out_shape=jax.ShapeDtypeStruct(shape, dtype),
        compiler_params=_params(("parallel",)),
    )(chip, w)


def _sibling_job(grads):
    n = len(grads)

    def copies(g_refs, l_refs, send_sems, recv_sems):
        x, y, c = _position()
        return [pltpu.make_async_remote_copy(
            src_ref=_half_rows(g_refs[wi], 1 - c, grads[wi].shape[1] // 2), dst_ref=l_refs[wi],
            send_sem=send_sems.at[wi], recv_sem=recv_sems.at[wi],
            device_id=(x, y, 1 - c), device_id_type=MESH) for wi in range(n)]

    def start(*refs):
        for cp in copies(*refs):
            cp.start()

    def finish(*refs):
        for cp in copies(*refs):
            cp.wait()

    outs = [jax.ShapeDtypeStruct((g.shape[0], g.shape[1] // 2, g.shape[2]), g.dtype) for g in grads]
    return _CommJob(grads, outs, {}, n, start, finish)


def _chip_sum(g, landed, core, name):
    _, R, C = g.shape
    hr = R // 2
    tr = _pick(hr, (256, 352, 128))
    nr = hr // tr

    def body(c_ref, g_ref, l_ref, o_ref):
        o_ref[...] = (g_ref[...] + l_ref[...]).astype(BF16)

    grid_spec = pltpu.PrefetchScalarGridSpec(
        num_scalar_prefetch=1, grid=(N_CHIPS, nr),
        in_specs=[pl.BlockSpec((None, tr, C), lambda j, r, c_ref: (j, c_ref[0] * nr + r, 0)),
                  pl.BlockSpec((None, tr, C), lambda j, r, c_ref: (j, r, 0))],
        out_specs=pl.BlockSpec((None, tr, C), lambda j, r, c_ref: (j, r, 0)))
    return pl.pallas_call(
        body, name=name, grid_spec=grid_spec, out_shape=jax.ShapeDtypeStruct((N_CHIPS, hr, C), BF16),
        compiler_params=_params(("parallel", "parallel")),
    )(core, g, landed)


def _across_job(parts):
    n = len(parts)

    def copy(p_refs, l_refs, send_sems, recv_sems, wi, k, to):
        x, y, _ = _position()
        me = 2 * x + y
        return pltpu.make_async_remote_copy(
            src_ref=p_refs[wi].at[me ^ k], dst_ref=l_refs[wi].at[me],
            send_sem=send_sems.at[3 * wi + k - 1], recv_sem=recv_sems.at[3 * wi + k - 1],
            device_id=to, device_id_type=MESH)

    def start(p_refs, l_refs, send_sems, recv_sems):
        x, y, c = _position()
        for wi in range(n):
            for k in (1, 2, 3):
                copy(p_refs, l_refs, send_sems, recv_sems, wi, k, (*_chip_at(x, y, k), c)).start()

    def finish(p_refs, l_refs, send_sems, recv_sems):
        x, y, c = _position()
        me = 2 * x + y
        for wi in range(n):
            for k in (1, 2, 3):
                slot = l_refs[wi].at[me ^ k]
                pltpu.make_async_remote_copy(
                    src_ref=slot, dst_ref=slot, send_sem=send_sems.at[3 * wi + k - 1],
                    recv_sem=recv_sems.at[3 * wi + k - 1], device_id=(x, y, c), device_id_type=MESH).wait_recv()
        for wi in range(n):
            for k in (1, 2, 3):
                copy(p_refs, l_refs, send_sems, recv_sems, wi, k, (x, y, c)).wait_send()

    return _CommJob(parts, [jax.ShapeDtypeStruct(p.shape, p.dtype) for p in parts], {}, 3 * n, start, finish)


class _Reducer:
    def __init__(self, core):
        self.core, self.parts, self.across, self.pending = core, {}, {}, []

    def to_sibling(self, layer, big):
        names = list(big)
        flat = [big[k] for k in names]

        def sink(landed):
            for k, g, la in zip(names, flat, landed):
                self.parts[k, layer] = _chip_sum(g, la, self.core, f"chip_sum_{k}_{layer}")
                self.pending.append((k, layer))

        return _sibling_job(flat), sink

    def ready(self, layer, big):
        job, sink = self.to_sibling(layer, big)
        sink(_run_job(job, f"grads_to_sibling_{next(iter(big))}_{layer}"))

    def take(self):
        keys, self.pending = self.pending, []
        if not keys:
            return None, lambda results: None

        def sink(results):
            self.across.update(zip(keys, results))

        return _across_job([self.parts[key] for key in keys]), sink


def _sum_chips(parts, landed, where, layer, depth, prev, name):
    _, hr, C = landed.shape
    tr = _pick(hr, (256, 352, 128))
    nr = hr // tr

    def body(*refs):
        own_ref, slots, o_ref = refs[1], refs[2:2 + N_CHIPS], refs[-1]
        chip = refs[0][0]
        total = None
        for q in range(N_CHIPS):
            term = jnp.where(chip == q, own_ref[...], slots[q][...]).astype(F32)
            total = term if total is None else total + term
        o_ref[...] = total

    def slot_spec(q):
        return pl.BlockSpec((None, tr, C), lambda r, w: (jnp.where(w[0] == q, (q + 1) % N_CHIPS, q), r, 0))

    in_specs = [pl.BlockSpec((None, tr, C), lambda r, w: (w[0], r, 0))] + [slot_spec(q) for q in range(N_CHIPS)]
    args = [where, parts] + [landed] * N_CHIPS
    aliases = {}
    if prev is not None:
        in_specs.append(ANY)
        args.append(prev)
        aliases = {len(args) - 1: 0}
    grid_spec = pltpu.PrefetchScalarGridSpec(
        num_scalar_prefetch=1, grid=(nr,), in_specs=in_specs,
        out_specs=pl.BlockSpec((None, tr, C), lambda r, w: (layer, w[1] * nr + r, 0)))
    return pl.pallas_call(
        body, name=name, grid_spec=grid_spec, out_shape=jax.ShapeDtypeStruct((depth, 2 * hr, C), F32),
        input_output_aliases=aliases, compiler_params=_params(("parallel",)),
    )(*args)


def _rs_join_halves(reduced):
    n = len(reduced)

    def body(*refs):
        o_refs = refs[n:2 * n]
        send_sems, recv_sems = refs[2 * n:]
        x, y, c = _position()
        sent = []
        for wi in range(n):
            hr = reduced[wi].shape[1] // 2
            mine = _half_rows(o_refs[wi], c, hr)
            cp = pltpu.make_async_remote_copy(
                src_ref=mine, dst_ref=mine, send_sem=send_sems.at[wi], recv_sem=recv_sems.at[wi],
                device_id=(x, y, 1 - c), device_id_type=MESH)
            cp.start()
            sent.append(cp)
        for wi in range(n):
            hr = reduced[wi].shape[1] // 2
            theirs = _half_rows(o_refs[wi], 1 - c, hr)
            pltpu.make_async_remote_copy(
                src_ref=theirs, dst_ref=theirs, send_sem=send_sems.at[wi], recv_sem=recv_sems.at[wi],
                device_id=(x, y, c), device_id_type=MESH).wait_recv()
        for cp in sent:
            cp.wait_send()

    return pl.pallas_call(
        body, name="grads_join_halves", out_shape=[jax.ShapeDtypeStruct(r.shape, r.dtype) for r in reduced],
        in_specs=[ANY] * n, out_specs=[ANY] * n, input_output_aliases={i: i for i in range(n)},
        scratch_shapes=[pltpu.SemaphoreType.DMA((n,)), pltpu.SemaphoreType.DMA((n,))],
    )(*reduced)


def _small_allreduce(vec):
    rows = vec.shape[0]

    def body(v_ref, o_ref, land, send_sems, recv_sems):
        x, y, c = _position()
        me = 4 * x + 2 * y + c
        land[0] = v_ref[...]
        sent = []
        for k in range(1, N_DEV):
            to = (1 - x if k & 4 else x, 1 - y if k & 2 else y, 1 - c if k & 1 else c)
            cp = pltpu.make_async_remote_copy(
                src_ref=v_ref, dst_ref=land.at[k], send_sem=send_sems.at[k - 1], recv_sem=recv_sems.at[k - 1],
                device_id=to, device_id_type=MESH)
            cp.start()
            sent.append(cp)
        for cp in sent:
            cp.wait_recv()
        acc = land[me]
        for e in range(1, N_DEV):
            acc = acc + land[me ^ e]
        o_ref[...] = acc
        for cp in sent:
            cp.wait_send()

    vmem = pl.BlockSpec(memory_space=pltpu.VMEM)
    return pl.pallas_call(
        body, name="small_allreduce", out_shape=jax.ShapeDtypeStruct(vec.shape, F32),
        in_specs=[vmem], out_specs=vmem,
        scratch_shapes=[pltpu.VMEM((N_DEV, rows, LANES), F32), pltpu.SemaphoreType.DMA((N_DEV - 1,)),
                        pltpu.SemaphoreType.DMA((N_DEV - 1,))],
    )(vec)


def _adam_math(w, g, m, v):
    m = ADAM_B1 * m + (1.0 - ADAM_B1) * g
    v = ADAM_B2 * v + (1.0 - ADAM_B2) * jnp.square(g)
    m_hat = m / (1.0 - ADAM_B1 ** ADAM_STEP)
    v_hat = v / (1.0 - ADAM_B2 ** ADAM_STEP)
    delta = -ADAM_LR * (m_hat / (jnp.sqrt(v_hat) + ADAM_EPS) + ADAM_WD * w)
    return delta, m, v


def _adam(w, g, m, v, name):
    def body(w_ref, g_ref, m_ref, v_ref, d_ref, nm_ref, nv_ref):
        d_ref[...], nm_ref[...], nv_ref[...] = _adam_math(w_ref[...], g_ref[...], m_ref[...], v_ref[...])

    if w.ndim == 3:
        lyr, R, C = w.shape
        tr = _pick(R, (256, 352, 128))
        blk = pl.BlockSpec((None, tr, C), lambda l, r: (l, r, 0))
        grid, sem = (lyr, R // tr), ("parallel", "parallel")
    else:
        blk = pl.BlockSpec(w.shape, lambda: (0, 0))
        grid, sem = (), None
    return pl.pallas_call(
        body, name=name, grid=grid, in_specs=[blk] * 4, out_specs=[blk] * 3,
        out_shape=[jax.ShapeDtypeStruct(w.shape, F32)] * 3, compiler_params=_params(sem),
    )(w, g, m, v)


def _rows(a, pad_to=8):
    r = a.reshape(-1, LANES)
    extra = (-r.shape[0]) % pad_to
    return jnp.pad(r, ((0, extra), (0, 0))) if extra else r


def _pack(arrays):
    return jnp.concatenate([_rows(a) for a in arrays], axis=0)


def _unpack(slab, shapes):
    out, at = [], 0
    for shp in shapes:
        nrow = math.prod(shp) // LANES
        out.append(slab[at:at + nrow].reshape(shp))
        at += nrow + (-nrow) % 8
    return out


BIG = ("w_in", "w_out", "w_gate_t", "w_up_t", "w_down")
BIG_COL_SHARDED = (True, False, False, False, False)
TRANSPOSED = {"w_gate_t": "w_gate", "w_up_t": "w_up"}


def kernel(x, meta_tokens, mix_norm_g, w_in, conv_dw_w, conv_dw_b, conv_ln_g, conv_ln_b, w_out, ffn_norm_g, w_gate, w_up, w_down, final_norm_g, loss_target, m_meta_tokens, m_mix_norm_g, m_w_in, m_conv_dw_w, m_conv_dw_b, m_conv_ln_g, m_conv_ln_b, m_w_out, m_ffn_norm_g, m_w_gate, m_w_up, m_w_down, m_final_norm_g, v_meta_tokens, v_mix_norm_g, v_w_in, v_conv_dw_w, v_conv_dw_b, v_conv_ln_g, v_conv_ln_b, v_w_out, v_ffn_norm_g, v_w_gate, v_w_up, v_w_down, v_final_norm_g):
    n_meta, seq = meta_tokens.shape[0], x.shape[1]
    D = x.shape[2]
    depth, taps, c_shard = conv_dw_w.shape
    C = conv_dw_b.shape[-1]
    chip = (2 * lax.axis_index("x") + lax.axis_index("y")).astype(jnp.int32)
    core = lax.axis_index("c").astype(jnp.int32).reshape(1)
    chip1 = chip.reshape(1)
    where = jnp.concatenate([chip1, core])
    big_w = dict(w_in=w_in, w_out=w_out, w_gate=w_gate, w_up=w_up, w_down=w_down)
    big_m = dict(w_in=m_w_in, w_out=m_w_out, w_gate=m_w_gate, w_up=m_w_up, w_down=m_w_down)
    big_v = dict(w_in=v_w_in, w_out=v_w_out, w_gate=v_w_gate, w_up=v_w_up, w_down=v_w_down)

    small_shard = _pack([conv_dw_w, meta_tokens])[None]
    to_send = {k: jnp.swapaxes(big_w[TRANSPOSED[k]], 1, 2) if k in TRANSPOSED else big_w[k] for k in BIG}
    col = dict(zip(BIG, BIG_COL_SHARDED))
    wts = {k: [_place_shard(to_send[k], l, chip1, col[k], BF16, f"place_{k}_{l}") for l in range(depth)] for k in BIG}
    small_placed = _place_shard(small_shard, 0, chip1, False, F32, "place_small")

    def gather_job(keys, extra=()):
        arrays = [wts[k][l] for k, l in keys] + list(extra)
        shapes = [(1,) + to_send[k].shape[1:] for k, _ in keys] + [(1,) + small_shard.shape[1:]] * len(extra)
        return _gather_job(arrays, shapes, [col[k] for k, _ in keys] + [False] * len(extra))

    first_keys = [("w_in", 0)]
    *first, small_full = _run_job(gather_job(first_keys, [small_placed]), "gather_first")
    for (k, l), arr in zip(first_keys, first):
        wts[k][l] = arr
    behind_keys = {("attn", 0): [("w_out", 0), ("w_gate_t", 0), ("w_up_t", 0)], ("conv", 0): [("w_down", 0)]}
    for l in range(1, depth):
        behind_keys["ffn", l - 1] = [("w_in", l), ("w_out", l), ("w_gate_t", l)]
        behind_keys["down", l - 1] = [("w_up_t", l)]
        behind_keys["attn", l] = [("w_down", l)]
    gather_behind = {host: (gather_job(keys), keys) for host, keys in behind_keys.items()}

    rows_shard = small_shard.shape[1]
    dw_full, meta_full = [], []
    for j in range(N_CHIPS):
        dwj, mj = _unpack(small_full[0, j * rows_shard:(j + 1) * rows_shard],
                          [conv_dw_w.shape, meta_tokens.shape])
        dw_full.append(dwj)
        meta_full.append(mj)
    dw_w_full = jnp.concatenate(dw_full, axis=2)
    meta = jnp.concatenate(meta_full, axis=1)

    L = n_meta + seq
    Lp = -(-L // QUERY_BLOCK) * QUERY_BLOCK
    h0 = jnp.concatenate([meta, x[0], jnp.zeros((Lp - L, D), F32)], axis=0)
    target = jnp.pad(loss_target[0], ((n_meta, Lp - L), (0, 0)))
    reducer = _Reducer(core)
    loss, dh0, grads = _local_step(h0, target, n_meta, seq, (mix_norm_g, ffn_norm_g),
                                   (dw_w_full, conv_dw_b, conv_ln_g, conv_ln_b), wts, final_norm_g,
                                   gather_behind, reducer)
    loss = lax.psum(loss[0, 0], ("x", "y", "c"))
    grad_x = dh0[n_meta:L][None]
    job, sink = reducer.take()
    sink(_run_job(job, "grads_across_last"))

    reduced = []
    for k in BIG:
        arr = None
        for l in range(depth):
            arr = _sum_chips(reducer.parts[k, l], reducer.across[k, l], where, l, depth, arr, f"sum_chips_{k}_{l}")
        reduced.append(arr)
    big_g = dict(zip(BIG, _rs_join_halves(reduced)))

    small_names = ("mix_g", "ffn_g", "dw_b", "ln_g", "ln_b", "dw_w")
    small = [jnp.stack(grads[k]) for k in small_names] + [grads["final_g"], dh0[:n_meta]]
    small_shapes = [a.shape for a in small]
    g_mix, g_ffn, g_dwb, g_lng, g_lnb, g_dww, g_final, g_meta = _unpack(_small_allreduce(_pack(small)), small_shapes)
    g_dww = lax.dynamic_slice_in_dim(g_dww, chip * c_shard, c_shard, axis=2)
    g_meta = lax.dynamic_slice_in_dim(g_meta, chip * meta_tokens.shape[1], meta_tokens.shape[1], axis=1)

    out_g, out_d, out_m, out_v = {}, {}, {}, {}
    for kk in BIG:
        k = TRANSPOSED.get(kk, kk)
        view = (lambda a: jnp.swapaxes(a, 1, 2)) if kk in TRANSPOSED else (lambda a: a)
        res = _adam(view(big_w[k]), big_g[kk], view(big_m[k]), view(big_v[k]), f"adam_{k}")
        out_g[k] = view(big_g[kk])
        out_d[k], out_m[k], out_v[k] = (view(a) for a in res)
    small_order = ("meta_tokens", "mix_norm_g", "conv_dw_w", "conv_dw_b", "conv_ln_g", "conv_ln_b",
                   "ffn_norm_g", "final_norm_g")
    sw = dict(meta_tokens=meta_tokens, mix_norm_g=mix_norm_g, conv_dw_w=conv_dw_w, conv_dw_b=conv_dw_b,
              conv_ln_g=conv_ln_g, conv_ln_b=conv_ln_b, ffn_norm_g=ffn_norm_g, final_norm_g=final_norm_g)
    sm = dict(meta_tokens=m_meta_tokens, mix_norm_g=m_mix_norm_g, conv_dw_w=m_conv_dw_w, conv_dw_b=m_conv_dw_b,
              conv_ln_g=m_conv_ln_g, conv_ln_b=m_conv_ln_b, ffn_norm_g=m_ffn_norm_g, final_norm_g=m_final_norm_g)
    sv = dict(meta_tokens=v_meta_tokens, mix_norm_g=v_mix_norm_g, conv_dw_w=v_conv_dw_w, conv_dw_b=v_conv_dw_b,
              conv_ln_g=v_conv_ln_g, conv_ln_b=v_conv_ln_b, ffn_norm_g=v_ffn_norm_g, final_norm_g=v_final_norm_g)
    sg = dict(meta_tokens=g_meta, mix_norm_g=g_mix, conv_dw_w=g_dww, conv_dw_b=g_dwb, conv_ln_g=g_lng,
              conv_ln_b=g_lnb, ffn_norm_g=g_ffn, final_norm_g=g_final)
    def slab(d):
        return _pack([d[k] for k in small_order])
    shapes = [sw[k].shape for k in small_order]
    deltas = _adam(slab(sw), slab(sg), slab(sm), slab(sv), "adam_small")
    for res, dst in zip(deltas, (out_d, out_m, out_v)):
        dst.update(zip(small_order, _unpack(res, shapes)))
    out_g.update(sg)

    order = ("meta_tokens", "mix_norm_g", "w_in", "conv_dw_w", "conv_dw_b", "conv_ln_g", "conv_ln_b", "w_out",
             "ffn_norm_g", "w_gate", "w_up", "w_down", "final_norm_g")
    return (loss, grad_x, *[out_g[k] for k in order], *[out_d[k] for k in order],
            *[out_m[k] for k in order], *[out_v[k] for k in order])
```

```python
import functools
import math

import jax
import jax.numpy as jnp
from jax import lax
from jax.experimental import pallas as pl
from jax.experimental.pallas import tpu as pltpu

F32 = jnp.float32
BF16 = jnp.bfloat16
MESH = pl.DeviceIdType.MESH

EPS = 1e-6
QUERY_BLOCK = 128
LANES = 128
HEAD_DIM = 64
LOG_STICK_FLOOR = -40.0
CONV_HALO = 32
N_CHIPS = 4
N_DEV = 8
VMEM_LIMIT = 56 * 1024 * 1024

ADAM_LR = 0.001
ADAM_B1 = 0.9
ADAM_B2 = 0.999
ADAM_EPS = 1e-08
ADAM_WD = 0.01
ADAM_STEP = 10


def _pick(n, prefs):
    for p in prefs:
        if n % p == 0:
            return p
    return n


def _params(sem=None):
    return pltpu.CompilerParams(dimension_semantics=sem, vmem_limit_bytes=VMEM_LIMIT)


def _sigmoid(x):
    return 1.0 / (1.0 + jnp.exp(-x))


def _rmsnorm_bwd(h, g, dy, dh_in, name, job=None):
    L, D = h.shape
    T = _pick(L, (384, 128))

    def body(h_ref, g_ref, dy_ref, dhin_ref, dh_ref, dg_ref):
        x = h_ref[...]
        dyv = dy_ref[...]
        r = lax.rsqrt(jnp.mean(x * x, axis=-1, keepdims=True) + EPS)
        xh = x * r
        dxh = dyv * g_ref[...]
        dh_ref[...] = dhin_ref[...] + r * (dxh - xh * jnp.mean(dxh * xh, axis=-1, keepdims=True))

        @pl.when(pl.program_id(0) == 0)
        def _():
            dg_ref[...] = jnp.zeros_like(dg_ref)

        dg_ref[...] += jnp.sum(dyv * xh, axis=0, keepdims=True)

    row = pl.BlockSpec((T, D), lambda i: (i, 0))
    vec = pl.BlockSpec((1, D), lambda i: (0, 0))
    (dh, dg), arrived = _call_with_job(
        body, job, name=name, grid=(L // T,),
        in_specs=[row, vec, row, row], out_specs=[row, vec],
        out_shape=[jax.ShapeDtypeStruct((L, D), F32), jax.ShapeDtypeStruct((1, D), F32)],
        scratch_shapes=[], args=(h, g, dy, dh_in))
    return (dh, dg) if job is None else (dh, dg, arrived)


def _loss_head(h, g, target, n_meta, seq, name):
    L, D = h.shape
    T = _pick(L, (384, 128))

    def body(h_ref, g_ref, t_ref, loss_ref, dh_ref, dg_ref):
        i = pl.program_id(0)
        x = h_ref[...]
        gv = g_ref[...]
        r = lax.rsqrt(jnp.mean(x * x, axis=-1, keepdims=True) + EPS)
        xh = x * r
        y = xh * gv
        rows = i * T + lax.broadcasted_iota(jnp.int32, (T, 1), 0)
        live = (rows >= n_meta) & (rows < n_meta + seq)
        diff = jnp.where(live, y - t_ref[...], 0.0)
        dyv = diff / D
        dxh = dyv * gv
        dh_ref[...] = r * (dxh - xh * jnp.mean(dxh * xh, axis=-1, keepdims=True))

        @pl.when(i == 0)
        def _():
            dg_ref[...] = jnp.zeros_like(dg_ref)
            loss_ref[...] = jnp.zeros_like(loss_ref)

        dg_ref[...] += jnp.sum(dyv * xh, axis=0, keepdims=True)
        per_row = jnp.mean(diff * diff, axis=-1, keepdims=True)
        loss_ref[...] += 0.5 * jnp.sum(per_row, axis=0, keepdims=True)

    row = pl.BlockSpec((T, D), lambda i: (i, 0))
    vec = pl.BlockSpec((1, D), lambda i: (0, 0))
    one = pl.BlockSpec((1, 1), lambda i: (0, 0))
    return pl.pallas_call(
        body, name=name, grid=(L // T,),
        in_specs=[row, vec, row], out_specs=[one, row, vec],
        out_shape=[jax.ShapeDtypeStruct((1, 1), F32), jax.ShapeDtypeStruct((L, D), F32),
                   jax.ShapeDtypeStruct((1, D), F32)],
        compiler_params=_params(("arbitrary",)),
    )(h, g, target)


def _ffn_tiles(M, F):
    return _pick(M, (352, 384, 128)), _pick(F, (1408, 512, 256, 128))


def _resident(shape):
    return pl.BlockSpec((None,) + tuple(shape[1:]), lambda *_: (0,) * len(shape), pipeline_mode=pl.Buffered(1))


def _normed_rows(h_ref, g_ref, hn_ref, keep_ref):
    @pl.when(pl.program_id(1) == 0)
    def _():
        x = h_ref[...]
        r = lax.rsqrt(jnp.mean(x * x, axis=-1, keepdims=True) + EPS)
        keep_ref[...] = (x * r * g_ref[...]).astype(BF16)
        hn_ref[...] = keep_ref[...]


def _norm_in_proj(h, g, w_in, n_cols, name):
    M, D = h.shape
    tm = _pick(M, (1056, 384, 128))
    tn = _pick(n_cols, (512, 256, 128))

    def body(h_ref, g_ref, w_ref, hn_ref, o_ref, keep_ref):
        _normed_rows(h_ref, g_ref, hn_ref, keep_ref)
        o_ref[...] = jnp.dot(keep_ref[...], w_ref[...], preferred_element_type=F32).astype(o_ref.dtype)

    rows = pl.BlockSpec((tm, D), lambda i, j: (i, 0))
    return pl.pallas_call(
        body, name=name, grid=(M // tm, n_cols // tn),
        in_specs=[rows, pl.BlockSpec((1, D), lambda i, j: (0, 0)), pl.BlockSpec((None, D, tn), lambda i, j: (0, 0, j))],
        out_specs=[rows, pl.BlockSpec((tm, tn), lambda i, j: (i, j))],
        out_shape=[jax.ShapeDtypeStruct((M, D), BF16), jax.ShapeDtypeStruct((M, n_cols), BF16)],
        scratch_shapes=[pltpu.VMEM((tm, D), BF16)],
        compiler_params=_params(("parallel", "arbitrary")),
    )(h, g, w_in)


def _ffn_up(h, norm_g, w_gate_t, w_up_t, name, job=None):
    M, D = h.shape
    F = w_gate_t.shape[1]
    tm, tf = _ffn_tiles(M, F)
    nt = (((1,), (1,)), ((), ()))

    def body(h_ref, ng_ref, wg_ref, wu_ref, hn_ref, g_ref, u_ref, a_ref):
        x = h_ref[...]
        r = lax.rsqrt(jnp.mean(x * x, axis=-1, keepdims=True) + EPS)
        hv = (x * r * ng_ref[...]).astype(BF16)
        hn_ref[...] = hv
        for c in range(0, F, tf):
            gv = lax.dot_general(hv, wg_ref[c:c + tf, :], nt, preferred_element_type=F32)
            uv = lax.dot_general(hv, wu_ref[c:c + tf, :], nt, preferred_element_type=F32)
            g_ref[:, c:c + tf] = gv
            u_ref[:, c:c + tf] = uv
            a_ref[:, c:c + tf] = (gv * _sigmoid(gv) * uv).astype(a_ref.dtype)

    rows = pl.BlockSpec((tm, D), lambda i: (i, 0))
    wide = pl.BlockSpec((tm, F), lambda i: (i, 0))
    return _call_with_job(
        body, job, name=name, grid=(M // tm,),
        in_specs=[rows, pl.BlockSpec((1, D), lambda i: (0, 0)), _resident(w_gate_t.shape), _resident(w_up_t.shape)],
        out_specs=[rows, wide, wide, wide],
        out_shape=[jax.ShapeDtypeStruct((M, D), BF16), jax.ShapeDtypeStruct((M, F), F32),
                   jax.ShapeDtypeStruct((M, F), F32), jax.ShapeDtypeStruct((M, F), BF16)],
        scratch_shapes=[], args=(h, norm_g, w_gate_t, w_up_t))


def _ffn_down_bwd(dh, w_down, g, u, name):
    M, D = dh.shape
    F = g.shape[1]
    tm, tf = _ffn_tiles(M, F)

    def body(d_ref, w_ref, g_ref, u_ref, dg_ref, du_ref):
        dhv = d_ref[...].astype(BF16)
        for c in range(0, F, tf):
            dv = lax.dot_general(dhv, w_ref[c:c + tf, :], (((1,), (1,)), ((), ())), preferred_element_type=F32)
            gv = g_ref[:, c:c + tf]
            s = _sigmoid(gv)
            du_ref[:, c:c + tf] = (dv * (gv * s)).astype(du_ref.dtype)
            dg_ref[:, c:c + tf] = (dv * u_ref[:, c:c + tf] * (s * (1.0 + gv * (1.0 - s)))).astype(dg_ref.dtype)

    wide = pl.BlockSpec((tm, F), lambda i: (i, 0))
    return pl.pallas_call(
        body, name=name, grid=(M // tm,),
        in_specs=[pl.BlockSpec((tm, D), lambda i: (i, 0)), _resident(w_down.shape), wide, wide],
        out_specs=[wide, wide],
        out_shape=[jax.ShapeDtypeStruct((M, F), BF16), jax.ShapeDtypeStruct((M, F), BF16)],
        compiler_params=_params(("parallel",)),
    )(dh, w_down, g, u)


def _mm_nn(pairs, out_dtype, name, residual=None, cols=None, job=None):
    M = pairs[0][0].shape[0]
    col0, N = cols if cols is not None else (0, pairs[0][1].shape[-1])
    tm = _pick(M, (1056, 384, 128))
    shallow = sum(p[0].shape[1] for p in pairs) <= 1024
    tn = _pick(math.gcd(N, col0) if col0 else N, ((1024,) if shallow else ()) + (640, 512, 256, 128))
    jb = col0 // tn
    n = len(pairs)

    def body(*refs):
        a_refs, w_refs = refs[:n], refs[n:2 * n]
        o_ref = refs[-1]
        acc = None
        for a_ref, w_ref in zip(a_refs, w_refs):
            d = jnp.dot(a_ref[...].astype(BF16), w_ref[...], preferred_element_type=F32)
            acc = d if acc is None else acc + d
        if residual is not None:
            acc = acc + refs[2 * n][...]
        o_ref[...] = acc.astype(o_ref.dtype)

    in_specs = [pl.BlockSpec((tm, a.shape[1]), lambda i, j: (i, 0)) for a, _, _, _ in pairs]
    for a, _, layer, kblk in pairs:
        in_specs.append(pl.BlockSpec((None, a.shape[1], tn), functools.partial(lambda i, j, l, kb: (l, kb, j + jb), l=layer, kb=kblk)))
    args = [p[0] for p in pairs] + [p[1] for p in pairs]
    if residual is not None:
        in_specs.append(pl.BlockSpec((tm, tn), lambda i, j: (i, j)))
        args.append(residual)
    (out,), arrived = _call_with_job(
        body, job, name=name, grid=(M // tm, N // tn), in_specs=in_specs,
        out_specs=[pl.BlockSpec((tm, tn), lambda i, j: (i, j))],
        out_shape=[jax.ShapeDtypeStruct((M, N), out_dtype)], scratch_shapes=[], args=args)
    return out if job is None else (out, arrived)


def _mm_nt(pairs, out_dtype, name):
    M = pairs[0][0].shape[0]
    K = pairs[0][1].shape[1]
    tm = _pick(M, (1056, 384, 128))
    shallow = sum(p[0].shape[1] for p in pairs) <= 1024
    tk = _pick(K, ((1024,) if shallow else ()) + (512, 1408, 256, 128))
    n = len(pairs)

    def body(*refs):
        d_refs, w_refs = refs[:n], refs[n:2 * n]
        o_ref = refs[-1]
        acc = None
        for d_ref, w_ref in zip(d_refs, w_refs):
            d = lax.dot_general(d_ref[...].astype(BF16), w_ref[...], (((1,), (1,)), ((), ())),
                                preferred_element_type=F32)
            acc = d if acc is None else acc + d
        o_ref[...] = acc.astype(o_ref.dtype)

    in_specs = [pl.BlockSpec((tm, d.shape[1]), lambda i, j: (i, 0)) for d, _, _, _ in pairs]
    for d, _, layer, cblk in pairs:
        in_specs.append(pl.BlockSpec((None, tk, d.shape[1]), functools.partial(lambda i, j, l, cb: (l, j, cb), l=layer, cb=cblk)))
    args = [p[0] for p in pairs] + [p[1] for p in pairs]
    return pl.pallas_call(
        body, name=name, grid=(M // tm, K // tk), in_specs=in_specs,
        out_specs=pl.BlockSpec((tm, tk), lambda i, j: (i, j)),
        out_shape=jax.ShapeDtypeStruct((M, K), out_dtype),
        compiler_params=_params(("parallel", "parallel")),
    )(*args)


def _mm_tn(a, b, name, col_sharded, chips=N_CHIPS):
    M, K = a.shape
    N = b.shape[1]
    tm = _pick(M, (1056, 384, 128))
    tk = _pick(K, (1024, 1408, 512, 256, 128))
    tn = N // N_CHIPS if col_sharded else _pick(N, (512, 128))

    def body(a_ref, b_ref, o_ref):
        @pl.when(pl.program_id(2) == 0)
        def _():
            o_ref[...] = jnp.zeros_like(o_ref)

        o_ref[...] += lax.dot_general(a_ref[...].astype(BF16), b_ref[...].astype(BF16),
                                      (((0,), (0,)), ((), ())), preferred_element_type=F32)

    if col_sharded:
        out_shape = jax.ShapeDtypeStruct((N_CHIPS, K, tn), F32)
        out_spec = pl.BlockSpec((None, tk, tn), lambda k, j, m: (j, k, 0))
    else:
        out_shape = jax.ShapeDtypeStruct((K, N), F32)
        out_spec = pl.BlockSpec((tk, tn), lambda k, j, m: (k, j))
    out = pl.pallas_call(
        body, name=name, grid=(K // tk, N // tn, M // tm),
        in_specs=[pl.BlockSpec((tm, tk), lambda k, j, m: (m, k)), pl.BlockSpec((tm, tn), lambda k, j, m: (m, j))],
        out_specs=out_spec, out_shape=out_shape,
        compiler_params=_params(("parallel", "parallel", "arbitrary")),
    )(a, b)
    return out if col_sharded else out.reshape(chips, K // chips, N)


def _stack_heads(x, scale=None):
    lane = lax.broadcasted_iota(jnp.int32, x.shape, 1)
    zero = jnp.zeros_like(x)
    lo = jnp.where(lane < HEAD_DIM, x, zero)
    hi = jnp.where(lane < HEAD_DIM, zero, x)
    out = jnp.concatenate([lo, hi], axis=0)
    return out if scale is None else out * scale


def _unstack_heads(x2):
    qb = x2.shape[0] // 2
    lane = lax.broadcasted_iota(jnp.int32, (qb, LANES), 1)
    return jnp.where(lane < HEAD_DIM, x2[:qb], x2[qb:])


def _dot2(x, m):
    xh = x.astype(BF16)
    xl = (x - xh.astype(F32)).astype(BF16)
    return jnp.dot(xh, m, preferred_element_type=F32) + jnp.dot(xl, m, preferred_element_type=F32)


ATTN_CHUNK = 32


def _in_lockstep(staged):
    waiting, live = list(staged), []
    while waiting or live:
        if waiting:
            live.append(waiting.pop(0))
        for gen in list(live):
            if next(gen, StopIteration) is StopIteration:
                live.remove(gen)


def _row_chunks():
    return [slice(r, r + ATTN_CHUNK) for r in range(0, 2 * QUERY_BLOCK, ATTN_CHUNK)]


def _chunk_valid(rows, older):
    shape = (rows.stop - rows.start, older + QUERY_BLOCK)
    r = (rows.start + lax.broadcasted_iota(jnp.int32, shape, 0)) & (QUERY_BLOCK - 1)
    return lax.broadcasted_iota(jnp.int32, shape, 1) < r + older


def _split_to(x, hi_ref, lo_ref, rows):
    xh = x.astype(BF16)
    hi_ref[rows, :] = xh
    lo_ref[rows, :] = (x - xh.astype(F32)).astype(BF16)


def _triangle(keys, strict):
    r = lax.broadcasted_iota(jnp.int32, (keys, keys), 0)
    c = lax.broadcasted_iota(jnp.int32, (keys, keys), 1)
    return jnp.where((r > c) if strict else (r >= c), 1.0, 0.0).astype(BF16)


def _fill_attn_consts(tri_ref, mask_ref):
    n = 2 * QUERY_BLOCK
    tri_ref[0] = _triangle(n, True)
    tri_ref[1] = _triangle(n, False)
    mask_ref[...] = jnp.where(_chunk_valid(slice(0, n), QUERY_BLOCK), 1.0, 0.0)


def _valid(mask_ref, rows, older):
    cols = slice(0, 2 * QUERY_BLOCK) if older else slice(QUERY_BLOCK, 2 * QUERY_BLOCK)
    return mask_ref[rows, cols] > 0.5


def _row_total(first):
    lane = lax.broadcasted_iota(jnp.int32, first.shape, 1)
    total = jnp.sum(jnp.where(lane == 0, first, 0.0), axis=1, keepdims=True)
    return jnp.broadcast_to(total, first.shape)


def _pairs_per_step(n_pairs):
    return 2 if n_pairs % 2 == 0 else 1


def _lanes(g):
    return slice(g * LANES, (g + 1) * LANES)


def _sweep_older(i, step, carry_ref, first):
    def cond(state):
        n, live = state
        return jnp.logical_and(n < i, live)

    def older(state):
        n, _ = state
        step(i - 1 - n, False)
        return n + 1, jnp.max(carry_ref[...]) > LOG_STICK_FLOOR

    lax.while_loop(cond, older, (first, jnp.max(carry_ref[...]) > LOG_STICK_FLOOR))


class _CommJob:
    def __init__(self, inputs, out_shapes, aliases, n_sems, start, finish):
        self.inputs, self.out_shapes, self.aliases, self.n_sems = list(inputs), list(out_shapes), aliases, n_sems
        self.start, self.finish = start, finish


def _merge_jobs(jobs):
    jobs = [j for j in jobs if j is not None]
    if len(jobs) <= 1:
        return jobs[0] if jobs else None
    spans, aliases = [], {}
    i0 = o0 = s0 = 0
    for j in jobs:
        spans.append((i0, o0, s0))
        aliases.update({i0 + a: o0 + b for a, b in j.aliases.items()})
        i0, o0, s0 = i0 + len(j.inputs), o0 + len(j.out_shapes), s0 + j.n_sems

    def run(which):
        def go(ins, outs, send_sems, recv_sems):
            for j, (i, o, s) in zip(jobs, spans):
                getattr(j, which)(ins[i:i + len(j.inputs)], outs[o:o + len(j.out_shapes)],
                                  send_sems.at[pl.ds(s, j.n_sems)], recv_sems.at[pl.ds(s, j.n_sems)])
        return go

    return _CommJob([a for j in jobs for a in j.inputs], [s for j in jobs for s in j.out_shapes], aliases, s0,
                    run("start"), run("finish"))


def _call_with_job(core_body, job, *, name, grid, in_specs, out_specs, out_shape, scratch_shapes, args):
    sem = ("arbitrary",) * len(grid)
    if job is None:
        res = pl.pallas_call(core_body, name=name, grid=grid, in_specs=in_specs, out_specs=out_specs,
                             out_shape=out_shape, scratch_shapes=scratch_shapes, compiler_params=_params(sem))(*args)
        return list(res), []
    n_in, n_out, n_scr = len(in_specs), len(out_specs), len(scratch_shapes)
    m_in, m_out = len(job.inputs), len(job.out_shapes)

    def body(*refs):
        at = 0
        parts = []
        for count in (n_in, m_in, n_out, m_out, n_scr, 2):
            parts.append(refs[at:at + count])
            at += count
        ins, job_in, outs, job_outs, scratch, (send_sems, recv_sems) = parts
        first = functools.reduce(jnp.logical_and, [pl.program_id(a) == 0 for a in range(len(grid))])
        last = functools.reduce(jnp.logical_and, [pl.program_id(a) == grid[a] - 1 for a in range(len(grid))])

        @pl.when(first)
        def _():
            job.start(job_in, job_outs, send_sems, recv_sems)

        core_body(*ins, *outs, *scratch)

        @pl.when(last)
        def _():
            job.finish(job_in, job_outs, send_sems, recv_sems)

    res = pl.pallas_call(
        body, name=name, grid=grid, in_specs=list(in_specs) + [ANY] * m_in, out_specs=list(out_specs) + [ANY] * m_out,
        out_shape=list(out_shape) + job.out_shapes,
        input_output_aliases={n_in + a: n_out + b for a, b in job.aliases.items()},
        scratch_shapes=list(scratch_shapes) + [pltpu.SemaphoreType.DMA((job.n_sems,)), pltpu.SemaphoreType.DMA((job.n_sems,))],
        compiler_params=_params(sem),
    )(*args, *job.inputs)
    return list(res[:n_out]), list(res[n_out:])


def _run_job(job, name):
    m_in, m_out = len(job.inputs), len(job.out_shapes)

    def body(*refs):
        job_in, job_outs = refs[:m_in], refs[m_in:m_in + m_out]
        send_sems, recv_sems = refs[m_in + m_out:]
        job.start(job_in, job_outs, send_sems, recv_sems)
        job.finish(job_in, job_outs, send_sems, recv_sems)

    return list(pl.pallas_call(
        body, name=name, in_specs=[ANY] * m_in, out_specs=[ANY] * m_out, out_shape=job.out_shapes,
        input_output_aliases=dict(job.aliases),
        scratch_shapes=[pltpu.SemaphoreType.DMA((job.n_sems,)), pltpu.SemaphoreType.DMA((job.n_sems,))],
    )(*job.inputs))


def _attn_fwd(qkv, sb_width, name, job=None):
    L = qkv.shape[0]
    QB = QUERY_BLOCK
    nb = L // QB
    n_pairs = sb_width // LANES
    G = _pairs_per_step(n_pairs)
    W = G * LANES
    scale = 1.0 / math.sqrt(HEAD_DIM)

    def body(q_ref, k_ref, v_ref, o_ref, acc_ref, carry_ref, f32_ref, bf16_ref, tri_ref, mask_ref):
        i = pl.program_id(1)

        @pl.when(i == 0)
        def _():
            _fill_attn_consts(tri_ref, mask_ref)

        q2 = [_stack_heads(q_ref[:, _lanes(g)], scale) for g in range(G)]
        acc_ref[...] = jnp.zeros_like(acc_ref)
        carry_ref[...] = jnp.zeros_like(carry_ref)

        def step(g, j, older, masked):
            n = older + QB
            start = pl.multiple_of(j * QB, QB)
            kb = k_ref[pl.ds(start, n), _lanes(g)]
            vb = v_ref[pl.ds(start, n), _lanes(g)]
            zs, as_, bs = (f32_ref.at[g, t, :, :n] for t in range(3))
            hi, lo = (bf16_ref.at[g, t, :, :n] for t in range(2))
            zs[...] = lax.dot_general(q2[g], kb, (((1,), (1,)), ((), ())), preferred_element_type=F32)
            yield
            for rows in _row_chunks():
                z = zs[rows, :]
                sp = jnp.log(1.0 + jnp.exp(-jnp.abs(z)))
                b = jnp.minimum(-z, 0.0) - sp
                if masked:
                    b = jnp.where(_valid(mask_ref, rows, older),b, 0.0)
                as_[rows, :] = jnp.minimum(z, 0.0) - sp
                bs[rows, :] = b
                _split_to(b, hi, lo, rows)
            yield
            tri = tri_ref[0, :n, :n]
            zs[...] = (jnp.dot(hi[...], tri, preferred_element_type=F32)
                       + jnp.dot(lo[...], tri, preferred_element_type=F32))
            yield
            for rows in _row_chunks():
                excl = zs[rows, :]
                total = _row_total(excl[:, :LANES] + bs[rows, :LANES])
                if not masked:
                    excl = excl + jnp.tile(carry_ref[g, rows, :], (1, n // LANES))
                carry_ref[g, rows, :] += total
                w = jnp.exp(as_[rows, :] + excl)
                if masked:
                    w = jnp.where(_valid(mask_ref, rows, older),w, 0.0)
                _split_to(w, hi, lo, rows)
            yield
            acc_ref[g] += (jnp.dot(hi[...], vb, preferred_element_type=F32)
                           + jnp.dot(lo[...], vb, preferred_element_type=F32))

        @pl.when(i == 0)
        def _():
            _in_lockstep([step(g, 0, 0, True) for g in range(G)])

        @pl.when(i > 0)
        def _():
            _in_lockstep([step(g, i - 1, QB, True) for g in range(G)])
            for g in range(G):
                _sweep_older(i, lambda j, _, g=g: _in_lockstep([step(g, j, 0, False)]), carry_ref.at[g], 1)

        for g in range(G):
            o_ref[:, _lanes(g)] = _unstack_heads(acc_ref[g])

    n_steps = n_pairs // G
    (out,), job_out = _call_with_job(
        body, job, name=name, grid=(n_steps, nb),
        in_specs=[pl.BlockSpec((QB, W), lambda p, i: (i, p)),
                  pl.BlockSpec((L, W), lambda p, i: (0, n_steps + p)),
                  pl.BlockSpec((L, W), lambda p, i: (0, 2 * n_steps + p))],
        out_specs=[pl.BlockSpec((QB, W), lambda p, i: (i, p))],
        out_shape=[jax.ShapeDtypeStruct((L, sb_width), F32)],
        scratch_shapes=[pltpu.VMEM((G, 2 * QB, LANES), F32), pltpu.VMEM((G, 2 * QB, LANES), F32),
                        pltpu.VMEM((G, 3, 2 * QB, 2 * QB), F32), pltpu.VMEM((G, 2, 2 * QB, 2 * QB), BF16),
                        pltpu.VMEM((2, 2 * QB, 2 * QB), BF16), pltpu.VMEM((2 * QB, 2 * QB), F32)],
        args=(qkv, qkv, qkv))
    return out, job_out


def _attn_bwd(qkv, o, dmixed, sb_width, name, job=None):
    L = qkv.shape[0]
    QB = QUERY_BLOCK
    nb = L // QB
    n_pairs = sb_width // LANES
    G = _pairs_per_step(n_pairs)
    W = G * LANES
    scale = 1.0 / math.sqrt(HEAD_DIM)

    def body(q_ref, k_ref, v_ref, o_ref, do_ref, dq_ref, dk_ref, dv_ref,
             dq_acc, dk_acc, dv_acc, ce_ref, cr_ref, dtot_ref, f32_ref, bf16_ref, tri_ref, mask_ref):
        i = pl.program_id(1)

        @pl.when(i == 0)
        def _():
            dk_acc[...] = jnp.zeros_like(dk_acc)
            dv_acc[...] = jnp.zeros_like(dv_acc)
            _fill_attn_consts(tri_ref, mask_ref)

        q2s = [_stack_heads(q_ref[:, _lanes(g)], scale) for g in range(G)]
        do2s = [_stack_heads(do_ref[:, _lanes(g)].astype(BF16)) for g in range(G)]
        ones = jnp.ones((LANES, LANES), BF16)
        for g in range(G):
            ov = o_ref[:, _lanes(g)]
            dtot_ref[g] = _dot2(do2s[g].astype(F32) * jnp.concatenate([ov, ov], axis=0), ones)
        dq_acc[...] = jnp.zeros_like(dq_acc)
        ce_ref[...] = jnp.zeros_like(ce_ref)
        cr_ref[...] = jnp.zeros_like(cr_ref)

        def step(g, j, older, masked):
            n = older + QB
            wide = n // LANES
            q2, do2 = q2s[g], do2s[g]
            start = pl.multiple_of(j * QB, QB)
            kb = k_ref[pl.ds(start, n), _lanes(g)]
            vb = v_ref[pl.ds(start, n), _lanes(g)]
            zs, as_, bs, betas, gs = (f32_ref.at[g, t, :, :n] for t in range(5))
            hi, lo, wb = (bf16_ref.at[g, t, :, :n] for t in range(3))
            nt = (((1,), (1,)), ((), ()))
            zs[...] = lax.dot_general(q2, kb, nt, preferred_element_type=F32)
            gs[...] = lax.dot_general(do2, vb, nt, preferred_element_type=F32)
            yield
            for rows in _row_chunks():
                z = zs[rows, :]
                e = jnp.exp(-jnp.abs(z))
                sp = jnp.log(1.0 + e)
                b = jnp.minimum(-z, 0.0) - sp
                if masked:
                    b = jnp.where(_valid(mask_ref, rows, older),b, 0.0)
                rinv = 1.0 / (1.0 + e)
                as_[rows, :] = jnp.minimum(z, 0.0) - sp
                bs[rows, :] = b
                betas[rows, :] = jnp.where(z >= 0.0, rinv, e * rinv)
                _split_to(b, hi, lo, rows)
            yield
            tri = tri_ref[0, :n, :n]
            zs[...] = (jnp.dot(hi[...], tri, preferred_element_type=F32)
                       + jnp.dot(lo[...], tri, preferred_element_type=F32))
            yield
            for rows in _row_chunks():
                excl = zs[rows, :]
                total = _row_total(excl[:, :LANES] + bs[rows, :LANES])
                if not masked:
                    excl = excl + jnp.tile(ce_ref[g, rows, :], (1, wide))
                ce_ref[g, rows, :] += total
                w = jnp.exp(as_[rows, :] + excl)
                if masked:
                    w = jnp.where(_valid(mask_ref, rows, older),w, 0.0)
                wb[rows, :] = w.astype(BF16)
                gw = w * gs[rows, :]
                gs[rows, :] = gw
                _split_to(gw, hi, lo, rows)
            yield
            tri = tri_ref[1, :n, :n]
            zs[...] = (jnp.dot(hi[...], tri, preferred_element_type=F32)
                       + jnp.dot(lo[...], tri, preferred_element_type=F32))
            yield
            for rows in _row_chunks():
                rinc = zs[rows, :]
                total = _row_total(rinc[:, :LANES])
                if not masked:
                    rinc = rinc + jnp.tile(cr_ref[g, rows, :], (1, wide))
                cr_ref[g, rows, :] += total
                beta = betas[rows, :]
                dz = gs[rows, :] * (1.0 - beta) - beta * (jnp.tile(dtot_ref[g, rows, :], (1, wide)) - rinc)
                if masked:
                    dz = jnp.where(_valid(mask_ref, rows, older),dz, 0.0)
                hi[rows, :] = dz.astype(BF16)
            yield
            dzb = hi[...]
            dq_acc[g] += jnp.dot(dzb, kb, preferred_element_type=F32)
            dk_acc[pl.ds(start, n), _lanes(g)] += lax.dot_general(
                dzb, q2, (((0,), (0,)), ((), ())), preferred_element_type=F32)
            dv_acc[pl.ds(start, n), _lanes(g)] += lax.dot_general(
                wb[...], do2, (((0,), (0,)), ((), ())), preferred_element_type=F32)

        @pl.when(i == 0)
        def _():
            _in_lockstep([step(g, 0, 0, True) for g in range(G)])

        @pl.when(i > 0)
        def _():
            _in_lockstep([step(g, i - 1, QB, True) for g in range(G)])
            for g in range(G):
                _sweep_older(i, lambda j, _, g=g: _in_lockstep([step(g, j, 0, False)]), ce_ref.at[g], 1)

        for g in range(G):
            dq_ref[:, _lanes(g)] = (_unstack_heads(dq_acc[g]) * scale).astype(dq_ref.dtype)

        @pl.when(i == nb - 1)
        def _():
            dk_ref[...] = dk_acc[...].astype(dk_ref.dtype)
            dv_ref[...] = dv_acc[...].astype(dv_ref.dtype)

    n_steps = n_pairs // G
    blk = pl.BlockSpec((QB, W), lambda p, i: (i, p))
    col = pl.BlockSpec((L, W), lambda p, i: (0, p))
    return _call_with_job(
        body, job, name=name, grid=(n_steps, nb),
        in_specs=[blk,
                  pl.BlockSpec((L, W), lambda p, i: (0, n_steps + p)),
                  pl.BlockSpec((L, W), lambda p, i: (0, 2 * n_steps + p)),
                  blk, blk],
        out_specs=[blk, col, col],
        out_shape=[jax.ShapeDtypeStruct((L, sb_width), BF16)] * 3,
        scratch_shapes=[pltpu.VMEM((G, 2 * QB, LANES), F32), pltpu.VMEM((L, W), F32),
                        pltpu.VMEM((L, W), F32), pltpu.VMEM((G, 2 * QB, LANES), F32),
                        pltpu.VMEM((G, 2 * QB, LANES), F32), pltpu.VMEM((G, 2 * QB, LANES), F32),
                        pltpu.VMEM((G, 5, 2 * QB, 2 * QB), F32), pltpu.VMEM((G, 3, 2 * QB, 2 * QB), BF16),
                        pltpu.VMEM((2, 2 * QB, 2 * QB), BF16), pltpu.VMEM((2 * QB, 2 * QB), F32)],
        args=(qkv, qkv, qkv, o, dmixed))


def _conv_tile(L):
    return _pick(L, (384, 128))


def _glu(x, C):
    return x[:, :C] * _sigmoid(x[:, C:])


CONV_CHUNK = 32
SHIFT_TAIL = 24


def _fill_shifted(src_ref, dst_ref):
    n = dst_ref.shape[1]
    for r in range(1, 8):
        dst_ref[r - 1] = src_ref[r:r + n, :]


def _rows_at(src_ref, shifted_ref, start, n):
    q, r = divmod(start, 8)
    if r == 0:
        return src_ref[start:start + n, :]
    return shifted_ref[r - 1, 8 * q:8 * q + n, :]


def _conv_fwd(cacg, dw_w, dw_b, ln_g, ln_b, name, job=None):
    L, C2 = cacg.shape
    C = C2 // 2
    T = _conv_tile(L)
    H = CONV_HALO
    K = dw_w.shape[0]
    CH = CONV_CHUNK

    def body(x_ref, prev_ref, w_ref, b_ref, g_ref, beta_ref, o_ref, u_ref, us_ref):
        i = pl.program_id(0)
        u_ref[0:H, :] = jnp.where(i > 0, _glu(prev_ref[...], C), 0.0)
        u_ref[H:, :] = _glu(x_ref[...], C)
        _fill_shifted(u_ref, us_ref)
        for c0 in range(0, T, CH):
            y = jnp.broadcast_to(b_ref[...], (CH, C))
            for k in range(K):
                y = y + w_ref[k:k + 1, :] * _rows_at(u_ref, us_ref, c0 + H - (K - 1) + k, CH)
            mu = jnp.mean(y, axis=-1, keepdims=True)
            yc = y - mu
            rstd = lax.rsqrt(jnp.mean(yc * yc, axis=-1, keepdims=True) + EPS)
            ln = yc * rstd * g_ref[...] + beta_ref[...]
            o_ref[c0:c0 + CH, :] = (ln * _sigmoid(ln)).astype(o_ref.dtype)

    vec = pl.BlockSpec((1, C), lambda i: (0, 0))
    (out,), arrived = _call_with_job(
        body, job, name=name, grid=(L // T,),
        in_specs=[pl.BlockSpec((T, C2), lambda i: (i, 0)),
                  pl.BlockSpec((H, C2), lambda i: (jnp.maximum(i * (T // H) - 1, 0), 0)),
                  pl.BlockSpec((K, C), lambda i: (0, 0)), vec, vec, vec],
        out_specs=[pl.BlockSpec((T, C), lambda i: (i, 0))],
        out_shape=[jax.ShapeDtypeStruct((L, C), BF16)],
        scratch_shapes=[pltpu.VMEM((T + H, C), F32), pltpu.VMEM((7, T + SHIFT_TAIL, C), F32)],
        args=(cacg, cacg, dw_w, dw_b, ln_g, ln_b))
    return out, arrived


def _conv_bwd(cacg, dmixed, dw_w, dw_b, ln_g, ln_b, name, job=None):
    L, C2 = cacg.shape
    C = C2 // 2
    T = _conv_tile(L)
    H = CONV_HALO
    K = dw_w.shape[0]
    nt = L // T
    TE = T + H

    CH = CONV_CHUNK

    def body(x_ref, prev_ref, next_ref, d_ref, dnext_ref, w_ref, b_ref, g_ref, beta_ref,
             dca_ref, dcg_ref, dwt_ref, db_ref, dg_ref, dbeta_ref, u_ref, us_ref, dy_ref, dys_ref):
        i = pl.program_id(0)
        last = i == nt - 1

        @pl.when(i == 0)
        def _():
            dwt_ref[...] = jnp.zeros_like(dwt_ref)
            db_ref[...] = jnp.zeros_like(db_ref)
            dg_ref[...] = jnp.zeros_like(dg_ref)
            dbeta_ref[...] = jnp.zeros_like(dbeta_ref)

        u_ref[0:H, :] = jnp.where(i > 0, _glu(prev_ref[...], C), 0.0)
        u_ref[H:H + T, :] = _glu(x_ref[...], C)
        u_ref[H + T:, :] = _glu(next_ref[...], C)
        _fill_shifted(u_ref, us_ref)
        dg_acc = jnp.zeros((1, C), F32)
        dbeta_acc = jnp.zeros((1, C), F32)
        db_acc = jnp.zeros((1, C), F32)
        for c0 in range(0, TE, CH):
            y = jnp.broadcast_to(b_ref[...], (CH, C))
            for k in range(K):
                y = y + w_ref[k:k + 1, :] * _rows_at(u_ref, us_ref, c0 + H - (K - 1) + k, CH)
            mu = jnp.mean(y, axis=-1, keepdims=True)
            yc = y - mu
            rstd = lax.rsqrt(jnp.mean(yc * yc, axis=-1, keepdims=True) + EPS)
            yh = yc * rstd
            ln = yh * g_ref[...] + beta_ref[...]
            s = _sigmoid(ln)
            dout = d_ref[c0:c0 + CH, :] if c0 < T else jnp.where(last, 0.0, dnext_ref[c0 - T:c0 - T + CH, :])
            dln = dout * (s * (1.0 + ln * (1.0 - s)))
            dyh = dln * g_ref[...]
            dy = rstd * (dyh - jnp.mean(dyh, axis=-1, keepdims=True)
                         - yh * jnp.mean(dyh * yh, axis=-1, keepdims=True))
            dy_ref[c0:c0 + CH, :] = dy
            if c0 < T:
                dg_acc = dg_acc + jnp.sum(dln * yh, axis=0, keepdims=True)
                dbeta_acc = dbeta_acc + jnp.sum(dln, axis=0, keepdims=True)
                db_acc = db_acc + jnp.sum(dy, axis=0, keepdims=True)
        dg_ref[...] += dg_acc
        dbeta_ref[...] += dbeta_acc
        db_ref[...] += db_acc
        _fill_shifted(dy_ref, dys_ref)
        for k in range(K):
            dwt_ref[k:k + 1, :] += jnp.sum(
                dy_ref[0:T, :] * _rows_at(u_ref, us_ref, H - (K - 1) + k, T), axis=0, keepdims=True)
        for c0 in range(0, T, CH):
            du = jnp.zeros((CH, C), F32)
            for k in range(K):
                du = du + w_ref[k:k + 1, :] * _rows_at(dy_ref, dys_ref, c0 + (K - 1) - k, CH)
            x = x_ref[c0:c0 + CH, :]
            sg = _sigmoid(x[:, C:])
            dca_ref[c0:c0 + CH, :] = (du * sg).astype(dca_ref.dtype)
            dcg_ref[c0:c0 + CH, :] = (du * x[:, :C] * sg * (1.0 - sg)).astype(dcg_ref.dtype)

    nh = L // H
    vec = pl.BlockSpec((1, C), lambda i: (0, 0))
    row = pl.BlockSpec((T, C), lambda i: (i, 0))
    return _call_with_job(
        body, job, name=name, grid=(nt,),
        in_specs=[pl.BlockSpec((T, C2), lambda i: (i, 0)),
                  pl.BlockSpec((H, C2), lambda i: (jnp.maximum(i * (T // H) - 1, 0), 0)),
                  pl.BlockSpec((H, C2), lambda i: (jnp.minimum((i + 1) * (T // H), nh - 1), 0)),
                  pl.BlockSpec((T, C), lambda i: (i, 1)),
                  pl.BlockSpec((H, C), lambda i: (jnp.minimum((i + 1) * (T // H), nh - 1), 1)),
                  pl.BlockSpec((K, C), lambda i: (0, 0)), vec, vec, vec],
        out_specs=[row, row, pl.BlockSpec((H, C), lambda i: (0, 0)), vec, vec, vec],
        out_shape=[jax.ShapeDtypeStruct((L, C), BF16), jax.ShapeDtypeStruct((L, C), BF16),
                   jax.ShapeDtypeStruct((H, C), F32), jax.ShapeDtypeStruct((1, C), F32),
                   jax.ShapeDtypeStruct((1, C), F32), jax.ShapeDtypeStruct((1, C), F32)],
        scratch_shapes=[pltpu.VMEM((T + 2 * H, C), F32), pltpu.VMEM((7, TE + SHIFT_TAIL, C), F32),
                        pltpu.VMEM((TE, C), F32), pltpu.VMEM((7, T + SHIFT_TAIL, C), F32)],
        args=(cacg, cacg, cacg, dmixed, dmixed, dw_w, dw_b, ln_g, ln_b))


def _local_step(h0, target, n_meta, seq, norms, conv_p, wts, final_g, gather_behind, reducer):
    mix_g, ffn_g = norms
    dw_w, dw_b, ln_g, ln_b = conv_p
    depth = mix_g.shape[0]
    C = dw_b.shape[-1]
    sbw = (wts["w_in"][0].shape[-1] - 2 * C) // 3
    assert sbw == C, "the mixer halves must have equal width"
    row = lambda a, i: a[i][None, :]

    h = h0
    saved = []
    for i in range(depth):
        hn, proj_qkv = _norm_in_proj(h, row(mix_g, i), wts["w_in"][i], 3 * sbw, f"in_qkv_{i}")
        cacg = _mm_nn([(hn, wts["w_in"][i], 0, 0)], F32, f"in_conv_{i}", cols=(3 * sbw, 2 * C))
        def hosting(kind):
            job, keys = gather_behind.get((kind, i), (None, ()))

            def sink(arrived):
                for (wname, wl), arr in zip(keys, arrived):
                    wts[wname][wl] = arr

            return job, sink

        job, sink = hosting("attn")
        attn, arrived = _attn_fwd(proj_qkv, sbw, f"attn_fwd_{i}", job)
        sink(arrived)
        job, sink = hosting("conv")
        conv, arrived = _conv_fwd(cacg, dw_w[i], row(dw_b, i), row(ln_g, i), row(ln_b, i), f"conv_fwd_{i}", job)
        sink(arrived)
        h_mid = _mm_nn([(attn, wts["w_out"][i], 0, 0), (conv, wts["w_out"][i], 0, 1)], F32, f"out_proj_{i}",
                       residual=h)
        job, sink = hosting("ffn")
        (hn2, g, u, act), arrived = _ffn_up(h_mid, row(ffn_g, i), wts["w_gate_t"][i], wts["w_up_t"][i],
                                            f"ffn_up_{i}", job)
        sink(arrived)
        job, sink = hosting("down")
        h_out = _mm_nn([(act, wts["w_down"][i], 0, 0)], F32, f"down_{i}", residual=h_mid, job=job)
        if job is not None:
            h_out, arrived = h_out
            sink(arrived)
        saved.append((h, hn, proj_qkv, cacg, attn, conv, h_mid, hn2, g, u, act))
        h = h_out

    loss, dh, d_final_g = _loss_head(h, final_g[None, :], target, n_meta, seq, "loss_head")

    grads = {k: [None] * depth for k in ("mix_g", "ffn_g", "dw_w", "dw_b", "ln_g", "ln_b")}
    for i in reversed(range(depth)):
        h_in, hn, proj_qkv, cacg, attn, conv, h_mid, hn2, g, u, act = saved[i]
        big = {}
        dg, du = _ffn_down_bwd(dh, wts["w_down"][i], g, u, f"ffn_down_bwd_{i}")
        big["w_down"] = _mm_tn(act, dh, f"dw_down_{i}", col_sharded=False)
        dhn2 = _mm_nn([(dg, wts["w_gate_t"][i], 0, 0), (du, wts["w_up_t"][i], 0, 0)], F32, f"d_hn2_{i}")
        big["w_gate_t"] = _mm_tn(dg, hn2, f"dw_gate_{i}", col_sharded=False)
        big["w_up_t"] = _mm_tn(du, hn2, f"dw_up_{i}", col_sharded=False)
        dh, d_ffn = _rmsnorm_bwd(h_mid, row(ffn_g, i), dhn2, dh, f"ffn_norm_bwd_{i}")
        dmixed = _mm_nt([(dh, wts["w_out"][i], 0, 0)], F32, f"d_mixed_{i}")
        big["w_out"] = jnp.concatenate([_mm_tn(attn, dh, f"dw_out_attn_{i}", col_sharded=False, chips=2),
                                        _mm_tn(conv, dh, f"dw_out_conv_{i}", col_sharded=False, chips=2)], axis=0)
        sib_job, sib_sink = reducer.to_sibling(i, big)
        x_job, x_sink = reducer.take()
        (dq, dk, dv), arrived = _attn_bwd(proj_qkv, attn, dmixed, sbw, f"attn_bwd_{i}",
                                          _merge_jobs([sib_job, x_job]))
        sib_sink(arrived[:len(sib_job.out_shapes)])
        x_sink(arrived[len(sib_job.out_shapes):])
        x_job, x_sink = reducer.take()
        (dca, dcg, d_dw, d_b, d_lg, d_lb), arrived = _conv_bwd(
            cacg, dmixed, dw_w[i], row(dw_b, i), row(ln_g, i), row(ln_b, i), f"conv_bwd_{i}", x_job)
        x_sink(arrived)
        grads["dw_w"][i] = d_dw[:dw_w.shape[1]]
        grads["dw_b"][i], grads["ln_g"][i], grads["ln_b"][i] = d_b[0], d_lg[0], d_lb[0]
        dproj = jnp.concatenate([dq, dk, dv, dca, dcg], axis=1)
        dhn = _mm_nt([(dproj, wts["w_in"][i], 0, 0)], F32, f"d_hn_{i}")
        reducer.ready(i, {"w_in": _mm_tn(hn, dproj, f"dw_in_{i}", col_sharded=True)})
        if i == 0:
            x_job, x_sink = reducer.take()
            dh, d_mix, arrived = _rmsnorm_bwd(h_in, row(mix_g, i), dhn, dh, f"mix_norm_bwd_{i}", x_job)
            x_sink(arrived)
        else:
            dh, d_mix = _rmsnorm_bwd(h_in, row(mix_g, i), dhn, dh, f"mix_norm_bwd_{i}")
        grads["mix_g"][i], grads["ffn_g"][i] = d_mix[0], d_ffn[0]
    grads["final_g"] = d_final_g[0]
    return loss, dh, grads


ANY = pl.BlockSpec(memory_space=pl.ANY)


def _position():
    return lax.axis_index("x"), lax.axis_index("y"), lax.axis_index("c")


def _chip_at(x, y, k):
    return (1 - x if k & 2 else x), (1 - y if k & 1 else y)


def _half_rows(ref, half, rows, base=0):
    start = pl.multiple_of(base + half * rows, 8)
    lead = (slice(None),) * (len(ref.shape) - 2)
    return ref.at[(*lead, pl.ds(start, rows), slice(None))]


def _gather_job(fulls, shard_shapes, col_sharded):
    n = len(fulls)

    def tools(f_refs, send_sems, recv_sems):
        def block(wi, chip, half):
            _, R, C = shard_shapes[wi]
            if col_sharded[wi]:
                cols = pl.ds(pl.multiple_of(chip * C, LANES), C)
                return f_refs[wi].at[:, pl.ds(pl.multiple_of(half * (R // 2), 8), R // 2), cols]
            return _half_rows(f_refs[wi], half, R // 2, base=chip * R)

        def copy(wi, slot, blk, to):
            return pltpu.make_async_remote_copy(
                src_ref=blk, dst_ref=blk, send_sem=send_sems.at[6 * wi + slot],
                recv_sem=recv_sems.at[6 * wi + slot], device_id=to, device_id_type=MESH)

        return block, copy

    def start(_, f_refs, send_sems, recv_sems):
        block, copy = tools(f_refs, send_sems, recv_sems)
        x, y, c = _position()
        me = 2 * x + y
        for wi in range(n):
            for k in (1, 2, 3):
                copy(wi, k - 1, block(wi, me, c), (*_chip_at(x, y, k), c)).start()

    def finish(_, f_refs, send_sems, recv_sems):
        block, copy = tools(f_refs, send_sems, recv_sems)
        x, y, c = _position()
        me = 2 * x + y
        for wi in range(n):
            for k in (1, 2, 3):
                landed = block(wi, me ^ k, c)
                copy(wi, k - 1, landed, (x, y, c)).wait_recv()
                copy(wi, 2 + k, landed, (x, y, 1 - c)).start()
        for wi in range(n):
            for k in (1, 2, 3):
                copy(wi, 2 + k, block(wi, me ^ k, 1 - c), (x, y, c)).wait_recv()
        for wi in range(n):
            for k in (1, 2, 3):
                copy(wi, k - 1, block(wi, me, c), (x, y, c)).wait_send()
                copy(wi, 2 + k, block(wi, me ^ k, c), (x, y, c)).wait_send()

    return _CommJob(fulls, [jax.ShapeDtypeStruct(f.shape, f.dtype) for f in fulls], {i: i for i in range(n)},
                    6 * n, start, finish)


def _place_shard(w, layer, chip, col_sharded, dtype, name):
    _, R, C = w.shape
    tr = _pick(R, (256, 352, 128, 48))
    nr = R // tr

    def body(chip_ref, w_ref, o_ref):
        o_ref[...] = w_ref[...].astype(dtype)

    if col_sharded:
        shape = (1, R, N_CHIPS * C)
        out_spec = pl.BlockSpec((None, tr, C), lambda r, chip_ref: (0, r, chip_ref[0]))
    else:
        shape = (1, N_CHIPS * R, C)
        out_spec = pl.BlockSpec((None, tr, C), lambda r, chip_ref: (0, chip_ref[0] * nr + r, 0))
    grid_spec = pltpu.PrefetchScalarGridSpec(
        num_scalar_prefetch=1, grid=(nr,),
        in_specs=[pl.BlockSpec((None, tr, C), lambda r, chip_ref: (layer, r, 0))], out_specs=out_spec)
    return pl.pallas_call(
        body, name=name, grid_spec=grid_spec, out_shape=jax.ShapeDtypeStruct(shape, dtype),
        compiler_params=_params(("parallel",)),
    )(chip, w)


def _sibling_job(grads):
    n = len(grads)

    def copies(g_refs, l_refs, send_sems, recv_sems):
        x, y, c = _position()
        return [pltpu.make_async_remote_copy(
            src_ref=_half_rows(g_refs[wi], 1 - c, grads[wi].shape[1] // 2), dst_ref=l_refs[wi],
            send_sem=send_sems.at[wi], recv_sem=recv_sems.at[wi],
            device_id=(x, y, 1 - c), device_id_type=MESH) for wi in range(n)]

    def start(*refs):
        for cp in copies(*refs):
            cp.start()

    def finish(*refs):
        for cp in copies(*refs):
            cp.wait()

    outs = [jax.ShapeDtypeStruct((g.shape[0], g.shape[1] // 2, g.shape[2]), g.dtype) for g in grads]
    return _CommJob(grads, outs, {}, n, start, finish)


def _chip_sum(g, landed, core, name):
    _, R, C = g.shape
    hr = R // 2
    tr = _pick(hr, (256, 352, 128))
    nr = hr // tr

    def body(c_ref, g_ref, l_ref, o_ref):
        o_ref[...] = (g_ref[...] + l_ref[...]).astype(BF16)

    grid_spec = pltpu.PrefetchScalarGridSpec(
        num_scalar_prefetch=1, grid=(N_CHIPS, nr),
        in_specs=[pl.BlockSpec((None, tr, C), lambda j, r, c_ref: (j, c_ref[0] * nr + r, 0)),
                  pl.BlockSpec((None, tr, C), lambda j, r, c_ref: (j, r, 0))],
        out_specs=pl.BlockSpec((None, tr, C), lambda j, r, c_ref: (j, r, 0)))
    return pl.pallas_call(
        body, name=name, grid_spec=grid_spec, out_shape=jax.ShapeDtypeStruct((N_CHIPS, hr, C), BF16),
        compiler_params=_params(("parallel", "parallel")),
    )(core, g, landed)


def _across_job(parts):
    n = len(parts)

    def copy(p_refs, l_refs, send_sems, recv_sems, wi, k, to):
        x, y, _ = _position()
        me = 2 * x + y
        return pltpu.make_async_remote_copy(
            src_ref=p_refs[wi].at[me ^ k], dst_ref=l_refs[wi].at[me],
            send_sem=send_sems.at[3 * wi + k - 1], recv_sem=recv_sems.at[3 * wi + k - 1],
            device_id=to, device_id_type=MESH)

    def start(p_refs, l_refs, send_sems, recv_sems):
        x, y, c = _position()
        for wi in range(n):
            for k in (1, 2, 3):
                copy(p_refs, l_refs, send_sems, recv_sems, wi, k, (*_chip_at(x, y, k), c)).start()

    def finish(p_refs, l_refs, send_sems, recv_sems):
        x, y, c = _position()
        me = 2 * x + y
        for wi in range(n):
            for k in (1, 2, 3):
                slot = l_refs[wi].at[me ^ k]
                pltpu.make_async_remote_copy(
                    src_ref=slot, dst_ref=slot, send_sem=send_sems.at[3 * wi + k - 1],
                    recv_sem=recv_sems.at[3 * wi + k - 1], device_id=(x, y, c), device_id_type=MESH).wait_recv()
        for wi in range(n):
            for k in (1, 2, 3):
                copy(p_refs, l_refs, send_sems, recv_sems, wi, k, (x, y, c)).wait_send()

    return _CommJob(parts, [jax.ShapeDtypeStruct(p.shape, p.dtype) for p in parts], {}, 3 * n, start, finish)


class _Reducer:
    def __init__(self, core):
        self.core, self.parts, self.across, self.pending = core, {}, {}, []

    def to_sibling(self, layer, big):
        names = list(big)
        flat = [big[k] for k in names]

        def sink(landed):
            for k, g, la in zip(names, flat, landed):
                self.parts[k, layer] = _chip_sum(g, la, self.core, f"chip_sum_{k}_{layer}")
                self.pending.append((k, layer))

        return _sibling_job(flat), sink

    def ready(self, layer, big):
        job, sink = self.to_sibling(layer, big)
        sink(_run_job(job, f"grads_to_sibling_{next(iter(big))}_{layer}"))

    def take(self):
        keys, self.pending = self.pending, []
        if not keys:
            return None, lambda results: None

        def sink(results):
            self.across.update(zip(keys, results))

        return _across_job([self.parts[key] for key in keys]), sink


def _sum_chips(parts, landed, where, layer, depth, prev, name):
    _, hr, C = landed.shape
    tr = _pick(hr, (256, 352, 128))
    nr = hr // tr

    def body(*refs):
        own_ref, slots, o_ref = refs[1], refs[2:2 + N_CHIPS], refs[-1]
        chip = refs[0][0]
        total = None
        for q in range(N_CHIPS):
            term = jnp.where(chip == q, own_ref[...], slots[q][...]).astype(F32)
            total = term if total is None else total + term
        o_ref[...] = total

    def slot_spec(q):
        return pl.BlockSpec((None, tr, C), lambda r, w: (jnp.where(w[0] == q, (q + 1) % N_CHIPS, q), r, 0))

    in_specs = [pl.BlockSpec((None, tr, C), lambda r, w: (w[0], r, 0))] + [slot_spec(q) for q in range(N_CHIPS)]
    args = [where, parts] + [landed] * N_CHIPS
    aliases = {}
    if prev is not None:
        in_specs.append(ANY)
        args.append(prev)
        aliases = {len(args) - 1: 0}
    grid_spec = pltpu.PrefetchScalarGridSpec(
        num_scalar_prefetch=1, grid=(nr,), in_specs=in_specs,
        out_specs=pl.BlockSpec((None, tr, C), lambda r, w: (layer, w[1] * nr + r, 0)))
    return pl.pallas_call(
        body, name=name, grid_spec=grid_spec, out_shape=jax.ShapeDtypeStruct((depth, 2 * hr, C), F32),
        input_output_aliases=aliases, compiler_params=_params(("parallel",)),
    )(*args)


def _rs_join_halves(reduced):
    n = len(reduced)

    def body(*refs):
        o_refs = refs[n:2 * n]
        send_sems, recv_sems = refs[2 * n:]
        x, y, c = _position()
        sent = []
        for wi in range(n):
            hr = reduced[wi].shape[1] // 2
            mine = _half_rows(o_refs[wi], c, hr)
            cp = pltpu.make_async_remote_copy(
                src_ref=mine, dst_ref=mine, send_sem=send_sems.at[wi], recv_sem=recv_sems.at[wi],
                device_id=(x, y, 1 - c), device_id_type=MESH)
            cp.start()
            sent.append(cp)
        for wi in range(n):
            hr = reduced[wi].shape[1] // 2
            theirs = _half_rows(o_refs[wi], 1 - c, hr)
            pltpu.make_async_remote_copy(
                src_ref=theirs, dst_ref=theirs, send_sem=send_sems.at[wi], recv_sem=recv_sems.at[wi],
                device_id=(x, y, c), device_id_type=MESH).wait_recv()
        for cp in sent:
            cp.wait_send()

    return pl.pallas_call(
        body, name="grads_join_halves", out_shape=[jax.ShapeDtypeStruct(r.shape, r.dtype) for r in reduced],
        in_specs=[ANY] * n, out_specs=[ANY] * n, input_output_aliases={i: i for i in range(n)},
        scratch_shapes=[pltpu.SemaphoreType.DMA((n,)), pltpu.SemaphoreType.DMA((n,))],
    )(*reduced)


def _small_allreduce(vec):
    rows = vec.shape[0]

    def body(v_ref, o_ref, land, send_sems, recv_sems):
        x, y, c = _position()
        me = 4 * x + 2 * y + c
        land[0] = v_ref[...]
        sent = []
        for k in range(1, N_DEV):
            to = (1 - x if k & 4 else x, 1 - y if k & 2 else y, 1 - c if k & 1 else c)
            cp = pltpu.make_async_remote_copy(
                src_ref=v_ref, dst_ref=land.at[k], send_sem=send_sems.at[k - 1], recv_sem=recv_sems.at[k - 1],
                device_id=to, device_id_type=MESH)
            cp.start()
            sent.append(cp)
        for cp in sent:
            cp.wait_recv()
        acc = land[me]
        for e in range(1, N_DEV):
            acc = acc + land[me ^ e]
        o_ref[...] = acc
        for cp in sent:
            cp.wait_send()

    vmem = pl.BlockSpec(memory_space=pltpu.VMEM)
    return pl.pallas_call(
        body, name="small_allreduce", out_shape=jax.ShapeDtypeStruct(vec.shape, F32),
        in_specs=[vmem], out_specs=vmem,
        scratch_shapes=[pltpu.VMEM((N_DEV, rows, LANES), F32), pltpu.SemaphoreType.DMA((N_DEV - 1,)),
                        pltpu.SemaphoreType.DMA((N_DEV - 1,))],
    )(vec)


def _adam_math(w, g, m, v):
    m = ADAM_B1 * m + (1.0 - ADAM_B1) * g
    v = ADAM_B2 * v + (1.0 - ADAM_B2) * jnp.square(g)
    m_hat = m / (1.0 - ADAM_B1 ** ADAM_STEP)
    v_hat = v / (1.0 - ADAM_B2 ** ADAM_STEP)
    delta = -ADAM_LR * (m_hat / (jnp.sqrt(v_hat) + ADAM_EPS) + ADAM_WD * w)
    return delta, m, v


def _adam(w, g, m, v, name):
    def body(w_ref, g_ref, m_ref, v_ref, d_ref, nm_ref, nv_ref):
        d_ref[...], nm_ref[...], nv_ref[...] = _adam_math(w_ref[...], g_ref[...], m_ref[...], v_ref[...])

    if w.ndim == 3:
        lyr, R, C = w.shape
        tr = _pick(R, (256, 352, 128))
        blk = pl.BlockSpec((None, tr, C), lambda l, r: (l, r, 0))
        grid, sem = (lyr, R // tr), ("parallel", "parallel")
    else:
        blk = pl.BlockSpec(w.shape, lambda: (0, 0))
        grid, sem = (), None
    return pl.pallas_call(
        body, name=name, grid=grid, in_specs=[blk] * 4, out_specs=[blk] * 3,
        out_shape=[jax.ShapeDtypeStruct(w.shape, F32)] * 3, compiler_params=_params(sem),
    )(w, g, m, v)


def _rows(a, pad_to=8):
    r = a.reshape(-1, LANES)
    extra = (-r.shape[0]) % pad_to
    return jnp.pad(r, ((0, extra), (0, 0))) if extra else r


def _pack(arrays):
    return jnp.concatenate([_rows(a) for a in arrays], axis=0)


def _unpack(slab, shapes):
    out, at = [], 0
    for shp in shapes:
        nrow = math.prod(shp) // LANES
        out.append(slab[at:at + nrow].reshape(shp))
        at += nrow + (-nrow) % 8
    return out


BIG = ("w_in", "w_out", "w_gate_t", "w_up_t", "w_down")
BIG_COL_SHARDED = (True, False, False, False, False)
TRANSPOSED = {"w_gate_t": "w_gate", "w_up_t": "w_up"}


def kernel(x, meta_tokens, mix_norm_g, w_in, conv_dw_w, conv_dw_b, conv_ln_g, conv_ln_b, w_out, ffn_norm_g, w_gate, w_up, w_down, final_norm_g, loss_target, m_meta_tokens, m_mix_norm_g, m_w_in, m_conv_dw_w, m_conv_dw_b, m_conv_ln_g, m_conv_ln_b, m_w_out, m_ffn_norm_g, m_w_gate, m_w_up, m_w_down, m_final_norm_g, v_meta_tokens, v_mix_norm_g, v_w_in, v_conv_dw_w, v_conv_dw_b, v_conv_ln_g, v_conv_ln_b, v_w_out, v_ffn_norm_g, v_w_gate, v_w_up, v_w_down, v_final_norm_g):
    n_meta, seq = meta_tokens.shape[0], x.shape[1]
    D = x.shape[2]
    depth, taps, c_shard = conv_dw_w.shape
    C = conv_dw_b.shape[-1]
    chip = (2 * lax.axis_index("x") + lax.axis_index("y")).astype(jnp.int32)
    core = lax.axis_index("c").astype(jnp.int32).reshape(1)
    chip1 = chip.reshape(1)
    where = jnp.concatenate([chip1, core])
    big_w = dict(w_in=w_in, w_out=w_out, w_gate=w_gate, w_up=w_up, w_down=w_down)
    big_m = dict(w_in=m_w_in, w_out=m_w_out, w_gate=m_w_gate, w_up=m_w_up, w_down=m_w_down)
    big_v = dict(w_in=v_w_in, w_out=v_w_out, w_gate=v_w_gate, w_up=v_w_up, w_down=v_w_down)

    small_shard = _pack([conv_dw_w, meta_tokens])[None]
    to_send = {k: jnp.swapaxes(big_w[TRANSPOSED[k]], 1, 2) if k in TRANSPOSED else big_w[k] for k in BIG}
    col = dict(zip(BIG, BIG_COL_SHARDED))
    wts = {k: [_place_shard(to_send[k], l, chip1, col[k], BF16, f"place_{k}_{l}") for l in range(depth)] for k in BIG}
    small_placed = _place_shard(small_shard, 0, chip1, False, F32, "place_small")

    def gather_job(keys, extra=()):
        arrays = [wts[k][l] for k, l in keys] + list(extra)
        shapes = [(1,) + to_send[k].shape[1:] for k, _ in keys] + [(1,) + small_shard.shape[1:]] * len(extra)
        return _gather_job(arrays, shapes, [col[k] for k, _ in keys] + [False] * len(extra))

    first_keys = [("w_in", 0)]
    *first, small_full = _run_job(gather_job(first_keys, [small_placed]), "gather_first")
    for (k, l), arr in zip(first_keys, first):
        wts[k][l] = arr
    behind_keys = {("attn", 0): [("w_out", 0), ("w_gate_t", 0), ("w_up_t", 0)], ("conv", 0): [("w_down", 0)]}
    for l in range(1, depth):
        behind_keys["ffn", l - 1] = [("w_in", l), ("w_out", l), ("w_gate_t", l)]
        behind_keys["down", l - 1] = [("w_up_t", l)]
        behind_keys["attn", l] = [("w_down", l)]
    gather_behind = {host: (gather_job(keys), keys) for host, keys in behind_keys.items()}

    rows_shard = small_shard.shape[1]
    dw_full, meta_full = [], []
    for j in range(N_CHIPS):
        dwj, mj = _unpack(small_full[0, j * rows_shard:(j + 1) * rows_shard],
                          [conv_dw_w.shape, meta_tokens.shape])
        dw_full.append(dwj)
        meta_full.append(mj)
    dw_w_full = jnp.concatenate(dw_full, axis=2)
    meta = jnp.concatenate(meta_full, axis=1)

    L = n_meta + seq
    Lp = -(-L // QUERY_BLOCK) * QUERY_BLOCK
    h0 = jnp.concatenate([meta, x[0], jnp.zeros((Lp - L, D), F32)], axis=0)
    target = jnp.pad(loss_target[0], ((n_meta, Lp - L), (0, 0)))
    reducer = _Reducer(core)
    loss, dh0, grads = _local_step(h0, target, n_meta, seq, (mix_norm_g, ffn_norm_g),
                                   (dw_w_full, conv_dw_b, conv_ln_g, conv_ln_b), wts, final_norm_g,
                                   gather_behind, reducer)
    loss = lax.psum(loss[0, 0], ("x", "y", "c"))
    grad_x = dh0[n_meta:L][None]

    reduced = []
    for k in BIG:
        arr = None
        for l in range(depth):
            arr = _sum_chips(reducer.parts[k, l], reducer.across[k, l], where, l, depth, arr, f"sum_chips_{k}_{l}")
        reduced.append(arr)
    big_g = dict(zip(BIG, _rs_join_halves(reduced)))

    small_names = ("mix_g", "ffn_g", "dw_b", "ln_g", "ln_b", "dw_w")
    small = [jnp.stack(grads[k]) for k in small_names] + [grads["final_g"], dh0[:n_meta]]
    small_shapes = [a.shape for a in small]
    g_mix, g_ffn, g_dwb, g_lng, g_lnb, g_dww, g_final, g_meta = _unpack(_small_allreduce(_pack(small)), small_shapes)
    g_dww = lax.dynamic_slice_in_dim(g_dww, chip * c_shard, c_shard, axis=2)
    g_meta = lax.dynamic_slice_in_dim(g_meta, chip * meta_tokens.shape[1], meta_tokens.shape[1], axis=1)

    out_g, out_d, out_m, out_v = {}, {}, {}, {}
    for kk in BIG:
        k = TRANSPOSED.get(kk, kk)
        view = (lambda a: jnp.swapaxes(a, 1, 2)) if kk in TRANSPOSED else (lambda a: a)
        res = _adam(view(big_w[k]), big_g[kk], view(big_m[k]), view(big_v[k]), f"adam_{k}")
        out_g[k] = view(big_g[kk])
        out_d[k], out_m[k], out_v[k] = (view(a) for a in res)
    small_order = ("meta_tokens", "mix_norm_g", "conv_dw_w", "conv_dw_b", "conv_ln_g", "conv_ln_b",
                   "ffn_norm_g", "final_norm_g")
    sw = dict(meta_tokens=meta_tokens, mix_norm_g=mix_norm_g, conv_dw_w=conv_dw_w, conv_dw_b=conv_dw_b,
              conv_ln_g=conv_ln_g, conv_ln_b=conv_ln_b, ffn_norm_g=ffn_norm_g, final_norm_g=final_norm_g)
    sm = dict(meta_tokens=m_meta_tokens, mix_norm_g=m_mix_norm_g, conv_dw_w=m_conv_dw_w, conv_dw_b=m_conv_dw_b,
              conv_ln_g=m_conv_ln_g, conv_ln_b=m_conv_ln_b, ffn_norm_g=m_ffn_norm_g, final_norm_g=m_final_norm_g)
    sv = dict(meta_tokens=v_meta_tokens, mix_norm_g=v_mix_norm_g, conv_dw_w=v_conv_dw_w, conv_dw_b=v_conv_dw_b,
              conv_ln_g=v_conv_ln_g, conv_ln_b=v_conv_ln_b, ffn_norm_g=v_ffn_norm_g, final_norm_g=v_final_norm_g)
    sg = dict(meta_tokens=g_meta, mix_norm_g=g_mix, conv_dw_w=g_dww, conv_dw_b=g_dwb, conv_ln_g=g_lng,
              conv_ln_b=g_lnb, ffn_norm_g=g_ffn, final_norm_g=g_final)
    def slab(d):
        return _pack([d[k] for k in small_order])
    shapes = [sw[k].shape for k in small_order]
    deltas = _adam(slab(sw), slab(sg), slab(sm), slab(sv), "adam_small")
    for res, dst in zip(deltas, (out_d, out_m, out_v)):
        dst.update(zip(small_order, _unpack(res, shapes)))
    out_g.update(sg)

    order = ("meta_tokens", "mix_norm_g", "w_in", "conv_dw_w", "conv_dw_b", "conv_ln_g", "conv_ln_b", "w_out",
             "ffn_norm_g", "w_gate", "w_up", "w_down", "final_norm_g")
    return (loss, grad_x, *[out_g[k] for k in order], *[out_d[k] for k in order],
            *[out_m[k] for k in order], *[out_v[k] for k in order])
```

```python
import functools
import math

import jax
import jax.numpy as jnp
from jax import lax
from jax.experimental import pallas as pl
from jax.experimental.pallas import tpu as pltpu

F32 = jnp.float32
BF16 = jnp.bfloat16
MESH = pl.DeviceIdType.MESH

EPS = 1e-6
QUERY_BLOCK = 128
LANES = 128
HEAD_DIM = 64
LOG_STICK_FLOOR = -40.0
CONV_HALO = 32
N_CHIPS = 4
N_DEV = 8
VMEM_LIMIT = 56 * 1024 * 1024

ADAM_LR = 0.001
ADAM_B1 = 0.9
ADAM_B2 = 0.999
ADAM_EPS = 1e-08
ADAM_WD = 0.01
ADAM_STEP = 10


def _pick(n, prefs):
    for p in prefs:
        if n % p == 0:
            return p
    return n


def _params(sem=None):
    return pltpu.CompilerParams(dimension_semantics=sem, vmem_limit_bytes=VMEM_LIMIT)


def _sigmoid(x):
    return 1.0 / (1.0 + jnp.exp(-x))


def _rmsnorm_bwd(h, g, dy, dh_in, name, job=None):
    L, D = h.shape
    T = _pick(L, (384, 128))

    def body(h_ref, g_ref, dy_ref, dhin_ref, dh_ref, dg_ref):
        x = h_ref[...]
        dyv = dy_ref[...]
        r = lax.rsqrt(jnp.mean(x * x, axis=-1, keepdims=True) + EPS)
        xh = x * r
        dxh = dyv * g_ref[...]
        dh_ref[...] = dhin_ref[...] + r * (dxh - xh * jnp.mean(dxh * xh, axis=-1, keepdims=True))

        @pl.when(pl.program_id(0) == 0)
        def _():
            dg_ref[...] = jnp.zeros_like(dg_ref)

        dg_ref[...] += jnp.sum(dyv * xh, axis=0, keepdims=True)

    row = pl.BlockSpec((T, D), lambda i: (i, 0))
    vec = pl.BlockSpec((1, D), lambda i: (0, 0))
    (dh, dg), arrived = _call_with_job(
        body, job, name=name, grid=(L // T,),
        in_specs=[row, vec, row, row], out_specs=[row, vec],
        out_shape=[jax.ShapeDtypeStruct((L, D), F32), jax.ShapeDtypeStruct((1, D), F32)],
        scratch_shapes=[], args=(h, g, dy, dh_in))
    return (dh, dg) if job is None else (dh, dg, arrived)


def _loss_head(h, g, target, n_meta, seq, name):
    L, D = h.shape
    T = _pick(L, (384, 128))

    def body(h_ref, g_ref, t_ref, loss_ref, dh_ref, dg_ref):
        i = pl.program_id(0)
        x = h_ref[...]
        gv = g_ref[...]
        r = lax.rsqrt(jnp.mean(x * x, axis=-1, keepdims=True) + EPS)
        xh = x * r
        y = xh * gv
        rows = i * T + lax.broadcasted_iota(jnp.int32, (T, 1), 0)
        live = (rows >= n_meta) & (rows < n_meta + seq)
        diff = jnp.where(live, y - t_ref[...], 0.0)
        dyv = diff / D
        dxh = dyv * gv
        dh_ref[...] = r * (dxh - xh * jnp.mean(dxh * xh, axis=-1, keepdims=True))

        @pl.when(i == 0)
        def _():
            dg_ref[...] = jnp.zeros_like(dg_ref)
            loss_ref[...] = jnp.zeros_like(loss_ref)

        dg_ref[...] += jnp.sum(dyv * xh, axis=0, keepdims=True)
        per_row = jnp.mean(diff * diff, axis=-1, keepdims=True)
        loss_ref[...] += 0.5 * jnp.sum(per_row, axis=0, keepdims=True)

    row = pl.BlockSpec((T, D), lambda i: (i, 0))
    vec = pl.BlockSpec((1, D), lambda i: (0, 0))
    one = pl.BlockSpec((1, 1), lambda i: (0, 0))
    return pl.pallas_call(
        body, name=name, grid=(L // T,),
        in_specs=[row, vec, row], out_specs=[one, row, vec],
        out_shape=[jax.ShapeDtypeStruct((1, 1), F32), jax.ShapeDtypeStruct((L, D), F32),
                   jax.ShapeDtypeStruct((1, D), F32)],
        compiler_params=_params(("arbitrary",)),
    )(h, g, target)


def _ffn_tiles(M, F):
    return _pick(M, (352, 384, 128)), _pick(F, (1408, 512, 256, 128))


def _resident(shape):
    return pl.BlockSpec((None,) + tuple(shape[1:]), lambda *_: (0,) * len(shape), pipeline_mode=pl.Buffered(1))


def _normed_rows(h_ref, g_ref, hn_ref, keep_ref):
    @pl.when(pl.program_id(1) == 0)
    def _():
        x = h_ref[...]
        r = lax.rsqrt(jnp.mean(x * x, axis=-1, keepdims=True) + EPS)
        keep_ref[...] = (x * r * g_ref[...]).astype(BF16)
        hn_ref[...] = keep_ref[...]


def _norm_in_proj(h, g, w_in, n_cols, name):
    M, D = h.shape
    tm = _pick(M, (1056, 384, 128))
    tn = _pick(n_cols, (512, 256, 128))

    def body(h_ref, g_ref, w_ref, hn_ref, o_ref, keep_ref):
        _normed_rows(h_ref, g_ref, hn_ref, keep_ref)
        o_ref[...] = jnp.dot(keep_ref[...], w_ref[...], preferred_element_type=F32).astype(o_ref.dtype)

    rows = pl.BlockSpec((tm, D), lambda i, j: (i, 0))
    return pl.pallas_call(
        body, name=name, grid=(M // tm, n_cols // tn),
        in_specs=[rows, pl.BlockSpec((1, D), lambda i, j: (0, 0)), pl.BlockSpec((None, D, tn), lambda i, j: (0, 0, j))],
        out_specs=[rows, pl.BlockSpec((tm, tn), lambda i, j: (i, j))],
        out_shape=[jax.ShapeDtypeStruct((M, D), BF16), jax.ShapeDtypeStruct((M, n_cols), BF16)],
        scratch_shapes=[pltpu.VMEM((tm, D), BF16)],
        compiler_params=_params(("parallel", "arbitrary")),
    )(h, g, w_in)


def _ffn_up(h, norm_g, w_gate_t, w_up_t, name, job=None):
    M, D = h.shape
    F = w_gate_t.shape[1]
    tm, tf = _ffn_tiles(M, F)
    nt = (((1,), (1,)), ((), ()))

    def body(h_ref, ng_ref, wg_ref, wu_ref, hn_ref, g_ref, u_ref, a_ref):
        x = h_ref[...]
        r = lax.rsqrt(jnp.mean(x * x, axis=-1, keepdims=True) + EPS)
        hv = (x * r * ng_ref[...]).astype(BF16)
        hn_ref[...] = hv
        for c in range(0, F, tf):
            gv = lax.dot_general(hv, wg_ref[c:c + tf, :], nt, preferred_element_type=F32)
            uv = lax.dot_general(hv, wu_ref[c:c + tf, :], nt, preferred_element_type=F32)
            g_ref[:, c:c + tf] = gv.astype(g_ref.dtype)
            u_ref[:, c:c + tf] = uv.astype(u_ref.dtype)
            a_ref[:, c:c + tf] = (gv * _sigmoid(gv) * uv).astype(a_ref.dtype)

    rows = pl.BlockSpec((tm, D), lambda i: (i, 0))
    wide = pl.BlockSpec((tm, F), lambda i: (i, 0))
    return _call_with_job(
        body, job, name=name, grid=(M // tm,),
        in_specs=[rows, pl.BlockSpec((1, D), lambda i: (0, 0)), _resident(w_gate_t.shape), _resident(w_up_t.shape)],
        out_specs=[rows, wide, wide, wide],
        out_shape=[jax.ShapeDtypeStruct((M, D), BF16), jax.ShapeDtypeStruct((M, F), BF16),
                   jax.ShapeDtypeStruct((M, F), BF16), jax.ShapeDtypeStruct((M, F), BF16)],
        scratch_shapes=[], args=(h, norm_g, w_gate_t, w_up_t))


def _ffn_down_bwd(dh, w_down, g, u, name):
    M, D = dh.shape
    F = g.shape[1]
    tm, tf = _ffn_tiles(M, F)

    def body(d_ref, w_ref, g_ref, u_ref, dg_ref, du_ref):
        dhv = d_ref[...].astype(BF16)
        for c in range(0, F, tf):
            dv = lax.dot_general(dhv, w_ref[c:c + tf, :], (((1,), (1,)), ((), ())), preferred_element_type=F32)
            gv = g_ref[:, c:c + tf].astype(F32)
            s = _sigmoid(gv)
            du_ref[:, c:c + tf] = (dv * (gv * s)).astype(du_ref.dtype)
            dg_ref[:, c:c + tf] = (dv * u_ref[:, c:c + tf].astype(F32)
                                   * (s * (1.0 + gv * (1.0 - s)))).astype(dg_ref.dtype)

    wide = pl.BlockSpec((tm, F), lambda i: (i, 0))
    return pl.pallas_call(
        body, name=name, grid=(M // tm,),
        in_specs=[pl.BlockSpec((tm, D), lambda i: (i, 0)), _resident(w_down.shape), wide, wide],
        out_specs=[wide, wide],
        out_shape=[jax.ShapeDtypeStruct((M, F), BF16), jax.ShapeDtypeStruct((M, F), BF16)],
        compiler_params=_params(("parallel",)),
    )(dh, w_down, g, u)


def _mm_nn(pairs, out_dtype, name, residual=None, cols=None, job=None):
    M = pairs[0][0].shape[0]
    col0, N = cols if cols is not None else (0, pairs[0][1].shape[-1])
    tm = _pick(M, (1056, 384, 128))
    shallow = sum(p[0].shape[1] for p in pairs) <= 1024
    tn = _pick(math.gcd(N, col0) if col0 else N, ((1024,) if shallow else ()) + (640, 512, 256, 128))
    jb = col0 // tn
    n = len(pairs)

    def body(*refs):
        a_refs, w_refs = refs[:n], refs[n:2 * n]
        o_ref = refs[-1]
        acc = None
        for a_ref, w_ref in zip(a_refs, w_refs):
            d = jnp.dot(a_ref[...].astype(BF16), w_ref[...], preferred_element_type=F32)
            acc = d if acc is None else acc + d
        if residual is not None:
            acc = acc + refs[2 * n][...]
        o_ref[...] = acc.astype(o_ref.dtype)

    in_specs = [pl.BlockSpec((tm, a.shape[1]), lambda i, j: (i, 0)) for a, _, _, _ in pairs]
    for a, _, layer, kblk in pairs:
        in_specs.append(pl.BlockSpec((None, a.shape[1], tn), functools.partial(lambda i, j, l, kb: (l, kb, j + jb), l=layer, kb=kblk)))
    args = [p[0] for p in pairs] + [p[1] for p in pairs]
    if residual is not None:
        in_specs.append(pl.BlockSpec((tm, tn), lambda i, j: (i, j)))
        args.append(residual)
    (out,), arrived = _call_with_job(
        body, job, name=name, grid=(M // tm, N // tn), in_specs=in_specs,
        out_specs=[pl.BlockSpec((tm, tn), lambda i, j: (i, j))],
        out_shape=[jax.ShapeDtypeStruct((M, N), out_dtype)], scratch_shapes=[], args=args)
    return out if job is None else (out, arrived)


def _mm_nt(pairs, out_dtype, name):
    M = pairs[0][0].shape[0]
    K = pairs[0][1].shape[1]
    tm = _pick(M, (1056, 384, 128))
    shallow = sum(p[0].shape[1] for p in pairs) <= 1024
    tk = _pick(K, ((1024,) if shallow else ()) + (512, 1408, 256, 128))
    n = len(pairs)

    def body(*refs):
        d_refs, w_refs = refs[:n], refs[n:2 * n]
        o_ref = refs[-1]
        acc = None
        for d_ref, w_ref in zip(d_refs, w_refs):
            d = lax.dot_general(d_ref[...].astype(BF16), w_ref[...], (((1,), (1,)), ((), ())),
                                preferred_element_type=F32)
            acc = d if acc is None else acc + d
        o_ref[...] = acc.astype(o_ref.dtype)

    in_specs = [pl.BlockSpec((tm, d.shape[1]), lambda i, j: (i, 0)) for d, _, _, _ in pairs]
    for d, _, layer, cblk in pairs:
        in_specs.append(pl.BlockSpec((None, tk, d.shape[1]), functools.partial(lambda i, j, l, cb: (l, j, cb), l=layer, cb=cblk)))
    args = [p[0] for p in pairs] + [p[1] for p in pairs]
    return pl.pallas_call(
        body, name=name, grid=(M // tm, K // tk), in_specs=in_specs,
        out_specs=pl.BlockSpec((tm, tk), lambda i, j: (i, j)),
        out_shape=jax.ShapeDtypeStruct((M, K), out_dtype),
        compiler_params=_params(("parallel", "parallel")),
    )(*args)


def _mm_tn(a, b, name, col_sharded, chips=N_CHIPS):
    M, K = a.shape
    N = b.shape[1]
    tm = _pick(M, (1056, 384, 128))
    tk = _pick(K, (1024, 1408, 512, 256, 128))
    tn = N // N_CHIPS if col_sharded else _pick(N, (512, 128))

    def body(a_ref, b_ref, o_ref):
        @pl.when(pl.program_id(2) == 0)
        def _():
            o_ref[...] = jnp.zeros_like(o_ref)

        o_ref[...] += lax.dot_general(a_ref[...].astype(BF16), b_ref[...].astype(BF16),
                                      (((0,), (0,)), ((), ())), preferred_element_type=F32)

    if col_sharded:
        out_shape = jax.ShapeDtypeStruct((N_CHIPS, K, tn), F32)
        out_spec = pl.BlockSpec((None, tk, tn), lambda k, j, m: (j, k, 0))
    else:
        out_shape = jax.ShapeDtypeStruct((K, N), F32)
        out_spec = pl.BlockSpec((tk, tn), lambda k, j, m: (k, j))
    out = pl.pallas_call(
        body, name=name, grid=(K // tk, N // tn, M // tm),
        in_specs=[pl.BlockSpec((tm, tk), lambda k, j, m: (m, k)), pl.BlockSpec((tm, tn), lambda k, j, m: (m, j))],
        out_specs=out_spec, out_shape=out_shape,
        compiler_params=_params(("parallel", "parallel", "arbitrary")),
    )(a, b)
    return out if col_sharded else out.reshape(chips, K // chips, N)


def _stack_heads(x, scale=None):
    lane = lax.broadcasted_iota(jnp.int32, x.shape, 1)
    zero = jnp.zeros_like(x)
    lo = jnp.where(lane < HEAD_DIM, x, zero)
    hi = jnp.where(lane < HEAD_DIM, zero, x)
    out = jnp.concatenate([lo, hi], axis=0)
    return out if scale is None else out * scale


def _unstack_heads(x2):
    qb = x2.shape[0] // 2
    lane = lax.broadcasted_iota(jnp.int32, (qb, LANES), 1)
    return jnp.where(lane < HEAD_DIM, x2[:qb], x2[qb:])


def _dot2(x, m):
    xh = x.astype(BF16)
    xl = (x - xh.astype(F32)).astype(BF16)
    return jnp.dot(xh, m, preferred_element_type=F32) + jnp.dot(xl, m, preferred_element_type=F32)


ATTN_CHUNK = 32


def _in_lockstep(staged):
    waiting, live = list(staged), []
    while waiting or live:
        if waiting:
            live.append(waiting.pop(0))
        for gen in list(live):
            if next(gen, StopIteration) is StopIteration:
                live.remove(gen)


def _row_chunks():
    return [slice(r, r + ATTN_CHUNK) for r in range(0, 2 * QUERY_BLOCK, ATTN_CHUNK)]


def _chunk_valid(rows, older):
    shape = (rows.stop - rows.start, older + QUERY_BLOCK)
    r = (rows.start + lax.broadcasted_iota(jnp.int32, shape, 0)) & (QUERY_BLOCK - 1)
    return lax.broadcasted_iota(jnp.int32, shape, 1) < r + older


def _split_to(x, hi_ref, lo_ref, rows):
    xh = x.astype(BF16)
    hi_ref[rows, :] = xh
    lo_ref[rows, :] = (x - xh.astype(F32)).astype(BF16)


def _triangle(keys, strict):
    r = lax.broadcasted_iota(jnp.int32, (keys, keys), 0)
    c = lax.broadcasted_iota(jnp.int32, (keys, keys), 1)
    return jnp.where((r > c) if strict else (r >= c), 1.0, 0.0).astype(BF16)


def _fill_attn_consts(tri_ref, mask_ref):
    n = 2 * QUERY_BLOCK
    tri_ref[0] = _triangle(n, True)
    tri_ref[1] = _triangle(n, False)
    mask_ref[...] = jnp.where(_chunk_valid(slice(0, n), QUERY_BLOCK), 1.0, 0.0)


def _valid(mask_ref, rows, older):
    cols = slice(0, 2 * QUERY_BLOCK) if older else slice(QUERY_BLOCK, 2 * QUERY_BLOCK)
    return mask_ref[rows, cols] > 0.5


def _row_total(first):
    lane = lax.broadcasted_iota(jnp.int32, first.shape, 1)
    total = jnp.sum(jnp.where(lane == 0, first, 0.0), axis=1, keepdims=True)
    return jnp.broadcast_to(total, first.shape)


def _pairs_per_step(n_pairs, most):
    return max(g for g in (1, 2, 4) if g <= most and n_pairs % g == 0)


def _lanes(g):
    return slice(g * LANES, (g + 1) * LANES)


def _sweep_older(i, step, carry_ref, first):
    def cond(state):
        n, live = state
        return jnp.logical_and(n < i, live)

    def older(state):
        n, _ = state
        step(i - 1 - n, False)
        return n + 1, jnp.max(carry_ref[...]) > LOG_STICK_FLOOR

    lax.while_loop(cond, older, (first, jnp.max(carry_ref[...]) > LOG_STICK_FLOOR))


class _CommJob:
    def __init__(self, inputs, out_shapes, aliases, n_sems, start, finish):
        self.inputs, self.out_shapes, self.aliases, self.n_sems = list(inputs), list(out_shapes), aliases, n_sems
        self.start, self.finish = start, finish


def _merge_jobs(jobs):
    jobs = [j for j in jobs if j is not None]
    if len(jobs) <= 1:
        return jobs[0] if jobs else None
    spans, aliases = [], {}
    i0 = o0 = s0 = 0
    for j in jobs:
        spans.append((i0, o0, s0))
        aliases.update({i0 + a: o0 + b for a, b in j.aliases.items()})
        i0, o0, s0 = i0 + len(j.inputs), o0 + len(j.out_shapes), s0 + j.n_sems

    def run(which):
        def go(ins, outs, send_sems, recv_sems):
            for j, (i, o, s) in zip(jobs, spans):
                getattr(j, which)(ins[i:i + len(j.inputs)], outs[o:o + len(j.out_shapes)],
                                  send_sems.at[pl.ds(s, j.n_sems)], recv_sems.at[pl.ds(s, j.n_sems)])
        return go

    return _CommJob([a for j in jobs for a in j.inputs], [s for j in jobs for s in j.out_shapes], aliases, s0,
                    run("start"), run("finish"))


def _call_with_job(core_body, job, *, name, grid, in_specs, out_specs, out_shape, scratch_shapes, args):
    sem = ("arbitrary",) * len(grid)
    if job is None:
        res = pl.pallas_call(core_body, name=name, grid=grid, in_specs=in_specs, out_specs=out_specs,
                             out_shape=out_shape, scratch_shapes=scratch_shapes, compiler_params=_params(sem))(*args)
        return list(res), []
    n_in, n_out, n_scr = len(in_specs), len(out_specs), len(scratch_shapes)
    m_in, m_out = len(job.inputs), len(job.out_shapes)

    def body(*refs):
        at = 0
        parts = []
        for count in (n_in, m_in, n_out, m_out, n_scr, 2):
            parts.append(refs[at:at + count])
            at += count
        ins, job_in, outs, job_outs, scratch, (send_sems, recv_sems) = parts
        first = functools.reduce(jnp.logical_and, [pl.program_id(a) == 0 for a in range(len(grid))])
        last = functools.reduce(jnp.logical_and, [pl.program_id(a) == grid[a] - 1 for a in range(len(grid))])

        @pl.when(first)
        def _():
            job.start(job_in, job_outs, send_sems, recv_sems)

        core_body(*ins, *outs, *scratch)

        @pl.when(last)
        def _():
            job.finish(job_in, job_outs, send_sems, recv_sems)

    res = pl.pallas_call(
        body, name=name, grid=grid, in_specs=list(in_specs) + [ANY] * m_in, out_specs=list(out_specs) + [ANY] * m_out,
        out_shape=list(out_shape) + job.out_shapes,
        input_output_aliases={n_in + a: n_out + b for a, b in job.aliases.items()},
        scratch_shapes=list(scratch_shapes) + [pltpu.SemaphoreType.DMA((job.n_sems,)), pltpu.SemaphoreType.DMA((job.n_sems,))],
        compiler_params=_params(sem),
    )(*args, *job.inputs)
    return list(res[:n_out]), list(res[n_out:])


def _run_job(job, name):
    m_in, m_out = len(job.inputs), len(job.out_shapes)

    def body(*refs):
        job_in, job_outs = refs[:m_in], refs[m_in:m_in + m_out]
        send_sems, recv_sems = refs[m_in + m_out:]
        job.start(job_in, job_outs, send_sems, recv_sems)
        job.finish(job_in, job_outs, send_sems, recv_sems)

    return list(pl.pallas_call(
        body, name=name, in_specs=[ANY] * m_in, out_specs=[ANY] * m_out, out_shape=job.out_shapes,
        input_output_aliases=dict(job.aliases),
        scratch_shapes=[pltpu.SemaphoreType.DMA((job.n_sems,)), pltpu.SemaphoreType.DMA((job.n_sems,))],
    )(*job.inputs))


def _attn_fwd(qkv, sb_width, name, job=None):
    L = qkv.shape[0]
    QB = QUERY_BLOCK
    nb = L // QB
    n_pairs = sb_width // LANES
    G = _pairs_per_step(n_pairs, 4)
    W = G * LANES
    scale = 1.0 / math.sqrt(HEAD_DIM)

    def body(q_ref, k_ref, v_ref, o_ref, acc_ref, carry_ref, f32_ref, bf16_ref, tri_ref, mask_ref):
        i = pl.program_id(1)

        @pl.when(i == 0)
        def _():
            _fill_attn_consts(tri_ref, mask_ref)

        q2 = [_stack_heads(q_ref[:, _lanes(g)], scale) for g in range(G)]
        acc_ref[...] = jnp.zeros_like(acc_ref)
        carry_ref[...] = jnp.zeros_like(carry_ref)

        def step(g, j, older, masked):
            n = older + QB
            start = pl.multiple_of(j * QB, QB)
            kb = k_ref[pl.ds(start, n), _lanes(g)]
            vb = v_ref[pl.ds(start, n), _lanes(g)]
            zs, as_, bs = (f32_ref.at[g, t, :, :n] for t in range(3))
            hi, lo = (bf16_ref.at[g, t, :, :n] for t in range(2))
            zs[...] = lax.dot_general(q2[g], kb, (((1,), (1,)), ((), ())), preferred_element_type=F32)
            yield
            for rows in _row_chunks():
                z = zs[rows, :]
                sp = jnp.log(1.0 + jnp.exp(-jnp.abs(z)))
                b = jnp.minimum(-z, 0.0) - sp
                if masked:
                    b = jnp.where(_valid(mask_ref, rows, older),b, 0.0)
                as_[rows, :] = jnp.minimum(z, 0.0) - sp
                bs[rows, :] = b
                _split_to(b, hi, lo, rows)
            yield
            tri = tri_ref[0, :n, :n]
            zs[...] = (jnp.dot(hi[...], tri, preferred_element_type=F32)
                       + jnp.dot(lo[...], tri, preferred_element_type=F32))
            yield
            for rows in _row_chunks():
                excl = zs[rows, :]
                total = _row_total(excl[:, :LANES] + bs[rows, :LANES])
                if not masked:
                    excl = excl + jnp.tile(carry_ref[g, rows, :], (1, n // LANES))
                carry_ref[g, rows, :] += total
                w = jnp.exp(as_[rows, :] + excl)
                if masked:
                    w = jnp.where(_valid(mask_ref, rows, older),w, 0.0)
                _split_to(w, hi, lo, rows)
            yield
            acc_ref[g] += (jnp.dot(hi[...], vb, preferred_element_type=F32)
                           + jnp.dot(lo[...], vb, preferred_element_type=F32))

        @pl.when(i == 0)
        def _():
            _in_lockstep([step(g, 0, 0, True) for g in range(G)])

        @pl.when(i > 0)
        def _():
            _in_lockstep([step(g, i - 1, QB, True) for g in range(G)])
            for g in range(G):
                _sweep_older(i, lambda j, _, g=g: _in_lockstep([step(g, j, 0, False)]), carry_ref.at[g], 1)

        for g in range(G):
            o_ref[:, _lanes(g)] = _unstack_heads(acc_ref[g])

    n_steps = n_pairs // G
    (out,), job_out = _call_with_job(
        body, job, name=name, grid=(n_steps, nb),
        in_specs=[pl.BlockSpec((QB, W), lambda p, i: (i, p)),
                  pl.BlockSpec((L, W), lambda p, i: (0, n_steps + p)),
                  pl.BlockSpec((L, W), lambda p, i: (0, 2 * n_steps + p))],
        out_specs=[pl.BlockSpec((QB, W), lambda p, i: (i, p))],
        out_shape=[jax.ShapeDtypeStruct((L, sb_width), F32)],
        scratch_shapes=[pltpu.VMEM((G, 2 * QB, LANES), F32), pltpu.VMEM((G, 2 * QB, LANES), F32),
                        pltpu.VMEM((G, 3, 2 * QB, 2 * QB), F32), pltpu.VMEM((G, 2, 2 * QB, 2 * QB), BF16),
                        pltpu.VMEM((2, 2 * QB, 2 * QB), BF16), pltpu.VMEM((2 * QB, 2 * QB), F32)],
        args=(qkv, qkv, qkv))
    return out, job_out


def _attn_bwd(qkv, o, dmixed, sb_width, name, job=None):
    L = qkv.shape[0]
    QB = QUERY_BLOCK
    nb = L // QB
    n_pairs = sb_width // LANES
    G = _pairs_per_step(n_pairs, 2)
    W = G * LANES
    scale = 1.0 / math.sqrt(HEAD_DIM)

    def body(q_ref, k_ref, v_ref, o_ref, do_ref, dq_ref, dk_ref, dv_ref,
             dq_acc, dk_acc, dv_acc, ce_ref, cr_ref, dtot_ref, f32_ref, bf16_ref, tri_ref, mask_ref):
        i = pl.program_id(1)

        @pl.when(i == 0)
        def _():
            dk_acc[...] = jnp.zeros_like(dk_acc)
            dv_acc[...] = jnp.zeros_like(dv_acc)
            _fill_attn_consts(tri_ref, mask_ref)

        q2s = [_stack_heads(q_ref[:, _lanes(g)], scale) for g in range(G)]
        do2s = [_stack_heads(do_ref[:, _lanes(g)].astype(BF16)) for g in range(G)]
        ones = jnp.ones((LANES, LANES), BF16)
        for g in range(G):
            ov = o_ref[:, _lanes(g)]
            dtot_ref[g] = _dot2(do2s[g].astype(F32) * jnp.concatenate([ov, ov], axis=0), ones)
        dq_acc[...] = jnp.zeros_like(dq_acc)
        ce_ref[...] = jnp.zeros_like(ce_ref)
        cr_ref[...] = jnp.zeros_like(cr_ref)

        def step(g, j, older, masked):
            n = older + QB
            wide = n // LANES
            q2, do2 = q2s[g], do2s[g]
            start = pl.multiple_of(j * QB, QB)
            kb = k_ref[pl.ds(start, n), _lanes(g)]
            vb = v_ref[pl.ds(start, n), _lanes(g)]
            zs, as_, bs, betas, gs = (f32_ref.at[g, t, :, :n] for t in range(5))
            hi, lo, wb = (bf16_ref.at[g, t, :, :n] for t in range(3))
            nt = (((1,), (1,)), ((), ()))
            zs[...] = lax.dot_general(q2, kb, nt, preferred_element_type=F32)
            gs[...] = lax.dot_general(do2, vb, nt, preferred_element_type=F32)
            yield
            for rows in _row_chunks():
                z = zs[rows, :]
                e = jnp.exp(-jnp.abs(z))
                sp = jnp.log(1.0 + e)
                b = jnp.minimum(-z, 0.0) - sp
                if masked:
                    b = jnp.where(_valid(mask_ref, rows, older),b, 0.0)
                rinv = 1.0 / (1.0 + e)
                as_[rows, :] = jnp.minimum(z, 0.0) - sp
                bs[rows, :] = b
                betas[rows, :] = jnp.where(z >= 0.0, rinv, e * rinv)
                _split_to(b, hi, lo, rows)
            yield
            tri = tri_ref[0, :n, :n]
            zs[...] = (jnp.dot(hi[...], tri, preferred_element_type=F32)
                       + jnp.dot(lo[...], tri, preferred_element_type=F32))
            yield
            for rows in _row_chunks():
                excl = zs[rows, :]
                total = _row_total(excl[:, :LANES] + bs[rows, :LANES])
                if not masked:
                    excl = excl + jnp.tile(ce_ref[g, rows, :], (1, wide))
                ce_ref[g, rows, :] += total
                w = jnp.exp(as_[rows, :] + excl)
                if masked:
                    w = jnp.where(_valid(mask_ref, rows, older),w, 0.0)
                wb[rows, :] = w.astype(BF16)
                gw = w * gs[rows, :]
                gs[rows, :] = gw
                _split_to(gw, hi, lo, rows)
            yield
            tri = tri_ref[1, :n, :n]
            zs[...] = (jnp.dot(hi[...], tri, preferred_element_type=F32)
                       + jnp.dot(lo[...], tri, preferred_element_type=F32))
            yield
            for rows in _row_chunks():
                rinc = zs[rows, :]
                total = _row_total(rinc[:, :LANES])
                if not masked:
                    rinc = rinc + jnp.tile(cr_ref[g, rows, :], (1, wide))
                cr_ref[g, rows, :] += total
                beta = betas[rows, :]
                dz = gs[rows, :] * (1.0 - beta) - beta * (jnp.tile(dtot_ref[g, rows, :], (1, wide)) - rinc)
                if masked:
                    dz = jnp.where(_valid(mask_ref, rows, older),dz, 0.0)
                hi[rows, :] = dz.astype(BF16)
            yield
            dzb = hi[...]
            dq_acc[g] += jnp.dot(dzb, kb, preferred_element_type=F32)
            dk_acc[pl.ds(start, n), _lanes(g)] += lax.dot_general(
                dzb, q2, (((0,), (0,)), ((), ())), preferred_element_type=F32)
            dv_acc[pl.ds(start, n), _lanes(g)] += lax.dot_general(
                wb[...], do2, (((0,), (0,)), ((), ())), preferred_element_type=F32)

        @pl.when(i == 0)
        def _():
            _in_lockstep([step(g, 0, 0, True) for g in range(G)])

        @pl.when(i > 0)
        def _():
            _in_lockstep([step(g, i - 1, QB, True) for g in range(G)])
            for g in range(G):
                _sweep_older(i, lambda j, _, g=g: _in_lockstep([step(g, j, 0, False)]), ce_ref.at[g], 1)

        for g in range(G):
            dq_ref[:, _lanes(g)] = (_unstack_heads(dq_acc[g]) * scale).astype(dq_ref.dtype)

        @pl.when(i == nb - 1)
        def _():
            dk_ref[...] = dk_acc[...].astype(dk_ref.dtype)
            dv_ref[...] = dv_acc[...].astype(dv_ref.dtype)

    n_steps = n_pairs // G
    blk = pl.BlockSpec((QB, W), lambda p, i: (i, p))
    col = pl.BlockSpec((L, W), lambda p, i: (0, p))
    return _call_with_job(
        body, job, name=name, grid=(n_steps, nb),
        in_specs=[blk,
                  pl.BlockSpec((L, W), lambda p, i: (0, n_steps + p)),
                  pl.BlockSpec((L, W), lambda p, i: (0, 2 * n_steps + p)),
                  blk, blk],
        out_specs=[blk, col, col],
        out_shape=[jax.ShapeDtypeStruct((L, sb_width), BF16)] * 3,
        scratch_shapes=[pltpu.VMEM((G, 2 * QB, LANES), F32), pltpu.VMEM((L, W), F32),
                        pltpu.VMEM((L, W), F32), pltpu.VMEM((G, 2 * QB, LANES), F32),
                        pltpu.VMEM((G, 2 * QB, LANES), F32), pltpu.VMEM((G, 2 * QB, LANES), F32),
                        pltpu.VMEM((G, 5, 2 * QB, 2 * QB), F32), pltpu.VMEM((G, 3, 2 * QB, 2 * QB), BF16),
                        pltpu.VMEM((2, 2 * QB, 2 * QB), BF16), pltpu.VMEM((2 * QB, 2 * QB), F32)],
        args=(qkv, qkv, qkv, o, dmixed))


def _conv_tile(L):
    return _pick(L, (384, 128))


def _glu(x, C):
    return x[:, :C] * _sigmoid(x[:, C:])


CONV_CHUNK = 32
SHIFT_TAIL = 24


def _fill_shifted(src_ref, dst_ref):
    n = dst_ref.shape[1]
    for r in range(1, 8):
        dst_ref[r - 1] = src_ref[r:r + n, :]


def _rows_at(src_ref, shifted_ref, start, n):
    q, r = divmod(start, 8)
    if r == 0:
        return src_ref[start:start + n, :]
    return shifted_ref[r - 1, 8 * q:8 * q + n, :]


def _conv_fwd(cacg, dw_w, dw_b, ln_g, ln_b, name, job=None):
    L, C2 = cacg.shape
    C = C2 // 2
    T = _conv_tile(L)
    H = CONV_HALO
    K = dw_w.shape[0]
    CH = CONV_CHUNK

    def body(x_ref, prev_ref, w_ref, b_ref, g_ref, beta_ref, o_ref, y_ref, u_ref, us_ref):
        i = pl.program_id(0)
        u_ref[0:H, :] = jnp.where(i > 0, _glu(prev_ref[...], C), 0.0)
        u_ref[H:, :] = _glu(x_ref[...], C)
        _fill_shifted(u_ref, us_ref)
        for c0 in range(0, T, CH):
            y = jnp.broadcast_to(b_ref[...], (CH, C))
            for k in range(K):
                y = y + w_ref[k:k + 1, :] * _rows_at(u_ref, us_ref, c0 + H - (K - 1) + k, CH)
            y_ref[c0:c0 + CH, :] = y
            mu = jnp.mean(y, axis=-1, keepdims=True)
            yc = y - mu
            rstd = lax.rsqrt(jnp.mean(yc * yc, axis=-1, keepdims=True) + EPS)
            ln = yc * rstd * g_ref[...] + beta_ref[...]
            o_ref[c0:c0 + CH, :] = (ln * _sigmoid(ln)).astype(o_ref.dtype)

    vec = pl.BlockSpec((1, C), lambda i: (0, 0))
    tile = pl.BlockSpec((T, C), lambda i: (i, 0))
    (out, y), arrived = _call_with_job(
        body, job, name=name, grid=(L // T,),
        in_specs=[pl.BlockSpec((T, C2), lambda i: (i, 0)),
                  pl.BlockSpec((H, C2), lambda i: (jnp.maximum(i * (T // H) - 1, 0), 0)),
                  pl.BlockSpec((K, C), lambda i: (0, 0)), vec, vec, vec],
        out_specs=[tile, tile],
        out_shape=[jax.ShapeDtypeStruct((L, C), BF16), jax.ShapeDtypeStruct((L, C), F32)],
        scratch_shapes=[pltpu.VMEM((T + H, C), F32), pltpu.VMEM((7, T + SHIFT_TAIL, C), F32)],
        args=(cacg, cacg, dw_w, dw_b, ln_g, ln_b))
    return out, y, arrived


def _conv_bwd(cacg, y, dmixed, dw_w, ln_g, ln_b, name, job=None):
    L, C2 = cacg.shape
    C = C2 // 2
    T = _conv_tile(L)
    H = CONV_HALO
    K = dw_w.shape[0]
    nt = L // T
    TE = T + H

    CH = CONV_CHUNK

    def body(x_ref, prev_ref, y_ref, ynext_ref, d_ref, dnext_ref, w_ref, g_ref, beta_ref,
             dca_ref, dcg_ref, dwt_ref, db_ref, dg_ref, dbeta_ref, u_ref, us_ref, dy_ref, dys_ref):
        i = pl.program_id(0)
        last = i == nt - 1

        @pl.when(i == 0)
        def _():
            dwt_ref[...] = jnp.zeros_like(dwt_ref)
            db_ref[...] = jnp.zeros_like(db_ref)
            dg_ref[...] = jnp.zeros_like(dg_ref)
            dbeta_ref[...] = jnp.zeros_like(dbeta_ref)

        u_ref[0:H, :] = jnp.where(i > 0, _glu(prev_ref[...], C), 0.0)
        u_ref[H:, :] = _glu(x_ref[...], C)
        _fill_shifted(u_ref, us_ref)
        dg_acc = jnp.zeros((1, C), F32)
        dbeta_acc = jnp.zeros((1, C), F32)
        db_acc = jnp.zeros((1, C), F32)
        for c0 in range(0, TE, CH):
            y = y_ref[c0:c0 + CH, :] if c0 < T else ynext_ref[c0 - T:c0 - T + CH, :]
            mu = jnp.mean(y, axis=-1, keepdims=True)
            yc = y - mu
            rstd = lax.rsqrt(jnp.mean(yc * yc, axis=-1, keepdims=True) + EPS)
            yh = yc * rstd
            ln = yh * g_ref[...] + beta_ref[...]
            s = _sigmoid(ln)
            dout = d_ref[c0:c0 + CH, :] if c0 < T else jnp.where(last, 0.0, dnext_ref[c0 - T:c0 - T + CH, :])
            dln = dout * (s * (1.0 + ln * (1.0 - s)))
            dyh = dln * g_ref[...]
            dy = rstd * (dyh - jnp.mean(dyh, axis=-1, keepdims=True)
                         - yh * jnp.mean(dyh * yh, axis=-1, keepdims=True))
            dy_ref[c0:c0 + CH, :] = dy
            if c0 < T:
                dg_acc = dg_acc + jnp.sum(dln * yh, axis=0, keepdims=True)
                dbeta_acc = dbeta_acc + jnp.sum(dln, axis=0, keepdims=True)
                db_acc = db_acc + jnp.sum(dy, axis=0, keepdims=True)
        dg_ref[...] += dg_acc
        dbeta_ref[...] += dbeta_acc
        db_ref[...] += db_acc
        _fill_shifted(dy_ref, dys_ref)
        for k in range(K):
            dwt_ref[k:k + 1, :] += jnp.sum(
                dy_ref[0:T, :] * _rows_at(u_ref, us_ref, H - (K - 1) + k, T), axis=0, keepdims=True)
        for c0 in range(0, T, CH):
            du = jnp.zeros((CH, C), F32)
            for k in range(K):
                du = du + w_ref[k:k + 1, :] * _rows_at(dy_ref, dys_ref, c0 + (K - 1) - k, CH)
            x = x_ref[c0:c0 + CH, :]
            sg = _sigmoid(x[:, C:])
            dca_ref[c0:c0 + CH, :] = (du * sg).astype(dca_ref.dtype)
            dcg_ref[c0:c0 + CH, :] = (du * x[:, :C] * sg * (1.0 - sg)).astype(dcg_ref.dtype)

    nh = L // H
    vec = pl.BlockSpec((1, C), lambda i: (0, 0))
    row = pl.BlockSpec((T, C), lambda i: (i, 0))
    after = lambda i: jnp.minimum((i + 1) * (T // H), nh - 1)
    return _call_with_job(
        body, job, name=name, grid=(nt,),
        in_specs=[pl.BlockSpec((T, C2), lambda i: (i, 0)),
                  pl.BlockSpec((H, C2), lambda i: (jnp.maximum(i * (T // H) - 1, 0), 0)),
                  row, pl.BlockSpec((H, C), lambda i: (after(i), 0)),
                  pl.BlockSpec((T, C), lambda i: (i, 1)), pl.BlockSpec((H, C), lambda i: (after(i), 1)),
                  pl.BlockSpec((K, C), lambda i: (0, 0)), vec, vec],
        out_specs=[row, row, pl.BlockSpec((H, C), lambda i: (0, 0)), vec, vec, vec],
        out_shape=[jax.ShapeDtypeStruct((L, C), BF16), jax.ShapeDtypeStruct((L, C), BF16),
                   jax.ShapeDtypeStruct((H, C), F32), jax.ShapeDtypeStruct((1, C), F32),
                   jax.ShapeDtypeStruct((1, C), F32), jax.ShapeDtypeStruct((1, C), F32)],
        scratch_shapes=[pltpu.VMEM((T + H, C), F32), pltpu.VMEM((7, T + SHIFT_TAIL, C), F32),
                        pltpu.VMEM((TE, C), F32), pltpu.VMEM((7, T + SHIFT_TAIL, C), F32)],
        args=(cacg, cacg, y, y, dmixed, dmixed, dw_w, ln_g, ln_b))


def _local_step(h0, target, n_meta, seq, norms, conv_p, wts, final_g, gather_behind, reducer):
    mix_g, ffn_g = norms
    dw_w, dw_b, ln_g, ln_b = conv_p
    depth = mix_g.shape[0]
    C = dw_b.shape[-1]
    sbw = (wts["w_in"][0].shape[-1] - 2 * C) // 3
    assert sbw == C, "the mixer halves must have equal width"
    row = lambda a, i: a[i][None, :]

    h = h0
    saved = []
    for i in range(depth):
        hn, proj_qkv = _norm_in_proj(h, row(mix_g, i), wts["w_in"][i], 3 * sbw, f"in_qkv_{i}")
        cacg = _mm_nn([(hn, wts["w_in"][i], 0, 0)], F32, f"in_conv_{i}", cols=(3 * sbw, 2 * C))
        def hosting(kind):
            job, keys = gather_behind.get((kind, i), (None, ()))

            def sink(arrived):
                for (wname, wl), arr in zip(keys, arrived):
                    wts[wname][wl] = arr

            return job, sink

        job, sink = hosting("attn")
        attn, arrived = _attn_fwd(proj_qkv, sbw, f"attn_fwd_{i}", job)
        sink(arrived)
        job, sink = hosting("conv")
        conv, conv_y, arrived = _conv_fwd(cacg, dw_w[i], row(dw_b, i), row(ln_g, i), row(ln_b, i),
                                          f"conv_fwd_{i}", job)
        sink(arrived)
        h_mid = _mm_nn([(attn, wts["w_out"][i], 0, 0), (conv, wts["w_out"][i], 0, 1)], F32, f"out_proj_{i}",
                       residual=h)
        job, sink = hosting("ffn")
        (hn2, g, u, act), arrived = _ffn_up(h_mid, row(ffn_g, i), wts["w_gate_t"][i], wts["w_up_t"][i],
                                            f"ffn_up_{i}", job)
        sink(arrived)
        job, sink = hosting("down")
        h_out = _mm_nn([(act, wts["w_down"][i], 0, 0)], F32, f"down_{i}", residual=h_mid, job=job)
        if job is not None:
            h_out, arrived = h_out
            sink(arrived)
        saved.append((h, hn, proj_qkv, cacg, attn, conv, conv_y, h_mid, hn2, g, u, act))
        h = h_out

    loss, dh, d_final_g = _loss_head(h, final_g[None, :], target, n_meta, seq, "loss_head")

    grads = {k: [None] * depth for k in ("mix_g", "ffn_g", "dw_w", "dw_b", "ln_g", "ln_b")}
    for i in reversed(range(depth)):
        h_in, hn, proj_qkv, cacg, attn, conv, conv_y, h_mid, hn2, g, u, act = saved[i]
        big = {}
        dg, du = _ffn_down_bwd(dh, wts["w_down"][i], g, u, f"ffn_down_bwd_{i}")
        big["w_down"] = _mm_tn(act, dh, f"dw_down_{i}", col_sharded=False)
        dhn2 = _mm_nn([(dg, wts["w_gate_t"][i], 0, 0), (du, wts["w_up_t"][i], 0, 0)], F32, f"d_hn2_{i}")
        big["w_gate_t"] = _mm_tn(dg, hn2, f"dw_gate_{i}", col_sharded=False)
        big["w_up_t"] = _mm_tn(du, hn2, f"dw_up_{i}", col_sharded=False)
        dh, d_ffn = _rmsnorm_bwd(h_mid, row(ffn_g, i), dhn2, dh, f"ffn_norm_bwd_{i}")
        dmixed = _mm_nt([(dh, wts["w_out"][i], 0, 0)], F32, f"d_mixed_{i}")
        big["w_out"] = jnp.concatenate([_mm_tn(attn, dh, f"dw_out_attn_{i}", col_sharded=False, chips=2),
                                        _mm_tn(conv, dh, f"dw_out_conv_{i}", col_sharded=False, chips=2)], axis=0)
        sib_job, sib_sink = reducer.to_sibling(i, big)
        x_job, x_sink = reducer.take()
        (dq, dk, dv), arrived = _attn_bwd(proj_qkv, attn, dmixed, sbw, f"attn_bwd_{i}",
                                          _merge_jobs([sib_job, x_job]))
        sib_sink(arrived[:len(sib_job.out_shapes)])
        x_sink(arrived[len(sib_job.out_shapes):])
        x_job, x_sink = reducer.take()
        (dca, dcg, d_dw, d_b, d_lg, d_lb), arrived = _conv_bwd(
            cacg, conv_y, dmixed, dw_w[i], row(ln_g, i), row(ln_b, i), f"conv_bwd_{i}", x_job)
        x_sink(arrived)
        grads["dw_w"][i] = d_dw[:dw_w.shape[1]]
        grads["dw_b"][i], grads["ln_g"][i], grads["ln_b"][i] = d_b[0], d_lg[0], d_lb[0]
        dproj = jnp.concatenate([dq, dk, dv, dca, dcg], axis=1)
        dhn = _mm_nt([(dproj, wts["w_in"][i], 0, 0)], F32, f"d_hn_{i}")
        reducer.ready(i, {"w_in": _mm_tn(hn, dproj, f"dw_in_{i}", col_sharded=True)})
        if i == 0:
            x_job, x_sink = reducer.take()
            dh, d_mix, arrived = _rmsnorm_bwd(h_in, row(mix_g, i), dhn, dh, f"mix_norm_bwd_{i}", x_job)
            x_sink(arrived)
        else:
            dh, d_mix = _rmsnorm_bwd(h_in, row(mix_g, i), dhn, dh, f"mix_norm_bwd_{i}")
        grads["mix_g"][i], grads["ffn_g"][i] = d_mix[0], d_ffn[0]
    grads["final_g"] = d_final_g[0]
    return loss, dh, grads


ANY = pl.BlockSpec(memory_space=pl.ANY)


def _position():
    return lax.axis_index("x"), lax.axis_index("y"), lax.axis_index("c")


def _chip_at(x, y, k):
    return (1 - x if k & 2 else x), (1 - y if k & 1 else y)


def _half_rows(ref, half, rows, base=0):
    start = pl.multiple_of(base + half * rows, 8)
    lead = (slice(None),) * (len(ref.shape) - 2)
    return ref.at[(*lead, pl.ds(start, rows), slice(None))]


def _gather_job(fulls, shard_shapes, col_sharded):
    n = len(fulls)

    def tools(f_refs, send_sems, recv_sems):
        def block(wi, chip, half):
            _, R, C = shard_shapes[wi]
            if col_sharded[wi]:
                cols = pl.ds(pl.multiple_of(chip * C, LANES), C)
                return f_refs[wi].at[:, pl.ds(pl.multiple_of(half * (R // 2), 8), R // 2), cols]
            return _half_rows(f_refs[wi], half, R // 2, base=chip * R)

        def copy(wi, slot, blk, to):
            return pltpu.make_async_remote_copy(
                src_ref=blk, dst_ref=blk, send_sem=send_sems.at[6 * wi + slot],
                recv_sem=recv_sems.at[6 * wi + slot], device_id=to, device_id_type=MESH)

        return block, copy

    def start(_, f_refs, send_sems, recv_sems):
        block, copy = tools(f_refs, send_sems, recv_sems)
        x, y, c = _position()
        me = 2 * x + y
        for wi in range(n):
            for k in (1, 2, 3):
                copy(wi, k - 1, block(wi, me, c), (*_chip_at(x, y, k), c)).start()

    def finish(_, f_refs, send_sems, recv_sems):
        block, copy = tools(f_refs, send_sems, recv_sems)
        x, y, c = _position()
        me = 2 * x + y
        for wi in range(n):
            for k in (1, 2, 3):
                landed = block(wi, me ^ k, c)
                copy(wi, k - 1, landed, (x, y, c)).wait_recv()
                copy(wi, 2 + k, landed, (x, y, 1 - c)).start()
        for wi in range(n):
            for k in (1, 2, 3):
                copy(wi, 2 + k, block(wi, me ^ k, 1 - c), (x, y, c)).wait_recv()
        for wi in range(n):
            for k in (1, 2, 3):
                copy(wi, k - 1, block(wi, me, c), (x, y, c)).wait_send()
                copy(wi, 2 + k, block(wi, me ^ k, c), (x, y, c)).wait_send()

    return _CommJob(fulls, [jax.ShapeDtypeStruct(f.shape, f.dtype) for f in fulls], {i: i for i in range(n)},
                    6 * n, start, finish)


def _place_shard(w, layer, chip, col_sharded, dtype, name):
    _, R, C = w.shape
    tr = _pick(R, (256, 352, 128, 48))
    nr = R // tr

    def body(chip_ref, w_ref, o_ref):
        o_ref[...] = w_ref[...].astype(dtype)

    if col_sharded:
        shape = (1, R, N_CHIPS * C)
        out_spec = pl.BlockSpec((None, tr, C), lambda r, chip_ref: (0, r, chip_ref[0]))
    else:
        shape = (1, N_CHIPS * R, C)
        out_spec = pl.BlockSpec((None, tr, C), lambda r, chip_ref: (0, chip_ref[0] * nr + r, 0))
    grid_spec = pltpu.PrefetchScalarGridSpec(
        num_scalar_prefetch=1, grid=(nr,),
        in_specs=[pl.BlockSpec((None, tr, C), lambda r, chip_ref: (layer, r, 0))], out_specs=out_spec)
    return pl.pallas_call(
        body, name=name, grid_spec=grid_spec, out_shape=jax.ShapeDtypeStruct(shape, dtype),
        compiler_params=_params(("parallel",)),
    )(chip, w)


def _sibling_job(grads):
    n = len(grads)

    def copies(g_refs, l_refs, send_sems, recv_sems):
        x, y, c = _position()
        return [pltpu.make_async_remote_copy(
            src_ref=_half_rows(g_refs[wi], 1 - c, grads[wi].shape[1] // 2), dst_ref=l_refs[wi],
            send_sem=send_sems.at[wi], recv_sem=recv_sems.at[wi],
            device_id=(x, y, 1 - c), device_id_type=MESH) for wi in range(n)]

    def start(*refs):
        for cp in copies(*refs):
            cp.start()

    def finish(*refs):
        for cp in copies(*refs):
            cp.wait()

    outs = [jax.ShapeDtypeStruct((g.shape[0], g.shape[1] // 2, g.shape[2]), g.dtype) for g in grads]
    return _CommJob(grads, outs, {}, n, start, finish)


def _chip_sum(g, landed, core, name):
    _, R, C = g.shape
    hr = R // 2
    tr = _pick(hr, (256, 352, 128))
    nr = hr // tr

    def body(c_ref, g_ref, l_ref, o_ref):
        o_ref[...] = (g_ref[...] + l_ref[...]).astype(BF16)

    grid_spec = pltpu.PrefetchScalarGridSpec(
        num_scalar_prefetch=1, grid=(N_CHIPS, nr),
        in_specs=[pl.BlockSpec((None, tr, C), lambda j, r, c_ref: (j, c_ref[0] * nr + r, 0)),
                  pl.BlockSpec((None, tr, C), lambda j, r, c_ref: (j, r, 0))],
        out_specs=pl.BlockSpec((None, tr, C), lambda j, r, c_ref: (j, r, 0)))
    return pl.pallas_call(
        body, name=name, grid_spec=grid_spec, out_shape=jax.ShapeDtypeStruct((N_CHIPS, hr, C), BF16),
        compiler_params=_params(("parallel", "parallel")),
    )(core, g, landed)


def _across_job(parts):
    n = len(parts)

    def copy(p_refs, l_refs, send_sems, recv_sems, wi, k, to):
        x, y, _ = _position()
        me = 2 * x + y
        return pltpu.make_async_remote_copy(
            src_ref=p_refs[wi].at[me ^ k], dst_ref=l_refs[wi].at[me],
            send_sem=send_sems.at[3 * wi + k - 1], recv_sem=recv_sems.at[3 * wi + k - 1],
            device_id=to, device_id_type=MESH)

    def start(p_refs, l_refs, send_sems, recv_sems):
        x, y, c = _position()
        for wi in range(n):
            for k in (1, 2, 3):
                copy(p_refs, l_refs, send_sems, recv_sems, wi, k, (*_chip_at(x, y, k), c)).start()

    def finish(p_refs, l_refs, send_sems, recv_sems):
        x, y, c = _position()
        me = 2 * x + y
        for wi in range(n):
            for k in (1, 2, 3):
                slot = l_refs[wi].at[me ^ k]
                pltpu.make_async_remote_copy(
                    src_ref=slot, dst_ref=slot, send_sem=send_sems.at[3 * wi + k - 1],
                    recv_sem=recv_sems.at[3 * wi + k - 1], device_id=(x, y, c), device_id_type=MESH).wait_recv()
        for wi in range(n):
            for k in (1, 2, 3):
                copy(p_refs, l_refs, send_sems, recv_sems, wi, k, (x, y, c)).wait_send()

    return _CommJob(parts, [jax.ShapeDtypeStruct(p.shape, p.dtype) for p in parts], {}, 3 * n, start, finish)


class _Reducer:
    def __init__(self, core):
        self.core, self.parts, self.across, self.pending = core, {}, {}, []

    def to_sibling(self, layer, big):
        names = list(big)
        flat = [big[k] for k in names]

        def sink(landed):
            for k, g, la in zip(names, flat, landed):
                self.parts[k, layer] = _chip_sum(g, la, self.core, f"chip_sum_{k}_{layer}")
                self.pending.append((k, layer))

        return _sibling_job(flat), sink

    def ready(self, layer, big):
        job, sink = self.to_sibling(layer, big)
        sink(_run_job(job, f"grads_to_sibling_{next(iter(big))}_{layer}"))

    def take(self):
        keys, self.pending = self.pending, []
        if not keys:
            return None, lambda results: None

        def sink(results):
            self.across.update(zip(keys, results))

        return _across_job([self.parts[key] for key in keys]), sink


def _sum_chips(parts, landed, where, layer, depth, prev, name):
    _, hr, C = landed.shape
    tr = _pick(hr, (256, 352, 128))
    nr = hr // tr

    def body(*refs):
        own_ref, slots, o_ref = refs[1], refs[2:2 + N_CHIPS], refs[-1]
        chip = refs[0][0]
        total = None
        for q in range(N_CHIPS):
            term = jnp.where(chip == q, own_ref[...], slots[q][...]).astype(F32)
            total = term if total is None else total + term
        o_ref[...] = total

    def slot_spec(q):
        return pl.BlockSpec((None, tr, C), lambda r, w: (jnp.where(w[0] == q, (q + 1) % N_CHIPS, q), r, 0))

    in_specs = [pl.BlockSpec((None, tr, C), lambda r, w: (w[0], r, 0))] + [slot_spec(q) for q in range(N_CHIPS)]
    args = [where, parts] + [landed] * N_CHIPS
    aliases = {}
    if prev is not None:
        in_specs.append(ANY)
        args.append(prev)
        aliases = {len(args) - 1: 0}
    grid_spec = pltpu.PrefetchScalarGridSpec(
        num_scalar_prefetch=1, grid=(nr,), in_specs=in_specs,
        out_specs=pl.BlockSpec((None, tr, C), lambda r, w: (layer, w[1] * nr + r, 0)))
    return pl.pallas_call(
        body, name=name, grid_spec=grid_spec, out_shape=jax.ShapeDtypeStruct((depth, 2 * hr, C), F32),
        input_output_aliases=aliases, compiler_params=_params(("parallel",)),
    )(*args)


def _rs_join_halves(reduced):
    n = len(reduced)

    def body(*refs):
        o_refs = refs[n:2 * n]
        send_sems, recv_sems = refs[2 * n:]
        x, y, c = _position()
        sent = []
        for wi in range(n):
            hr = reduced[wi].shape[1] // 2
            mine = _half_rows(o_refs[wi], c, hr)
            cp = pltpu.make_async_remote_copy(
                src_ref=mine, dst_ref=mine, send_sem=send_sems.at[wi], recv_sem=recv_sems.at[wi],
                device_id=(x, y, 1 - c), device_id_type=MESH)
            cp.start()
            sent.append(cp)
        for wi in range(n):
            hr = reduced[wi].shape[1] // 2
            theirs = _half_rows(o_refs[wi], 1 - c, hr)
            pltpu.make_async_remote_copy(
                src_ref=theirs, dst_ref=theirs, send_sem=send_sems.at[wi], recv_sem=recv_sems.at[wi],
                device_id=(x, y, c), device_id_type=MESH).wait_recv()
        for cp in sent:
            cp.wait_send()

    return pl.pallas_call(
        body, name="grads_join_halves", out_shape=[jax.ShapeDtypeStruct(r.shape, r.dtype) for r in reduced],
        in_specs=[ANY] * n, out_specs=[ANY] * n, input_output_aliases={i: i for i in range(n)},
        scratch_shapes=[pltpu.SemaphoreType.DMA((n,)), pltpu.SemaphoreType.DMA((n,))],
    )(*reduced)


def _small_allreduce(vec):
    rows = vec.shape[0]

    def body(v_ref, o_ref, land, send_sems, recv_sems):
        x, y, c = _position()
        me = 4 * x + 2 * y + c
        land[0] = v_ref[...]
        sent = []
        for k in range(1, N_DEV):
            to = (1 - x if k & 4 else x, 1 - y if k & 2 else y, 1 - c if k & 1 else c)
            cp = pltpu.make_async_remote_copy(
                src_ref=v_ref, dst_ref=land.at[k], send_sem=send_sems.at[k - 1], recv_sem=recv_sems.at[k - 1],
                device_id=to, device_id_type=MESH)
            cp.start()
            sent.append(cp)
        for cp in sent:
            cp.wait_recv()
        acc = land[me]
        for e in range(1, N_DEV):
            acc = acc + land[me ^ e]
        o_ref[...] = acc
        for cp in sent:
            cp.wait_send()

    vmem = pl.BlockSpec(memory_space=pltpu.VMEM)
    return pl.pallas_call(
        body, name="small_allreduce", out_shape=jax.ShapeDtypeStruct(vec.shape, F32),
        in_specs=[vmem], out_specs=vmem,
        scratch_shapes=[pltpu.VMEM((N_DEV, rows, LANES), F32), pltpu.SemaphoreType.DMA((N_DEV - 1,)),
                        pltpu.SemaphoreType.DMA((N_DEV - 1,))],
    )(vec)


def _adam_math(w, g, m, v):
    m = ADAM_B1 * m + (1.0 - ADAM_B1) * g
    v = ADAM_B2 * v + (1.0 - ADAM_B2) * jnp.square(g)
    m_hat = m / (1.0 - ADAM_B1 ** ADAM_STEP)
    v_hat = v / (1.0 - ADAM_B2 ** ADAM_STEP)
    delta = -ADAM_LR * (m_hat / (jnp.sqrt(v_hat) + ADAM_EPS) + ADAM_WD * w)
    return delta, m, v


def _adam(w, g, m, v, name):
    def body(w_ref, g_ref, m_ref, v_ref, d_ref, nm_ref, nv_ref):
        d_ref[...], nm_ref[...], nv_ref[...] = _adam_math(w_ref[...], g_ref[...], m_ref[...], v_ref[...])

    if w.ndim == 3:
        lyr, R, C = w.shape
        tr = _pick(R, (256, 352, 128))
        blk = pl.BlockSpec((None, tr, C), lambda l, r: (l, r, 0))
        grid, sem = (lyr, R // tr), ("parallel", "parallel")
    else:
        blk = pl.BlockSpec(w.shape, lambda: (0, 0))
        grid, sem = (), None
    return pl.pallas_call(
        body, name=name, grid=grid, in_specs=[blk] * 4, out_specs=[blk] * 3,
        out_shape=[jax.ShapeDtypeStruct(w.shape, F32)] * 3, compiler_params=_params(sem),
    )(w, g, m, v)


def _rows(a, pad_to=8):
    r = a.reshape(-1, LANES)
    extra = (-r.shape[0]) % pad_to
    return jnp.pad(r, ((0, extra), (0, 0))) if extra else r


def _pack(arrays):
    return jnp.concatenate([_rows(a) for a in arrays], axis=0)


def _unpack(slab, shapes):
    out, at = [], 0
    for shp in shapes:
        nrow = math.prod(shp) // LANES
        out.append(slab[at:at + nrow].reshape(shp))
        at += nrow + (-nrow) % 8
    return out


BIG = ("w_in", "w_out", "w_gate_t", "w_up_t", "w_down")
BIG_COL_SHARDED = (True, False, False, False, False)
TRANSPOSED = {"w_gate_t": "w_gate", "w_up_t": "w_up"}


def kernel(x, meta_tokens, mix_norm_g, w_in, conv_dw_w, conv_dw_b, conv_ln_g, conv_ln_b, w_out, ffn_norm_g, w_gate, w_up, w_down, final_norm_g, loss_target, m_meta_tokens, m_mix_norm_g, m_w_in, m_conv_dw_w, m_conv_dw_b, m_conv_ln_g, m_conv_ln_b, m_w_out, m_ffn_norm_g, m_w_gate, m_w_up, m_w_down, m_final_norm_g, v_meta_tokens, v_mix_norm_g, v_w_in, v_conv_dw_w, v_conv_dw_b, v_conv_ln_g, v_conv_ln_b, v_w_out, v_ffn_norm_g, v_w_gate, v_w_up, v_w_down, v_final_norm_g):
    n_meta, seq = meta_tokens.shape[0], x.shape[1]
    D = x.shape[2]
    depth, taps, c_shard = conv_dw_w.shape
    C = conv_dw_b.shape[-1]
    chip = (2 * lax.axis_index("x") + lax.axis_index("y")).astype(jnp.int32)
    core = lax.axis_index("c").astype(jnp.int32).reshape(1)
    chip1 = chip.reshape(1)
    where = jnp.concatenate([chip1, core])
    big_w = dict(w_in=w_in, w_out=w_out, w_gate=w_gate, w_up=w_up, w_down=w_down)
    big_m = dict(w_in=m_w_in, w_out=m_w_out, w_gate=m_w_gate, w_up=m_w_up, w_down=m_w_down)
    big_v = dict(w_in=v_w_in, w_out=v_w_out, w_gate=v_w_gate, w_up=v_w_up, w_down=v_w_down)

    small_shard = _pack([conv_dw_w, meta_tokens])[None]
    to_send = {k: jnp.swapaxes(big_w[TRANSPOSED[k]], 1, 2) if k in TRANSPOSED else big_w[k] for k in BIG}
    col = dict(zip(BIG, BIG_COL_SHARDED))
    wts = {k: [_place_shard(to_send[k], l, chip1, col[k], BF16, f"place_{k}_{l}") for l in range(depth)] for k in BIG}
    small_placed = _place_shard(small_shard, 0, chip1, False, F32, "place_small")

    def gather_job(keys, extra=()):
        arrays = [wts[k][l] for k, l in keys] + list(extra)
        shapes = [(1,) + to_send[k].shape[1:] for k, _ in keys] + [(1,) + small_shard.shape[1:]] * len(extra)
        return _gather_job(arrays, shapes, [col[k] for k, _ in keys] + [False] * len(extra))

    first_keys = [("w_in", 0)]
    *first, small_full = _run_job(gather_job(first_keys, [small_placed]), "gather_first")
    for (k, l), arr in zip(first_keys, first):
        wts[k][l] = arr
    behind_keys = {("attn", 0): [("w_out", 0), ("w_gate_t", 0), ("w_up_t", 0)], ("conv", 0): [("w_down", 0)]}
    for l in range(1, depth):
        behind_keys["ffn", l - 1] = [("w_in", l), ("w_out", l), ("w_gate_t", l)]
        behind_keys["down", l - 1] = [("w_up_t", l)]
        behind_keys["attn", l] = [("w_down", l)]
    gather_behind = {host: (gather_job(keys), keys) for host, keys in behind_keys.items()}

    rows_shard = small_shard.shape[1]
    dw_full, meta_full = [], []
    for j in range(N_CHIPS):
        dwj, mj = _unpack(small_full[0, j * rows_shard:(j + 1) * rows_shard],
                          [conv_dw_w.shape, meta_tokens.shape])
        dw_full.append(dwj)
        meta_full.append(mj)
    dw_w_full = jnp.concatenate(dw_full, axis=2)
    meta = jnp.concatenate(meta_full, axis=1)

    L = n_meta + seq
    Lp = -(-L // QUERY_BLOCK) * QUERY_BLOCK
    h0 = jnp.concatenate([meta, x[0], jnp.zeros((Lp - L, D), F32)], axis=0)
    target = jnp.pad(loss_target[0], ((n_meta, Lp - L), (0, 0)))
    reducer = _Reducer(core)
    loss, dh0, grads = _local_step(h0, target, n_meta, seq, (mix_norm_g, ffn_norm_g),
                                   (dw_w_full, conv_dw_b, conv_ln_g, conv_ln_b), wts, final_norm_g,
                                   gather_behind, reducer)
    loss = lax.psum(loss[0, 0], ("x", "y", "c"))
    grad_x = dh0[n_meta:L][None]

    reduced = []
    for k in BIG:
        arr = None
        for l in range(depth):
            arr = _sum_chips(reducer.parts[k, l], reducer.across[k, l], where, l, depth, arr, f"sum_chips_{k}_{l}")
        reduced.append(arr)
    big_g = dict(zip(BIG, _rs_join_halves(reduced)))

    small_names = ("mix_g", "ffn_g", "dw_b", "ln_g", "ln_b", "dw_w")
    small = [jnp.stack(grads[k]) for k in small_names] + [grads["final_g"], dh0[:n_meta]]
    small_shapes = [a.shape for a in small]
    g_mix, g_ffn, g_dwb, g_lng, g_lnb, g_dww, g_final, g_meta = _unpack(_small_allreduce(_pack(small)), small_shapes)
    g_dww = lax.dynamic_slice_in_dim(g_dww, chip * c_shard, c_shard, axis=2)
    g_meta = lax.dynamic_slice_in_dim(g_meta, chip * meta_tokens.shape[1], meta_tokens.shape[1], axis=1)

    out_g, out_d, out_m, out_v = {}, {}, {}, {}
    for kk in BIG:
        k = TRANSPOSED.get(kk, kk)
        view = (lambda a: jnp.swapaxes(a, 1, 2)) if kk in TRANSPOSED else (lambda a: a)
        res = _adam(view(big_w[k]), big_g[kk], view(big_m[k]), view(big_v[k]), f"adam_{k}")
        out_g[k] = view(big_g[kk])
        out_d[k], out_m[k], out_v[k] = (view(a) for a in res)
    small_order = ("meta_tokens", "mix_norm_g", "conv_dw_w", "conv_dw_b", "conv_ln_g", "conv_ln_b",
                   "ffn_norm_g", "final_norm_g")
    sw = dict(meta_tokens=meta_tokens, mix_norm_g=mix_norm_g, conv_dw_w=conv_dw_w, conv_dw_b=conv_dw_b,
              conv_ln_g=conv_ln_g, conv_ln_b=conv_ln_b, ffn_norm_g=ffn_norm_g, final_norm_g=final_norm_g)
    sm = dict(meta_tokens=m_meta_tokens, mix_norm_g=m_mix_norm_g, conv_dw_w=m_conv_dw_w, conv_dw_b=m_conv_dw_b,
              conv_ln_g=m_conv_ln_g, conv_ln_b=m_conv_ln_b, ffn_norm_g=m_ffn_norm_g, final_norm_g=m_final_norm_g)
    sv = dict(meta_tokens=v_meta_tokens, mix_norm_g=v_mix_norm_g, conv_dw_w=v_conv_dw_w, conv_dw_b=v_conv_dw_b,
              conv_ln_g=v_conv_ln_g, conv_ln_b=v_conv_ln_b, ffn_norm_g=v_ffn_norm_g, final_norm_g=v_final_norm_g)
    sg = dict(meta_tokens=g_meta, mix_norm_g=g_mix, conv_dw_w=g_dww, conv_dw_b=g_dwb, conv_ln_g=g_lng,
              conv_ln_b=g_lnb, ffn_norm_g=g_ffn, final_norm_g=g_final)
    def slab(d):
        return _pack([d[k] for k in small_order])
    shapes = [sw[k].shape for k in small_order]
    deltas = _adam(slab(sw), slab(sg), slab(sm), slab(sv), "adam_small")
    for res, dst in zip(deltas, (out_d, out_m, out_v)):
        dst.update(zip(small_order, _unpack(res, shapes)))
    out_g.update(sg)

    order = ("meta_tokens", "mix_norm_g", "w_in", "conv_dw_w", "conv_dw_b", "conv_ln_g", "conv_ln_b", "w_out",
             "ffn_norm_g", "w_gate", "w_up", "w_down", "final_norm_g")
    return (loss, grad_x, *[out_g[k] for k in order], *[out_d[k] for k in order],
            *[out_m[k] for k in order], *[out_v[k] for k in order])
```

```python
import functools
import math

import jax
import jax.numpy as jnp
from jax import lax
from jax.experimental import pallas as pl
from jax.experimental.pallas import tpu as pltpu

F32 = jnp.float32
BF16 = jnp.bfloat16
MESH = pl.DeviceIdType.MESH

EPS = 1e-6
QUERY_BLOCK = 128
LANES = 128
HEAD_DIM = 64
LOG_STICK_FLOOR = -40.0
CONV_HALO = 32
N_CHIPS = 4
N_DEV = 8
VMEM_LIMIT = 56 * 1024 * 1024

ADAM_LR = 0.001
ADAM_B1 = 0.9
ADAM_B2 = 0.999
ADAM_EPS = 1e-08
ADAM_WD = 0.01
ADAM_STEP = 10


def _pick(n, prefs):
    for p in prefs:
        if n % p == 0:
            return p
    return n


def _params(sem=None):
    return pltpu.CompilerParams(dimension_semantics=sem, vmem_limit_bytes=VMEM_LIMIT)


def _sigmoid(x):
    return 1.0 / (1.0 + jnp.exp(-x))


def _rmsnorm_bwd(h, g, dy, dh_in, name, job=None):
    L, D = h.shape
    T = _pick(L, (384, 128))

    def body(h_ref, g_ref, dy_ref, dhin_ref, dh_ref, dg_ref):
        x = h_ref[...]
        dyv = dy_ref[...]
        r = lax.rsqrt(jnp.mean(x * x, axis=-1, keepdims=True) + EPS)
        xh = x * r
        dxh = dyv * g_ref[...]
        dh_ref[...] = dhin_ref[...] + r * (dxh - xh * jnp.mean(dxh * xh, axis=-1, keepdims=True))

        @pl.when(pl.program_id(0) == 0)
        def _():
            dg_ref[...] = jnp.zeros_like(dg_ref)

        dg_ref[...] += jnp.sum(dyv * xh, axis=0, keepdims=True)

    row = pl.BlockSpec((T, D), lambda i: (i, 0))
    vec = pl.BlockSpec((1, D), lambda i: (0, 0))
    (dh, dg), arrived = _call_with_job(
        body, job, name=name, grid=(L // T,),
        in_specs=[row, vec, row, row], out_specs=[row, vec],
        out_shape=[jax.ShapeDtypeStruct((L, D), F32), jax.ShapeDtypeStruct((1, D), F32)],
        scratch_shapes=[], args=(h, g, dy, dh_in))
    return (dh, dg) if job is None else (dh, dg, arrived)


def _loss_head(h, g, target, n_meta, seq, name):
    L, D = h.shape
    T = _pick(L, (384, 128))

    def body(h_ref, g_ref, t_ref, loss_ref, dh_ref, dg_ref):
        i = pl.program_id(0)
        x = h_ref[...]
        gv = g_ref[...]
        r = lax.rsqrt(jnp.mean(x * x, axis=-1, keepdims=True) + EPS)
        xh = x * r
        y = xh * gv
        rows = i * T + lax.broadcasted_iota(jnp.int32, (T, 1), 0)
        live = (rows >= n_meta) & (rows < n_meta + seq)
        diff = jnp.where(live, y - t_ref[...], 0.0)
        dyv = diff / D
        dxh = dyv * gv
        dh_ref[...] = r * (dxh - xh * jnp.mean(dxh * xh, axis=-1, keepdims=True))

        @pl.when(i == 0)
        def _():
            dg_ref[...] = jnp.zeros_like(dg_ref)
            loss_ref[...] = jnp.zeros_like(loss_ref)

        dg_ref[...] += jnp.sum(dyv * xh, axis=0, keepdims=True)
        per_row = jnp.mean(diff * diff, axis=-1, keepdims=True)
        loss_ref[...] += 0.5 * jnp.sum(per_row, axis=0, keepdims=True)

    row = pl.BlockSpec((T, D), lambda i: (i, 0))
    vec = pl.BlockSpec((1, D), lambda i: (0, 0))
    one = pl.BlockSpec((1, 1), lambda i: (0, 0))
    return pl.pallas_call(
        body, name=name, grid=(L // T,),
        in_specs=[row, vec, row], out_specs=[one, row, vec],
        out_shape=[jax.ShapeDtypeStruct((1, 1), F32), jax.ShapeDtypeStruct((L, D), F32),
                   jax.ShapeDtypeStruct((1, D), F32)],
        compiler_params=_params(("arbitrary",)),
    )(h, g, target)


def _ffn_tiles(M, F):
    return _pick(M, (352, 384, 128)), _pick(F, (1408, 512, 256, 128))


def _resident(shape):
    return pl.BlockSpec((None,) + tuple(shape[1:]), lambda *_: (0,) * len(shape), pipeline_mode=pl.Buffered(1))


def _normed_rows(h_ref, g_ref, hn_ref, keep_ref):
    @pl.when(pl.program_id(1) == 0)
    def _():
        x = h_ref[...]
        r = lax.rsqrt(jnp.mean(x * x, axis=-1, keepdims=True) + EPS)
        keep_ref[...] = (x * r * g_ref[...]).astype(BF16)
        hn_ref[...] = keep_ref[...]


def _norm_in_proj(h, g, w_in, n_cols, name):
    M, D = h.shape
    tm = _pick(M, (1056, 384, 128))
    tn = _pick(n_cols, (512, 256, 128))

    def body(h_ref, g_ref, w_ref, hn_ref, o_ref, keep_ref):
        _normed_rows(h_ref, g_ref, hn_ref, keep_ref)
        o_ref[...] = jnp.dot(keep_ref[...], w_ref[...], preferred_element_type=F32).astype(o_ref.dtype)

    rows = pl.BlockSpec((tm, D), lambda i, j: (i, 0))
    return pl.pallas_call(
        body, name=name, grid=(M // tm, n_cols // tn),
        in_specs=[rows, pl.BlockSpec((1, D), lambda i, j: (0, 0)), pl.BlockSpec((None, D, tn), lambda i, j: (0, 0, j))],
        out_specs=[rows, pl.BlockSpec((tm, tn), lambda i, j: (i, j))],
        out_shape=[jax.ShapeDtypeStruct((M, D), BF16), jax.ShapeDtypeStruct((M, n_cols), BF16)],
        scratch_shapes=[pltpu.VMEM((tm, D), BF16)],
        compiler_params=_params(("parallel", "arbitrary")),
    )(h, g, w_in)


def _ffn_up(h, norm_g, w_gate_t, w_up_t, name, job=None):
    M, D = h.shape
    F = w_gate_t.shape[1]
    tm, tf = _ffn_tiles(M, F)
    nt = (((1,), (1,)), ((), ()))

    def body(h_ref, ng_ref, wg_ref, wu_ref, hn_ref, g_ref, u_ref, a_ref):
        x = h_ref[...]
        r = lax.rsqrt(jnp.mean(x * x, axis=-1, keepdims=True) + EPS)
        hv = (x * r * ng_ref[...]).astype(BF16)
        hn_ref[...] = hv
        for c in range(0, F, tf):
            gv = lax.dot_general(hv, wg_ref[c:c + tf, :], nt, preferred_element_type=F32)
            uv = lax.dot_general(hv, wu_ref[c:c + tf, :], nt, preferred_element_type=F32)
            g_ref[:, c:c + tf] = gv.astype(g_ref.dtype)
            u_ref[:, c:c + tf] = uv.astype(u_ref.dtype)
            a_ref[:, c:c + tf] = (gv * _sigmoid(gv) * uv).astype(a_ref.dtype)

    rows = pl.BlockSpec((tm, D), lambda i: (i, 0))
    wide = pl.BlockSpec((tm, F), lambda i: (i, 0))
    return _call_with_job(
        body, job, name=name, grid=(M // tm,),
        in_specs=[rows, pl.BlockSpec((1, D), lambda i: (0, 0)), _resident(w_gate_t.shape), _resident(w_up_t.shape)],
        out_specs=[rows, wide, wide, wide],
        out_shape=[jax.ShapeDtypeStruct((M, D), BF16), jax.ShapeDtypeStruct((M, F), BF16),
                   jax.ShapeDtypeStruct((M, F), BF16), jax.ShapeDtypeStruct((M, F), BF16)],
        scratch_shapes=[], args=(h, norm_g, w_gate_t, w_up_t))


def _ffn_down_bwd(dh, w_down, g, u, name):
    M, D = dh.shape
    F = g.shape[1]
    tm, tf = _ffn_tiles(M, F)

    def body(d_ref, w_ref, g_ref, u_ref, dg_ref, du_ref):
        dhv = d_ref[...].astype(BF16)
        for c in range(0, F, tf):
            dv = lax.dot_general(dhv, w_ref[c:c + tf, :], (((1,), (1,)), ((), ())), preferred_element_type=F32)
            gv = g_ref[:, c:c + tf].astype(F32)
            s = _sigmoid(gv)
            du_ref[:, c:c + tf] = (dv * (gv * s)).astype(du_ref.dtype)
            dg_ref[:, c:c + tf] = (dv * u_ref[:, c:c + tf].astype(F32)
                                   * (s * (1.0 + gv * (1.0 - s)))).astype(dg_ref.dtype)

    wide = pl.BlockSpec((tm, F), lambda i: (i, 0))
    return pl.pallas_call(
        body, name=name, grid=(M // tm,),
        in_specs=[pl.BlockSpec((tm, D), lambda i: (i, 0)), _resident(w_down.shape), wide, wide],
        out_specs=[wide, wide],
        out_shape=[jax.ShapeDtypeStruct((M, F), BF16), jax.ShapeDtypeStruct((M, F), BF16)],
        compiler_params=_params(("parallel",)),
    )(dh, w_down, g, u)


def _mm_nn(pairs, out_dtype, name, residual=None, cols=None, job=None):
    M = pairs[0][0].shape[0]
    col0, N = cols if cols is not None else (0, pairs[0][1].shape[-1])
    tm = _pick(M, (1056, 384, 128))
    shallow = sum(p[0].shape[1] for p in pairs) <= 1024
    tn = _pick(math.gcd(N, col0) if col0 else N, ((1024,) if shallow else ()) + (640, 512, 256, 128))
    jb = col0 // tn
    n = len(pairs)

    def body(*refs):
        a_refs, w_refs = refs[:n], refs[n:2 * n]
        o_ref = refs[-1]
        acc = None
        for a_ref, w_ref in zip(a_refs, w_refs):
            d = jnp.dot(a_ref[...].astype(BF16), w_ref[...], preferred_element_type=F32)
            acc = d if acc is None else acc + d
        if residual is not None:
            acc = acc + refs[2 * n][...]
        o_ref[...] = acc.astype(o_ref.dtype)

    in_specs = [pl.BlockSpec((tm, a.shape[1]), lambda i, j: (i, 0)) for a, _, _, _ in pairs]
    for a, _, layer, kblk in pairs:
        in_specs.append(pl.BlockSpec((None, a.shape[1], tn), functools.partial(lambda i, j, l, kb: (l, kb, j + jb), l=layer, kb=kblk)))
    args = [p[0] for p in pairs] + [p[1] for p in pairs]
    if residual is not None:
        in_specs.append(pl.BlockSpec((tm, tn), lambda i, j: (i, j)))
        args.append(residual)
    (out,), arrived = _call_with_job(
        body, job, name=name, grid=(M // tm, N // tn), in_specs=in_specs,
        out_specs=[pl.BlockSpec((tm, tn), lambda i, j: (i, j))],
        out_shape=[jax.ShapeDtypeStruct((M, N), out_dtype)], scratch_shapes=[], args=args)
    return out if job is None else (out, arrived)


def _mm_nt(pairs, out_dtype, name):
    M = pairs[0][0].shape[0]
    K = pairs[0][1].shape[1]
    tm = _pick(M, (1056, 384, 128))
    shallow = sum(p[0].shape[1] for p in pairs) <= 1024
    tk = _pick(K, ((1024,) if shallow else ()) + (512, 1408, 256, 128))
    n = len(pairs)

    def body(*refs):
        d_refs, w_refs = refs[:n], refs[n:2 * n]
        o_ref = refs[-1]
        acc = None
        for d_ref, w_ref in zip(d_refs, w_refs):
            d = lax.dot_general(d_ref[...].astype(BF16), w_ref[...], (((1,), (1,)), ((), ())),
                                preferred_element_type=F32)
            acc = d if acc is None else acc + d
        o_ref[...] = acc.astype(o_ref.dtype)

    in_specs = [pl.BlockSpec((tm, d.shape[1]), lambda i, j: (i, 0)) for d, _, _, _ in pairs]
    for d, _, layer, cblk in pairs:
        in_specs.append(pl.BlockSpec((None, tk, d.shape[1]), functools.partial(lambda i, j, l, cb: (l, j, cb), l=layer, cb=cblk)))
    args = [p[0] for p in pairs] + [p[1] for p in pairs]
    return pl.pallas_call(
        body, name=name, grid=(M // tm, K // tk), in_specs=in_specs,
        out_specs=pl.BlockSpec((tm, tk), lambda i, j: (i, j)),
        out_shape=jax.ShapeDtypeStruct((M, K), out_dtype),
        compiler_params=_params(("parallel", "parallel")),
    )(*args)


def _mm_tn(a, b, name, col_sharded, chips=N_CHIPS):
    M, K = a.shape
    N = b.shape[1]
    tm = _pick(M, (1056, 384, 128))
    tk = _pick(K, (1024, 1408, 512, 256, 128))
    tn = N // N_CHIPS if col_sharded else _pick(N, (512, 128))

    def body(a_ref, b_ref, o_ref):
        @pl.when(pl.program_id(2) == 0)
        def _():
            o_ref[...] = jnp.zeros_like(o_ref)

        o_ref[...] += lax.dot_general(a_ref[...].astype(BF16), b_ref[...].astype(BF16),
                                      (((0,), (0,)), ((), ())), preferred_element_type=F32)

    if col_sharded:
        out_shape = jax.ShapeDtypeStruct((N_CHIPS, K, tn), F32)
        out_spec = pl.BlockSpec((None, tk, tn), lambda k, j, m: (j, k, 0))
    else:
        out_shape = jax.ShapeDtypeStruct((K, N), F32)
        out_spec = pl.BlockSpec((tk, tn), lambda k, j, m: (k, j))
    out = pl.pallas_call(
        body, name=name, grid=(K // tk, N // tn, M // tm),
        in_specs=[pl.BlockSpec((tm, tk), lambda k, j, m: (m, k)), pl.BlockSpec((tm, tn), lambda k, j, m: (m, j))],
        out_specs=out_spec, out_shape=out_shape,
        compiler_params=_params(("parallel", "parallel", "arbitrary")),
    )(a, b)
    return out if col_sharded else out.reshape(chips, K // chips, N)


def _stack_heads(x, scale=None):
    lane = lax.broadcasted_iota(jnp.int32, x.shape, 1)
    zero = jnp.zeros_like(x)
    lo = jnp.where(lane < HEAD_DIM, x, zero)
    hi = jnp.where(lane < HEAD_DIM, zero, x)
    out = jnp.concatenate([lo, hi], axis=0)
    return out if scale is None else out * scale


def _unstack_heads(x2):
    qb = x2.shape[0] // 2
    lane = lax.broadcasted_iota(jnp.int32, (qb, LANES), 1)
    return jnp.where(lane < HEAD_DIM, x2[:qb], x2[qb:])


def _dot2(x, m):
    xh = x.astype(BF16)
    xl = (x - xh.astype(F32)).astype(BF16)
    return jnp.dot(xh, m, preferred_element_type=F32) + jnp.dot(xl, m, preferred_element_type=F32)


ATTN_CHUNK = 32


def _in_lockstep(staged):
    waiting, live = list(staged), []
    while waiting or live:
        if waiting:
            live.append(waiting.pop(0))
        for gen in list(live):
            if next(gen, StopIteration) is StopIteration:
                live.remove(gen)


def _row_chunks():
    return [slice(r, r + ATTN_CHUNK) for r in range(0, 2 * QUERY_BLOCK, ATTN_CHUNK)]


def _chunk_valid(rows, older):
    shape = (rows.stop - rows.start, older + QUERY_BLOCK)
    r = (rows.start + lax.broadcasted_iota(jnp.int32, shape, 0)) & (QUERY_BLOCK - 1)
    return lax.broadcasted_iota(jnp.int32, shape, 1) < r + older


def _split_to(x, hi_ref, lo_ref, rows):
    xh = x.astype(BF16)
    hi_ref[rows, :] = xh
    lo_ref[rows, :] = (x - xh.astype(F32)).astype(BF16)


def _triangle(keys, strict):
    r = lax.broadcasted_iota(jnp.int32, (keys, keys), 0)
    c = lax.broadcasted_iota(jnp.int32, (keys, keys), 1)
    return jnp.where((r > c) if strict else (r >= c), 1.0, 0.0).astype(BF16)


def _fill_attn_consts(tri_ref, mask_ref):
    n = 2 * QUERY_BLOCK
    tri_ref[0] = _triangle(n, True)
    tri_ref[1] = _triangle(n, False)
    mask_ref[...] = jnp.where(_chunk_valid(slice(0, n), QUERY_BLOCK), 1.0, 0.0)


def _valid(mask_ref, rows, older):
    cols = slice(0, 2 * QUERY_BLOCK) if older else slice(QUERY_BLOCK, 2 * QUERY_BLOCK)
    return mask_ref[rows, cols] > 0.5


def _row_total(first):
    lane = lax.broadcasted_iota(jnp.int32, first.shape, 1)
    total = jnp.sum(jnp.where(lane == 0, first, 0.0), axis=1, keepdims=True)
    return jnp.broadcast_to(total, first.shape)


def _pairs_per_step(n_pairs, most):
    return max(g for g in (1, 2, 4) if g <= most and n_pairs % g == 0)


def _lanes(g):
    return slice(g * LANES, (g + 1) * LANES)


def _sweep_older(i, step, carry_ref, first):
    def cond(state):
        n, live = state
        return jnp.logical_and(n < i, live)

    def older(state):
        n, _ = state
        step(i - 1 - n, False)
        return n + 1, jnp.max(carry_ref[...]) > LOG_STICK_FLOOR

    lax.while_loop(cond, older, (first, jnp.max(carry_ref[...]) > LOG_STICK_FLOOR))


class _CommJob:
    def __init__(self, inputs, out_shapes, aliases, n_sems, start, finish):
        self.inputs, self.out_shapes, self.aliases, self.n_sems = list(inputs), list(out_shapes), aliases, n_sems
        self.start, self.finish = start, finish


def _merge_jobs(jobs):
    jobs = [j for j in jobs if j is not None]
    if len(jobs) <= 1:
        return jobs[0] if jobs else None
    spans, aliases = [], {}
    i0 = o0 = s0 = 0
    for j in jobs:
        spans.append((i0, o0, s0))
        aliases.update({i0 + a: o0 + b for a, b in j.aliases.items()})
        i0, o0, s0 = i0 + len(j.inputs), o0 + len(j.out_shapes), s0 + j.n_sems

    def run(which):
        def go(ins, outs, send_sems, recv_sems):
            for j, (i, o, s) in zip(jobs, spans):
                getattr(j, which)(ins[i:i + len(j.inputs)], outs[o:o + len(j.out_shapes)],
                                  send_sems.at[pl.ds(s, j.n_sems)], recv_sems.at[pl.ds(s, j.n_sems)])
        return go

    return _CommJob([a for j in jobs for a in j.inputs], [s for j in jobs for s in j.out_shapes], aliases, s0,
                    run("start"), run("finish"))


def _call_with_job(core_body, job, *, name, grid, in_specs, out_specs, out_shape, scratch_shapes, args):
    sem = ("arbitrary",) * len(grid)
    if job is None:
        res = pl.pallas_call(core_body, name=name, grid=grid, in_specs=in_specs, out_specs=out_specs,
                             out_shape=out_shape, scratch_shapes=scratch_shapes, compiler_params=_params(sem))(*args)
        return list(res), []
    n_in, n_out, n_scr = len(in_specs), len(out_specs), len(scratch_shapes)
    m_in, m_out = len(job.inputs), len(job.out_shapes)

    def body(*refs):
        at = 0
        parts = []
        for count in (n_in, m_in, n_out, m_out, n_scr, 2):
            parts.append(refs[at:at + count])
            at += count
        ins, job_in, outs, job_outs, scratch, (send_sems, recv_sems) = parts
        first = functools.reduce(jnp.logical_and, [pl.program_id(a) == 0 for a in range(len(grid))])
        last = functools.reduce(jnp.logical_and, [pl.program_id(a) == grid[a] - 1 for a in range(len(grid))])

        @pl.when(first)
        def _():
            job.start(job_in, job_outs, send_sems, recv_sems)

        core_body(*ins, *outs, *scratch)

        @pl.when(last)
        def _():
            job.finish(job_in, job_outs, send_sems, recv_sems)

    res = pl.pallas_call(
        body, name=name, grid=grid, in_specs=list(in_specs) + [ANY] * m_in, out_specs=list(out_specs) + [ANY] * m_out,
        out_shape=list(out_shape) + job.out_shapes,
        input_output_aliases={n_in + a: n_out + b for a, b in job.aliases.items()},
        scratch_shapes=list(scratch_shapes) + [pltpu.SemaphoreType.DMA((job.n_sems,)), pltpu.SemaphoreType.DMA((job.n_sems,))],
        compiler_params=_params(sem),
    )(*args, *job.inputs)
    return list(res[:n_out]), list(res[n_out:])


def _run_job(job, name):
    m_in, m_out = len(job.inputs), len(job.out_shapes)

    def body(*refs):
        job_in, job_outs = refs[:m_in], refs[m_in:m_in + m_out]
        send_sems, recv_sems = refs[m_in + m_out:]
        job.start(job_in, job_outs, send_sems, recv_sems)
        job.finish(job_in, job_outs, send_sems, recv_sems)

    return list(pl.pallas_call(
        body, name=name, in_specs=[ANY] * m_in, out_specs=[ANY] * m_out, out_shape=job.out_shapes,
        input_output_aliases=dict(job.aliases),
        scratch_shapes=[pltpu.SemaphoreType.DMA((job.n_sems,)), pltpu.SemaphoreType.DMA((job.n_sems,))],
    )(*job.inputs))


def _attn_fwd(qkv, sb_width, name, job=None):
    L = qkv.shape[0]
    QB = QUERY_BLOCK
    nb = L // QB
    n_pairs = sb_width // LANES
    G = _pairs_per_step(n_pairs, 4)
    W = G * LANES
    scale = 1.0 / math.sqrt(HEAD_DIM)

    def body(q_ref, k_ref, v_ref, o_ref, acc_ref, carry_ref, f32_ref, bf16_ref, tri_ref, mask_ref):
        i = pl.program_id(1)

        @pl.when(i == 0)
        def _():
            _fill_attn_consts(tri_ref, mask_ref)

        q2 = [_stack_heads(q_ref[:, _lanes(g)], scale) for g in range(G)]
        acc_ref[...] = jnp.zeros_like(acc_ref)
        carry_ref[...] = jnp.zeros_like(carry_ref)

        def step(g, j, older, masked):
            n = older + QB
            start = pl.multiple_of(j * QB, QB)
            kb = k_ref[pl.ds(start, n), _lanes(g)]
            vb = v_ref[pl.ds(start, n), _lanes(g)]
            zs, as_, bs = (f32_ref.at[g, t, :, :n] for t in range(3))
            hi, lo = (bf16_ref.at[g, t, :, :n] for t in range(2))
            zs[...] = lax.dot_general(q2[g], kb, (((1,), (1,)), ((), ())), preferred_element_type=F32)
            yield
            for rows in _row_chunks():
                z = zs[rows, :]
                sp = jnp.log(1.0 + jnp.exp(-jnp.abs(z)))
                b = jnp.minimum(-z, 0.0) - sp
                if masked:
                    b = jnp.where(_valid(mask_ref, rows, older),b, 0.0)
                as_[rows, :] = jnp.minimum(z, 0.0) - sp
                bs[rows, :] = b
                _split_to(b, hi, lo, rows)
            yield
            tri = tri_ref[0, :n, :n]
            zs[...] = (jnp.dot(hi[...], tri, preferred_element_type=F32)
                       + jnp.dot(lo[...], tri, preferred_element_type=F32))
            yield
            for rows in _row_chunks():
                excl = zs[rows, :]
                total = _row_total(excl[:, :LANES] + bs[rows, :LANES])
                if not masked:
                    excl = excl + jnp.tile(carry_ref[g, rows, :], (1, n // LANES))
                carry_ref[g, rows, :] += total
                w = jnp.exp(as_[rows, :] + excl)
                if masked:
                    w = jnp.where(_valid(mask_ref, rows, older),w, 0.0)
                _split_to(w, hi, lo, rows)
            yield
            acc_ref[g] += (jnp.dot(hi[...], vb, preferred_element_type=F32)
                           + jnp.dot(lo[...], vb, preferred_element_type=F32))

        @pl.when(i == 0)
        def _():
            _in_lockstep([step(g, 0, 0, True) for g in range(G)])

        @pl.when(i > 0)
        def _():
            _in_lockstep([step(g, i - 1, QB, True) for g in range(G)])
            for g in range(G):
                _sweep_older(i, lambda j, _, g=g: _in_lockstep([step(g, j, 0, False)]), carry_ref.at[g], 1)

        for g in range(G):
            o_ref[:, _lanes(g)] = _unstack_heads(acc_ref[g])

    n_steps = n_pairs // G
    (out,), job_out = _call_with_job(
        body, job, name=name, grid=(n_steps, nb),
        in_specs=[pl.BlockSpec((QB, W), lambda p, i: (i, p)),
                  pl.BlockSpec((L, W), lambda p, i: (0, n_steps + p)),
                  pl.BlockSpec((L, W), lambda p, i: (0, 2 * n_steps + p))],
        out_specs=[pl.BlockSpec((QB, W), lambda p, i: (i, p))],
        out_shape=[jax.ShapeDtypeStruct((L, sb_width), F32)],
        scratch_shapes=[pltpu.VMEM((G, 2 * QB, LANES), F32), pltpu.VMEM((G, 2 * QB, LANES), F32),
                        pltpu.VMEM((G, 3, 2 * QB, 2 * QB), F32), pltpu.VMEM((G, 2, 2 * QB, 2 * QB), BF16),
                        pltpu.VMEM((2, 2 * QB, 2 * QB), BF16), pltpu.VMEM((2 * QB, 2 * QB), F32)],
        args=(qkv, qkv, qkv))
    return out, job_out


def _attn_bwd(qkv, o, dmixed, sb_width, name, job=None):
    L = qkv.shape[0]
    QB = QUERY_BLOCK
    nb = L // QB
    n_pairs = sb_width // LANES
    G = _pairs_per_step(n_pairs, 2)
    W = G * LANES
    scale = 1.0 / math.sqrt(HEAD_DIM)

    def body(q_ref, k_ref, v_ref, o_ref, do_ref, dq_ref, dk_ref, dv_ref,
             dq_acc, dk_acc, dv_acc, ce_ref, cr_ref, dtot_ref, f32_ref, bf16_ref, tri_ref, mask_ref):
        i = pl.program_id(1)

        @pl.when(i == 0)
        def _():
            dk_acc[...] = jnp.zeros_like(dk_acc)
            dv_acc[...] = jnp.zeros_like(dv_acc)
            _fill_attn_consts(tri_ref, mask_ref)

        q2s = [_stack_heads(q_ref[:, _lanes(g)], scale) for g in range(G)]
        do2s = [_stack_heads(do_ref[:, _lanes(g)].astype(BF16)) for g in range(G)]
        ones = jnp.ones((LANES, LANES), BF16)
        for g in range(G):
            ov = o_ref[:, _lanes(g)]
            dtot_ref[g] = _dot2(do2s[g].astype(F32) * jnp.concatenate([ov, ov], axis=0), ones)
        dq_acc[...] = jnp.zeros_like(dq_acc)
        ce_ref[...] = jnp.zeros_like(ce_ref)
        cr_ref[...] = jnp.zeros_like(cr_ref)

        def step(g, j, older, masked):
            n = older + QB
            wide = n // LANES
            q2, do2 = q2s[g], do2s[g]
            start = pl.multiple_of(j * QB, QB)
            kb = k_ref[pl.ds(start, n), _lanes(g)]
            vb = v_ref[pl.ds(start, n), _lanes(g)]
            zs, as_, bs, betas, gs = (f32_ref.at[g, t, :, :n] for t in range(5))
            hi, lo, wb = (bf16_ref.at[g, t, :, :n] for t in range(3))
            nt = (((1,), (1,)), ((), ()))
            zs[...] = lax.dot_general(q2, kb, nt, preferred_element_type=F32)
            gs[...] = lax.dot_general(do2, vb, nt, preferred_element_type=F32)
            yield
            for rows in _row_chunks():
                z = zs[rows, :]
                e = jnp.exp(-jnp.abs(z))
                sp = jnp.log(1.0 + e)
                b = jnp.minimum(-z, 0.0) - sp
                if masked:
                    b = jnp.where(_valid(mask_ref, rows, older),b, 0.0)
                rinv = 1.0 / (1.0 + e)
                as_[rows, :] = jnp.minimum(z, 0.0) - sp
                bs[rows, :] = b
                betas[rows, :] = jnp.where(z >= 0.0, rinv, e * rinv)
                _split_to(b, hi, lo, rows)
            yield
            tri = tri_ref[0, :n, :n]
            zs[...] = (jnp.dot(hi[...], tri, preferred_element_type=F32)
                       + jnp.dot(lo[...], tri, preferred_element_type=F32))
            yield
            for rows in _row_chunks():
                excl = zs[rows, :]
                total = _row_total(excl[:, :LANES] + bs[rows, :LANES])
                if not masked:
                    excl = excl + jnp.tile(ce_ref[g, rows, :], (1, wide))
                ce_ref[g, rows, :] += total
                w = jnp.exp(as_[rows, :] + excl)
                if masked:
                    w = jnp.where(_valid(mask_ref, rows, older),w, 0.0)
                wb[rows, :] = w.astype(BF16)
                gw = w * gs[rows, :]
                gs[rows, :] = gw
                _split_to(gw, hi, lo, rows)
            yield
            tri = tri_ref[1, :n, :n]
            zs[...] = (jnp.dot(hi[...], tri, preferred_element_type=F32)
                       + jnp.dot(lo[...], tri, preferred_element_type=F32))
            yield
            for rows in _row_chunks():
                rinc = zs[rows, :]
                total = _row_total(rinc[:, :LANES])
                if not masked:
                    rinc = rinc + jnp.tile(cr_ref[g, rows, :], (1, wide))
                cr_ref[g, rows, :] += total
                beta = betas[rows, :]
                dz = gs[rows, :] * (1.0 - beta) - beta * (jnp.tile(dtot_ref[g, rows, :], (1, wide)) - rinc)
                if masked:
                    dz = jnp.where(_valid(mask_ref, rows, older),dz, 0.0)
                hi[rows, :] = dz.astype(BF16)
            yield
            dzb = hi[...]
            dq_acc[g] += jnp.dot(dzb, kb, preferred_element_type=F32)
            dk_acc[pl.ds(start, n), _lanes(g)] += lax.dot_general(
                dzb, q2, (((0,), (0,)), ((), ())), preferred_element_type=F32)
            dv_acc[pl.ds(start, n), _lanes(g)] += lax.dot_general(
                wb[...], do2, (((0,), (0,)), ((), ())), preferred_element_type=F32)

        @pl.when(i == 0)
        def _():
            _in_lockstep([step(g, 0, 0, True) for g in range(G)])

        @pl.when(i > 0)
        def _():
            _in_lockstep([step(g, i - 1, QB, True) for g in range(G)])
            for g in range(G):
                _sweep_older(i, lambda j, _, g=g: _in_lockstep([step(g, j, 0, False)]), ce_ref.at[g], 1)

        for g in range(G):
            dq_ref[:, _lanes(g)] = (_unstack_heads(dq_acc[g]) * scale).astype(dq_ref.dtype)

        @pl.when(i == nb - 1)
        def _():
            dk_ref[...] = dk_acc[...].astype(dk_ref.dtype)
            dv_ref[...] = dv_acc[...].astype(dv_ref.dtype)

    n_steps = n_pairs // G
    blk = pl.BlockSpec((QB, W), lambda p, i: (i, p))
    col = pl.BlockSpec((L, W), lambda p, i: (0, p))
    return _call_with_job(
        body, job, name=name, grid=(n_steps, nb),
        in_specs=[blk,
                  pl.BlockSpec((L, W), lambda p, i: (0, n_steps + p)),
                  pl.BlockSpec((L, W), lambda p, i: (0, 2 * n_steps + p)),
                  blk, blk],
        out_specs=[blk, col, col],
        out_shape=[jax.ShapeDtypeStruct((L, sb_width), BF16)] * 3,
        scratch_shapes=[pltpu.VMEM((G, 2 * QB, LANES), F32), pltpu.VMEM((L, W), F32),
                        pltpu.VMEM((L, W), F32), pltpu.VMEM((G, 2 * QB, LANES), F32),
                        pltpu.VMEM((G, 2 * QB, LANES), F32), pltpu.VMEM((G, 2 * QB, LANES), F32),
                        pltpu.VMEM((G, 5, 2 * QB, 2 * QB), F32), pltpu.VMEM((G, 3, 2 * QB, 2 * QB), BF16),
                        pltpu.VMEM((2, 2 * QB, 2 * QB), BF16), pltpu.VMEM((2 * QB, 2 * QB), F32)],
        args=(qkv, qkv, qkv, o, dmixed))


def _conv_tile(L):
    return _pick(L, (384, 128))


def _glu(x, C):
    return x[:, :C] * _sigmoid(x[:, C:])


CONV_CHUNK = 32
SHIFT_TAIL = 24


def _fill_shifted(src_ref, dst_ref):
    n = dst_ref.shape[1]
    for r in range(1, 8):
        dst_ref[r - 1] = src_ref[r:r + n, :]


def _rows_at(src_ref, shifted_ref, start, n):
    q, r = divmod(start, 8)
    if r == 0:
        return src_ref[start:start + n, :]
    return shifted_ref[r - 1, 8 * q:8 * q + n, :]


def _conv_fwd(cacg, dw_w, dw_b, ln_g, ln_b, name, job=None):
    L, C2 = cacg.shape
    C = C2 // 2
    T = _conv_tile(L)
    H = CONV_HALO
    K = dw_w.shape[0]
    CH = CONV_CHUNK

    def body(x_ref, prev_ref, w_ref, b_ref, g_ref, beta_ref, o_ref, y_ref, u_ref, us_ref):
        i = pl.program_id(0)
        u_ref[0:H, :] = jnp.where(i > 0, _glu(prev_ref[...], C), 0.0)
        u_ref[H:, :] = _glu(x_ref[...], C)
        _fill_shifted(u_ref, us_ref)
        for c0 in range(0, T, CH):
            y = jnp.broadcast_to(b_ref[...], (CH, C))
            for k in range(K):
                y = y + w_ref[k:k + 1, :] * _rows_at(u_ref, us_ref, c0 + H - (K - 1) + k, CH)
            y_ref[c0:c0 + CH, :] = y
            mu = jnp.mean(y, axis=-1, keepdims=True)
            yc = y - mu
            rstd = lax.rsqrt(jnp.mean(yc * yc, axis=-1, keepdims=True) + EPS)
            ln = yc * rstd * g_ref[...] + beta_ref[...]
            o_ref[c0:c0 + CH, :] = (ln * _sigmoid(ln)).astype(o_ref.dtype)

    vec = pl.BlockSpec((1, C), lambda i: (0, 0))
    tile = pl.BlockSpec((T, C), lambda i: (i, 0))
    (out, y), arrived = _call_with_job(
        body, job, name=name, grid=(L // T,),
        in_specs=[pl.BlockSpec((T, C2), lambda i: (i, 0)),
                  pl.BlockSpec((H, C2), lambda i: (jnp.maximum(i * (T // H) - 1, 0), 0)),
                  pl.BlockSpec((K, C), lambda i: (0, 0)), vec, vec, vec],
        out_specs=[tile, tile],
        out_shape=[jax.ShapeDtypeStruct((L, C), BF16), jax.ShapeDtypeStruct((L, C), F32)],
        scratch_shapes=[pltpu.VMEM((T + H, C), F32), pltpu.VMEM((7, T + SHIFT_TAIL, C), F32)],
        args=(cacg, cacg, dw_w, dw_b, ln_g, ln_b))
    return out, y, arrived


def _conv_bwd(cacg, y, dmixed, dw_w, ln_g, ln_b, name, job=None):
    L, C2 = cacg.shape
    C = C2 // 2
    T = _conv_tile(L)
    H = CONV_HALO
    K = dw_w.shape[0]
    nt = L // T
    TE = T + H

    CH = CONV_CHUNK

    def body(x_ref, prev_ref, y_ref, ynext_ref, d_ref, dnext_ref, w_ref, g_ref, beta_ref,
             dca_ref, dcg_ref, dwt_ref, db_ref, dg_ref, dbeta_ref, u_ref, us_ref, dy_ref, dys_ref):
        i = pl.program_id(0)
        last = i == nt - 1

        @pl.when(i == 0)
        def _():
            dwt_ref[...] = jnp.zeros_like(dwt_ref)
            db_ref[...] = jnp.zeros_like(db_ref)
            dg_ref[...] = jnp.zeros_like(dg_ref)
            dbeta_ref[...] = jnp.zeros_like(dbeta_ref)

        u_ref[0:H, :] = jnp.where(i > 0, _glu(prev_ref[...], C), 0.0)
        u_ref[H:, :] = _glu(x_ref[...], C)
        _fill_shifted(u_ref, us_ref)
        dg_acc = jnp.zeros((1, C), F32)
        dbeta_acc = jnp.zeros((1, C), F32)
        db_acc = jnp.zeros((1, C), F32)
        for c0 in range(0, TE, CH):
            y = y_ref[c0:c0 + CH, :] if c0 < T else ynext_ref[c0 - T:c0 - T + CH, :]
            mu = jnp.mean(y, axis=-1, keepdims=True)
            yc = y - mu
            rstd = lax.rsqrt(jnp.mean(yc * yc, axis=-1, keepdims=True) + EPS)
            yh = yc * rstd
            ln = yh * g_ref[...] + beta_ref[...]
            s = _sigmoid(ln)
            dout = d_ref[c0:c0 + CH, :] if c0 < T else jnp.where(last, 0.0, dnext_ref[c0 - T:c0 - T + CH, :])
            dln = dout * (s * (1.0 + ln * (1.0 - s)))
            dyh = dln * g_ref[...]
            dy = rstd * (dyh - jnp.mean(dyh, axis=-1, keepdims=True)
                         - yh * jnp.mean(dyh * yh, axis=-1, keepdims=True))
            dy_ref[c0:c0 + CH, :] = dy
            if c0 < T:
                dg_acc = dg_acc + jnp.sum(dln * yh, axis=0, keepdims=True)
                dbeta_acc = dbeta_acc + jnp.sum(dln, axis=0, keepdims=True)
                db_acc = db_acc + jnp.sum(dy, axis=0, keepdims=True)
        dg_ref[...] += dg_acc
        dbeta_ref[...] += dbeta_acc
        db_ref[...] += db_acc
        _fill_shifted(dy_ref, dys_ref)
        for k in range(K):
            dwt_ref[k:k + 1, :] += jnp.sum(
                dy_ref[0:T, :] * _rows_at(u_ref, us_ref, H - (K - 1) + k, T), axis=0, keepdims=True)
        for c0 in range(0, T, CH):
            du = jnp.zeros((CH, C), F32)
            for k in range(K):
                du = du + w_ref[k:k + 1, :] * _rows_at(dy_ref, dys_ref, c0 + (K - 1) - k, CH)
            x = x_ref[c0:c0 + CH, :]
            sg = _sigmoid(x[:, C:])
            dca_ref[c0:c0 + CH, :] = (du * sg).astype(dca_ref.dtype)
            dcg_ref[c0:c0 + CH, :] = (du * x[:, :C] * sg * (1.0 - sg)).astype(dcg_ref.dtype)

    nh = L // H
    vec = pl.BlockSpec((1, C), lambda i: (0, 0))
    row = pl.BlockSpec((T, C), lambda i: (i, 0))
    after = lambda i: jnp.minimum((i + 1) * (T // H), nh - 1)
    return _call_with_job(
        body, job, name=name, grid=(nt,),
        in_specs=[pl.BlockSpec((T, C2), lambda i: (i, 0)),
                  pl.BlockSpec((H, C2), lambda i: (jnp.maximum(i * (T // H) - 1, 0), 0)),
                  row, pl.BlockSpec((H, C), lambda i: (after(i), 0)),
                  pl.BlockSpec((T, C), lambda i: (i, 1)), pl.BlockSpec((H, C), lambda i: (after(i), 1)),
                  pl.BlockSpec((K, C), lambda i: (0, 0)), vec, vec],
        out_specs=[row, row, pl.BlockSpec((H, C), lambda i: (0, 0)), vec, vec, vec],
        out_shape=[jax.ShapeDtypeStruct((L, C), BF16), jax.ShapeDtypeStruct((L, C), BF16),
                   jax.ShapeDtypeStruct((H, C), F32), jax.ShapeDtypeStruct((1, C), F32),
                   jax.ShapeDtypeStruct((1, C), F32), jax.ShapeDtypeStruct((1, C), F32)],
        scratch_shapes=[pltpu.VMEM((T + H, C), F32), pltpu.VMEM((7, T + SHIFT_TAIL, C), F32),
                        pltpu.VMEM((TE, C), F32), pltpu.VMEM((7, T + SHIFT_TAIL, C), F32)],
        args=(cacg, cacg, y, y, dmixed, dmixed, dw_w, ln_g, ln_b))


def _local_step(h0, target, n_meta, seq, norms, conv_p, wts, final_g, gather_behind, reducer):
    mix_g, ffn_g = norms
    dw_w, dw_b, ln_g, ln_b = conv_p
    depth = mix_g.shape[0]
    C = dw_b.shape[-1]
    sbw = (wts["w_in"][0].shape[-1] - 2 * C) // 3
    assert sbw == C, "the mixer halves must have equal width"
    row = lambda a, i: a[i][None, :]

    h = h0
    saved = []
    for i in range(depth):
        hn, proj_qkv = _norm_in_proj(h, row(mix_g, i), wts["w_in"][i], 3 * sbw, f"in_qkv_{i}")
        cacg = _mm_nn([(hn, wts["w_in"][i], 0, 0)], F32, f"in_conv_{i}", cols=(3 * sbw, 2 * C))
        def hosting(kind):
            job, keys = gather_behind.get((kind, i), (None, ()))

            def sink(arrived):
                for (wname, wl), arr in zip(keys, arrived):
                    wts[wname][wl] = arr

            return job, sink

        job, sink = hosting("attn")
        attn, arrived = _attn_fwd(proj_qkv, sbw, f"attn_fwd_{i}", job)
        sink(arrived)
        job, sink = hosting("conv")
        conv, conv_y, arrived = _conv_fwd(cacg, dw_w[i], row(dw_b, i), row(ln_g, i), row(ln_b, i),
                                          f"conv_fwd_{i}", job)
        sink(arrived)
        h_mid = _mm_nn([(attn, wts["w_out"][i], 0, 0), (conv, wts["w_out"][i], 0, 1)], F32, f"out_proj_{i}",
                       residual=h)
        job, sink = hosting("ffn")
        (hn2, g, u, act), arrived = _ffn_up(h_mid, row(ffn_g, i), wts["w_gate_t"][i], wts["w_up_t"][i],
                                            f"ffn_up_{i}", job)
        sink(arrived)
        job, sink = hosting("down")
        h_out = _mm_nn([(act, wts["w_down"][i], 0, 0)], F32, f"down_{i}", residual=h_mid, job=job)
        if job is not None:
            h_out, arrived = h_out
            sink(arrived)
        saved.append((h, hn, proj_qkv, cacg, attn, conv, conv_y, h_mid, hn2, g, u, act))
        h = h_out

    loss, dh, d_final_g = _loss_head(h, final_g[None, :], target, n_meta, seq, "loss_head")

    grads = {k: [None] * depth for k in ("mix_g", "ffn_g", "dw_w", "dw_b", "ln_g", "ln_b")}
    for i in reversed(range(depth)):
        h_in, hn, proj_qkv, cacg, attn, conv, conv_y, h_mid, hn2, g, u, act = saved[i]
        big = {}
        dg, du = _ffn_down_bwd(dh, wts["w_down"][i], g, u, f"ffn_down_bwd_{i}")
        big["w_down"] = _mm_tn(act, dh, f"dw_down_{i}", col_sharded=False)
        dhn2 = _mm_nn([(dg, wts["w_gate_t"][i], 0, 0), (du, wts["w_up_t"][i], 0, 0)], F32, f"d_hn2_{i}")
        big["w_gate_t"] = _mm_tn(dg, hn2, f"dw_gate_{i}", col_sharded=False)
        big["w_up_t"] = _mm_tn(du, hn2, f"dw_up_{i}", col_sharded=False)
        sib_job, sib_sink = reducer.to_sibling(i, big)
        dh, d_ffn, arrived = _rmsnorm_bwd(h_mid, row(ffn_g, i), dhn2, dh, f"ffn_norm_bwd_{i}", sib_job)
        sib_sink(arrived)
        dmixed = _mm_nt([(dh, wts["w_out"][i], 0, 0)], F32, f"d_mixed_{i}")
        dw_out = jnp.concatenate([_mm_tn(attn, dh, f"dw_out_attn_{i}", col_sharded=False, chips=2),
                                  _mm_tn(conv, dh, f"dw_out_conv_{i}", col_sharded=False, chips=2)], axis=0)
        sib_job, sib_sink = reducer.to_sibling(i, {"w_out": dw_out})
        x_job, x_sink = reducer.take()
        (dq, dk, dv), arrived = _attn_bwd(proj_qkv, attn, dmixed, sbw, f"attn_bwd_{i}",
                                          _merge_jobs([sib_job, x_job]))
        sib_sink(arrived[:len(sib_job.out_shapes)])
        x_sink(arrived[len(sib_job.out_shapes):])
        x_job, x_sink = reducer.take()
        (dca, dcg, d_dw, d_b, d_lg, d_lb), arrived = _conv_bwd(
            cacg, conv_y, dmixed, dw_w[i], row(ln_g, i), row(ln_b, i), f"conv_bwd_{i}", x_job)
        x_sink(arrived)
        grads["dw_w"][i] = d_dw[:dw_w.shape[1]]
        grads["dw_b"][i], grads["ln_g"][i], grads["ln_b"][i] = d_b[0], d_lg[0], d_lb[0]
        dproj = jnp.concatenate([dq, dk, dv, dca, dcg], axis=1)
        dhn = _mm_nt([(dproj, wts["w_in"][i], 0, 0)], F32, f"d_hn_{i}")
        reducer.ready(i, {"w_in": _mm_tn(hn, dproj, f"dw_in_{i}", col_sharded=True)})
        if i == 0:
            x_job, x_sink = reducer.take()
            dh, d_mix, arrived = _rmsnorm_bwd(h_in, row(mix_g, i), dhn, dh, f"mix_norm_bwd_{i}", x_job)
            x_sink(arrived)
        else:
            dh, d_mix = _rmsnorm_bwd(h_in, row(mix_g, i), dhn, dh, f"mix_norm_bwd_{i}")
        grads["mix_g"][i], grads["ffn_g"][i] = d_mix[0], d_ffn[0]
    grads["final_g"] = d_final_g[0]
    return loss, dh, grads


ANY = pl.BlockSpec(memory_space=pl.ANY)


def _position():
    return lax.axis_index("x"), lax.axis_index("y"), lax.axis_index("c")


def _chip_at(x, y, k):
    return (1 - x if k & 2 else x), (1 - y if k & 1 else y)


def _half_rows(ref, half, rows, base=0):
    start = pl.multiple_of(base + half * rows, 8)
    lead = (slice(None),) * (len(ref.shape) - 2)
    return ref.at[(*lead, pl.ds(start, rows), slice(None))]


def _gather_job(fulls, shard_shapes, col_sharded):
    n = len(fulls)

    def tools(f_refs, send_sems, recv_sems):
        def block(wi, chip, half):
            _, R, C = shard_shapes[wi]
            if col_sharded[wi]:
                cols = pl.ds(pl.multiple_of(chip * C, LANES), C)
                return f_refs[wi].at[:, pl.ds(pl.multiple_of(half * (R // 2), 8), R // 2), cols]
            return _half_rows(f_refs[wi], half, R // 2, base=chip * R)

        def copy(wi, slot, blk, to):
            return pltpu.make_async_remote_copy(
                src_ref=blk, dst_ref=blk, send_sem=send_sems.at[6 * wi + slot],
                recv_sem=recv_sems.at[6 * wi + slot], device_id=to, device_id_type=MESH)

        return block, copy

    def start(_, f_refs, send_sems, recv_sems):
        block, copy = tools(f_refs, send_sems, recv_sems)
        x, y, c = _position()
        me = 2 * x + y
        for wi in range(n):
            for k in (1, 2, 3):
                copy(wi, k - 1, block(wi, me, c), (*_chip_at(x, y, k), c)).start()

    def finish(_, f_refs, send_sems, recv_sems):
        block, copy = tools(f_refs, send_sems, recv_sems)
        x, y, c = _position()
        me = 2 * x + y
        for wi in range(n):
            for k in (1, 2, 3):
                landed = block(wi, me ^ k, c)
                copy(wi, k - 1, landed, (x, y, c)).wait_recv()
                copy(wi, 2 + k, landed, (x, y, 1 - c)).start()
        for wi in range(n):
            for k in (1, 2, 3):
                copy(wi, 2 + k, block(wi, me ^ k, 1 - c), (x, y, c)).wait_recv()
        for wi in range(n):
            for k in (1, 2, 3):
                copy(wi, k - 1, block(wi, me, c), (x, y, c)).wait_send()
                copy(wi, 2 + k, block(wi, me ^ k, c), (x, y, c)).wait_send()

    return _CommJob(fulls, [jax.ShapeDtypeStruct(f.shape, f.dtype) for f in fulls], {i: i for i in range(n)},
                    6 * n, start, finish)


def _place_shard(w, layer, chip, col_sharded, dtype, name):
    _, R, C = w.shape
    tr = _pick(R, (256, 352, 128, 48))
    nr = R // tr

    def body(chip_ref, w_ref, o_ref):
        o_ref[...] = w_ref[...].astype(dtype)

    if col_sharded:
        shape = (1, R, N_CHIPS * C)
        out_spec = pl.BlockSpec((None, tr, C), lambda r, chip_ref: (0, r, chip_ref[0]))
    else:
        shape = (1, N_CHIPS * R, C)
        out_spec = pl.BlockSpec((None, tr, C), lambda r, chip_ref: (0, chip_ref[0] * nr + r, 0))
    grid_spec = pltpu.PrefetchScalarGridSpec(
        num_scalar_prefetch=1, grid=(nr,),
        in_specs=[pl.BlockSpec((None, tr, C), lambda r, chip_ref: (layer, r, 0))], out_specs=out_spec)
    return pl.pallas_call(
        body, name=name, grid_spec=grid_spec, out_shape=jax.ShapeDtypeStruct(shape, dtype),
        compiler_params=_params(("parallel",)),
    )(chip, w)


def _sibling_job(grads):
    n = len(grads)

    def copies(g_refs, l_refs, send_sems, recv_sems):
        x, y, c = _position()
        return [pltpu.make_async_remote_copy(
            src_ref=_half_rows(g_refs[wi], 1 - c, grads[wi].shape[1] // 2), dst_ref=l_refs[wi],
            send_sem=send_sems.at[wi], recv_sem=recv_sems.at[wi],
            device_id=(x, y, 1 - c), device_id_type=MESH) for wi in range(n)]

    def start(*refs):
        for cp in copies(*refs):
            cp.start()

    def finish(*refs):
        for cp in copies(*refs):
            cp.wait()

    outs = [jax.ShapeDtypeStruct((g.shape[0], g.shape[1] // 2, g.shape[2]), g.dtype) for g in grads]
    return _CommJob(grads, outs, {}, n, start, finish)


def _chip_sum(g, landed, core, name):
    _, R, C = g.shape
    hr = R // 2
    tr = _pick(hr, (256, 352, 128))
    nr = hr // tr

    def body(c_ref, g_ref, l_ref, o_ref):
        o_ref[...] = (g_ref[...] + l_ref[...]).astype(BF16)

    grid_spec = pltpu.PrefetchScalarGridSpec(
        num_scalar_prefetch=1, grid=(N_CHIPS, nr),
        in_specs=[pl.BlockSpec((None, tr, C), lambda j, r, c_ref: (j, c_ref[0] * nr + r, 0)),
                  pl.BlockSpec((None, tr, C), lambda j, r, c_ref: (j, r, 0))],
        out_specs=pl.BlockSpec((None, tr, C), lambda j, r, c_ref: (j, r, 0)))
    return pl.pallas_call(
        body, name=name, grid_spec=grid_spec, out_shape=jax.ShapeDtypeStruct((N_CHIPS, hr, C), BF16),
        compiler_params=_params(("parallel", "parallel")),
    )(core, g, landed)


def _across_job(parts):
    n = len(parts)

    def copy(p_refs, l_refs, send_sems, recv_sems, wi, k, to):
        x, y, _ = _position()
        me = 2 * x + y
        return pltpu.make_async_remote_copy(
            src_ref=p_refs[wi].at[me ^ k], dst_ref=l_refs[wi].at[me],
            send_sem=send_sems.at[3 * wi + k - 1], recv_sem=recv_sems.at[3 * wi + k - 1],
            device_id=to, device_id_type=MESH)

    def start(p_refs, l_refs, send_sems, recv_sems):
        x, y, c = _position()
        for wi in range(n):
            for k in (1, 2, 3):
                copy(p_refs, l_refs, send_sems, recv_sems, wi, k, (*_chip_at(x, y, k), c)).start()

    def finish(p_refs, l_refs, send_sems, recv_sems):
        x, y, c = _position()
        me = 2 * x + y
        for wi in range(n):
            for k in (1, 2, 3):
                slot = l_refs[wi].at[me ^ k]
                pltpu.make_async_remote_copy(
                    src_ref=slot, dst_ref=slot, send_sem=send_sems.at[3 * wi + k - 1],
                    recv_sem=recv_sems.at[3 * wi + k - 1], device_id=(x, y, c), device_id_type=MESH).wait_recv()
        for wi in range(n):
            for k in (1, 2, 3):
                copy(p_refs, l_refs, send_sems, recv_sems, wi, k, (x, y, c)).wait_send()

    return _CommJob(parts, [jax.ShapeDtypeStruct(p.shape, p.dtype) for p in parts], {}, 3 * n, start, finish)


class _Reducer:
    def __init__(self, core):
        self.core, self.parts, self.across, self.pending = core, {}, {}, []

    def to_sibling(self, layer, big):
        names = list(big)
        flat = [big[k] for k in names]

        def sink(landed):
            for k, g, la in zip(names, flat, landed):
                self.parts[k, layer] = _chip_sum(g, la, self.core, f"chip_sum_{k}_{layer}")
                self.pending.append((k, layer))

        return _sibling_job(flat), sink

    def ready(self, layer, big):
        job, sink = self.to_sibling(layer, big)
        sink(_run_job(job, f"grads_to_sibling_{next(iter(big))}_{layer}"))

    def take(self):
        keys, self.pending = self.pending, []
        if not keys:
            return None, lambda results: None

        def sink(results):
            self.across.update(zip(keys, results))

        return _across_job([self.parts[key] for key in keys]), sink


def _sum_chips(parts, landed, where, layer, depth, prev, name):
    _, hr, C = landed.shape
    tr = _pick(hr, (256, 352, 128))
    nr = hr // tr

    def body(*refs):
        own_ref, slots, o_ref = refs[1], refs[2:2 + N_CHIPS], refs[-1]
        chip = refs[0][0]
        total = None
        for q in range(N_CHIPS):
            term = jnp.where(chip == q, own_ref[...], slots[q][...]).astype(F32)
            total = term if total is None else total + term
        o_ref[...] = total

    def slot_spec(q):
        return pl.BlockSpec((None, tr, C), lambda r, w: (jnp.where(w[0] == q, (q + 1) % N_CHIPS, q), r, 0))

    in_specs = [pl.BlockSpec((None, tr, C), lambda r, w: (w[0], r, 0))] + [slot_spec(q) for q in range(N_CHIPS)]
    args = [where, parts] + [landed] * N_CHIPS
    aliases = {}
    if prev is not None:
        in_specs.append(ANY)
        args.append(prev)
        aliases = {len(args) - 1: 0}
    grid_spec = pltpu.PrefetchScalarGridSpec(
        num_scalar_prefetch=1, grid=(nr,), in_specs=in_specs,
        out_specs=pl.BlockSpec((None, tr, C), lambda r, w: (layer, w[1] * nr + r, 0)))
    return pl.pallas_call(
        body, name=name, grid_spec=grid_spec, out_shape=jax.ShapeDtypeStruct((depth, 2 * hr, C), F32),
        input_output_aliases=aliases, compiler_params=_params(("parallel",)),
    )(*args)


def _rs_join_halves(reduced):
    n = len(reduced)

    def body(*refs):
        o_refs = refs[n:2 * n]
        send_sems, recv_sems = refs[2 * n:]
        x, y, c = _position()
        sent = []
        for wi in range(n):
            hr = reduced[wi].shape[1] // 2
            mine = _half_rows(o_refs[wi], c, hr)
            cp = pltpu.make_async_remote_copy(
                src_ref=mine, dst_ref=mine, send_sem=send_sems.at[wi], recv_sem=recv_sems.at[wi],
                device_id=(x, y, 1 - c), device_id_type=MESH)
            cp.start()
            sent.append(cp)
        for wi in range(n):
            hr = reduced[wi].shape[1] // 2
            theirs = _half_rows(o_refs[wi], 1 - c, hr)
            pltpu.make_async_remote_copy(
                src_ref=theirs, dst_ref=theirs, send_sem=send_sems.at[wi], recv_sem=recv_sems.at[wi],
                device_id=(x, y, c), device_id_type=MESH).wait_recv()
        for cp in sent:
            cp.wait_send()

    return pl.pallas_call(
        body, name="grads_join_halves", out_shape=[jax.ShapeDtypeStruct(r.shape, r.dtype) for r in reduced],
        in_specs=[ANY] * n, out_specs=[ANY] * n, input_output_aliases={i: i for i in range(n)},
        scratch_shapes=[pltpu.SemaphoreType.DMA((n,)), pltpu.SemaphoreType.DMA((n,))],
    )(*reduced)


def _small_allreduce(vec):
    rows = vec.shape[0]

    def body(v_ref, o_ref, land, send_sems, recv_sems):
        x, y, c = _position()
        me = 4 * x + 2 * y + c
        land[0] = v_ref[...]
        sent = []
        for k in range(1, N_DEV):
            to = (1 - x if k & 4 else x, 1 - y if k & 2 else y, 1 - c if k & 1 else c)
            cp = pltpu.make_async_remote_copy(
                src_ref=v_ref, dst_ref=land.at[k], send_sem=send_sems.at[k - 1], recv_sem=recv_sems.at[k - 1],
                device_id=to, device_id_type=MESH)
            cp.start()
            sent.append(cp)
        for cp in sent:
            cp.wait_recv()
        acc = land[me]
        for e in range(1, N_DEV):
            acc = acc + land[me ^ e]
        o_ref[...] = acc
        for cp in sent:
            cp.wait_send()

    vmem = pl.BlockSpec(memory_space=pltpu.VMEM)
    return pl.pallas_call(
        body, name="small_allreduce", out_shape=jax.ShapeDtypeStruct(vec.shape, F32),
        in_specs=[vmem], out_specs=vmem,
        scratch_shapes=[pltpu.VMEM((N_DEV, rows, LANES), F32), pltpu.SemaphoreType.DMA((N_DEV - 1,)),
                        pltpu.SemaphoreType.DMA((N_DEV - 1,))],
    )(vec)


def _adam_math(w, g, m, v):
    m = ADAM_B1 * m + (1.0 - ADAM_B1) * g
    v = ADAM_B2 * v + (1.0 - ADAM_B2) * jnp.square(g)
    m_hat = m / (1.0 - ADAM_B1 ** ADAM_STEP)
    v_hat = v / (1.0 - ADAM_B2 ** ADAM_STEP)
    delta = -ADAM_LR * (m_hat / (jnp.sqrt(v_hat) + ADAM_EPS) + ADAM_WD * w)
    return delta, m, v


def _adam(w, g, m, v, name):
    def body(w_ref, g_ref, m_ref, v_ref, d_ref, nm_ref, nv_ref):
        d_ref[...], nm_ref[...], nv_ref[...] = _adam_math(w_ref[...], g_ref[...], m_ref[...], v_ref[...])

    if w.ndim == 3:
        lyr, R, C = w.shape
        tr = _pick(R, (256, 352, 128))
        blk = pl.BlockSpec((None, tr, C), lambda l, r: (l, r, 0))
        grid, sem = (lyr, R // tr), ("parallel", "parallel")
    else:
        blk = pl.BlockSpec(w.shape, lambda: (0, 0))
        grid, sem = (), None
    return pl.pallas_call(
        body, name=name, grid=grid, in_specs=[blk] * 4, out_specs=[blk] * 3,
        out_shape=[jax.ShapeDtypeStruct(w.shape, F32)] * 3, compiler_params=_params(sem),
    )(w, g, m, v)


def _rows(a, pad_to=8):
    r = a.reshape(-1, LANES)
    extra = (-r.shape[0]) % pad_to
    return jnp.pad(r, ((0, extra), (0, 0))) if extra else r


def _pack(arrays):
    return jnp.concatenate([_rows(a) for a in arrays], axis=0)


def _unpack(slab, shapes):
    out, at = [], 0
    for shp in shapes:
        nrow = math.prod(shp) // LANES
        out.append(slab[at:at + nrow].reshape(shp))
        at += nrow + (-nrow) % 8
    return out


BIG = ("w_in", "w_out", "w_gate_t", "w_up_t", "w_down")
BIG_COL_SHARDED = (True, False, False, False, False)
TRANSPOSED = {"w_gate_t": "w_gate", "w_up_t": "w_up"}


def kernel(x, meta_tokens, mix_norm_g, w_in, conv_dw_w, conv_dw_b, conv_ln_g, conv_ln_b, w_out, ffn_norm_g, w_gate, w_up, w_down, final_norm_g, loss_target, m_meta_tokens, m_mix_norm_g, m_w_in, m_conv_dw_w, m_conv_dw_b, m_conv_ln_g, m_conv_ln_b, m_w_out, m_ffn_norm_g, m_w_gate, m_w_up, m_w_down, m_final_norm_g, v_meta_tokens, v_mix_norm_g, v_w_in, v_conv_dw_w, v_conv_dw_b, v_conv_ln_g, v_conv_ln_b, v_w_out, v_ffn_norm_g, v_w_gate, v_w_up, v_w_down, v_final_norm_g):
    n_meta, seq = meta_tokens.shape[0], x.shape[1]
    D = x.shape[2]
    depth, taps, c_shard = conv_dw_w.shape
    C = conv_dw_b.shape[-1]
    chip = (2 * lax.axis_index("x") + lax.axis_index("y")).astype(jnp.int32)
    core = lax.axis_index("c").astype(jnp.int32).reshape(1)
    chip1 = chip.reshape(1)
    where = jnp.concatenate([chip1, core])
    big_w = dict(w_in=w_in, w_out=w_out, w_gate=w_gate, w_up=w_up, w_down=w_down)
    big_m = dict(w_in=m_w_in, w_out=m_w_out, w_gate=m_w_gate, w_up=m_w_up, w_down=m_w_down)
    big_v = dict(w_in=v_w_in, w_out=v_w_out, w_gate=v_w_gate, w_up=v_w_up, w_down=v_w_down)

    small_shard = _pack([conv_dw_w, meta_tokens])[None]
    to_send = {k: jnp.swapaxes(big_w[TRANSPOSED[k]], 1, 2) if k in TRANSPOSED else big_w[k] for k in BIG}
    col = dict(zip(BIG, BIG_COL_SHARDED))
    wts = {k: [_place_shard(to_send[k], l, chip1, col[k], BF16, f"place_{k}_{l}") for l in range(depth)] for k in BIG}
    small_placed = _place_shard(small_shard, 0, chip1, False, F32, "place_small")

    def gather_job(keys, extra=()):
        arrays = [wts[k][l] for k, l in keys] + list(extra)
        shapes = [(1,) + to_send[k].shape[1:] for k, _ in keys] + [(1,) + small_shard.shape[1:]] * len(extra)
        return _gather_job(arrays, shapes, [col[k] for k, _ in keys] + [False] * len(extra))

    first_keys = [("w_in", 0)]
    *first, small_full = _run_job(gather_job(first_keys, [small_placed]), "gather_first")
    for (k, l), arr in zip(first_keys, first):
        wts[k][l] = arr
    behind_keys = {("attn", 0): [("w_out", 0), ("w_gate_t", 0), ("w_up_t", 0)], ("conv", 0): [("w_down", 0)]}
    for l in range(1, depth):
        behind_keys["ffn", l - 1] = [("w_in", l), ("w_out", l), ("w_gate_t", l)]
        behind_keys["down", l - 1] = [("w_up_t", l)]
        behind_keys["attn", l] = [("w_down", l)]
    gather_behind = {host: (gather_job(keys), keys) for host, keys in behind_keys.items()}

    rows_shard = small_shard.shape[1]
    dw_full, meta_full = [], []
    for j in range(N_CHIPS):
        dwj, mj = _unpack(small_full[0, j * rows_shard:(j + 1) * rows_shard],
                          [conv_dw_w.shape, meta_tokens.shape])
        dw_full.append(dwj)
        meta_full.append(mj)
    dw_w_full = jnp.concatenate(dw_full, axis=2)
    meta = jnp.concatenate(meta_full, axis=1)

    L = n_meta + seq
    Lp = -(-L // QUERY_BLOCK) * QUERY_BLOCK
    h0 = jnp.concatenate([meta, x[0], jnp.zeros((Lp - L, D), F32)], axis=0)
    target = jnp.pad(loss_target[0], ((n_meta, Lp - L), (0, 0)))
    reducer = _Reducer(core)
    loss, dh0, grads = _local_step(h0, target, n_meta, seq, (mix_norm_g, ffn_norm_g),
                                   (dw_w_full, conv_dw_b, conv_ln_g, conv_ln_b), wts, final_norm_g,
                                   gather_behind, reducer)
    loss = lax.psum(loss[0, 0], ("x", "y", "c"))
    grad_x = dh0[n_meta:L][None]

    reduced = []
    for k in BIG:
        arr = None
        for l in range(depth):
            arr = _sum_chips(reducer.parts[k, l], reducer.across[k, l], where, l, depth, arr, f"sum_chips_{k}_{l}")
        reduced.append(arr)
    big_g = dict(zip(BIG, _rs_join_halves(reduced)))

    small_names = ("mix_g", "ffn_g", "dw_b", "ln_g", "ln_b", "dw_w")
    small = [jnp.stack(grads[k]) for k in small_names] + [grads["final_g"], dh0[:n_meta]]
    small_shapes = [a.shape for a in small]
    g_mix, g_ffn, g_dwb, g_lng, g_lnb, g_dww, g_final, g_meta = _unpack(_small_allreduce(_pack(small)), small_shapes)
    g_dww = lax.dynamic_slice_in_dim(g_dww, chip * c_shard, c_shard, axis=2)
    g_meta = lax.dynamic_slice_in_dim(g_meta, chip * meta_tokens.shape[1], meta_tokens.shape[1], axis=1)

    out_g, out_d, out_m, out_v = {}, {}, {}, {}
    for kk in BIG:
        k = TRANSPOSED.get(kk, kk)
        view = (lambda a: jnp.swapaxes(a, 1, 2)) if kk in TRANSPOSED else (lambda a: a)
        res = _adam(view(big_w[k]), big_g[kk], view(big_m[k]), view(big_v[k]), f"adam_{k}")
        out_g[k] = view(big_g[kk])
        out_d[k], out_m[k], out_v[k] = (view(a) for a in res)
    small_order = ("meta_tokens", "mix_norm_g", "conv_dw_w", "conv_dw_b", "conv_ln_g", "conv_ln_b",
                   "ffn_norm_g", "final_norm_g")
    sw = dict(meta_tokens=meta_tokens, mix_norm_g=mix_norm_g, conv_dw_w=conv_dw_w, conv_dw_b=conv_dw_b,
              conv_ln_g=conv_ln_g, conv_ln_b=conv_ln_b, ffn_norm_g=ffn_norm_g, final_norm_g=final_norm_g)
    sm = dict(meta_tokens=m_meta_tokens, mix_norm_g=m_mix_norm_g, conv_dw_w=m_conv_dw_w, conv_dw_b=m_conv_dw_b,
              conv_ln_g=m_conv_ln_g, conv_ln_b=m_conv_ln_b, ffn_norm_g=m_ffn_norm_g, final_norm_g=m_final_norm_g)
    sv = dict(meta_tokens=v_meta_tokens, mix_norm_g=v_mix_norm_g, conv_dw_w=v_conv_dw_w, conv_dw_b=v_conv_dw_b,
              conv_ln_g=v_conv_ln_g, conv_ln_b=v_conv_ln_b, ffn_norm_g=v_ffn_norm_g, final_norm_g=v_final_norm_g)
    sg = dict(meta_tokens=g_meta, mix_norm_g=g_mix, conv_dw_w=g_dww, conv_dw_b=g_dwb, conv_ln_g=g_lng,
              conv_ln_b=g_lnb, ffn_norm_g=g_ffn, final_norm_g=g_final)
    def slab(d):
        return _pack([d[k] for k in small_order])
    shapes = [sw[k].shape for k in small_order]
    deltas = _adam(slab(sw), slab(sg), slab(sm), slab(sv), "adam_small")
    for res, dst in zip(deltas, (out_d, out_m, out_v)):
        dst.update(zip(small_order, _unpack(res, shapes)))
    out_g.update(sg)

    order = ("meta_tokens", "mix_norm_g", "w_in", "conv_dw_w", "conv_dw_b", "conv_ln_g", "conv_ln_b", "w_out",
             "ffn_norm_g", "w_gate", "w_up", "w_down", "final_norm_g")
    return (loss, grad_x, *[out_g[k] for k in order], *[out_d[k] for k in order],
            *[out_m[k] for k in order], *[out_v[k] for k in order])
```

```python
import functools
import math

import jax
import jax.numpy as jnp
from jax import lax
from jax.experimental import pallas as pl
from jax.experimental.pallas import tpu as pltpu

F32 = jnp.float32
BF16 = jnp.bfloat16
MESH = pl.DeviceIdType.MESH

EPS = 1e-6
QUERY_BLOCK = 128
LANES = 128
HEAD_DIM = 64
LOG_STICK_FLOOR = -40.0
CONV_HALO = 32
N_CHIPS = 4
N_DEV = 8
VMEM_LIMIT = 56 * 1024 * 1024

ADAM_LR = 0.001
ADAM_B1 = 0.9
ADAM_B2 = 0.999
ADAM_EPS = 1e-08
ADAM_WD = 0.01
ADAM_STEP = 10


def _pick(n, prefs):
    for p in prefs:
        if n % p == 0:
            return p
    return n


def _params(sem=None):
    return pltpu.CompilerParams(dimension_semantics=sem, vmem_limit_bytes=VMEM_LIMIT)


def _sigmoid(x):
    return 1.0 / (1.0 + jnp.exp(-x))


def _rmsnorm_bwd(h, g, dy, dh_in, name, job=None):
    L, D = h.shape
    T = _pick(L, (384, 128))

    def body(h_ref, g_ref, dy_ref, dhin_ref, dh_ref, dg_ref):
        x = h_ref[...]
        dyv = dy_ref[...]
        r = lax.rsqrt(jnp.mean(x * x, axis=-1, keepdims=True) + EPS)
        xh = x * r
        dxh = dyv * g_ref[...]
        dh_ref[...] = dhin_ref[...] + r * (dxh - xh * jnp.mean(dxh * xh, axis=-1, keepdims=True))

        @pl.when(pl.program_id(0) == 0)
        def _():
            dg_ref[...] = jnp.zeros_like(dg_ref)

        dg_ref[...] += jnp.sum(dyv * xh, axis=0, keepdims=True)

    row = pl.BlockSpec((T, D), lambda i: (i, 0))
    vec = pl.BlockSpec((1, D), lambda i: (0, 0))
    (dh, dg), arrived = _call_with_job(
        body, job, name=name, grid=(L // T,),
        in_specs=[row, vec, row, row], out_specs=[row, vec],
        out_shape=[jax.ShapeDtypeStruct((L, D), F32), jax.ShapeDtypeStruct((1, D), F32)],
        scratch_shapes=[], args=(h, g, dy, dh_in))
    return (dh, dg) if job is None else (dh, dg, arrived)


def _loss_head(h, g, target, n_meta, seq, name):
    L, D = h.shape
    T = _pick(L, (384, 128))

    def body(h_ref, g_ref, t_ref, loss_ref, dh_ref, dg_ref):
        i = pl.program_id(0)
        x = h_ref[...]
        gv = g_ref[...]
        r = lax.rsqrt(jnp.mean(x * x, axis=-1, keepdims=True) + EPS)
        xh = x * r
        y = xh * gv
        rows = i * T + lax.broadcasted_iota(jnp.int32, (T, 1), 0)
        live = (rows >= n_meta) & (rows < n_meta + seq)
        diff = jnp.where(live, y - t_ref[...], 0.0)
        dyv = diff / D
        dxh = dyv * gv
        dh_ref[...] = r * (dxh - xh * jnp.mean(dxh * xh, axis=-1, keepdims=True))

        @pl.when(i == 0)
        def _():
            dg_ref[...] = jnp.zeros_like(dg_ref)
            loss_ref[...] = jnp.zeros_like(loss_ref)

        dg_ref[...] += jnp.sum(dyv * xh, axis=0, keepdims=True)
        per_row = jnp.mean(diff * diff, axis=-1, keepdims=True)
        loss_ref[...] += 0.5 * jnp.sum(per_row, axis=0, keepdims=True)

    row = pl.BlockSpec((T, D), lambda i: (i, 0))
    vec = pl.BlockSpec((1, D), lambda i: (0, 0))
    one = pl.BlockSpec((1, 1), lambda i: (0, 0))
    return pl.pallas_call(
        body, name=name, grid=(L // T,),
        in_specs=[row, vec, row], out_specs=[one, row, vec],
        out_shape=[jax.ShapeDtypeStruct((1, 1), F32), jax.ShapeDtypeStruct((L, D), F32),
                   jax.ShapeDtypeStruct((1, D), F32)],
        compiler_params=_params(("arbitrary",)),
    )(h, g, target)


def _ffn_tiles(M, F):
    return _pick(M, (352, 384, 128)), _pick(F, (1408, 512, 256, 128))


def _resident(shape):
    return pl.BlockSpec((None,) + tuple(shape[1:]), lambda *_: (0,) * len(shape), pipeline_mode=pl.Buffered(1))


def _normed_rows(h_ref, g_ref, hn_ref, keep_ref):
    @pl.when(pl.program_id(1) == 0)
    def _():
        x = h_ref[...]
        r = lax.rsqrt(jnp.mean(x * x, axis=-1, keepdims=True) + EPS)
        keep_ref[...] = (x * r * g_ref[...]).astype(BF16)
        hn_ref[...] = keep_ref[...]


def _norm_in_proj(h, g, w_in, n_cols, name):
    M, D = h.shape
    tm = _pick(M, (1056, 384, 128))
    tn = _pick(n_cols, (512, 256, 128))

    def body(h_ref, g_ref, w_ref, hn_ref, o_ref, keep_ref):
        _normed_rows(h_ref, g_ref, hn_ref, keep_ref)
        o_ref[...] = jnp.dot(keep_ref[...], w_ref[...], preferred_element_type=F32).astype(o_ref.dtype)

    rows = pl.BlockSpec((tm, D), lambda i, j: (i, 0))
    return pl.pallas_call(
        body, name=name, grid=(M // tm, n_cols // tn),
        in_specs=[rows, pl.BlockSpec((1, D), lambda i, j: (0, 0)), pl.BlockSpec((None, D, tn), lambda i, j: (0, 0, j))],
        out_specs=[rows, pl.BlockSpec((tm, tn), lambda i, j: (i, j))],
        out_shape=[jax.ShapeDtypeStruct((M, D), BF16), jax.ShapeDtypeStruct((M, n_cols), BF16)],
        scratch_shapes=[pltpu.VMEM((tm, D), BF16)],
        compiler_params=_params(("parallel", "arbitrary")),
    )(h, g, w_in)


def _ffn_up(h, norm_g, w_gate_t, w_up_t, name, job=None):
    M, D = h.shape
    F = w_gate_t.shape[1]
    tm, tf = _ffn_tiles(M, F)
    nt = (((1,), (1,)), ((), ()))

    def body(h_ref, ng_ref, wg_ref, wu_ref, hn_ref, g_ref, u_ref, a_ref):
        x = h_ref[...]
        r = lax.rsqrt(jnp.mean(x * x, axis=-1, keepdims=True) + EPS)
        hv = (x * r * ng_ref[...]).astype(BF16)
        hn_ref[...] = hv
        for c in range(0, F, tf):
            gv = lax.dot_general(hv, wg_ref[c:c + tf, :], nt, preferred_element_type=F32)
            uv = lax.dot_general(hv, wu_ref[c:c + tf, :], nt, preferred_element_type=F32)
            g_ref[:, c:c + tf] = gv.astype(g_ref.dtype)
            u_ref[:, c:c + tf] = uv.astype(u_ref.dtype)
            a_ref[:, c:c + tf] = (gv * _sigmoid(gv) * uv).astype(a_ref.dtype)

    rows = pl.BlockSpec((tm, D), lambda i: (i, 0))
    wide = pl.BlockSpec((tm, F), lambda i: (i, 0))
    return _call_with_job(
        body, job, name=name, grid=(M // tm,),
        in_specs=[rows, pl.BlockSpec((1, D), lambda i: (0, 0)), _resident(w_gate_t.shape), _resident(w_up_t.shape)],
        out_specs=[rows, wide, wide, wide],
        out_shape=[jax.ShapeDtypeStruct((M, D), BF16), jax.ShapeDtypeStruct((M, F), BF16),
                   jax.ShapeDtypeStruct((M, F), BF16), jax.ShapeDtypeStruct((M, F), BF16)],
        scratch_shapes=[], args=(h, norm_g, w_gate_t, w_up_t))


def _ffn_down_bwd(dh, w_down, g, u, name):
    M, D = dh.shape
    F = g.shape[1]
    tm, tf = _ffn_tiles(M, F)

    def body(d_ref, w_ref, g_ref, u_ref, dg_ref, du_ref):
        dhv = d_ref[...].astype(BF16)
        for c in range(0, F, tf):
            dv = lax.dot_general(dhv, w_ref[c:c + tf, :], (((1,), (1,)), ((), ())), preferred_element_type=F32)
            gv = g_ref[:, c:c + tf].astype(F32)
            s = _sigmoid(gv)
            du_ref[:, c:c + tf] = (dv * (gv * s)).astype(du_ref.dtype)
            dg_ref[:, c:c + tf] = (dv * u_ref[:, c:c + tf].astype(F32)
                                   * (s * (1.0 + gv * (1.0 - s)))).astype(dg_ref.dtype)

    wide = pl.BlockSpec((tm, F), lambda i: (i, 0))
    return pl.pallas_call(
        body, name=name, grid=(M // tm,),
        in_specs=[pl.BlockSpec((tm, D), lambda i: (i, 0)), _resident(w_down.shape), wide, wide],
        out_specs=[wide, wide],
        out_shape=[jax.ShapeDtypeStruct((M, F), BF16), jax.ShapeDtypeStruct((M, F), BF16)],
        compiler_params=_params(("parallel",)),
    )(dh, w_down, g, u)


def _mm_nn(pairs, out_dtype, name, residual=None, cols=None, job=None):
    M = pairs[0][0].shape[0]
    col0, N = cols if cols is not None else (0, pairs[0][1].shape[-1])
    tm = _pick(M, (1056, 384, 128))
    shallow = sum(p[0].shape[1] for p in pairs) <= 1024
    tn = _pick(math.gcd(N, col0) if col0 else N, ((1024,) if shallow else ()) + (640, 512, 256, 128))
    jb = col0 // tn
    n = len(pairs)

    def body(*refs):
        a_refs, w_refs = refs[:n], refs[n:2 * n]
        o_ref = refs[-1]
        acc = None
        for a_ref, w_ref in zip(a_refs, w_refs):
            d = jnp.dot(a_ref[...].astype(BF16), w_ref[...], preferred_element_type=F32)
            acc = d if acc is None else acc + d
        if residual is not None:
            acc = acc + refs[2 * n][...]
        o_ref[...] = acc.astype(o_ref.dtype)

    in_specs = [pl.BlockSpec((tm, a.shape[1]), lambda i, j: (i, 0)) for a, _, _, _ in pairs]
    for a, _, layer, kblk in pairs:
        in_specs.append(pl.BlockSpec((None, a.shape[1], tn), functools.partial(lambda i, j, l, kb: (l, kb, j + jb), l=layer, kb=kblk)))
    args = [p[0] for p in pairs] + [p[1] for p in pairs]
    if residual is not None:
        in_specs.append(pl.BlockSpec((tm, tn), lambda i, j: (i, j)))
        args.append(residual)
    (out,), arrived = _call_with_job(
        body, job, name=name, grid=(M // tm, N // tn), in_specs=in_specs,
        out_specs=[pl.BlockSpec((tm, tn), lambda i, j: (i, j))],
        out_shape=[jax.ShapeDtypeStruct((M, N), out_dtype)], scratch_shapes=[], args=args)
    return out if job is None else (out, arrived)


def _mm_nt(pairs, out_dtype, name):
    M = pairs[0][0].shape[0]
    K = pairs[0][1].shape[1]
    tm = _pick(M, (1056, 384, 128))
    shallow = sum(p[0].shape[1] for p in pairs) <= 1024
    tk = _pick(K, ((1024,) if shallow else ()) + (512, 1408, 256, 128))
    n = len(pairs)

    def body(*refs):
        d_refs, w_refs = refs[:n], refs[n:2 * n]
        o_ref = refs[-1]
        acc = None
        for d_ref, w_ref in zip(d_refs, w_refs):
            d = lax.dot_general(d_ref[...].astype(BF16), w_ref[...], (((1,), (1,)), ((), ())),
                                preferred_element_type=F32)
            acc = d if acc is None else acc + d
        o_ref[...] = acc.astype(o_ref.dtype)

    in_specs = [pl.BlockSpec((tm, d.shape[1]), lambda i, j: (i, 0)) for d, _, _, _ in pairs]
    for d, _, layer, cblk in pairs:
        in_specs.append(pl.BlockSpec((None, tk, d.shape[1]), functools.partial(lambda i, j, l, cb: (l, j, cb), l=layer, cb=cblk)))
    args = [p[0] for p in pairs] + [p[1] for p in pairs]
    return pl.pallas_call(
        body, name=name, grid=(M // tm, K // tk), in_specs=in_specs,
        out_specs=pl.BlockSpec((tm, tk), lambda i, j: (i, j)),
        out_shape=jax.ShapeDtypeStruct((M, K), out_dtype),
        compiler_params=_params(("parallel", "parallel")),
    )(*args)


def _mm_tn(a, b, name, col_sharded, chips=N_CHIPS):
    M, K = a.shape
    N = b.shape[1]
    tm = _pick(M, (1056, 384, 128))
    tk = _pick(K, (1024, 1408, 512, 256, 128))
    tn = N // N_CHIPS if col_sharded else _pick(N, (512, 128))

    def body(a_ref, b_ref, o_ref):
        @pl.when(pl.program_id(2) == 0)
        def _():
            o_ref[...] = jnp.zeros_like(o_ref)

        o_ref[...] += lax.dot_general(a_ref[...].astype(BF16), b_ref[...].astype(BF16),
                                      (((0,), (0,)), ((), ())), preferred_element_type=F32)

    if col_sharded:
        out_shape = jax.ShapeDtypeStruct((N_CHIPS, K, tn), F32)
        out_spec = pl.BlockSpec((None, tk, tn), lambda k, j, m: (j, k, 0))
    else:
        out_shape = jax.ShapeDtypeStruct((K, N), F32)
        out_spec = pl.BlockSpec((tk, tn), lambda k, j, m: (k, j))
    out = pl.pallas_call(
        body, name=name, grid=(K // tk, N // tn, M // tm),
        in_specs=[pl.BlockSpec((tm, tk), lambda k, j, m: (m, k)), pl.BlockSpec((tm, tn), lambda k, j, m: (m, j))],
        out_specs=out_spec, out_shape=out_shape,
        compiler_params=_params(("parallel", "parallel", "arbitrary")),
    )(a, b)
    return out if col_sharded else out.reshape(chips, K // chips, N)


def _stack_heads(x, scale=None):
    lane = lax.broadcasted_iota(jnp.int32, x.shape, 1)
    zero = jnp.zeros_like(x)
    lo = jnp.where(lane < HEAD_DIM, x, zero)
    hi = jnp.where(lane < HEAD_DIM, zero, x)
    out = jnp.concatenate([lo, hi], axis=0)
    return out if scale is None else out * scale


def _unstack_heads(x2):
    qb = x2.shape[0] // 2
    lane = lax.broadcasted_iota(jnp.int32, (qb, LANES), 1)
    return jnp.where(lane < HEAD_DIM, x2[:qb], x2[qb:])


def _dot2(x, m):
    xh = x.astype(BF16)
    xl = (x - xh.astype(F32)).astype(BF16)
    return jnp.dot(xh, m, preferred_element_type=F32) + jnp.dot(xl, m, preferred_element_type=F32)


ATTN_CHUNK = 32


def _in_lockstep(staged):
    waiting, live = list(staged), []
    while waiting or live:
        if waiting:
            live.append(waiting.pop(0))
        for gen in list(live):
            if next(gen, StopIteration) is StopIteration:
                live.remove(gen)


def _row_chunks():
    return [slice(r, r + ATTN_CHUNK) for r in range(0, 2 * QUERY_BLOCK, ATTN_CHUNK)]


def _chunk_valid(rows, older):
    shape = (rows.stop - rows.start, older + QUERY_BLOCK)
    r = (rows.start + lax.broadcasted_iota(jnp.int32, shape, 0)) & (QUERY_BLOCK - 1)
    return lax.broadcasted_iota(jnp.int32, shape, 1) < r + older


def _split_to(x, hi_ref, lo_ref, rows):
    xh = x.astype(BF16)
    hi_ref[rows, :] = xh
    lo_ref[rows, :] = (x - xh.astype(F32)).astype(BF16)


def _triangle(keys, strict):
    r = lax.broadcasted_iota(jnp.int32, (keys, keys), 0)
    c = lax.broadcasted_iota(jnp.int32, (keys, keys), 1)
    return jnp.where((r > c) if strict else (r >= c), 1.0, 0.0).astype(BF16)


def _fill_attn_consts(tri_ref, mask_ref):
    n = 2 * QUERY_BLOCK
    tri_ref[0] = _triangle(n, True)
    tri_ref[1] = _triangle(n, False)
    mask_ref[...] = jnp.where(_chunk_valid(slice(0, n), QUERY_BLOCK), 1.0, 0.0)


def _valid(mask_ref, rows, older):
    cols = slice(0, 2 * QUERY_BLOCK) if older else slice(QUERY_BLOCK, 2 * QUERY_BLOCK)
    return mask_ref[rows, cols] > 0.5


def _row_total(first):
    lane = lax.broadcasted_iota(jnp.int32, first.shape, 1)
    total = jnp.sum(jnp.where(lane == 0, first, 0.0), axis=1, keepdims=True)
    return jnp.broadcast_to(total, first.shape)


def _pairs_per_step(n_pairs, most):
    return max(g for g in (1, 2, 4) if g <= most and n_pairs % g == 0)


def _lanes(g):
    return slice(g * LANES, (g + 1) * LANES)


def _sweep_older(i, step, carry_ref, first):
    def cond(state):
        n, live = state
        return jnp.logical_and(n < i, live)

    def older(state):
        n, _ = state
        step(i - 1 - n, False)
        return n + 1, jnp.max(carry_ref[...]) > LOG_STICK_FLOOR

    lax.while_loop(cond, older, (first, jnp.max(carry_ref[...]) > LOG_STICK_FLOOR))


class _CommJob:
    def __init__(self, inputs, out_shapes, aliases, n_sems, start, finish):
        self.inputs, self.out_shapes, self.aliases, self.n_sems = list(inputs), list(out_shapes), aliases, n_sems
        self.start, self.finish = start, finish


def _merge_jobs(jobs):
    jobs = [j for j in jobs if j is not None]
    if len(jobs) <= 1:
        return jobs[0] if jobs else None
    spans, aliases = [], {}
    i0 = o0 = s0 = 0
    for j in jobs:
        spans.append((i0, o0, s0))
        aliases.update({i0 + a: o0 + b for a, b in j.aliases.items()})
        i0, o0, s0 = i0 + len(j.inputs), o0 + len(j.out_shapes), s0 + j.n_sems

    def run(which):
        def go(ins, outs, send_sems, recv_sems):
            for j, (i, o, s) in zip(jobs, spans):
                getattr(j, which)(ins[i:i + len(j.inputs)], outs[o:o + len(j.out_shapes)],
                                  send_sems.at[pl.ds(s, j.n_sems)], recv_sems.at[pl.ds(s, j.n_sems)])
        return go

    return _CommJob([a for j in jobs for a in j.inputs], [s for j in jobs for s in j.out_shapes], aliases, s0,
                    run("start"), run("finish"))


def _call_with_job(core_body, job, *, name, grid, in_specs, out_specs, out_shape, scratch_shapes, args):
    sem = ("arbitrary",) * len(grid)
    if job is None:
        res = pl.pallas_call(core_body, name=name, grid=grid, in_specs=in_specs, out_specs=out_specs,
                             out_shape=out_shape, scratch_shapes=scratch_shapes, compiler_params=_params(sem))(*args)
        return list(res), []
    n_in, n_out, n_scr = len(in_specs), len(out_specs), len(scratch_shapes)
    m_in, m_out = len(job.inputs), len(job.out_shapes)

    def body(*refs):
        at = 0
        parts = []
        for count in (n_in, m_in, n_out, m_out, n_scr, 2):
            parts.append(refs[at:at + count])
            at += count
        ins, job_in, outs, job_outs, scratch, (send_sems, recv_sems) = parts
        first = functools.reduce(jnp.logical_and, [pl.program_id(a) == 0 for a in range(len(grid))])
        last = functools.reduce(jnp.logical_and, [pl.program_id(a) == grid[a] - 1 for a in range(len(grid))])

        @pl.when(first)
        def _():
            job.start(job_in, job_outs, send_sems, recv_sems)

        core_body(*ins, *outs, *scratch)

        @pl.when(last)
        def _():
            job.finish(job_in, job_outs, send_sems, recv_sems)

    res = pl.pallas_call(
        body, name=name, grid=grid, in_specs=list(in_specs) + [ANY] * m_in, out_specs=list(out_specs) + [ANY] * m_out,
        out_shape=list(out_shape) + job.out_shapes,
        input_output_aliases={n_in + a: n_out + b for a, b in job.aliases.items()},
        scratch_shapes=list(scratch_shapes) + [pltpu.SemaphoreType.DMA((job.n_sems,)), pltpu.SemaphoreType.DMA((job.n_sems,))],
        compiler_params=_params(sem),
    )(*args, *job.inputs)
    return list(res[:n_out]), list(res[n_out:])


def _run_job(job, name):
    m_in, m_out = len(job.inputs), len(job.out_shapes)

    def body(*refs):
        job_in, job_outs = refs[:m_in], refs[m_in:m_in + m_out]
        send_sems, recv_sems = refs[m_in + m_out:]
        job.start(job_in, job_outs, send_sems, recv_sems)
        job.finish(job_in, job_outs, send_sems, recv_sems)

    return list(pl.pallas_call(
        body, name=name, in_specs=[ANY] * m_in, out_specs=[ANY] * m_out, out_shape=job.out_shapes,
        input_output_aliases=dict(job.aliases),
        scratch_shapes=[pltpu.SemaphoreType.DMA((job.n_sems,)), pltpu.SemaphoreType.DMA((job.n_sems,))],
    )(*job.inputs))


def _attn_fwd(qkv, sb_width, name, job=None):
    L = qkv.shape[0]
    QB = QUERY_BLOCK
    nb = L // QB
    n_pairs = sb_width // LANES
    G = _pairs_per_step(n_pairs, 4)
    W = G * LANES
    scale = 1.0 / math.sqrt(HEAD_DIM)

    def body(q_ref, k_ref, v_ref, o_ref, acc_ref, carry_ref, f32_ref, bf16_ref, tri_ref, mask_ref):
        i = pl.program_id(1)

        @pl.when(i == 0)
        def _():
            _fill_attn_consts(tri_ref, mask_ref)

        q2 = [_stack_heads(q_ref[:, _lanes(g)], scale) for g in range(G)]
        acc_ref[...] = jnp.zeros_like(acc_ref)
        carry_ref[...] = jnp.zeros_like(carry_ref)

        def step(g, j, older, masked):
            n = older + QB
            start = pl.multiple_of(j * QB, QB)
            kb = k_ref[pl.ds(start, n), _lanes(g)]
            vb = v_ref[pl.ds(start, n), _lanes(g)]
            zs, as_, bs = (f32_ref.at[g, t, :, :n] for t in range(3))
            hi, lo = (bf16_ref.at[g, t, :, :n] for t in range(2))
            zs[...] = lax.dot_general(q2[g], kb, (((1,), (1,)), ((), ())), preferred_element_type=F32)
            yield
            for rows in _row_chunks():
                z = zs[rows, :]
                sp = jnp.log(1.0 + jnp.exp(-jnp.abs(z)))
                b = jnp.minimum(-z, 0.0) - sp
                if masked:
                    b = jnp.where(_valid(mask_ref, rows, older),b, 0.0)
                as_[rows, :] = jnp.minimum(z, 0.0) - sp
                bs[rows, :] = b
                _split_to(b, hi, lo, rows)
            yield
            tri = tri_ref[0, :n, :n]
            zs[...] = (jnp.dot(hi[...], tri, preferred_element_type=F32)
                       + jnp.dot(lo[...], tri, preferred_element_type=F32))
            yield
            for rows in _row_chunks():
                excl = zs[rows, :]
                total = _row_total(excl[:, :LANES] + bs[rows, :LANES])
                if not masked:
                    excl = excl + jnp.tile(carry_ref[g, rows, :], (1, n // LANES))
                carry_ref[g, rows, :] += total
                w = jnp.exp(as_[rows, :] + excl)
                if masked:
                    w = jnp.where(_valid(mask_ref, rows, older),w, 0.0)
                _split_to(w, hi, lo, rows)
            yield
            acc_ref[g] += (jnp.dot(hi[...], vb, preferred_element_type=F32)
                           + jnp.dot(lo[...], vb, preferred_element_type=F32))

        @pl.when(i == 0)
        def _():
            _in_lockstep([step(g, 0, 0, True) for g in range(G)])

        @pl.when(i > 0)
        def _():
            _in_lockstep([step(g, i - 1, QB, True) for g in range(G)])
            for g in range(G):
                _sweep_older(i, lambda j, _, g=g: _in_lockstep([step(g, j, 0, False)]), carry_ref.at[g], 1)

        for g in range(G):
            o_ref[:, _lanes(g)] = _unstack_heads(acc_ref[g])

    n_steps = n_pairs // G
    (out,), job_out = _call_with_job(
        body, job, name=name, grid=(n_steps, nb),
        in_specs=[pl.BlockSpec((QB, W), lambda p, i: (i, p)),
                  pl.BlockSpec((L, W), lambda p, i: (0, n_steps + p)),
                  pl.BlockSpec((L, W), lambda p, i: (0, 2 * n_steps + p))],
        out_specs=[pl.BlockSpec((QB, W), lambda p, i: (i, p))],
        out_shape=[jax.ShapeDtypeStruct((L, sb_width), F32)],
        scratch_shapes=[pltpu.VMEM((G, 2 * QB, LANES), F32), pltpu.VMEM((G, 2 * QB, LANES), F32),
                        pltpu.VMEM((G, 3, 2 * QB, 2 * QB), F32), pltpu.VMEM((G, 2, 2 * QB, 2 * QB), BF16),
                        pltpu.VMEM((2, 2 * QB, 2 * QB), BF16), pltpu.VMEM((2 * QB, 2 * QB), F32)],
        args=(qkv, qkv, qkv))
    return out, job_out


def _attn_bwd(qkv, o, dmixed, sb_width, name, job=None):
    L = qkv.shape[0]
    QB = QUERY_BLOCK
    nb = L // QB
    n_pairs = sb_width // LANES
    G = _pairs_per_step(n_pairs, 2)
    W = G * LANES
    scale = 1.0 / math.sqrt(HEAD_DIM)

    def body(q_ref, k_ref, v_ref, o_ref, do_ref, dq_ref, dk_ref, dv_ref,
             dq_acc, dk_acc, dv_acc, ce_ref, cr_ref, dtot_ref, f32_ref, bf16_ref, tri_ref, mask_ref):
        i = pl.program_id(1)

        @pl.when(i == 0)
        def _():
            dk_acc[...] = jnp.zeros_like(dk_acc)
            dv_acc[...] = jnp.zeros_like(dv_acc)
            _fill_attn_consts(tri_ref, mask_ref)

        q2s = [_stack_heads(q_ref[:, _lanes(g)], scale) for g in range(G)]
        do2s = [_stack_heads(do_ref[:, _lanes(g)].astype(BF16)) for g in range(G)]
        ones = jnp.ones((LANES, LANES), BF16)
        for g in range(G):
            ov = o_ref[:, _lanes(g)]
            dtot_ref[g] = _dot2(do2s[g].astype(F32) * jnp.concatenate([ov, ov], axis=0), ones)
        dq_acc[...] = jnp.zeros_like(dq_acc)
        ce_ref[...] = jnp.zeros_like(ce_ref)
        cr_ref[...] = jnp.zeros_like(cr_ref)

        def step(g, j, older, masked):
            n = older + QB
            wide = n // LANES
            q2, do2 = q2s[g], do2s[g]
            start = pl.multiple_of(j * QB, QB)
            kb = k_ref[pl.ds(start, n), _lanes(g)]
            vb = v_ref[pl.ds(start, n), _lanes(g)]
            zs, as_, bs, betas, gs = (f32_ref.at[g, t, :, :n] for t in range(5))
            hi, lo, wb = (bf16_ref.at[g, t, :, :n] for t in range(3))
            nt = (((1,), (1,)), ((), ()))
            zs[...] = lax.dot_general(q2, kb, nt, preferred_element_type=F32)
            gs[...] = lax.dot_general(do2, vb, nt, preferred_element_type=F32)
            yield
            for rows in _row_chunks():
                z = zs[rows, :]
                e = jnp.exp(-jnp.abs(z))
                sp = jnp.log(1.0 + e)
                b = jnp.minimum(-z, 0.0) - sp
                if masked:
                    b = jnp.where(_valid(mask_ref, rows, older),b, 0.0)
                rinv = 1.0 / (1.0 + e)
                as_[rows, :] = jnp.minimum(z, 0.0) - sp
                bs[rows, :] = b
                betas[rows, :] = jnp.where(z >= 0.0, rinv, e * rinv)
                _split_to(b, hi, lo, rows)
            yield
            tri = tri_ref[0, :n, :n]
            zs[...] = (jnp.dot(hi[...], tri, preferred_element_type=F32)
                       + jnp.dot(lo[...], tri, preferred_element_type=F32))
            yield
            for rows in _row_chunks():
                excl = zs[rows, :]
                total = _row_total(excl[:, :LANES] + bs[rows, :LANES])
                if not masked:
                    excl = excl + jnp.tile(ce_ref[g, rows, :], (1, wide))
                ce_ref[g, rows, :] += total
                w = jnp.exp(as_[rows, :] + excl)
                if masked:
                    w = jnp.where(_valid(mask_ref, rows, older),w, 0.0)
                wb[rows, :] = w.astype(BF16)
                gw = w * gs[rows, :]
                gs[rows, :] = gw
                _split_to(gw, hi, lo, rows)
            yield
            tri = tri_ref[1, :n, :n]
            zs[...] = (jnp.dot(hi[...], tri, preferred_element_type=F32)
                       + jnp.dot(lo[...], tri, preferred_element_type=F32))
            yield
            for rows in _row_chunks():
                rinc = zs[rows, :]
                total = _row_total(rinc[:, :LANES])
                if not masked:
                    rinc = rinc + jnp.tile(cr_ref[g, rows, :], (1, wide))
                cr_ref[g, rows, :] += total
                beta = betas[rows, :]
                dz = gs[rows, :] * (1.0 - beta) - beta * (jnp.tile(dtot_ref[g, rows, :], (1, wide)) - rinc)
                if masked:
                    dz = jnp.where(_valid(mask_ref, rows, older),dz, 0.0)
                hi[rows, :] = dz.astype(BF16)
            yield
            dzb = hi[...]
            dq_acc[g] += jnp.dot(dzb, kb, preferred_element_type=F32)
            dk_acc[pl.ds(start, n), _lanes(g)] += lax.dot_general(
                dzb, q2, (((0,), (0,)), ((), ())), preferred_element_type=F32)
            dv_acc[pl.ds(start, n), _lanes(g)] += lax.dot_general(
                wb[...], do2, (((0,), (0,)), ((), ())), preferred_element_type=F32)

        @pl.when(i == 0)
        def _():
            _in_lockstep([step(g, 0, 0, True) for g in range(G)])

        @pl.when(i > 0)
        def _():
            _in_lockstep([step(g, i - 1, QB, True) for g in range(G)])
            for g in range(G):
                _sweep_older(i, lambda j, _, g=g: _in_lockstep([step(g, j, 0, False)]), ce_ref.at[g], 1)

        for g in range(G):
            dq_ref[:, _lanes(g)] = (_unstack_heads(dq_acc[g]) * scale).astype(dq_ref.dtype)

        @pl.when(i == nb - 1)
        def _():
            dk_ref[...] = dk_acc[...].astype(dk_ref.dtype)
            dv_ref[...] = dv_acc[...].astype(dv_ref.dtype)

    n_steps = n_pairs // G
    blk = pl.BlockSpec((QB, W), lambda p, i: (i, p))
    col = pl.BlockSpec((L, W), lambda p, i: (0, p))
    return _call_with_job(
        body, job, name=name, grid=(n_steps, nb),
        in_specs=[blk,
                  pl.BlockSpec((L, W), lambda p, i: (0, n_steps + p)),
                  pl.BlockSpec((L, W), lambda p, i: (0, 2 * n_steps + p)),
                  blk, blk],
        out_specs=[blk, col, col],
        out_shape=[jax.ShapeDtypeStruct((L, sb_width), BF16)] * 3,
        scratch_shapes=[pltpu.VMEM((G, 2 * QB, LANES), F32), pltpu.VMEM((L, W), F32),
                        pltpu.VMEM((L, W), F32), pltpu.VMEM((G, 2 * QB, LANES), F32),
                        pltpu.VMEM((G, 2 * QB, LANES), F32), pltpu.VMEM((G, 2 * QB, LANES), F32),
                        pltpu.VMEM((G, 5, 2 * QB, 2 * QB), F32), pltpu.VMEM((G, 3, 2 * QB, 2 * QB), BF16),
                        pltpu.VMEM((2, 2 * QB, 2 * QB), BF16), pltpu.VMEM((2 * QB, 2 * QB), F32)],
        args=(qkv, qkv, qkv, o, dmixed))


def _conv_tile(L):
    return _pick(L, (384, 128))


def _glu(x, C):
    return x[:, :C] * _sigmoid(x[:, C:])


CONV_CHUNK = 32
SHIFT_TAIL = 24


def _fill_shifted(src_ref, dst_ref):
    n = dst_ref.shape[1]
    for r in range(1, 8):
        dst_ref[r - 1] = src_ref[r:r + n, :]


def _rows_at(src_ref, shifted_ref, start, n):
    q, r = divmod(start, 8)
    if r == 0:
        return src_ref[start:start + n, :]
    return shifted_ref[r - 1, 8 * q:8 * q + n, :]


def _conv_fwd(cacg, dw_w, dw_b, ln_g, ln_b, name, job=None):
    L, C2 = cacg.shape
    C = C2 // 2
    T = _conv_tile(L)
    H = CONV_HALO
    K = dw_w.shape[0]
    CH = CONV_CHUNK

    def body(x_ref, prev_ref, w_ref, b_ref, g_ref, beta_ref, o_ref, y_ref, u_ref, us_ref):
        i = pl.program_id(0)
        u_ref[0:H, :] = jnp.where(i > 0, _glu(prev_ref[...], C), 0.0)
        u_ref[H:, :] = _glu(x_ref[...], C)
        _fill_shifted(u_ref, us_ref)
        for c0 in range(0, T, CH):
            y = jnp.broadcast_to(b_ref[...], (CH, C))
            for k in range(K):
                y = y + w_ref[k:k + 1, :] * _rows_at(u_ref, us_ref, c0 + H - (K - 1) + k, CH)
            y_ref[c0:c0 + CH, :] = y
            mu = jnp.mean(y, axis=-1, keepdims=True)
            yc = y - mu
            rstd = lax.rsqrt(jnp.mean(yc * yc, axis=-1, keepdims=True) + EPS)
            ln = yc * rstd * g_ref[...] + beta_ref[...]
            o_ref[c0:c0 + CH, :] = (ln * _sigmoid(ln)).astype(o_ref.dtype)

    vec = pl.BlockSpec((1, C), lambda i: (0, 0))
    tile = pl.BlockSpec((T, C), lambda i: (i, 0))
    (out, y), arrived = _call_with_job(
        body, job, name=name, grid=(L // T,),
        in_specs=[pl.BlockSpec((T, C2), lambda i: (i, 0)),
                  pl.BlockSpec((H, C2), lambda i: (jnp.maximum(i * (T // H) - 1, 0), 0)),
                  pl.BlockSpec((K, C), lambda i: (0, 0)), vec, vec, vec],
        out_specs=[tile, tile],
        out_shape=[jax.ShapeDtypeStruct((L, C), BF16), jax.ShapeDtypeStruct((L, C), F32)],
        scratch_shapes=[pltpu.VMEM((T + H, C), F32), pltpu.VMEM((7, T + SHIFT_TAIL, C), F32)],
        args=(cacg, cacg, dw_w, dw_b, ln_g, ln_b))
    return out, y, arrived


def _conv_bwd(cacg, y, dmixed, dw_w, ln_g, ln_b, name, job=None):
    L, C2 = cacg.shape
    C = C2 // 2
    T = _conv_tile(L)
    H = CONV_HALO
    K = dw_w.shape[0]
    nt = L // T
    TE = T + H

    CH = CONV_CHUNK

    def body(x_ref, prev_ref, y_ref, ynext_ref, d_ref, dnext_ref, w_ref, g_ref, beta_ref,
             dca_ref, dcg_ref, dwt_ref, db_ref, dg_ref, dbeta_ref, u_ref, us_ref, dy_ref, dys_ref):
        i = pl.program_id(0)
        last = i == nt - 1

        @pl.when(i == 0)
        def _():
            dwt_ref[...] = jnp.zeros_like(dwt_ref)
            db_ref[...] = jnp.zeros_like(db_ref)
            dg_ref[...] = jnp.zeros_like(dg_ref)
            dbeta_ref[...] = jnp.zeros_like(dbeta_ref)

        u_ref[0:H, :] = jnp.where(i > 0, _glu(prev_ref[...], C), 0.0)
        u_ref[H:, :] = _glu(x_ref[...], C)
        _fill_shifted(u_ref, us_ref)
        dg_acc = jnp.zeros((1, C), F32)
        dbeta_acc = jnp.zeros((1, C), F32)
        db_acc = jnp.zeros((1, C), F32)
        for c0 in range(0, TE, CH):
            y = y_ref[c0:c0 + CH, :] if c0 < T else ynext_ref[c0 - T:c0 - T + CH, :]
            mu = jnp.mean(y, axis=-1, keepdims=True)
            yc = y - mu
            rstd = lax.rsqrt(jnp.mean(yc * yc, axis=-1, keepdims=True) + EPS)
            yh = yc * rstd
            ln = yh * g_ref[...] + beta_ref[...]
            s = _sigmoid(ln)
            dout = d_ref[c0:c0 + CH, :] if c0 < T else jnp.where(last, 0.0, dnext_ref[c0 - T:c0 - T + CH, :])
            dln = dout * (s * (1.0 + ln * (1.0 - s)))
            dyh = dln * g_ref[...]
            dy = rstd * (dyh - jnp.mean(dyh, axis=-1, keepdims=True)
                         - yh * jnp.mean(dyh * yh, axis=-1, keepdims=True))
            dy_ref[c0:c0 + CH, :] = dy
            if c0 < T:
                dg_acc = dg_acc + jnp.sum(dln * yh, axis=0, keepdims=True)
                dbeta_acc = dbeta_acc + jnp.sum(dln, axis=0, keepdims=True)
                db_acc = db_acc + jnp.sum(dy, axis=0, keepdims=True)
        dg_ref[...] += dg_acc
        dbeta_ref[...] += dbeta_acc
        db_ref[...] += db_acc
        _fill_shifted(dy_ref, dys_ref)
        for k in range(K):
            dwt_ref[k:k + 1, :] += jnp.sum(
                dy_ref[0:T, :] * _rows_at(u_ref, us_ref, H - (K - 1) + k, T), axis=0, keepdims=True)
        for c0 in range(0, T, CH):
            du = jnp.zeros((CH, C), F32)
            for k in range(K):
                du = du + w_ref[k:k + 1, :] * _rows_at(dy_ref, dys_ref, c0 + (K - 1) - k, CH)
            x = x_ref[c0:c0 + CH, :]
            sg = _sigmoid(x[:, C:])
            dca_ref[c0:c0 + CH, :] = (du * sg).astype(dca_ref.dtype)
            dcg_ref[c0:c0 + CH, :] = (du * x[:, :C] * sg * (1.0 - sg)).astype(dcg_ref.dtype)

    nh = L // H
    vec = pl.BlockSpec((1, C), lambda i: (0, 0))
    row = pl.BlockSpec((T, C), lambda i: (i, 0))
    after = lambda i: jnp.minimum((i + 1) * (T // H), nh - 1)
    return _call_with_job(
        body, job, name=name, grid=(nt,),
        in_specs=[pl.BlockSpec((T, C2), lambda i: (i, 0)),
                  pl.BlockSpec((H, C2), lambda i: (jnp.maximum(i * (T // H) - 1, 0), 0)),
                  row, pl.BlockSpec((H, C), lambda i: (after(i), 0)),
                  pl.BlockSpec((T, C), lambda i: (i, 1)), pl.BlockSpec((H, C), lambda i: (after(i), 1)),
                  pl.BlockSpec((K, C), lambda i: (0, 0)), vec, vec],
        out_specs=[row, row, pl.BlockSpec((H, C), lambda i: (0, 0)), vec, vec, vec],
        out_shape=[jax.ShapeDtypeStruct((L, C), BF16), jax.ShapeDtypeStruct((L, C), BF16),
                   jax.ShapeDtypeStruct((H, C), F32), jax.ShapeDtypeStruct((1, C), F32),
                   jax.ShapeDtypeStruct((1, C), F32), jax.ShapeDtypeStruct((1, C), F32)],
        scratch_shapes=[pltpu.VMEM((T + H, C), F32), pltpu.VMEM((7, T + SHIFT_TAIL, C), F32),
                        pltpu.VMEM((TE, C), F32), pltpu.VMEM((7, T + SHIFT_TAIL, C), F32)],
        args=(cacg, cacg, y, y, dmixed, dmixed, dw_w, ln_g, ln_b))


def _local_step(h0, target, n_meta, seq, norms, conv_p, wts, final_g, gather_behind, reducer):
    mix_g, ffn_g = norms
    dw_w, dw_b, ln_g, ln_b = conv_p
    depth = mix_g.shape[0]
    C = dw_b.shape[-1]
    sbw = (wts["w_in"][0].shape[-1] - 2 * C) // 3
    assert sbw == C, "the mixer halves must have equal width"
    row = lambda a, i: a[i][None, :]

    h = h0
    saved = []
    for i in range(depth):
        hn, proj_qkv = _norm_in_proj(h, row(mix_g, i), wts["w_in"][i], 3 * sbw, f"in_qkv_{i}")
        cacg = _mm_nn([(hn, wts["w_in"][i], 0, 0)], F32, f"in_conv_{i}", cols=(3 * sbw, 2 * C))
        def hosting(kind):
            job, keys = gather_behind.get((kind, i), (None, ()))

            def sink(arrived):
                for (wname, wl), arr in zip(keys, arrived):
                    wts[wname][wl] = arr

            return job, sink

        job, sink = hosting("attn")
        attn, arrived = _attn_fwd(proj_qkv, sbw, f"attn_fwd_{i}", job)
        sink(arrived)
        job, sink = hosting("conv")
        conv, conv_y, arrived = _conv_fwd(cacg, dw_w[i], row(dw_b, i), row(ln_g, i), row(ln_b, i),
                                          f"conv_fwd_{i}", job)
        sink(arrived)
        h_mid = _mm_nn([(attn, wts["w_out"][i], 0, 0), (conv, wts["w_out"][i], 0, 1)], F32, f"out_proj_{i}",
                       residual=h)
        job, sink = hosting("ffn")
        (hn2, g, u, act), arrived = _ffn_up(h_mid, row(ffn_g, i), wts["w_gate_t"][i], wts["w_up_t"][i],
                                            f"ffn_up_{i}", job)
        sink(arrived)
        job, sink = hosting("down")
        h_out = _mm_nn([(act, wts["w_down"][i], 0, 0)], F32, f"down_{i}", residual=h_mid, job=job)
        if job is not None:
            h_out, arrived = h_out
            sink(arrived)
        saved.append((h, hn, proj_qkv, cacg, attn, conv, conv_y, h_mid, hn2, g, u, act))
        h = h_out

    loss, dh, d_final_g = _loss_head(h, final_g[None, :], target, n_meta, seq, "loss_head")

    grads = {k: [None] * depth for k in ("mix_g", "ffn_g", "dw_w", "dw_b", "ln_g", "ln_b")}
    for i in reversed(range(depth)):
        h_in, hn, proj_qkv, cacg, attn, conv, conv_y, h_mid, hn2, g, u, act = saved[i]
        big = {}
        dg, du = _ffn_down_bwd(dh, wts["w_down"][i], g, u, f"ffn_down_bwd_{i}")
        big["w_down"] = _mm_tn(act, dh, f"dw_down_{i}", col_sharded=False)
        dhn2 = _mm_nn([(dg, wts["w_gate_t"][i], 0, 0), (du, wts["w_up_t"][i], 0, 0)], F32, f"d_hn2_{i}")
        big["w_gate_t"] = _mm_tn(dg, hn2, f"dw_gate_{i}", col_sharded=False)
        big["w_up_t"] = _mm_tn(du, hn2, f"dw_up_{i}", col_sharded=False)
        sib_job, sib_sink = reducer.to_sibling(i, big)
        dh, d_ffn, arrived = _rmsnorm_bwd(h_mid, row(ffn_g, i), dhn2, dh, f"ffn_norm_bwd_{i}", sib_job)
        sib_sink(arrived)
        dmixed = _mm_nt([(dh, wts["w_out"][i], 0, 0)], F32, f"d_mixed_{i}")
        dw_out = jnp.concatenate([_mm_tn(attn, dh, f"dw_out_attn_{i}", col_sharded=False, chips=2),
                                  _mm_tn(conv, dh, f"dw_out_conv_{i}", col_sharded=False, chips=2)], axis=0)
        sib_job, sib_sink = reducer.to_sibling(i, {"w_out": dw_out})
        x_job, x_sink = reducer.take()
        (dq, dk, dv), arrived = _attn_bwd(proj_qkv, attn, dmixed, sbw, f"attn_bwd_{i}",
                                          _merge_jobs([sib_job, x_job]))
        sib_sink(arrived[:len(sib_job.out_shapes)])
        x_sink(arrived[len(sib_job.out_shapes):])
        x_job, x_sink = reducer.take()
        (dca, dcg, d_dw, d_b, d_lg, d_lb), arrived = _conv_bwd(
            cacg, conv_y, dmixed, dw_w[i], row(ln_g, i), row(ln_b, i), f"conv_bwd_{i}", x_job)
        x_sink(arrived)
        grads["dw_w"][i] = d_dw
        grads["dw_b"][i], grads["ln_g"][i], grads["ln_b"][i] = d_b, d_lg, d_lb
        dproj = jnp.concatenate([dq, dk, dv, dca, dcg], axis=1)
        dhn = _mm_nt([(dproj, wts["w_in"][i], 0, 0)], F32, f"d_hn_{i}")
        reducer.ready(i, {"w_in": _mm_tn(hn, dproj, f"dw_in_{i}", col_sharded=True)})
        if i == 0:
            x_job, x_sink = reducer.take()
            dh, d_mix, arrived = _rmsnorm_bwd(h_in, row(mix_g, i), dhn, dh, f"mix_norm_bwd_{i}", x_job)
            x_sink(arrived)
        else:
            dh, d_mix = _rmsnorm_bwd(h_in, row(mix_g, i), dhn, dh, f"mix_norm_bwd_{i}")
        grads["mix_g"][i], grads["ffn_g"][i] = d_mix, d_ffn
    grads["final_g"] = d_final_g
    return loss, dh, grads


ANY = pl.BlockSpec(memory_space=pl.ANY)


def _position():
    return lax.axis_index("x"), lax.axis_index("y"), lax.axis_index("c")


def _chip_at(x, y, k):
    return (1 - x if k & 2 else x), (1 - y if k & 1 else y)


def _half_rows(ref, half, rows, base=0):
    start = pl.multiple_of(base + half * rows, 8)
    lead = (slice(None),) * (len(ref.shape) - 2)
    return ref.at[(*lead, pl.ds(start, rows), slice(None))]


def _gather_job(fulls, shard_shapes, col_sharded):
    n = len(fulls)

    def tools(f_refs, send_sems, recv_sems):
        def block(wi, chip, half):
            _, R, C = shard_shapes[wi]
            if col_sharded[wi]:
                cols = pl.ds(pl.multiple_of(chip * C, LANES), C)
                return f_refs[wi].at[:, pl.ds(pl.multiple_of(half * (R // 2), 8), R // 2), cols]
            return _half_rows(f_refs[wi], half, R // 2, base=chip * R)

        def copy(wi, slot, blk, to):
            return pltpu.make_async_remote_copy(
                src_ref=blk, dst_ref=blk, send_sem=send_sems.at[6 * wi + slot],
                recv_sem=recv_sems.at[6 * wi + slot], device_id=to, device_id_type=MESH)

        return block, copy

    def start(_, f_refs, send_sems, recv_sems):
        block, copy = tools(f_refs, send_sems, recv_sems)
        x, y, c = _position()
        me = 2 * x + y
        for wi in range(n):
            for k in (1, 2, 3):
                copy(wi, k - 1, block(wi, me, c), (*_chip_at(x, y, k), c)).start()

    def finish(_, f_refs, send_sems, recv_sems):
        block, copy = tools(f_refs, send_sems, recv_sems)
        x, y, c = _position()
        me = 2 * x + y
        for wi in range(n):
            for k in (1, 2, 3):
                landed = block(wi, me ^ k, c)
                copy(wi, k - 1, landed, (x, y, c)).wait_recv()
                copy(wi, 2 + k, landed, (x, y, 1 - c)).start()
        for wi in range(n):
            for k in (1, 2, 3):
                copy(wi, 2 + k, block(wi, me ^ k, 1 - c), (x, y, c)).wait_recv()
        for wi in range(n):
            for k in (1, 2, 3):
                copy(wi, k - 1, block(wi, me, c), (x, y, c)).wait_send()
                copy(wi, 2 + k, block(wi, me ^ k, c), (x, y, c)).wait_send()

    return _CommJob(fulls, [jax.ShapeDtypeStruct(f.shape, f.dtype) for f in fulls], {i: i for i in range(n)},
                    6 * n, start, finish)


def _place_shard(w, layer, chip, col_sharded, dtype, name):
    _, R, C = w.shape
    tr = _pick(R, (256, 352, 128, 48))
    nr = R // tr

    def body(chip_ref, w_ref, o_ref):
        o_ref[...] = w_ref[...].astype(dtype)

    if col_sharded:
        shape = (1, R, N_CHIPS * C)
        out_spec = pl.BlockSpec((None, tr, C), lambda r, chip_ref: (0, r, chip_ref[0]))
    else:
        shape = (1, N_CHIPS * R, C)
        out_spec = pl.BlockSpec((None, tr, C), lambda r, chip_ref: (0, chip_ref[0] * nr + r, 0))
    grid_spec = pltpu.PrefetchScalarGridSpec(
        num_scalar_prefetch=1, grid=(nr,),
        in_specs=[pl.BlockSpec((None, tr, C), lambda r, chip_ref: (layer, r, 0))], out_specs=out_spec)
    return pl.pallas_call(
        body, name=name, grid_spec=grid_spec, out_shape=jax.ShapeDtypeStruct(shape, dtype),
        compiler_params=_params(("parallel",)),
    )(chip, w)


def _sibling_job(grads):
    n = len(grads)

    def copies(g_refs, l_refs, send_sems, recv_sems):
        x, y, c = _position()
        return [pltpu.make_async_remote_copy(
            src_ref=_half_rows(g_refs[wi], 1 - c, grads[wi].shape[1] // 2), dst_ref=l_refs[wi],
            send_sem=send_sems.at[wi], recv_sem=recv_sems.at[wi],
            device_id=(x, y, 1 - c), device_id_type=MESH) for wi in range(n)]

    def start(*refs):
        for cp in copies(*refs):
            cp.start()

    def finish(*refs):
        for cp in copies(*refs):
            cp.wait()

    outs = [jax.ShapeDtypeStruct((g.shape[0], g.shape[1] // 2, g.shape[2]), g.dtype) for g in grads]
    return _CommJob(grads, outs, {}, n, start, finish)


def _chip_sum(g, landed, core, name):
    _, R, C = g.shape
    hr = R // 2
    tr = _pick(hr, (256, 352, 128))
    nr = hr // tr

    def body(c_ref, g_ref, l_ref, o_ref):
        o_ref[...] = (g_ref[...] + l_ref[...]).astype(BF16)

    grid_spec = pltpu.PrefetchScalarGridSpec(
        num_scalar_prefetch=1, grid=(N_CHIPS, nr),
        in_specs=[pl.BlockSpec((None, tr, C), lambda j, r, c_ref: (j, c_ref[0] * nr + r, 0)),
                  pl.BlockSpec((None, tr, C), lambda j, r, c_ref: (j, r, 0))],
        out_specs=pl.BlockSpec((None, tr, C), lambda j, r, c_ref: (j, r, 0)))
    return pl.pallas_call(
        body, name=name, grid_spec=grid_spec, out_shape=jax.ShapeDtypeStruct((N_CHIPS, hr, C), BF16),
        compiler_params=_params(("parallel", "parallel")),
    )(core, g, landed)


def _across_job(parts):
    n = len(parts)

    def copy(p_refs, l_refs, send_sems, recv_sems, wi, k, to):
        x, y, _ = _position()
        me = 2 * x + y
        return pltpu.make_async_remote_copy(
            src_ref=p_refs[wi].at[me ^ k], dst_ref=l_refs[wi].at[me],
            send_sem=send_sems.at[3 * wi + k - 1], recv_sem=recv_sems.at[3 * wi + k - 1],
            device_id=to, device_id_type=MESH)

    def start(p_refs, l_refs, send_sems, recv_sems):
        x, y, c = _position()
        for wi in range(n):
            for k in (1, 2, 3):
                copy(p_refs, l_refs, send_sems, recv_sems, wi, k, (*_chip_at(x, y, k), c)).start()

    def finish(p_refs, l_refs, send_sems, recv_sems):
        x, y, c = _position()
        me = 2 * x + y
        for wi in range(n):
            for k in (1, 2, 3):
                slot = l_refs[wi].at[me ^ k]
                pltpu.make_async_remote_copy(
                    src_ref=slot, dst_ref=slot, send_sem=send_sems.at[3 * wi + k - 1],
                    recv_sem=recv_sems.at[3 * wi + k - 1], device_id=(x, y, c), device_id_type=MESH).wait_recv()
        for wi in range(n):
            for k in (1, 2, 3):
                copy(p_refs, l_refs, send_sems, recv_sems, wi, k, (x, y, c)).wait_send()

    return _CommJob(parts, [jax.ShapeDtypeStruct(p.shape, p.dtype) for p in parts], {}, 3 * n, start, finish)


class _Reducer:
    def __init__(self, core):
        self.core, self.parts, self.across, self.pending = core, {}, {}, []

    def to_sibling(self, layer, big):
        names = list(big)
        flat = [big[k] for k in names]

        def sink(landed):
            for k, g, la in zip(names, flat, landed):
                self.parts[k, layer] = _chip_sum(g, la, self.core, f"chip_sum_{k}_{layer}")
                self.pending.append((k, layer))

        return _sibling_job(flat), sink

    def ready(self, layer, big):
        job, sink = self.to_sibling(layer, big)
        sink(_run_job(job, f"grads_to_sibling_{next(iter(big))}_{layer}"))

    def take(self):
        keys, self.pending = self.pending, []
        if not keys:
            return None, lambda results: None

        def sink(results):
            self.across.update(zip(keys, results))

        return _across_job([self.parts[key] for key in keys]), sink


def _sum_chips(parts, landed, where, layer, depth, prev, name):
    _, hr, C = landed.shape
    tr = _pick(hr, (256, 352, 128))
    nr = hr // tr

    def body(*refs):
        own_ref, slots, o_ref = refs[1], refs[2:2 + N_CHIPS], refs[-1]
        chip = refs[0][0]
        total = None
        for q in range(N_CHIPS):
            term = jnp.where(chip == q, own_ref[...], slots[q][...]).astype(F32)
            total = term if total is None else total + term
        o_ref[...] = total

    def slot_spec(q):
        return pl.BlockSpec((None, tr, C), lambda r, w: (jnp.where(w[0] == q, (q + 1) % N_CHIPS, q), r, 0))

    in_specs = [pl.BlockSpec((None, tr, C), lambda r, w: (w[0], r, 0))] + [slot_spec(q) for q in range(N_CHIPS)]
    args = [where, parts] + [landed] * N_CHIPS
    aliases = {}
    if prev is not None:
        in_specs.append(ANY)
        args.append(prev)
        aliases = {len(args) - 1: 0}
    grid_spec = pltpu.PrefetchScalarGridSpec(
        num_scalar_prefetch=1, grid=(nr,), in_specs=in_specs,
        out_specs=pl.BlockSpec((None, tr, C), lambda r, w: (layer, w[1] * nr + r, 0)))
    return pl.pallas_call(
        body, name=name, grid_spec=grid_spec, out_shape=jax.ShapeDtypeStruct((depth, 2 * hr, C), F32),
        input_output_aliases=aliases, compiler_params=_params(("parallel",)),
    )(*args)


def _rs_join_halves(reduced):
    n = len(reduced)

    def body(*refs):
        o_refs = refs[n:2 * n]
        send_sems, recv_sems = refs[2 * n:]
        x, y, c = _position()
        sent = []
        for wi in range(n):
            hr = reduced[wi].shape[1] // 2
            mine = _half_rows(o_refs[wi], c, hr)
            cp = pltpu.make_async_remote_copy(
                src_ref=mine, dst_ref=mine, send_sem=send_sems.at[wi], recv_sem=recv_sems.at[wi],
                device_id=(x, y, 1 - c), device_id_type=MESH)
            cp.start()
            sent.append(cp)
        for wi in range(n):
            hr = reduced[wi].shape[1] // 2
            theirs = _half_rows(o_refs[wi], 1 - c, hr)
            pltpu.make_async_remote_copy(
                src_ref=theirs, dst_ref=theirs, send_sem=send_sems.at[wi], recv_sem=recv_sems.at[wi],
                device_id=(x, y, c), device_id_type=MESH).wait_recv()
        for cp in sent:
            cp.wait_send()

    return pl.pallas_call(
        body, name="grads_join_halves", out_shape=[jax.ShapeDtypeStruct(r.shape, r.dtype) for r in reduced],
        in_specs=[ANY] * n, out_specs=[ANY] * n, input_output_aliases={i: i for i in range(n)},
        scratch_shapes=[pltpu.SemaphoreType.DMA((n,)), pltpu.SemaphoreType.DMA((n,))],
    )(*reduced)


def _adam_math(w, g, m, v):
    m = ADAM_B1 * m + (1.0 - ADAM_B1) * g
    v = ADAM_B2 * v + (1.0 - ADAM_B2) * jnp.square(g)
    m_hat = m / (1.0 - ADAM_B1 ** ADAM_STEP)
    v_hat = v / (1.0 - ADAM_B2 ** ADAM_STEP)
    delta = -ADAM_LR * (m_hat / (jnp.sqrt(v_hat) + ADAM_EPS) + ADAM_WD * w)
    return delta, m, v


def _small_reduce_adam(vec_grads, dww_grads, final_grad, meta_grad, state):
    vec_names = list(vec_grads)
    depth = len(dww_grads)
    D = final_grad.shape[1]
    n_meta = meta_grad.shape[0]
    taps_pad, C = dww_grads[0].shape
    names = vec_names + ["final", "meta", "dww"]
    at, row0 = 0, {}
    for k in vec_names:
        row0[k] = at
        at += depth
    row0["final"] = at
    at = -(-(at + 1) // 8) * 8
    row0["meta"] = at
    at += -(-n_meta // 8) * 8
    row0["dww"] = at
    rows = at + depth * taps_pad
    lanes = max(D, C)
    n_g = len(vec_names) * depth + depth + 2

    def body(*refs):
        g_refs = refs[:n_g]
        st = refs[n_g:n_g + 3 * len(names)]
        outs = refs[n_g + 3 * len(names):n_g + 7 * len(names)]
        slab, land, send_sems, recv_sems = refs[n_g + 7 * len(names):]
        x, y, c = _position()
        me = 4 * x + 2 * y + c
        chip = 2 * x + y
        slab[...] = jnp.zeros_like(slab)
        it = iter(g_refs)
        for k in vec_names:
            for l in range(depth):
                g_ref = next(it)
                slab[row0[k] + l:row0[k] + l + 1, 0:g_ref.shape[1]] = g_ref[...]
        for l in range(depth):
            slab[row0["dww"] + l * taps_pad:row0["dww"] + (l + 1) * taps_pad, 0:C] = next(it)[...]
        slab[row0["final"]:row0["final"] + 1, 0:D] = next(it)[...]
        slab[row0["meta"]:row0["meta"] + n_meta, 0:D] = next(it)[...]
        sent = []
        for k in range(1, N_DEV):
            to = (1 - x if k & 4 else x, 1 - y if k & 2 else y, 1 - c if k & 1 else c)
            cp = pltpu.make_async_remote_copy(
                src_ref=slab, dst_ref=land.at[k], send_sem=send_sems.at[k - 1], recv_sem=recv_sems.at[k - 1],
                device_id=to, device_id_type=MESH)
            cp.start()
            sent.append(cp)
        land[0] = slab[...]
        for cp in sent:
            cp.wait_recv()
        for cp in sent:
            cp.wait_send()
        total = land[me]
        for e in range(1, N_DEV):
            total = total + land[me ^ e]
        slab[...] = total

        def mine(r0, n_rows, width):
            got = slab[r0:r0 + n_rows, 0:width]
            for j in range(1, N_CHIPS):
                got = jnp.where(chip == j, slab[r0:r0 + n_rows, j * width:(j + 1) * width], got)
            return got

        def update(i, g, index=()):
            w_ref, m_ref, v_ref = st[3 * i:3 * i + 3]
            o = outs[4 * i:4 * i + 4]
            at = index if index else Ellipsis
            res = (g,) + _adam_math(w_ref[at], g, m_ref[at], v_ref[at])
            for o_ref, val in zip(o, res):
                o_ref[at] = val

        for i, k in enumerate(vec_names):
            width = st[3 * i].shape[1]
            update(i, slab[row0[k]:row0[k] + depth, 0:width])
        base = len(vec_names)
        update(base, slab[row0["final"]:row0["final"] + 1, 0:D])
        update(base + 1, mine(row0["meta"], n_meta, D // N_CHIPS))
        taps = st[3 * (base + 2)].shape[1]
        for l in range(depth):
            update(base + 2, mine(row0["dww"] + l * taps_pad, taps, C // N_CHIPS), (l,))

    flat_g = [g for k in vec_names for g in vec_grads[k]] + list(dww_grads) + [final_grad, meta_grad]
    flat_state = [a for k in names for a in state[k]]
    vmem = pl.BlockSpec(memory_space=pltpu.VMEM)
    out_shape = [jax.ShapeDtypeStruct(state[k][0].shape, F32) for k in names for _ in range(4)]
    res = pl.pallas_call(
        body, name="small_reduce_adam", out_shape=out_shape,
        in_specs=[vmem] * (len(flat_g) + len(flat_state)), out_specs=[vmem] * len(out_shape),
        scratch_shapes=[pltpu.VMEM((rows, lanes), F32), pltpu.VMEM((N_DEV, rows, lanes), F32),
                        pltpu.SemaphoreType.DMA((N_DEV - 1,)), pltpu.SemaphoreType.DMA((N_DEV - 1,))],
    )(*flat_g, *flat_state)
    return {k: tuple(res[4 * i:4 * i + 4]) for i, k in enumerate(names)}


def _adam(w, g, m, v, name):
    def body(w_ref, g_ref, m_ref, v_ref, d_ref, nm_ref, nv_ref):
        d_ref[...], nm_ref[...], nv_ref[...] = _adam_math(w_ref[...], g_ref[...], m_ref[...], v_ref[...])

    if w.ndim == 3:
        lyr, R, C = w.shape
        tr = _pick(R, (256, 352, 128))
        blk = pl.BlockSpec((None, tr, C), lambda l, r: (l, r, 0))
        grid, sem = (lyr, R // tr), ("parallel", "parallel")
    else:
        blk = pl.BlockSpec(w.shape, lambda: (0, 0))
        grid, sem = (), None
    return pl.pallas_call(
        body, name=name, grid=grid, in_specs=[blk] * 4, out_specs=[blk] * 3,
        out_shape=[jax.ShapeDtypeStruct(w.shape, F32)] * 3, compiler_params=_params(sem),
    )(w, g, m, v)


def _rows(a, pad_to=8):
    r = a.reshape(-1, LANES)
    extra = (-r.shape[0]) % pad_to
    return jnp.pad(r, ((0, extra), (0, 0))) if extra else r


def _pack(arrays):
    return jnp.concatenate([_rows(a) for a in arrays], axis=0)


def _unpack(slab, shapes):
    out, at = [], 0
    for shp in shapes:
        nrow = math.prod(shp) // LANES
        out.append(slab[at:at + nrow].reshape(shp))
        at += nrow + (-nrow) % 8
    return out


BIG = ("w_in", "w_out", "w_gate_t", "w_up_t", "w_down")
BIG_COL_SHARDED = (True, False, False, False, False)
TRANSPOSED = {"w_gate_t": "w_gate", "w_up_t": "w_up"}


def kernel(x, meta_tokens, mix_norm_g, w_in, conv_dw_w, conv_dw_b, conv_ln_g, conv_ln_b, w_out, ffn_norm_g, w_gate, w_up, w_down, final_norm_g, loss_target, m_meta_tokens, m_mix_norm_g, m_w_in, m_conv_dw_w, m_conv_dw_b, m_conv_ln_g, m_conv_ln_b, m_w_out, m_ffn_norm_g, m_w_gate, m_w_up, m_w_down, m_final_norm_g, v_meta_tokens, v_mix_norm_g, v_w_in, v_conv_dw_w, v_conv_dw_b, v_conv_ln_g, v_conv_ln_b, v_w_out, v_ffn_norm_g, v_w_gate, v_w_up, v_w_down, v_final_norm_g):
    n_meta, seq = meta_tokens.shape[0], x.shape[1]
    D = x.shape[2]
    depth, taps, c_shard = conv_dw_w.shape
    C = conv_dw_b.shape[-1]
    chip = (2 * lax.axis_index("x") + lax.axis_index("y")).astype(jnp.int32)
    core = lax.axis_index("c").astype(jnp.int32).reshape(1)
    chip1 = chip.reshape(1)
    where = jnp.concatenate([chip1, core])
    big_w = dict(w_in=w_in, w_out=w_out, w_gate=w_gate, w_up=w_up, w_down=w_down)
    big_m = dict(w_in=m_w_in, w_out=m_w_out, w_gate=m_w_gate, w_up=m_w_up, w_down=m_w_down)
    big_v = dict(w_in=v_w_in, w_out=v_w_out, w_gate=v_w_gate, w_up=v_w_up, w_down=v_w_down)

    small_shard = _pack([conv_dw_w, meta_tokens])[None]
    to_send = {k: jnp.swapaxes(big_w[TRANSPOSED[k]], 1, 2) if k in TRANSPOSED else big_w[k] for k in BIG}
    col = dict(zip(BIG, BIG_COL_SHARDED))
    wts = {k: [_place_shard(to_send[k], l, chip1, col[k], BF16, f"place_{k}_{l}") for l in range(depth)] for k in BIG}
    small_placed = _place_shard(small_shard, 0, chip1, False, F32, "place_small")

    def gather_job(keys, extra=()):
        arrays = [wts[k][l] for k, l in keys] + list(extra)
        shapes = [(1,) + to_send[k].shape[1:] for k, _ in keys] + [(1,) + small_shard.shape[1:]] * len(extra)
        return _gather_job(arrays, shapes, [col[k] for k, _ in keys] + [False] * len(extra))

    first_keys = [("w_in", 0)]
    *first, small_full = _run_job(gather_job(first_keys, [small_placed]), "gather_first")
    for (k, l), arr in zip(first_keys, first):
        wts[k][l] = arr
    behind_keys = {("attn", 0): [("w_out", 0), ("w_gate_t", 0), ("w_up_t", 0)], ("conv", 0): [("w_down", 0)]}
    for l in range(1, depth):
        behind_keys["ffn", l - 1] = [("w_in", l), ("w_out", l), ("w_gate_t", l)]
        behind_keys["down", l - 1] = [("w_up_t", l)]
        behind_keys["attn", l] = [("w_down", l)]
    gather_behind = {host: (gather_job(keys), keys) for host, keys in behind_keys.items()}

    rows_shard = small_shard.shape[1]
    dw_full, meta_full = [], []
    for j in range(N_CHIPS):
        dwj, mj = _unpack(small_full[0, j * rows_shard:(j + 1) * rows_shard],
                          [conv_dw_w.shape, meta_tokens.shape])
        dw_full.append(dwj)
        meta_full.append(mj)
    dw_w_full = jnp.concatenate(dw_full, axis=2)
    meta = jnp.concatenate(meta_full, axis=1)

    L = n_meta + seq
    Lp = -(-L // QUERY_BLOCK) * QUERY_BLOCK
    h0 = jnp.concatenate([meta, x[0], jnp.zeros((Lp - L, D), F32)], axis=0)
    target = jnp.pad(loss_target[0], ((n_meta, Lp - L), (0, 0)))
    reducer = _Reducer(core)
    loss, dh0, grads = _local_step(h0, target, n_meta, seq, (mix_norm_g, ffn_norm_g),
                                   (dw_w_full, conv_dw_b, conv_ln_g, conv_ln_b), wts, final_norm_g,
                                   gather_behind, reducer)
    loss = lax.psum(loss[0, 0], ("x", "y", "c"))
    grad_x = dh0[n_meta:L][None]

    reduced = []
    for k in BIG:
        arr = None
        for l in range(depth):
            arr = _sum_chips(reducer.parts[k, l], reducer.across[k, l], where, l, depth, arr, f"sum_chips_{k}_{l}")
        reduced.append(arr)
    big_g = dict(zip(BIG, _rs_join_halves(reduced)))

    out_g, out_d, out_m, out_v = {}, {}, {}, {}
    for kk in BIG:
        k = TRANSPOSED.get(kk, kk)
        view = (lambda a: jnp.swapaxes(a, 1, 2)) if kk in TRANSPOSED else (lambda a: a)
        res = _adam(view(big_w[k]), big_g[kk], view(big_m[k]), view(big_v[k]), f"adam_{k}")
        out_g[k] = view(big_g[kk])
        out_d[k], out_m[k], out_v[k] = (view(a) for a in res)

    as_row = lambda a: a.reshape(1, -1)
    state = dict(mix_g=(mix_norm_g, m_mix_norm_g, v_mix_norm_g), ffn_g=(ffn_norm_g, m_ffn_norm_g, v_ffn_norm_g),
                 dw_b=(conv_dw_b, m_conv_dw_b, v_conv_dw_b), ln_g=(conv_ln_g, m_conv_ln_g, v_conv_ln_g),
                 ln_b=(conv_ln_b, m_conv_ln_b, v_conv_ln_b),
                 final=(as_row(final_norm_g), as_row(m_final_norm_g), as_row(v_final_norm_g)),
                 meta=(meta_tokens, m_meta_tokens, v_meta_tokens), dww=(conv_dw_w, m_conv_dw_w, v_conv_dw_w))
    vec_names = ("mix_g", "ffn_g", "dw_b", "ln_g", "ln_b")
    small = _small_reduce_adam({k: grads[k] for k in vec_names}, grads["dw_w"], grads["final_g"], dh0[:n_meta], state)
    out_name = dict(mix_g="mix_norm_g", ffn_g="ffn_norm_g", dw_b="conv_dw_b", ln_g="conv_ln_g", ln_b="conv_ln_b",
                    final="final_norm_g", meta="meta_tokens", dww="conv_dw_w")
    for k, res in small.items():
        if k == "final":
            res = tuple(a.reshape(-1) for a in res)
        out_g[out_name[k]], out_d[out_name[k]], out_m[out_name[k]], out_v[out_name[k]] = res

    order = ("meta_tokens", "mix_norm_g", "w_in", "conv_dw_w", "conv_dw_b", "conv_ln_g", "conv_ln_b", "w_out",
             "ffn_norm_g", "w_gate", "w_up", "w_down", "final_norm_g")
    return (loss, grad_x, *[out_g[k] for k in order], *[out_d[k] for k in order],
            *[out_m[k] for k in order], *[out_v[k] for k in order])
```

```python
import functools
import math

import jax
import jax.numpy as jnp
from jax import lax
from jax.experimental import pallas as pl
from jax.experimental.pallas import tpu as pltpu

F32 = jnp.float32
BF16 = jnp.bfloat16
MESH = pl.DeviceIdType.MESH

EPS = 1e-6
QUERY_BLOCK = 128
LANES = 128
HEAD_DIM = 64
LOG_STICK_FLOOR = -40.0
CONV_HALO = 32
N_CHIPS = 4
N_DEV = 8
VMEM_LIMIT = 56 * 1024 * 1024

ADAM_LR = 0.001
ADAM_B1 = 0.9
ADAM_B2 = 0.999
ADAM_EPS = 1e-08
ADAM_WD = 0.01
ADAM_STEP = 10


def _pick(n, prefs):
    for p in prefs:
        if n % p == 0:
            return p
    return n


def _params(sem=None):
    return pltpu.CompilerParams(dimension_semantics=sem, vmem_limit_bytes=VMEM_LIMIT)


def _sigmoid(x):
    return 1.0 / (1.0 + jnp.exp(-x))


def _rmsnorm_bwd(h, g, dy, dh_in, name, job=None):
    L, D = h.shape
    T = _pick(L, (384, 128))

    def body(h_ref, g_ref, dy_ref, dhin_ref, dh_ref, dg_ref):
        x = h_ref[...]
        dyv = dy_ref[...]
        r = lax.rsqrt(jnp.mean(x * x, axis=-1, keepdims=True) + EPS)
        xh = x * r
        dxh = dyv * g_ref[...]
        dh_ref[...] = dhin_ref[...] + r * (dxh - xh * jnp.mean(dxh * xh, axis=-1, keepdims=True))

        @pl.when(pl.program_id(0) == 0)
        def _():
            dg_ref[...] = jnp.zeros_like(dg_ref)

        dg_ref[...] += jnp.sum(dyv * xh, axis=0, keepdims=True)

    row = pl.BlockSpec((T, D), lambda i: (i, 0))
    vec = pl.BlockSpec((1, D), lambda i: (0, 0))
    (dh, dg), arrived = _call_with_job(
        body, job, name=name, grid=(L // T,),
        in_specs=[row, vec, row, row], out_specs=[row, vec],
        out_shape=[jax.ShapeDtypeStruct((L, D), F32), jax.ShapeDtypeStruct((1, D), F32)],
        scratch_shapes=[], args=(h, g, dy, dh_in))
    return (dh, dg) if job is None else (dh, dg, arrived)


def _loss_head(h, g, target, n_meta, seq, name):
    L, D = h.shape
    T = _pick(L, (384, 128))

    def body(h_ref, g_ref, t_ref, loss_ref, dh_ref, dg_ref):
        i = pl.program_id(0)
        x = h_ref[...]
        gv = g_ref[...]
        r = lax.rsqrt(jnp.mean(x * x, axis=-1, keepdims=True) + EPS)
        xh = x * r
        y = xh * gv
        rows = i * T + lax.broadcasted_iota(jnp.int32, (T, 1), 0)
        live = (rows >= n_meta) & (rows < n_meta + seq)
        diff = jnp.where(live, y - t_ref[...], 0.0)
        dyv = diff / D
        dxh = dyv * gv
        dh_ref[...] = r * (dxh - xh * jnp.mean(dxh * xh, axis=-1, keepdims=True))

        @pl.when(i == 0)
        def _():
            dg_ref[...] = jnp.zeros_like(dg_ref)
            loss_ref[...] = jnp.zeros_like(loss_ref)

        dg_ref[...] += jnp.sum(dyv * xh, axis=0, keepdims=True)
        per_row = jnp.mean(diff * diff, axis=-1, keepdims=True)
        loss_ref[...] += 0.5 * jnp.sum(per_row, axis=0, keepdims=True)

    row = pl.BlockSpec((T, D), lambda i: (i, 0))
    vec = pl.BlockSpec((1, D), lambda i: (0, 0))
    one = pl.BlockSpec((1, 1), lambda i: (0, 0))
    return pl.pallas_call(
        body, name=name, grid=(L // T,),
        in_specs=[row, vec, row], out_specs=[one, row, vec],
        out_shape=[jax.ShapeDtypeStruct((1, 1), F32), jax.ShapeDtypeStruct((L, D), F32),
                   jax.ShapeDtypeStruct((1, D), F32)],
        compiler_params=_params(("arbitrary",)),
    )(h, g, target)


def _ffn_tiles(M, F):
    return _pick(M, (352, 384, 128)), _pick(F, (1408, 512, 256, 128))


def _resident(shape):
    return pl.BlockSpec((None,) + tuple(shape[1:]), lambda *_: (0,) * len(shape), pipeline_mode=pl.Buffered(1))


def _normed_rows(h_ref, g_ref, hn_ref, keep_ref):
    @pl.when(pl.program_id(1) == 0)
    def _():
        x = h_ref[...]
        r = lax.rsqrt(jnp.mean(x * x, axis=-1, keepdims=True) + EPS)
        keep_ref[...] = (x * r * g_ref[...]).astype(BF16)
        hn_ref[...] = keep_ref[...]


def _norm_in_proj(h, g, w_in, n_cols, name):
    M, D = h.shape
    tm = _pick(M, (1056, 384, 128))
    tn = _pick(n_cols, (512, 256, 128))

    def body(h_ref, g_ref, w_ref, hn_ref, o_ref, keep_ref):
        _normed_rows(h_ref, g_ref, hn_ref, keep_ref)
        o_ref[...] = jnp.dot(keep_ref[...], w_ref[...], preferred_element_type=F32).astype(o_ref.dtype)

    rows = pl.BlockSpec((tm, D), lambda i, j: (i, 0))
    return pl.pallas_call(
        body, name=name, grid=(M // tm, n_cols // tn),
        in_specs=[rows, pl.BlockSpec((1, D), lambda i, j: (0, 0)), pl.BlockSpec((None, D, tn), lambda i, j: (0, 0, j))],
        out_specs=[rows, pl.BlockSpec((tm, tn), lambda i, j: (i, j))],
        out_shape=[jax.ShapeDtypeStruct((M, D), BF16), jax.ShapeDtypeStruct((M, n_cols), BF16)],
        scratch_shapes=[pltpu.VMEM((tm, D), BF16)],
        compiler_params=_params(("parallel", "arbitrary")),
    )(h, g, w_in)


def _ffn_up(h, norm_g, w_gate_t, w_up_t, name, job=None):
    M, D = h.shape
    F = w_gate_t.shape[1]
    tm, tf = _ffn_tiles(M, F)
    nt = (((1,), (1,)), ((), ()))

    def body(h_ref, ng_ref, wg_ref, wu_ref, hn_ref, g_ref, u_ref, a_ref):
        x = h_ref[...]
        r = lax.rsqrt(jnp.mean(x * x, axis=-1, keepdims=True) + EPS)
        hv = (x * r * ng_ref[...]).astype(BF16)
        hn_ref[...] = hv
        for c in range(0, F, tf):
            gv = lax.dot_general(hv, wg_ref[c:c + tf, :], nt, preferred_element_type=F32)
            uv = lax.dot_general(hv, wu_ref[c:c + tf, :], nt, preferred_element_type=F32)
            g_ref[:, c:c + tf] = gv.astype(g_ref.dtype)
            u_ref[:, c:c + tf] = uv.astype(u_ref.dtype)
            a_ref[:, c:c + tf] = (gv * _sigmoid(gv) * uv).astype(a_ref.dtype)

    rows = pl.BlockSpec((tm, D), lambda i: (i, 0))
    wide = pl.BlockSpec((tm, F), lambda i: (i, 0))
    return _call_with_job(
        body, job, name=name, grid=(M // tm,),
        in_specs=[rows, pl.BlockSpec((1, D), lambda i: (0, 0)), _resident(w_gate_t.shape), _resident(w_up_t.shape)],
        out_specs=[rows, wide, wide, wide],
        out_shape=[jax.ShapeDtypeStruct((M, D), BF16), jax.ShapeDtypeStruct((M, F), BF16),
                   jax.ShapeDtypeStruct((M, F), BF16), jax.ShapeDtypeStruct((M, F), BF16)],
        scratch_shapes=[], args=(h, norm_g, w_gate_t, w_up_t))


def _ffn_down_bwd(dh, w_down, g, u, name):
    M, D = dh.shape
    F = g.shape[1]
    tm, tf = _ffn_tiles(M, F)

    def body(d_ref, w_ref, g_ref, u_ref, dg_ref, du_ref):
        dhv = d_ref[...].astype(BF16)
        for c in range(0, F, tf):
            dv = lax.dot_general(dhv, w_ref[c:c + tf, :], (((1,), (1,)), ((), ())), preferred_element_type=F32)
            gv = g_ref[:, c:c + tf].astype(F32)
            s = _sigmoid(gv)
            du_ref[:, c:c + tf] = (dv * (gv * s)).astype(du_ref.dtype)
            dg_ref[:, c:c + tf] = (dv * u_ref[:, c:c + tf].astype(F32)
                                   * (s * (1.0 + gv * (1.0 - s)))).astype(dg_ref.dtype)

    wide = pl.BlockSpec((tm, F), lambda i: (i, 0))
    return pl.pallas_call(
        body, name=name, grid=(M // tm,),
        in_specs=[pl.BlockSpec((tm, D), lambda i: (i, 0)), _resident(w_down.shape), wide, wide],
        out_specs=[wide, wide],
        out_shape=[jax.ShapeDtypeStruct((M, F), BF16), jax.ShapeDtypeStruct((M, F), BF16)],
        compiler_params=_params(("parallel",)),
    )(dh, w_down, g, u)


def _mm_nn(pairs, out_dtype, name, residual=None, cols=None, job=None):
    M = pairs[0][0].shape[0]
    col0, N = cols if cols is not None else (0, pairs[0][1].shape[-1])
    tm = _pick(M, (1056, 384, 128))
    shallow = sum(p[0].shape[1] for p in pairs) <= 1024
    tn = _pick(math.gcd(N, col0) if col0 else N, ((1024,) if shallow else ()) + (640, 512, 256, 128))
    jb = col0 // tn
    n = len(pairs)

    def body(*refs):
        a_refs, w_refs = refs[:n], refs[n:2 * n]
        o_ref = refs[-1]
        acc = None
        for a_ref, w_ref in zip(a_refs, w_refs):
            d = jnp.dot(a_ref[...].astype(BF16), w_ref[...], preferred_element_type=F32)
            acc = d if acc is None else acc + d
        if residual is not None:
            acc = acc + refs[2 * n][...]
        o_ref[...] = acc.astype(o_ref.dtype)

    in_specs = [pl.BlockSpec((tm, a.shape[1]), lambda i, j: (i, 0)) for a, _, _, _ in pairs]
    for a, _, layer, kblk in pairs:
        in_specs.append(pl.BlockSpec((None, a.shape[1], tn), functools.partial(lambda i, j, l, kb: (l, kb, j + jb), l=layer, kb=kblk)))
    args = [p[0] for p in pairs] + [p[1] for p in pairs]
    if residual is not None:
        in_specs.append(pl.BlockSpec((tm, tn), lambda i, j: (i, j)))
        args.append(residual)
    (out,), arrived = _call_with_job(
        body, job, name=name, grid=(M // tm, N // tn), in_specs=in_specs,
        out_specs=[pl.BlockSpec((tm, tn), lambda i, j: (i, j))],
        out_shape=[jax.ShapeDtypeStruct((M, N), out_dtype)], scratch_shapes=[], args=args)
    return out if job is None else (out, arrived)


def _mm_nt(pairs, out_dtype, name):
    M = pairs[0][0].shape[0]
    K = pairs[0][1].shape[1]
    tm = _pick(M, (1056, 384, 128))
    shallow = sum(p[0].shape[1] for p in pairs) <= 1024
    tk = _pick(K, ((1024,) if shallow else ()) + (512, 1408, 256, 128))
    n = len(pairs)

    def body(*refs):
        d_refs, w_refs = refs[:n], refs[n:2 * n]
        o_ref = refs[-1]
        acc = None
        for d_ref, w_ref in zip(d_refs, w_refs):
            d = lax.dot_general(d_ref[...].astype(BF16), w_ref[...], (((1,), (1,)), ((), ())),
                                preferred_element_type=F32)
            acc = d if acc is None else acc + d
        o_ref[...] = acc.astype(o_ref.dtype)

    in_specs = [pl.BlockSpec((tm, d.shape[1]), lambda i, j: (i, 0)) for d, _, _, _ in pairs]
    for d, _, layer, cblk in pairs:
        in_specs.append(pl.BlockSpec((None, tk, d.shape[1]), functools.partial(lambda i, j, l, cb: (l, j, cb), l=layer, cb=cblk)))
    args = [p[0] for p in pairs] + [p[1] for p in pairs]
    return pl.pallas_call(
        body, name=name, grid=(M // tm, K // tk), in_specs=in_specs,
        out_specs=pl.BlockSpec((tm, tk), lambda i, j: (i, j)),
        out_shape=jax.ShapeDtypeStruct((M, K), out_dtype),
        compiler_params=_params(("parallel", "parallel")),
    )(*args)


def _mm_tn(a, b, name, col_sharded, chips=N_CHIPS):
    M, K = a.shape
    N = b.shape[1]
    tm = _pick(M, (1056, 384, 128))
    tk = _pick(K, (1024, 1408, 512, 256, 128))
    tn = N // N_CHIPS if col_sharded else _pick(N, (512, 128))

    def body(a_ref, b_ref, o_ref):
        @pl.when(pl.program_id(2) == 0)
        def _():
            o_ref[...] = jnp.zeros_like(o_ref)

        o_ref[...] += lax.dot_general(a_ref[...].astype(BF16), b_ref[...].astype(BF16),
                                      (((0,), (0,)), ((), ())), preferred_element_type=F32)

    if col_sharded:
        out_shape = jax.ShapeDtypeStruct((N_CHIPS, K, tn), F32)
        out_spec = pl.BlockSpec((None, tk, tn), lambda k, j, m: (j, k, 0))
    else:
        out_shape = jax.ShapeDtypeStruct((K, N), F32)
        out_spec = pl.BlockSpec((tk, tn), lambda k, j, m: (k, j))
    out = pl.pallas_call(
        body, name=name, grid=(K // tk, N // tn, M // tm),
        in_specs=[pl.BlockSpec((tm, tk), lambda k, j, m: (m, k)), pl.BlockSpec((tm, tn), lambda k, j, m: (m, j))],
        out_specs=out_spec, out_shape=out_shape,
        compiler_params=_params(("parallel", "parallel", "arbitrary")),
    )(a, b)
    return out if col_sharded else out.reshape(chips, K // chips, N)


def _stack_heads(x, scale=None):
    lane = lax.broadcasted_iota(jnp.int32, x.shape, 1)
    zero = jnp.zeros_like(x)
    lo = jnp.where(lane < HEAD_DIM, x, zero)
    hi = jnp.where(lane < HEAD_DIM, zero, x)
    out = jnp.concatenate([lo, hi], axis=0)
    return out if scale is None else out * scale


def _unstack_heads(x2):
    qb = x2.shape[0] // 2
    lane = lax.broadcasted_iota(jnp.int32, (qb, LANES), 1)
    return jnp.where(lane < HEAD_DIM, x2[:qb], x2[qb:])


def _dot2(x, m):
    xh = x.astype(BF16)
    xl = (x - xh.astype(F32)).astype(BF16)
    return jnp.dot(xh, m, preferred_element_type=F32) + jnp.dot(xl, m, preferred_element_type=F32)


ATTN_CHUNK = 32


def _in_lockstep(staged):
    waiting, live = list(staged), []
    while waiting or live:
        if waiting:
            live.append(waiting.pop(0))
        for gen in list(live):
            if next(gen, StopIteration) is StopIteration:
                live.remove(gen)


def _row_chunks():
    return [slice(r, r + ATTN_CHUNK) for r in range(0, 2 * QUERY_BLOCK, ATTN_CHUNK)]


def _chunk_valid(rows, older):
    shape = (rows.stop - rows.start, older + QUERY_BLOCK)
    r = (rows.start + lax.broadcasted_iota(jnp.int32, shape, 0)) & (QUERY_BLOCK - 1)
    return lax.broadcasted_iota(jnp.int32, shape, 1) < r + older


def _split_to(x, hi_ref, lo_ref, rows):
    xh = x.astype(BF16)
    hi_ref[rows, :] = xh
    lo_ref[rows, :] = (x - xh.astype(F32)).astype(BF16)


def _triangle(keys, strict):
    r = lax.broadcasted_iota(jnp.int32, (keys, keys), 0)
    c = lax.broadcasted_iota(jnp.int32, (keys, keys), 1)
    return jnp.where((r > c) if strict else (r >= c), 1.0, 0.0).astype(BF16)


def _fill_attn_consts(tri_ref, mask_ref):
    n = 2 * QUERY_BLOCK
    tri_ref[0] = _triangle(n, True)
    tri_ref[1] = _triangle(n, False)
    mask_ref[...] = jnp.where(_chunk_valid(slice(0, n), QUERY_BLOCK), 1.0, 0.0)


def _valid(mask_ref, rows, older):
    cols = slice(0, 2 * QUERY_BLOCK) if older else slice(QUERY_BLOCK, 2 * QUERY_BLOCK)
    return mask_ref[rows, cols] > 0.5


def _row_total(first):
    lane = lax.broadcasted_iota(jnp.int32, first.shape, 1)
    total = jnp.sum(jnp.where(lane == 0, first, 0.0), axis=1, keepdims=True)
    return jnp.broadcast_to(total, first.shape)


def _pairs_per_step(n_pairs, most):
    return max(g for g in (1, 2, 4) if g <= most and n_pairs % g == 0)


def _lanes(g):
    return slice(g * LANES, (g + 1) * LANES)


def _sweep_older(i, step, carry_ref, first):
    def cond(state):
        n, live = state
        return jnp.logical_and(n < i, live)

    def older(state):
        n, _ = state
        step(i - 1 - n, False)
        return n + 1, jnp.max(carry_ref[...]) > LOG_STICK_FLOOR

    lax.while_loop(cond, older, (first, jnp.max(carry_ref[...]) > LOG_STICK_FLOOR))


class _CommJob:
    def __init__(self, inputs, out_shapes, aliases, n_sems, start, finish):
        self.inputs, self.out_shapes, self.aliases, self.n_sems = list(inputs), list(out_shapes), aliases, n_sems
        self.start, self.finish = start, finish


def _merge_jobs(jobs):
    jobs = [j for j in jobs if j is not None]
    if len(jobs) <= 1:
        return jobs[0] if jobs else None
    spans, aliases = [], {}
    i0 = o0 = s0 = 0
    for j in jobs:
        spans.append((i0, o0, s0))
        aliases.update({i0 + a: o0 + b for a, b in j.aliases.items()})
        i0, o0, s0 = i0 + len(j.inputs), o0 + len(j.out_shapes), s0 + j.n_sems

    def run(which):
        def go(ins, outs, send_sems, recv_sems):
            for j, (i, o, s) in zip(jobs, spans):
                getattr(j, which)(ins[i:i + len(j.inputs)], outs[o:o + len(j.out_shapes)],
                                  send_sems.at[pl.ds(s, j.n_sems)], recv_sems.at[pl.ds(s, j.n_sems)])
        return go

    return _CommJob([a for j in jobs for a in j.inputs], [s for j in jobs for s in j.out_shapes], aliases, s0,
                    run("start"), run("finish"))


def _call_with_job(core_body, job, *, name, grid, in_specs, out_specs, out_shape, scratch_shapes, args):
    sem = ("arbitrary",) * len(grid)
    if job is None:
        res = pl.pallas_call(core_body, name=name, grid=grid, in_specs=in_specs, out_specs=out_specs,
                             out_shape=out_shape, scratch_shapes=scratch_shapes, compiler_params=_params(sem))(*args)
        return list(res), []
    n_in, n_out, n_scr = len(in_specs), len(out_specs), len(scratch_shapes)
    m_in, m_out = len(job.inputs), len(job.out_shapes)

    def body(*refs):
        at = 0
        parts = []
        for count in (n_in, m_in, n_out, m_out, n_scr, 2):
            parts.append(refs[at:at + count])
            at += count
        ins, job_in, outs, job_outs, scratch, (send_sems, recv_sems) = parts
        first = functools.reduce(jnp.logical_and, [pl.program_id(a) == 0 for a in range(len(grid))])
        last = functools.reduce(jnp.logical_and, [pl.program_id(a) == grid[a] - 1 for a in range(len(grid))])

        @pl.when(first)
        def _():
            job.start(job_in, job_outs, send_sems, recv_sems)

        core_body(*ins, *outs, *scratch)

        @pl.when(last)
        def _():
            job.finish(job_in, job_outs, send_sems, recv_sems)

    res = pl.pallas_call(
        body, name=name, grid=grid, in_specs=list(in_specs) + [ANY] * m_in, out_specs=list(out_specs) + [ANY] * m_out,
        out_shape=list(out_shape) + job.out_shapes,
        input_output_aliases={n_in + a: n_out + b for a, b in job.aliases.items()},
        scratch_shapes=list(scratch_shapes) + [pltpu.SemaphoreType.DMA((job.n_sems,)), pltpu.SemaphoreType.DMA((job.n_sems,))],
        compiler_params=_params(sem),
    )(*args, *job.inputs)
    return list(res[:n_out]), list(res[n_out:])


def _run_job(job, name):
    m_in, m_out = len(job.inputs), len(job.out_shapes)

    def body(*refs):
        job_in, job_outs = refs[:m_in], refs[m_in:m_in + m_out]
        send_sems, recv_sems = refs[m_in + m_out:]
        job.start(job_in, job_outs, send_sems, recv_sems)
        job.finish(job_in, job_outs, send_sems, recv_sems)

    return list(pl.pallas_call(
        body, name=name, in_specs=[ANY] * m_in, out_specs=[ANY] * m_out, out_shape=job.out_shapes,
        input_output_aliases=dict(job.aliases),
        scratch_shapes=[pltpu.SemaphoreType.DMA((job.n_sems,)), pltpu.SemaphoreType.DMA((job.n_sems,))],
    )(*job.inputs))


def _attn_fwd(qkv, sb_width, name, job=None):
    L = qkv.shape[0]
    QB = QUERY_BLOCK
    nb = L // QB
    n_pairs = sb_width // LANES
    G = _pairs_per_step(n_pairs, 4)
    W = G * LANES
    scale = 1.0 / math.sqrt(HEAD_DIM)

    def body(q_ref, k_ref, v_ref, o_ref, acc_ref, carry_ref, f32_ref, bf16_ref, tri_ref, mask_ref):
        i = pl.program_id(1)

        @pl.when(i == 0)
        def _():
            _fill_attn_consts(tri_ref, mask_ref)

        q2 = [_stack_heads(q_ref[:, _lanes(g)], scale) for g in range(G)]
        acc_ref[...] = jnp.zeros_like(acc_ref)
        carry_ref[...] = jnp.zeros_like(carry_ref)

        def step(g, j, older, masked):
            n = older + QB
            start = pl.multiple_of(j * QB, QB)
            kb = k_ref[pl.ds(start, n), _lanes(g)]
            vb = v_ref[pl.ds(start, n), _lanes(g)]
            zs, as_, bs = (f32_ref.at[g, t, :, :n] for t in range(3))
            hi, lo = (bf16_ref.at[g, t, :, :n] for t in range(2))
            zs[...] = lax.dot_general(q2[g], kb, (((1,), (1,)), ((), ())), preferred_element_type=F32)
            yield
            for rows in _row_chunks():
                z = zs[rows, :]
                sp = jnp.log(1.0 + jnp.exp(-jnp.abs(z)))
                b = jnp.minimum(-z, 0.0) - sp
                if masked:
                    b = jnp.where(_valid(mask_ref, rows, older),b, 0.0)
                as_[rows, :] = jnp.minimum(z, 0.0) - sp
                bs[rows, :] = b
                _split_to(b, hi, lo, rows)
            yield
            tri = tri_ref[0, :n, :n]
            zs[...] = (jnp.dot(hi[...], tri, preferred_element_type=F32)
                       + jnp.dot(lo[...], tri, preferred_element_type=F32))
            yield
            for rows in _row_chunks():
                excl = zs[rows, :]
                total = _row_total(excl[:, :LANES] + bs[rows, :LANES])
                if not masked:
                    excl = excl + jnp.tile(carry_ref[g, rows, :], (1, n // LANES))
                carry_ref[g, rows, :] += total
                w = jnp.exp(as_[rows, :] + excl)
                if masked:
                    w = jnp.where(_valid(mask_ref, rows, older),w, 0.0)
                _split_to(w, hi, lo, rows)
            yield
            acc_ref[g] += (jnp.dot(hi[...], vb, preferred_element_type=F32)
                           + jnp.dot(lo[...], vb, preferred_element_type=F32))

        @pl.when(i == 0)
        def _():
            _in_lockstep([step(g, 0, 0, True) for g in range(G)])

        @pl.when(i > 0)
        def _():
            _in_lockstep([step(g, i - 1, QB, True) for g in range(G)])
            for g in range(G):
                _sweep_older(i, lambda j, _, g=g: _in_lockstep([step(g, j, 0, False)]), carry_ref.at[g], 1)

        for g in range(G):
            o_ref[:, _lanes(g)] = _unstack_heads(acc_ref[g])

    n_steps = n_pairs // G
    (out,), job_out = _call_with_job(
        body, job, name=name, grid=(n_steps, nb),
        in_specs=[pl.BlockSpec((QB, W), lambda p, i: (i, p)),
                  pl.BlockSpec((L, W), lambda p, i: (0, n_steps + p)),
                  pl.BlockSpec((L, W), lambda p, i: (0, 2 * n_steps + p))],
        out_specs=[pl.BlockSpec((QB, W), lambda p, i: (i, p))],
        out_shape=[jax.ShapeDtypeStruct((L, sb_width), F32)],
        scratch_shapes=[pltpu.VMEM((G, 2 * QB, LANES), F32), pltpu.VMEM((G, 2 * QB, LANES), F32),
                        pltpu.VMEM((G, 3, 2 * QB, 2 * QB), F32), pltpu.VMEM((G, 2, 2 * QB, 2 * QB), BF16),
                        pltpu.VMEM((2, 2 * QB, 2 * QB), BF16), pltpu.VMEM((2 * QB, 2 * QB), F32)],
        args=(qkv, qkv, qkv))
    return out, job_out


def _attn_bwd(qkv, o, dmixed, sb_width, name, job=None):
    L = qkv.shape[0]
    QB = QUERY_BLOCK
    nb = L // QB
    n_pairs = sb_width // LANES
    G = _pairs_per_step(n_pairs, 2)
    W = G * LANES
    scale = 1.0 / math.sqrt(HEAD_DIM)

    def body(q_ref, k_ref, v_ref, o_ref, do_ref, dq_ref, dk_ref, dv_ref,
             dq_acc, dk_acc, dv_acc, ce_ref, cr_ref, dtot_ref, f32_ref, bf16_ref, tri_ref, mask_ref):
        i = pl.program_id(1)

        @pl.when(i == 0)
        def _():
            dk_acc[...] = jnp.zeros_like(dk_acc)
            dv_acc[...] = jnp.zeros_like(dv_acc)
            _fill_attn_consts(tri_ref, mask_ref)

        q2s = [_stack_heads(q_ref[:, _lanes(g)], scale) for g in range(G)]
        do2s = [_stack_heads(do_ref[:, _lanes(g)].astype(BF16)) for g in range(G)]
        ones = jnp.ones((LANES, LANES), BF16)
        for g in range(G):
            ov = o_ref[:, _lanes(g)]
            dtot_ref[g] = _dot2(do2s[g].astype(F32) * jnp.concatenate([ov, ov], axis=0), ones)
        dq_acc[...] = jnp.zeros_like(dq_acc)
        ce_ref[...] = jnp.zeros_like(ce_ref)
        cr_ref[...] = jnp.zeros_like(cr_ref)

        def step(g, j, older, masked):
            n = older + QB
            wide = n // LANES
            q2, do2 = q2s[g], do2s[g]
            start = pl.multiple_of(j * QB, QB)
            kb = k_ref[pl.ds(start, n), _lanes(g)]
            vb = v_ref[pl.ds(start, n), _lanes(g)]
            zs, as_, bs, betas, gs = (f32_ref.at[g, t, :, :n] for t in range(5))
            hi, lo, wb = (bf16_ref.at[g, t, :, :n] for t in range(3))
            nt = (((1,), (1,)), ((), ()))
            zs[...] = lax.dot_general(q2, kb, nt, preferred_element_type=F32)
            gs[...] = lax.dot_general(do2, vb, nt, preferred_element_type=F32)
            yield
            for rows in _row_chunks():
                z = zs[rows, :]
                e = jnp.exp(-jnp.abs(z))
                sp = jnp.log(1.0 + e)
                b = jnp.minimum(-z, 0.0) - sp
                if masked:
                    b = jnp.where(_valid(mask_ref, rows, older),b, 0.0)
                rinv = 1.0 / (1.0 + e)
                as_[rows, :] = jnp.minimum(z, 0.0) - sp
                bs[rows, :] = b
                betas[rows, :] = jnp.where(z >= 0.0, rinv, e * rinv)
                _split_to(b, hi, lo, rows)
            yield
            tri = tri_ref[0, :n, :n]
            zs[...] = (jnp.dot(hi[...], tri, preferred_element_type=F32)
                       + jnp.dot(lo[...], tri, preferred_element_type=F32))
            yield
            for rows in _row_chunks():
                excl = zs[rows, :]
                total = _row_total(excl[:, :LANES] + bs[rows, :LANES])
                if not masked:
                    excl = excl + jnp.tile(ce_ref[g, rows, :], (1, wide))
                ce_ref[g, rows, :] += total
                w = jnp.exp(as_[rows, :] + excl)
                if masked:
                    w = jnp.where(_valid(mask_ref, rows, older),w, 0.0)
                wb[rows, :] = w.astype(BF16)
                gw = w * gs[rows, :]
                gs[rows, :] = gw
                _split_to(gw, hi, lo, rows)
            yield
            tri = tri_ref[1, :n, :n]
            zs[...] = (jnp.dot(hi[...], tri, preferred_element_type=F32)
                       + jnp.dot(lo[...], tri, preferred_element_type=F32))
            yield
            for rows in _row_chunks():
                rinc = zs[rows, :]
                total = _row_total(rinc[:, :LANES])
                if not masked:
                    rinc = rinc + jnp.tile(cr_ref[g, rows, :], (1, wide))
                cr_ref[g, rows, :] += total
                beta = betas[rows, :]
                dz = gs[rows, :] * (1.0 - beta) - beta * (jnp.tile(dtot_ref[g, rows, :], (1, wide)) - rinc)
                if masked:
                    dz = jnp.where(_valid(mask_ref, rows, older),dz, 0.0)
                hi[rows, :] = dz.astype(BF16)
            yield
            dzb = hi[...]
            dq_acc[g] += jnp.dot(dzb, kb, preferred_element_type=F32)
            dk_acc[pl.ds(start, n), _lanes(g)] += lax.dot_general(
                dzb, q2, (((0,), (0,)), ((), ())), preferred_element_type=F32)
            dv_acc[pl.ds(start, n), _lanes(g)] += lax.dot_general(
                wb[...], do2, (((0,), (0,)), ((), ())), preferred_element_type=F32)

        @pl.when(i == 0)
        def _():
            _in_lockstep([step(g, 0, 0, True) for g in range(G)])

        @pl.when(i > 0)
        def _():
            _in_lockstep([step(g, i - 1, QB, True) for g in range(G)])
            for g in range(G):
                _sweep_older(i, lambda j, _, g=g: _in_lockstep([step(g, j, 0, False)]), ce_ref.at[g], 1)

        for g in range(G):
            dq_ref[:, _lanes(g)] = (_unstack_heads(dq_acc[g]) * scale).astype(dq_ref.dtype)

        @pl.when(i == nb - 1)
        def _():
            dk_ref[...] = dk_acc[...].astype(dk_ref.dtype)
            dv_ref[...] = dv_acc[...].astype(dv_ref.dtype)

    n_steps = n_pairs // G
    blk = pl.BlockSpec((QB, W), lambda p, i: (i, p))
    col = pl.BlockSpec((L, W), lambda p, i: (0, p))
    return _call_with_job(
        body, job, name=name, grid=(n_steps, nb),
        in_specs=[blk,
                  pl.BlockSpec((L, W), lambda p, i: (0, n_steps + p)),
                  pl.BlockSpec((L, W), lambda p, i: (0, 2 * n_steps + p)),
                  blk, blk],
        out_specs=[blk, col, col],
        out_shape=[jax.ShapeDtypeStruct((L, sb_width), BF16)] * 3,
        scratch_shapes=[pltpu.VMEM((G, 2 * QB, LANES), F32), pltpu.VMEM((L, W), F32),
                        pltpu.VMEM((L, W), F32), pltpu.VMEM((G, 2 * QB, LANES), F32),
                        pltpu.VMEM((G, 2 * QB, LANES), F32), pltpu.VMEM((G, 2 * QB, LANES), F32),
                        pltpu.VMEM((G, 5, 2 * QB, 2 * QB), F32), pltpu.VMEM((G, 3, 2 * QB, 2 * QB), BF16),
                        pltpu.VMEM((2, 2 * QB, 2 * QB), BF16), pltpu.VMEM((2 * QB, 2 * QB), F32)],
        args=(qkv, qkv, qkv, o, dmixed))


def _conv_tile(L):
    return _pick(L, (384, 128))


def _glu(x, C):
    return x[:, :C] * _sigmoid(x[:, C:])


CONV_CHUNK = 32
SHIFT_TAIL = 24


def _fill_shifted(src_ref, dst_ref):
    n = dst_ref.shape[1]
    for r in range(1, 8):
        dst_ref[r - 1] = src_ref[r:r + n, :]


def _rows_at(src_ref, shifted_ref, start, n):
    q, r = divmod(start, 8)
    if r == 0:
        return src_ref[start:start + n, :]
    return shifted_ref[r - 1, 8 * q:8 * q + n, :]


def _conv_fwd(cacg, dw_w, dw_b, ln_g, ln_b, name, job=None):
    L, C2 = cacg.shape
    C = C2 // 2
    T = _conv_tile(L)
    H = CONV_HALO
    K = dw_w.shape[0]
    CH = CONV_CHUNK

    def body(x_ref, prev_ref, w_ref, b_ref, g_ref, beta_ref, o_ref, y_ref, u_ref, us_ref):
        i = pl.program_id(0)
        u_ref[0:H, :] = jnp.where(i > 0, _glu(prev_ref[...], C), 0.0)
        u_ref[H:, :] = _glu(x_ref[...], C)
        _fill_shifted(u_ref, us_ref)
        for c0 in range(0, T, CH):
            y = jnp.broadcast_to(b_ref[...], (CH, C))
            for k in range(K):
                y = y + w_ref[k:k + 1, :] * _rows_at(u_ref, us_ref, c0 + H - (K - 1) + k, CH)
            y_ref[c0:c0 + CH, :] = y
            mu = jnp.mean(y, axis=-1, keepdims=True)
            yc = y - mu
            rstd = lax.rsqrt(jnp.mean(yc * yc, axis=-1, keepdims=True) + EPS)
            ln = yc * rstd * g_ref[...] + beta_ref[...]
            o_ref[c0:c0 + CH, :] = (ln * _sigmoid(ln)).astype(o_ref.dtype)

    vec = pl.BlockSpec((1, C), lambda i: (0, 0))
    tile = pl.BlockSpec((T, C), lambda i: (i, 0))
    (out, y), arrived = _call_with_job(
        body, job, name=name, grid=(L // T,),
        in_specs=[pl.BlockSpec((T, C2), lambda i: (i, 0)),
                  pl.BlockSpec((H, C2), lambda i: (jnp.maximum(i * (T // H) - 1, 0), 0)),
                  pl.BlockSpec((K, C), lambda i: (0, 0)), vec, vec, vec],
        out_specs=[tile, tile],
        out_shape=[jax.ShapeDtypeStruct((L, C), BF16), jax.ShapeDtypeStruct((L, C), F32)],
        scratch_shapes=[pltpu.VMEM((T + H, C), F32), pltpu.VMEM((7, T + SHIFT_TAIL, C), F32)],
        args=(cacg, cacg, dw_w, dw_b, ln_g, ln_b))
    return out, y, arrived


def _conv_bwd(cacg, y, dmixed, dw_w, ln_g, ln_b, name, job=None):
    L, C2 = cacg.shape
    C = C2 // 2
    T = _conv_tile(L)
    H = CONV_HALO
    K = dw_w.shape[0]
    nt = L // T
    TE = T + H

    CH = CONV_CHUNK

    def body(x_ref, prev_ref, y_ref, ynext_ref, d_ref, dnext_ref, w_ref, g_ref, beta_ref,
             dca_ref, dcg_ref, dwt_ref, db_ref, dg_ref, dbeta_ref, u_ref, us_ref, dy_ref, dys_ref):
        i = pl.program_id(0)
        last = i == nt - 1

        @pl.when(i == 0)
        def _():
            dwt_ref[...] = jnp.zeros_like(dwt_ref)
            db_ref[...] = jnp.zeros_like(db_ref)
            dg_ref[...] = jnp.zeros_like(dg_ref)
            dbeta_ref[...] = jnp.zeros_like(dbeta_ref)

        u_ref[0:H, :] = jnp.where(i > 0, _glu(prev_ref[...], C), 0.0)
        u_ref[H:, :] = _glu(x_ref[...], C)
        _fill_shifted(u_ref, us_ref)
        dg_acc = jnp.zeros((1, C), F32)
        dbeta_acc = jnp.zeros((1, C), F32)
        db_acc = jnp.zeros((1, C), F32)
        for c0 in range(0, TE, CH):
            y = y_ref[c0:c0 + CH, :] if c0 < T else ynext_ref[c0 - T:c0 - T + CH, :]
            mu = jnp.mean(y, axis=-1, keepdims=True)
            yc = y - mu
            rstd = lax.rsqrt(jnp.mean(yc * yc, axis=-1, keepdims=True) + EPS)
            yh = yc * rstd
            ln = yh * g_ref[...] + beta_ref[...]
            s = _sigmoid(ln)
            dout = d_ref[c0:c0 + CH, :] if c0 < T else jnp.where(last, 0.0, dnext_ref[c0 - T:c0 - T + CH, :])
            dln = dout * (s * (1.0 + ln * (1.0 - s)))
            dyh = dln * g_ref[...]
            dy = rstd * (dyh - jnp.mean(dyh, axis=-1, keepdims=True)
                         - yh * jnp.mean(dyh * yh, axis=-1, keepdims=True))
            dy_ref[c0:c0 + CH, :] = dy
            if c0 < T:
                dg_acc = dg_acc + jnp.sum(dln * yh, axis=0, keepdims=True)
                dbeta_acc = dbeta_acc + jnp.sum(dln, axis=0, keepdims=True)
                db_acc = db_acc + jnp.sum(dy, axis=0, keepdims=True)
        dg_ref[...] += dg_acc
        dbeta_ref[...] += dbeta_acc
        db_ref[...] += db_acc
        _fill_shifted(dy_ref, dys_ref)
        for k in range(K):
            dwt_ref[k:k + 1, :] += jnp.sum(
                dy_ref[0:T, :] * _rows_at(u_ref, us_ref, H - (K - 1) + k, T), axis=0, keepdims=True)
        for c0 in range(0, T, CH):
            du = jnp.zeros((CH, C), F32)
            for k in range(K):
                du = du + w_ref[k:k + 1, :] * _rows_at(dy_ref, dys_ref, c0 + (K - 1) - k, CH)
            x = x_ref[c0:c0 + CH, :]
            sg = _sigmoid(x[:, C:])
            dca_ref[c0:c0 + CH, :] = (du * sg).astype(dca_ref.dtype)
            dcg_ref[c0:c0 + CH, :] = (du * x[:, :C] * sg * (1.0 - sg)).astype(dcg_ref.dtype)

    nh = L // H
    vec = pl.BlockSpec((1, C), lambda i: (0, 0))
    row = pl.BlockSpec((T, C), lambda i: (i, 0))
    after = lambda i: jnp.minimum((i + 1) * (T // H), nh - 1)
    return _call_with_job(
        body, job, name=name, grid=(nt,),
        in_specs=[pl.BlockSpec((T, C2), lambda i: (i, 0)),
                  pl.BlockSpec((H, C2), lambda i: (jnp.maximum(i * (T // H) - 1, 0), 0)),
                  row, pl.BlockSpec((H, C), lambda i: (after(i), 0)),
                  pl.BlockSpec((T, C), lambda i: (i, 1)), pl.BlockSpec((H, C), lambda i: (after(i), 1)),
                  pl.BlockSpec((K, C), lambda i: (0, 0)), vec, vec],
        out_specs=[row, row, pl.BlockSpec((H, C), lambda i: (0, 0)), vec, vec, vec],
        out_shape=[jax.ShapeDtypeStruct((L, C), BF16), jax.ShapeDtypeStruct((L, C), BF16),
                   jax.ShapeDtypeStruct((H, C), F32), jax.ShapeDtypeStruct((1, C), F32),
                   jax.ShapeDtypeStruct((1, C), F32), jax.ShapeDtypeStruct((1, C), F32)],
        scratch_shapes=[pltpu.VMEM((T + H, C), F32), pltpu.VMEM((7, T + SHIFT_TAIL, C), F32),
                        pltpu.VMEM((TE, C), F32), pltpu.VMEM((7, T + SHIFT_TAIL, C), F32)],
        args=(cacg, cacg, y, y, dmixed, dmixed, dw_w, ln_g, ln_b))


def _local_step(h0, target, n_meta, seq, norms, conv_p, wts, final_g, gather_behind, reducer):
    mix_g, ffn_g = norms
    dw_w, dw_b, ln_g, ln_b = conv_p
    depth = mix_g.shape[0]
    C = dw_b.shape[-1]
    sbw = (wts["w_in"][0].shape[-1] - 2 * C) // 3
    assert sbw == C, "the mixer halves must have equal width"
    row = lambda a, i: a[i][None, :]

    h = h0
    saved = []
    for i in range(depth):
        hn, proj_qkv = _norm_in_proj(h, row(mix_g, i), wts["w_in"][i], 3 * sbw, f"in_qkv_{i}")
        cacg = _mm_nn([(hn, wts["w_in"][i], 0, 0)], F32, f"in_conv_{i}", cols=(3 * sbw, 2 * C))
        def hosting(kind):
            job, keys = gather_behind.get((kind, i), (None, ()))

            def sink(arrived):
                for (wname, wl), arr in zip(keys, arrived):
                    wts[wname][wl] = arr

            return job, sink

        job, sink = hosting("attn")
        attn, arrived = _attn_fwd(proj_qkv, sbw, f"attn_fwd_{i}", job)
        sink(arrived)
        job, sink = hosting("conv")
        conv, conv_y, arrived = _conv_fwd(cacg, dw_w[i], row(dw_b, i), row(ln_g, i), row(ln_b, i),
                                          f"conv_fwd_{i}", job)
        sink(arrived)
        h_mid = _mm_nn([(attn, wts["w_out"][i], 0, 0), (conv, wts["w_out"][i], 0, 1)], F32, f"out_proj_{i}",
                       residual=h)
        job, sink = hosting("ffn")
        (hn2, g, u, act), arrived = _ffn_up(h_mid, row(ffn_g, i), wts["w_gate_t"][i], wts["w_up_t"][i],
                                            f"ffn_up_{i}", job)
        sink(arrived)
        job, sink = hosting("down")
        h_out = _mm_nn([(act, wts["w_down"][i], 0, 0)], F32, f"down_{i}", residual=h_mid, job=job)
        if job is not None:
            h_out, arrived = h_out
            sink(arrived)
        saved.append((h, hn, proj_qkv, cacg, attn, conv, conv_y, h_mid, hn2, g, u, act))
        h = h_out

    loss, dh, d_final_g = _loss_head(h, final_g[None, :], target, n_meta, seq, "loss_head")

    grads = {k: [None] * depth for k in ("mix_g", "ffn_g", "dw_w", "dw_b", "ln_g", "ln_b")}
    for i in reversed(range(depth)):
        h_in, hn, proj_qkv, cacg, attn, conv, conv_y, h_mid, hn2, g, u, act = saved[i]
        big = {}
        dg, du = _ffn_down_bwd(dh, wts["w_down"][i], g, u, f"ffn_down_bwd_{i}")
        big["w_down"] = _mm_tn(act, dh, f"dw_down_{i}", col_sharded=False)
        big["w_gate_t"] = _mm_tn(dg, hn2, f"dw_gate_{i}", col_sharded=False)
        big["w_up_t"] = _mm_tn(du, hn2, f"dw_up_{i}", col_sharded=False)
        sib_job, sib_sink = reducer.to_sibling(i, big)
        dhn2, arrived = _mm_nn([(dg, wts["w_gate_t"][i], 0, 0), (du, wts["w_up_t"][i], 0, 0)], F32, f"d_hn2_{i}",
                               job=sib_job)
        sib_sink(arrived)
        dh, d_ffn = _rmsnorm_bwd(h_mid, row(ffn_g, i), dhn2, dh, f"ffn_norm_bwd_{i}")
        dmixed = _mm_nt([(dh, wts["w_out"][i], 0, 0)], F32, f"d_mixed_{i}")
        dw_out = jnp.concatenate([_mm_tn(attn, dh, f"dw_out_attn_{i}", col_sharded=False, chips=2),
                                  _mm_tn(conv, dh, f"dw_out_conv_{i}", col_sharded=False, chips=2)], axis=0)
        sib_job, sib_sink = reducer.to_sibling(i, {"w_out": dw_out})
        x_job, x_sink = reducer.take()
        (dq, dk, dv), arrived = _attn_bwd(proj_qkv, attn, dmixed, sbw, f"attn_bwd_{i}",
                                          _merge_jobs([sib_job, x_job]))
        sib_sink(arrived[:len(sib_job.out_shapes)])
        x_sink(arrived[len(sib_job.out_shapes):])
        x_job, x_sink = reducer.take()
        (dca, dcg, d_dw, d_b, d_lg, d_lb), arrived = _conv_bwd(
            cacg, conv_y, dmixed, dw_w[i], row(ln_g, i), row(ln_b, i), f"conv_bwd_{i}", x_job)
        x_sink(arrived)
        grads["dw_w"][i] = d_dw
        grads["dw_b"][i], grads["ln_g"][i], grads["ln_b"][i] = d_b, d_lg, d_lb
        dproj = jnp.concatenate([dq, dk, dv, dca, dcg], axis=1)
        dhn = _mm_nt([(dproj, wts["w_in"][i], 0, 0)], F32, f"d_hn_{i}")
        reducer.ready(i, {"w_in": _mm_tn(hn, dproj, f"dw_in_{i}", col_sharded=True)})
        if i == 0:
            x_job, x_sink = reducer.take()
            dh, d_mix, arrived = _rmsnorm_bwd(h_in, row(mix_g, i), dhn, dh, f"mix_norm_bwd_{i}", x_job)
            x_sink(arrived)
        else:
            dh, d_mix = _rmsnorm_bwd(h_in, row(mix_g, i), dhn, dh, f"mix_norm_bwd_{i}")
        grads["mix_g"][i], grads["ffn_g"][i] = d_mix, d_ffn
    grads["final_g"] = d_final_g
    return loss, dh, grads


ANY = pl.BlockSpec(memory_space=pl.ANY)


def _position():
    return lax.axis_index("x"), lax.axis_index("y"), lax.axis_index("c")


def _chip_at(x, y, k):
    return (1 - x if k & 2 else x), (1 - y if k & 1 else y)


def _half_rows(ref, half, rows, base=0):
    start = pl.multiple_of(base + half * rows, 8)
    lead = (slice(None),) * (len(ref.shape) - 2)
    return ref.at[(*lead, pl.ds(start, rows), slice(None))]


def _gather_job(fulls, shard_shapes, col_sharded):
    n = len(fulls)

    def tools(f_refs, send_sems, recv_sems):
        def block(wi, chip, half):
            _, R, C = shard_shapes[wi]
            if col_sharded[wi]:
                cols = pl.ds(pl.multiple_of(chip * C, LANES), C)
                return f_refs[wi].at[:, pl.ds(pl.multiple_of(half * (R // 2), 8), R // 2), cols]
            return _half_rows(f_refs[wi], half, R // 2, base=chip * R)

        def copy(wi, slot, blk, to):
            return pltpu.make_async_remote_copy(
                src_ref=blk, dst_ref=blk, send_sem=send_sems.at[6 * wi + slot],
                recv_sem=recv_sems.at[6 * wi + slot], device_id=to, device_id_type=MESH)

        return block, copy

    def start(_, f_refs, send_sems, recv_sems):
        block, copy = tools(f_refs, send_sems, recv_sems)
        x, y, c = _position()
        me = 2 * x + y
        for wi in range(n):
            for k in (1, 2, 3):
                copy(wi, k - 1, block(wi, me, c), (*_chip_at(x, y, k), c)).start()

    def finish(_, f_refs, send_sems, recv_sems):
        block, copy = tools(f_refs, send_sems, recv_sems)
        x, y, c = _position()
        me = 2 * x + y
        for wi in range(n):
            for k in (1, 2, 3):
                landed = block(wi, me ^ k, c)
                copy(wi, k - 1, landed, (x, y, c)).wait_recv()
                copy(wi, 2 + k, landed, (x, y, 1 - c)).start()
        for wi in range(n):
            for k in (1, 2, 3):
                copy(wi, 2 + k, block(wi, me ^ k, 1 - c), (x, y, c)).wait_recv()
        for wi in range(n):
            for k in (1, 2, 3):
                copy(wi, k - 1, block(wi, me, c), (x, y, c)).wait_send()
                copy(wi, 2 + k, block(wi, me ^ k, c), (x, y, c)).wait_send()

    return _CommJob(fulls, [jax.ShapeDtypeStruct(f.shape, f.dtype) for f in fulls], {i: i for i in range(n)},
                    6 * n, start, finish)


def _place_shard(w, layer, chip, col_sharded, dtype, name):
    _, R, C = w.shape
    tr = _pick(R, (256, 352, 128, 48))
    nr = R // tr

    def body(chip_ref, w_ref, o_ref):
        o_ref[...] = w_ref[...].astype(dtype)

    if col_sharded:
        shape = (1, R, N_CHIPS * C)
        out_spec = pl.BlockSpec((None, tr, C), lambda r, chip_ref: (0, r, chip_ref[0]))
    else:
        shape = (1, N_CHIPS * R, C)
        out_spec = pl.BlockSpec((None, tr, C), lambda r, chip_ref: (0, chip_ref[0] * nr + r, 0))
    grid_spec = pltpu.PrefetchScalarGridSpec(
        num_scalar_prefetch=1, grid=(nr,),
        in_specs=[pl.BlockSpec((None, tr, C), lambda r, chip_ref: (layer, r, 0))], out_specs=out_spec)
    return pl.pallas_call(
        body, name=name, grid_spec=grid_spec, out_shape=jax.ShapeDtypeStruct(shape, dtype),
        compiler_params=_params(("parallel",)),
    )(chip, w)


def _sibling_job(grads):
    n = len(grads)

    def copies(g_refs, l_refs, send_sems, recv_sems):
        x, y, c = _position()
        return [pltpu.make_async_remote_copy(
            src_ref=_half_rows(g_refs[wi], 1 - c, grads[wi].shape[1] // 2), dst_ref=l_refs[wi],
            send_sem=send_sems.at[wi], recv_sem=recv_sems.at[wi],
            device_id=(x, y, 1 - c), device_id_type=MESH) for wi in range(n)]

    def start(*refs):
        for cp in copies(*refs):
            cp.start()

    def finish(*refs):
        for cp in copies(*refs):
            cp.wait()

    outs = [jax.ShapeDtypeStruct((g.shape[0], g.shape[1] // 2, g.shape[2]), g.dtype) for g in grads]
    return _CommJob(grads, outs, {}, n, start, finish)


def _chip_sum(g, landed, core, name):
    _, R, C = g.shape
    hr = R // 2
    tr = _pick(hr, (256, 352, 128))
    nr = hr // tr

    def body(c_ref, g_ref, l_ref, o_ref):
        o_ref[...] = (g_ref[...] + l_ref[...]).astype(BF16)

    grid_spec = pltpu.PrefetchScalarGridSpec(
        num_scalar_prefetch=1, grid=(N_CHIPS, nr),
        in_specs=[pl.BlockSpec((None, tr, C), lambda j, r, c_ref: (j, c_ref[0] * nr + r, 0)),
                  pl.BlockSpec((None, tr, C), lambda j, r, c_ref: (j, r, 0))],
        out_specs=pl.BlockSpec((None, tr, C), lambda j, r, c_ref: (j, r, 0)))
    return pl.pallas_call(
        body, name=name, grid_spec=grid_spec, out_shape=jax.ShapeDtypeStruct((N_CHIPS, hr, C), BF16),
        compiler_params=_params(("parallel", "parallel")),
    )(core, g, landed)


def _across_job(parts):
    n = len(parts)

    def copy(p_refs, l_refs, send_sems, recv_sems, wi, k, to):
        x, y, _ = _position()
        me = 2 * x + y
        return pltpu.make_async_remote_copy(
            src_ref=p_refs[wi].at[me ^ k], dst_ref=l_refs[wi].at[me],
            send_sem=send_sems.at[3 * wi + k - 1], recv_sem=recv_sems.at[3 * wi + k - 1],
            device_id=to, device_id_type=MESH)

    def start(p_refs, l_refs, send_sems, recv_sems):
        x, y, c = _position()
        for wi in range(n):
            for k in (1, 2, 3):
                copy(p_refs, l_refs, send_sems, recv_sems, wi, k, (*_chip_at(x, y, k), c)).start()

    def finish(p_refs, l_refs, send_sems, recv_sems):
        x, y, c = _position()
        me = 2 * x + y
        for wi in range(n):
            for k in (1, 2, 3):
                slot = l_refs[wi].at[me ^ k]
                pltpu.make_async_remote_copy(
                    src_ref=slot, dst_ref=slot, send_sem=send_sems.at[3 * wi + k - 1],
                    recv_sem=recv_sems.at[3 * wi + k - 1], device_id=(x, y, c), device_id_type=MESH).wait_recv()
        for wi in range(n):
            for k in (1, 2, 3):
                copy(p_refs, l_refs, send_sems, recv_sems, wi, k, (x, y, c)).wait_send()

    return _CommJob(parts, [jax.ShapeDtypeStruct(p.shape, p.dtype) for p in parts], {}, 3 * n, start, finish)


class _Reducer:
    def __init__(self, core):
        self.core, self.parts, self.across, self.pending = core, {}, {}, []

    def to_sibling(self, layer, big):
        names = list(big)
        flat = [big[k] for k in names]

        def sink(landed):
            for k, g, la in zip(names, flat, landed):
                self.parts[k, layer] = _chip_sum(g, la, self.core, f"chip_sum_{k}_{layer}")
                self.pending.append((k, layer))

        return _sibling_job(flat), sink

    def ready(self, layer, big):
        job, sink = self.to_sibling(layer, big)
        sink(_run_job(job, f"grads_to_sibling_{next(iter(big))}_{layer}"))

    def take(self):
        keys, self.pending = self.pending, []
        if not keys:
            return None, lambda results: None

        def sink(results):
            self.across.update(zip(keys, results))

        return _across_job([self.parts[key] for key in keys]), sink


def _sum_chips(parts, landed, where, layer, depth, prev, name):
    _, hr, C = landed.shape
    tr = _pick(hr, (256, 352, 128))
    nr = hr // tr

    def body(*refs):
        own_ref, slots, o_ref = refs[1], refs[2:2 + N_CHIPS], refs[-1]
        chip = refs[0][0]
        total = None
        for q in range(N_CHIPS):
            term = jnp.where(chip == q, own_ref[...], slots[q][...]).astype(F32)
            total = term if total is None else total + term
        o_ref[...] = total

    def slot_spec(q):
        return pl.BlockSpec((None, tr, C), lambda r, w: (jnp.where(w[0] == q, (q + 1) % N_CHIPS, q), r, 0))

    in_specs = [pl.BlockSpec((None, tr, C), lambda r, w: (w[0], r, 0))] + [slot_spec(q) for q in range(N_CHIPS)]
    args = [where, parts] + [landed] * N_CHIPS
    aliases = {}
    if prev is not None:
        in_specs.append(ANY)
        args.append(prev)
        aliases = {len(args) - 1: 0}
    grid_spec = pltpu.PrefetchScalarGridSpec(
        num_scalar_prefetch=1, grid=(nr,), in_specs=in_specs,
        out_specs=pl.BlockSpec((None, tr, C), lambda r, w: (layer, w[1] * nr + r, 0)))
    return pl.pallas_call(
        body, name=name, grid_spec=grid_spec, out_shape=jax.ShapeDtypeStruct((depth, 2 * hr, C), F32),
        input_output_aliases=aliases, compiler_params=_params(("parallel",)),
    )(*args)


def _rs_join_halves(reduced):
    n = len(reduced)

    def body(*refs):
        o_refs = refs[n:2 * n]
        send_sems, recv_sems = refs[2 * n:]
        x, y, c = _position()
        sent = []
        for wi in range(n):
            hr = reduced[wi].shape[1] // 2
            mine = _half_rows(o_refs[wi], c, hr)
            cp = pltpu.make_async_remote_copy(
                src_ref=mine, dst_ref=mine, send_sem=send_sems.at[wi], recv_sem=recv_sems.at[wi],
                device_id=(x, y, 1 - c), device_id_type=MESH)
            cp.start()
            sent.append(cp)
        for wi in range(n):
            hr = reduced[wi].shape[1] // 2
            theirs = _half_rows(o_refs[wi], 1 - c, hr)
            pltpu.make_async_remote_copy(
                src_ref=theirs, dst_ref=theirs, send_sem=send_sems.at[wi], recv_sem=recv_sems.at[wi],
                device_id=(x, y, c), device_id_type=MESH).wait_recv()
        for cp in sent:
            cp.wait_send()

    return pl.pallas_call(
        body, name="grads_join_halves", out_shape=[jax.ShapeDtypeStruct(r.shape, r.dtype) for r in reduced],
        in_specs=[ANY] * n, out_specs=[ANY] * n, input_output_aliases={i: i for i in range(n)},
        scratch_shapes=[pltpu.SemaphoreType.DMA((n,)), pltpu.SemaphoreType.DMA((n,))],
    )(*reduced)


def _adam_math(w, g, m, v):
    m = ADAM_B1 * m + (1.0 - ADAM_B1) * g
    v = ADAM_B2 * v + (1.0 - ADAM_B2) * jnp.square(g)
    m_hat = m / (1.0 - ADAM_B1 ** ADAM_STEP)
    v_hat = v / (1.0 - ADAM_B2 ** ADAM_STEP)
    delta = -ADAM_LR * (m_hat / (jnp.sqrt(v_hat) + ADAM_EPS) + ADAM_WD * w)
    return delta, m, v


def _small_reduce_adam(vec_grads, dww_grads, final_grad, meta_grad, state):
    vec_names = list(vec_grads)
    depth = len(dww_grads)
    D = final_grad.shape[1]
    n_meta = meta_grad.shape[0]
    taps_pad, C = dww_grads[0].shape
    names = vec_names + ["final", "meta", "dww"]
    at, row0 = 0, {}
    for k in vec_names:
        row0[k] = at
        at += depth
    row0["final"] = at
    at = -(-(at + 1) // 8) * 8
    row0["meta"] = at
    at += -(-n_meta // 8) * 8
    row0["dww"] = at
    rows = at + depth * taps_pad
    lanes = max(D, C)
    n_g = len(vec_names) * depth + depth + 2

    def body(*refs):
        g_refs = refs[:n_g]
        st = refs[n_g:n_g + 3 * len(names)]
        outs = refs[n_g + 3 * len(names):n_g + 7 * len(names)]
        slab, land, send_sems, recv_sems = refs[n_g + 7 * len(names):]
        x, y, c = _position()
        me = 4 * x + 2 * y + c
        chip = 2 * x + y
        slab[...] = jnp.zeros_like(slab)
        it = iter(g_refs)
        for k in vec_names:
            for l in range(depth):
                g_ref = next(it)
                slab[row0[k] + l:row0[k] + l + 1, 0:g_ref.shape[1]] = g_ref[...]
        for l in range(depth):
            slab[row0["dww"] + l * taps_pad:row0["dww"] + (l + 1) * taps_pad, 0:C] = next(it)[...]
        slab[row0["final"]:row0["final"] + 1, 0:D] = next(it)[...]
        slab[row0["meta"]:row0["meta"] + n_meta, 0:D] = next(it)[...]
        sent = []
        for k in range(1, N_DEV):
            to = (1 - x if k & 4 else x, 1 - y if k & 2 else y, 1 - c if k & 1 else c)
            cp = pltpu.make_async_remote_copy(
                src_ref=slab, dst_ref=land.at[k], send_sem=send_sems.at[k - 1], recv_sem=recv_sems.at[k - 1],
                device_id=to, device_id_type=MESH)
            cp.start()
            sent.append(cp)
        land[0] = slab[...]
        for cp in sent:
            cp.wait_recv()
        for cp in sent:
            cp.wait_send()
        total = land[me]
        for e in range(1, N_DEV):
            total = total + land[me ^ e]
        slab[...] = total

        def mine(r0, n_rows, width):
            got = slab[r0:r0 + n_rows, 0:width]
            for j in range(1, N_CHIPS):
                got = jnp.where(chip == j, slab[r0:r0 + n_rows, j * width:(j + 1) * width], got)
            return got

        def update(i, g, index=()):
            w_ref, m_ref, v_ref = st[3 * i:3 * i + 3]
            o = outs[4 * i:4 * i + 4]
            at = index if index else Ellipsis
            res = (g,) + _adam_math(w_ref[at], g, m_ref[at], v_ref[at])
            for o_ref, val in zip(o, res):
                o_ref[at] = val

        for i, k in enumerate(vec_names):
            width = st[3 * i].shape[1]
            update(i, slab[row0[k]:row0[k] + depth, 0:width])
        base = len(vec_names)
        update(base, slab[row0["final"]:row0["final"] + 1, 0:D])
        update(base + 1, mine(row0["meta"], n_meta, D // N_CHIPS))
        taps = st[3 * (base + 2)].shape[1]
        for l in range(depth):
            update(base + 2, mine(row0["dww"] + l * taps_pad, taps, C // N_CHIPS), (l,))

    flat_g = [g for k in vec_names for g in vec_grads[k]] + list(dww_grads) + [final_grad, meta_grad]
    flat_state = [a for k in names for a in state[k]]
    vmem = pl.BlockSpec(memory_space=pltpu.VMEM)
    out_shape = [jax.ShapeDtypeStruct(state[k][0].shape, F32) for k in names for _ in range(4)]
    res = pl.pallas_call(
        body, name="small_reduce_adam", out_shape=out_shape,
        in_specs=[vmem] * (len(flat_g) + len(flat_state)), out_specs=[vmem] * len(out_shape),
        scratch_shapes=[pltpu.VMEM((rows, lanes), F32), pltpu.VMEM((N_DEV, rows, lanes), F32),
                        pltpu.SemaphoreType.DMA((N_DEV - 1,)), pltpu.SemaphoreType.DMA((N_DEV - 1,))],
    )(*flat_g, *flat_state)
    return {k: tuple(res[4 * i:4 * i + 4]) for i, k in enumerate(names)}


def _adam(w, g, m, v, name):
    def body(w_ref, g_ref, m_ref, v_ref, d_ref, nm_ref, nv_ref):
        d_ref[...], nm_ref[...], nv_ref[...] = _adam_math(w_ref[...], g_ref[...], m_ref[...], v_ref[...])

    if w.ndim == 3:
        lyr, R, C = w.shape
        tr = _pick(R, (256, 352, 128))
        blk = pl.BlockSpec((None, tr, C), lambda l, r: (l, r, 0))
        grid, sem = (lyr, R // tr), ("parallel", "parallel")
    else:
        blk = pl.BlockSpec(w.shape, lambda: (0, 0))
        grid, sem = (), None
    return pl.pallas_call(
        body, name=name, grid=grid, in_specs=[blk] * 4, out_specs=[blk] * 3,
        out_shape=[jax.ShapeDtypeStruct(w.shape, F32)] * 3, compiler_params=_params(sem),
    )(w, g, m, v)


def _rows(a, pad_to=8):
    r = a.reshape(-1, LANES)
    extra = (-r.shape[0]) % pad_to
    return jnp.pad(r, ((0, extra), (0, 0))) if extra else r


def _pack(arrays):
    return jnp.concatenate([_rows(a) for a in arrays], axis=0)


def _unpack(slab, shapes):
    out, at = [], 0
    for shp in shapes:
        nrow = math.prod(shp) // LANES
        out.append(slab[at:at + nrow].reshape(shp))
        at += nrow + (-nrow) % 8
    return out


BIG = ("w_in", "w_out", "w_gate_t", "w_up_t", "w_down")
BIG_COL_SHARDED = (True, False, False, False, False)
TRANSPOSED = {"w_gate_t": "w_gate", "w_up_t": "w_up"}


def kernel(x, meta_tokens, mix_norm_g, w_in, conv_dw_w, conv_dw_b, conv_ln_g, conv_ln_b, w_out, ffn_norm_g, w_gate, w_up, w_down, final_norm_g, loss_target, m_meta_tokens, m_mix_norm_g, m_w_in, m_conv_dw_w, m_conv_dw_b, m_conv_ln_g, m_conv_ln_b, m_w_out, m_ffn_norm_g, m_w_gate, m_w_up, m_w_down, m_final_norm_g, v_meta_tokens, v_mix_norm_g, v_w_in, v_conv_dw_w, v_conv_dw_b, v_conv_ln_g, v_conv_ln_b, v_w_out, v_ffn_norm_g, v_w_gate, v_w_up, v_w_down, v_final_norm_g):
    n_meta, seq = meta_tokens.shape[0], x.shape[1]
    D = x.shape[2]
    depth, taps, c_shard = conv_dw_w.shape
    C = conv_dw_b.shape[-1]
    chip = (2 * lax.axis_index("x") + lax.axis_index("y")).astype(jnp.int32)
    core = lax.axis_index("c").astype(jnp.int32).reshape(1)
    chip1 = chip.reshape(1)
    where = jnp.concatenate([chip1, core])
    big_w = dict(w_in=w_in, w_out=w_out, w_gate=w_gate, w_up=w_up, w_down=w_down)
    big_m = dict(w_in=m_w_in, w_out=m_w_out, w_gate=m_w_gate, w_up=m_w_up, w_down=m_w_down)
    big_v = dict(w_in=v_w_in, w_out=v_w_out, w_gate=v_w_gate, w_up=v_w_up, w_down=v_w_down)

    small_shard = _pack([conv_dw_w, meta_tokens])[None]
    to_send = {k: jnp.swapaxes(big_w[TRANSPOSED[k]], 1, 2) if k in TRANSPOSED else big_w[k] for k in BIG}
    col = dict(zip(BIG, BIG_COL_SHARDED))
    wts = {k: [_place_shard(to_send[k], l, chip1, col[k], BF16, f"place_{k}_{l}") for l in range(depth)] for k in BIG}
    small_placed = _place_shard(small_shard, 0, chip1, False, F32, "place_small")

    def gather_job(keys, extra=()):
        arrays = [wts[k][l] for k, l in keys] + list(extra)
        shapes = [(1,) + to_send[k].shape[1:] for k, _ in keys] + [(1,) + small_shard.shape[1:]] * len(extra)
        return _gather_job(arrays, shapes, [col[k] for k, _ in keys] + [False] * len(extra))

    first_keys = [("w_in", 0)]
    *first, small_full = _run_job(gather_job(first_keys, [small_placed]), "gather_first")
    for (k, l), arr in zip(first_keys, first):
        wts[k][l] = arr
    behind_keys = {("attn", 0): [("w_out", 0), ("w_gate_t", 0), ("w_up_t", 0)], ("conv", 0): [("w_down", 0)]}
    for l in range(1, depth):
        behind_keys["ffn", l - 1] = [("w_in", l), ("w_gate_t", l)]
        behind_keys["down", l - 1] = [("w_up_t", l)]
        behind_keys["attn", l] = [("w_out", l), ("w_down", l)]
    gather_behind = {host: (gather_job(keys), keys) for host, keys in behind_keys.items()}

    rows_shard = small_shard.shape[1]
    dw_full, meta_full = [], []
    for j in range(N_CHIPS):
        dwj, mj = _unpack(small_full[0, j * rows_shard:(j + 1) * rows_shard],
                          [conv_dw_w.shape, meta_tokens.shape])
        dw_full.append(dwj)
        meta_full.append(mj)
    dw_w_full = jnp.concatenate(dw_full, axis=2)
    meta = jnp.concatenate(meta_full, axis=1)

    L = n_meta + seq
    Lp = -(-L // QUERY_BLOCK) * QUERY_BLOCK
    h0 = jnp.concatenate([meta, x[0], jnp.zeros((Lp - L, D), F32)], axis=0)
    target = jnp.pad(loss_target[0], ((n_meta, Lp - L), (0, 0)))
    reducer = _Reducer(core)
    loss, dh0, grads = _local_step(h0, target, n_meta, seq, (mix_norm_g, ffn_norm_g),
                                   (dw_w_full, conv_dw_b, conv_ln_g, conv_ln_b), wts, final_norm_g,
                                   gather_behind, reducer)
    loss = lax.psum(loss[0, 0], ("x", "y", "c"))
    grad_x = dh0[n_meta:L][None]

    reduced = []
    for k in BIG:
        arr = None
        for l in range(depth):
            arr = _sum_chips(reducer.parts[k, l], reducer.across[k, l], where, l, depth, arr, f"sum_chips_{k}_{l}")
        reduced.append(arr)
    big_g = dict(zip(BIG, _rs_join_halves(reduced)))

    out_g, out_d, out_m, out_v = {}, {}, {}, {}
    for kk in BIG:
        k = TRANSPOSED.get(kk, kk)
        view = (lambda a: jnp.swapaxes(a, 1, 2)) if kk in TRANSPOSED else (lambda a: a)
        res = _adam(view(big_w[k]), big_g[kk], view(big_m[k]), view(big_v[k]), f"adam_{k}")
        out_g[k] = view(big_g[kk])
        out_d[k], out_m[k], out_v[k] = (view(a) for a in res)

    as_row = lambda a: a.reshape(1, -1)
    state = dict(mix_g=(mix_norm_g, m_mix_norm_g, v_mix_norm_g), ffn_g=(ffn_norm_g, m_ffn_norm_g, v_ffn_norm_g),
                 dw_b=(conv_dw_b, m_conv_dw_b, v_conv_dw_b), ln_g=(conv_ln_g, m_conv_ln_g, v_conv_ln_g),
                 ln_b=(conv_ln_b, m_conv_ln_b, v_conv_ln_b),
                 final=(as_row(final_norm_g), as_row(m_final_norm_g), as_row(v_final_norm_g)),
                 meta=(meta_tokens, m_meta_tokens, v_meta_tokens), dww=(conv_dw_w, m_conv_dw_w, v_conv_dw_w))
    vec_names = ("mix_g", "ffn_g", "dw_b", "ln_g", "ln_b")
    small = _small_reduce_adam({k: grads[k] for k in vec_names}, grads["dw_w"], grads["final_g"], dh0[:n_meta], state)
    out_name = dict(mix_g="mix_norm_g", ffn_g="ffn_norm_g", dw_b="conv_dw_b", ln_g="conv_ln_g", ln_b="conv_ln_b",
                    final="final_norm_g", meta="meta_tokens", dww="conv_dw_w")
    for k, res in small.items():
        if k == "final":
            res = tuple(a.reshape(-1) for a in res)
        out_g[out_name[k]], out_d[out_name[k]], out_m[out_name[k]], out_v[out_name[k]] = res

    order = ("meta_tokens", "mix_norm_g", "w_in", "conv_dw_w", "conv_dw_b", "conv_ln_g", "conv_ln_b", "w_out",
             "ffn_norm_g", "w_gate", "w_up", "w_down", "final_norm_g")
    return (loss, grad_x, *[out_g[k] for k in order], *[out_d[k] for k in order],
            *[out_m[k] for k in order], *[out_v[k] for k in order])
```

```python
import functools
import math

import jax
import jax.numpy as jnp
from jax import lax
from jax.experimental import pallas as pl
from jax.experimental.pallas import tpu as pltpu

F32 = jnp.float32
BF16 = jnp.bfloat16
MESH = pl.DeviceIdType.MESH

EPS = 1e-6
QUERY_BLOCK = 128
LANES = 128
HEAD_DIM = 64
LOG_STICK_FLOOR = -40.0
CONV_HALO = 32
N_CHIPS = 4
N_DEV = 8
VMEM_LIMIT = 56 * 1024 * 1024

ADAM_LR = 0.001
ADAM_B1 = 0.9
ADAM_B2 = 0.999
ADAM_EPS = 1e-08
ADAM_WD = 0.01
ADAM_STEP = 10


def _pick(n, prefs):
    for p in prefs:
        if n % p == 0:
            return p
    return n


def _params(sem=None):
    return pltpu.CompilerParams(dimension_semantics=sem, vmem_limit_bytes=VMEM_LIMIT)


def _sigmoid(x):
    return 1.0 / (1.0 + jnp.exp(-x))


def _rmsnorm_bwd(h, g, dy, dh_in, name, job=None):
    L, D = h.shape
    T = _pick(L, (384, 128))

    def body(h_ref, g_ref, dy_ref, dhin_ref, dh_ref, dg_ref):
        x = h_ref[...]
        dyv = dy_ref[...]
        r = lax.rsqrt(jnp.mean(x * x, axis=-1, keepdims=True) + EPS)
        xh = x * r
        dxh = dyv * g_ref[...]
        dh_ref[...] = dhin_ref[...] + r * (dxh - xh * jnp.mean(dxh * xh, axis=-1, keepdims=True))

        @pl.when(pl.program_id(0) == 0)
        def _():
            dg_ref[...] = jnp.zeros_like(dg_ref)

        dg_ref[...] += jnp.sum(dyv * xh, axis=0, keepdims=True)

    row = pl.BlockSpec((T, D), lambda i: (i, 0))
    vec = pl.BlockSpec((1, D), lambda i: (0, 0))
    (dh, dg), arrived = _call_with_job(
        body, job, name=name, grid=(L // T,),
        in_specs=[row, vec, row, row], out_specs=[row, vec],
        out_shape=[jax.ShapeDtypeStruct((L, D), F32), jax.ShapeDtypeStruct((1, D), F32)],
        scratch_shapes=[], args=(h, g, dy, dh_in))
    return (dh, dg) if job is None else (dh, dg, arrived)


def _loss_head(h, g, target, n_meta, seq, name):
    L, D = h.shape
    T = _pick(L, (384, 128))

    def body(h_ref, g_ref, t_ref, loss_ref, dh_ref, dg_ref):
        i = pl.program_id(0)
        x = h_ref[...]
        gv = g_ref[...]
        r = lax.rsqrt(jnp.mean(x * x, axis=-1, keepdims=True) + EPS)
        xh = x * r
        y = xh * gv
        rows = i * T + lax.broadcasted_iota(jnp.int32, (T, 1), 0)
        live = (rows >= n_meta) & (rows < n_meta + seq)
        diff = jnp.where(live, y - t_ref[...], 0.0)
        dyv = diff / D
        dxh = dyv * gv
        dh_ref[...] = r * (dxh - xh * jnp.mean(dxh * xh, axis=-1, keepdims=True))

        @pl.when(i == 0)
        def _():
            dg_ref[...] = jnp.zeros_like(dg_ref)
            loss_ref[...] = jnp.zeros_like(loss_ref)

        dg_ref[...] += jnp.sum(dyv * xh, axis=0, keepdims=True)
        per_row = jnp.mean(diff * diff, axis=-1, keepdims=True)
        loss_ref[...] += 0.5 * jnp.sum(per_row, axis=0, keepdims=True)

    row = pl.BlockSpec((T, D), lambda i: (i, 0))
    vec = pl.BlockSpec((1, D), lambda i: (0, 0))
    one = pl.BlockSpec((1, 1), lambda i: (0, 0))
    return pl.pallas_call(
        body, name=name, grid=(L // T,),
        in_specs=[row, vec, row], out_specs=[one, row, vec],
        out_shape=[jax.ShapeDtypeStruct((1, 1), F32), jax.ShapeDtypeStruct((L, D), F32),
                   jax.ShapeDtypeStruct((1, D), F32)],
        compiler_params=_params(("arbitrary",)),
    )(h, g, target)


def _ffn_tiles(M, F):
    return _pick(M, (352, 384, 128)), _pick(F, (1408, 512, 256, 128))


def _resident(shape):
    return pl.BlockSpec((None,) + tuple(shape[1:]), lambda *_: (0,) * len(shape), pipeline_mode=pl.Buffered(1))


def _norm_in_proj(h, g, w_in, n_qkv, name):
    M, D = h.shape
    N = w_in.shape[2]
    tm = _pick(M, (352, 384, 128))
    tn = _pick(math.gcd(n_qkv, N - n_qkv), (512, 256, 128))

    def body(h_ref, g_ref, w_ref, hn_ref, qkv_ref, rest_ref):
        x = h_ref[...]
        r = lax.rsqrt(jnp.mean(x * x, axis=-1, keepdims=True) + EPS)
        hv = (x * r * g_ref[...]).astype(BF16)
        hn_ref[...] = hv
        for c in range(0, N, tn):
            out = jnp.dot(hv, w_ref[:, c:c + tn], preferred_element_type=F32)
            if c < n_qkv:
                qkv_ref[:, c:c + tn] = out.astype(qkv_ref.dtype)
            else:
                rest_ref[:, c - n_qkv:c - n_qkv + tn] = out

    rows = pl.BlockSpec((tm, D), lambda i: (i, 0))
    return pl.pallas_call(
        body, name=name, grid=(M // tm,),
        in_specs=[rows, pl.BlockSpec((1, D), lambda i: (0, 0)), _resident(w_in.shape)],
        out_specs=[rows, pl.BlockSpec((tm, n_qkv), lambda i: (i, 0)), pl.BlockSpec((tm, N - n_qkv), lambda i: (i, 0))],
        out_shape=[jax.ShapeDtypeStruct((M, D), BF16), jax.ShapeDtypeStruct((M, n_qkv), BF16),
                   jax.ShapeDtypeStruct((M, N - n_qkv), F32)],
        compiler_params=_params(("parallel",)),
    )(h, g, w_in)


def _ffn_up(h, norm_g, w_gate_t, w_up_t, name, job=None):
    M, D = h.shape
    F = w_gate_t.shape[1]
    tm, tf = _ffn_tiles(M, F)
    nt = (((1,), (1,)), ((), ()))

    def body(h_ref, ng_ref, wg_ref, wu_ref, hn_ref, g_ref, u_ref, a_ref):
        x = h_ref[...]
        r = lax.rsqrt(jnp.mean(x * x, axis=-1, keepdims=True) + EPS)
        hv = (x * r * ng_ref[...]).astype(BF16)
        hn_ref[...] = hv
        for c in range(0, F, tf):
            gv = lax.dot_general(hv, wg_ref[c:c + tf, :], nt, preferred_element_type=F32)
            uv = lax.dot_general(hv, wu_ref[c:c + tf, :], nt, preferred_element_type=F32)
            g_ref[:, c:c + tf] = gv.astype(g_ref.dtype)
            u_ref[:, c:c + tf] = uv.astype(u_ref.dtype)
            a_ref[:, c:c + tf] = (gv * _sigmoid(gv) * uv).astype(a_ref.dtype)

    rows = pl.BlockSpec((tm, D), lambda i: (i, 0))
    wide = pl.BlockSpec((tm, F), lambda i: (i, 0))
    return _call_with_job(
        body, job, name=name, grid=(M // tm,),
        in_specs=[rows, pl.BlockSpec((1, D), lambda i: (0, 0)), _resident(w_gate_t.shape), _resident(w_up_t.shape)],
        out_specs=[rows, wide, wide, wide],
        out_shape=[jax.ShapeDtypeStruct((M, D), BF16), jax.ShapeDtypeStruct((M, F), BF16),
                   jax.ShapeDtypeStruct((M, F), BF16), jax.ShapeDtypeStruct((M, F), BF16)],
        scratch_shapes=[], args=(h, norm_g, w_gate_t, w_up_t))


def _ffn_down_bwd(dh, w_down, g, u, name):
    M, D = dh.shape
    F = g.shape[1]
    tm, tf = _ffn_tiles(M, F)

    def body(d_ref, w_ref, g_ref, u_ref, dg_ref, du_ref):
        dhv = d_ref[...].astype(BF16)
        for c in range(0, F, tf):
            dv = lax.dot_general(dhv, w_ref[c:c + tf, :], (((1,), (1,)), ((), ())), preferred_element_type=F32)
            gv = g_ref[:, c:c + tf].astype(F32)
            s = _sigmoid(gv)
            du_ref[:, c:c + tf] = (dv * (gv * s)).astype(du_ref.dtype)
            dg_ref[:, c:c + tf] = (dv * u_ref[:, c:c + tf].astype(F32)
                                   * (s * (1.0 + gv * (1.0 - s)))).astype(dg_ref.dtype)

    wide = pl.BlockSpec((tm, F), lambda i: (i, 0))
    return pl.pallas_call(
        body, name=name, grid=(M // tm,),
        in_specs=[pl.BlockSpec((tm, D), lambda i: (i, 0)), _resident(w_down.shape), wide, wide],
        out_specs=[wide, wide],
        out_shape=[jax.ShapeDtypeStruct((M, F), BF16), jax.ShapeDtypeStruct((M, F), BF16)],
        compiler_params=_params(("parallel",)),
    )(dh, w_down, g, u)


def _mm_nn(pairs, out_dtype, name, residual=None, cols=None, job=None):
    M = pairs[0][0].shape[0]
    col0, N = cols if cols is not None else (0, pairs[0][1].shape[-1])
    tm = _pick(M, (1056, 384, 128))
    shallow = sum(p[0].shape[1] for p in pairs) <= 1024
    tn = _pick(math.gcd(N, col0) if col0 else N, ((1024,) if shallow else ()) + (640, 512, 256, 128))
    jb = col0 // tn
    n = len(pairs)

    def body(*refs):
        a_refs, w_refs = refs[:n], refs[n:2 * n]
        o_ref = refs[-1]
        acc = None
        for a_ref, w_ref in zip(a_refs, w_refs):
            d = jnp.dot(a_ref[...].astype(BF16), w_ref[...], preferred_element_type=F32)
            acc = d if acc is None else acc + d
        if residual is not None:
            acc = acc + refs[2 * n][...]
        o_ref[...] = acc.astype(o_ref.dtype)

    in_specs = [pl.BlockSpec((tm, a.shape[1]), lambda i, j: (i, 0)) for a, _, _, _ in pairs]
    for a, _, layer, kblk in pairs:
        in_specs.append(pl.BlockSpec((None, a.shape[1], tn), functools.partial(lambda i, j, l, kb: (l, kb, j + jb), l=layer, kb=kblk)))
    args = [p[0] for p in pairs] + [p[1] for p in pairs]
    if residual is not None:
        in_specs.append(pl.BlockSpec((tm, tn), lambda i, j: (i, j)))
        args.append(residual)
    (out,), arrived = _call_with_job(
        body, job, name=name, grid=(M // tm, N // tn), in_specs=in_specs,
        out_specs=[pl.BlockSpec((tm, tn), lambda i, j: (i, j))],
        out_shape=[jax.ShapeDtypeStruct((M, N), out_dtype)], scratch_shapes=[], args=args)
    return out if job is None else (out, arrived)


def _mm_nt(pairs, out_dtype, name):
    M = pairs[0][0].shape[0]
    K = pairs[0][1].shape[1]
    tm = _pick(M, (1056, 384, 128))
    shallow = sum(p[0].shape[1] for p in pairs) <= 1024
    tk = _pick(K, ((1024,) if shallow else ()) + (512, 1408, 256, 128))
    n = len(pairs)

    def body(*refs):
        d_refs, w_refs = refs[:n], refs[n:2 * n]
        o_ref = refs[-1]
        acc = None
        for d_ref, w_ref in zip(d_refs, w_refs):
            d = lax.dot_general(d_ref[...].astype(BF16), w_ref[...], (((1,), (1,)), ((), ())),
                                preferred_element_type=F32)
            acc = d if acc is None else acc + d
        o_ref[...] = acc.astype(o_ref.dtype)

    in_specs = [pl.BlockSpec((tm, d.shape[1]), lambda i, j: (i, 0)) for d, _, _, _ in pairs]
    for d, _, layer, cblk in pairs:
        in_specs.append(pl.BlockSpec((None, tk, d.shape[1]), functools.partial(lambda i, j, l, cb: (l, j, cb), l=layer, cb=cblk)))
    args = [p[0] for p in pairs] + [p[1] for p in pairs]
    return pl.pallas_call(
        body, name=name, grid=(M // tm, K // tk), in_specs=in_specs,
        out_specs=pl.BlockSpec((tm, tk), lambda i, j: (i, j)),
        out_shape=jax.ShapeDtypeStruct((M, K), out_dtype),
        compiler_params=_params(("parallel", "parallel")),
    )(*args)


def _mm_tn(a, b, name, col_sharded, chips=N_CHIPS):
    M, K = a.shape
    N = b.shape[1]
    tm = _pick(M, (1056, 384, 128))
    tk = _pick(K, (1024, 1408, 512, 256, 128))
    tn = N // N_CHIPS if col_sharded else _pick(N, (1024, 512, 128))

    def body(a_ref, b_ref, o_ref):
        @pl.when(pl.program_id(2) == 0)
        def _():
            o_ref[...] = jnp.zeros_like(o_ref)

        o_ref[...] += lax.dot_general(a_ref[...].astype(BF16), b_ref[...].astype(BF16),
                                      (((0,), (0,)), ((), ())), preferred_element_type=F32)

    if col_sharded:
        out_shape = jax.ShapeDtypeStruct((N_CHIPS, K, tn), F32)
        out_spec = pl.BlockSpec((None, tk, tn), lambda k, j, m: (j, k, 0))
    else:
        out_shape = jax.ShapeDtypeStruct((K, N), F32)
        out_spec = pl.BlockSpec((tk, tn), lambda k, j, m: (k, j))
    out = pl.pallas_call(
        body, name=name, grid=(K // tk, N // tn, M // tm),
        in_specs=[pl.BlockSpec((tm, tk), lambda k, j, m: (m, k)), pl.BlockSpec((tm, tn), lambda k, j, m: (m, j))],
        out_specs=out_spec, out_shape=out_shape,
        compiler_params=_params(("parallel", "parallel", "arbitrary")),
    )(a, b)
    return out if col_sharded else out.reshape(chips, K // chips, N)


def _stack_heads(x, scale=None):
    lane = lax.broadcasted_iota(jnp.int32, x.shape, 1)
    zero = jnp.zeros_like(x)
    lo = jnp.where(lane < HEAD_DIM, x, zero)
    hi = jnp.where(lane < HEAD_DIM, zero, x)
    out = jnp.concatenate([lo, hi], axis=0)
    return out if scale is None else out * scale


def _unstack_heads(x2):
    qb = x2.shape[0] // 2
    lane = lax.broadcasted_iota(jnp.int32, (qb, LANES), 1)
    return jnp.where(lane < HEAD_DIM, x2[:qb], x2[qb:])


def _dot2(x, m):
    xh = x.astype(BF16)
    xl = (x - xh.astype(F32)).astype(BF16)
    return jnp.dot(xh, m, preferred_element_type=F32) + jnp.dot(xl, m, preferred_element_type=F32)


ATTN_CHUNK = 32


def _in_lockstep(staged):
    waiting, live = list(staged), []
    while waiting or live:
        if waiting:
            live.append(waiting.pop(0))
        for gen in list(live):
            if next(gen, StopIteration) is StopIteration:
                live.remove(gen)


def _row_chunks():
    return [slice(r, r + ATTN_CHUNK) for r in range(0, 2 * QUERY_BLOCK, ATTN_CHUNK)]


def _chunk_valid(rows, older):
    shape = (rows.stop - rows.start, older + QUERY_BLOCK)
    r = (rows.start + lax.broadcasted_iota(jnp.int32, shape, 0)) & (QUERY_BLOCK - 1)
    return lax.broadcasted_iota(jnp.int32, shape, 1) < r + older


def _split_to(x, hi_ref, lo_ref, rows):
    xh = x.astype(BF16)
    hi_ref[rows, :] = xh
    lo_ref[rows, :] = (x - xh.astype(F32)).astype(BF16)


def _triangle(keys, strict):
    r = lax.broadcasted_iota(jnp.int32, (keys, keys), 0)
    c = lax.broadcasted_iota(jnp.int32, (keys, keys), 1)
    return jnp.where((r > c) if strict else (r >= c), 1.0, 0.0).astype(BF16)


def _fill_attn_consts(tri_ref, mask_ref):
    n = 2 * QUERY_BLOCK
    tri_ref[0] = _triangle(n, True)
    tri_ref[1] = _triangle(n, False)
    mask_ref[...] = jnp.where(_chunk_valid(slice(0, n), QUERY_BLOCK), 1.0, 0.0)


def _valid(mask_ref, rows, older):
    cols = slice(0, 2 * QUERY_BLOCK) if older else slice(QUERY_BLOCK, 2 * QUERY_BLOCK)
    return mask_ref[rows, cols] > 0.5


def _row_total(first):
    lane = lax.broadcasted_iota(jnp.int32, first.shape, 1)
    total = jnp.sum(jnp.where(lane == 0, first, 0.0), axis=1, keepdims=True)
    return jnp.broadcast_to(total, first.shape)


def _pairs_per_step(n_pairs, most):
    return max(g for g in (1, 2, 4) if g <= most and n_pairs % g == 0)


def _lanes(g):
    return slice(g * LANES, (g + 1) * LANES)


def _sweep_older(i, step, carry_ref, first):
    def cond(state):
        n, live = state
        return jnp.logical_and(n < i, live)

    def older(state):
        n, _ = state
        step(i - 1 - n, False)
        return n + 1, jnp.max(carry_ref[...]) > LOG_STICK_FLOOR

    lax.while_loop(cond, older, (first, jnp.max(carry_ref[...]) > LOG_STICK_FLOOR))


class _CommJob:
    def __init__(self, inputs, out_shapes, aliases, n_sems, start, finish):
        self.inputs, self.out_shapes, self.aliases, self.n_sems = list(inputs), list(out_shapes), aliases, n_sems
        self.start, self.finish = start, finish


def _merge_jobs(jobs):
    jobs = [j for j in jobs if j is not None]
    if len(jobs) <= 1:
        return jobs[0] if jobs else None
    spans, aliases = [], {}
    i0 = o0 = s0 = 0
    for j in jobs:
        spans.append((i0, o0, s0))
        aliases.update({i0 + a: o0 + b for a, b in j.aliases.items()})
        i0, o0, s0 = i0 + len(j.inputs), o0 + len(j.out_shapes), s0 + j.n_sems

    def run(which):
        def go(ins, outs, send_sems, recv_sems):
            for j, (i, o, s) in zip(jobs, spans):
                getattr(j, which)(ins[i:i + len(j.inputs)], outs[o:o + len(j.out_shapes)],
                                  send_sems.at[pl.ds(s, j.n_sems)], recv_sems.at[pl.ds(s, j.n_sems)])
        return go

    return _CommJob([a for j in jobs for a in j.inputs], [s for j in jobs for s in j.out_shapes], aliases, s0,
                    run("start"), run("finish"))


def _call_with_job(core_body, job, *, name, grid, in_specs, out_specs, out_shape, scratch_shapes, args):
    sem = ("arbitrary",) * len(grid)
    if job is None:
        res = pl.pallas_call(core_body, name=name, grid=grid, in_specs=in_specs, out_specs=out_specs,
                             out_shape=out_shape, scratch_shapes=scratch_shapes, compiler_params=_params(sem))(*args)
        return list(res), []
    n_in, n_out, n_scr = len(in_specs), len(out_specs), len(scratch_shapes)
    m_in, m_out = len(job.inputs), len(job.out_shapes)

    def body(*refs):
        at = 0
        parts = []
        for count in (n_in, m_in, n_out, m_out, n_scr, 2):
            parts.append(refs[at:at + count])
            at += count
        ins, job_in, outs, job_outs, scratch, (send_sems, recv_sems) = parts
        first = functools.reduce(jnp.logical_and, [pl.program_id(a) == 0 for a in range(len(grid))])
        last = functools.reduce(jnp.logical_and, [pl.program_id(a) == grid[a] - 1 for a in range(len(grid))])

        @pl.when(first)
        def _():
            job.start(job_in, job_outs, send_sems, recv_sems)

        core_body(*ins, *outs, *scratch)

        @pl.when(last)
        def _():
            job.finish(job_in, job_outs, send_sems, recv_sems)

    res = pl.pallas_call(
        body, name=name, grid=grid, in_specs=list(in_specs) + [ANY] * m_in, out_specs=list(out_specs) + [ANY] * m_out,
        out_shape=list(out_shape) + job.out_shapes,
        input_output_aliases={n_in + a: n_out + b for a, b in job.aliases.items()},
        scratch_shapes=list(scratch_shapes) + [pltpu.SemaphoreType.DMA((job.n_sems,)), pltpu.SemaphoreType.DMA((job.n_sems,))],
        compiler_params=_params(sem),
    )(*args, *job.inputs)
    return list(res[:n_out]), list(res[n_out:])


def _run_job(job, name):
    m_in, m_out = len(job.inputs), len(job.out_shapes)

    def body(*refs):
        job_in, job_outs = refs[:m_in], refs[m_in:m_in + m_out]
        send_sems, recv_sems = refs[m_in + m_out:]
        job.start(job_in, job_outs, send_sems, recv_sems)
        job.finish(job_in, job_outs, send_sems, recv_sems)

    return list(pl.pallas_call(
        body, name=name, in_specs=[ANY] * m_in, out_specs=[ANY] * m_out, out_shape=job.out_shapes,
        input_output_aliases=dict(job.aliases),
        scratch_shapes=[pltpu.SemaphoreType.DMA((job.n_sems,)), pltpu.SemaphoreType.DMA((job.n_sems,))],
    )(*job.inputs))


def _attn_fwd(qkv, sb_width, name, job=None):
    L = qkv.shape[0]
    QB = QUERY_BLOCK
    nb = L // QB
    n_pairs = sb_width // LANES
    G = _pairs_per_step(n_pairs, 4)
    W = G * LANES
    scale = 1.0 / math.sqrt(HEAD_DIM)

    def body(q_ref, k_ref, v_ref, o_ref, acc_ref, carry_ref, f32_ref, bf16_ref, tri_ref, mask_ref):
        i = pl.program_id(1)

        @pl.when(i == 0)
        def _():
            _fill_attn_consts(tri_ref, mask_ref)

        q2 = [_stack_heads(q_ref[:, _lanes(g)], scale) for g in range(G)]
        acc_ref[...] = jnp.zeros_like(acc_ref)
        carry_ref[...] = jnp.zeros_like(carry_ref)

        def step(g, j, older, masked):
            n = older + QB
            start = pl.multiple_of(j * QB, QB)
            kb = k_ref[pl.ds(start, n), _lanes(g)]
            vb = v_ref[pl.ds(start, n), _lanes(g)]
            zs, as_, bs = (f32_ref.at[g, t, :, :n] for t in range(3))
            hi, lo = (bf16_ref.at[g, t, :, :n] for t in range(2))
            zs[...] = lax.dot_general(q2[g], kb, (((1,), (1,)), ((), ())), preferred_element_type=F32)
            yield
            for rows in _row_chunks():
                z = zs[rows, :]
                sp = jnp.log(1.0 + jnp.exp(-jnp.abs(z)))
                b = jnp.minimum(-z, 0.0) - sp
                if masked:
                    b = jnp.where(_valid(mask_ref, rows, older),b, 0.0)
                as_[rows, :] = jnp.minimum(z, 0.0) - sp
                bs[rows, :] = b
                _split_to(b, hi, lo, rows)
            yield
            tri = tri_ref[0, :n, :n]
            zs[...] = (jnp.dot(hi[...], tri, preferred_element_type=F32)
                       + jnp.dot(lo[...], tri, preferred_element_type=F32))
            yield
            for rows in _row_chunks():
                excl = zs[rows, :]
                total = _row_total(excl[:, :LANES] + bs[rows, :LANES])
                if not masked:
                    excl = excl + jnp.tile(carry_ref[g, rows, :], (1, n // LANES))
                carry_ref[g, rows, :] += total
                w = jnp.exp(as_[rows, :] + excl)
                if masked:
                    w = jnp.where(_valid(mask_ref, rows, older),w, 0.0)
                _split_to(w, hi, lo, rows)
            yield
            acc_ref[g] += (jnp.dot(hi[...], vb, preferred_element_type=F32)
                           + jnp.dot(lo[...], vb, preferred_element_type=F32))

        @pl.when(i == 0)
        def _():
            _in_lockstep([step(g, 0, 0, True) for g in range(G)])

        @pl.when(i > 0)
        def _():
            _in_lockstep([step(g, i - 1, QB, True) for g in range(G)])
            for g in range(G):
                _sweep_older(i, lambda j, _, g=g: _in_lockstep([step(g, j, 0, False)]), carry_ref.at[g], 1)

        for g in range(G):
            o_ref[:, _lanes(g)] = _unstack_heads(acc_ref[g])

    n_steps = n_pairs // G
    (out,), job_out = _call_with_job(
        body, job, name=name, grid=(n_steps, nb),
        in_specs=[pl.BlockSpec((QB, W), lambda p, i: (i, p)),
                  pl.BlockSpec((L, W), lambda p, i: (0, n_steps + p)),
                  pl.BlockSpec((L, W), lambda p, i: (0, 2 * n_steps + p))],
        out_specs=[pl.BlockSpec((QB, W), lambda p, i: (i, p))],
        out_shape=[jax.ShapeDtypeStruct((L, sb_width), F32)],
        scratch_shapes=[pltpu.VMEM((G, 2 * QB, LANES), F32), pltpu.VMEM((G, 2 * QB, LANES), F32),
                        pltpu.VMEM((G, 3, 2 * QB, 2 * QB), F32), pltpu.VMEM((G, 2, 2 * QB, 2 * QB), BF16),
                        pltpu.VMEM((2, 2 * QB, 2 * QB), BF16), pltpu.VMEM((2 * QB, 2 * QB), F32)],
        args=(qkv, qkv, qkv))
    return out, job_out


def _attn_bwd(qkv, o, dmixed, sb_width, name, job=None):
    L = qkv.shape[0]
    QB = QUERY_BLOCK
    nb = L // QB
    n_pairs = sb_width // LANES
    G = _pairs_per_step(n_pairs, 2)
    W = G * LANES
    scale = 1.0 / math.sqrt(HEAD_DIM)

    def body(q_ref, k_ref, v_ref, o_ref, do_ref, dq_ref, dk_ref, dv_ref,
             dq_acc, dk_acc, dv_acc, ce_ref, cr_ref, dtot_ref, f32_ref, bf16_ref, tri_ref, mask_ref):
        i = pl.program_id(1)

        @pl.when(i == 0)
        def _():
            dk_acc[...] = jnp.zeros_like(dk_acc)
            dv_acc[...] = jnp.zeros_like(dv_acc)
            _fill_attn_consts(tri_ref, mask_ref)

        q2s = [_stack_heads(q_ref[:, _lanes(g)], scale) for g in range(G)]
        do2s = [_stack_heads(do_ref[:, _lanes(g)].astype(BF16)) for g in range(G)]
        ones = jnp.ones((LANES, LANES), BF16)
        for g in range(G):
            ov = o_ref[:, _lanes(g)]
            dtot_ref[g] = _dot2(do2s[g].astype(F32) * jnp.concatenate([ov, ov], axis=0), ones)
        dq_acc[...] = jnp.zeros_like(dq_acc)
        ce_ref[...] = jnp.zeros_like(ce_ref)
        cr_ref[...] = jnp.zeros_like(cr_ref)

        def step(g, j, older, masked):
            n = older + QB
            wide = n // LANES
            q2, do2 = q2s[g], do2s[g]
            start = pl.multiple_of(j * QB, QB)
            kb = k_ref[pl.ds(start, n), _lanes(g)]
            vb = v_ref[pl.ds(start, n), _lanes(g)]
            zs, as_, bs, betas, gs = (f32_ref.at[g, t, :, :n] for t in range(5))
            hi, lo, wb = (bf16_ref.at[g, t, :, :n] for t in range(3))
            nt = (((1,), (1,)), ((), ()))
            zs[...] = lax.dot_general(q2, kb, nt, preferred_element_type=F32)
            gs[...] = lax.dot_general(do2, vb, nt, preferred_element_type=F32)
            yield
            for rows in _row_chunks():
                z = zs[rows, :]
                e = jnp.exp(-jnp.abs(z))
                sp = jnp.log(1.0 + e)
                b = jnp.minimum(-z, 0.0) - sp
                if masked:
                    b = jnp.where(_valid(mask_ref, rows, older),b, 0.0)
                rinv = 1.0 / (1.0 + e)
                as_[rows, :] = jnp.minimum(z, 0.0) - sp
                bs[rows, :] = b
                betas[rows, :] = jnp.where(z >= 0.0, rinv, e * rinv)
                _split_to(b, hi, lo, rows)
            yield
            tri = tri_ref[0, :n, :n]
            zs[...] = (jnp.dot(hi[...], tri, preferred_element_type=F32)
                       + jnp.dot(lo[...], tri, preferred_element_type=F32))
            yield
            for rows in _row_chunks():
                excl = zs[rows, :]
                total = _row_total(excl[:, :LANES] + bs[rows, :LANES])
                if not masked:
                    excl = excl + jnp.tile(ce_ref[g, rows, :], (1, wide))
                ce_ref[g, rows, :] += total
                w = jnp.exp(as_[rows, :] + excl)
                if masked:
                    w = jnp.where(_valid(mask_ref, rows, older),w, 0.0)
                wb[rows, :] = w.astype(BF16)
                gw = w * gs[rows, :]
                gs[rows, :] = gw
                _split_to(gw, hi, lo, rows)
            yield
            tri = tri_ref[1, :n, :n]
            zs[...] = (jnp.dot(hi[...], tri, preferred_element_type=F32)
                       + jnp.dot(lo[...], tri, preferred_element_type=F32))
            yield
            for rows in _row_chunks():
                rinc = zs[rows, :]
                total = _row_total(rinc[:, :LANES])
                if not masked:
                    rinc = rinc + jnp.tile(cr_ref[g, rows, :], (1, wide))
                cr_ref[g, rows, :] += total
                beta = betas[rows, :]
                dz = gs[rows, :] * (1.0 - beta) - beta * (jnp.tile(dtot_ref[g, rows, :], (1, wide)) - rinc)
                if masked:
                    dz = jnp.where(_valid(mask_ref, rows, older),dz, 0.0)
                hi[rows, :] = dz.astype(BF16)
            yield
            dzb = hi[...]
            dq_acc[g] += jnp.dot(dzb, kb, preferred_element_type=F32)
            dk_acc[pl.ds(start, n), _lanes(g)] += lax.dot_general(
                dzb, q2, (((0,), (0,)), ((), ())), preferred_element_type=F32)
            dv_acc[pl.ds(start, n), _lanes(g)] += lax.dot_general(
                wb[...], do2, (((0,), (0,)), ((), ())), preferred_element_type=F32)

        @pl.when(i == 0)
        def _():
            _in_lockstep([step(g, 0, 0, True) for g in range(G)])

        @pl.when(i > 0)
        def _():
            _in_lockstep([step(g, i - 1, QB, True) for g in range(G)])
            for g in range(G):
                _sweep_older(i, lambda j, _, g=g: _in_lockstep([step(g, j, 0, False)]), ce_ref.at[g], 1)

        for g in range(G):
            dq_ref[:, _lanes(g)] = (_unstack_heads(dq_acc[g]) * scale).astype(dq_ref.dtype)

        @pl.when(i == nb - 1)
        def _():
            dk_ref[...] = dk_acc[...].astype(dk_ref.dtype)
            dv_ref[...] = dv_acc[...].astype(dv_ref.dtype)

    n_steps = n_pairs // G
    blk = pl.BlockSpec((QB, W), lambda p, i: (i, p))
    col = pl.BlockSpec((L, W), lambda p, i: (0, p))
    return _call_with_job(
        body, job, name=name, grid=(n_steps, nb),
        in_specs=[blk,
                  pl.BlockSpec((L, W), lambda p, i: (0, n_steps + p)),
                  pl.BlockSpec((L, W), lambda p, i: (0, 2 * n_steps + p)),
                  blk, blk],
        out_specs=[blk, col, col],
        out_shape=[jax.ShapeDtypeStruct((L, sb_width), BF16)] * 3,
        scratch_shapes=[pltpu.VMEM((G, 2 * QB, LANES), F32), pltpu.VMEM((L, W), F32),
                        pltpu.VMEM((L, W), F32), pltpu.VMEM((G, 2 * QB, LANES), F32),
                        pltpu.VMEM((G, 2 * QB, LANES), F32), pltpu.VMEM((G, 2 * QB, LANES), F32),
                        pltpu.VMEM((G, 5, 2 * QB, 2 * QB), F32), pltpu.VMEM((G, 3, 2 * QB, 2 * QB), BF16),
                        pltpu.VMEM((2, 2 * QB, 2 * QB), BF16), pltpu.VMEM((2 * QB, 2 * QB), F32)],
        args=(qkv, qkv, qkv, o, dmixed))


def _conv_tile(L):
    return _pick(L, (384, 128))


def _glu(x, C):
    return x[:, :C] * _sigmoid(x[:, C:])


CONV_CHUNK = 32
SHIFT_TAIL = 24


def _fill_shifted(src_ref, dst_ref):
    n = dst_ref.shape[1]
    for r in range(1, 8):
        dst_ref[r - 1] = src_ref[r:r + n, :]


def _rows_at(src_ref, shifted_ref, start, n):
    q, r = divmod(start, 8)
    if r == 0:
        return src_ref[start:start + n, :]
    return shifted_ref[r - 1, 8 * q:8 * q + n, :]


def _conv_fwd(cacg, dw_w, dw_b, ln_g, ln_b, name, job=None):
    L, C2 = cacg.shape
    C = C2 // 2
    T = _conv_tile(L)
    H = CONV_HALO
    K = dw_w.shape[0]
    CH = CONV_CHUNK

    def body(x_ref, prev_ref, w_ref, b_ref, g_ref, beta_ref, o_ref, y_ref, u_ref, us_ref):
        i = pl.program_id(0)
        u_ref[0:H, :] = jnp.where(i > 0, _glu(prev_ref[...], C), 0.0)
        u_ref[H:, :] = _glu(x_ref[...], C)
        _fill_shifted(u_ref, us_ref)
        for c0 in range(0, T, CH):
            y = jnp.broadcast_to(b_ref[...], (CH, C))
            for k in range(K):
                y = y + w_ref[k:k + 1, :] * _rows_at(u_ref, us_ref, c0 + H - (K - 1) + k, CH)
            y_ref[c0:c0 + CH, :] = y
            mu = jnp.mean(y, axis=-1, keepdims=True)
            yc = y - mu
            rstd = lax.rsqrt(jnp.mean(yc * yc, axis=-1, keepdims=True) + EPS)
            ln = yc * rstd * g_ref[...] + beta_ref[...]
            o_ref[c0:c0 + CH, :] = (ln * _sigmoid(ln)).astype(o_ref.dtype)

    vec = pl.BlockSpec((1, C), lambda i: (0, 0))
    tile = pl.BlockSpec((T, C), lambda i: (i, 0))
    (out, y), arrived = _call_with_job(
        body, job, name=name, grid=(L // T,),
        in_specs=[pl.BlockSpec((T, C2), lambda i: (i, 0)),
                  pl.BlockSpec((H, C2), lambda i: (jnp.maximum(i * (T // H) - 1, 0), 0)),
                  pl.BlockSpec((K, C), lambda i: (0, 0)), vec, vec, vec],
        out_specs=[tile, tile],
        out_shape=[jax.ShapeDtypeStruct((L, C), BF16), jax.ShapeDtypeStruct((L, C), F32)],
        scratch_shapes=[pltpu.VMEM((T + H, C), F32), pltpu.VMEM((7, T + SHIFT_TAIL, C), F32)],
        args=(cacg, cacg, dw_w, dw_b, ln_g, ln_b))
    return out, y, arrived


def _conv_bwd(cacg, y, dmixed, dw_w, ln_g, ln_b, name, job=None):
    L, C2 = cacg.shape
    C = C2 // 2
    T = _conv_tile(L)
    H = CONV_HALO
    K = dw_w.shape[0]
    nt = L // T
    TE = T + H

    CH = CONV_CHUNK

    def body(x_ref, prev_ref, y_ref, ynext_ref, d_ref, dnext_ref, w_ref, g_ref, beta_ref,
             dca_ref, dcg_ref, dwt_ref, db_ref, dg_ref, dbeta_ref, u_ref, us_ref, dy_ref, dys_ref):
        i = pl.program_id(0)
        last = i == nt - 1

        @pl.when(i == 0)
        def _():
            dwt_ref[...] = jnp.zeros_like(dwt_ref)
            db_ref[...] = jnp.zeros_like(db_ref)
            dg_ref[...] = jnp.zeros_like(dg_ref)
            dbeta_ref[...] = jnp.zeros_like(dbeta_ref)

        u_ref[0:H, :] = jnp.where(i > 0, _glu(prev_ref[...], C), 0.0)
        u_ref[H:, :] = _glu(x_ref[...], C)
        _fill_shifted(u_ref, us_ref)
        dg_acc = jnp.zeros((1, C), F32)
        dbeta_acc = jnp.zeros((1, C), F32)
        db_acc = jnp.zeros((1, C), F32)
        for c0 in range(0, TE, CH):
            y = y_ref[c0:c0 + CH, :] if c0 < T else ynext_ref[c0 - T:c0 - T + CH, :]
            mu = jnp.mean(y, axis=-1, keepdims=True)
            yc = y - mu
            rstd = lax.rsqrt(jnp.mean(yc * yc, axis=-1, keepdims=True) + EPS)
            yh = yc * rstd
            ln = yh * g_ref[...] + beta_ref[...]
            s = _sigmoid(ln)
            dout = d_ref[c0:c0 + CH, :] if c0 < T else jnp.where(last, 0.0, dnext_ref[c0 - T:c0 - T + CH, :])
            dln = dout * (s * (1.0 + ln * (1.0 - s)))
            dyh = dln * g_ref[...]
            dy = rstd * (dyh - jnp.mean(dyh, axis=-1, keepdims=True)
                         - yh * jnp.mean(dyh * yh, axis=-1, keepdims=True))
            dy_ref[c0:c0 + CH, :] = dy
            if c0 < T:
                dg_acc = dg_acc + jnp.sum(dln * yh, axis=0, keepdims=True)
                dbeta_acc = dbeta_acc + jnp.sum(dln, axis=0, keepdims=True)
                db_acc = db_acc + jnp.sum(dy, axis=0, keepdims=True)
        dg_ref[...] += dg_acc
        dbeta_ref[...] += dbeta_acc
        db_ref[...] += db_acc
        _fill_shifted(dy_ref, dys_ref)
        for k in range(K):
            dwt_ref[k:k + 1, :] += jnp.sum(
                dy_ref[0:T, :] * _rows_at(u_ref, us_ref, H - (K - 1) + k, T), axis=0, keepdims=True)
        for c0 in range(0, T, CH):
            du = jnp.zeros((CH, C), F32)
            for k in range(K):
                du = du + w_ref[k:k + 1, :] * _rows_at(dy_ref, dys_ref, c0 + (K - 1) - k, CH)
            x = x_ref[c0:c0 + CH, :]
            sg = _sigmoid(x[:, C:])
            dca_ref[c0:c0 + CH, :] = (du * sg).astype(dca_ref.dtype)
            dcg_ref[c0:c0 + CH, :] = (du * x[:, :C] * sg * (1.0 - sg)).astype(dcg_ref.dtype)

    nh = L // H
    vec = pl.BlockSpec((1, C), lambda i: (0, 0))
    row = pl.BlockSpec((T, C), lambda i: (i, 0))
    after = lambda i: jnp.minimum((i + 1) * (T // H), nh - 1)
    return _call_with_job(
        body, job, name=name, grid=(nt,),
        in_specs=[pl.BlockSpec((T, C2), lambda i: (i, 0)),
                  pl.BlockSpec((H, C2), lambda i: (jnp.maximum(i * (T // H) - 1, 0), 0)),
                  row, pl.BlockSpec((H, C), lambda i: (after(i), 0)),
                  pl.BlockSpec((T, C), lambda i: (i, 1)), pl.BlockSpec((H, C), lambda i: (after(i), 1)),
                  pl.BlockSpec((K, C), lambda i: (0, 0)), vec, vec],
        out_specs=[row, row, pl.BlockSpec((H, C), lambda i: (0, 0)), vec, vec, vec],
        out_shape=[jax.ShapeDtypeStruct((L, C), BF16), jax.ShapeDtypeStruct((L, C), BF16),
                   jax.ShapeDtypeStruct((H, C), F32), jax.ShapeDtypeStruct((1, C), F32),
                   jax.ShapeDtypeStruct((1, C), F32), jax.ShapeDtypeStruct((1, C), F32)],
        scratch_shapes=[pltpu.VMEM((T + H, C), F32), pltpu.VMEM((7, T + SHIFT_TAIL, C), F32),
                        pltpu.VMEM((TE, C), F32), pltpu.VMEM((7, T + SHIFT_TAIL, C), F32)],
        args=(cacg, cacg, y, y, dmixed, dmixed, dw_w, ln_g, ln_b))


def _local_step(h0, target, n_meta, seq, norms, conv_p, wts, final_g, gather_behind, reducer):
    mix_g, ffn_g = norms
    dw_w, dw_b, ln_g, ln_b = conv_p
    depth = mix_g.shape[0]
    C = dw_b.shape[-1]
    sbw = (wts["w_in"][0].shape[-1] - 2 * C) // 3
    assert sbw == C, "the mixer halves must have equal width"
    row = lambda a, i: a[i][None, :]

    h = h0
    saved = []
    for i in range(depth):
        hn, proj_qkv, cacg = _norm_in_proj(h, row(mix_g, i), wts["w_in"][i], 3 * sbw, f"in_proj_{i}")
        def hosting(kind):
            job, keys = gather_behind.get((kind, i), (None, ()))

            def sink(arrived):
                for (wname, wl), arr in zip(keys, arrived):
                    wts[wname][wl] = arr

            return job, sink

        job, sink = hosting("attn")
        attn, arrived = _attn_fwd(proj_qkv, sbw, f"attn_fwd_{i}", job)
        sink(arrived)
        job, sink = hosting("conv")
        conv, conv_y, arrived = _conv_fwd(cacg, dw_w[i], row(dw_b, i), row(ln_g, i), row(ln_b, i),
                                          f"conv_fwd_{i}", job)
        sink(arrived)
        h_mid = _mm_nn([(attn, wts["w_out"][i], 0, 0), (conv, wts["w_out"][i], 0, 1)], F32, f"out_proj_{i}",
                       residual=h)
        job, sink = hosting("ffn")
        (hn2, g, u, act), arrived = _ffn_up(h_mid, row(ffn_g, i), wts["w_gate_t"][i], wts["w_up_t"][i],
                                            f"ffn_up_{i}", job)
        sink(arrived)
        job, sink = hosting("down")
        h_out = _mm_nn([(act, wts["w_down"][i], 0, 0)], F32, f"down_{i}", residual=h_mid, job=job)
        if job is not None:
            h_out, arrived = h_out
            sink(arrived)
        saved.append((h, hn, proj_qkv, cacg, attn, conv, conv_y, h_mid, hn2, g, u, act))
        h = h_out

    loss, dh, d_final_g = _loss_head(h, final_g[None, :], target, n_meta, seq, "loss_head")

    grads = {k: [None] * depth for k in ("mix_g", "ffn_g", "dw_w", "dw_b", "ln_g", "ln_b")}
    for i in reversed(range(depth)):
        h_in, hn, proj_qkv, cacg, attn, conv, conv_y, h_mid, hn2, g, u, act = saved[i]
        big = {}
        dg, du = _ffn_down_bwd(dh, wts["w_down"][i], g, u, f"ffn_down_bwd_{i}")
        big["w_down"] = _mm_tn(act, dh, f"dw_down_{i}", col_sharded=False)
        big["w_gate_t"] = _mm_tn(dg, hn2, f"dw_gate_{i}", col_sharded=False)
        big["w_up_t"] = _mm_tn(du, hn2, f"dw_up_{i}", col_sharded=False)
        sib_job, sib_sink = reducer.to_sibling(i, big)
        dhn2, arrived = _mm_nn([(dg, wts["w_gate_t"][i], 0, 0), (du, wts["w_up_t"][i], 0, 0)], F32, f"d_hn2_{i}",
                               job=sib_job)
        sib_sink(arrived)
        dh, d_ffn = _rmsnorm_bwd(h_mid, row(ffn_g, i), dhn2, dh, f"ffn_norm_bwd_{i}")
        dmixed = _mm_nt([(dh, wts["w_out"][i], 0, 0)], F32, f"d_mixed_{i}")
        dw_out = jnp.concatenate([_mm_tn(attn, dh, f"dw_out_attn_{i}", col_sharded=False, chips=2),
                                  _mm_tn(conv, dh, f"dw_out_conv_{i}", col_sharded=False, chips=2)], axis=0)
        sib_job, sib_sink = reducer.to_sibling(i, {"w_out": dw_out})
        x_job, x_sink = reducer.take()
        (dq, dk, dv), arrived = _attn_bwd(proj_qkv, attn, dmixed, sbw, f"attn_bwd_{i}",
                                          _merge_jobs([sib_job, x_job]))
        sib_sink(arrived[:len(sib_job.out_shapes)])
        x_sink(arrived[len(sib_job.out_shapes):])
        x_job, x_sink = reducer.take()
        (dca, dcg, d_dw, d_b, d_lg, d_lb), arrived = _conv_bwd(
            cacg, conv_y, dmixed, dw_w[i], row(ln_g, i), row(ln_b, i), f"conv_bwd_{i}", x_job)
        x_sink(arrived)
        grads["dw_w"][i] = d_dw
        grads["dw_b"][i], grads["ln_g"][i], grads["ln_b"][i] = d_b, d_lg, d_lb
        dproj = jnp.concatenate([dq, dk, dv, dca, dcg], axis=1)
        dhn = _mm_nt([(dproj, wts["w_in"][i], 0, 0)], F32, f"d_hn_{i}")
        reducer.ready(i, {"w_in": _mm_tn(hn, dproj, f"dw_in_{i}", col_sharded=True)})
        if i == 0:
            x_job, x_sink = reducer.take()
            dh, d_mix, arrived = _rmsnorm_bwd(h_in, row(mix_g, i), dhn, dh, f"mix_norm_bwd_{i}", x_job)
            x_sink(arrived)
        else:
            dh, d_mix = _rmsnorm_bwd(h_in, row(mix_g, i), dhn, dh, f"mix_norm_bwd_{i}")
        grads["mix_g"][i], grads["ffn_g"][i] = d_mix, d_ffn
    grads["final_g"] = d_final_g
    return loss, dh, grads


ANY = pl.BlockSpec(memory_space=pl.ANY)


def _position():
    return lax.axis_index("x"), lax.axis_index("y"), lax.axis_index("c")


def _chip_at(x, y, k):
    return (1 - x if k & 2 else x), (1 - y if k & 1 else y)


def _half_rows(ref, half, rows, base=0):
    start = pl.multiple_of(base + half * rows, 8)
    lead = (slice(None),) * (len(ref.shape) - 2)
    return ref.at[(*lead, pl.ds(start, rows), slice(None))]


def _gather_job(fulls, shard_shapes, col_sharded):
    n = len(fulls)

    def tools(f_refs, send_sems, recv_sems):
        def block(wi, chip, half):
            _, R, C = shard_shapes[wi]
            if col_sharded[wi]:
                cols = pl.ds(pl.multiple_of(chip * C, LANES), C)
                return f_refs[wi].at[:, pl.ds(pl.multiple_of(half * (R // 2), 8), R // 2), cols]
            return _half_rows(f_refs[wi], half, R // 2, base=chip * R)

        def copy(wi, slot, blk, to):
            return pltpu.make_async_remote_copy(
                src_ref=blk, dst_ref=blk, send_sem=send_sems.at[6 * wi + slot],
                recv_sem=recv_sems.at[6 * wi + slot], device_id=to, device_id_type=MESH)

        return block, copy

    def start(_, f_refs, send_sems, recv_sems):
        block, copy = tools(f_refs, send_sems, recv_sems)
        x, y, c = _position()
        me = 2 * x + y
        for wi in range(n):
            for k in (1, 2, 3):
                copy(wi, k - 1, block(wi, me, c), (*_chip_at(x, y, k), c)).start()

    def finish(_, f_refs, send_sems, recv_sems):
        block, copy = tools(f_refs, send_sems, recv_sems)
        x, y, c = _position()
        me = 2 * x + y
        for wi in range(n):
            for k in (1, 2, 3):
                landed = block(wi, me ^ k, c)
                copy(wi, k - 1, landed, (x, y, c)).wait_recv()
                copy(wi, 2 + k, landed, (x, y, 1 - c)).start()
        for wi in range(n):
            for k in (1, 2, 3):
                copy(wi, 2 + k, block(wi, me ^ k, 1 - c), (x, y, c)).wait_recv()
        for wi in range(n):
            for k in (1, 2, 3):
                copy(wi, k - 1, block(wi, me, c), (x, y, c)).wait_send()
                copy(wi, 2 + k, block(wi, me ^ k, c), (x, y, c)).wait_send()

    return _CommJob(fulls, [jax.ShapeDtypeStruct(f.shape, f.dtype) for f in fulls], {i: i for i in range(n)},
                    6 * n, start, finish)


def _place_shard(w, layer, chip, col_sharded, dtype, name):
    _, R, C = w.shape
    tr = _pick(R, (256, 352, 128, 48))
    nr = R // tr

    def body(chip_ref, w_ref, o_ref):
        o_ref[...] = w_ref[...].astype(dtype)

    if col_sharded:
        shape = (1, R, N_CHIPS * C)
        out_spec = pl.BlockSpec((None, tr, C), lambda r, chip_ref: (0, r, chip_ref[0]))
    else:
        shape = (1, N_CHIPS * R, C)
        out_spec = pl.BlockSpec((None, tr, C), lambda r, chip_ref: (0, chip_ref[0] * nr + r, 0))
    grid_spec = pltpu.PrefetchScalarGridSpec(
        num_scalar_prefetch=1, grid=(nr,),
        in_specs=[pl.BlockSpec((None, tr, C), lambda r, chip_ref: (layer, r, 0))], out_specs=out_spec)
    return pl.pallas_call(
        body, name=name, grid_spec=grid_spec, out_shape=jax.ShapeDtypeStruct(shape, dtype),
        compiler_params=_params(("parallel",)),
    )(chip, w)


def _sibling_job(grads):
    n = len(grads)

    def copies(g_refs, l_refs, send_sems, recv_sems):
        x, y, c = _position()
        return [pltpu.make_async_remote_copy(
            src_ref=_half_rows(g_refs[wi], 1 - c, grads[wi].shape[1] // 2), dst_ref=l_refs[wi],
            send_sem=send_sems.at[wi], recv_sem=recv_sems.at[wi],
            device_id=(x, y, 1 - c), device_id_type=MESH) for wi in range(n)]

    def start(*refs):
        for cp in copies(*refs):
            cp.start()

    def finish(*refs):
        for cp in copies(*refs):
            cp.wait()

    outs = [jax.ShapeDtypeStruct((g.shape[0], g.shape[1] // 2, g.shape[2]), g.dtype) for g in grads]
    return _CommJob(grads, outs, {}, n, start, finish)


def _chip_sum(g, landed, core, name):
    _, R, C = g.shape
    hr = R // 2
    tr = _pick(hr, (256, 352, 128))
    nr = hr // tr

    def body(c_ref, g_ref, l_ref, o_ref):
        o_ref[...] = (g_ref[...] + l_ref[...]).astype(BF16)

    grid_spec = pltpu.PrefetchScalarGridSpec(
        num_scalar_prefetch=1, grid=(N_CHIPS, nr),
        in_specs=[pl.BlockSpec((None, tr, C), lambda j, r, c_ref: (j, c_ref[0] * nr + r, 0)),
                  pl.BlockSpec((None, tr, C), lambda j, r, c_ref: (j, r, 0))],
        out_specs=pl.BlockSpec((None, tr, C), lambda j, r, c_ref: (j, r, 0)))
    return pl.pallas_call(
        body, name=name, grid_spec=grid_spec, out_shape=jax.ShapeDtypeStruct((N_CHIPS, hr, C), BF16),
        compiler_params=_params(("parallel", "parallel")),
    )(core, g, landed)


def _across_job(parts):
    n = len(parts)

    def copy(p_refs, l_refs, send_sems, recv_sems, wi, k, to):
        x, y, _ = _position()
        me = 2 * x + y
        return pltpu.make_async_remote_copy(
            src_ref=p_refs[wi].at[me ^ k], dst_ref=l_refs[wi].at[me],
            send_sem=send_sems.at[3 * wi + k - 1], recv_sem=recv_sems.at[3 * wi + k - 1],
            device_id=to, device_id_type=MESH)

    def start(p_refs, l_refs, send_sems, recv_sems):
        x, y, c = _position()
        for wi in range(n):
            for k in (1, 2, 3):
                copy(p_refs, l_refs, send_sems, recv_sems, wi, k, (*_chip_at(x, y, k), c)).start()

    def finish(p_refs, l_refs, send_sems, recv_sems):
        x, y, c = _position()
        me = 2 * x + y
        for wi in range(n):
            for k in (1, 2, 3):
                slot = l_refs[wi].at[me ^ k]
                pltpu.make_async_remote_copy(
                    src_ref=slot, dst_ref=slot, send_sem=send_sems.at[3 * wi + k - 1],
                    recv_sem=recv_sems.at[3 * wi + k - 1], device_id=(x, y, c), device_id_type=MESH).wait_recv()
        for wi in range(n):
            for k in (1, 2, 3):
                copy(p_refs, l_refs, send_sems, recv_sems, wi, k, (x, y, c)).wait_send()

    return _CommJob(parts, [jax.ShapeDtypeStruct(p.shape, p.dtype) for p in parts], {}, 3 * n, start, finish)


class _Reducer:
    def __init__(self, core):
        self.core, self.parts, self.across, self.pending = core, {}, {}, []

    def to_sibling(self, layer, big):
        names = list(big)
        flat = [big[k] for k in names]

        def sink(landed):
            for k, g, la in zip(names, flat, landed):
                self.parts[k, layer] = _chip_sum(g, la, self.core, f"chip_sum_{k}_{layer}")
                self.pending.append((k, layer))

        return _sibling_job(flat), sink

    def ready(self, layer, big):
        job, sink = self.to_sibling(layer, big)
        sink(_run_job(job, f"grads_to_sibling_{next(iter(big))}_{layer}"))

    def take(self):
        keys, self.pending = self.pending, []
        if not keys:
            return None, lambda results: None

        def sink(results):
            self.across.update(zip(keys, results))

        return _across_job([self.parts[key] for key in keys]), sink


def _sum_chips(parts, landed, where, layer, depth, prev, name):
    _, hr, C = landed.shape
    tr = _pick(hr, (256, 352, 128))
    nr = hr // tr

    def body(*refs):
        own_ref, slots, o_ref = refs[1], refs[2:2 + N_CHIPS], refs[-1]
        chip = refs[0][0]
        total = None
        for q in range(N_CHIPS):
            term = jnp.where(chip == q, own_ref[...], slots[q][...]).astype(F32)
            total = term if total is None else total + term
        o_ref[...] = total

    def slot_spec(q):
        return pl.BlockSpec((None, tr, C), lambda r, w: (jnp.where(w[0] == q, (q + 1) % N_CHIPS, q), r, 0))

    in_specs = [pl.BlockSpec((None, tr, C), lambda r, w: (w[0], r, 0))] + [slot_spec(q) for q in range(N_CHIPS)]
    args = [where, parts] + [landed] * N_CHIPS
    aliases = {}
    if prev is not None:
        in_specs.append(ANY)
        args.append(prev)
        aliases = {len(args) - 1: 0}
    grid_spec = pltpu.PrefetchScalarGridSpec(
        num_scalar_prefetch=1, grid=(nr,), in_specs=in_specs,
        out_specs=pl.BlockSpec((None, tr, C), lambda r, w: (layer, w[1] * nr + r, 0)))
    return pl.pallas_call(
        body, name=name, grid_spec=grid_spec, out_shape=jax.ShapeDtypeStruct((depth, 2 * hr, C), F32),
        input_output_aliases=aliases, compiler_params=_params(("parallel",)),
    )(*args)


def _rs_join_halves(reduced):
    n = len(reduced)

    def body(*refs):
        o_refs = refs[n:2 * n]
        send_sems, recv_sems = refs[2 * n:]
        x, y, c = _position()
        sent = []
        for wi in range(n):
            hr = reduced[wi].shape[1] // 2
            mine = _half_rows(o_refs[wi], c, hr)
            cp = pltpu.make_async_remote_copy(
                src_ref=mine, dst_ref=mine, send_sem=send_sems.at[wi], recv_sem=recv_sems.at[wi],
                device_id=(x, y, 1 - c), device_id_type=MESH)
            cp.start()
            sent.append(cp)
        for wi in range(n):
            hr = reduced[wi].shape[1] // 2
            theirs = _half_rows(o_refs[wi], 1 - c, hr)
            pltpu.make_async_remote_copy(
                src_ref=theirs, dst_ref=theirs, send_sem=send_sems.at[wi], recv_sem=recv_sems.at[wi],
                device_id=(x, y, c), device_id_type=MESH).wait_recv()
        for cp in sent:
            cp.wait_send()

    return pl.pallas_call(
        body, name="grads_join_halves", out_shape=[jax.ShapeDtypeStruct(r.shape, r.dtype) for r in reduced],
        in_specs=[ANY] * n, out_specs=[ANY] * n, input_output_aliases={i: i for i in range(n)},
        scratch_shapes=[pltpu.SemaphoreType.DMA((n,)), pltpu.SemaphoreType.DMA((n,))],
    )(*reduced)


def _adam_math(w, g, m, v):
    m = ADAM_B1 * m + (1.0 - ADAM_B1) * g
    v = ADAM_B2 * v + (1.0 - ADAM_B2) * jnp.square(g)
    m_hat = m / (1.0 - ADAM_B1 ** ADAM_STEP)
    v_hat = v / (1.0 - ADAM_B2 ** ADAM_STEP)
    delta = -ADAM_LR * (m_hat / (jnp.sqrt(v_hat) + ADAM_EPS) + ADAM_WD * w)
    return delta, m, v


def _small_reduce_adam(vec_grads, dww_grads, final_grad, meta_grad, state):
    vec_names = list(vec_grads)
    depth = len(dww_grads)
    D = final_grad.shape[1]
    n_meta = meta_grad.shape[0]
    taps_pad, C = dww_grads[0].shape
    names = vec_names + ["final", "meta", "dww"]
    at, row0 = 0, {}
    for k in vec_names:
        row0[k] = at
        at += depth
    row0["final"] = at
    at = -(-(at + 1) // 8) * 8
    row0["meta"] = at
    at += -(-n_meta // 8) * 8
    row0["dww"] = at
    rows = at + depth * taps_pad
    lanes = max(D, C)
    n_g = len(vec_names) * depth + depth + 2

    def body(*refs):
        g_refs = refs[:n_g]
        st = refs[n_g:n_g + 3 * len(names)]
        outs = refs[n_g + 3 * len(names):n_g + 7 * len(names)]
        slab, land, send_sems, recv_sems = refs[n_g + 7 * len(names):]
        x, y, c = _position()
        me = 4 * x + 2 * y + c
        chip = 2 * x + y
        slab[...] = jnp.zeros_like(slab)
        it = iter(g_refs)
        for k in vec_names:
            for l in range(depth):
                g_ref = next(it)
                slab[row0[k] + l:row0[k] + l + 1, 0:g_ref.shape[1]] = g_ref[...]
        for l in range(depth):
            slab[row0["dww"] + l * taps_pad:row0["dww"] + (l + 1) * taps_pad, 0:C] = next(it)[...]
        slab[row0["final"]:row0["final"] + 1, 0:D] = next(it)[...]
        slab[row0["meta"]:row0["meta"] + n_meta, 0:D] = next(it)[...]
        sent = []
        for k in range(1, N_DEV):
            to = (1 - x if k & 4 else x, 1 - y if k & 2 else y, 1 - c if k & 1 else c)
            cp = pltpu.make_async_remote_copy(
                src_ref=slab, dst_ref=land.at[k], send_sem=send_sems.at[k - 1], recv_sem=recv_sems.at[k - 1],
                device_id=to, device_id_type=MESH)
            cp.start()
            sent.append(cp)
        land[0] = slab[...]
        for cp in sent:
            cp.wait_recv()
        for cp in sent:
            cp.wait_send()
        total = land[me]
        for e in range(1, N_DEV):
            total = total + land[me ^ e]
        slab[...] = total

        def mine(r0, n_rows, width):
            got = slab[r0:r0 + n_rows, 0:width]
            for j in range(1, N_CHIPS):
                got = jnp.where(chip == j, slab[r0:r0 + n_rows, j * width:(j + 1) * width], got)
            return got

        def update(i, g, index=()):
            w_ref, m_ref, v_ref = st[3 * i:3 * i + 3]
            o = outs[4 * i:4 * i + 4]
            at = index if index else Ellipsis
            res = (g,) + _adam_math(w_ref[at], g, m_ref[at], v_ref[at])
            for o_ref, val in zip(o, res):
                o_ref[at] = val

        for i, k in enumerate(vec_names):
            width = st[3 * i].shape[1]
            update(i, slab[row0[k]:row0[k] + depth, 0:width])
        base = len(vec_names)
        update(base, slab[row0["final"]:row0["final"] + 1, 0:D])
        update(base + 1, mine(row0["meta"], n_meta, D // N_CHIPS))
        taps = st[3 * (base + 2)].shape[1]
        for l in range(depth):
            update(base + 2, mine(row0["dww"] + l * taps_pad, taps, C // N_CHIPS), (l,))

    flat_g = [g for k in vec_names for g in vec_grads[k]] + list(dww_grads) + [final_grad, meta_grad]
    flat_state = [a for k in names for a in state[k]]
    vmem = pl.BlockSpec(memory_space=pltpu.VMEM)
    out_shape = [jax.ShapeDtypeStruct(state[k][0].shape, F32) for k in names for _ in range(4)]
    res = pl.pallas_call(
        body, name="small_reduce_adam", out_shape=out_shape,
        in_specs=[vmem] * (len(flat_g) + len(flat_state)), out_specs=[vmem] * len(out_shape),
        scratch_shapes=[pltpu.VMEM((rows, lanes), F32), pltpu.VMEM((N_DEV, rows, lanes), F32),
                        pltpu.SemaphoreType.DMA((N_DEV - 1,)), pltpu.SemaphoreType.DMA((N_DEV - 1,))],
    )(*flat_g, *flat_state)
    return {k: tuple(res[4 * i:4 * i + 4]) for i, k in enumerate(names)}


def _adam(w, g, m, v, name):
    def body(w_ref, g_ref, m_ref, v_ref, d_ref, nm_ref, nv_ref):
        d_ref[...], nm_ref[...], nv_ref[...] = _adam_math(w_ref[...], g_ref[...], m_ref[...], v_ref[...])

    if w.ndim == 3:
        lyr, R, C = w.shape
        tr = _pick(R, (256, 352, 128))
        blk = pl.BlockSpec((None, tr, C), lambda l, r: (l, r, 0))
        grid, sem = (lyr, R // tr), ("parallel", "parallel")
    else:
        blk = pl.BlockSpec(w.shape, lambda: (0, 0))
        grid, sem = (), None
    return pl.pallas_call(
        body, name=name, grid=grid, in_specs=[blk] * 4, out_specs=[blk] * 3,
        out_shape=[jax.ShapeDtypeStruct(w.shape, F32)] * 3, compiler_params=_params(sem),
    )(w, g, m, v)


def _rows(a, pad_to=8):
    r = a.reshape(-1, LANES)
    extra = (-r.shape[0]) % pad_to
    return jnp.pad(r, ((0, extra), (0, 0))) if extra else r


def _pack(arrays):
    return jnp.concatenate([_rows(a) for a in arrays], axis=0)


def _unpack(slab, shapes):
    out, at = [], 0
    for shp in shapes:
        nrow = math.prod(shp) // LANES
        out.append(slab[at:at + nrow].reshape(shp))
        at += nrow + (-nrow) % 8
    return out


BIG = ("w_in", "w_out", "w_gate_t", "w_up_t", "w_down")
BIG_COL_SHARDED = (True, False, False, False, False)
TRANSPOSED = {"w_gate_t": "w_gate", "w_up_t": "w_up"}


def kernel(x, meta_tokens, mix_norm_g, w_in, conv_dw_w, conv_dw_b, conv_ln_g, conv_ln_b, w_out, ffn_norm_g, w_gate, w_up, w_down, final_norm_g, loss_target, m_meta_tokens, m_mix_norm_g, m_w_in, m_conv_dw_w, m_conv_dw_b, m_conv_ln_g, m_conv_ln_b, m_w_out, m_ffn_norm_g, m_w_gate, m_w_up, m_w_down, m_final_norm_g, v_meta_tokens, v_mix_norm_g, v_w_in, v_conv_dw_w, v_conv_dw_b, v_conv_ln_g, v_conv_ln_b, v_w_out, v_ffn_norm_g, v_w_gate, v_w_up, v_w_down, v_final_norm_g):
    n_meta, seq = meta_tokens.shape[0], x.shape[1]
    D = x.shape[2]
    depth, taps, c_shard = conv_dw_w.shape
    C = conv_dw_b.shape[-1]
    chip = (2 * lax.axis_index("x") + lax.axis_index("y")).astype(jnp.int32)
    core = lax.axis_index("c").astype(jnp.int32).reshape(1)
    chip1 = chip.reshape(1)
    where = jnp.concatenate([chip1, core])
    big_w = dict(w_in=w_in, w_out=w_out, w_gate=w_gate, w_up=w_up, w_down=w_down)
    big_m = dict(w_in=m_w_in, w_out=m_w_out, w_gate=m_w_gate, w_up=m_w_up, w_down=m_w_down)
    big_v = dict(w_in=v_w_in, w_out=v_w_out, w_gate=v_w_gate, w_up=v_w_up, w_down=v_w_down)

    small_shard = _pack([conv_dw_w, meta_tokens])[None]
    to_send = {k: jnp.swapaxes(big_w[TRANSPOSED[k]], 1, 2) if k in TRANSPOSED else big_w[k] for k in BIG}
    col = dict(zip(BIG, BIG_COL_SHARDED))
    wts = {k: [_place_shard(to_send[k], l, chip1, col[k], BF16, f"place_{k}_{l}") for l in range(depth)] for k in BIG}
    small_placed = _place_shard(small_shard, 0, chip1, False, F32, "place_small")

    def gather_job(keys, extra=()):
        arrays = [wts[k][l] for k, l in keys] + list(extra)
        shapes = [(1,) + to_send[k].shape[1:] for k, _ in keys] + [(1,) + small_shard.shape[1:]] * len(extra)
        return _gather_job(arrays, shapes, [col[k] for k, _ in keys] + [False] * len(extra))

    first_keys = [("w_in", 0)]
    *first, small_full = _run_job(gather_job(first_keys, [small_placed]), "gather_first")
    for (k, l), arr in zip(first_keys, first):
        wts[k][l] = arr
    behind_keys = {("attn", 0): [("w_out", 0), ("w_gate_t", 0), ("w_up_t", 0)], ("conv", 0): [("w_down", 0)]}
    for l in range(1, depth):
        behind_keys["ffn", l - 1] = [("w_in", l), ("w_gate_t", l)]
        behind_keys["down", l - 1] = [("w_up_t", l)]
        behind_keys["attn", l] = [("w_out", l), ("w_down", l)]
    gather_behind = {host: (gather_job(keys), keys) for host, keys in behind_keys.items()}

    rows_shard = small_shard.shape[1]
    dw_full, meta_full = [], []
    for j in range(N_CHIPS):
        dwj, mj = _unpack(small_full[0, j * rows_shard:(j + 1) * rows_shard],
                          [conv_dw_w.shape, meta_tokens.shape])
        dw_full.append(dwj)
        meta_full.append(mj)
    dw_w_full = jnp.concatenate(dw_full, axis=2)
    meta = jnp.concatenate(meta_full, axis=1)

    L = n_meta + seq
    Lp = -(-L // QUERY_BLOCK) * QUERY_BLOCK
    h0 = jnp.concatenate([meta, x[0], jnp.zeros((Lp - L, D), F32)], axis=0)
    target = jnp.pad(loss_target[0], ((n_meta, Lp - L), (0, 0)))
    reducer = _Reducer(core)
    loss, dh0, grads = _local_step(h0, target, n_meta, seq, (mix_norm_g, ffn_norm_g),
                                   (dw_w_full, conv_dw_b, conv_ln_g, conv_ln_b), wts, final_norm_g,
                                   gather_behind, reducer)
    loss = lax.psum(loss[0, 0], ("x", "y", "c"))
    grad_x = dh0[n_meta:L][None]

    reduced = []
    for k in BIG:
        arr = None
        for l in range(depth):
            arr = _sum_chips(reducer.parts[k, l], reducer.across[k, l], where, l, depth, arr, f"sum_chips_{k}_{l}")
        reduced.append(arr)
    big_g = dict(zip(BIG, _rs_join_halves(reduced)))

    out_g, out_d, out_m, out_v = {}, {}, {}, {}
    for kk in BIG:
        k = TRANSPOSED.get(kk, kk)
        view = (lambda a: jnp.swapaxes(a, 1, 2)) if kk in TRANSPOSED else (lambda a: a)
        res = _adam(view(big_w[k]), big_g[kk], view(big_m[k]), view(big_v[k]), f"adam_{k}")
        out_g[k] = view(big_g[kk])
        out_d[k], out_m[k], out_v[k] = (view(a) for a in res)

    as_row = lambda a: a.reshape(1, -1)
    state = dict(mix_g=(mix_norm_g, m_mix_norm_g, v_mix_norm_g), ffn_g=(ffn_norm_g, m_ffn_norm_g, v_ffn_norm_g),
                 dw_b=(conv_dw_b, m_conv_dw_b, v_conv_dw_b), ln_g=(conv_ln_g, m_conv_ln_g, v_conv_ln_g),
                 ln_b=(conv_ln_b, m_conv_ln_b, v_conv_ln_b),
                 final=(as_row(final_norm_g), as_row(m_final_norm_g), as_row(v_final_norm_g)),
                 meta=(meta_tokens, m_meta_tokens, v_meta_tokens), dww=(conv_dw_w, m_conv_dw_w, v_conv_dw_w))
    vec_names = ("mix_g", "ffn_g", "dw_b", "ln_g", "ln_b")
    small = _small_reduce_adam({k: grads[k] for k in vec_names}, grads["dw_w"], grads["final_g"], dh0[:n_meta], state)
    out_name = dict(mix_g="mix_norm_g", ffn_g="ffn_norm_g", dw_b="conv_dw_b", ln_g="conv_ln_g", ln_b="conv_ln_b",
                    final="final_norm_g", meta="meta_tokens", dww="conv_dw_w")
    for k, res in small.items():
        if k == "final":
            res = tuple(a.reshape(-1) for a in res)
        out_g[out_name[k]], out_d[out_name[k]], out_m[out_name[k]], out_v[out_name[k]] = res

    order = ("meta_tokens", "mix_norm_g", "w_in", "conv_dw_w", "conv_dw_b", "conv_ln_g", "conv_ln_b", "w_out",
             "ffn_norm_g", "w_gate", "w_up", "w_down", "final_norm_g")
    return (loss, grad_x, *[out_g[k] for k in order], *[out_d[k] for k in order],
            *[out_m[k] for k in order], *[out_v[k] for k in order])
```

```python
import functools
import math

import jax
import jax.numpy as jnp
from jax import lax
from jax.experimental import pallas as pl
from jax.experimental.pallas import tpu as pltpu

F32 = jnp.float32
BF16 = jnp.bfloat16
MESH = pl.DeviceIdType.MESH

EPS = 1e-6
QUERY_BLOCK = 128
LANES = 128
HEAD_DIM = 64
LOG_STICK_FLOOR = -40.0
CONV_HALO = 32
N_CHIPS = 4
N_DEV = 8
VMEM_LIMIT = 56 * 1024 * 1024

ADAM_LR = 0.001
ADAM_B1 = 0.9
ADAM_B2 = 0.999
ADAM_EPS = 1e-08
ADAM_WD = 0.01
ADAM_STEP = 10


def _pick(n, prefs):
    for p in prefs:
        if n % p == 0:
            return p
    return n


def _params(sem=None):
    return pltpu.CompilerParams(dimension_semantics=sem, vmem_limit_bytes=VMEM_LIMIT)


def _sigmoid(x):
    return 1.0 / (1.0 + jnp.exp(-x))


def _rmsnorm_bwd(h, g, dy, dh_in, name, job=None):
    L, D = h.shape
    T = _pick(L, (384, 128))

    def body(h_ref, g_ref, dy_ref, dhin_ref, dh_ref, dg_ref):
        x = h_ref[...]
        dyv = dy_ref[...]
        r = lax.rsqrt(jnp.mean(x * x, axis=-1, keepdims=True) + EPS)
        xh = x * r
        dxh = dyv * g_ref[...]
        dh_ref[...] = dhin_ref[...] + r * (dxh - xh * jnp.mean(dxh * xh, axis=-1, keepdims=True))

        @pl.when(pl.program_id(0) == 0)
        def _():
            dg_ref[...] = jnp.zeros_like(dg_ref)

        dg_ref[...] += jnp.sum(dyv * xh, axis=0, keepdims=True)

    row = pl.BlockSpec((T, D), lambda i: (i, 0))
    vec = pl.BlockSpec((1, D), lambda i: (0, 0))
    (dh, dg), arrived = _call_with_job(
        body, job, name=name, grid=(L // T,),
        in_specs=[row, vec, row, row], out_specs=[row, vec],
        out_shape=[jax.ShapeDtypeStruct((L, D), F32), jax.ShapeDtypeStruct((1, D), F32)],
        scratch_shapes=[], args=(h, g, dy, dh_in))
    return (dh, dg) if job is None else (dh, dg, arrived)


def _loss_head(h, g, target, n_meta, seq, name):
    L, D = h.shape
    T = _pick(L, (384, 128))

    def body(h_ref, g_ref, t_ref, loss_ref, dh_ref, dg_ref):
        i = pl.program_id(0)
        x = h_ref[...]
        gv = g_ref[...]
        r = lax.rsqrt(jnp.mean(x * x, axis=-1, keepdims=True) + EPS)
        xh = x * r
        y = xh * gv
        rows = i * T + lax.broadcasted_iota(jnp.int32, (T, 1), 0)
        live = (rows >= n_meta) & (rows < n_meta + seq)
        diff = jnp.where(live, y - t_ref[...], 0.0)
        dyv = diff / D
        dxh = dyv * gv
        dh_ref[...] = r * (dxh - xh * jnp.mean(dxh * xh, axis=-1, keepdims=True))

        @pl.when(i == 0)
        def _():
            dg_ref[...] = jnp.zeros_like(dg_ref)
            loss_ref[...] = jnp.zeros_like(loss_ref)

        dg_ref[...] += jnp.sum(dyv * xh, axis=0, keepdims=True)
        per_row = jnp.mean(diff * diff, axis=-1, keepdims=True)
        loss_ref[...] += 0.5 * jnp.sum(per_row, axis=0, keepdims=True)

    row = pl.BlockSpec((T, D), lambda i: (i, 0))
    vec = pl.BlockSpec((1, D), lambda i: (0, 0))
    one = pl.BlockSpec((1, 1), lambda i: (0, 0))
    return pl.pallas_call(
        body, name=name, grid=(L // T,),
        in_specs=[row, vec, row], out_specs=[one, row, vec],
        out_shape=[jax.ShapeDtypeStruct((1, 1), F32), jax.ShapeDtypeStruct((L, D), F32),
                   jax.ShapeDtypeStruct((1, D), F32)],
        compiler_params=_params(("arbitrary",)),
    )(h, g, target)


def _ffn_tiles(M, F):
    return _pick(M, (352, 384, 128)), _pick(F, (1408, 512, 256, 128))


def _resident(shape):
    return pl.BlockSpec((None,) + tuple(shape[1:]), lambda *_: (0,) * len(shape), pipeline_mode=pl.Buffered(1))


def _norm_in_proj(h, g, w_in, n_qkv, name):
    M, D = h.shape
    N = w_in.shape[2]
    tm = _pick(M, (352, 384, 128))
    tn = _pick(math.gcd(n_qkv, N - n_qkv), (512, 256, 128))

    def body(h_ref, g_ref, w_ref, hn_ref, qkv_ref, rest_ref):
        x = h_ref[...]
        r = lax.rsqrt(jnp.mean(x * x, axis=-1, keepdims=True) + EPS)
        hv = (x * r * g_ref[...]).astype(BF16)
        hn_ref[...] = hv
        for c in range(0, N, tn):
            out = jnp.dot(hv, w_ref[:, c:c + tn], preferred_element_type=F32)
            if c < n_qkv:
                qkv_ref[:, c:c + tn] = out.astype(qkv_ref.dtype)
            else:
                rest_ref[:, c - n_qkv:c - n_qkv + tn] = out

    rows = pl.BlockSpec((tm, D), lambda i: (i, 0))
    return pl.pallas_call(
        body, name=name, grid=(M // tm,),
        in_specs=[rows, pl.BlockSpec((1, D), lambda i: (0, 0)), _resident(w_in.shape)],
        out_specs=[rows, pl.BlockSpec((tm, n_qkv), lambda i: (i, 0)), pl.BlockSpec((tm, N - n_qkv), lambda i: (i, 0))],
        out_shape=[jax.ShapeDtypeStruct((M, D), BF16), jax.ShapeDtypeStruct((M, n_qkv), BF16),
                   jax.ShapeDtypeStruct((M, N - n_qkv), F32)],
        compiler_params=_params(("parallel",)),
    )(h, g, w_in)


def _ffn_up(h, norm_g, w_gate_t, w_up_t, name, job=None):
    M, D = h.shape
    F = w_gate_t.shape[1]
    tm, tf = _ffn_tiles(M, F)
    nt = (((1,), (1,)), ((), ()))

    def body(h_ref, ng_ref, wg_ref, wu_ref, hn_ref, g_ref, u_ref, a_ref):
        x = h_ref[...]
        r = lax.rsqrt(jnp.mean(x * x, axis=-1, keepdims=True) + EPS)
        hv = (x * r * ng_ref[...]).astype(BF16)
        hn_ref[...] = hv
        for c in range(0, F, tf):
            gv = lax.dot_general(hv, wg_ref[c:c + tf, :], nt, preferred_element_type=F32)
            uv = lax.dot_general(hv, wu_ref[c:c + tf, :], nt, preferred_element_type=F32)
            g_ref[:, c:c + tf] = gv.astype(g_ref.dtype)
            u_ref[:, c:c + tf] = uv.astype(u_ref.dtype)
            a_ref[:, c:c + tf] = (gv * _sigmoid(gv) * uv).astype(a_ref.dtype)

    rows = pl.BlockSpec((tm, D), lambda i: (i, 0))
    wide = pl.BlockSpec((tm, F), lambda i: (i, 0))
    return _call_with_job(
        body, job, name=name, grid=(M // tm,),
        in_specs=[rows, pl.BlockSpec((1, D), lambda i: (0, 0)), _resident(w_gate_t.shape), _resident(w_up_t.shape)],
        out_specs=[rows, wide, wide, wide],
        out_shape=[jax.ShapeDtypeStruct((M, D), BF16), jax.ShapeDtypeStruct((M, F), BF16),
                   jax.ShapeDtypeStruct((M, F), BF16), jax.ShapeDtypeStruct((M, F), BF16)],
        scratch_shapes=[], args=(h, norm_g, w_gate_t, w_up_t))


def _ffn_down_bwd(dh, w_down, g, u, name):
    M, D = dh.shape
    F = g.shape[1]
    tm, tf = _ffn_tiles(M, F)

    def body(d_ref, w_ref, g_ref, u_ref, dg_ref, du_ref):
        dhv = d_ref[...].astype(BF16)
        for c in range(0, F, tf):
            dv = lax.dot_general(dhv, w_ref[c:c + tf, :], (((1,), (1,)), ((), ())), preferred_element_type=F32)
            gv = g_ref[:, c:c + tf].astype(F32)
            s = _sigmoid(gv)
            du_ref[:, c:c + tf] = (dv * (gv * s)).astype(du_ref.dtype)
            dg_ref[:, c:c + tf] = (dv * u_ref[:, c:c + tf].astype(F32)
                                   * (s * (1.0 + gv * (1.0 - s)))).astype(dg_ref.dtype)

    wide = pl.BlockSpec((tm, F), lambda i: (i, 0))
    return pl.pallas_call(
        body, name=name, grid=(M // tm,),
        in_specs=[pl.BlockSpec((tm, D), lambda i: (i, 0)), _resident(w_down.shape), wide, wide],
        out_specs=[wide, wide],
        out_shape=[jax.ShapeDtypeStruct((M, F), BF16), jax.ShapeDtypeStruct((M, F), BF16)],
        compiler_params=_params(("parallel",)),
    )(dh, w_down, g, u)


def _mm_rows(pairs, name, transposed=False, residual=None, job=None):
    M = pairs[0][0].shape[0]
    N = pairs[0][1].shape[1 if transposed else 2]
    tm = _pick(M, (352, 384, 128))
    tn = _pick(N, (512, 256, 128))
    n = len(pairs)

    def body(*refs):
        a_refs, w_refs = refs[:n], refs[n:2 * n]
        o_ref = refs[-1]
        lhs = [a_ref[...].astype(BF16) for a_ref in a_refs]
        for c in range(0, N, tn):
            acc = None
            for a, w_ref, (_, _, blk) in zip(lhs, w_refs, pairs):
                k = a.shape[1]
                if transposed:
                    d = lax.dot_general(a, w_ref[c:c + tn, blk * k:(blk + 1) * k], (((1,), (1,)), ((), ())),
                                        preferred_element_type=F32)
                else:
                    d = jnp.dot(a, w_ref[blk * k:(blk + 1) * k, c:c + tn], preferred_element_type=F32)
                acc = d if acc is None else acc + d
            if residual is not None:
                acc = acc + refs[2 * n][:, c:c + tn]
            o_ref[:, c:c + tn] = acc

    in_specs = [pl.BlockSpec((tm, a.shape[1]), lambda i: (i, 0)) for a, _, _ in pairs]
    in_specs += [_resident(w.shape) for _, w, _ in pairs]
    args = [p[0] for p in pairs] + [p[1] for p in pairs]
    wide = pl.BlockSpec((tm, N), lambda i: (i, 0))
    if residual is not None:
        in_specs.append(wide)
        args.append(residual)
    (out,), arrived = _call_with_job(
        body, job, name=name, grid=(M // tm,), in_specs=in_specs, out_specs=[wide],
        out_shape=[jax.ShapeDtypeStruct((M, N), F32)], scratch_shapes=[], args=args)
    return out if job is None else (out, arrived)


def _mm_tn(a, b, name, col_sharded, chips=N_CHIPS):
    M, K = a.shape
    N = b.shape[1]
    tm = _pick(M, (1056, 384, 128))
    tk = _pick(K, (1024, 1408, 512, 256, 128))
    tn = N // N_CHIPS if col_sharded else _pick(N, (1024, 512, 128))

    def body(a_ref, b_ref, o_ref):
        @pl.when(pl.program_id(2) == 0)
        def _():
            o_ref[...] = jnp.zeros_like(o_ref)

        o_ref[...] += lax.dot_general(a_ref[...].astype(BF16), b_ref[...].astype(BF16),
                                      (((0,), (0,)), ((), ())), preferred_element_type=F32)

    if col_sharded:
        out_shape = jax.ShapeDtypeStruct((N_CHIPS, K, tn), F32)
        out_spec = pl.BlockSpec((None, tk, tn), lambda k, j, m: (j, k, 0))
    else:
        out_shape = jax.ShapeDtypeStruct((K, N), F32)
        out_spec = pl.BlockSpec((tk, tn), lambda k, j, m: (k, j))
    out = pl.pallas_call(
        body, name=name, grid=(K // tk, N // tn, M // tm),
        in_specs=[pl.BlockSpec((tm, tk), lambda k, j, m: (m, k)), pl.BlockSpec((tm, tn), lambda k, j, m: (m, j))],
        out_specs=out_spec, out_shape=out_shape,
        compiler_params=_params(("parallel", "parallel", "arbitrary")),
    )(a, b)
    return out if col_sharded else out.reshape(chips, K // chips, N)


def _stack_heads(x, scale=None):
    lane = lax.broadcasted_iota(jnp.int32, x.shape, 1)
    zero = jnp.zeros_like(x)
    lo = jnp.where(lane < HEAD_DIM, x, zero)
    hi = jnp.where(lane < HEAD_DIM, zero, x)
    out = jnp.concatenate([lo, hi], axis=0)
    return out if scale is None else out * scale


def _unstack_heads(x2):
    qb = x2.shape[0] // 2
    lane = lax.broadcasted_iota(jnp.int32, (qb, LANES), 1)
    return jnp.where(lane < HEAD_DIM, x2[:qb], x2[qb:])


def _dot2(x, m):
    xh = x.astype(BF16)
    xl = (x - xh.astype(F32)).astype(BF16)
    return jnp.dot(xh, m, preferred_element_type=F32) + jnp.dot(xl, m, preferred_element_type=F32)


ATTN_CHUNK = 32


def _in_lockstep(staged):
    waiting, live = list(staged), []
    while waiting or live:
        if waiting:
            live.append(waiting.pop(0))
        for gen in list(live):
            if next(gen, StopIteration) is StopIteration:
                live.remove(gen)


def _row_chunks():
    return [slice(r, r + ATTN_CHUNK) for r in range(0, 2 * QUERY_BLOCK, ATTN_CHUNK)]


def _chunk_valid(rows, older):
    shape = (rows.stop - rows.start, older + QUERY_BLOCK)
    r = (rows.start + lax.broadcasted_iota(jnp.int32, shape, 0)) & (QUERY_BLOCK - 1)
    return lax.broadcasted_iota(jnp.int32, shape, 1) < r + older


def _split_to(x, hi_ref, lo_ref, rows):
    xh = x.astype(BF16)
    hi_ref[rows, :] = xh
    lo_ref[rows, :] = (x - xh.astype(F32)).astype(BF16)


def _triangle(keys, strict):
    r = lax.broadcasted_iota(jnp.int32, (keys, keys), 0)
    c = lax.broadcasted_iota(jnp.int32, (keys, keys), 1)
    return jnp.where((r > c) if strict else (r >= c), 1.0, 0.0).astype(BF16)


def _fill_attn_consts(tri_ref, mask_ref):
    n = 2 * QUERY_BLOCK
    tri_ref[0] = _triangle(n, True)
    tri_ref[1] = _triangle(n, False)
    mask_ref[...] = jnp.where(_chunk_valid(slice(0, n), QUERY_BLOCK), 1.0, 0.0)


def _valid(mask_ref, rows, older):
    cols = slice(0, 2 * QUERY_BLOCK) if older else slice(QUERY_BLOCK, 2 * QUERY_BLOCK)
    return mask_ref[rows, cols] > 0.5


def _row_total(first):
    lane = lax.broadcasted_iota(jnp.int32, first.shape, 1)
    total = jnp.sum(jnp.where(lane == 0, first, 0.0), axis=1, keepdims=True)
    return jnp.broadcast_to(total, first.shape)


def _pairs_per_step(n_pairs, most):
    return max(g for g in (1, 2, 4) if g <= most and n_pairs % g == 0)


def _lanes(g):
    return slice(g * LANES, (g + 1) * LANES)


def _sweep_older(i, step, carry_ref, first):
    def cond(state):
        n, live = state
        return jnp.logical_and(n < i, live)

    def older(state):
        n, _ = state
        step(i - 1 - n, False)
        return n + 1, jnp.max(carry_ref[...]) > LOG_STICK_FLOOR

    lax.while_loop(cond, older, (first, jnp.max(carry_ref[...]) > LOG_STICK_FLOOR))


class _CommJob:
    def __init__(self, inputs, out_shapes, aliases, n_sems, start, finish):
        self.inputs, self.out_shapes, self.aliases, self.n_sems = list(inputs), list(out_shapes), aliases, n_sems
        self.start, self.finish = start, finish


def _merge_jobs(jobs):
    jobs = [j for j in jobs if j is not None]
    if len(jobs) <= 1:
        return jobs[0] if jobs else None
    spans, aliases = [], {}
    i0 = o0 = s0 = 0
    for j in jobs:
        spans.append((i0, o0, s0))
        aliases.update({i0 + a: o0 + b for a, b in j.aliases.items()})
        i0, o0, s0 = i0 + len(j.inputs), o0 + len(j.out_shapes), s0 + j.n_sems

    def run(which):
        def go(ins, outs, send_sems, recv_sems):
            for j, (i, o, s) in zip(jobs, spans):
                getattr(j, which)(ins[i:i + len(j.inputs)], outs[o:o + len(j.out_shapes)],
                                  send_sems.at[pl.ds(s, j.n_sems)], recv_sems.at[pl.ds(s, j.n_sems)])
        return go

    return _CommJob([a for j in jobs for a in j.inputs], [s for j in jobs for s in j.out_shapes], aliases, s0,
                    run("start"), run("finish"))


def _call_with_job(core_body, job, *, name, grid, in_specs, out_specs, out_shape, scratch_shapes, args):
    sem = ("arbitrary",) * len(grid)
    if job is None:
        res = pl.pallas_call(core_body, name=name, grid=grid, in_specs=in_specs, out_specs=out_specs,
                             out_shape=out_shape, scratch_shapes=scratch_shapes, compiler_params=_params(sem))(*args)
        return list(res), []
    n_in, n_out, n_scr = len(in_specs), len(out_specs), len(scratch_shapes)
    m_in, m_out = len(job.inputs), len(job.out_shapes)

    def body(*refs):
        at = 0
        parts = []
        for count in (n_in, m_in, n_out, m_out, n_scr, 2):
            parts.append(refs[at:at + count])
            at += count
        ins, job_in, outs, job_outs, scratch, (send_sems, recv_sems) = parts
        first = functools.reduce(jnp.logical_and, [pl.program_id(a) == 0 for a in range(len(grid))])
        last = functools.reduce(jnp.logical_and, [pl.program_id(a) == grid[a] - 1 for a in range(len(grid))])

        @pl.when(first)
        def _():
            job.start(job_in, job_outs, send_sems, recv_sems)

        core_body(*ins, *outs, *scratch)

        @pl.when(last)
        def _():
            job.finish(job_in, job_outs, send_sems, recv_sems)

    res = pl.pallas_call(
        body, name=name, grid=grid, in_specs=list(in_specs) + [ANY] * m_in, out_specs=list(out_specs) + [ANY] * m_out,
        out_shape=list(out_shape) + job.out_shapes,
        input_output_aliases={n_in + a: n_out + b for a, b in job.aliases.items()},
        scratch_shapes=list(scratch_shapes) + [pltpu.SemaphoreType.DMA((job.n_sems,)), pltpu.SemaphoreType.DMA((job.n_sems,))],
        compiler_params=_params(sem),
    )(*args, *job.inputs)
    return list(res[:n_out]), list(res[n_out:])


def _run_job(job, name):
    m_in, m_out = len(job.inputs), len(job.out_shapes)

    def body(*refs):
        job_in, job_outs = refs[:m_in], refs[m_in:m_in + m_out]
        send_sems, recv_sems = refs[m_in + m_out:]
        job.start(job_in, job_outs, send_sems, recv_sems)
        job.finish(job_in, job_outs, send_sems, recv_sems)

    return list(pl.pallas_call(
        body, name=name, in_specs=[ANY] * m_in, out_specs=[ANY] * m_out, out_shape=job.out_shapes,
        input_output_aliases=dict(job.aliases),
        scratch_shapes=[pltpu.SemaphoreType.DMA((job.n_sems,)), pltpu.SemaphoreType.DMA((job.n_sems,))],
    )(*job.inputs))


def _attn_fwd(qkv, sb_width, name, job=None):
    L = qkv.shape[0]
    QB = QUERY_BLOCK
    nb = L // QB
    n_pairs = sb_width // LANES
    G = _pairs_per_step(n_pairs, 4)
    W = G * LANES
    scale = 1.0 / math.sqrt(HEAD_DIM)

    def body(q_ref, k_ref, v_ref, o_ref, acc_ref, carry_ref, f32_ref, bf16_ref, tri_ref, mask_ref):
        i = pl.program_id(1)

        @pl.when(i == 0)
        def _():
            _fill_attn_consts(tri_ref, mask_ref)

        q2 = [_stack_heads(q_ref[:, _lanes(g)], scale) for g in range(G)]
        acc_ref[...] = jnp.zeros_like(acc_ref)
        carry_ref[...] = jnp.zeros_like(carry_ref)

        def step(g, j, older, masked):
            n = older + QB
            start = pl.multiple_of(j * QB, QB)
            kb = k_ref[pl.ds(start, n), _lanes(g)]
            vb = v_ref[pl.ds(start, n), _lanes(g)]
            zs, as_, bs = (f32_ref.at[g, t, :, :n] for t in range(3))
            hi, lo = (bf16_ref.at[g, t, :, :n] for t in range(2))
            zs[...] = lax.dot_general(q2[g], kb, (((1,), (1,)), ((), ())), preferred_element_type=F32)
            yield
            for rows in _row_chunks():
                z = zs[rows, :]
                sp = jnp.log(1.0 + jnp.exp(-jnp.abs(z)))
                b = jnp.minimum(-z, 0.0) - sp
                if masked:
                    b = jnp.where(_valid(mask_ref, rows, older),b, 0.0)
                as_[rows, :] = jnp.minimum(z, 0.0) - sp
                bs[rows, :] = b
                _split_to(b, hi, lo, rows)
            yield
            tri = tri_ref[0, :n, :n]
            zs[...] = (jnp.dot(hi[...], tri, preferred_element_type=F32)
                       + jnp.dot(lo[...], tri, preferred_element_type=F32))
            yield
            for rows in _row_chunks():
                excl = zs[rows, :]
                total = _row_total(excl[:, :LANES] + bs[rows, :LANES])
                if not masked:
                    excl = excl + jnp.tile(carry_ref[g, rows, :], (1, n // LANES))
                carry_ref[g, rows, :] += total
                w = jnp.exp(as_[rows, :] + excl)
                if masked:
                    w = jnp.where(_valid(mask_ref, rows, older),w, 0.0)
                _split_to(w, hi, lo, rows)
            yield
            acc_ref[g] += (jnp.dot(hi[...], vb, preferred_element_type=F32)
                           + jnp.dot(lo[...], vb, preferred_element_type=F32))

        @pl.when(i == 0)
        def _():
            _in_lockstep([step(g, 0, 0, True) for g in range(G)])

        @pl.when(i > 0)
        def _():
            _in_lockstep([step(g, i - 1, QB, True) for g in range(G)])
            for g in range(G):
                _sweep_older(i, lambda j, _, g=g: _in_lockstep([step(g, j, 0, False)]), carry_ref.at[g], 1)

        for g in range(G):
            o_ref[:, _lanes(g)] = _unstack_heads(acc_ref[g])

    n_steps = n_pairs // G
    (out,), job_out = _call_with_job(
        body, job, name=name, grid=(n_steps, nb),
        in_specs=[pl.BlockSpec((QB, W), lambda p, i: (i, p)),
                  pl.BlockSpec((L, W), lambda p, i: (0, n_steps + p)),
                  pl.BlockSpec((L, W), lambda p, i: (0, 2 * n_steps + p))],
        out_specs=[pl.BlockSpec((QB, W), lambda p, i: (i, p))],
        out_shape=[jax.ShapeDtypeStruct((L, sb_width), F32)],
        scratch_shapes=[pltpu.VMEM((G, 2 * QB, LANES), F32), pltpu.VMEM((G, 2 * QB, LANES), F32),
                        pltpu.VMEM((G, 3, 2 * QB, 2 * QB), F32), pltpu.VMEM((G, 2, 2 * QB, 2 * QB), BF16),
                        pltpu.VMEM((2, 2 * QB, 2 * QB), BF16), pltpu.VMEM((2 * QB, 2 * QB), F32)],
        args=(qkv, qkv, qkv))
    return out, job_out


def _attn_bwd(qkv, o, dmixed, sb_width, name, job=None):
    L = qkv.shape[0]
    QB = QUERY_BLOCK
    nb = L // QB
    n_pairs = sb_width // LANES
    G = _pairs_per_step(n_pairs, 2)
    W = G * LANES
    scale = 1.0 / math.sqrt(HEAD_DIM)

    def body(q_ref, k_ref, v_ref, o_ref, do_ref, dq_ref, dk_ref, dv_ref,
             dq_acc, dk_acc, dv_acc, ce_ref, cr_ref, dtot_ref, f32_ref, bf16_ref, tri_ref, mask_ref):
        i = pl.program_id(1)

        @pl.when(i == 0)
        def _():
            dk_acc[...] = jnp.zeros_like(dk_acc)
            dv_acc[...] = jnp.zeros_like(dv_acc)
            _fill_attn_consts(tri_ref, mask_ref)

        q2s = [_stack_heads(q_ref[:, _lanes(g)], scale) for g in range(G)]
        do2s = [_stack_heads(do_ref[:, _lanes(g)].astype(BF16)) for g in range(G)]
        ones = jnp.ones((LANES, LANES), BF16)
        for g in range(G):
            ov = o_ref[:, _lanes(g)]
            dtot_ref[g] = _dot2(do2s[g].astype(F32) * jnp.concatenate([ov, ov], axis=0), ones)
        dq_acc[...] = jnp.zeros_like(dq_acc)
        ce_ref[...] = jnp.zeros_like(ce_ref)
        cr_ref[...] = jnp.zeros_like(cr_ref)

        def step(g, j, older, masked):
            n = older + QB
            wide = n // LANES
            q2, do2 = q2s[g], do2s[g]
            start = pl.multiple_of(j * QB, QB)
            kb = k_ref[pl.ds(start, n), _lanes(g)]
            vb = v_ref[pl.ds(start, n), _lanes(g)]
            zs, as_, bs, betas, gs = (f32_ref.at[g, t, :, :n] for t in range(5))
            hi, lo, wb = (bf16_ref.at[g, t, :, :n] for t in range(3))
            nt = (((1,), (1,)), ((), ()))
            zs[...] = lax.dot_general(q2, kb, nt, preferred_element_type=F32)
            gs[...] = lax.dot_general(do2, vb, nt, preferred_element_type=F32)
            yield
            for rows in _row_chunks():
                z = zs[rows, :]
                e = jnp.exp(-jnp.abs(z))
                sp = jnp.log(1.0 + e)
                b = jnp.minimum(-z, 0.0) - sp
                if masked:
                    b = jnp.where(_valid(mask_ref, rows, older),b, 0.0)
                rinv = 1.0 / (1.0 + e)
                as_[rows, :] = jnp.minimum(z, 0.0) - sp
                bs[rows, :] = b
                betas[rows, :] = jnp.where(z >= 0.0, rinv, e * rinv)
                _split_to(b, hi, lo, rows)
            yield
            tri = tri_ref[0, :n, :n]
            zs[...] = (jnp.dot(hi[...], tri, preferred_element_type=F32)
                       + jnp.dot(lo[...], tri, preferred_element_type=F32))
            yield
            for rows in _row_chunks():
                excl = zs[rows, :]
                total = _row_total(excl[:, :LANES] + bs[rows, :LANES])
                if not masked:
                    excl = excl + jnp.tile(ce_ref[g, rows, :], (1, wide))
                ce_ref[g, rows, :] += total
                w = jnp.exp(as_[rows, :] + excl)
                if masked:
                    w = jnp.where(_valid(mask_ref, rows, older),w, 0.0)
                wb[rows, :] = w.astype(BF16)
                gw = w * gs[rows, :]
                gs[rows, :] = gw
                _split_to(gw, hi, lo, rows)
            yield
            tri = tri_ref[1, :n, :n]
            zs[...] = (jnp.dot(hi[...], tri, preferred_element_type=F32)
                       + jnp.dot(lo[...], tri, preferred_element_type=F32))
            yield
            for rows in _row_chunks():
                rinc = zs[rows, :]
                total = _row_total(rinc[:, :LANES])
                if not masked:
                    rinc = rinc + jnp.tile(cr_ref[g, rows, :], (1, wide))
                cr_ref[g, rows, :] += total
                beta = betas[rows, :]
                dz = gs[rows, :] * (1.0 - beta) - beta * (jnp.tile(dtot_ref[g, rows, :], (1, wide)) - rinc)
                if masked:
                    dz = jnp.where(_valid(mask_ref, rows, older),dz, 0.0)
                hi[rows, :] = dz.astype(BF16)
            yield
            dzb = hi[...]
            dq_acc[g] += jnp.dot(dzb, kb, preferred_element_type=F32)
            dk_acc[pl.ds(start, n), _lanes(g)] += lax.dot_general(
                dzb, q2, (((0,), (0,)), ((), ())), preferred_element_type=F32)
            dv_acc[pl.ds(start, n), _lanes(g)] += lax.dot_general(
                wb[...], do2, (((0,), (0,)), ((), ())), preferred_element_type=F32)

        @pl.when(i == 0)
        def _():
            _in_lockstep([step(g, 0, 0, True) for g in range(G)])

        @pl.when(i > 0)
        def _():
            _in_lockstep([step(g, i - 1, QB, True) for g in range(G)])
            for g in range(G):
                _sweep_older(i, lambda j, _, g=g: _in_lockstep([step(g, j, 0, False)]), ce_ref.at[g], 1)

        for g in range(G):
            dq_ref[:, _lanes(g)] = (_unstack_heads(dq_acc[g]) * scale).astype(dq_ref.dtype)

        @pl.when(i == nb - 1)
        def _():
            dk_ref[...] = dk_acc[...].astype(dk_ref.dtype)
            dv_ref[...] = dv_acc[...].astype(dv_ref.dtype)

    n_steps = n_pairs // G
    blk = pl.BlockSpec((QB, W), lambda p, i: (i, p))
    col = pl.BlockSpec((L, W), lambda p, i: (0, p))
    return _call_with_job(
        body, job, name=name, grid=(n_steps, nb),
        in_specs=[blk,
                  pl.BlockSpec((L, W), lambda p, i: (0, n_steps + p)),
                  pl.BlockSpec((L, W), lambda p, i: (0, 2 * n_steps + p)),
                  blk, blk],
        out_specs=[blk, col, col],
        out_shape=[jax.ShapeDtypeStruct((L, sb_width), BF16)] * 3,
        scratch_shapes=[pltpu.VMEM((G, 2 * QB, LANES), F32), pltpu.VMEM((L, W), F32),
                        pltpu.VMEM((L, W), F32), pltpu.VMEM((G, 2 * QB, LANES), F32),
                        pltpu.VMEM((G, 2 * QB, LANES), F32), pltpu.VMEM((G, 2 * QB, LANES), F32),
                        pltpu.VMEM((G, 5, 2 * QB, 2 * QB), F32), pltpu.VMEM((G, 3, 2 * QB, 2 * QB), BF16),
                        pltpu.VMEM((2, 2 * QB, 2 * QB), BF16), pltpu.VMEM((2 * QB, 2 * QB), F32)],
        args=(qkv, qkv, qkv, o, dmixed))


def _conv_tile(L):
    return _pick(L, (384, 128))


def _glu(x, C):
    return x[:, :C] * _sigmoid(x[:, C:])


CONV_CHUNK = 32
SHIFT_TAIL = 24


def _fill_shifted(src_ref, dst_ref):
    n = dst_ref.shape[1]
    for r in range(1, 8):
        dst_ref[r - 1] = src_ref[r:r + n, :]


def _rows_at(src_ref, shifted_ref, start, n):
    q, r = divmod(start, 8)
    if r == 0:
        return src_ref[start:start + n, :]
    return shifted_ref[r - 1, 8 * q:8 * q + n, :]


def _conv_fwd(cacg, dw_w, dw_b, ln_g, ln_b, name, job=None):
    L, C2 = cacg.shape
    C = C2 // 2
    T = _conv_tile(L)
    H = CONV_HALO
    K = dw_w.shape[0]
    CH = CONV_CHUNK

    def body(x_ref, prev_ref, w_ref, b_ref, g_ref, beta_ref, o_ref, y_ref, u_ref, us_ref):
        i = pl.program_id(0)
        u_ref[0:H, :] = jnp.where(i > 0, _glu(prev_ref[...], C), 0.0)
        u_ref[H:, :] = _glu(x_ref[...], C)
        _fill_shifted(u_ref, us_ref)
        for c0 in range(0, T, CH):
            y = jnp.broadcast_to(b_ref[...], (CH, C))
            for k in range(K):
                y = y + w_ref[k:k + 1, :] * _rows_at(u_ref, us_ref, c0 + H - (K - 1) + k, CH)
            y_ref[c0:c0 + CH, :] = y
            mu = jnp.mean(y, axis=-1, keepdims=True)
            yc = y - mu
            rstd = lax.rsqrt(jnp.mean(yc * yc, axis=-1, keepdims=True) + EPS)
            ln = yc * rstd * g_ref[...] + beta_ref[...]
            o_ref[c0:c0 + CH, :] = (ln * _sigmoid(ln)).astype(o_ref.dtype)

    vec = pl.BlockSpec((1, C), lambda i: (0, 0))
    tile = pl.BlockSpec((T, C), lambda i: (i, 0))
    (out, y), arrived = _call_with_job(
        body, job, name=name, grid=(L // T,),
        in_specs=[pl.BlockSpec((T, C2), lambda i: (i, 0)),
                  pl.BlockSpec((H, C2), lambda i: (jnp.maximum(i * (T // H) - 1, 0), 0)),
                  pl.BlockSpec((K, C), lambda i: (0, 0)), vec, vec, vec],
        out_specs=[tile, tile],
        out_shape=[jax.ShapeDtypeStruct((L, C), BF16), jax.ShapeDtypeStruct((L, C), F32)],
        scratch_shapes=[pltpu.VMEM((T + H, C), F32), pltpu.VMEM((7, T + SHIFT_TAIL, C), F32)],
        args=(cacg, cacg, dw_w, dw_b, ln_g, ln_b))
    return out, y, arrived


def _conv_bwd(cacg, y, dmixed, dw_w, ln_g, ln_b, name, job=None):
    L, C2 = cacg.shape
    C = C2 // 2
    T = _conv_tile(L)
    H = CONV_HALO
    K = dw_w.shape[0]
    nt = L // T
    TE = T + H

    CH = CONV_CHUNK

    def body(x_ref, prev_ref, y_ref, ynext_ref, d_ref, dnext_ref, w_ref, g_ref, beta_ref,
             dca_ref, dcg_ref, dwt_ref, db_ref, dg_ref, dbeta_ref, u_ref, us_ref, dy_ref, dys_ref):
        i = pl.program_id(0)
        last = i == nt - 1

        @pl.when(i == 0)
        def _():
            dwt_ref[...] = jnp.zeros_like(dwt_ref)
            db_ref[...] = jnp.zeros_like(db_ref)
            dg_ref[...] = jnp.zeros_like(dg_ref)
            dbeta_ref[...] = jnp.zeros_like(dbeta_ref)

        u_ref[0:H, :] = jnp.where(i > 0, _glu(prev_ref[...], C), 0.0)
        u_ref[H:, :] = _glu(x_ref[...], C)
        _fill_shifted(u_ref, us_ref)
        dg_acc = jnp.zeros((1, C), F32)
        dbeta_acc = jnp.zeros((1, C), F32)
        db_acc = jnp.zeros((1, C), F32)
        for c0 in range(0, TE, CH):
            y = y_ref[c0:c0 + CH, :] if c0 < T else ynext_ref[c0 - T:c0 - T + CH, :]
            mu = jnp.mean(y, axis=-1, keepdims=True)
            yc = y - mu
            rstd = lax.rsqrt(jnp.mean(yc * yc, axis=-1, keepdims=True) + EPS)
            yh = yc * rstd
            ln = yh * g_ref[...] + beta_ref[...]
            s = _sigmoid(ln)
            dout = d_ref[c0:c0 + CH, :] if c0 < T else jnp.where(last, 0.0, dnext_ref[c0 - T:c0 - T + CH, :])
            dln = dout * (s * (1.0 + ln * (1.0 - s)))
            dyh = dln * g_ref[...]
            dy = rstd * (dyh - jnp.mean(dyh, axis=-1, keepdims=True)
                         - yh * jnp.mean(dyh * yh, axis=-1, keepdims=True))
            dy_ref[c0:c0 + CH, :] = dy
            if c0 < T:
                dg_acc = dg_acc + jnp.sum(dln * yh, axis=0, keepdims=True)
                dbeta_acc = dbeta_acc + jnp.sum(dln, axis=0, keepdims=True)
                db_acc = db_acc + jnp.sum(dy, axis=0, keepdims=True)
        dg_ref[...] += dg_acc
        dbeta_ref[...] += dbeta_acc
        db_ref[...] += db_acc
        _fill_shifted(dy_ref, dys_ref)
        for k in range(K):
            dwt_ref[k:k + 1, :] += jnp.sum(
                dy_ref[0:T, :] * _rows_at(u_ref, us_ref, H - (K - 1) + k, T), axis=0, keepdims=True)
        for c0 in range(0, T, CH):
            du = jnp.zeros((CH, C), F32)
            for k in range(K):
                du = du + w_ref[k:k + 1, :] * _rows_at(dy_ref, dys_ref, c0 + (K - 1) - k, CH)
            x = x_ref[c0:c0 + CH, :]
            sg = _sigmoid(x[:, C:])
            dca_ref[c0:c0 + CH, :] = (du * sg).astype(dca_ref.dtype)
            dcg_ref[c0:c0 + CH, :] = (du * x[:, :C] * sg * (1.0 - sg)).astype(dcg_ref.dtype)

    nh = L // H
    vec = pl.BlockSpec((1, C), lambda i: (0, 0))
    row = pl.BlockSpec((T, C), lambda i: (i, 0))
    after = lambda i: jnp.minimum((i + 1) * (T // H), nh - 1)
    return _call_with_job(
        body, job, name=name, grid=(nt,),
        in_specs=[pl.BlockSpec((T, C2), lambda i: (i, 0)),
                  pl.BlockSpec((H, C2), lambda i: (jnp.maximum(i * (T // H) - 1, 0), 0)),
                  row, pl.BlockSpec((H, C), lambda i: (after(i), 0)),
                  pl.BlockSpec((T, C), lambda i: (i, 1)), pl.BlockSpec((H, C), lambda i: (after(i), 1)),
                  pl.BlockSpec((K, C), lambda i: (0, 0)), vec, vec],
        out_specs=[row, row, pl.BlockSpec((H, C), lambda i: (0, 0)), vec, vec, vec],
        out_shape=[jax.ShapeDtypeStruct((L, C), BF16), jax.ShapeDtypeStruct((L, C), BF16),
                   jax.ShapeDtypeStruct((H, C), F32), jax.ShapeDtypeStruct((1, C), F32),
                   jax.ShapeDtypeStruct((1, C), F32), jax.ShapeDtypeStruct((1, C), F32)],
        scratch_shapes=[pltpu.VMEM((T + H, C), F32), pltpu.VMEM((7, T + SHIFT_TAIL, C), F32),
                        pltpu.VMEM((TE, C), F32), pltpu.VMEM((7, T + SHIFT_TAIL, C), F32)],
        args=(cacg, cacg, y, y, dmixed, dmixed, dw_w, ln_g, ln_b))


def _local_step(h0, target, n_meta, seq, norms, conv_p, wts, final_g, gather_behind, reducer):
    mix_g, ffn_g = norms
    dw_w, dw_b, ln_g, ln_b = conv_p
    depth = mix_g.shape[0]
    C = dw_b.shape[-1]
    sbw = (wts["w_in"][0].shape[-1] - 2 * C) // 3
    assert sbw == C, "the mixer halves must have equal width"
    row = lambda a, i: a[i][None, :]

    h = h0
    saved = []
    for i in range(depth):
        hn, proj_qkv, cacg = _norm_in_proj(h, row(mix_g, i), wts["w_in"][i], 3 * sbw, f"in_proj_{i}")
        def hosting(kind):
            job, keys = gather_behind.get((kind, i), (None, ()))

            def sink(arrived):
                for (wname, wl), arr in zip(keys, arrived):
                    wts[wname][wl] = arr

            return job, sink

        job, sink = hosting("attn")
        attn, arrived = _attn_fwd(proj_qkv, sbw, f"attn_fwd_{i}", job)
        sink(arrived)
        job, sink = hosting("conv")
        conv, conv_y, arrived = _conv_fwd(cacg, dw_w[i], row(dw_b, i), row(ln_g, i), row(ln_b, i),
                                          f"conv_fwd_{i}", job)
        sink(arrived)
        h_mid = _mm_rows([(attn, wts["w_out"][i], 0), (conv, wts["w_out"][i], 1)], f"out_proj_{i}", residual=h)
        job, sink = hosting("ffn")
        (hn2, g, u, act), arrived = _ffn_up(h_mid, row(ffn_g, i), wts["w_gate_t"][i], wts["w_up_t"][i],
                                            f"ffn_up_{i}", job)
        sink(arrived)
        job, sink = hosting("down")
        h_out = _mm_rows([(act, wts["w_down"][i], 0)], f"down_{i}", residual=h_mid, job=job)
        if job is not None:
            h_out, arrived = h_out
            sink(arrived)
        saved.append((h, hn, proj_qkv, cacg, attn, conv, conv_y, h_mid, hn2, g, u, act))
        h = h_out

    loss, dh, d_final_g = _loss_head(h, final_g[None, :], target, n_meta, seq, "loss_head")

    grads = {k: [None] * depth for k in ("mix_g", "ffn_g", "dw_w", "dw_b", "ln_g", "ln_b")}
    for i in reversed(range(depth)):
        h_in, hn, proj_qkv, cacg, attn, conv, conv_y, h_mid, hn2, g, u, act = saved[i]
        big = {}
        dg, du = _ffn_down_bwd(dh, wts["w_down"][i], g, u, f"ffn_down_bwd_{i}")
        big["w_down"] = _mm_tn(act, dh, f"dw_down_{i}", col_sharded=False)
        big["w_gate_t"] = _mm_tn(dg, hn2, f"dw_gate_{i}", col_sharded=False)
        big["w_up_t"] = _mm_tn(du, hn2, f"dw_up_{i}", col_sharded=False)
        sib_job, sib_sink = reducer.to_sibling(i, big)
        dhn2, arrived = _mm_rows([(dg, wts["w_gate_t"][i], 0), (du, wts["w_up_t"][i], 0)], f"d_hn2_{i}", job=sib_job)
        sib_sink(arrived)
        dh, d_ffn = _rmsnorm_bwd(h_mid, row(ffn_g, i), dhn2, dh, f"ffn_norm_bwd_{i}")
        dmixed = _mm_rows([(dh, wts["w_out"][i], 0)], f"d_mixed_{i}", transposed=True)
        dw_out = jnp.concatenate([_mm_tn(attn, dh, f"dw_out_attn_{i}", col_sharded=False, chips=2),
                                  _mm_tn(conv, dh, f"dw_out_conv_{i}", col_sharded=False, chips=2)], axis=0)
        sib_job, sib_sink = reducer.to_sibling(i, {"w_out": dw_out})
        x_job, x_sink = reducer.take()
        (dq, dk, dv), arrived = _attn_bwd(proj_qkv, attn, dmixed, sbw, f"attn_bwd_{i}",
                                          _merge_jobs([sib_job, x_job]))
        sib_sink(arrived[:len(sib_job.out_shapes)])
        x_sink(arrived[len(sib_job.out_shapes):])
        x_job, x_sink = reducer.take()
        (dca, dcg, d_dw, d_b, d_lg, d_lb), arrived = _conv_bwd(
            cacg, conv_y, dmixed, dw_w[i], row(ln_g, i), row(ln_b, i), f"conv_bwd_{i}", x_job)
        x_sink(arrived)
        grads["dw_w"][i] = d_dw
        grads["dw_b"][i], grads["ln_g"][i], grads["ln_b"][i] = d_b, d_lg, d_lb
        dproj = jnp.concatenate([dq, dk, dv, dca, dcg], axis=1)
        dhn = _mm_rows([(dproj, wts["w_in"][i], 0)], f"d_hn_{i}", transposed=True)
        reducer.ready(i, {"w_in": _mm_tn(hn, dproj, f"dw_in_{i}", col_sharded=True)})
        if i == 0:
            x_job, x_sink = reducer.take()
            dh, d_mix, arrived = _rmsnorm_bwd(h_in, row(mix_g, i), dhn, dh, f"mix_norm_bwd_{i}", x_job)
            x_sink(arrived)
        else:
            dh, d_mix = _rmsnorm_bwd(h_in, row(mix_g, i), dhn, dh, f"mix_norm_bwd_{i}")
        grads["mix_g"][i], grads["ffn_g"][i] = d_mix, d_ffn
    grads["final_g"] = d_final_g
    return loss, dh, grads


ANY = pl.BlockSpec(memory_space=pl.ANY)


def _position():
    return lax.axis_index("x"), lax.axis_index("y"), lax.axis_index("c")


def _chip_at(x, y, k):
    return (1 - x if k & 2 else x), (1 - y if k & 1 else y)


def _half_rows(ref, half, rows, base=0):
    start = pl.multiple_of(base + half * rows, 8)
    lead = (slice(None),) * (len(ref.shape) - 2)
    return ref.at[(*lead, pl.ds(start, rows), slice(None))]


def _gather_job(fulls, shard_shapes, col_sharded):
    n = len(fulls)

    def tools(f_refs, send_sems, recv_sems):
        def block(wi, chip, half):
            _, R, C = shard_shapes[wi]
            if col_sharded[wi]:
                cols = pl.ds(pl.multiple_of(chip * C, LANES), C)
                return f_refs[wi].at[:, pl.ds(pl.multiple_of(half * (R // 2), 8), R // 2), cols]
            return _half_rows(f_refs[wi], half, R // 2, base=chip * R)

        def copy(wi, slot, blk, to):
            return pltpu.make_async_remote_copy(
                src_ref=blk, dst_ref=blk, send_sem=send_sems.at[6 * wi + slot],
                recv_sem=recv_sems.at[6 * wi + slot], device_id=to, device_id_type=MESH)

        return block, copy

    def start(_, f_refs, send_sems, recv_sems):
        block, copy = tools(f_refs, send_sems, recv_sems)
        x, y, c = _position()
        me = 2 * x + y
        for wi in range(n):
            for k in (1, 2, 3):
                copy(wi, k - 1, block(wi, me, c), (*_chip_at(x, y, k), c)).start()

    def finish(_, f_refs, send_sems, recv_sems):
        block, copy = tools(f_refs, send_sems, recv_sems)
        x, y, c = _position()
        me = 2 * x + y
        for wi in range(n):
            for k in (1, 2, 3):
                landed = block(wi, me ^ k, c)
                copy(wi, k - 1, landed, (x, y, c)).wait_recv()
                copy(wi, 2 + k, landed, (x, y, 1 - c)).start()
        for wi in range(n):
            for k in (1, 2, 3):
                copy(wi, 2 + k, block(wi, me ^ k, 1 - c), (x, y, c)).wait_recv()
        for wi in range(n):
            for k in (1, 2, 3):
                copy(wi, k - 1, block(wi, me, c), (x, y, c)).wait_send()
                copy(wi, 2 + k, block(wi, me ^ k, c), (x, y, c)).wait_send()

    return _CommJob(fulls, [jax.ShapeDtypeStruct(f.shape, f.dtype) for f in fulls], {i: i for i in range(n)},
                    6 * n, start, finish)


def _place_shard(w, layer, chip, col_sharded, dtype, name):
    _, R, C = w.shape
    tr = _pick(R, (256, 352, 128, 48))
    nr = R // tr

    def body(chip_ref, w_ref, o_ref):
        o_ref[...] = w_ref[...].astype(dtype)

    if col_sharded:
        shape = (1, R, N_CHIPS * C)
        out_spec = pl.BlockSpec((None, tr, C), lambda r, chip_ref: (0, r, chip_ref[0]))
    else:
        shape = (1, N_CHIPS * R, C)
        out_spec = pl.BlockSpec((None, tr, C), lambda r, chip_ref: (0, chip_ref[0] * nr + r, 0))
    grid_spec = pltpu.PrefetchScalarGridSpec(
        num_scalar_prefetch=1, grid=(nr,),
        in_specs=[pl.BlockSpec((None, tr, C), lambda r, chip_ref: (layer, r, 0))], out_specs=out_spec)
    return pl.pallas_call(
        body, name=name, grid_spec=grid_spec, out_shape=jax.ShapeDtypeStruct(shape, dtype),
        compiler_params=_params(("parallel",)),
    )(chip, w)


def _sibling_job(grads):
    n = len(grads)

    def copies(g_refs, l_refs, send_sems, recv_sems):
        x, y, c = _position()
        return [pltpu.make_async_remote_copy(
            src_ref=_half_rows(g_refs[wi], 1 - c, grads[wi].shape[1] // 2), dst_ref=l_refs[wi],
            send_sem=send_sems.at[wi], recv_sem=recv_sems.at[wi],
            device_id=(x, y, 1 - c), device_id_type=MESH) for wi in range(n)]

    def start(*refs):
        for cp in copies(*refs):
            cp.start()

    def finish(*refs):
        for cp in copies(*refs):
            cp.wait()

    outs = [jax.ShapeDtypeStruct((g.shape[0], g.shape[1] // 2, g.shape[2]), g.dtype) for g in grads]
    return _CommJob(grads, outs, {}, n, start, finish)


def _chip_sum(g, landed, core, name):
    _, R, C = g.shape
    hr = R // 2
    tr = _pick(hr, (256, 352, 128))
    nr = hr // tr

    def body(c_ref, g_ref, l_ref, o_ref):
        o_ref[...] = (g_ref[...] + l_ref[...]).astype(BF16)

    grid_spec = pltpu.PrefetchScalarGridSpec(
        num_scalar_prefetch=1, grid=(N_CHIPS, nr),
        in_specs=[pl.BlockSpec((None, tr, C), lambda j, r, c_ref: (j, c_ref[0] * nr + r, 0)),
                  pl.BlockSpec((None, tr, C), lambda j, r, c_ref: (j, r, 0))],
        out_specs=pl.BlockSpec((None, tr, C), lambda j, r, c_ref: (j, r, 0)))
    return pl.pallas_call(
        body, name=name, grid_spec=grid_spec, out_shape=jax.ShapeDtypeStruct((N_CHIPS, hr, C), BF16),
        compiler_params=_params(("parallel", "parallel")),
    )(core, g, landed)


def _across_job(parts):
    n = len(parts)

    def copy(p_refs, l_refs, send_sems, recv_sems, wi, k, to):
        x, y, _ = _position()
        me = 2 * x + y
        return pltpu.make_async_remote_copy(
            src_ref=p_refs[wi].at[me ^ k], dst_ref=l_refs[wi].at[me],
            send_sem=send_sems.at[3 * wi + k - 1], recv_sem=recv_sems.at[3 * wi + k - 1],
            device_id=to, device_id_type=MESH)

    def start(p_refs, l_refs, send_sems, recv_sems):
        x, y, c = _position()
        for wi in range(n):
            for k in (1, 2, 3):
                copy(p_refs, l_refs, send_sems, recv_sems, wi, k, (*_chip_at(x, y, k), c)).start()

    def finish(p_refs, l_refs, send_sems, recv_sems):
        x, y, c = _position()
        me = 2 * x + y
        for wi in range(n):
            for k in (1, 2, 3):
                slot = l_refs[wi].at[me ^ k]
                pltpu.make_async_remote_copy(
                    src_ref=slot, dst_ref=slot, send_sem=send_sems.at[3 * wi + k - 1],
                    recv_sem=recv_sems.at[3 * wi + k - 1], device_id=(x, y, c), device_id_type=MESH).wait_recv()
        for wi in range(n):
            for k in (1, 2, 3):
                copy(p_refs, l_refs, send_sems, recv_sems, wi, k, (x, y, c)).wait_send()

    return _CommJob(parts, [jax.ShapeDtypeStruct(p.shape, p.dtype) for p in parts], {}, 3 * n, start, finish)


class _Reducer:
    def __init__(self, core):
        self.core, self.parts, self.across, self.pending = core, {}, {}, []

    def to_sibling(self, layer, big):
        names = list(big)
        flat = [big[k] for k in names]

        def sink(landed):
            for k, g, la in zip(names, flat, landed):
                self.parts[k, layer] = _chip_sum(g, la, self.core, f"chip_sum_{k}_{layer}")
                self.pending.append((k, layer))

        return _sibling_job(flat), sink

    def ready(self, layer, big):
        job, sink = self.to_sibling(layer, big)
        sink(_run_job(job, f"grads_to_sibling_{next(iter(big))}_{layer}"))

    def take(self):
        keys, self.pending = self.pending, []
        if not keys:
            return None, lambda results: None

        def sink(results):
            self.across.update(zip(keys, results))

        return _across_job([self.parts[key] for key in keys]), sink


def _sum_chips(parts, landed, where, layer, depth, prev, name):
    _, hr, C = landed.shape
    tr = _pick(hr, (256, 352, 128))
    nr = hr // tr

    def body(*refs):
        own_ref, slots, o_ref = refs[1], refs[2:2 + N_CHIPS], refs[-1]
        chip = refs[0][0]
        total = None
        for q in range(N_CHIPS):
            term = jnp.where(chip == q, own_ref[...], slots[q][...]).astype(F32)
            total = term if total is None else total + term
        o_ref[...] = total

    def slot_spec(q):
        return pl.BlockSpec((None, tr, C), lambda r, w: (jnp.where(w[0] == q, (q + 1) % N_CHIPS, q), r, 0))

    in_specs = [pl.BlockSpec((None, tr, C), lambda r, w: (w[0], r, 0))] + [slot_spec(q) for q in range(N_CHIPS)]
    args = [where, parts] + [landed] * N_CHIPS
    aliases = {}
    if prev is not None:
        in_specs.append(ANY)
        args.append(prev)
        aliases = {len(args) - 1: 0}
    grid_spec = pltpu.PrefetchScalarGridSpec(
        num_scalar_prefetch=1, grid=(nr,), in_specs=in_specs,
        out_specs=pl.BlockSpec((None, tr, C), lambda r, w: (layer, w[1] * nr + r, 0)))
    return pl.pallas_call(
        body, name=name, grid_spec=grid_spec, out_shape=jax.ShapeDtypeStruct((depth, 2 * hr, C), F32),
        input_output_aliases=aliases, compiler_params=_params(("parallel",)),
    )(*args)


def _rs_join_halves(reduced):
    n = len(reduced)

    def body(*refs):
        o_refs = refs[n:2 * n]
        send_sems, recv_sems = refs[2 * n:]
        x, y, c = _position()
        sent = []
        for wi in range(n):
            hr = reduced[wi].shape[1] // 2
            mine = _half_rows(o_refs[wi], c, hr)
            cp = pltpu.make_async_remote_copy(
                src_ref=mine, dst_ref=mine, send_sem=send_sems.at[wi], recv_sem=recv_sems.at[wi],
                device_id=(x, y, 1 - c), device_id_type=MESH)
            cp.start()
            sent.append(cp)
        for wi in range(n):
            hr = reduced[wi].shape[1] // 2
            theirs = _half_rows(o_refs[wi], 1 - c, hr)
            pltpu.make_async_remote_copy(
                src_ref=theirs, dst_ref=theirs, send_sem=send_sems.at[wi], recv_sem=recv_sems.at[wi],
                device_id=(x, y, c), device_id_type=MESH).wait_recv()
        for cp in sent:
            cp.wait_send()

    return pl.pallas_call(
        body, name="grads_join_halves", out_shape=[jax.ShapeDtypeStruct(r.shape, r.dtype) for r in reduced],
        in_specs=[ANY] * n, out_specs=[ANY] * n, input_output_aliases={i: i for i in range(n)},
        scratch_shapes=[pltpu.SemaphoreType.DMA((n,)), pltpu.SemaphoreType.DMA((n,))],
    )(*reduced)


def _adam_math(w, g, m, v):
    m = ADAM_B1 * m + (1.0 - ADAM_B1) * g
    v = ADAM_B2 * v + (1.0 - ADAM_B2) * jnp.square(g)
    m_hat = m / (1.0 - ADAM_B1 ** ADAM_STEP)
    v_hat = v / (1.0 - ADAM_B2 ** ADAM_STEP)
    delta = -ADAM_LR * (m_hat / (jnp.sqrt(v_hat) + ADAM_EPS) + ADAM_WD * w)
    return delta, m, v


def _small_reduce_adam(vec_grads, dww_grads, final_grad, meta_grad, state):
    vec_names = list(vec_grads)
    depth = len(dww_grads)
    D = final_grad.shape[1]
    n_meta = meta_grad.shape[0]
    taps_pad, C = dww_grads[0].shape
    names = vec_names + ["final", "meta", "dww"]
    at, row0 = 0, {}
    for k in vec_names:
        row0[k] = at
        at += depth
    row0["final"] = at
    at = -(-(at + 1) // 8) * 8
    row0["meta"] = at
    at += -(-n_meta // 8) * 8
    row0["dww"] = at
    rows = at + depth * taps_pad
    lanes = max(D, C)
    n_g = len(vec_names) * depth + depth + 2

    def body(*refs):
        g_refs = refs[:n_g]
        st = refs[n_g:n_g + 3 * len(names)]
        outs = refs[n_g + 3 * len(names):n_g + 7 * len(names)]
        slab, land, send_sems, recv_sems = refs[n_g + 7 * len(names):]
        x, y, c = _position()
        me = 4 * x + 2 * y + c
        chip = 2 * x + y
        slab[...] = jnp.zeros_like(slab)
        it = iter(g_refs)
        for k in vec_names:
            for l in range(depth):
                g_ref = next(it)
                slab[row0[k] + l:row0[k] + l + 1, 0:g_ref.shape[1]] = g_ref[...]
        for l in range(depth):
            slab[row0["dww"] + l * taps_pad:row0["dww"] + (l + 1) * taps_pad, 0:C] = next(it)[...]
        slab[row0["final"]:row0["final"] + 1, 0:D] = next(it)[...]
        slab[row0["meta"]:row0["meta"] + n_meta, 0:D] = next(it)[...]
        sent = []
        for k in range(1, N_DEV):
            to = (1 - x if k & 4 else x, 1 - y if k & 2 else y, 1 - c if k & 1 else c)
            cp = pltpu.make_async_remote_copy(
                src_ref=slab, dst_ref=land.at[k], send_sem=send_sems.at[k - 1], recv_sem=recv_sems.at[k - 1],
                device_id=to, device_id_type=MESH)
            cp.start()
            sent.append(cp)
        land[0] = slab[...]
        for cp in sent:
            cp.wait_recv()
        for cp in sent:
            cp.wait_send()
        total = land[me]
        for e in range(1, N_DEV):
            total = total + land[me ^ e]
        slab[...] = total

        def mine(r0, n_rows, width):
            got = slab[r0:r0 + n_rows, 0:width]
            for j in range(1, N_CHIPS):
                got = jnp.where(chip == j, slab[r0:r0 + n_rows, j * width:(j + 1) * width], got)
            return got

        def update(i, g, index=()):
            w_ref, m_ref, v_ref = st[3 * i:3 * i + 3]
            o = outs[4 * i:4 * i + 4]
            at = index if index else Ellipsis
            res = (g,) + _adam_math(w_ref[at], g, m_ref[at], v_ref[at])
            for o_ref, val in zip(o, res):
                o_ref[at] = val

        for i, k in enumerate(vec_names):
            width = st[3 * i].shape[1]
            update(i, slab[row0[k]:row0[k] + depth, 0:width])
        base = len(vec_names)
        update(base, slab[row0["final"]:row0["final"] + 1, 0:D])
        update(base + 1, mine(row0["meta"], n_meta, D // N_CHIPS))
        taps = st[3 * (base + 2)].shape[1]
        for l in range(depth):
            update(base + 2, mine(row0["dww"] + l * taps_pad, taps, C // N_CHIPS), (l,))

    flat_g = [g for k in vec_names for g in vec_grads[k]] + list(dww_grads) + [final_grad, meta_grad]
    flat_state = [a for k in names for a in state[k]]
    vmem = pl.BlockSpec(memory_space=pltpu.VMEM)
    out_shape = [jax.ShapeDtypeStruct(state[k][0].shape, F32) for k in names for _ in range(4)]
    res = pl.pallas_call(
        body, name="small_reduce_adam", out_shape=out_shape,
        in_specs=[vmem] * (len(flat_g) + len(flat_state)), out_specs=[vmem] * len(out_shape),
        scratch_shapes=[pltpu.VMEM((rows, lanes), F32), pltpu.VMEM((N_DEV, rows, lanes), F32),
                        pltpu.SemaphoreType.DMA((N_DEV - 1,)), pltpu.SemaphoreType.DMA((N_DEV - 1,))],
    )(*flat_g, *flat_state)
    return {k: tuple(res[4 * i:4 * i + 4]) for i, k in enumerate(names)}


def _adam(w, g, m, v, name):
    def body(w_ref, g_ref, m_ref, v_ref, d_ref, nm_ref, nv_ref):
        d_ref[...], nm_ref[...], nv_ref[...] = _adam_math(w_ref[...], g_ref[...], m_ref[...], v_ref[...])

    if w.ndim == 3:
        lyr, R, C = w.shape
        tr = _pick(R, (256, 352, 128))
        blk = pl.BlockSpec((None, tr, C), lambda l, r: (l, r, 0))
        grid, sem = (lyr, R // tr), ("parallel", "parallel")
    else:
        blk = pl.BlockSpec(w.shape, lambda: (0, 0))
        grid, sem = (), None
    return pl.pallas_call(
        body, name=name, grid=grid, in_specs=[blk] * 4, out_specs=[blk] * 3,
        out_shape=[jax.ShapeDtypeStruct(w.shape, F32)] * 3, compiler_params=_params(sem),
    )(w, g, m, v)


def _rows(a, pad_to=8):
    r = a.reshape(-1, LANES)
    extra = (-r.shape[0]) % pad_to
    return jnp.pad(r, ((0, extra), (0, 0))) if extra else r


def _pack(arrays):
    return jnp.concatenate([_rows(a) for a in arrays], axis=0)


def _unpack(slab, shapes):
    out, at = [], 0
    for shp in shapes:
        nrow = math.prod(shp) // LANES
        out.append(slab[at:at + nrow].reshape(shp))
        at += nrow + (-nrow) % 8
    return out


BIG = ("w_in", "w_out", "w_gate_t", "w_up_t", "w_down")
BIG_COL_SHARDED = (True, False, False, False, False)
TRANSPOSED = {"w_gate_t": "w_gate", "w_up_t": "w_up"}


def kernel(x, meta_tokens, mix_norm_g, w_in, conv_dw_w, conv_dw_b, conv_ln_g, conv_ln_b, w_out, ffn_norm_g, w_gate, w_up, w_down, final_norm_g, loss_target, m_meta_tokens, m_mix_norm_g, m_w_in, m_conv_dw_w, m_conv_dw_b, m_conv_ln_g, m_conv_ln_b, m_w_out, m_ffn_norm_g, m_w_gate, m_w_up, m_w_down, m_final_norm_g, v_meta_tokens, v_mix_norm_g, v_w_in, v_conv_dw_w, v_conv_dw_b, v_conv_ln_g, v_conv_ln_b, v_w_out, v_ffn_norm_g, v_w_gate, v_w_up, v_w_down, v_final_norm_g):
    n_meta, seq = meta_tokens.shape[0], x.shape[1]
    D = x.shape[2]
    depth, taps, c_shard = conv_dw_w.shape
    C = conv_dw_b.shape[-1]
    chip = (2 * lax.axis_index("x") + lax.axis_index("y")).astype(jnp.int32)
    core = lax.axis_index("c").astype(jnp.int32).reshape(1)
    chip1 = chip.reshape(1)
    where = jnp.concatenate([chip1, core])
    big_w = dict(w_in=w_in, w_out=w_out, w_gate=w_gate, w_up=w_up, w_down=w_down)
    big_m = dict(w_in=m_w_in, w_out=m_w_out, w_gate=m_w_gate, w_up=m_w_up, w_down=m_w_down)
    big_v = dict(w_in=v_w_in, w_out=v_w_out, w_gate=v_w_gate, w_up=v_w_up, w_down=v_w_down)

    small_shard = _pack([conv_dw_w, meta_tokens])[None]
    to_send = {k: jnp.swapaxes(big_w[TRANSPOSED[k]], 1, 2) if k in TRANSPOSED else big_w[k] for k in BIG}
    col = dict(zip(BIG, BIG_COL_SHARDED))
    wts = {k: [_place_shard(to_send[k], l, chip1, col[k], BF16, f"place_{k}_{l}") for l in range(depth)] for k in BIG}
    small_placed = _place_shard(small_shard, 0, chip1, False, F32, "place_small")

    def gather_job(keys, extra=()):
        arrays = [wts[k][l] for k, l in keys] + list(extra)
        shapes = [(1,) + to_send[k].shape[1:] for k, _ in keys] + [(1,) + small_shard.shape[1:]] * len(extra)
        return _gather_job(arrays, shapes, [col[k] for k, _ in keys] + [False] * len(extra))

    first_keys = [("w_in", 0)]
    *first, small_full = _run_job(gather_job(first_keys, [small_placed]), "gather_first")
    for (k, l), arr in zip(first_keys, first):
        wts[k][l] = arr
    behind_keys = {("attn", 0): [("w_out", 0), ("w_gate_t", 0), ("w_up_t", 0)], ("conv", 0): [("w_down", 0)]}
    for l in range(1, depth):
        behind_keys["ffn", l - 1] = [("w_in", l), ("w_gate_t", l)]
        behind_keys["down", l - 1] = [("w_up_t", l)]
        behind_keys["attn", l] = [("w_out", l), ("w_down", l)]
    gather_behind = {host: (gather_job(keys), keys) for host, keys in behind_keys.items()}

    rows_shard = small_shard.shape[1]
    dw_full, meta_full = [], []
    for j in range(N_CHIPS):
        dwj, mj = _unpack(small_full[0, j * rows_shard:(j + 1) * rows_shard],
                          [conv_dw_w.shape, meta_tokens.shape])
        dw_full.append(dwj)
        meta_full.append(mj)
    dw_w_full = jnp.concatenate(dw_full, axis=2)
    meta = jnp.concatenate(meta_full, axis=1)

    L = n_meta + seq
    Lp = -(-L // QUERY_BLOCK) * QUERY_BLOCK
    h0 = jnp.concatenate([meta, x[0], jnp.zeros((Lp - L, D), F32)], axis=0)
    target = jnp.pad(loss_target[0], ((n_meta, Lp - L), (0, 0)))
    reducer = _Reducer(core)
    loss, dh0, grads = _local_step(h0, target, n_meta, seq, (mix_norm_g, ffn_norm_g),
                                   (dw_w_full, conv_dw_b, conv_ln_g, conv_ln_b), wts, final_norm_g,
                                   gather_behind, reducer)
    loss = lax.psum(loss[0, 0], ("x", "y", "c"))
    grad_x = dh0[n_meta:L][None]

    reduced = []
    for k in BIG:
        arr = None
        for l in range(depth):
            arr = _sum_chips(reducer.parts[k, l], reducer.across[k, l], where, l, depth, arr, f"sum_chips_{k}_{l}")
        reduced.append(arr)
    big_g = dict(zip(BIG, _rs_join_halves(reduced)))

    out_g, out_d, out_m, out_v = {}, {}, {}, {}
    for kk in BIG:
        k = TRANSPOSED.get(kk, kk)
        view = (lambda a: jnp.swapaxes(a, 1, 2)) if kk in TRANSPOSED else (lambda a: a)
        res = _adam(view(big_w[k]), big_g[kk], view(big_m[k]), view(big_v[k]), f"adam_{k}")
        out_g[k] = view(big_g[kk])
        out_d[k], out_m[k], out_v[k] = (view(a) for a in res)

    as_row = lambda a: a.reshape(1, -1)
    state = dict(mix_g=(mix_norm_g, m_mix_norm_g, v_mix_norm_g), ffn_g=(ffn_norm_g, m_ffn_norm_g, v_ffn_norm_g),
                 dw_b=(conv_dw_b, m_conv_dw_b, v_conv_dw_b), ln_g=(conv_ln_g, m_conv_ln_g, v_conv_ln_g),
                 ln_b=(conv_ln_b, m_conv_ln_b, v_conv_ln_b),
                 final=(as_row(final_norm_g), as_row(m_final_norm_g), as_row(v_final_norm_g)),
                 meta=(meta_tokens, m_meta_tokens, v_meta_tokens), dww=(conv_dw_w, m_conv_dw_w, v_conv_dw_w))
    vec_names = ("mix_g", "ffn_g", "dw_b", "ln_g", "ln_b")
    small = _small_reduce_adam({k: grads[k] for k in vec_names}, grads["dw_w"], grads["final_g"], dh0[:n_meta], state)
    out_name = dict(mix_g="mix_norm_g", ffn_g="ffn_norm_g", dw_b="conv_dw_b", ln_g="conv_ln_g", ln_b="conv_ln_b",
                    final="final_norm_g", meta="meta_tokens", dww="conv_dw_w")
    for k, res in small.items():
        if k == "final":
            res = tuple(a.reshape(-1) for a in res)
        out_g[out_name[k]], out_d[out_name[k]], out_m[out_name[k]], out_v[out_name[k]] = res

    order = ("meta_tokens", "mix_norm_g", "w_in", "conv_dw_w", "conv_dw_b", "conv_ln_g", "conv_ln_b", "w_out",
             "ffn_norm_g", "w_gate", "w_up", "w_down", "final_norm_g")
    return (loss, grad_x, *[out_g[k] for k in order], *[out_d[k] for k in order],
            *[out_m[k] for k in order], *[out_v[k] for k in order])
```

```python
import functools
import math

import jax
import jax.numpy as jnp
from jax import lax
from jax.experimental import pallas as pl
from jax.experimental.pallas import tpu as pltpu

F32 = jnp.float32
BF16 = jnp.bfloat16
MESH = pl.DeviceIdType.MESH

EPS = 1e-6
QUERY_BLOCK = 128
LANES = 128
HEAD_DIM = 64
LOG_STICK_FLOOR = -40.0
CONV_HALO = 32
N_CHIPS = 4
N_DEV = 8
VMEM_LIMIT = 56 * 1024 * 1024

ADAM_LR = 0.001
ADAM_B1 = 0.9
ADAM_B2 = 0.999
ADAM_EPS = 1e-08
ADAM_WD = 0.01
ADAM_STEP = 10


def _pick(n, prefs):
    for p in prefs:
        if n % p == 0:
            return p
    return n


def _params(sem=None):
    return pltpu.CompilerParams(dimension_semantics=sem, vmem_limit_bytes=VMEM_LIMIT)


def _sigmoid(x):
    return 1.0 / (1.0 + jnp.exp(-x))


def _loss_head(h, g, target, n_meta, seq, name):
    L, D = h.shape
    T = _pick(L, (384, 128))

    def body(h_ref, g_ref, t_ref, loss_ref, dh_ref, dg_ref):
        i = pl.program_id(0)
        x = h_ref[...]
        gv = g_ref[...]
        r = lax.rsqrt(jnp.mean(x * x, axis=-1, keepdims=True) + EPS)
        xh = x * r
        y = xh * gv
        rows = i * T + lax.broadcasted_iota(jnp.int32, (T, 1), 0)
        live = (rows >= n_meta) & (rows < n_meta + seq)
        diff = jnp.where(live, y - t_ref[...], 0.0)
        dyv = diff / D
        dxh = dyv * gv
        dh_ref[...] = r * (dxh - xh * jnp.mean(dxh * xh, axis=-1, keepdims=True))

        @pl.when(i == 0)
        def _():
            dg_ref[...] = jnp.zeros_like(dg_ref)
            loss_ref[...] = jnp.zeros_like(loss_ref)

        dg_ref[...] += jnp.sum(dyv * xh, axis=0, keepdims=True)
        per_row = jnp.mean(diff * diff, axis=-1, keepdims=True)
        loss_ref[...] += 0.5 * jnp.sum(per_row, axis=0, keepdims=True)

    row = pl.BlockSpec((T, D), lambda i: (i, 0))
    vec = pl.BlockSpec((1, D), lambda i: (0, 0))
    one = pl.BlockSpec((1, 1), lambda i: (0, 0))
    return pl.pallas_call(
        body, name=name, grid=(L // T,),
        in_specs=[row, vec, row], out_specs=[one, row, vec],
        out_shape=[jax.ShapeDtypeStruct((1, 1), F32), jax.ShapeDtypeStruct((L, D), F32),
                   jax.ShapeDtypeStruct((1, D), F32)],
        compiler_params=_params(("arbitrary",)),
    )(h, g, target)


def _ffn_tiles(M, F):
    return _pick(M, (352, 384, 128)), _pick(F, (1408, 512, 256, 128))


def _resident(shape):
    return pl.BlockSpec((None,) + tuple(shape[1:]), lambda *_: (0,) * len(shape), pipeline_mode=pl.Buffered(1))


def _norm_in_proj(h, g, w_in, n_qkv, name):
    M, D = h.shape
    N = w_in.shape[2]
    tm = _pick(M, (352, 384, 128))
    tn = _pick(math.gcd(n_qkv, N - n_qkv), (512, 256, 128))

    def body(h_ref, g_ref, w_ref, hn_ref, qkv_ref, rest_ref):
        x = h_ref[...]
        r = lax.rsqrt(jnp.mean(x * x, axis=-1, keepdims=True) + EPS)
        hv = (x * r * g_ref[...]).astype(BF16)
        hn_ref[...] = hv
        for c in range(0, N, tn):
            out = jnp.dot(hv, w_ref[:, c:c + tn], preferred_element_type=F32)
            if c < n_qkv:
                qkv_ref[:, c:c + tn] = out.astype(qkv_ref.dtype)
            else:
                rest_ref[:, c - n_qkv:c - n_qkv + tn] = out

    rows = pl.BlockSpec((tm, D), lambda i: (i, 0))
    return pl.pallas_call(
        body, name=name, grid=(M // tm,),
        in_specs=[rows, pl.BlockSpec((1, D), lambda i: (0, 0)), _resident(w_in.shape)],
        out_specs=[rows, pl.BlockSpec((tm, n_qkv), lambda i: (i, 0)), pl.BlockSpec((tm, N - n_qkv), lambda i: (i, 0))],
        out_shape=[jax.ShapeDtypeStruct((M, D), BF16), jax.ShapeDtypeStruct((M, n_qkv), BF16),
                   jax.ShapeDtypeStruct((M, N - n_qkv), F32)],
        compiler_params=_params(("parallel",)),
    )(h, g, w_in)


def _ffn_up(h, norm_g, w_gate_t, w_up_t, name, job=None):
    M, D = h.shape
    F = w_gate_t.shape[1]
    tm, tf = _ffn_tiles(M, F)
    nt = (((1,), (1,)), ((), ()))

    def body(h_ref, ng_ref, wg_ref, wu_ref, hn_ref, g_ref, u_ref, a_ref):
        x = h_ref[...]
        r = lax.rsqrt(jnp.mean(x * x, axis=-1, keepdims=True) + EPS)
        hv = (x * r * ng_ref[...]).astype(BF16)
        hn_ref[...] = hv
        for c in range(0, F, tf):
            gv = lax.dot_general(hv, wg_ref[c:c + tf, :], nt, preferred_element_type=F32)
            uv = lax.dot_general(hv, wu_ref[c:c + tf, :], nt, preferred_element_type=F32)
            g_ref[:, c:c + tf] = gv.astype(g_ref.dtype)
            u_ref[:, c:c + tf] = uv.astype(u_ref.dtype)
            a_ref[:, c:c + tf] = (gv * _sigmoid(gv) * uv).astype(a_ref.dtype)

    rows = pl.BlockSpec((tm, D), lambda i: (i, 0))
    wide = pl.BlockSpec((tm, F), lambda i: (i, 0))
    return _call_with_job(
        body, job, name=name, grid=(M // tm,),
        in_specs=[rows, pl.BlockSpec((1, D), lambda i: (0, 0)), _resident(w_gate_t.shape), _resident(w_up_t.shape)],
        out_specs=[rows, wide, wide, wide],
        out_shape=[jax.ShapeDtypeStruct((M, D), BF16), jax.ShapeDtypeStruct((M, F), BF16),
                   jax.ShapeDtypeStruct((M, F), BF16), jax.ShapeDtypeStruct((M, F), BF16)],
        scratch_shapes=[], args=(h, norm_g, w_gate_t, w_up_t))


def _ffn_down_bwd(dh, w_down, g, u, name):
    M, D = dh.shape
    F = g.shape[1]
    tm, tf = _ffn_tiles(M, F)

    def body(d_ref, w_ref, g_ref, u_ref, dg_ref, du_ref):
        dhv = d_ref[...].astype(BF16)
        for c in range(0, F, tf):
            dv = lax.dot_general(dhv, w_ref[c:c + tf, :], (((1,), (1,)), ((), ())), preferred_element_type=F32)
            gv = g_ref[:, c:c + tf].astype(F32)
            s = _sigmoid(gv)
            du_ref[:, c:c + tf] = (dv * (gv * s)).astype(du_ref.dtype)
            dg_ref[:, c:c + tf] = (dv * u_ref[:, c:c + tf].astype(F32)
                                   * (s * (1.0 + gv * (1.0 - s)))).astype(dg_ref.dtype)

    wide = pl.BlockSpec((tm, F), lambda i: (i, 0))
    return pl.pallas_call(
        body, name=name, grid=(M // tm,),
        in_specs=[pl.BlockSpec((tm, D), lambda i: (i, 0)), _resident(w_down.shape), wide, wide],
        out_specs=[wide, wide],
        out_shape=[jax.ShapeDtypeStruct((M, F), BF16), jax.ShapeDtypeStruct((M, F), BF16)],
        compiler_params=_params(("parallel",)),
    )(dh, w_down, g, u)


def _mm_rows(pairs, name, transposed=False, residual=None, norm_bwd=None, job=None):
    M = pairs[0][0].shape[0]
    N = pairs[0][1].shape[1 if transposed else 2]
    shallow = sum(p[0].shape[1] for p in pairs) <= 1024
    tm = _pick(M, ((1056,) if shallow else ()) + (352, 384, 128))
    tn = _pick(N, (512, 256, 128))
    n = len(pairs)

    def body(*refs):
        a_refs, w_refs, extra = refs[:n], refs[n:2 * n], refs[2 * n:]
        p_ref = extra[-1] if norm_bwd is not None else extra[-1 if residual is None else 1]
        lhs = [a_ref[...].astype(BF16) for a_ref in a_refs]
        for c in range(0, N, tn):
            acc = None
            for a, w_ref, (_, _, blk) in zip(lhs, w_refs, pairs):
                k = a.shape[1]
                if transposed:
                    d = lax.dot_general(a, w_ref[c:c + tn, blk * k:(blk + 1) * k], (((1,), (1,)), ((), ())),
                                        preferred_element_type=F32)
                else:
                    d = jnp.dot(a, w_ref[blk * k:(blk + 1) * k, c:c + tn], preferred_element_type=F32)
                acc = d if acc is None else acc + d
            if residual is not None:
                acc = acc + extra[0][:, c:c + tn]
            p_ref[:, c:c + tn] = acc
        if norm_bwd is not None:
            h_ref, g_ref, dhin_ref, dh_ref, dg_ref, _ = extra
            x = h_ref[...]
            dyv = p_ref[...]
            r = lax.rsqrt(jnp.mean(x * x, axis=-1, keepdims=True) + EPS)
            xh = x * r
            dxh = dyv * g_ref[...]
            dh_ref[...] = dhin_ref[...] + r * (dxh - xh * jnp.mean(dxh * xh, axis=-1, keepdims=True))

            @pl.when(pl.program_id(0) == 0)
            def _():
                dg_ref[...] = jnp.zeros_like(dg_ref)

            dg_ref[...] += jnp.sum(dyv * xh, axis=0, keepdims=True)

    in_specs = [pl.BlockSpec((tm, a.shape[1]), lambda i: (i, 0)) for a, _, _ in pairs]
    in_specs += [_resident(w.shape) for _, w, _ in pairs]
    args = [p[0] for p in pairs] + [p[1] for p in pairs]
    wide = pl.BlockSpec((tm, N), lambda i: (i, 0))
    vec = pl.BlockSpec((1, N), lambda i: (0, 0))
    out_specs, out_shape, scratch = [wide], [jax.ShapeDtypeStruct((M, N), F32)], []
    if residual is not None:
        in_specs.append(wide)
        args.append(residual)
    if norm_bwd is not None:
        in_specs += [wide, vec, wide]
        args += list(norm_bwd)
        out_specs.append(vec)
        out_shape.append(jax.ShapeDtypeStruct((1, N), F32))
        scratch = [pltpu.VMEM((tm, N), F32)]
    outs, arrived = _call_with_job(body, job, name=name, grid=(M // tm,), in_specs=in_specs, out_specs=out_specs,
                                   out_shape=out_shape, scratch_shapes=scratch, args=args)
    res = tuple(outs) if norm_bwd is not None else (outs[0],)
    res = res if job is None else res + (arrived,)
    return res[0] if len(res) == 1 else res


def _mm_tn(a, b, name, col_sharded, chips=N_CHIPS):
    M, K = a.shape
    N = b.shape[1]
    tm = _pick(M, (1056, 384, 128))
    tk = _pick(K, (1024, 1408, 512, 256, 128))
    tn = N // N_CHIPS if col_sharded else _pick(N, (1024, 512, 128))

    def body(a_ref, b_ref, o_ref):
        @pl.when(pl.program_id(2) == 0)
        def _():
            o_ref[...] = jnp.zeros_like(o_ref)

        o_ref[...] += lax.dot_general(a_ref[...].astype(BF16), b_ref[...].astype(BF16),
                                      (((0,), (0,)), ((), ())), preferred_element_type=F32)

    if col_sharded:
        out_shape = jax.ShapeDtypeStruct((N_CHIPS, K, tn), F32)
        out_spec = pl.BlockSpec((None, tk, tn), lambda k, j, m: (j, k, 0))
    else:
        out_shape = jax.ShapeDtypeStruct((K, N), F32)
        out_spec = pl.BlockSpec((tk, tn), lambda k, j, m: (k, j))
    out = pl.pallas_call(
        body, name=name, grid=(K // tk, N // tn, M // tm),
        in_specs=[pl.BlockSpec((tm, tk), lambda k, j, m: (m, k)), pl.BlockSpec((tm, tn), lambda k, j, m: (m, j))],
        out_specs=out_spec, out_shape=out_shape,
        compiler_params=_params(("parallel", "parallel", "arbitrary")),
    )(a, b)
    return out if col_sharded else out.reshape(chips, K // chips, N)


def _stack_heads(x, scale=None):
    lane = lax.broadcasted_iota(jnp.int32, x.shape, 1)
    zero = jnp.zeros_like(x)
    lo = jnp.where(lane < HEAD_DIM, x, zero)
    hi = jnp.where(lane < HEAD_DIM, zero, x)
    out = jnp.concatenate([lo, hi], axis=0)
    return out if scale is None else out * scale


def _unstack_heads(x2):
    qb = x2.shape[0] // 2
    lane = lax.broadcasted_iota(jnp.int32, (qb, LANES), 1)
    return jnp.where(lane < HEAD_DIM, x2[:qb], x2[qb:])


def _dot2(x, m):
    xh = x.astype(BF16)
    xl = (x - xh.astype(F32)).astype(BF16)
    return jnp.dot(xh, m, preferred_element_type=F32) + jnp.dot(xl, m, preferred_element_type=F32)


ATTN_CHUNK = 32


def _in_lockstep(staged):
    waiting, live = list(staged), []
    while waiting or live:
        if waiting:
            live.append(waiting.pop(0))
        for gen in list(live):
            if next(gen, StopIteration) is StopIteration:
                live.remove(gen)


def _row_chunks():
    return [slice(r, r + ATTN_CHUNK) for r in range(0, 2 * QUERY_BLOCK, ATTN_CHUNK)]


def _chunk_valid(rows, older):
    shape = (rows.stop - rows.start, older + QUERY_BLOCK)
    r = (rows.start + lax.broadcasted_iota(jnp.int32, shape, 0)) & (QUERY_BLOCK - 1)
    return lax.broadcasted_iota(jnp.int32, shape, 1) < r + older


def _split_to(x, hi_ref, lo_ref, rows):
    xh = x.astype(BF16)
    hi_ref[rows, :] = xh
    lo_ref[rows, :] = (x - xh.astype(F32)).astype(BF16)


def _triangle(keys, strict):
    r = lax.broadcasted_iota(jnp.int32, (keys, keys), 0)
    c = lax.broadcasted_iota(jnp.int32, (keys, keys), 1)
    return jnp.where((r > c) if strict else (r >= c), 1.0, 0.0).astype(BF16)


def _fill_attn_consts(tri_ref, mask_ref):
    n = 2 * QUERY_BLOCK
    tri_ref[0] = _triangle(n, True)
    tri_ref[1] = _triangle(n, False)
    mask_ref[...] = jnp.where(_chunk_valid(slice(0, n), QUERY_BLOCK), 1.0, 0.0)


def _valid(mask_ref, rows, older):
    cols = slice(0, 2 * QUERY_BLOCK) if older else slice(QUERY_BLOCK, 2 * QUERY_BLOCK)
    return mask_ref[rows, cols] > 0.5


def _row_total(first):
    lane = lax.broadcasted_iota(jnp.int32, first.shape, 1)
    total = jnp.sum(jnp.where(lane == 0, first, 0.0), axis=1, keepdims=True)
    return jnp.broadcast_to(total, first.shape)


def _pairs_per_step(n_pairs, most):
    return max(g for g in (1, 2, 4) if g <= most and n_pairs % g == 0)


def _lanes(g):
    return slice(g * LANES, (g + 1) * LANES)


def _sweep_older(i, step, carry_ref, first):
    def cond(state):
        n, live = state
        return jnp.logical_and(n < i, live)

    def older(state):
        n, _ = state
        step(i - 1 - n, False)
        return n + 1, jnp.max(carry_ref[...]) > LOG_STICK_FLOOR

    lax.while_loop(cond, older, (first, jnp.max(carry_ref[...]) > LOG_STICK_FLOOR))


class _CommJob:
    def __init__(self, inputs, out_shapes, aliases, n_sems, start, finish):
        self.inputs, self.out_shapes, self.aliases, self.n_sems = list(inputs), list(out_shapes), aliases, n_sems
        self.start, self.finish = start, finish


def _merge_jobs(jobs):
    jobs = [j for j in jobs if j is not None]
    if len(jobs) <= 1:
        return jobs[0] if jobs else None
    spans, aliases = [], {}
    i0 = o0 = s0 = 0
    for j in jobs:
        spans.append((i0, o0, s0))
        aliases.update({i0 + a: o0 + b for a, b in j.aliases.items()})
        i0, o0, s0 = i0 + len(j.inputs), o0 + len(j.out_shapes), s0 + j.n_sems

    def run(which):
        def go(ins, outs, send_sems, recv_sems):
            for j, (i, o, s) in zip(jobs, spans):
                getattr(j, which)(ins[i:i + len(j.inputs)], outs[o:o + len(j.out_shapes)],
                                  send_sems.at[pl.ds(s, j.n_sems)], recv_sems.at[pl.ds(s, j.n_sems)])
        return go

    return _CommJob([a for j in jobs for a in j.inputs], [s for j in jobs for s in j.out_shapes], aliases, s0,
                    run("start"), run("finish"))


def _call_with_job(core_body, job, *, name, grid, in_specs, out_specs, out_shape, scratch_shapes, args):
    sem = ("arbitrary",) * len(grid)
    if job is None:
        res = pl.pallas_call(core_body, name=name, grid=grid, in_specs=in_specs, out_specs=out_specs,
                             out_shape=out_shape, scratch_shapes=scratch_shapes, compiler_params=_params(sem))(*args)
        return list(res), []
    n_in, n_out, n_scr = len(in_specs), len(out_specs), len(scratch_shapes)
    m_in, m_out = len(job.inputs), len(job.out_shapes)

    def body(*refs):
        at = 0
        parts = []
        for count in (n_in, m_in, n_out, m_out, n_scr, 2):
            parts.append(refs[at:at + count])
            at += count
        ins, job_in, outs, job_outs, scratch, (send_sems, recv_sems) = parts
        first = functools.reduce(jnp.logical_and, [pl.program_id(a) == 0 for a in range(len(grid))])
        last = functools.reduce(jnp.logical_and, [pl.program_id(a) == grid[a] - 1 for a in range(len(grid))])

        @pl.when(first)
        def _():
            job.start(job_in, job_outs, send_sems, recv_sems)

        core_body(*ins, *outs, *scratch)

        @pl.when(last)
        def _():
            job.finish(job_in, job_outs, send_sems, recv_sems)

    res = pl.pallas_call(
        body, name=name, grid=grid, in_specs=list(in_specs) + [ANY] * m_in, out_specs=list(out_specs) + [ANY] * m_out,
        out_shape=list(out_shape) + job.out_shapes,
        input_output_aliases={n_in + a: n_out + b for a, b in job.aliases.items()},
        scratch_shapes=list(scratch_shapes) + [pltpu.SemaphoreType.DMA((job.n_sems,)), pltpu.SemaphoreType.DMA((job.n_sems,))],
        compiler_params=_params(sem),
    )(*args, *job.inputs)
    return list(res[:n_out]), list(res[n_out:])


def _run_job(job, name):
    m_in, m_out = len(job.inputs), len(job.out_shapes)

    def body(*refs):
        job_in, job_outs = refs[:m_in], refs[m_in:m_in + m_out]
        send_sems, recv_sems = refs[m_in + m_out:]
        job.start(job_in, job_outs, send_sems, recv_sems)
        job.finish(job_in, job_outs, send_sems, recv_sems)

    return list(pl.pallas_call(
        body, name=name, in_specs=[ANY] * m_in, out_specs=[ANY] * m_out, out_shape=job.out_shapes,
        input_output_aliases=dict(job.aliases),
        scratch_shapes=[pltpu.SemaphoreType.DMA((job.n_sems,)), pltpu.SemaphoreType.DMA((job.n_sems,))],
    )(*job.inputs))


def _attn_fwd(qkv, sb_width, name, job=None):
    L = qkv.shape[0]
    QB = QUERY_BLOCK
    nb = L // QB
    n_pairs = sb_width // LANES
    G = _pairs_per_step(n_pairs, 4)
    W = G * LANES
    scale = 1.0 / math.sqrt(HEAD_DIM)

    def body(q_ref, k_ref, v_ref, o_ref, acc_ref, carry_ref, f32_ref, bf16_ref, tri_ref, mask_ref):
        i = pl.program_id(1)

        @pl.when(i == 0)
        def _():
            _fill_attn_consts(tri_ref, mask_ref)

        q2 = [_stack_heads(q_ref[:, _lanes(g)], scale) for g in range(G)]
        acc_ref[...] = jnp.zeros_like(acc_ref)
        carry_ref[...] = jnp.zeros_like(carry_ref)

        def step(g, j, older, masked):
            n = older + QB
            start = pl.multiple_of(j * QB, QB)
            kb = k_ref[pl.ds(start, n), _lanes(g)]
            vb = v_ref[pl.ds(start, n), _lanes(g)]
            zs, as_, bs = (f32_ref.at[g, t, :, :n] for t in range(3))
            hi, lo = (bf16_ref.at[g, t, :, :n] for t in range(2))
            zs[...] = lax.dot_general(q2[g], kb, (((1,), (1,)), ((), ())), preferred_element_type=F32)
            yield
            for rows in _row_chunks():
                z = zs[rows, :]
                sp = jnp.log(1.0 + jnp.exp(-jnp.abs(z)))
                b = jnp.minimum(-z, 0.0) - sp
                if masked:
                    b = jnp.where(_valid(mask_ref, rows, older),b, 0.0)
                as_[rows, :] = jnp.minimum(z, 0.0) - sp
                bs[rows, :] = b
                _split_to(b, hi, lo, rows)
            yield
            tri = tri_ref[0, :n, :n]
            zs[...] = (jnp.dot(hi[...], tri, preferred_element_type=F32)
                       + jnp.dot(lo[...], tri, preferred_element_type=F32))
            yield
            for rows in _row_chunks():
                excl = zs[rows, :]
                total = _row_total(excl[:, :LANES] + bs[rows, :LANES])
                if not masked:
                    excl = excl + jnp.tile(carry_ref[g, rows, :], (1, n // LANES))
                carry_ref[g, rows, :] += total
                w = jnp.exp(as_[rows, :] + excl)
                if masked:
                    w = jnp.where(_valid(mask_ref, rows, older),w, 0.0)
                _split_to(w, hi, lo, rows)
            yield
            acc_ref[g] += (jnp.dot(hi[...], vb, preferred_element_type=F32)
                           + jnp.dot(lo[...], vb, preferred_element_type=F32))

        @pl.when(i == 0)
        def _():
            _in_lockstep([step(g, 0, 0, True) for g in range(G)])

        @pl.when(i > 0)
        def _():
            _in_lockstep([step(g, i - 1, QB, True) for g in range(G)])
            for g in range(G):
                _sweep_older(i, lambda j, _, g=g: _in_lockstep([step(g, j, 0, False)]), carry_ref.at[g], 1)

        for g in range(G):
            o_ref[:, _lanes(g)] = _unstack_heads(acc_ref[g])

    n_steps = n_pairs // G
    (out,), job_out = _call_with_job(
        body, job, name=name, grid=(n_steps, nb),
        in_specs=[pl.BlockSpec((QB, W), lambda p, i: (i, p)),
                  pl.BlockSpec((L, W), lambda p, i: (0, n_steps + p)),
                  pl.BlockSpec((L, W), lambda p, i: (0, 2 * n_steps + p))],
        out_specs=[pl.BlockSpec((QB, W), lambda p, i: (i, p))],
        out_shape=[jax.ShapeDtypeStruct((L, sb_width), F32)],
        scratch_shapes=[pltpu.VMEM((G, 2 * QB, LANES), F32), pltpu.VMEM((G, 2 * QB, LANES), F32),
                        pltpu.VMEM((G, 3, 2 * QB, 2 * QB), F32), pltpu.VMEM((G, 2, 2 * QB, 2 * QB), BF16),
                        pltpu.VMEM((2, 2 * QB, 2 * QB), BF16), pltpu.VMEM((2 * QB, 2 * QB), F32)],
        args=(qkv, qkv, qkv))
    return out, job_out


def _attn_bwd(qkv, o, dmixed, sb_width, name, job=None):
    L = qkv.shape[0]
    QB = QUERY_BLOCK
    nb = L // QB
    n_pairs = sb_width // LANES
    G = _pairs_per_step(n_pairs, 2)
    W = G * LANES
    scale = 1.0 / math.sqrt(HEAD_DIM)

    def body(q_ref, k_ref, v_ref, o_ref, do_ref, dq_ref, dk_ref, dv_ref,
             dq_acc, dk_acc, dv_acc, ce_ref, cr_ref, dtot_ref, f32_ref, bf16_ref, tri_ref, mask_ref):
        i = pl.program_id(1)

        @pl.when(i == 0)
        def _():
            dk_acc[...] = jnp.zeros_like(dk_acc)
            dv_acc[...] = jnp.zeros_like(dv_acc)
            _fill_attn_consts(tri_ref, mask_ref)

        q2s = [_stack_heads(q_ref[:, _lanes(g)], scale) for g in range(G)]
        do2s = [_stack_heads(do_ref[:, _lanes(g)].astype(BF16)) for g in range(G)]
        ones = jnp.ones((LANES, LANES), BF16)
        for g in range(G):
            ov = o_ref[:, _lanes(g)]
            dtot_ref[g] = _dot2(do2s[g].astype(F32) * jnp.concatenate([ov, ov], axis=0), ones)
        dq_acc[...] = jnp.zeros_like(dq_acc)
        ce_ref[...] = jnp.zeros_like(ce_ref)
        cr_ref[...] = jnp.zeros_like(cr_ref)

        def step(g, j, older, masked):
            n = older + QB
            wide = n // LANES
            q2, do2 = q2s[g], do2s[g]
            start = pl.multiple_of(j * QB, QB)
            kb = k_ref[pl.ds(start, n), _lanes(g)]
            vb = v_ref[pl.ds(start, n), _lanes(g)]
            zs, as_, bs, betas, gs = (f32_ref.at[g, t, :, :n] for t in range(5))
            hi, lo, wb = (bf16_ref.at[g, t, :, :n] for t in range(3))
            nt = (((1,), (1,)), ((), ()))
            zs[...] = lax.dot_general(q2, kb, nt, preferred_element_type=F32)
            gs[...] = lax.dot_general(do2, vb, nt, preferred_element_type=F32)
            yield
            for rows in _row_chunks():
                z = zs[rows, :]
                e = jnp.exp(-jnp.abs(z))
                sp = jnp.log(1.0 + e)
                b = jnp.minimum(-z, 0.0) - sp
                if masked:
                    b = jnp.where(_valid(mask_ref, rows, older),b, 0.0)
                rinv = 1.0 / (1.0 + e)
                as_[rows, :] = jnp.minimum(z, 0.0) - sp
                bs[rows, :] = b
                betas[rows, :] = jnp.where(z >= 0.0, rinv, e * rinv)
                _split_to(b, hi, lo, rows)
            yield
            tri = tri_ref[0, :n, :n]
            zs[...] = (jnp.dot(hi[...], tri, preferred_element_type=F32)
                       + jnp.dot(lo[...], tri, preferred_element_type=F32))
            yield
            for rows in _row_chunks():
                excl = zs[rows, :]
                total = _row_total(excl[:, :LANES] + bs[rows, :LANES])
                if not masked:
                    excl = excl + jnp.tile(ce_ref[g, rows, :], (1, wide))
                ce_ref[g, rows, :] += total
                w = jnp.exp(as_[rows, :] + excl)
                if masked:
                    w = jnp.where(_valid(mask_ref, rows, older),w, 0.0)
                wb[rows, :] = w.astype(BF16)
                gw = w * gs[rows, :]
                gs[rows, :] = gw
                _split_to(gw, hi, lo, rows)
            yield
            tri = tri_ref[1, :n, :n]
            zs[...] = (jnp.dot(hi[...], tri, preferred_element_type=F32)
                       + jnp.dot(lo[...], tri, preferred_element_type=F32))
            yield
            for rows in _row_chunks():
                rinc = zs[rows, :]
                total = _row_total(rinc[:, :LANES])
                if not masked:
                    rinc = rinc + jnp.tile(cr_ref[g, rows, :], (1, wide))
                cr_ref[g, rows, :] += total
                beta = betas[rows, :]
                dz = gs[rows, :] * (1.0 - beta) - beta * (jnp.tile(dtot_ref[g, rows, :], (1, wide)) - rinc)
                if masked:
                    dz = jnp.where(_valid(mask_ref, rows, older),dz, 0.0)
                hi[rows, :] = dz.astype(BF16)
            yield
            dzb = hi[...]
            dq_acc[g] += jnp.dot(dzb, kb, preferred_element_type=F32)
            dk_acc[pl.ds(start, n), _lanes(g)] += lax.dot_general(
                dzb, q2, (((0,), (0,)), ((), ())), preferred_element_type=F32)
            dv_acc[pl.ds(start, n), _lanes(g)] += lax.dot_general(
                wb[...], do2, (((0,), (0,)), ((), ())), preferred_element_type=F32)

        @pl.when(i == 0)
        def _():
            _in_lockstep([step(g, 0, 0, True) for g in range(G)])

        @pl.when(i > 0)
        def _():
            _in_lockstep([step(g, i - 1, QB, True) for g in range(G)])
            for g in range(G):
                _sweep_older(i, lambda j, _, g=g: _in_lockstep([step(g, j, 0, False)]), ce_ref.at[g], 1)

        for g in range(G):
            dq_ref[:, _lanes(g)] = (_unstack_heads(dq_acc[g]) * scale).astype(dq_ref.dtype)

        @pl.when(i == nb - 1)
        def _():
            dk_ref[...] = dk_acc[...].astype(dk_ref.dtype)
            dv_ref[...] = dv_acc[...].astype(dv_ref.dtype)

    n_steps = n_pairs // G
    blk = pl.BlockSpec((QB, W), lambda p, i: (i, p))
    col = pl.BlockSpec((L, W), lambda p, i: (0, p))
    return _call_with_job(
        body, job, name=name, grid=(n_steps, nb),
        in_specs=[blk,
                  pl.BlockSpec((L, W), lambda p, i: (0, n_steps + p)),
                  pl.BlockSpec((L, W), lambda p, i: (0, 2 * n_steps + p)),
                  blk, blk],
        out_specs=[blk, col, col],
        out_shape=[jax.ShapeDtypeStruct((L, sb_width), BF16)] * 3,
        scratch_shapes=[pltpu.VMEM((G, 2 * QB, LANES), F32), pltpu.VMEM((L, W), F32),
                        pltpu.VMEM((L, W), F32), pltpu.VMEM((G, 2 * QB, LANES), F32),
                        pltpu.VMEM((G, 2 * QB, LANES), F32), pltpu.VMEM((G, 2 * QB, LANES), F32),
                        pltpu.VMEM((G, 5, 2 * QB, 2 * QB), F32), pltpu.VMEM((G, 3, 2 * QB, 2 * QB), BF16),
                        pltpu.VMEM((2, 2 * QB, 2 * QB), BF16), pltpu.VMEM((2 * QB, 2 * QB), F32)],
        args=(qkv, qkv, qkv, o, dmixed))


def _conv_tile(L):
    return _pick(L, (384, 128))


def _glu(x, C):
    return x[:, :C] * _sigmoid(x[:, C:])


CONV_CHUNK = 32
SHIFT_TAIL = 24


def _fill_shifted(src_ref, dst_ref):
    n = dst_ref.shape[1]
    for r in range(1, 8):
        dst_ref[r - 1] = src_ref[r:r + n, :]


def _rows_at(src_ref, shifted_ref, start, n):
    q, r = divmod(start, 8)
    if r == 0:
        return src_ref[start:start + n, :]
    return shifted_ref[r - 1, 8 * q:8 * q + n, :]


def _conv_fwd(cacg, dw_w, dw_b, ln_g, ln_b, name, job=None):
    L, C2 = cacg.shape
    C = C2 // 2
    T = _conv_tile(L)
    H = CONV_HALO
    K = dw_w.shape[0]
    CH = CONV_CHUNK

    def body(x_ref, prev_ref, w_ref, b_ref, g_ref, beta_ref, o_ref, y_ref, u_ref, us_ref):
        i = pl.program_id(0)
        u_ref[0:H, :] = jnp.where(i > 0, _glu(prev_ref[...], C), 0.0)
        u_ref[H:, :] = _glu(x_ref[...], C)
        _fill_shifted(u_ref, us_ref)
        for c0 in range(0, T, CH):
            y = jnp.broadcast_to(b_ref[...], (CH, C))
            for k in range(K):
                y = y + w_ref[k:k + 1, :] * _rows_at(u_ref, us_ref, c0 + H - (K - 1) + k, CH)
            y_ref[c0:c0 + CH, :] = y
            mu = jnp.mean(y, axis=-1, keepdims=True)
            yc = y - mu
            rstd = lax.rsqrt(jnp.mean(yc * yc, axis=-1, keepdims=True) + EPS)
            ln = yc * rstd * g_ref[...] + beta_ref[...]
            o_ref[c0:c0 + CH, :] = (ln * _sigmoid(ln)).astype(o_ref.dtype)

    vec = pl.BlockSpec((1, C), lambda i: (0, 0))
    tile = pl.BlockSpec((T, C), lambda i: (i, 0))
    (out, y), arrived = _call_with_job(
        body, job, name=name, grid=(L // T,),
        in_specs=[pl.BlockSpec((T, C2), lambda i: (i, 0)),
                  pl.BlockSpec((H, C2), lambda i: (jnp.maximum(i * (T // H) - 1, 0), 0)),
                  pl.BlockSpec((K, C), lambda i: (0, 0)), vec, vec, vec],
        out_specs=[tile, tile],
        out_shape=[jax.ShapeDtypeStruct((L, C), BF16), jax.ShapeDtypeStruct((L, C), F32)],
        scratch_shapes=[pltpu.VMEM((T + H, C), F32), pltpu.VMEM((7, T + SHIFT_TAIL, C), F32)],
        args=(cacg, cacg, dw_w, dw_b, ln_g, ln_b))
    return out, y, arrived


def _conv_bwd(cacg, y, dmixed, dw_w, ln_g, ln_b, name, job=None):
    L, C2 = cacg.shape
    C = C2 // 2
    T = _conv_tile(L)
    H = CONV_HALO
    K = dw_w.shape[0]
    nt = L // T
    TE = T + H

    CH = CONV_CHUNK

    def body(x_ref, prev_ref, y_ref, ynext_ref, d_ref, dnext_ref, w_ref, g_ref, beta_ref,
             dca_ref, dcg_ref, dwt_ref, db_ref, dg_ref, dbeta_ref, u_ref, us_ref, dy_ref, dys_ref):
        i = pl.program_id(0)
        last = i == nt - 1

        @pl.when(i == 0)
        def _():
            dwt_ref[...] = jnp.zeros_like(dwt_ref)
            db_ref[...] = jnp.zeros_like(db_ref)
            dg_ref[...] = jnp.zeros_like(dg_ref)
            dbeta_ref[...] = jnp.zeros_like(dbeta_ref)

        u_ref[0:H, :] = jnp.where(i > 0, _glu(prev_ref[...], C), 0.0)
        u_ref[H:, :] = _glu(x_ref[...], C)
        _fill_shifted(u_ref, us_ref)
        dg_acc = jnp.zeros((1, C), F32)
        dbeta_acc = jnp.zeros((1, C), F32)
        db_acc = jnp.zeros((1, C), F32)
        for c0 in range(0, TE, CH):
            y = y_ref[c0:c0 + CH, :] if c0 < T else ynext_ref[c0 - T:c0 - T + CH, :]
            mu = jnp.mean(y, axis=-1, keepdims=True)
            yc = y - mu
            rstd = lax.rsqrt(jnp.mean(yc * yc, axis=-1, keepdims=True) + EPS)
            yh = yc * rstd
            ln = yh * g_ref[...] + beta_ref[...]
            s = _sigmoid(ln)
            dout = d_ref[c0:c0 + CH, :] if c0 < T else jnp.where(last, 0.0, dnext_ref[c0 - T:c0 - T + CH, :])
            dln = dout * (s * (1.0 + ln * (1.0 - s)))
            dyh = dln * g_ref[...]
            dy = rstd * (dyh - jnp.mean(dyh, axis=-1, keepdims=True)
                         - yh * jnp.mean(dyh * yh, axis=-1, keepdims=True))
            dy_ref[c0:c0 + CH, :] = dy
            if c0 < T:
                dg_acc = dg_acc + jnp.sum(dln * yh, axis=0, keepdims=True)
                dbeta_acc = dbeta_acc + jnp.sum(dln, axis=0, keepdims=True)
                db_acc = db_acc + jnp.sum(dy, axis=0, keepdims=True)
        dg_ref[...] += dg_acc
        dbeta_ref[...] += dbeta_acc
        db_ref[...] += db_acc
        _fill_shifted(dy_ref, dys_ref)
        for k in range(K):
            dwt_ref[k:k + 1, :] += jnp.sum(
                dy_ref[0:T, :] * _rows_at(u_ref, us_ref, H - (K - 1) + k, T), axis=0, keepdims=True)
        for c0 in range(0, T, CH):
            du = jnp.zeros((CH, C), F32)
            for k in range(K):
                du = du + w_ref[k:k + 1, :] * _rows_at(dy_ref, dys_ref, c0 + (K - 1) - k, CH)
            x = x_ref[c0:c0 + CH, :]
            sg = _sigmoid(x[:, C:])
            dca_ref[c0:c0 + CH, :] = (du * sg).astype(dca_ref.dtype)
            dcg_ref[c0:c0 + CH, :] = (du * x[:, :C] * sg * (1.0 - sg)).astype(dcg_ref.dtype)

    nh = L // H
    vec = pl.BlockSpec((1, C), lambda i: (0, 0))
    row = pl.BlockSpec((T, C), lambda i: (i, 0))
    after = lambda i: jnp.minimum((i + 1) * (T // H), nh - 1)
    return _call_with_job(
        body, job, name=name, grid=(nt,),
        in_specs=[pl.BlockSpec((T, C2), lambda i: (i, 0)),
                  pl.BlockSpec((H, C2), lambda i: (jnp.maximum(i * (T // H) - 1, 0), 0)),
                  row, pl.BlockSpec((H, C), lambda i: (after(i), 0)),
                  pl.BlockSpec((T, C), lambda i: (i, 1)), pl.BlockSpec((H, C), lambda i: (after(i), 1)),
                  pl.BlockSpec((K, C), lambda i: (0, 0)), vec, vec],
        out_specs=[row, row, pl.BlockSpec((H, C), lambda i: (0, 0)), vec, vec, vec],
        out_shape=[jax.ShapeDtypeStruct((L, C), BF16), jax.ShapeDtypeStruct((L, C), BF16),
                   jax.ShapeDtypeStruct((H, C), F32), jax.ShapeDtypeStruct((1, C), F32),
                   jax.ShapeDtypeStruct((1, C), F32), jax.ShapeDtypeStruct((1, C), F32)],
        scratch_shapes=[pltpu.VMEM((T + H, C), F32), pltpu.VMEM((7, T + SHIFT_TAIL, C), F32),
                        pltpu.VMEM((TE, C), F32), pltpu.VMEM((7, T + SHIFT_TAIL, C), F32)],
        args=(cacg, cacg, y, y, dmixed, dmixed, dw_w, ln_g, ln_b))


def _local_step(h0, target, n_meta, seq, norms, conv_p, wts, final_g, gather_behind, reducer):
    mix_g, ffn_g = norms
    dw_w, dw_b, ln_g, ln_b = conv_p
    depth = mix_g.shape[0]
    C = dw_b.shape[-1]
    sbw = (wts["w_in"][0].shape[-1] - 2 * C) // 3
    assert sbw == C, "the mixer halves must have equal width"
    row = lambda a, i: a[i][None, :]

    h = h0
    saved = []
    for i in range(depth):
        hn, proj_qkv, cacg = _norm_in_proj(h, row(mix_g, i), wts["w_in"][i], 3 * sbw, f"in_proj_{i}")
        def hosting(kind):
            job, keys = gather_behind.get((kind, i), (None, ()))

            def sink(arrived):
                for (wname, wl), arr in zip(keys, arrived):
                    wts[wname][wl] = arr

            return job, sink

        job, sink = hosting("attn")
        attn, arrived = _attn_fwd(proj_qkv, sbw, f"attn_fwd_{i}", job)
        sink(arrived)
        job, sink = hosting("conv")
        conv, conv_y, arrived = _conv_fwd(cacg, dw_w[i], row(dw_b, i), row(ln_g, i), row(ln_b, i),
                                          f"conv_fwd_{i}", job)
        sink(arrived)
        h_mid = _mm_rows([(attn, wts["w_out"][i], 0), (conv, wts["w_out"][i], 1)], f"out_proj_{i}", residual=h)
        job, sink = hosting("ffn")
        (hn2, g, u, act), arrived = _ffn_up(h_mid, row(ffn_g, i), wts["w_gate_t"][i], wts["w_up_t"][i],
                                            f"ffn_up_{i}", job)
        sink(arrived)
        job, sink = hosting("down")
        h_out = _mm_rows([(act, wts["w_down"][i], 0)], f"down_{i}", residual=h_mid, job=job)
        if job is not None:
            h_out, arrived = h_out
            sink(arrived)
        saved.append((h, hn, proj_qkv, cacg, attn, conv, conv_y, h_mid, hn2, g, u, act))
        h = h_out

    loss, dh, d_final_g = _loss_head(h, final_g[None, :], target, n_meta, seq, "loss_head")

    grads = {k: [None] * depth for k in ("mix_g", "ffn_g", "dw_w", "dw_b", "ln_g", "ln_b")}
    for i in reversed(range(depth)):
        h_in, hn, proj_qkv, cacg, attn, conv, conv_y, h_mid, hn2, g, u, act = saved[i]
        big = {}
        dg, du = _ffn_down_bwd(dh, wts["w_down"][i], g, u, f"ffn_down_bwd_{i}")
        big["w_down"] = _mm_tn(act, dh, f"dw_down_{i}", col_sharded=False)
        big["w_gate_t"] = _mm_tn(dg, hn2, f"dw_gate_{i}", col_sharded=False)
        big["w_up_t"] = _mm_tn(du, hn2, f"dw_up_{i}", col_sharded=False)
        sib_job, sib_sink = reducer.to_sibling(i, big)
        dh, d_ffn, arrived = _mm_rows([(dg, wts["w_gate_t"][i], 0), (du, wts["w_up_t"][i], 0)], f"d_hn2_{i}",
                                      norm_bwd=(h_mid, row(ffn_g, i), dh), job=sib_job)
        sib_sink(arrived)
        dmixed = _mm_rows([(dh, wts["w_out"][i], 0)], f"d_mixed_{i}", transposed=True)
        dw_out = jnp.concatenate([_mm_tn(attn, dh, f"dw_out_attn_{i}", col_sharded=False, chips=2),
                                  _mm_tn(conv, dh, f"dw_out_conv_{i}", col_sharded=False, chips=2)], axis=0)
        sib_job, sib_sink = reducer.to_sibling(i, {"w_out": dw_out})
        x_job, x_sink = reducer.take()
        (dq, dk, dv), arrived = _attn_bwd(proj_qkv, attn, dmixed, sbw, f"attn_bwd_{i}",
                                          _merge_jobs([sib_job, x_job]))
        sib_sink(arrived[:len(sib_job.out_shapes)])
        x_sink(arrived[len(sib_job.out_shapes):])
        x_job, x_sink = reducer.take()
        (dca, dcg, d_dw, d_b, d_lg, d_lb), arrived = _conv_bwd(
            cacg, conv_y, dmixed, dw_w[i], row(ln_g, i), row(ln_b, i), f"conv_bwd_{i}", x_job)
        x_sink(arrived)
        grads["dw_w"][i] = d_dw
        grads["dw_b"][i], grads["ln_g"][i], grads["ln_b"][i] = d_b, d_lg, d_lb
        dproj = jnp.concatenate([dq, dk, dv, dca, dcg], axis=1)
        reducer.ready(i, {"w_in": _mm_tn(hn, dproj, f"dw_in_{i}", col_sharded=True)})
        x_job, x_sink = reducer.take() if i == 0 else (None, None)
        res = _mm_rows([(dproj, wts["w_in"][i], 0)], f"d_hn_{i}", transposed=True,
                       norm_bwd=(h_in, row(mix_g, i), dh), job=x_job)
        dh, d_mix = res[:2]
        if x_job is not None:
            x_sink(res[2])
        grads["mix_g"][i], grads["ffn_g"][i] = d_mix, d_ffn
    grads["final_g"] = d_final_g
    return loss, dh, grads


ANY = pl.BlockSpec(memory_space=pl.ANY)


def _position():
    return lax.axis_index("x"), lax.axis_index("y"), lax.axis_index("c")


def _chip_at(x, y, k):
    return (1 - x if k & 2 else x), (1 - y if k & 1 else y)


def _half_rows(ref, half, rows, base=0):
    start = pl.multiple_of(base + half * rows, 8)
    lead = (slice(None),) * (len(ref.shape) - 2)
    return ref.at[(*lead, pl.ds(start, rows), slice(None))]


def _gather_job(fulls, shard_shapes, col_sharded):
    n = len(fulls)

    def tools(f_refs, send_sems, recv_sems):
        def block(wi, chip, half):
            _, R, C = shard_shapes[wi]
            if col_sharded[wi]:
                cols = pl.ds(pl.multiple_of(chip * C, LANES), C)
                return f_refs[wi].at[:, pl.ds(pl.multiple_of(half * (R // 2), 8), R // 2), cols]
            return _half_rows(f_refs[wi], half, R // 2, base=chip * R)

        def copy(wi, slot, blk, to):
            return pltpu.make_async_remote_copy(
                src_ref=blk, dst_ref=blk, send_sem=send_sems.at[6 * wi + slot],
                recv_sem=recv_sems.at[6 * wi + slot], device_id=to, device_id_type=MESH)

        return block, copy

    def start(_, f_refs, send_sems, recv_sems):
        block, copy = tools(f_refs, send_sems, recv_sems)
        x, y, c = _position()
        me = 2 * x + y
        for wi in range(n):
            for k in (1, 2, 3):
                copy(wi, k - 1, block(wi, me, c), (*_chip_at(x, y, k), c)).start()

    def finish(_, f_refs, send_sems, recv_sems):
        block, copy = tools(f_refs, send_sems, recv_sems)
        x, y, c = _position()
        me = 2 * x + y
        for wi in range(n):
            for k in (1, 2, 3):
                landed = block(wi, me ^ k, c)
                copy(wi, k - 1, landed, (x, y, c)).wait_recv()
                copy(wi, 2 + k, landed, (x, y, 1 - c)).start()
        for wi in range(n):
            for k in (1, 2, 3):
                copy(wi, 2 + k, block(wi, me ^ k, 1 - c), (x, y, c)).wait_recv()
        for wi in range(n):
            for k in (1, 2, 3):
                copy(wi, k - 1, block(wi, me, c), (x, y, c)).wait_send()
                copy(wi, 2 + k, block(wi, me ^ k, c), (x, y, c)).wait_send()

    return _CommJob(fulls, [jax.ShapeDtypeStruct(f.shape, f.dtype) for f in fulls], {i: i for i in range(n)},
                    6 * n, start, finish)


def _place_shard(w, layer, chip, col_sharded, dtype, name):
    _, R, C = w.shape
    tr = _pick(R, (256, 352, 128, 48))
    nr = R // tr

    def body(chip_ref, w_ref, o_ref):
        o_ref[...] = w_ref[...].astype(dtype)

    if col_sharded:
        shape = (1, R, N_CHIPS * C)
        out_spec = pl.BlockSpec((None, tr, C), lambda r, chip_ref: (0, r, chip_ref[0]))
    else:
        shape = (1, N_CHIPS * R, C)
        out_spec = pl.BlockSpec((None, tr, C), lambda r, chip_ref: (0, chip_ref[0] * nr + r, 0))
    grid_spec = pltpu.PrefetchScalarGridSpec(
        num_scalar_prefetch=1, grid=(nr,),
        in_specs=[pl.BlockSpec((None, tr, C), lambda r, chip_ref: (layer, r, 0))], out_specs=out_spec)
    return pl.pallas_call(
        body, name=name, grid_spec=grid_spec, out_shape=jax.ShapeDtypeStruct(shape, dtype),
        compiler_params=_params(("parallel",)),
    )(chip, w)


def _sibling_job(grads):
    n = len(grads)

    def copies(g_refs, l_refs, send_sems, recv_sems):
        x, y, c = _position()
        return [pltpu.make_async_remote_copy(
            src_ref=_half_rows(g_refs[wi], 1 - c, grads[wi].shape[1] // 2), dst_ref=l_refs[wi],
            send_sem=send_sems.at[wi], recv_sem=recv_sems.at[wi],
            device_id=(x, y, 1 - c), device_id_type=MESH) for wi in range(n)]

    def start(*refs):
        for cp in copies(*refs):
            cp.start()

    def finish(*refs):
        for cp in copies(*refs):
            cp.wait()

    outs = [jax.ShapeDtypeStruct((g.shape[0], g.shape[1] // 2, g.shape[2]), g.dtype) for g in grads]
    return _CommJob(grads, outs, {}, n, start, finish)


def _chip_sum(g, landed, core, name):
    _, R, C = g.shape
    hr = R // 2
    tr = _pick(hr, (256, 352, 128))
    nr = hr // tr

    def body(c_ref, g_ref, l_ref, o_ref):
        o_ref[...] = (g_ref[...] + l_ref[...]).astype(BF16)

    grid_spec = pltpu.PrefetchScalarGridSpec(
        num_scalar_prefetch=1, grid=(N_CHIPS, nr),
        in_specs=[pl.BlockSpec((None, tr, C), lambda j, r, c_ref: (j, c_ref[0] * nr + r, 0)),
                  pl.BlockSpec((None, tr, C), lambda j, r, c_ref: (j, r, 0))],
        out_specs=pl.BlockSpec((None, tr, C), lambda j, r, c_ref: (j, r, 0)))
    return pl.pallas_call(
        body, name=name, grid_spec=grid_spec, out_shape=jax.ShapeDtypeStruct((N_CHIPS, hr, C), BF16),
        compiler_params=_params(("parallel", "parallel")),
    )(core, g, landed)


def _across_job(parts):
    n = len(parts)

    def copy(p_refs, l_refs, send_sems, recv_sems, wi, k, to):
        x, y, _ = _position()
        me = 2 * x + y
        return pltpu.make_async_remote_copy(
            src_ref=p_refs[wi].at[me ^ k], dst_ref=l_refs[wi].at[me],
            send_sem=send_sems.at[3 * wi + k - 1], recv_sem=recv_sems.at[3 * wi + k - 1],
            device_id=to, device_id_type=MESH)

    def start(p_refs, l_refs, send_sems, recv_sems):
        x, y, c = _position()
        for wi in range(n):
            for k in (1, 2, 3):
                copy(p_refs, l_refs, send_sems, recv_sems, wi, k, (*_chip_at(x, y, k), c)).start()

    def finish(p_refs, l_refs, send_sems, recv_sems):
        x, y, c = _position()
        me = 2 * x + y
        for wi in range(n):
            for k in (1, 2, 3):
                slot = l_refs[wi].at[me ^ k]
                pltpu.make_async_remote_copy(
                    src_ref=slot, dst_ref=slot, send_sem=send_sems.at[3 * wi + k - 1],
                    recv_sem=recv_sems.at[3 * wi + k - 1], device_id=(x, y, c), device_id_type=MESH).wait_recv()
        for wi in range(n):
            for k in (1, 2, 3):
                copy(p_refs, l_refs, send_sems, recv_sems, wi, k, (x, y, c)).wait_send()

    return _CommJob(parts, [jax.ShapeDtypeStruct(p.shape, p.dtype) for p in parts], {}, 3 * n, start, finish)


class _Reducer:
    def __init__(self, core):
        self.core, self.parts, self.across, self.pending = core, {}, {}, []

    def to_sibling(self, layer, big):
        names = list(big)
        flat = [big[k] for k in names]

        def sink(landed):
            for k, g, la in zip(names, flat, landed):
                self.parts[k, layer] = _chip_sum(g, la, self.core, f"chip_sum_{k}_{layer}")
                self.pending.append((k, layer))

        return _sibling_job(flat), sink

    def ready(self, layer, big):
        job, sink = self.to_sibling(layer, big)
        sink(_run_job(job, f"grads_to_sibling_{next(iter(big))}_{layer}"))

    def take(self):
        keys, self.pending = self.pending, []
        if not keys:
            return None, lambda results: None

        def sink(results):
            self.across.update(zip(keys, results))

        return _across_job([self.parts[key] for key in keys]), sink


def _sum_chips(parts, landed, where, layer, depth, prev, name):
    _, hr, C = landed.shape
    tr = _pick(hr, (256, 352, 128))
    nr = hr // tr

    def body(*refs):
        own_ref, slots, o_ref = refs[1], refs[2:2 + N_CHIPS], refs[-1]
        chip = refs[0][0]
        total = None
        for q in range(N_CHIPS):
            term = jnp.where(chip == q, own_ref[...], slots[q][...]).astype(F32)
            total = term if total is None else total + term
        o_ref[...] = total

    def slot_spec(q):
        return pl.BlockSpec((None, tr, C), lambda r, w: (jnp.where(w[0] == q, (q + 1) % N_CHIPS, q), r, 0))

    in_specs = [pl.BlockSpec((None, tr, C), lambda r, w: (w[0], r, 0))] + [slot_spec(q) for q in range(N_CHIPS)]
    args = [where, parts] + [landed] * N_CHIPS
    aliases = {}
    if prev is not None:
        in_specs.append(ANY)
        args.append(prev)
        aliases = {len(args) - 1: 0}
    grid_spec = pltpu.PrefetchScalarGridSpec(
        num_scalar_prefetch=1, grid=(nr,), in_specs=in_specs,
        out_specs=pl.BlockSpec((None, tr, C), lambda r, w: (layer, w[1] * nr + r, 0)))
    return pl.pallas_call(
        body, name=name, grid_spec=grid_spec, out_shape=jax.ShapeDtypeStruct((depth, 2 * hr, C), F32),
        input_output_aliases=aliases, compiler_params=_params(("parallel",)),
    )(*args)


def _rs_join_halves(reduced):
    n = len(reduced)

    def body(*refs):
        o_refs = refs[n:2 * n]
        send_sems, recv_sems = refs[2 * n:]
        x, y, c = _position()
        sent = []
        for wi in range(n):
            hr = reduced[wi].shape[1] // 2
            mine = _half_rows(o_refs[wi], c, hr)
            cp = pltpu.make_async_remote_copy(
                src_ref=mine, dst_ref=mine, send_sem=send_sems.at[wi], recv_sem=recv_sems.at[wi],
                device_id=(x, y, 1 - c), device_id_type=MESH)
            cp.start()
            sent.append(cp)
        for wi in range(n):
            hr = reduced[wi].shape[1] // 2
            theirs = _half_rows(o_refs[wi], 1 - c, hr)
            pltpu.make_async_remote_copy(
                src_ref=theirs, dst_ref=theirs, send_sem=send_sems.at[wi], recv_sem=recv_sems.at[wi],
                device_id=(x, y, c), device_id_type=MESH).wait_recv()
        for cp in sent:
            cp.wait_send()

    return pl.pallas_call(
        body, name="grads_join_halves", out_shape=[jax.ShapeDtypeStruct(r.shape, r.dtype) for r in reduced],
        in_specs=[ANY] * n, out_specs=[ANY] * n, input_output_aliases={i: i for i in range(n)},
        scratch_shapes=[pltpu.SemaphoreType.DMA((n,)), pltpu.SemaphoreType.DMA((n,))],
    )(*reduced)


def _adam_math(w, g, m, v):
    m = ADAM_B1 * m + (1.0 - ADAM_B1) * g
    v = ADAM_B2 * v + (1.0 - ADAM_B2) * jnp.square(g)
    m_hat = m / (1.0 - ADAM_B1 ** ADAM_STEP)
    v_hat = v / (1.0 - ADAM_B2 ** ADAM_STEP)
    delta = -ADAM_LR * (m_hat / (jnp.sqrt(v_hat) + ADAM_EPS) + ADAM_WD * w)
    return delta, m, v


def _small_reduce_adam(vec_grads, dww_grads, final_grad, meta_grad, state):
    vec_names = list(vec_grads)
    depth = len(dww_grads)
    D = final_grad.shape[1]
    n_meta = meta_grad.shape[0]
    taps_pad, C = dww_grads[0].shape
    names = vec_names + ["final", "meta", "dww"]
    at, row0 = 0, {}
    for k in vec_names:
        row0[k] = at
        at += depth
    row0["final"] = at
    at = -(-(at + 1) // 8) * 8
    row0["meta"] = at
    at += -(-n_meta // 8) * 8
    row0["dww"] = at
    rows = at + depth * taps_pad
    lanes = max(D, C)
    n_g = len(vec_names) * depth + depth + 2

    def body(*refs):
        g_refs = refs[:n_g]
        st = refs[n_g:n_g + 3 * len(names)]
        outs = refs[n_g + 3 * len(names):n_g + 7 * len(names)]
        slab, land, send_sems, recv_sems = refs[n_g + 7 * len(names):]
        x, y, c = _position()
        me = 4 * x + 2 * y + c
        chip = 2 * x + y
        slab[...] = jnp.zeros_like(slab)
        it = iter(g_refs)
        for k in vec_names:
            for l in range(depth):
                g_ref = next(it)
                slab[row0[k] + l:row0[k] + l + 1, 0:g_ref.shape[1]] = g_ref[...]
        for l in range(depth):
            slab[row0["dww"] + l * taps_pad:row0["dww"] + (l + 1) * taps_pad, 0:C] = next(it)[...]
        slab[row0["final"]:row0["final"] + 1, 0:D] = next(it)[...]
        slab[row0["meta"]:row0["meta"] + n_meta, 0:D] = next(it)[...]
        sent = []
        for k in range(1, N_DEV):
            to = (1 - x if k & 4 else x, 1 - y if k & 2 else y, 1 - c if k & 1 else c)
            cp = pltpu.make_async_remote_copy(
                src_ref=slab, dst_ref=land.at[k], send_sem=send_sems.at[k - 1], recv_sem=recv_sems.at[k - 1],
                device_id=to, device_id_type=MESH)
            cp.start()
            sent.append(cp)
        land[0] = slab[...]
        for cp in sent:
            cp.wait_recv()
        for cp in sent:
            cp.wait_send()
        total = land[me]
        for e in range(1, N_DEV):
            total = total + land[me ^ e]
        slab[...] = total

        def mine(r0, n_rows, width):
            got = slab[r0:r0 + n_rows, 0:width]
            for j in range(1, N_CHIPS):
                got = jnp.where(chip == j, slab[r0:r0 + n_rows, j * width:(j + 1) * width], got)
            return got

        def update(i, g, index=()):
            w_ref, m_ref, v_ref = st[3 * i:3 * i + 3]
            o = outs[4 * i:4 * i + 4]
            at = index if index else Ellipsis
            res = (g,) + _adam_math(w_ref[at], g, m_ref[at], v_ref[at])
            for o_ref, val in zip(o, res):
                o_ref[at] = val

        for i, k in enumerate(vec_names):
            width = st[3 * i].shape[1]
            update(i, slab[row0[k]:row0[k] + depth, 0:width])
        base = len(vec_names)
        update(base, slab[row0["final"]:row0["final"] + 1, 0:D])
        update(base + 1, mine(row0["meta"], n_meta, D // N_CHIPS))
        taps = st[3 * (base + 2)].shape[1]
        for l in range(depth):
            update(base + 2, mine(row0["dww"] + l * taps_pad, taps, C // N_CHIPS), (l,))

    flat_g = [g for k in vec_names for g in vec_grads[k]] + list(dww_grads) + [final_grad, meta_grad]
    flat_state = [a for k in names for a in state[k]]
    vmem = pl.BlockSpec(memory_space=pltpu.VMEM)
    out_shape = [jax.ShapeDtypeStruct(state[k][0].shape, F32) for k in names for _ in range(4)]
    res = pl.pallas_call(
        body, name="small_reduce_adam", out_shape=out_shape,
        in_specs=[vmem] * (len(flat_g) + len(flat_state)), out_specs=[vmem] * len(out_shape),
        scratch_shapes=[pltpu.VMEM((rows, lanes), F32), pltpu.VMEM((N_DEV, rows, lanes), F32),
                        pltpu.SemaphoreType.DMA((N_DEV - 1,)), pltpu.SemaphoreType.DMA((N_DEV - 1,))],
    )(*flat_g, *flat_state)
    return {k: tuple(res[4 * i:4 * i + 4]) for i, k in enumerate(names)}


def _adam(w, g, m, v, name):
    def body(w_ref, g_ref, m_ref, v_ref, d_ref, nm_ref, nv_ref):
        d_ref[...], nm_ref[...], nv_ref[...] = _adam_math(w_ref[...], g_ref[...], m_ref[...], v_ref[...])

    if w.ndim == 3:
        lyr, R, C = w.shape
        tr = _pick(R, (256, 352, 128))
        blk = pl.BlockSpec((None, tr, C), lambda l, r: (l, r, 0))
        grid, sem = (lyr, R // tr), ("parallel", "parallel")
    else:
        blk = pl.BlockSpec(w.shape, lambda: (0, 0))
        grid, sem = (), None
    return pl.pallas_call(
        body, name=name, grid=grid, in_specs=[blk] * 4, out_specs=[blk] * 3,
        out_shape=[jax.ShapeDtypeStruct(w.shape, F32)] * 3, compiler_params=_params(sem),
    )(w, g, m, v)


def _rows(a, pad_to=8):
    r = a.reshape(-1, LANES)
    extra = (-r.shape[0]) % pad_to
    return jnp.pad(r, ((0, extra), (0, 0))) if extra else r


def _pack(arrays):
    return jnp.concatenate([_rows(a) for a in arrays], axis=0)


def _unpack(slab, shapes):
    out, at = [], 0
    for shp in shapes:
        nrow = math.prod(shp) // LANES
        out.append(slab[at:at + nrow].reshape(shp))
        at += nrow + (-nrow) % 8
    return out


BIG = ("w_in", "w_out", "w_gate_t", "w_up_t", "w_down")
BIG_COL_SHARDED = (True, False, False, False, False)
TRANSPOSED = {"w_gate_t": "w_gate", "w_up_t": "w_up"}


def kernel(x, meta_tokens, mix_norm_g, w_in, conv_dw_w, conv_dw_b, conv_ln_g, conv_ln_b, w_out, ffn_norm_g, w_gate, w_up, w_down, final_norm_g, loss_target, m_meta_tokens, m_mix_norm_g, m_w_in, m_conv_dw_w, m_conv_dw_b, m_conv_ln_g, m_conv_ln_b, m_w_out, m_ffn_norm_g, m_w_gate, m_w_up, m_w_down, m_final_norm_g, v_meta_tokens, v_mix_norm_g, v_w_in, v_conv_dw_w, v_conv_dw_b, v_conv_ln_g, v_conv_ln_b, v_w_out, v_ffn_norm_g, v_w_gate, v_w_up, v_w_down, v_final_norm_g):
    n_meta, seq = meta_tokens.shape[0], x.shape[1]
    D = x.shape[2]
    depth, taps, c_shard = conv_dw_w.shape
    C = conv_dw_b.shape[-1]
    chip = (2 * lax.axis_index("x") + lax.axis_index("y")).astype(jnp.int32)
    core = lax.axis_index("c").astype(jnp.int32).reshape(1)
    chip1 = chip.reshape(1)
    where = jnp.concatenate([chip1, core])
    big_w = dict(w_in=w_in, w_out=w_out, w_gate=w_gate, w_up=w_up, w_down=w_down)
    big_m = dict(w_in=m_w_in, w_out=m_w_out, w_gate=m_w_gate, w_up=m_w_up, w_down=m_w_down)
    big_v = dict(w_in=v_w_in, w_out=v_w_out, w_gate=v_w_gate, w_up=v_w_up, w_down=v_w_down)

    small_shard = _pack([conv_dw_w, meta_tokens])[None]
    to_send = {k: jnp.swapaxes(big_w[TRANSPOSED[k]], 1, 2) if k in TRANSPOSED else big_w[k] for k in BIG}
    col = dict(zip(BIG, BIG_COL_SHARDED))
    wts = {k: [_place_shard(to_send[k], l, chip1, col[k], BF16, f"place_{k}_{l}") for l in range(depth)] for k in BIG}
    small_placed = _place_shard(small_shard, 0, chip1, False, F32, "place_small")

    def gather_job(keys, extra=()):
        arrays = [wts[k][l] for k, l in keys] + list(extra)
        shapes = [(1,) + to_send[k].shape[1:] for k, _ in keys] + [(1,) + small_shard.shape[1:]] * len(extra)
        return _gather_job(arrays, shapes, [col[k] for k, _ in keys] + [False] * len(extra))

    first_keys = [("w_in", 0)]
    *first, small_full = _run_job(gather_job(first_keys, [small_placed]), "gather_first")
    for (k, l), arr in zip(first_keys, first):
        wts[k][l] = arr
    behind_keys = {("attn", 0): [("w_out", 0), ("w_gate_t", 0), ("w_up_t", 0)], ("conv", 0): [("w_down", 0)]}
    for l in range(1, depth):
        behind_keys["ffn", l - 1] = [("w_in", l), ("w_gate_t", l)]
        behind_keys["down", l - 1] = [("w_up_t", l)]
        behind_keys["attn", l] = [("w_out", l), ("w_down", l)]
    gather_behind = {host: (gather_job(keys), keys) for host, keys in behind_keys.items()}

    rows_shard = small_shard.shape[1]
    dw_full, meta_full = [], []
    for j in range(N_CHIPS):
        dwj, mj = _unpack(small_full[0, j * rows_shard:(j + 1) * rows_shard],
                          [conv_dw_w.shape, meta_tokens.shape])
        dw_full.append(dwj)
        meta_full.append(mj)
    dw_w_full = jnp.concatenate(dw_full, axis=2)
    meta = jnp.concatenate(meta_full, axis=1)

    L = n_meta + seq
    Lp = -(-L // QUERY_BLOCK) * QUERY_BLOCK
    h0 = jnp.concatenate([meta, x[0], jnp.zeros((Lp - L, D), F32)], axis=0)
    target = jnp.pad(loss_target[0], ((n_meta, Lp - L), (0, 0)))
    reducer = _Reducer(core)
    loss, dh0, grads = _local_step(h0, target, n_meta, seq, (mix_norm_g, ffn_norm_g),
                                   (dw_w_full, conv_dw_b, conv_ln_g, conv_ln_b), wts, final_norm_g,
                                   gather_behind, reducer)
    loss = lax.psum(loss[0, 0], ("x", "y", "c"))
    grad_x = dh0[n_meta:L][None]

    reduced = []
    for k in BIG:
        arr = None
        for l in range(depth):
            arr = _sum_chips(reducer.parts[k, l], reducer.across[k, l], where, l, depth, arr, f"sum_chips_{k}_{l}")
        reduced.append(arr)
    big_g = dict(zip(BIG, _rs_join_halves(reduced)))

    out_g, out_d, out_m, out_v = {}, {}, {}, {}
    for kk in BIG:
        k = TRANSPOSED.get(kk, kk)
        view = (lambda a: jnp.swapaxes(a, 1, 2)) if kk in TRANSPOSED else (lambda a: a)
        res = _adam(view(big_w[k]), big_g[kk], view(big_m[k]), view(big_v[k]), f"adam_{k}")
        out_g[k] = view(big_g[kk])
        out_d[k], out_m[k], out_v[k] = (view(a) for a in res)

    as_row = lambda a: a.reshape(1, -1)
    state = dict(mix_g=(mix_norm_g, m_mix_norm_g, v_mix_norm_g), ffn_g=(ffn_norm_g, m_ffn_norm_g, v_ffn_norm_g),
                 dw_b=(conv_dw_b, m_conv_dw_b, v_conv_dw_b), ln_g=(conv_ln_g, m_conv_ln_g, v_conv_ln_g),
                 ln_b=(conv_ln_b, m_conv_ln_b, v_conv_ln_b),
                 final=(as_row(final_norm_g), as_row(m_final_norm_g), as_row(v_final_norm_g)),
                 meta=(meta_tokens, m_meta_tokens, v_meta_tokens), dww=(conv_dw_w, m_conv_dw_w, v_conv_dw_w))
    vec_names = ("mix_g", "ffn_g", "dw_b", "ln_g", "ln_b")
    small = _small_reduce_adam({k: grads[k] for k in vec_names}, grads["dw_w"], grads["final_g"], dh0[:n_meta], state)
    out_name = dict(mix_g="mix_norm_g", ffn_g="ffn_norm_g", dw_b="conv_dw_b", ln_g="conv_ln_g", ln_b="conv_ln_b",
                    final="final_norm_g", meta="meta_tokens", dww="conv_dw_w")
    for k, res in small.items():
        if k == "final":
            res = tuple(a.reshape(-1) for a in res)
        out_g[out_name[k]], out_d[out_name[k]], out_m[out_name[k]], out_v[out_name[k]] = res

    order = ("meta_tokens", "mix_norm_g", "w_in", "conv_dw_w", "conv_dw_b", "conv_ln_g", "conv_ln_b", "w_out",
             "ffn_norm_g", "w_gate", "w_up", "w_down", "final_norm_g")
    return (loss, grad_x, *[out_g[k] for k in order], *[out_d[k] for k in order],
            *[out_m[k] for k in order], *[out_v[k] for k in order])
```

```python
import functools
import math

import jax
import jax.numpy as jnp
from jax import lax
from jax.experimental import pallas as pl
from jax.experimental.pallas import tpu as pltpu

F32 = jnp.float32
BF16 = jnp.bfloat16
MESH = pl.DeviceIdType.MESH

EPS = 1e-6
QUERY_BLOCK = 128
LANES = 128
HEAD_DIM = 64
LOG_STICK_FLOOR = -40.0
CONV_HALO = 32
N_CHIPS = 4
N_DEV = 8
VMEM_LIMIT = 56 * 1024 * 1024

ADAM_LR = 0.001
ADAM_B1 = 0.9
ADAM_B2 = 0.999
ADAM_EPS = 1e-08
ADAM_WD = 0.01
ADAM_STEP = 10


def _pick(n, prefs):
    for p in prefs:
        if n % p == 0:
            return p
    return n


def _params(sem=None):
    return pltpu.CompilerParams(dimension_semantics=sem, vmem_limit_bytes=VMEM_LIMIT)


def _sigmoid(x):
    return 1.0 / (1.0 + jnp.exp(-x))


def _loss_head(h, g, target, n_meta, seq, name):
    L, D = h.shape
    T = _pick(L, (384, 128))

    def body(h_ref, g_ref, t_ref, loss_ref, dh_ref, dg_ref):
        i = pl.program_id(0)
        x = h_ref[...]
        gv = g_ref[...]
        r = lax.rsqrt(jnp.mean(x * x, axis=-1, keepdims=True) + EPS)
        xh = x * r
        y = xh * gv
        rows = i * T + lax.broadcasted_iota(jnp.int32, (T, 1), 0)
        live = (rows >= n_meta) & (rows < n_meta + seq)
        diff = jnp.where(live, y - t_ref[...], 0.0)
        dyv = diff / D
        dxh = dyv * gv
        dh_ref[...] = r * (dxh - xh * jnp.mean(dxh * xh, axis=-1, keepdims=True))

        @pl.when(i == 0)
        def _():
            dg_ref[...] = jnp.zeros_like(dg_ref)
            loss_ref[...] = jnp.zeros_like(loss_ref)

        dg_ref[...] += jnp.sum(dyv * xh, axis=0, keepdims=True)
        per_row = jnp.mean(diff * diff, axis=-1, keepdims=True)
        loss_ref[...] += 0.5 * jnp.sum(per_row, axis=0, keepdims=True)

    row = pl.BlockSpec((T, D), lambda i: (i, 0))
    vec = pl.BlockSpec((1, D), lambda i: (0, 0))
    one = pl.BlockSpec((1, 1), lambda i: (0, 0))
    return pl.pallas_call(
        body, name=name, grid=(L // T,),
        in_specs=[row, vec, row], out_specs=[one, row, vec],
        out_shape=[jax.ShapeDtypeStruct((1, 1), F32), jax.ShapeDtypeStruct((L, D), F32),
                   jax.ShapeDtypeStruct((1, D), F32)],
        compiler_params=_params(("arbitrary",)),
    )(h, g, target)


def _ffn_tiles(M, F):
    return _pick(M, (352, 384, 128)), _pick(F, (1408, 512, 256, 128))


def _resident(shape):
    return pl.BlockSpec((None,) + tuple(shape[1:]), lambda *_: (0,) * len(shape), pipeline_mode=pl.Buffered(1))


def _norm_in_proj(h, g, w_in, n_qkv, name):
    M, D = h.shape
    N = w_in.shape[2]
    tm = _pick(M, (352, 384, 128))
    tn = _pick(math.gcd(n_qkv, N - n_qkv), (512, 256, 128))

    def body(h_ref, g_ref, w_ref, hn_ref, qkv_ref, rest_ref):
        x = h_ref[...]
        r = lax.rsqrt(jnp.mean(x * x, axis=-1, keepdims=True) + EPS)
        hv = (x * r * g_ref[...]).astype(BF16)
        hn_ref[...] = hv
        for c in range(0, N, tn):
            out = jnp.dot(hv, w_ref[:, c:c + tn], preferred_element_type=F32)
            if c < n_qkv:
                qkv_ref[:, c:c + tn] = out.astype(qkv_ref.dtype)
            else:
                rest_ref[:, c - n_qkv:c - n_qkv + tn] = out

    rows = pl.BlockSpec((tm, D), lambda i: (i, 0))
    return pl.pallas_call(
        body, name=name, grid=(M // tm,),
        in_specs=[rows, pl.BlockSpec((1, D), lambda i: (0, 0)), _resident(w_in.shape)],
        out_specs=[rows, pl.BlockSpec((tm, n_qkv), lambda i: (i, 0)), pl.BlockSpec((tm, N - n_qkv), lambda i: (i, 0))],
        out_shape=[jax.ShapeDtypeStruct((M, D), BF16), jax.ShapeDtypeStruct((M, n_qkv), BF16),
                   jax.ShapeDtypeStruct((M, N - n_qkv), F32)],
        compiler_params=_params(("parallel",)),
    )(h, g, w_in)


def _ffn_up(h, norm_g, w_gate_t, w_up_t, name, job=None):
    M, D = h.shape
    F = w_gate_t.shape[1]
    tm, tf = _ffn_tiles(M, F)
    nt = (((1,), (1,)), ((), ()))

    def body(h_ref, ng_ref, wg_ref, wu_ref, hn_ref, g_ref, u_ref, a_ref):
        x = h_ref[...]
        r = lax.rsqrt(jnp.mean(x * x, axis=-1, keepdims=True) + EPS)
        hv = (x * r * ng_ref[...]).astype(BF16)
        hn_ref[...] = hv
        for c in range(0, F, tf):
            gv = lax.dot_general(hv, wg_ref[c:c + tf, :], nt, preferred_element_type=F32)
            uv = lax.dot_general(hv, wu_ref[c:c + tf, :], nt, preferred_element_type=F32)
            g_ref[:, c:c + tf] = gv.astype(g_ref.dtype)
            u_ref[:, c:c + tf] = uv.astype(u_ref.dtype)
            a_ref[:, c:c + tf] = (gv * _sigmoid(gv) * uv).astype(a_ref.dtype)

    rows = pl.BlockSpec((tm, D), lambda i: (i, 0))
    wide = pl.BlockSpec((tm, F), lambda i: (i, 0))
    return _call_with_job(
        body, job, name=name, grid=(M // tm,),
        in_specs=[rows, pl.BlockSpec((1, D), lambda i: (0, 0)), _resident(w_gate_t.shape), _resident(w_up_t.shape)],
        out_specs=[rows, wide, wide, wide],
        out_shape=[jax.ShapeDtypeStruct((M, D), BF16), jax.ShapeDtypeStruct((M, F), BF16),
                   jax.ShapeDtypeStruct((M, F), BF16), jax.ShapeDtypeStruct((M, F), BF16)],
        scratch_shapes=[], args=(h, norm_g, w_gate_t, w_up_t))


def _ffn_down_bwd(dh, w_down, g, u, name):
    M, D = dh.shape
    F = g.shape[1]
    tm, tf = _ffn_tiles(M, F)

    def body(d_ref, w_ref, g_ref, u_ref, dg_ref, du_ref):
        dhv = d_ref[...].astype(BF16)
        for c in range(0, F, tf):
            dv = lax.dot_general(dhv, w_ref[c:c + tf, :], (((1,), (1,)), ((), ())), preferred_element_type=F32)
            gv = g_ref[:, c:c + tf].astype(F32)
            s = _sigmoid(gv)
            du_ref[:, c:c + tf] = (dv * (gv * s)).astype(du_ref.dtype)
            dg_ref[:, c:c + tf] = (dv * u_ref[:, c:c + tf].astype(F32)
                                   * (s * (1.0 + gv * (1.0 - s)))).astype(dg_ref.dtype)

    wide = pl.BlockSpec((tm, F), lambda i: (i, 0))
    return pl.pallas_call(
        body, name=name, grid=(M // tm,),
        in_specs=[pl.BlockSpec((tm, D), lambda i: (i, 0)), _resident(w_down.shape), wide, wide],
        out_specs=[wide, wide],
        out_shape=[jax.ShapeDtypeStruct((M, F), BF16), jax.ShapeDtypeStruct((M, F), BF16)],
        compiler_params=_params(("parallel",)),
    )(dh, w_down, g, u)


def _mm_rows(pairs, name, transposed=False, residual=None, norm_bwd=None, job=None):
    M = pairs[0][0].shape[0]
    N = pairs[0][1].shape[1 if transposed else 2]
    shallow = sum(p[0].shape[1] for p in pairs) <= 1024
    tm = _pick(M, ((1056,) if shallow else ()) + (352, 384, 128))
    tn = _pick(N, (512, 256, 128))
    n = len(pairs)

    def body(*refs):
        a_refs, w_refs, extra = refs[:n], refs[n:2 * n], refs[2 * n:]
        p_ref = extra[-1] if norm_bwd is not None else extra[-1 if residual is None else 1]
        lhs = [a_ref[...].astype(BF16) for a_ref in a_refs]
        for c in range(0, N, tn):
            acc = None
            for a, w_ref, (_, _, blk) in zip(lhs, w_refs, pairs):
                k = a.shape[1]
                if transposed:
                    d = lax.dot_general(a, w_ref[c:c + tn, blk * k:(blk + 1) * k], (((1,), (1,)), ((), ())),
                                        preferred_element_type=F32)
                else:
                    d = jnp.dot(a, w_ref[blk * k:(blk + 1) * k, c:c + tn], preferred_element_type=F32)
                acc = d if acc is None else acc + d
            if residual is not None:
                acc = acc + extra[0][:, c:c + tn]
            p_ref[:, c:c + tn] = acc
        if norm_bwd is not None:
            h_ref, g_ref, dhin_ref, dh_ref, dg_ref, _ = extra
            x = h_ref[...]
            dyv = p_ref[...]
            r = lax.rsqrt(jnp.mean(x * x, axis=-1, keepdims=True) + EPS)
            xh = x * r
            dxh = dyv * g_ref[...]
            dh_ref[...] = dhin_ref[...] + r * (dxh - xh * jnp.mean(dxh * xh, axis=-1, keepdims=True))

            @pl.when(pl.program_id(0) == 0)
            def _():
                dg_ref[...] = jnp.zeros_like(dg_ref)

            dg_ref[...] += jnp.sum(dyv * xh, axis=0, keepdims=True)

    in_specs = [pl.BlockSpec((tm, a.shape[1]), lambda i: (i, 0)) for a, _, _ in pairs]
    in_specs += [_resident(w.shape) for _, w, _ in pairs]
    args = [p[0] for p in pairs] + [p[1] for p in pairs]
    wide = pl.BlockSpec((tm, N), lambda i: (i, 0))
    vec = pl.BlockSpec((1, N), lambda i: (0, 0))
    out_specs, out_shape, scratch = [wide], [jax.ShapeDtypeStruct((M, N), F32)], []
    if residual is not None:
        in_specs.append(wide)
        args.append(residual)
    if norm_bwd is not None:
        in_specs += [wide, vec, wide]
        args += list(norm_bwd)
        out_specs.append(vec)
        out_shape.append(jax.ShapeDtypeStruct((1, N), F32))
        scratch = [pltpu.VMEM((tm, N), F32)]
    outs, arrived = _call_with_job(body, job, name=name, grid=(M // tm,), in_specs=in_specs, out_specs=out_specs,
                                   out_shape=out_shape, scratch_shapes=scratch, args=args)
    res = tuple(outs) if norm_bwd is not None else (outs[0],)
    res = res if job is None else res + (arrived,)
    return res[0] if len(res) == 1 else res


def _mm_tn(a, b, name, col_sharded, chips=N_CHIPS):
    M, K = a.shape
    N = b.shape[1]
    tm = _pick(M, (1056, 384, 128))
    tk = _pick(K, (1024, 1408, 512, 256, 128))
    tn = N // N_CHIPS if col_sharded else _pick(N, (1024, 512, 128))

    def body(a_ref, b_ref, o_ref):
        @pl.when(pl.program_id(2) == 0)
        def _():
            o_ref[...] = jnp.zeros_like(o_ref)

        o_ref[...] += lax.dot_general(a_ref[...].astype(BF16), b_ref[...].astype(BF16),
                                      (((0,), (0,)), ((), ())), preferred_element_type=F32)

    if col_sharded:
        out_shape = jax.ShapeDtypeStruct((N_CHIPS, K, tn), F32)
        out_spec = pl.BlockSpec((None, tk, tn), lambda k, j, m: (j, k, 0))
    else:
        out_shape = jax.ShapeDtypeStruct((K, N), F32)
        out_spec = pl.BlockSpec((tk, tn), lambda k, j, m: (k, j))
    out = pl.pallas_call(
        body, name=name, grid=(K // tk, N // tn, M // tm),
        in_specs=[pl.BlockSpec((tm, tk), lambda k, j, m: (m, k)), pl.BlockSpec((tm, tn), lambda k, j, m: (m, j))],
        out_specs=out_spec, out_shape=out_shape,
        compiler_params=_params(("parallel", "parallel", "arbitrary")),
    )(a, b)
    return out if col_sharded else out.reshape(chips, K // chips, N)


def _stack_heads(x, scale=None):
    lane = lax.broadcasted_iota(jnp.int32, x.shape, 1)
    zero = jnp.zeros_like(x)
    lo = jnp.where(lane < HEAD_DIM, x, zero)
    hi = jnp.where(lane < HEAD_DIM, zero, x)
    out = jnp.concatenate([lo, hi], axis=0)
    return out if scale is None else out * scale


def _unstack_heads(x2):
    qb = x2.shape[0] // 2
    lane = lax.broadcasted_iota(jnp.int32, (qb, LANES), 1)
    return jnp.where(lane < HEAD_DIM, x2[:qb], x2[qb:])


def _dot2(x, m):
    xh = x.astype(BF16)
    xl = (x - xh.astype(F32)).astype(BF16)
    return jnp.dot(xh, m, preferred_element_type=F32) + jnp.dot(xl, m, preferred_element_type=F32)


ATTN_CHUNK = 32


def _in_lockstep(staged):
    waiting, live = list(staged), []
    while waiting or live:
        if waiting:
            live.append(waiting.pop(0))
        for gen in list(live):
            if next(gen, StopIteration) is StopIteration:
                live.remove(gen)


def _row_chunks():
    return [slice(r, r + ATTN_CHUNK) for r in range(0, 2 * QUERY_BLOCK, ATTN_CHUNK)]


def _chunk_valid(rows, older):
    shape = (rows.stop - rows.start, older + QUERY_BLOCK)
    r = (rows.start + lax.broadcasted_iota(jnp.int32, shape, 0)) & (QUERY_BLOCK - 1)
    return lax.broadcasted_iota(jnp.int32, shape, 1) < r + older


def _split_to(x, hi_ref, lo_ref, rows):
    xh = x.astype(BF16)
    hi_ref[rows, :] = xh
    lo_ref[rows, :] = (x - xh.astype(F32)).astype(BF16)


def _triangle(keys, strict):
    r = lax.broadcasted_iota(jnp.int32, (keys, keys), 0)
    c = lax.broadcasted_iota(jnp.int32, (keys, keys), 1)
    return jnp.where((r > c) if strict else (r >= c), 1.0, 0.0).astype(BF16)


def _fill_attn_consts(tri_ref, mask_ref):
    n = 2 * QUERY_BLOCK
    tri_ref[0] = _triangle(n, True)
    tri_ref[1] = _triangle(n, False)
    mask_ref[...] = jnp.where(_chunk_valid(slice(0, n), QUERY_BLOCK), 1.0, 0.0)


def _valid(mask_ref, rows, older):
    cols = slice(0, 2 * QUERY_BLOCK) if older else slice(QUERY_BLOCK, 2 * QUERY_BLOCK)
    return mask_ref[rows, cols] > 0.5


def _row_total(first):
    lane = lax.broadcasted_iota(jnp.int32, first.shape, 1)
    total = jnp.sum(jnp.where(lane == 0, first, 0.0), axis=1, keepdims=True)
    return jnp.broadcast_to(total, first.shape)


def _pairs_per_step(n_pairs, most):
    return max(g for g in (1, 2, 4) if g <= most and n_pairs % g == 0)


def _lanes(g):
    return slice(g * LANES, (g + 1) * LANES)


def _sweep_older(i, step, carry_ref, first):
    def cond(state):
        n, live = state
        return jnp.logical_and(n < i, live)

    def older(state):
        n, _ = state
        step(i - 1 - n, False)
        return n + 1, jnp.max(carry_ref[...]) > LOG_STICK_FLOOR

    lax.while_loop(cond, older, (first, jnp.max(carry_ref[...]) > LOG_STICK_FLOOR))


class _CommJob:
    def __init__(self, inputs, out_shapes, aliases, n_sems, start, finish):
        self.inputs, self.out_shapes, self.aliases, self.n_sems = list(inputs), list(out_shapes), aliases, n_sems
        self.start, self.finish = start, finish


def _merge_jobs(jobs):
    jobs = [j for j in jobs if j is not None]
    if len(jobs) <= 1:
        return jobs[0] if jobs else None
    spans, aliases = [], {}
    i0 = o0 = s0 = 0
    for j in jobs:
        spans.append((i0, o0, s0))
        aliases.update({i0 + a: o0 + b for a, b in j.aliases.items()})
        i0, o0, s0 = i0 + len(j.inputs), o0 + len(j.out_shapes), s0 + j.n_sems

    def run(which):
        def go(ins, outs, send_sems, recv_sems):
            for j, (i, o, s) in zip(jobs, spans):
                getattr(j, which)(ins[i:i + len(j.inputs)], outs[o:o + len(j.out_shapes)],
                                  send_sems.at[pl.ds(s, j.n_sems)], recv_sems.at[pl.ds(s, j.n_sems)])
        return go

    return _CommJob([a for j in jobs for a in j.inputs], [s for j in jobs for s in j.out_shapes], aliases, s0,
                    run("start"), run("finish"))


def _call_with_job(core_body, job, *, name, grid, in_specs, out_specs, out_shape, scratch_shapes, args):
    sem = ("arbitrary",) * len(grid)
    if job is None:
        res = pl.pallas_call(core_body, name=name, grid=grid, in_specs=in_specs, out_specs=out_specs,
                             out_shape=out_shape, scratch_shapes=scratch_shapes, compiler_params=_params(sem))(*args)
        return list(res), []
    n_in, n_out, n_scr = len(in_specs), len(out_specs), len(scratch_shapes)
    m_in, m_out = len(job.inputs), len(job.out_shapes)

    def body(*refs):
        at = 0
        parts = []
        for count in (n_in, m_in, n_out, m_out, n_scr, 2):
            parts.append(refs[at:at + count])
            at += count
        ins, job_in, outs, job_outs, scratch, (send_sems, recv_sems) = parts
        first = functools.reduce(jnp.logical_and, [pl.program_id(a) == 0 for a in range(len(grid))])
        last = functools.reduce(jnp.logical_and, [pl.program_id(a) == grid[a] - 1 for a in range(len(grid))])

        @pl.when(first)
        def _():
            job.start(job_in, job_outs, send_sems, recv_sems)

        core_body(*ins, *outs, *scratch)

        @pl.when(last)
        def _():
            job.finish(job_in, job_outs, send_sems, recv_sems)

    res = pl.pallas_call(
        body, name=name, grid=grid, in_specs=list(in_specs) + [ANY] * m_in, out_specs=list(out_specs) + [ANY] * m_out,
        out_shape=list(out_shape) + job.out_shapes,
        input_output_aliases={n_in + a: n_out + b for a, b in job.aliases.items()},
        scratch_shapes=list(scratch_shapes) + [pltpu.SemaphoreType.DMA((job.n_sems,)), pltpu.SemaphoreType.DMA((job.n_sems,))],
        compiler_params=_params(sem),
    )(*args, *job.inputs)
    return list(res[:n_out]), list(res[n_out:])


def _run_job(job, name):
    m_in, m_out = len(job.inputs), len(job.out_shapes)

    def body(*refs):
        job_in, job_outs = refs[:m_in], refs[m_in:m_in + m_out]
        send_sems, recv_sems = refs[m_in + m_out:]
        job.start(job_in, job_outs, send_sems, recv_sems)
        job.finish(job_in, job_outs, send_sems, recv_sems)

    return list(pl.pallas_call(
        body, name=name, in_specs=[ANY] * m_in, out_specs=[ANY] * m_out, out_shape=job.out_shapes,
        input_output_aliases=dict(job.aliases),
        scratch_shapes=[pltpu.SemaphoreType.DMA((job.n_sems,)), pltpu.SemaphoreType.DMA((job.n_sems,))],
    )(*job.inputs))


def _attn_fwd(qkv, sb_width, name, job=None):
    L = qkv.shape[0]
    QB = QUERY_BLOCK
    nb = L // QB
    n_pairs = sb_width // LANES
    G = _pairs_per_step(n_pairs, 4)
    W = G * LANES
    scale = 1.0 / math.sqrt(HEAD_DIM)

    def body(q_ref, k_ref, v_ref, o_ref, acc_ref, carry_ref, f32_ref, bf16_ref, tri_ref, mask_ref):
        i = pl.program_id(1)

        @pl.when(i == 0)
        def _():
            _fill_attn_consts(tri_ref, mask_ref)

        q2 = [_stack_heads(q_ref[:, _lanes(g)], scale) for g in range(G)]
        acc_ref[...] = jnp.zeros_like(acc_ref)
        carry_ref[...] = jnp.zeros_like(carry_ref)

        def step(g, j, older, masked):
            n = older + QB
            start = pl.multiple_of(j * QB, QB)
            kb = k_ref[pl.ds(start, n), _lanes(g)]
            vb = v_ref[pl.ds(start, n), _lanes(g)]
            zs, as_, bs = (f32_ref.at[g, t, :, :n] for t in range(3))
            hi, lo = (bf16_ref.at[g, t, :, :n] for t in range(2))
            zs[...] = lax.dot_general(q2[g], kb, (((1,), (1,)), ((), ())), preferred_element_type=F32)
            yield
            for rows in _row_chunks():
                z = zs[rows, :]
                sp = jnp.log(1.0 + jnp.exp(-jnp.abs(z)))
                b = jnp.minimum(-z, 0.0) - sp
                if masked:
                    b = jnp.where(_valid(mask_ref, rows, older),b, 0.0)
                as_[rows, :] = jnp.minimum(z, 0.0) - sp
                bs[rows, :] = b
                _split_to(b, hi, lo, rows)
            yield
            tri = tri_ref[0, :n, :n]
            zs[...] = (jnp.dot(hi[...], tri, preferred_element_type=F32)
                       + jnp.dot(lo[...], tri, preferred_element_type=F32))
            yield
            for rows in _row_chunks():
                excl = zs[rows, :]
                total = _row_total(excl[:, :LANES] + bs[rows, :LANES])
                if not masked:
                    excl = excl + jnp.tile(carry_ref[g, rows, :], (1, n // LANES))
                carry_ref[g, rows, :] += total
                w = jnp.exp(as_[rows, :] + excl)
                if masked:
                    w = jnp.where(_valid(mask_ref, rows, older),w, 0.0)
                _split_to(w, hi, lo, rows)
            yield
            acc_ref[g] += (jnp.dot(hi[...], vb, preferred_element_type=F32)
                           + jnp.dot(lo[...], vb, preferred_element_type=F32))

        @pl.when(i == 0)
        def _():
            _in_lockstep([step(g, 0, 0, True) for g in range(G)])

        @pl.when(i > 0)
        def _():
            _in_lockstep([step(g, i - 1, QB, True) for g in range(G)])
            for g in range(G):
                _sweep_older(i, lambda j, _, g=g: _in_lockstep([step(g, j, 0, False)]), carry_ref.at[g], 1)

        for g in range(G):
            o_ref[:, _lanes(g)] = _unstack_heads(acc_ref[g])

    n_steps = n_pairs // G
    (out,), job_out = _call_with_job(
        body, job, name=name, grid=(n_steps, nb),
        in_specs=[pl.BlockSpec((QB, W), lambda p, i: (i, p)),
                  pl.BlockSpec((L, W), lambda p, i: (0, n_steps + p)),
                  pl.BlockSpec((L, W), lambda p, i: (0, 2 * n_steps + p))],
        out_specs=[pl.BlockSpec((QB, W), lambda p, i: (i, p))],
        out_shape=[jax.ShapeDtypeStruct((L, sb_width), F32)],
        scratch_shapes=[pltpu.VMEM((G, 2 * QB, LANES), F32), pltpu.VMEM((G, 2 * QB, LANES), F32),
                        pltpu.VMEM((G, 3, 2 * QB, 2 * QB), F32), pltpu.VMEM((G, 2, 2 * QB, 2 * QB), BF16),
                        pltpu.VMEM((2, 2 * QB, 2 * QB), BF16), pltpu.VMEM((2 * QB, 2 * QB), F32)],
        args=(qkv, qkv, qkv))
    return out, job_out


def _attn_bwd(qkv, o, dmixed, sb_width, name, job=None):
    L = qkv.shape[0]
    QB = QUERY_BLOCK
    nb = L // QB
    n_pairs = sb_width // LANES
    G = _pairs_per_step(n_pairs, 2)
    W = G * LANES
    scale = 1.0 / math.sqrt(HEAD_DIM)

    def body(q_ref, k_ref, v_ref, o_ref, do_ref, dq_ref, dk_ref, dv_ref,
             dq_acc, dk_acc, dv_acc, ce_ref, cr_ref, dtot_ref, f32_ref, bf16_ref, tri_ref, mask_ref):
        i = pl.program_id(1)

        @pl.when(i == 0)
        def _():
            dk_acc[...] = jnp.zeros_like(dk_acc)
            dv_acc[...] = jnp.zeros_like(dv_acc)
            _fill_attn_consts(tri_ref, mask_ref)

        q2s = [_stack_heads(q_ref[:, _lanes(g)], scale) for g in range(G)]
        do2s = [_stack_heads(do_ref[:, _lanes(g)].astype(BF16)) for g in range(G)]
        ones = jnp.ones((LANES, LANES), BF16)
        for g in range(G):
            ov = o_ref[:, _lanes(g)]
            dtot_ref[g] = _dot2(do2s[g].astype(F32) * jnp.concatenate([ov, ov], axis=0), ones)
        dq_acc[...] = jnp.zeros_like(dq_acc)
        ce_ref[...] = jnp.zeros_like(ce_ref)
        cr_ref[...] = jnp.zeros_like(cr_ref)

        def step(g, j, older, masked):
            n = older + QB
            wide = n // LANES
            q2, do2 = q2s[g], do2s[g]
            start = pl.multiple_of(j * QB, QB)
            kb = k_ref[pl.ds(start, n), _lanes(g)]
            vb = v_ref[pl.ds(start, n), _lanes(g)]
            zs, as_, bs, betas, gs = (f32_ref.at[g, t, :, :n] for t in range(5))
            hi, lo, wb = (bf16_ref.at[g, t, :, :n] for t in range(3))
            nt = (((1,), (1,)), ((), ()))
            zs[...] = lax.dot_general(q2, kb, nt, preferred_element_type=F32)
            gs[...] = lax.dot_general(do2, vb, nt, preferred_element_type=F32)
            yield
            for rows in _row_chunks():
                z = zs[rows, :]
                e = jnp.exp(-jnp.abs(z))
                sp = jnp.log(1.0 + e)
                b = jnp.minimum(-z, 0.0) - sp
                if masked:
                    b = jnp.where(_valid(mask_ref, rows, older),b, 0.0)
                rinv = 1.0 / (1.0 + e)
                as_[rows, :] = jnp.minimum(z, 0.0) - sp
                bs[rows, :] = b
                betas[rows, :] = jnp.where(z >= 0.0, rinv, e * rinv)
                _split_to(b, hi, lo, rows)
            yield
            tri = tri_ref[0, :n, :n]
            zs[...] = (jnp.dot(hi[...], tri, preferred_element_type=F32)
                       + jnp.dot(lo[...], tri, preferred_element_type=F32))
            yield
            for rows in _row_chunks():
                excl = zs[rows, :]
                total = _row_total(excl[:, :LANES] + bs[rows, :LANES])
                if not masked:
                    excl = excl + jnp.tile(ce_ref[g, rows, :], (1, wide))
                ce_ref[g, rows, :] += total
                w = jnp.exp(as_[rows, :] + excl)
                if masked:
                    w = jnp.where(_valid(mask_ref, rows, older),w, 0.0)
                wb[rows, :] = w.astype(BF16)
                gw = w * gs[rows, :]
                gs[rows, :] = gw
                _split_to(gw, hi, lo, rows)
            yield
            tri = tri_ref[1, :n, :n]
            zs[...] = (jnp.dot(hi[...], tri, preferred_element_type=F32)
                       + jnp.dot(lo[...], tri, preferred_element_type=F32))
            yield
            for rows in _row_chunks():
                rinc = zs[rows, :]
                total = _row_total(rinc[:, :LANES])
                if not masked:
                    rinc = rinc + jnp.tile(cr_ref[g, rows, :], (1, wide))
                cr_ref[g, rows, :] += total
                beta = betas[rows, :]
                dz = gs[rows, :] * (1.0 - beta) - beta * (jnp.tile(dtot_ref[g, rows, :], (1, wide)) - rinc)
                if masked:
                    dz = jnp.where(_valid(mask_ref, rows, older),dz, 0.0)
                hi[rows, :] = dz.astype(BF16)
            yield
            dzb = hi[...]
            dq_acc[g] += jnp.dot(dzb, kb, preferred_element_type=F32)
            dk_acc[pl.ds(start, n), _lanes(g)] += lax.dot_general(
                dzb, q2, (((0,), (0,)), ((), ())), preferred_element_type=F32)
            dv_acc[pl.ds(start, n), _lanes(g)] += lax.dot_general(
                wb[...], do2, (((0,), (0,)), ((), ())), preferred_element_type=F32)

        @pl.when(i == 0)
        def _():
            _in_lockstep([step(g, 0, 0, True) for g in range(G)])

        @pl.when(i > 0)
        def _():
            _in_lockstep([step(g, i - 1, QB, True) for g in range(G)])
            for g in range(G):
                _sweep_older(i, lambda j, _, g=g: _in_lockstep([step(g, j, 0, False)]), ce_ref.at[g], 1)

        for g in range(G):
            dq_ref[:, _lanes(g)] = (_unstack_heads(dq_acc[g]) * scale).astype(dq_ref.dtype)

        @pl.when(i == nb - 1)
        def _():
            dk_ref[...] = dk_acc[...].astype(dk_ref.dtype)
            dv_ref[...] = dv_acc[...].astype(dv_ref.dtype)

    n_steps = n_pairs // G
    blk = pl.BlockSpec((QB, W), lambda p, i: (i, p))
    col = pl.BlockSpec((L, W), lambda p, i: (0, p))
    return _call_with_job(
        body, job, name=name, grid=(n_steps, nb),
        in_specs=[blk,
                  pl.BlockSpec((L, W), lambda p, i: (0, n_steps + p)),
                  pl.BlockSpec((L, W), lambda p, i: (0, 2 * n_steps + p)),
                  blk, blk],
        out_specs=[blk, col, col],
        out_shape=[jax.ShapeDtypeStruct((L, sb_width), BF16)] * 3,
        scratch_shapes=[pltpu.VMEM((G, 2 * QB, LANES), F32), pltpu.VMEM((L, W), F32),
                        pltpu.VMEM((L, W), F32), pltpu.VMEM((G, 2 * QB, LANES), F32),
                        pltpu.VMEM((G, 2 * QB, LANES), F32), pltpu.VMEM((G, 2 * QB, LANES), F32),
                        pltpu.VMEM((G, 5, 2 * QB, 2 * QB), F32), pltpu.VMEM((G, 3, 2 * QB, 2 * QB), BF16),
                        pltpu.VMEM((2, 2 * QB, 2 * QB), BF16), pltpu.VMEM((2 * QB, 2 * QB), F32)],
        args=(qkv, qkv, qkv, o, dmixed))


def _conv_tile(L):
    return _pick(L, (384, 128))


def _glu(x, C):
    return x[:, :C] * _sigmoid(x[:, C:])


CONV_CHUNK = 32
SHIFT_TAIL = 24


def _fill_shifted(src_ref, dst_ref):
    n = dst_ref.shape[1]
    for r in range(1, 8):
        dst_ref[r - 1] = src_ref[r:r + n, :]


def _rows_at(src_ref, shifted_ref, start, n):
    q, r = divmod(start, 8)
    if r == 0:
        return src_ref[start:start + n, :]
    return shifted_ref[r - 1, 8 * q:8 * q + n, :]


def _conv_fwd(cacg, dw_w, dw_b, ln_g, ln_b, name, job=None):
    L, C2 = cacg.shape
    C = C2 // 2
    T = _conv_tile(L)
    H = CONV_HALO
    K = dw_w.shape[0]
    CH = CONV_CHUNK

    def body(x_ref, prev_ref, w_ref, b_ref, g_ref, beta_ref, o_ref, y_ref, u_ref, us_ref):
        i = pl.program_id(0)
        u_ref[0:H, :] = jnp.where(i > 0, _glu(prev_ref[...], C), 0.0)
        u_ref[H:, :] = _glu(x_ref[...], C)
        _fill_shifted(u_ref, us_ref)
        for c0 in range(0, T, CH):
            y = jnp.broadcast_to(b_ref[...], (CH, C))
            for k in range(K):
                y = y + w_ref[k:k + 1, :] * _rows_at(u_ref, us_ref, c0 + H - (K - 1) + k, CH)
            y_ref[c0:c0 + CH, :] = y
            mu = jnp.mean(y, axis=-1, keepdims=True)
            yc = y - mu
            rstd = lax.rsqrt(jnp.mean(yc * yc, axis=-1, keepdims=True) + EPS)
            ln = yc * rstd * g_ref[...] + beta_ref[...]
            o_ref[c0:c0 + CH, :] = (ln * _sigmoid(ln)).astype(o_ref.dtype)

    vec = pl.BlockSpec((1, C), lambda i: (0, 0))
    tile = pl.BlockSpec((T, C), lambda i: (i, 0))
    (out, y), arrived = _call_with_job(
        body, job, name=name, grid=(L // T,),
        in_specs=[pl.BlockSpec((T, C2), lambda i: (i, 0)),
                  pl.BlockSpec((H, C2), lambda i: (jnp.maximum(i * (T // H) - 1, 0), 0)),
                  pl.BlockSpec((K, C), lambda i: (0, 0)), vec, vec, vec],
        out_specs=[tile, tile],
        out_shape=[jax.ShapeDtypeStruct((L, C), BF16), jax.ShapeDtypeStruct((L, C), F32)],
        scratch_shapes=[pltpu.VMEM((T + H, C), F32), pltpu.VMEM((7, T + SHIFT_TAIL, C), F32)],
        args=(cacg, cacg, dw_w, dw_b, ln_g, ln_b))
    return out, y, arrived


def _conv_bwd(cacg, y, dmixed, dw_w, ln_g, ln_b, name, job=None):
    L, C2 = cacg.shape
    C = C2 // 2
    T = _conv_tile(L)
    H = CONV_HALO
    K = dw_w.shape[0]
    nt = L // T
    TE = T + H

    CH = CONV_CHUNK

    def body(x_ref, prev_ref, y_ref, ynext_ref, d_ref, dnext_ref, w_ref, g_ref, beta_ref,
             dca_ref, dcg_ref, dwt_ref, db_ref, dg_ref, dbeta_ref, u_ref, us_ref, dy_ref, dys_ref):
        i = pl.program_id(0)
        last = i == nt - 1

        @pl.when(i == 0)
        def _():
            dwt_ref[...] = jnp.zeros_like(dwt_ref)
            db_ref[...] = jnp.zeros_like(db_ref)
            dg_ref[...] = jnp.zeros_like(dg_ref)
            dbeta_ref[...] = jnp.zeros_like(dbeta_ref)

        u_ref[0:H, :] = jnp.where(i > 0, _glu(prev_ref[...], C), 0.0)
        u_ref[H:, :] = _glu(x_ref[...], C)
        _fill_shifted(u_ref, us_ref)
        dg_acc = jnp.zeros((1, C), F32)
        dbeta_acc = jnp.zeros((1, C), F32)
        db_acc = jnp.zeros((1, C), F32)
        for c0 in range(0, TE, CH):
            y = y_ref[c0:c0 + CH, :] if c0 < T else ynext_ref[c0 - T:c0 - T + CH, :]
            mu = jnp.mean(y, axis=-1, keepdims=True)
            yc = y - mu
            rstd = lax.rsqrt(jnp.mean(yc * yc, axis=-1, keepdims=True) + EPS)
            yh = yc * rstd
            ln = yh * g_ref[...] + beta_ref[...]
            s = _sigmoid(ln)
            dout = d_ref[c0:c0 + CH, :] if c0 < T else jnp.where(last, 0.0, dnext_ref[c0 - T:c0 - T + CH, :])
            dln = dout * (s * (1.0 + ln * (1.0 - s)))
            dyh = dln * g_ref[...]
            dy = rstd * (dyh - jnp.mean(dyh, axis=-1, keepdims=True)
                         - yh * jnp.mean(dyh * yh, axis=-1, keepdims=True))
            dy_ref[c0:c0 + CH, :] = dy
            if c0 < T:
                dg_acc = dg_acc + jnp.sum(dln * yh, axis=0, keepdims=True)
                dbeta_acc = dbeta_acc + jnp.sum(dln, axis=0, keepdims=True)
                db_acc = db_acc + jnp.sum(dy, axis=0, keepdims=True)
        dg_ref[...] += dg_acc
        dbeta_ref[...] += dbeta_acc
        db_ref[...] += db_acc
        _fill_shifted(dy_ref, dys_ref)
        for k in range(K):
            dwt_ref[k:k + 1, :] += jnp.sum(
                dy_ref[0:T, :] * _rows_at(u_ref, us_ref, H - (K - 1) + k, T), axis=0, keepdims=True)
        for c0 in range(0, T, CH):
            du = jnp.zeros((CH, C), F32)
            for k in range(K):
                du = du + w_ref[k:k + 1, :] * _rows_at(dy_ref, dys_ref, c0 + (K - 1) - k, CH)
            x = x_ref[c0:c0 + CH, :]
            sg = _sigmoid(x[:, C:])
            dca_ref[c0:c0 + CH, :] = (du * sg).astype(dca_ref.dtype)
            dcg_ref[c0:c0 + CH, :] = (du * x[:, :C] * sg * (1.0 - sg)).astype(dcg_ref.dtype)

    nh = L // H
    vec = pl.BlockSpec((1, C), lambda i: (0, 0))
    row = pl.BlockSpec((T, C), lambda i: (i, 0))
    after = lambda i: jnp.minimum((i + 1) * (T // H), nh - 1)
    return _call_with_job(
        body, job, name=name, grid=(nt,),
        in_specs=[pl.BlockSpec((T, C2), lambda i: (i, 0)),
                  pl.BlockSpec((H, C2), lambda i: (jnp.maximum(i * (T // H) - 1, 0), 0)),
                  row, pl.BlockSpec((H, C), lambda i: (after(i), 0)),
                  pl.BlockSpec((T, C), lambda i: (i, 1)), pl.BlockSpec((H, C), lambda i: (after(i), 1)),
                  pl.BlockSpec((K, C), lambda i: (0, 0)), vec, vec],
        out_specs=[row, row, pl.BlockSpec((H, C), lambda i: (0, 0)), vec, vec, vec],
        out_shape=[jax.ShapeDtypeStruct((L, C), BF16), jax.ShapeDtypeStruct((L, C), BF16),
                   jax.ShapeDtypeStruct((H, C), F32), jax.ShapeDtypeStruct((1, C), F32),
                   jax.ShapeDtypeStruct((1, C), F32), jax.ShapeDtypeStruct((1, C), F32)],
        scratch_shapes=[pltpu.VMEM((T + H, C), F32), pltpu.VMEM((7, T + SHIFT_TAIL, C), F32),
                        pltpu.VMEM((TE, C), F32), pltpu.VMEM((7, T + SHIFT_TAIL, C), F32)],
        args=(cacg, cacg, y, y, dmixed, dmixed, dw_w, ln_g, ln_b))


def _local_step(h0, target, n_meta, seq, norms, conv_p, wts, final_g, gather_behind, reducer):
    mix_g, ffn_g = norms
    dw_w, dw_b, ln_g, ln_b = conv_p
    depth = mix_g.shape[0]
    C = dw_b.shape[-1]
    sbw = (wts["w_in"][0].shape[-1] - 2 * C) // 3
    assert sbw == C, "the mixer halves must have equal width"
    row = lambda a, i: a[i][None, :]

    h = h0
    saved = []
    for i in range(depth):
        hn, proj_qkv, cacg = _norm_in_proj(h, row(mix_g, i), wts["w_in"][i], 3 * sbw, f"in_proj_{i}")
        def hosting(kind):
            job, keys = gather_behind.get((kind, i), (None, ()))

            def sink(arrived):
                for (wname, wl), arr in zip(keys, arrived):
                    wts[wname][wl] = arr

            return job, sink

        job, sink = hosting("attn")
        attn, arrived = _attn_fwd(proj_qkv, sbw, f"attn_fwd_{i}", job)
        sink(arrived)
        job, sink = hosting("conv")
        conv, conv_y, arrived = _conv_fwd(cacg, dw_w[i], row(dw_b, i), row(ln_g, i), row(ln_b, i),
                                          f"conv_fwd_{i}", job)
        sink(arrived)
        h_mid = _mm_rows([(attn, wts["w_out"][i], 0), (conv, wts["w_out"][i], 1)], f"out_proj_{i}", residual=h)
        job, sink = hosting("ffn")
        (hn2, g, u, act), arrived = _ffn_up(h_mid, row(ffn_g, i), wts["w_gate_t"][i], wts["w_up_t"][i],
                                            f"ffn_up_{i}", job)
        sink(arrived)
        job, sink = hosting("down")
        h_out = _mm_rows([(act, wts["w_down"][i], 0)], f"down_{i}", residual=h_mid, job=job)
        if job is not None:
            h_out, arrived = h_out
            sink(arrived)
        saved.append((h, hn, proj_qkv, cacg, attn, conv, conv_y, h_mid, hn2, g, u, act))
        h = h_out

    loss, dh, d_final_g = _loss_head(h, final_g[None, :], target, n_meta, seq, "loss_head")

    grads = {k: [None] * depth for k in ("mix_g", "ffn_g", "dw_w", "dw_b", "ln_g", "ln_b")}
    for i in reversed(range(depth)):
        h_in, hn, proj_qkv, cacg, attn, conv, conv_y, h_mid, hn2, g, u, act = saved[i]
        big = {}
        dg, du = _ffn_down_bwd(dh, wts["w_down"][i], g, u, f"ffn_down_bwd_{i}")
        big["w_down"] = _mm_tn(act, dh, f"dw_down_{i}", col_sharded=False)
        big["w_gate_t"] = _mm_tn(dg, hn2, f"dw_gate_{i}", col_sharded=False)
        big["w_up_t"] = _mm_tn(du, hn2, f"dw_up_{i}", col_sharded=False)
        sib_job, sib_sink = reducer.to_sibling(i, big)
        dh, d_ffn, arrived = _mm_rows([(dg, wts["w_gate_t"][i], 0), (du, wts["w_up_t"][i], 0)], f"d_hn2_{i}",
                                      norm_bwd=(h_mid, row(ffn_g, i), dh), job=sib_job)
        sib_sink(arrived)
        dmixed = _mm_rows([(dh, wts["w_out"][i], 0)], f"d_mixed_{i}", transposed=True)
        dw_out = jnp.concatenate([_mm_tn(attn, dh, f"dw_out_attn_{i}", col_sharded=False, chips=2),
                                  _mm_tn(conv, dh, f"dw_out_conv_{i}", col_sharded=False, chips=2)], axis=0)
        sib_job, sib_sink = reducer.to_sibling(i, {"w_out": dw_out})
        x_job, x_sink = reducer.take()
        (dq, dk, dv), arrived = _attn_bwd(proj_qkv, attn, dmixed, sbw, f"attn_bwd_{i}",
                                          _merge_jobs([sib_job, x_job]))
        sib_sink(arrived[:len(sib_job.out_shapes)])
        x_sink(arrived[len(sib_job.out_shapes):])
        x_job, x_sink = reducer.take()
        (dca, dcg, d_dw, d_b, d_lg, d_lb), arrived = _conv_bwd(
            cacg, conv_y, dmixed, dw_w[i], row(ln_g, i), row(ln_b, i), f"conv_bwd_{i}", x_job)
        x_sink(arrived)
        grads["dw_w"][i] = d_dw
        grads["dw_b"][i], grads["ln_g"][i], grads["ln_b"][i] = d_b, d_lg, d_lb
        dproj = jnp.concatenate([dq, dk, dv, dca, dcg], axis=1)
        reducer.ready(i, {"w_in": _mm_tn(hn, dproj, f"dw_in_{i}", col_sharded=True)})
        x_job, x_sink = reducer.take() if i == 0 else (None, None)
        res = _mm_rows([(dproj, wts["w_in"][i], 0)], f"d_hn_{i}", transposed=True,
                       norm_bwd=(h_in, row(mix_g, i), dh), job=x_job)
        dh, d_mix = res[:2]
        if x_job is not None:
            x_sink(res[2])
        grads["mix_g"][i], grads["ffn_g"][i] = d_mix, d_ffn
    grads["final_g"] = d_final_g
    return loss, dh, grads


ANY = pl.BlockSpec(memory_space=pl.ANY)


def _position():
    return lax.axis_index("x"), lax.axis_index("y"), lax.axis_index("c")


def _chip_at(x, y, k):
    return (1 - x if k & 2 else x), (1 - y if k & 1 else y)


def _half_rows(ref, half, rows, base=0):
    start = pl.multiple_of(base + half * rows, 8)
    lead = (slice(None),) * (len(ref.shape) - 2)
    return ref.at[(*lead, pl.ds(start, rows), slice(None))]


def _gather_job(fulls, shard_shapes, col_sharded):
    n = len(fulls)

    def tools(f_refs, send_sems, recv_sems):
        def block(wi, chip, half):
            _, R, C = shard_shapes[wi]
            if col_sharded[wi]:
                cols = pl.ds(pl.multiple_of(chip * C, LANES), C)
                return f_refs[wi].at[:, pl.ds(pl.multiple_of(half * (R // 2), 8), R // 2), cols]
            return _half_rows(f_refs[wi], half, R // 2, base=chip * R)

        def copy(wi, slot, blk, to):
            return pltpu.make_async_remote_copy(
                src_ref=blk, dst_ref=blk, send_sem=send_sems.at[6 * wi + slot],
                recv_sem=recv_sems.at[6 * wi + slot], device_id=to, device_id_type=MESH)

        return block, copy

    def start(_, f_refs, send_sems, recv_sems):
        block, copy = tools(f_refs, send_sems, recv_sems)
        x, y, c = _position()
        me = 2 * x + y
        for wi in range(n):
            for k in (1, 2, 3):
                copy(wi, k - 1, block(wi, me, c), (*_chip_at(x, y, k), c)).start()

    def finish(_, f_refs, send_sems, recv_sems):
        block, copy = tools(f_refs, send_sems, recv_sems)
        x, y, c = _position()
        me = 2 * x + y
        for wi in range(n):
            for k in (1, 2, 3):
                landed = block(wi, me ^ k, c)
                copy(wi, k - 1, landed, (x, y, c)).wait_recv()
                copy(wi, 2 + k, landed, (x, y, 1 - c)).start()
        for wi in range(n):
            for k in (1, 2, 3):
                copy(wi, 2 + k, block(wi, me ^ k, 1 - c), (x, y, c)).wait_recv()
        for wi in range(n):
            for k in (1, 2, 3):
                copy(wi, k - 1, block(wi, me, c), (x, y, c)).wait_send()
                copy(wi, 2 + k, block(wi, me ^ k, c), (x, y, c)).wait_send()

    return _CommJob(fulls, [jax.ShapeDtypeStruct(f.shape, f.dtype) for f in fulls], {i: i for i in range(n)},
                    6 * n, start, finish)


def _place_shard(w, layer, chip, col_sharded, dtype, name):
    _, R, C = w.shape
    tr = _pick(R, (512, 704, 256, 128, 48))
    nr = R // tr

    def body(chip_ref, w_ref, o_ref):
        o_ref[...] = w_ref[...].astype(dtype)

    if col_sharded:
        shape = (1, R, N_CHIPS * C)
        out_spec = pl.BlockSpec((None, tr, C), lambda r, chip_ref: (0, r, chip_ref[0]))
    else:
        shape = (1, N_CHIPS * R, C)
        out_spec = pl.BlockSpec((None, tr, C), lambda r, chip_ref: (0, chip_ref[0] * nr + r, 0))
    grid_spec = pltpu.PrefetchScalarGridSpec(
        num_scalar_prefetch=1, grid=(nr,),
        in_specs=[pl.BlockSpec((None, tr, C), lambda r, chip_ref: (layer, r, 0))], out_specs=out_spec)
    return pl.pallas_call(
        body, name=name, grid_spec=grid_spec, out_shape=jax.ShapeDtypeStruct(shape, dtype),
        compiler_params=_params(("parallel",)),
    )(chip, w)


def _sibling_job(grads):
    n = len(grads)

    def copies(g_refs, l_refs, send_sems, recv_sems):
        x, y, c = _position()
        return [pltpu.make_async_remote_copy(
            src_ref=_half_rows(g_refs[wi], 1 - c, grads[wi].shape[1] // 2), dst_ref=l_refs[wi],
            send_sem=send_sems.at[wi], recv_sem=recv_sems.at[wi],
            device_id=(x, y, 1 - c), device_id_type=MESH) for wi in range(n)]

    def start(*refs):
        for cp in copies(*refs):
            cp.start()

    def finish(*refs):
        for cp in copies(*refs):
            cp.wait()

    outs = [jax.ShapeDtypeStruct((g.shape[0], g.shape[1] // 2, g.shape[2]), g.dtype) for g in grads]
    return _CommJob(grads, outs, {}, n, start, finish)


def _chip_sum(g, landed, core, name):
    _, R, C = g.shape
    hr = R // 2
    tr = _pick(hr, (512, 352, 256, 128))
    nr = hr // tr

    def body(c_ref, g_ref, l_ref, o_ref):
        o_ref[...] = (g_ref[...] + l_ref[...]).astype(BF16)

    grid_spec = pltpu.PrefetchScalarGridSpec(
        num_scalar_prefetch=1, grid=(N_CHIPS, nr),
        in_specs=[pl.BlockSpec((None, tr, C), lambda j, r, c_ref: (j, c_ref[0] * nr + r, 0)),
                  pl.BlockSpec((None, tr, C), lambda j, r, c_ref: (j, r, 0))],
        out_specs=pl.BlockSpec((None, tr, C), lambda j, r, c_ref: (j, r, 0)))
    return pl.pallas_call(
        body, name=name, grid_spec=grid_spec, out_shape=jax.ShapeDtypeStruct((N_CHIPS, hr, C), BF16),
        compiler_params=_params(("parallel", "parallel")),
    )(core, g, landed)


def _across_job(parts):
    n = len(parts)

    def copy(p_refs, l_refs, send_sems, recv_sems, wi, k, to):
        x, y, _ = _position()
        me = 2 * x + y
        return pltpu.make_async_remote_copy(
            src_ref=p_refs[wi].at[me ^ k], dst_ref=l_refs[wi].at[me],
            send_sem=send_sems.at[3 * wi + k - 1], recv_sem=recv_sems.at[3 * wi + k - 1],
            device_id=to, device_id_type=MESH)

    def start(p_refs, l_refs, send_sems, recv_sems):
        x, y, c = _position()
        for wi in range(n):
            for k in (1, 2, 3):
                copy(p_refs, l_refs, send_sems, recv_sems, wi, k, (*_chip_at(x, y, k), c)).start()

    def finish(p_refs, l_refs, send_sems, recv_sems):
        x, y, c = _position()
        me = 2 * x + y
        for wi in range(n):
            for k in (1, 2, 3):
                slot = l_refs[wi].at[me ^ k]
                pltpu.make_async_remote_copy(
                    src_ref=slot, dst_ref=slot, send_sem=send_sems.at[3 * wi + k - 1],
                    recv_sem=recv_sems.at[3 * wi + k - 1], device_id=(x, y, c), device_id_type=MESH).wait_recv()
        for wi in range(n):
            for k in (1, 2, 3):
                copy(p_refs, l_refs, send_sems, recv_sems, wi, k, (x, y, c)).wait_send()

    return _CommJob(parts, [jax.ShapeDtypeStruct(p.shape, p.dtype) for p in parts], {}, 3 * n, start, finish)


class _Reducer:
    def __init__(self, core):
        self.core, self.parts, self.across, self.pending = core, {}, {}, []

    def to_sibling(self, layer, big):
        names = list(big)
        flat = [big[k] for k in names]

        def sink(landed):
            for k, g, la in zip(names, flat, landed):
                self.parts[k, layer] = _chip_sum(g, la, self.core, f"chip_sum_{k}_{layer}")
                self.pending.append((k, layer))

        return _sibling_job(flat), sink

    def ready(self, layer, big):
        job, sink = self.to_sibling(layer, big)
        sink(_run_job(job, f"grads_to_sibling_{next(iter(big))}_{layer}"))

    def take(self):
        keys, self.pending = self.pending, []
        if not keys:
            return None, lambda results: None

        def sink(results):
            self.across.update(zip(keys, results))

        return _across_job([self.parts[key] for key in keys]), sink


def _sum_chips(parts, landed, where, layer, depth, prev, name):
    _, hr, C = landed.shape
    tr = _pick(hr, (512, 352, 256, 128))
    nr = hr // tr

    def body(*refs):
        own_ref, slots, o_ref = refs[1], refs[2:2 + N_CHIPS], refs[-1]
        chip = refs[0][0]
        total = None
        for q in range(N_CHIPS):
            term = jnp.where(chip == q, own_ref[...], slots[q][...]).astype(F32)
            total = term if total is None else total + term
        o_ref[...] = total

    def slot_spec(q):
        return pl.BlockSpec((None, tr, C), lambda r, w: (jnp.where(w[0] == q, (q + 1) % N_CHIPS, q), r, 0))

    in_specs = [pl.BlockSpec((None, tr, C), lambda r, w: (w[0], r, 0))] + [slot_spec(q) for q in range(N_CHIPS)]
    args = [where, parts] + [landed] * N_CHIPS
    aliases = {}
    if prev is not None:
        in_specs.append(ANY)
        args.append(prev)
        aliases = {len(args) - 1: 0}
    grid_spec = pltpu.PrefetchScalarGridSpec(
        num_scalar_prefetch=1, grid=(nr,), in_specs=in_specs,
        out_specs=pl.BlockSpec((None, tr, C), lambda r, w: (layer, w[1] * nr + r, 0)))
    return pl.pallas_call(
        body, name=name, grid_spec=grid_spec, out_shape=jax.ShapeDtypeStruct((depth, 2 * hr, C), F32),
        input_output_aliases=aliases, compiler_params=_params(("parallel",)),
    )(*args)


def _rs_join_halves(reduced):
    n = len(reduced)

    def body(*refs):
        o_refs = refs[n:2 * n]
        send_sems, recv_sems = refs[2 * n:]
        x, y, c = _position()
        sent = []
        for wi in range(n):
            hr = reduced[wi].shape[1] // 2
            mine = _half_rows(o_refs[wi], c, hr)
            cp = pltpu.make_async_remote_copy(
                src_ref=mine, dst_ref=mine, send_sem=send_sems.at[wi], recv_sem=recv_sems.at[wi],
                device_id=(x, y, 1 - c), device_id_type=MESH)
            cp.start()
            sent.append(cp)
        for wi in range(n):
            hr = reduced[wi].shape[1] // 2
            theirs = _half_rows(o_refs[wi], 1 - c, hr)
            pltpu.make_async_remote_copy(
                src_ref=theirs, dst_ref=theirs, send_sem=send_sems.at[wi], recv_sem=recv_sems.at[wi],
                device_id=(x, y, c), device_id_type=MESH).wait_recv()
        for cp in sent:
            cp.wait_send()

    return pl.pallas_call(
        body, name="grads_join_halves", out_shape=[jax.ShapeDtypeStruct(r.shape, r.dtype) for r in reduced],
        in_specs=[ANY] * n, out_specs=[ANY] * n, input_output_aliases={i: i for i in range(n)},
        scratch_shapes=[pltpu.SemaphoreType.DMA((n,)), pltpu.SemaphoreType.DMA((n,))],
    )(*reduced)


def _adam_math(w, g, m, v):
    m = ADAM_B1 * m + (1.0 - ADAM_B1) * g
    v = ADAM_B2 * v + (1.0 - ADAM_B2) * jnp.square(g)
    m_hat = m / (1.0 - ADAM_B1 ** ADAM_STEP)
    v_hat = v / (1.0 - ADAM_B2 ** ADAM_STEP)
    delta = -ADAM_LR * (m_hat / (jnp.sqrt(v_hat) + ADAM_EPS) + ADAM_WD * w)
    return delta, m, v


def _small_reduce_adam(vec_grads, dww_grads, final_grad, meta_grad, state):
    vec_names = list(vec_grads)
    depth = len(dww_grads)
    D = final_grad.shape[1]
    n_meta = meta_grad.shape[0]
    taps_pad, C = dww_grads[0].shape
    names = vec_names + ["final", "meta", "dww"]
    at, row0 = 0, {}
    for k in vec_names:
        row0[k] = at
        at += depth
    row0["final"] = at
    at = -(-(at + 1) // 8) * 8
    row0["meta"] = at
    at += -(-n_meta // 8) * 8
    row0["dww"] = at
    rows = at + depth * taps_pad
    lanes = max(D, C)
    n_g = len(vec_names) * depth + depth + 2

    def body(*refs):
        g_refs = refs[:n_g]
        st = refs[n_g:n_g + 3 * len(names)]
        outs = refs[n_g + 3 * len(names):n_g + 7 * len(names)]
        slab, land, send_sems, recv_sems = refs[n_g + 7 * len(names):]
        x, y, c = _position()
        me = 4 * x + 2 * y + c
        chip = 2 * x + y
        slab[...] = jnp.zeros_like(slab)
        it = iter(g_refs)
        for k in vec_names:
            for l in range(depth):
                g_ref = next(it)
                slab[row0[k] + l:row0[k] + l + 1, 0:g_ref.shape[1]] = g_ref[...]
        for l in range(depth):
            slab[row0["dww"] + l * taps_pad:row0["dww"] + (l + 1) * taps_pad, 0:C] = next(it)[...]
        slab[row0["final"]:row0["final"] + 1, 0:D] = next(it)[...]
        slab[row0["meta"]:row0["meta"] + n_meta, 0:D] = next(it)[...]
        sent = []
        for k in range(1, N_DEV):
            to = (1 - x if k & 4 else x, 1 - y if k & 2 else y, 1 - c if k & 1 else c)
            cp = pltpu.make_async_remote_copy(
                src_ref=slab, dst_ref=land.at[k], send_sem=send_sems.at[k - 1], recv_sem=recv_sems.at[k - 1],
                device_id=to, device_id_type=MESH)
            cp.start()
            sent.append(cp)
        land[0] = slab[...]
        for cp in sent:
            cp.wait_recv()
        for cp in sent:
            cp.wait_send()
        total = land[me]
        for e in range(1, N_DEV):
            total = total + land[me ^ e]
        slab[...] = total

        def mine(r0, n_rows, width):
            got = slab[r0:r0 + n_rows, 0:width]
            for j in range(1, N_CHIPS):
                got = jnp.where(chip == j, slab[r0:r0 + n_rows, j * width:(j + 1) * width], got)
            return got

        def update(i, g, index=()):
            w_ref, m_ref, v_ref = st[3 * i:3 * i + 3]
            o = outs[4 * i:4 * i + 4]
            at = index if index else Ellipsis
            res = (g,) + _adam_math(w_ref[at], g, m_ref[at], v_ref[at])
            for o_ref, val in zip(o, res):
                o_ref[at] = val

        for i, k in enumerate(vec_names):
            width = st[3 * i].shape[1]
            update(i, slab[row0[k]:row0[k] + depth, 0:width])
        base = len(vec_names)
        update(base, slab[row0["final"]:row0["final"] + 1, 0:D])
        update(base + 1, mine(row0["meta"], n_meta, D // N_CHIPS))
        taps = st[3 * (base + 2)].shape[1]
        for l in range(depth):
            update(base + 2, mine(row0["dww"] + l * taps_pad, taps, C // N_CHIPS), (l,))

    flat_g = [g for k in vec_names for g in vec_grads[k]] + list(dww_grads) + [final_grad, meta_grad]
    flat_state = [a for k in names for a in state[k]]
    vmem = pl.BlockSpec(memory_space=pltpu.VMEM)
    out_shape = [jax.ShapeDtypeStruct(state[k][0].shape, F32) for k in names for _ in range(4)]
    res = pl.pallas_call(
        body, name="small_reduce_adam", out_shape=out_shape,
        in_specs=[vmem] * (len(flat_g) + len(flat_state)), out_specs=[vmem] * len(out_shape),
        scratch_shapes=[pltpu.VMEM((rows, lanes), F32), pltpu.VMEM((N_DEV, rows, lanes), F32),
                        pltpu.SemaphoreType.DMA((N_DEV - 1,)), pltpu.SemaphoreType.DMA((N_DEV - 1,))],
    )(*flat_g, *flat_state)
    return {k: tuple(res[4 * i:4 * i + 4]) for i, k in enumerate(names)}


def _adam(w, g, m, v, name):
    def body(w_ref, g_ref, m_ref, v_ref, d_ref, nm_ref, nv_ref):
        d_ref[...], nm_ref[...], nv_ref[...] = _adam_math(w_ref[...], g_ref[...], m_ref[...], v_ref[...])

    lyr, R, C = w.shape
    tr = _pick(R, (512, 704, 256, 128))
    blk = pl.BlockSpec((None, tr, C), lambda l, r: (l, r, 0))
    return pl.pallas_call(
        body, name=name, grid=(lyr, R // tr), in_specs=[blk] * 4, out_specs=[blk] * 3,
        out_shape=[jax.ShapeDtypeStruct(w.shape, F32)] * 3, compiler_params=_params(("parallel", "parallel")),
    )(w, g, m, v)


def _rows(a, pad_to=8):
    r = a.reshape(-1, LANES)
    extra = (-r.shape[0]) % pad_to
    return jnp.pad(r, ((0, extra), (0, 0))) if extra else r


def _pack(arrays):
    return jnp.concatenate([_rows(a) for a in arrays], axis=0)


def _unpack(slab, shapes):
    out, at = [], 0
    for shp in shapes:
        nrow = math.prod(shp) // LANES
        out.append(slab[at:at + nrow].reshape(shp))
        at += nrow + (-nrow) % 8
    return out


BIG = ("w_in", "w_out", "w_gate_t", "w_up_t", "w_down")
BIG_COL_SHARDED = (True, False, False, False, False)
TRANSPOSED = {"w_gate_t": "w_gate", "w_up_t": "w_up"}


def kernel(x, meta_tokens, mix_norm_g, w_in, conv_dw_w, conv_dw_b, conv_ln_g, conv_ln_b, w_out, ffn_norm_g, w_gate, w_up, w_down, final_norm_g, loss_target, m_meta_tokens, m_mix_norm_g, m_w_in, m_conv_dw_w, m_conv_dw_b, m_conv_ln_g, m_conv_ln_b, m_w_out, m_ffn_norm_g, m_w_gate, m_w_up, m_w_down, m_final_norm_g, v_meta_tokens, v_mix_norm_g, v_w_in, v_conv_dw_w, v_conv_dw_b, v_conv_ln_g, v_conv_ln_b, v_w_out, v_ffn_norm_g, v_w_gate, v_w_up, v_w_down, v_final_norm_g):
    n_meta, seq = meta_tokens.shape[0], x.shape[1]
    D = x.shape[2]
    depth, taps, c_shard = conv_dw_w.shape
    C = conv_dw_b.shape[-1]
    chip = (2 * lax.axis_index("x") + lax.axis_index("y")).astype(jnp.int32)
    core = lax.axis_index("c").astype(jnp.int32).reshape(1)
    chip1 = chip.reshape(1)
    where = jnp.concatenate([chip1, core])
    big_w = dict(w_in=w_in, w_out=w_out, w_gate=w_gate, w_up=w_up, w_down=w_down)
    big_m = dict(w_in=m_w_in, w_out=m_w_out, w_gate=m_w_gate, w_up=m_w_up, w_down=m_w_down)
    big_v = dict(w_in=v_w_in, w_out=v_w_out, w_gate=v_w_gate, w_up=v_w_up, w_down=v_w_down)

    small_shard = _pack([conv_dw_w, meta_tokens])[None]
    to_send = {k: jnp.swapaxes(big_w[TRANSPOSED[k]], 1, 2) if k in TRANSPOSED else big_w[k] for k in BIG}
    col = dict(zip(BIG, BIG_COL_SHARDED))
    wts = {k: [_place_shard(to_send[k], l, chip1, col[k], BF16, f"place_{k}_{l}") for l in range(depth)] for k in BIG}
    small_placed = _place_shard(small_shard, 0, chip1, False, F32, "place_small")

    def gather_job(keys, extra=()):
        arrays = [wts[k][l] for k, l in keys] + list(extra)
        shapes = [(1,) + to_send[k].shape[1:] for k, _ in keys] + [(1,) + small_shard.shape[1:]] * len(extra)
        return _gather_job(arrays, shapes, [col[k] for k, _ in keys] + [False] * len(extra))

    first_keys = [("w_in", 0)]
    *first, small_full = _run_job(gather_job(first_keys, [small_placed]), "gather_first")
    for (k, l), arr in zip(first_keys, first):
        wts[k][l] = arr
    behind_keys = {("attn", 0): [("w_out", 0), ("w_gate_t", 0), ("w_up_t", 0)], ("conv", 0): [("w_down", 0)]}
    for l in range(1, depth):
        behind_keys["ffn", l - 1] = [("w_in", l), ("w_gate_t", l)]
        behind_keys["down", l - 1] = [("w_up_t", l)]
        behind_keys["attn", l] = [("w_out", l), ("w_down", l)]
    gather_behind = {host: (gather_job(keys), keys) for host, keys in behind_keys.items()}

    rows_shard = small_shard.shape[1]
    dw_full, meta_full = [], []
    for j in range(N_CHIPS):
        dwj, mj = _unpack(small_full[0, j * rows_shard:(j + 1) * rows_shard],
                          [conv_dw_w.shape, meta_tokens.shape])
        dw_full.append(dwj)
        meta_full.append(mj)
    dw_w_full = jnp.concatenate(dw_full, axis=2)
    meta = jnp.concatenate(meta_full, axis=1)

    L = n_meta + seq
    Lp = -(-L // QUERY_BLOCK) * QUERY_BLOCK
    h0 = jnp.concatenate([meta, x[0], jnp.zeros((Lp - L, D), F32)], axis=0)
    target = jnp.pad(loss_target[0], ((n_meta, Lp - L), (0, 0)))
    reducer = _Reducer(core)
    loss, dh0, grads = _local_step(h0, target, n_meta, seq, (mix_norm_g, ffn_norm_g),
                                   (dw_w_full, conv_dw_b, conv_ln_g, conv_ln_b), wts, final_norm_g,
                                   gather_behind, reducer)
    loss = lax.psum(loss[0, 0], ("x", "y", "c"))
    grad_x = dh0[n_meta:L][None]

    reduced = []
    for k in BIG:
        arr = None
        for l in range(depth):
            arr = _sum_chips(reducer.parts[k, l], reducer.across[k, l], where, l, depth, arr, f"sum_chips_{k}_{l}")
        reduced.append(arr)
    big_g = dict(zip(BIG, _rs_join_halves(reduced)))

    out_g, out_d, out_m, out_v = {}, {}, {}, {}
    for kk in BIG:
        k = TRANSPOSED.get(kk, kk)
        view = (lambda a: jnp.swapaxes(a, 1, 2)) if kk in TRANSPOSED else (lambda a: a)
        res = _adam(view(big_w[k]), big_g[kk], view(big_m[k]), view(big_v[k]), f"adam_{k}")
        out_g[k] = view(big_g[kk])
        out_d[k], out_m[k], out_v[k] = (view(a) for a in res)

    as_row = lambda a: a.reshape(1, -1)
    state = dict(mix_g=(mix_norm_g, m_mix_norm_g, v_mix_norm_g), ffn_g=(ffn_norm_g, m_ffn_norm_g, v_ffn_norm_g),
                 dw_b=(conv_dw_b, m_conv_dw_b, v_conv_dw_b), ln_g=(conv_ln_g, m_conv_ln_g, v_conv_ln_g),
                 ln_b=(conv_ln_b, m_conv_ln_b, v_conv_ln_b),
                 final=(as_row(final_norm_g), as_row(m_final_norm_g), as_row(v_final_norm_g)),
                 meta=(meta_tokens, m_meta_tokens, v_meta_tokens), dww=(conv_dw_w, m_conv_dw_w, v_conv_dw_w))
    vec_names = ("mix_g", "ffn_g", "dw_b", "ln_g", "ln_b")
    small = _small_reduce_adam({k: grads[k] for k in vec_names}, grads["dw_w"], grads["final_g"], dh0[:n_meta], state)
    out_name = dict(mix_g="mix_norm_g", ffn_g="ffn_norm_g", dw_b="conv_dw_b", ln_g="conv_ln_g", ln_b="conv_ln_b",
                    final="final_norm_g", meta="meta_tokens", dww="conv_dw_w")
    for k, res in small.items():
        if k == "final":
            res = tuple(a.reshape(-1) for a in res)
        out_g[out_name[k]], out_d[out_name[k]], out_m[out_name[k]], out_v[out_name[k]] = res

    order = ("meta_tokens", "mix_norm_g", "w_in", "conv_dw_w", "conv_dw_b", "conv_ln_g", "conv_ln_b", "w_out",
             "ffn_norm_g", "w_gate", "w_up", "w_down", "final_norm_g")
    return (loss, grad_x, *[out_g[k] for k in order], *[out_d[k] for k in order],
            *[out_m[k] for k in order], *[out_v[k] for k in order])
```

```python
import functools
import math

import jax
import jax.numpy as jnp
from jax import lax
from jax.experimental import pallas as pl
from jax.experimental.pallas import tpu as pltpu

F32 = jnp.float32
BF16 = jnp.bfloat16
MESH = pl.DeviceIdType.MESH

EPS = 1e-6
QUERY_BLOCK = 128
LANES = 128
HEAD_DIM = 64
LOG_STICK_FLOOR = -40.0
CONV_HALO = 32
N_CHIPS = 4
N_DEV = 8
VMEM_LIMIT = 56 * 1024 * 1024

ADAM_LR = 0.001
ADAM_B1 = 0.9
ADAM_B2 = 0.999
ADAM_EPS = 1e-08
ADAM_WD = 0.01
ADAM_STEP = 10


def _pick(n, prefs):
    for p in prefs:
        if n % p == 0:
            return p
    return n


def _params(sem=None):
    return pltpu.CompilerParams(dimension_semantics=sem, vmem_limit_bytes=VMEM_LIMIT)


def _sigmoid(x):
    return 1.0 / (1.0 + jnp.exp(-x))


def _loss_head(h, g, target, n_meta, seq, name):
    L, D = h.shape
    T = _pick(L, (384, 128))

    def body(h_ref, g_ref, t_ref, loss_ref, dh_ref, dg_ref):
        i = pl.program_id(0)
        x = h_ref[...]
        gv = g_ref[...]
        r = lax.rsqrt(jnp.mean(x * x, axis=-1, keepdims=True) + EPS)
        xh = x * r
        y = xh * gv
        rows = i * T + lax.broadcasted_iota(jnp.int32, (T, 1), 0)
        live = (rows >= n_meta) & (rows < n_meta + seq)
        diff = jnp.where(live, y - t_ref[...], 0.0)
        dyv = diff / D
        dxh = dyv * gv
        dh_ref[...] = r * (dxh - xh * jnp.mean(dxh * xh, axis=-1, keepdims=True))

        @pl.when(i == 0)
        def _():
            dg_ref[...] = jnp.zeros_like(dg_ref)
            loss_ref[...] = jnp.zeros_like(loss_ref)

        dg_ref[...] += jnp.sum(dyv * xh, axis=0, keepdims=True)
        per_row = jnp.mean(diff * diff, axis=-1, keepdims=True)
        loss_ref[...] += 0.5 * jnp.sum(per_row, axis=0, keepdims=True)

    row = pl.BlockSpec((T, D), lambda i: (i, 0))
    vec = pl.BlockSpec((1, D), lambda i: (0, 0))
    one = pl.BlockSpec((1, 1), lambda i: (0, 0))
    return pl.pallas_call(
        body, name=name, grid=(L // T,),
        in_specs=[row, vec, row], out_specs=[one, row, vec],
        out_shape=[jax.ShapeDtypeStruct((1, 1), F32), jax.ShapeDtypeStruct((L, D), F32),
                   jax.ShapeDtypeStruct((1, D), F32)],
        compiler_params=_params(("arbitrary",)),
    )(h, g, target)


def _ffn_tiles(M, F):
    return _pick(M, (352, 384, 128)), _pick(F, (1408, 512, 256, 128))


def _resident(shape):
    return pl.BlockSpec((None,) + tuple(shape[1:]), lambda *_: (0,) * len(shape), pipeline_mode=pl.Buffered(1))


def _norm_in_proj(h, g, w_in, n_qkv, name):
    M, D = h.shape
    N = w_in.shape[2]
    tm = _pick(M, (352, 384, 128))
    tn = _pick(math.gcd(n_qkv, N - n_qkv), (512, 256, 128))

    def body(h_ref, g_ref, w_ref, hn_ref, qkv_ref, rest_ref):
        x = h_ref[...]
        r = lax.rsqrt(jnp.mean(x * x, axis=-1, keepdims=True) + EPS)
        hv = (x * r * g_ref[...]).astype(BF16)
        hn_ref[...] = hv
        for c in range(0, N, tn):
            out = jnp.dot(hv, w_ref[:, c:c + tn], preferred_element_type=F32)
            if c < n_qkv:
                qkv_ref[:, c:c + tn] = out.astype(qkv_ref.dtype)
            else:
                rest_ref[:, c - n_qkv:c - n_qkv + tn] = out

    rows = pl.BlockSpec((tm, D), lambda i: (i, 0))
    return pl.pallas_call(
        body, name=name, grid=(M // tm,),
        in_specs=[rows, pl.BlockSpec((1, D), lambda i: (0, 0)), _resident(w_in.shape)],
        out_specs=[rows, pl.BlockSpec((tm, n_qkv), lambda i: (i, 0)), pl.BlockSpec((tm, N - n_qkv), lambda i: (i, 0))],
        out_shape=[jax.ShapeDtypeStruct((M, D), BF16), jax.ShapeDtypeStruct((M, n_qkv), BF16),
                   jax.ShapeDtypeStruct((M, N - n_qkv), F32)],
        compiler_params=_params(("parallel",)),
    )(h, g, w_in)


def _ffn_up(h, norm_g, w_gate_t, w_up_t, name, job=None):
    M, D = h.shape
    F = w_gate_t.shape[1]
    tm, tf = _ffn_tiles(M, F)
    nt = (((1,), (1,)), ((), ()))

    def body(h_ref, ng_ref, wg_ref, wu_ref, hn_ref, g_ref, u_ref, a_ref):
        x = h_ref[...]
        r = lax.rsqrt(jnp.mean(x * x, axis=-1, keepdims=True) + EPS)
        hv = (x * r * ng_ref[...]).astype(BF16)
        hn_ref[...] = hv
        for c in range(0, F, tf):
            gv = lax.dot_general(hv, wg_ref[c:c + tf, :], nt, preferred_element_type=F32)
            uv = lax.dot_general(hv, wu_ref[c:c + tf, :], nt, preferred_element_type=F32)
            g_ref[:, c:c + tf] = gv.astype(g_ref.dtype)
            u_ref[:, c:c + tf] = uv.astype(u_ref.dtype)
            a_ref[:, c:c + tf] = (gv * _sigmoid(gv) * uv).astype(a_ref.dtype)

    rows = pl.BlockSpec((tm, D), lambda i: (i, 0))
    wide = pl.BlockSpec((tm, F), lambda i: (i, 0))
    return _call_with_job(
        body, job, name=name, grid=(M // tm,),
        in_specs=[rows, pl.BlockSpec((1, D), lambda i: (0, 0)), _resident(w_gate_t.shape), _resident(w_up_t.shape)],
        out_specs=[rows, wide, wide, wide],
        out_shape=[jax.ShapeDtypeStruct((M, D), BF16), jax.ShapeDtypeStruct((M, F), BF16),
                   jax.ShapeDtypeStruct((M, F), BF16), jax.ShapeDtypeStruct((M, F), BF16)],
        scratch_shapes=[], args=(h, norm_g, w_gate_t, w_up_t))


def _ffn_down_bwd(dh, w_down, g, u, name):
    M, D = dh.shape
    F = g.shape[1]
    tm, tf = _ffn_tiles(M, F)

    def body(d_ref, w_ref, g_ref, u_ref, dg_ref, du_ref):
        dhv = d_ref[...].astype(BF16)
        for c in range(0, F, tf):
            dv = lax.dot_general(dhv, w_ref[c:c + tf, :], (((1,), (1,)), ((), ())), preferred_element_type=F32)
            gv = g_ref[:, c:c + tf].astype(F32)
            s = _sigmoid(gv)
            du_ref[:, c:c + tf] = (dv * (gv * s)).astype(du_ref.dtype)
            dg_ref[:, c:c + tf] = (dv * u_ref[:, c:c + tf].astype(F32)
                                   * (s * (1.0 + gv * (1.0 - s)))).astype(dg_ref.dtype)

    wide = pl.BlockSpec((tm, F), lambda i: (i, 0))
    return pl.pallas_call(
        body, name=name, grid=(M // tm,),
        in_specs=[pl.BlockSpec((tm, D), lambda i: (i, 0)), _resident(w_down.shape), wide, wide],
        out_specs=[wide, wide],
        out_shape=[jax.ShapeDtypeStruct((M, F), BF16), jax.ShapeDtypeStruct((M, F), BF16)],
        compiler_params=_params(("parallel",)),
    )(dh, w_down, g, u)


def _mm_rows(pairs, name, transposed=False, residual=None, norm_bwd=None, job=None):
    M = pairs[0][0].shape[0]
    N = pairs[0][1].shape[1 if transposed else 2]
    shallow = sum(p[0].shape[1] for p in pairs) <= 1024
    tm = _pick(M, ((1056,) if shallow else ()) + (352, 384, 128))
    tn = _pick(N, (512, 256, 128))
    n = len(pairs)

    def body(*refs):
        a_refs, w_refs, extra = refs[:n], refs[n:2 * n], refs[2 * n:]
        p_ref = extra[-1] if norm_bwd is not None else extra[-1 if residual is None else 1]
        lhs = [a_ref[...].astype(BF16) for a_ref in a_refs]
        for c in range(0, N, tn):
            acc = None
            for a, w_ref, (_, _, blk) in zip(lhs, w_refs, pairs):
                k = a.shape[1]
                if transposed:
                    d = lax.dot_general(a, w_ref[c:c + tn, blk * k:(blk + 1) * k], (((1,), (1,)), ((), ())),
                                        preferred_element_type=F32)
                else:
                    d = jnp.dot(a, w_ref[blk * k:(blk + 1) * k, c:c + tn], preferred_element_type=F32)
                acc = d if acc is None else acc + d
            if residual is not None:
                acc = acc + extra[0][:, c:c + tn]
            p_ref[:, c:c + tn] = acc
        if norm_bwd is not None:
            h_ref, g_ref, dhin_ref, dh_ref, dg_ref, _ = extra
            x = h_ref[...]
            dyv = p_ref[...]
            r = lax.rsqrt(jnp.mean(x * x, axis=-1, keepdims=True) + EPS)
            xh = x * r
            dxh = dyv * g_ref[...]
            dh_ref[...] = dhin_ref[...] + r * (dxh - xh * jnp.mean(dxh * xh, axis=-1, keepdims=True))

            @pl.when(pl.program_id(0) == 0)
            def _():
                dg_ref[...] = jnp.zeros_like(dg_ref)

            dg_ref[...] += jnp.sum(dyv * xh, axis=0, keepdims=True)

    in_specs = [pl.BlockSpec((tm, a.shape[1]), lambda i: (i, 0)) for a, _, _ in pairs]
    in_specs += [_resident(w.shape) for _, w, _ in pairs]
    args = [p[0] for p in pairs] + [p[1] for p in pairs]
    wide = pl.BlockSpec((tm, N), lambda i: (i, 0))
    vec = pl.BlockSpec((1, N), lambda i: (0, 0))
    out_specs, out_shape, scratch = [wide], [jax.ShapeDtypeStruct((M, N), F32)], []
    if residual is not None:
        in_specs.append(wide)
        args.append(residual)
    if norm_bwd is not None:
        in_specs += [wide, vec, wide]
        args += list(norm_bwd)
        out_specs.append(vec)
        out_shape.append(jax.ShapeDtypeStruct((1, N), F32))
        scratch = [pltpu.VMEM((tm, N), F32)]
    outs, arrived = _call_with_job(body, job, name=name, grid=(M // tm,), in_specs=in_specs, out_specs=out_specs,
                                   out_shape=out_shape, scratch_shapes=scratch, args=args)
    res = tuple(outs) if norm_bwd is not None else (outs[0],)
    res = res if job is None else res + (arrived,)
    return res[0] if len(res) == 1 else res


def _mm_tn(a, b, name, col_sharded, chips=N_CHIPS):
    M, K = a.shape
    N = b.shape[1]
    tm = _pick(M, (1056, 384, 128))
    tk = _pick(K, (1024, 1408, 512, 256, 128))
    tn = N // N_CHIPS if col_sharded else _pick(N, (1024, 512, 128))

    def body(a_ref, b_ref, o_ref):
        @pl.when(pl.program_id(2) == 0)
        def _():
            o_ref[...] = jnp.zeros_like(o_ref)

        o_ref[...] += lax.dot_general(a_ref[...].astype(BF16), b_ref[...].astype(BF16),
                                      (((0,), (0,)), ((), ())), preferred_element_type=F32)

    if col_sharded:
        out_shape = jax.ShapeDtypeStruct((N_CHIPS, K, tn), F32)
        out_spec = pl.BlockSpec((None, tk, tn), lambda k, j, m: (j, k, 0))
    else:
        out_shape = jax.ShapeDtypeStruct((K, N), F32)
        out_spec = pl.BlockSpec((tk, tn), lambda k, j, m: (k, j))
    out = pl.pallas_call(
        body, name=name, grid=(K // tk, N // tn, M // tm),
        in_specs=[pl.BlockSpec((tm, tk), lambda k, j, m: (m, k)), pl.BlockSpec((tm, tn), lambda k, j, m: (m, j))],
        out_specs=out_spec, out_shape=out_shape,
        compiler_params=_params(("parallel", "parallel", "arbitrary")),
    )(a, b)
    return out if col_sharded else out.reshape(chips, K // chips, N)


def _stack_heads(x, scale=None):
    lane = lax.broadcasted_iota(jnp.int32, x.shape, 1)
    zero = jnp.zeros_like(x)
    lo = jnp.where(lane < HEAD_DIM, x, zero)
    hi = jnp.where(lane < HEAD_DIM, zero, x)
    out = jnp.concatenate([lo, hi], axis=0)
    return out if scale is None else out * scale


def _unstack_heads(x2):
    qb = x2.shape[0] // 2
    lane = lax.broadcasted_iota(jnp.int32, (qb, LANES), 1)
    return jnp.where(lane < HEAD_DIM, x2[:qb], x2[qb:])


ATTN_CHUNK = 32


def _in_lockstep(staged):
    waiting, live = list(staged), []
    while waiting or live:
        if waiting:
            live.append(waiting.pop(0))
        for gen in list(live):
            if next(gen, StopIteration) is StopIteration:
                live.remove(gen)


def _row_chunks():
    return [slice(r, r + ATTN_CHUNK) for r in range(0, 2 * QUERY_BLOCK, ATTN_CHUNK)]


def _chunk_valid(rows, older):
    shape = (rows.stop - rows.start, older + QUERY_BLOCK)
    r = (rows.start + lax.broadcasted_iota(jnp.int32, shape, 0)) & (QUERY_BLOCK - 1)
    return lax.broadcasted_iota(jnp.int32, shape, 1) < r + older


def _split_to(x, hi_ref, lo_ref, rows):
    xh = x.astype(BF16)
    hi_ref[rows, :] = xh
    lo_ref[rows, :] = (x - xh.astype(F32)).astype(BF16)


def _triangle(keys, strict):
    r = lax.broadcasted_iota(jnp.int32, (keys, keys), 0)
    c = lax.broadcasted_iota(jnp.int32, (keys, keys), 1)
    return jnp.where((r > c) if strict else (r >= c), 1.0, 0.0).astype(BF16)


def _fill_attn_consts(tri_ref, mask_ref):
    n = 2 * QUERY_BLOCK
    tri_ref[0] = _triangle(n, True)
    tri_ref[1] = _triangle(n, False)
    mask_ref[...] = jnp.where(_chunk_valid(slice(0, n), QUERY_BLOCK), 1.0, 0.0)


def _valid(mask_ref, rows, older):
    cols = slice(0, 2 * QUERY_BLOCK) if older else slice(QUERY_BLOCK, 2 * QUERY_BLOCK)
    return mask_ref[rows, cols] > 0.5


def _row_total(first):
    lane = lax.broadcasted_iota(jnp.int32, first.shape, 1)
    total = jnp.sum(jnp.where(lane == 0, first, 0.0), axis=1, keepdims=True)
    return jnp.broadcast_to(total, first.shape)


def _pairs_per_step(n_pairs, most):
    return max(g for g in (1, 2, 4) if g <= most and n_pairs % g == 0)


def _lanes(g):
    return slice(g * LANES, (g + 1) * LANES)


def _sweep_older(i, step, carry_ref, first):
    def cond(state):
        n, live = state
        return jnp.logical_and(n < i, live)

    def older(state):
        n, _ = state
        step(i - 1 - n, False)
        return n + 1, jnp.max(carry_ref[...]) > LOG_STICK_FLOOR

    lax.while_loop(cond, older, (first, jnp.max(carry_ref[...]) > LOG_STICK_FLOOR))


class _CommJob:
    def __init__(self, inputs, out_shapes, aliases, n_sems, start, finish):
        self.inputs, self.out_shapes, self.aliases, self.n_sems = list(inputs), list(out_shapes), aliases, n_sems
        self.start, self.finish = start, finish


def _merge_jobs(jobs):
    jobs = [j for j in jobs if j is not None]
    if len(jobs) <= 1:
        return jobs[0] if jobs else None
    spans, aliases = [], {}
    i0 = o0 = s0 = 0
    for j in jobs:
        spans.append((i0, o0, s0))
        aliases.update({i0 + a: o0 + b for a, b in j.aliases.items()})
        i0, o0, s0 = i0 + len(j.inputs), o0 + len(j.out_shapes), s0 + j.n_sems

    def run(which):
        def go(ins, outs, send_sems, recv_sems):
            for j, (i, o, s) in zip(jobs, spans):
                getattr(j, which)(ins[i:i + len(j.inputs)], outs[o:o + len(j.out_shapes)],
                                  send_sems.at[pl.ds(s, j.n_sems)], recv_sems.at[pl.ds(s, j.n_sems)])
        return go

    return _CommJob([a for j in jobs for a in j.inputs], [s for j in jobs for s in j.out_shapes], aliases, s0,
                    run("start"), run("finish"))


def _call_with_job(core_body, job, *, name, grid, in_specs, out_specs, out_shape, scratch_shapes, args):
    sem = ("arbitrary",) * len(grid)
    if job is None:
        res = pl.pallas_call(core_body, name=name, grid=grid, in_specs=in_specs, out_specs=out_specs,
                             out_shape=out_shape, scratch_shapes=scratch_shapes, compiler_params=_params(sem))(*args)
        return list(res), []
    n_in, n_out, n_scr = len(in_specs), len(out_specs), len(scratch_shapes)
    m_in, m_out = len(job.inputs), len(job.out_shapes)

    def body(*refs):
        at = 0
        parts = []
        for count in (n_in, m_in, n_out, m_out, n_scr, 2):
            parts.append(refs[at:at + count])
            at += count
        ins, job_in, outs, job_outs, scratch, (send_sems, recv_sems) = parts
        first = functools.reduce(jnp.logical_and, [pl.program_id(a) == 0 for a in range(len(grid))])
        last = functools.reduce(jnp.logical_and, [pl.program_id(a) == grid[a] - 1 for a in range(len(grid))])

        @pl.when(first)
        def _():
            job.start(job_in, job_outs, send_sems, recv_sems)

        core_body(*ins, *outs, *scratch)

        @pl.when(last)
        def _():
            job.finish(job_in, job_outs, send_sems, recv_sems)

    res = pl.pallas_call(
        body, name=name, grid=grid, in_specs=list(in_specs) + [ANY] * m_in, out_specs=list(out_specs) + [ANY] * m_out,
        out_shape=list(out_shape) + job.out_shapes,
        input_output_aliases={n_in + a: n_out + b for a, b in job.aliases.items()},
        scratch_shapes=list(scratch_shapes) + [pltpu.SemaphoreType.DMA((job.n_sems,)), pltpu.SemaphoreType.DMA((job.n_sems,))],
        compiler_params=_params(sem),
    )(*args, *job.inputs)
    return list(res[:n_out]), list(res[n_out:])


def _run_job(job, name):
    m_in, m_out = len(job.inputs), len(job.out_shapes)

    def body(*refs):
        job_in, job_outs = refs[:m_in], refs[m_in:m_in + m_out]
        send_sems, recv_sems = refs[m_in + m_out:]
        job.start(job_in, job_outs, send_sems, recv_sems)
        job.finish(job_in, job_outs, send_sems, recv_sems)

    return list(pl.pallas_call(
        body, name=name, in_specs=[ANY] * m_in, out_specs=[ANY] * m_out, out_shape=job.out_shapes,
        input_output_aliases=dict(job.aliases),
        scratch_shapes=[pltpu.SemaphoreType.DMA((job.n_sems,)), pltpu.SemaphoreType.DMA((job.n_sems,))],
    )(*job.inputs))


def _attn_fwd(qkv, sb_width, name, job=None):
    L = qkv.shape[0]
    QB = QUERY_BLOCK
    nb = L // QB
    n_pairs = sb_width // LANES
    G = _pairs_per_step(n_pairs, 4)
    W = G * LANES
    scale = 1.0 / math.sqrt(HEAD_DIM)

    def body(q_ref, k_ref, v_ref, o_ref, acc_ref, carry_ref, f32_ref, bf16_ref, tri_ref, mask_ref):
        i = pl.program_id(1)

        @pl.when(i == 0)
        def _():
            _fill_attn_consts(tri_ref, mask_ref)

        q2 = [_stack_heads(q_ref[:, _lanes(g)], scale) for g in range(G)]
        acc_ref[...] = jnp.zeros_like(acc_ref)
        carry_ref[...] = jnp.zeros_like(carry_ref)

        def step(g, j, older, masked):
            n = older + QB
            start = pl.multiple_of(j * QB, QB)
            kb = k_ref[pl.ds(start, n), _lanes(g)]
            vb = v_ref[pl.ds(start, n), _lanes(g)]
            zs, as_, bs = (f32_ref.at[g, t, :, :n] for t in range(3))
            hi, lo = (bf16_ref.at[g, t, :, :n] for t in range(2))
            zs[...] = lax.dot_general(q2[g], kb, (((1,), (1,)), ((), ())), preferred_element_type=F32)
            yield
            for rows in _row_chunks():
                z = zs[rows, :]
                sp = jnp.log(1.0 + jnp.exp(-jnp.abs(z)))
                b = jnp.minimum(-z, 0.0) - sp
                if masked:
                    b = jnp.where(_valid(mask_ref, rows, older),b, 0.0)
                as_[rows, :] = jnp.minimum(z, 0.0) - sp
                bs[rows, :] = b
                _split_to(b, hi, lo, rows)
            yield
            tri = tri_ref[0, :n, :n]
            zs[...] = (jnp.dot(hi[...], tri, preferred_element_type=F32)
                       + jnp.dot(lo[...], tri, preferred_element_type=F32))
            yield
            for rows in _row_chunks():
                excl = zs[rows, :]
                total = _row_total(excl[:, :LANES] + bs[rows, :LANES])
                if not masked:
                    excl = excl + jnp.tile(carry_ref[g, rows, :], (1, n // LANES))
                carry_ref[g, rows, :] += total
                w = jnp.exp(as_[rows, :] + excl)
                if masked:
                    w = jnp.where(_valid(mask_ref, rows, older),w, 0.0)
                _split_to(w, hi, lo, rows)
            yield
            acc_ref[g] += (jnp.dot(hi[...], vb, preferred_element_type=F32)
                           + jnp.dot(lo[...], vb, preferred_element_type=F32))

        @pl.when(i == 0)
        def _():
            _in_lockstep([step(g, 0, 0, True) for g in range(G)])

        @pl.when(i > 0)
        def _():
            _in_lockstep([step(g, i - 1, QB, True) for g in range(G)])
            for g in range(G):
                _sweep_older(i, lambda j, _, g=g: _in_lockstep([step(g, j, 0, False)]), carry_ref.at[g], 1)

        for g in range(G):
            o_ref[:, _lanes(g)] = _unstack_heads(acc_ref[g])

    n_steps = n_pairs // G
    (out,), job_out = _call_with_job(
        body, job, name=name, grid=(n_steps, nb),
        in_specs=[pl.BlockSpec((QB, W), lambda p, i: (i, p)),
                  pl.BlockSpec((L, W), lambda p, i: (0, n_steps + p)),
                  pl.BlockSpec((L, W), lambda p, i: (0, 2 * n_steps + p))],
        out_specs=[pl.BlockSpec((QB, W), lambda p, i: (i, p))],
        out_shape=[jax.ShapeDtypeStruct((L, sb_width), F32)],
        scratch_shapes=[pltpu.VMEM((G, 2 * QB, LANES), F32), pltpu.VMEM((G, 2 * QB, LANES), F32),
                        pltpu.VMEM((G, 3, 2 * QB, 2 * QB), F32), pltpu.VMEM((G, 2, 2 * QB, 2 * QB), BF16),
                        pltpu.VMEM((2, 2 * QB, 2 * QB), BF16), pltpu.VMEM((2 * QB, 2 * QB), F32)],
        args=(qkv, qkv, qkv))
    return out, job_out


def _attn_bwd(qkv, o, dmixed, sb_width, name, job=None):
    L = qkv.shape[0]
    QB = QUERY_BLOCK
    nb = L // QB
    n_pairs = sb_width // LANES
    G = _pairs_per_step(n_pairs, 2)
    W = G * LANES
    scale = 1.0 / math.sqrt(HEAD_DIM)

    def body(q_ref, k_ref, v_ref, o_ref, do_ref, dq_ref, dk_ref, dv_ref,
             dq_acc, dk_acc, dv_acc, ce_ref, cr_ref, dtot_ref, f32_ref, bf16_ref, tri_ref, mask_ref):
        i = pl.program_id(1)

        @pl.when(i == 0)
        def _():
            dk_acc[...] = jnp.zeros_like(dk_acc)
            dv_acc[...] = jnp.zeros_like(dv_acc)
            _fill_attn_consts(tri_ref, mask_ref)

        q2s = [_stack_heads(q_ref[:, _lanes(g)], scale) for g in range(G)]
        do2s = [_stack_heads(do_ref[:, _lanes(g)].astype(BF16)) for g in range(G)]
        for g in range(G):
            ov = o_ref[:, _lanes(g)]
            per_row = jnp.sum(do2s[g].astype(F32) * jnp.concatenate([ov, ov], axis=0), axis=1, keepdims=True)
            dtot_ref[g] = jnp.broadcast_to(per_row, (2 * QB, LANES))
        dq_acc[...] = jnp.zeros_like(dq_acc)
        ce_ref[...] = jnp.zeros_like(ce_ref)
        cr_ref[...] = jnp.zeros_like(cr_ref)

        def step(g, j, older, masked):
            n = older + QB
            wide = n // LANES
            q2, do2 = q2s[g], do2s[g]
            start = pl.multiple_of(j * QB, QB)
            kb = k_ref[pl.ds(start, n), _lanes(g)]
            vb = v_ref[pl.ds(start, n), _lanes(g)]
            zs, as_, bs, betas, gs = (f32_ref.at[g, t, :, :n] for t in range(5))
            hi, lo, wb = (bf16_ref.at[g, t, :, :n] for t in range(3))
            nt = (((1,), (1,)), ((), ()))
            zs[...] = lax.dot_general(q2, kb, nt, preferred_element_type=F32)
            gs[...] = lax.dot_general(do2, vb, nt, preferred_element_type=F32)
            yield
            for rows in _row_chunks():
                z = zs[rows, :]
                e = jnp.exp(-jnp.abs(z))
                sp = jnp.log(1.0 + e)
                b = jnp.minimum(-z, 0.0) - sp
                if masked:
                    b = jnp.where(_valid(mask_ref, rows, older),b, 0.0)
                rinv = 1.0 / (1.0 + e)
                as_[rows, :] = jnp.minimum(z, 0.0) - sp
                bs[rows, :] = b
                betas[rows, :] = jnp.where(z >= 0.0, rinv, e * rinv)
                _split_to(b, hi, lo, rows)
            yield
            tri = tri_ref[0, :n, :n]
            zs[...] = (jnp.dot(hi[...], tri, preferred_element_type=F32)
                       + jnp.dot(lo[...], tri, preferred_element_type=F32))
            yield
            for rows in _row_chunks():
                excl = zs[rows, :]
                total = _row_total(excl[:, :LANES] + bs[rows, :LANES])
                if not masked:
                    excl = excl + jnp.tile(ce_ref[g, rows, :], (1, wide))
                ce_ref[g, rows, :] += total
                w = jnp.exp(as_[rows, :] + excl)
                if masked:
                    w = jnp.where(_valid(mask_ref, rows, older),w, 0.0)
                wb[rows, :] = w.astype(BF16)
                gw = w * gs[rows, :]
                gs[rows, :] = gw
                _split_to(gw, hi, lo, rows)
            yield
            tri = tri_ref[1, :n, :n]
            zs[...] = (jnp.dot(hi[...], tri, preferred_element_type=F32)
                       + jnp.dot(lo[...], tri, preferred_element_type=F32))
            yield
            for rows in _row_chunks():
                rinc = zs[rows, :]
                total = _row_total(rinc[:, :LANES])
                if not masked:
                    rinc = rinc + jnp.tile(cr_ref[g, rows, :], (1, wide))
                cr_ref[g, rows, :] += total
                beta = betas[rows, :]
                dz = gs[rows, :] * (1.0 - beta) - beta * (jnp.tile(dtot_ref[g, rows, :], (1, wide)) - rinc)
                if masked:
                    dz = jnp.where(_valid(mask_ref, rows, older),dz, 0.0)
                hi[rows, :] = dz.astype(BF16)
            yield
            dzb = hi[...]
            dq_acc[g] += jnp.dot(dzb, kb, preferred_element_type=F32)
            dk_acc[pl.ds(start, n), _lanes(g)] += lax.dot_general(
                dzb, q2, (((0,), (0,)), ((), ())), preferred_element_type=F32)
            dv_acc[pl.ds(start, n), _lanes(g)] += lax.dot_general(
                wb[...], do2, (((0,), (0,)), ((), ())), preferred_element_type=F32)

        @pl.when(i == 0)
        def _():
            _in_lockstep([step(g, 0, 0, True) for g in range(G)])

        @pl.when(i > 0)
        def _():
            _in_lockstep([step(g, i - 1, QB, True) for g in range(G)])
            for g in range(G):
                _sweep_older(i, lambda j, _, g=g: _in_lockstep([step(g, j, 0, False)]), ce_ref.at[g], 1)

        for g in range(G):
            dq_ref[:, _lanes(g)] = (_unstack_heads(dq_acc[g]) * scale).astype(dq_ref.dtype)

        @pl.when(i == nb - 1)
        def _():
            dk_ref[...] = dk_acc[...].astype(dk_ref.dtype)
            dv_ref[...] = dv_acc[...].astype(dv_ref.dtype)

    n_steps = n_pairs // G
    blk = pl.BlockSpec((QB, W), lambda p, i: (i, p))
    col = pl.BlockSpec((L, W), lambda p, i: (0, p))
    return _call_with_job(
        body, job, name=name, grid=(n_steps, nb),
        in_specs=[blk,
                  pl.BlockSpec((L, W), lambda p, i: (0, n_steps + p)),
                  pl.BlockSpec((L, W), lambda p, i: (0, 2 * n_steps + p)),
                  blk, blk],
        out_specs=[blk, col, col],
        out_shape=[jax.ShapeDtypeStruct((L, sb_width), BF16)] * 3,
        scratch_shapes=[pltpu.VMEM((G, 2 * QB, LANES), F32), pltpu.VMEM((L, W), F32),
                        pltpu.VMEM((L, W), F32), pltpu.VMEM((G, 2 * QB, LANES), F32),
                        pltpu.VMEM((G, 2 * QB, LANES), F32), pltpu.VMEM((G, 2 * QB, LANES), F32),
                        pltpu.VMEM((G, 5, 2 * QB, 2 * QB), F32), pltpu.VMEM((G, 3, 2 * QB, 2 * QB), BF16),
                        pltpu.VMEM((2, 2 * QB, 2 * QB), BF16), pltpu.VMEM((2 * QB, 2 * QB), F32)],
        args=(qkv, qkv, qkv, o, dmixed))


def _conv_tile(L):
    return _pick(L, (384, 128))


def _glu(x, C):
    return x[:, :C] * _sigmoid(x[:, C:])


CONV_CHUNK = 32
SHIFT_TAIL = 24


def _fill_shifted(src_ref, dst_ref):
    n = dst_ref.shape[1]
    for r in range(1, 8):
        dst_ref[r - 1] = src_ref[r:r + n, :]


def _rows_at(src_ref, shifted_ref, start, n):
    q, r = divmod(start, 8)
    if r == 0:
        return src_ref[start:start + n, :]
    return shifted_ref[r - 1, 8 * q:8 * q + n, :]


def _conv_fwd(cacg, dw_w, dw_b, ln_g, ln_b, name, job=None):
    L, C2 = cacg.shape
    C = C2 // 2
    T = _conv_tile(L)
    H = CONV_HALO
    K = dw_w.shape[0]
    CH = CONV_CHUNK

    def body(x_ref, prev_ref, w_ref, b_ref, g_ref, beta_ref, o_ref, y_ref, u_ref, us_ref):
        i = pl.program_id(0)
        u_ref[0:H, :] = jnp.where(i > 0, _glu(prev_ref[...], C), 0.0)
        u_ref[H:, :] = _glu(x_ref[...], C)
        _fill_shifted(u_ref, us_ref)
        for c0 in range(0, T, CH):
            y = jnp.broadcast_to(b_ref[...], (CH, C))
            for k in range(K):
                y = y + w_ref[k:k + 1, :] * _rows_at(u_ref, us_ref, c0 + H - (K - 1) + k, CH)
            y_ref[c0:c0 + CH, :] = y
            mu = jnp.mean(y, axis=-1, keepdims=True)
            yc = y - mu
            rstd = lax.rsqrt(jnp.mean(yc * yc, axis=-1, keepdims=True) + EPS)
            ln = yc * rstd * g_ref[...] + beta_ref[...]
            o_ref[c0:c0 + CH, :] = (ln * _sigmoid(ln)).astype(o_ref.dtype)

    vec = pl.BlockSpec((1, C), lambda i: (0, 0))
    tile = pl.BlockSpec((T, C), lambda i: (i, 0))
    (out, y), arrived = _call_with_job(
        body, job, name=name, grid=(L // T,),
        in_specs=[pl.BlockSpec((T, C2), lambda i: (i, 0)),
                  pl.BlockSpec((H, C2), lambda i: (jnp.maximum(i * (T // H) - 1, 0), 0)),
                  pl.BlockSpec((K, C), lambda i: (0, 0)), vec, vec, vec],
        out_specs=[tile, tile],
        out_shape=[jax.ShapeDtypeStruct((L, C), BF16), jax.ShapeDtypeStruct((L, C), F32)],
        scratch_shapes=[pltpu.VMEM((T + H, C), F32), pltpu.VMEM((7, T + SHIFT_TAIL, C), F32)],
        args=(cacg, cacg, dw_w, dw_b, ln_g, ln_b))
    return out, y, arrived


def _conv_bwd(cacg, y, dmixed, dw_w, ln_g, ln_b, name, job=None):
    L, C2 = cacg.shape
    C = C2 // 2
    T = _conv_tile(L)
    H = CONV_HALO
    K = dw_w.shape[0]
    nt = L // T
    TE = T + H

    CH = CONV_CHUNK

    def body(x_ref, prev_ref, y_ref, ynext_ref, d_ref, dnext_ref, w_ref, g_ref, beta_ref,
             dca_ref, dcg_ref, dwt_ref, db_ref, dg_ref, dbeta_ref, u_ref, us_ref, dy_ref, dys_ref):
        i = pl.program_id(0)
        last = i == nt - 1

        @pl.when(i == 0)
        def _():
            dwt_ref[...] = jnp.zeros_like(dwt_ref)
            db_ref[...] = jnp.zeros_like(db_ref)
            dg_ref[...] = jnp.zeros_like(dg_ref)
            dbeta_ref[...] = jnp.zeros_like(dbeta_ref)

        u_ref[0:H, :] = jnp.where(i > 0, _glu(prev_ref[...], C), 0.0)
        u_ref[H:, :] = _glu(x_ref[...], C)
        _fill_shifted(u_ref, us_ref)
        dg_acc = jnp.zeros((1, C), F32)
        dbeta_acc = jnp.zeros((1, C), F32)
        db_acc = jnp.zeros((1, C), F32)
        for c0 in range(0, TE, CH):
            y = y_ref[c0:c0 + CH, :] if c0 < T else ynext_ref[c0 - T:c0 - T + CH, :]
            mu = jnp.mean(y, axis=-1, keepdims=True)
            yc = y - mu
            rstd = lax.rsqrt(jnp.mean(yc * yc, axis=-1, keepdims=True) + EPS)
            yh = yc * rstd
            ln = yh * g_ref[...] + beta_ref[...]
            s = _sigmoid(ln)
            dout = d_ref[c0:c0 + CH, :] if c0 < T else jnp.where(last, 0.0, dnext_ref[c0 - T:c0 - T + CH, :])
            dln = dout * (s * (1.0 + ln * (1.0 - s)))
            dyh = dln * g_ref[...]
            dy = rstd * (dyh - jnp.mean(dyh, axis=-1, keepdims=True)
                         - yh * jnp.mean(dyh * yh, axis=-1, keepdims=True))
            dy_ref[c0:c0 + CH, :] = dy
            if c0 < T:
                dg_acc = dg_acc + jnp.sum(dln * yh, axis=0, keepdims=True)
                dbeta_acc = dbeta_acc + jnp.sum(dln, axis=0, keepdims=True)
                db_acc = db_acc + jnp.sum(dy, axis=0, keepdims=True)
        dg_ref[...] += dg_acc
        dbeta_ref[...] += dbeta_acc
        db_ref[...] += db_acc
        _fill_shifted(dy_ref, dys_ref)
        for k in range(K):
            dwt_ref[k:k + 1, :] += jnp.sum(
                dy_ref[0:T, :] * _rows_at(u_ref, us_ref, H - (K - 1) + k, T), axis=0, keepdims=True)
        for c0 in range(0, T, CH):
            du = jnp.zeros((CH, C), F32)
            for k in range(K):
                du = du + w_ref[k:k + 1, :] * _rows_at(dy_ref, dys_ref, c0 + (K - 1) - k, CH)
            x = x_ref[c0:c0 + CH, :]
            sg = _sigmoid(x[:, C:])
            dca_ref[c0:c0 + CH, :] = (du * sg).astype(dca_ref.dtype)
            dcg_ref[c0:c0 + CH, :] = (du * x[:, :C] * sg * (1.0 - sg)).astype(dcg_ref.dtype)

    nh = L // H
    vec = pl.BlockSpec((1, C), lambda i: (0, 0))
    row = pl.BlockSpec((T, C), lambda i: (i, 0))
    after = lambda i: jnp.minimum((i + 1) * (T // H), nh - 1)
    return _call_with_job(
        body, job, name=name, grid=(nt,),
        in_specs=[pl.BlockSpec((T, C2), lambda i: (i, 0)),
                  pl.BlockSpec((H, C2), lambda i: (jnp.maximum(i * (T // H) - 1, 0), 0)),
                  row, pl.BlockSpec((H, C), lambda i: (after(i), 0)),
                  pl.BlockSpec((T, C), lambda i: (i, 1)), pl.BlockSpec((H, C), lambda i: (after(i), 1)),
                  pl.BlockSpec((K, C), lambda i: (0, 0)), vec, vec],
        out_specs=[row, row, pl.BlockSpec((H, C), lambda i: (0, 0)), vec, vec, vec],
        out_shape=[jax.ShapeDtypeStruct((L, C), BF16), jax.ShapeDtypeStruct((L, C), BF16),
                   jax.ShapeDtypeStruct((H, C), F32), jax.ShapeDtypeStruct((1, C), F32),
                   jax.ShapeDtypeStruct((1, C), F32), jax.ShapeDtypeStruct((1, C), F32)],
        scratch_shapes=[pltpu.VMEM((T + H, C), F32), pltpu.VMEM((7, T + SHIFT_TAIL, C), F32),
                        pltpu.VMEM((TE, C), F32), pltpu.VMEM((7, T + SHIFT_TAIL, C), F32)],
        args=(cacg, cacg, y, y, dmixed, dmixed, dw_w, ln_g, ln_b))


def _local_step(h0, target, n_meta, seq, norms, conv_p, wts, final_g, gather_behind, reducer):
    mix_g, ffn_g = norms
    dw_w, dw_b, ln_g, ln_b = conv_p
    depth = mix_g.shape[0]
    C = dw_b.shape[-1]
    sbw = (wts["w_in"][0].shape[-1] - 2 * C) // 3
    assert sbw == C, "the mixer halves must have equal width"
    row = lambda a, i: a[i][None, :]

    h = h0
    saved = []
    for i in range(depth):
        hn, proj_qkv, cacg = _norm_in_proj(h, row(mix_g, i), wts["w_in"][i], 3 * sbw, f"in_proj_{i}")
        def hosting(kind):
            job, keys = gather_behind.get((kind, i), (None, ()))

            def sink(arrived):
                for (wname, wl), arr in zip(keys, arrived):
                    wts[wname][wl] = arr

            return job, sink

        job, sink = hosting("attn")
        attn, arrived = _attn_fwd(proj_qkv, sbw, f"attn_fwd_{i}", job)
        sink(arrived)
        job, sink = hosting("conv")
        conv, conv_y, arrived = _conv_fwd(cacg, dw_w[i], row(dw_b, i), row(ln_g, i), row(ln_b, i),
                                          f"conv_fwd_{i}", job)
        sink(arrived)
        h_mid = _mm_rows([(attn, wts["w_out"][i], 0), (conv, wts["w_out"][i], 1)], f"out_proj_{i}", residual=h)
        job, sink = hosting("ffn")
        (hn2, g, u, act), arrived = _ffn_up(h_mid, row(ffn_g, i), wts["w_gate_t"][i], wts["w_up_t"][i],
                                            f"ffn_up_{i}", job)
        sink(arrived)
        job, sink = hosting("down")
        h_out = _mm_rows([(act, wts["w_down"][i], 0)], f"down_{i}", residual=h_mid, job=job)
        if job is not None:
            h_out, arrived = h_out
            sink(arrived)
        saved.append((h, hn, proj_qkv, cacg, attn, conv, conv_y, h_mid, hn2, g, u, act))
        h = h_out

    loss, dh, d_final_g = _loss_head(h, final_g[None, :], target, n_meta, seq, "loss_head")

    grads = {k: [None] * depth for k in ("mix_g", "ffn_g", "dw_w", "dw_b", "ln_g", "ln_b")}
    for i in reversed(range(depth)):
        h_in, hn, proj_qkv, cacg, attn, conv, conv_y, h_mid, hn2, g, u, act = saved[i]
        big = {}
        dg, du = _ffn_down_bwd(dh, wts["w_down"][i], g, u, f"ffn_down_bwd_{i}")
        big["w_down"] = _mm_tn(act, dh, f"dw_down_{i}", col_sharded=False)
        big["w_gate_t"] = _mm_tn(dg, hn2, f"dw_gate_{i}", col_sharded=False)
        big["w_up_t"] = _mm_tn(du, hn2, f"dw_up_{i}", col_sharded=False)
        sib_job, sib_sink = reducer.to_sibling(i, big)
        dh, d_ffn, arrived = _mm_rows([(dg, wts["w_gate_t"][i], 0), (du, wts["w_up_t"][i], 0)], f"d_hn2_{i}",
                                      norm_bwd=(h_mid, row(ffn_g, i), dh), job=sib_job)
        sib_sink(arrived)
        dmixed = _mm_rows([(dh, wts["w_out"][i], 0)], f"d_mixed_{i}", transposed=True)
        dw_out = jnp.concatenate([_mm_tn(attn, dh, f"dw_out_attn_{i}", col_sharded=False, chips=2),
                                  _mm_tn(conv, dh, f"dw_out_conv_{i}", col_sharded=False, chips=2)], axis=0)
        sib_job, sib_sink = reducer.to_sibling(i, {"w_out": dw_out})
        x_job, x_sink = reducer.take()
        (dq, dk, dv), arrived = _attn_bwd(proj_qkv, attn, dmixed, sbw, f"attn_bwd_{i}",
                                          _merge_jobs([sib_job, x_job]))
        sib_sink(arrived[:len(sib_job.out_shapes)])
        x_sink(arrived[len(sib_job.out_shapes):])
        x_job, x_sink = reducer.take()
        (dca, dcg, d_dw, d_b, d_lg, d_lb), arrived = _conv_bwd(
            cacg, conv_y, dmixed, dw_w[i], row(ln_g, i), row(ln_b, i), f"conv_bwd_{i}", x_job)
        x_sink(arrived)
        grads["dw_w"][i] = d_dw
        grads["dw_b"][i], grads["ln_g"][i], grads["ln_b"][i] = d_b, d_lg, d_lb
        dproj = jnp.concatenate([dq, dk, dv, dca, dcg], axis=1)
        reducer.ready(i, {"w_in": _mm_tn(hn, dproj, f"dw_in_{i}", col_sharded=True)})
        x_job, x_sink = reducer.take() if i == 0 else (None, None)
        res = _mm_rows([(dproj, wts["w_in"][i], 0)], f"d_hn_{i}", transposed=True,
                       norm_bwd=(h_in, row(mix_g, i), dh), job=x_job)
        dh, d_mix = res[:2]
        if x_job is not None:
            x_sink(res[2])
        grads["mix_g"][i], grads["ffn_g"][i] = d_mix, d_ffn
    grads["final_g"] = d_final_g
    return loss, dh, grads


ANY = pl.BlockSpec(memory_space=pl.ANY)


def _position():
    return lax.axis_index("x"), lax.axis_index("y"), lax.axis_index("c")


def _chip_at(x, y, k):
    return (1 - x if k & 2 else x), (1 - y if k & 1 else y)


def _half_rows(ref, half, rows, base=0):
    start = pl.multiple_of(base + half * rows, 8)
    lead = (slice(None),) * (len(ref.shape) - 2)
    return ref.at[(*lead, pl.ds(start, rows), slice(None))]


def _gather_job(fulls, shard_shapes, col_sharded):
    n = len(fulls)

    def tools(f_refs, send_sems, recv_sems):
        def block(wi, chip, half):
            _, R, C = shard_shapes[wi]
            if col_sharded[wi]:
                cols = pl.ds(pl.multiple_of(chip * C, LANES), C)
                return f_refs[wi].at[:, pl.ds(pl.multiple_of(half * (R // 2), 8), R // 2), cols]
            return _half_rows(f_refs[wi], half, R // 2, base=chip * R)

        def copy(wi, slot, blk, to):
            return pltpu.make_async_remote_copy(
                src_ref=blk, dst_ref=blk, send_sem=send_sems.at[6 * wi + slot],
                recv_sem=recv_sems.at[6 * wi + slot], device_id=to, device_id_type=MESH)

        return block, copy

    def start(_, f_refs, send_sems, recv_sems):
        block, copy = tools(f_refs, send_sems, recv_sems)
        x, y, c = _position()
        me = 2 * x + y
        for wi in range(n):
            for k in (1, 2, 3):
                copy(wi, k - 1, block(wi, me, c), (*_chip_at(x, y, k), c)).start()

    def finish(_, f_refs, send_sems, recv_sems):
        block, copy = tools(f_refs, send_sems, recv_sems)
        x, y, c = _position()
        me = 2 * x + y
        for wi in range(n):
            for k in (1, 2, 3):
                landed = block(wi, me ^ k, c)
                copy(wi, k - 1, landed, (x, y, c)).wait_recv()
                copy(wi, 2 + k, landed, (x, y, 1 - c)).start()
        for wi in range(n):
            for k in (1, 2, 3):
                copy(wi, 2 + k, block(wi, me ^ k, 1 - c), (x, y, c)).wait_recv()
        for wi in range(n):
            for k in (1, 2, 3):
                copy(wi, k - 1, block(wi, me, c), (x, y, c)).wait_send()
                copy(wi, 2 + k, block(wi, me ^ k, c), (x, y, c)).wait_send()

    return _CommJob(fulls, [jax.ShapeDtypeStruct(f.shape, f.dtype) for f in fulls], {i: i for i in range(n)},
                    6 * n, start, finish)


def _place_shard(w, layer, chip, col_sharded, dtype, name):
    _, R, C = w.shape
    tr = _pick(R, (512, 704, 256, 128, 48))
    nr = R // tr

    def body(chip_ref, w_ref, o_ref):
        o_ref[...] = w_ref[...].astype(dtype)

    if col_sharded:
        shape = (1, R, N_CHIPS * C)
        out_spec = pl.BlockSpec((None, tr, C), lambda r, chip_ref: (0, r, chip_ref[0]))
    else:
        shape = (1, N_CHIPS * R, C)
        out_spec = pl.BlockSpec((None, tr, C), lambda r, chip_ref: (0, chip_ref[0] * nr + r, 0))
    grid_spec = pltpu.PrefetchScalarGridSpec(
        num_scalar_prefetch=1, grid=(nr,),
        in_specs=[pl.BlockSpec((None, tr, C), lambda r, chip_ref: (layer, r, 0))], out_specs=out_spec)
    return pl.pallas_call(
        body, name=name, grid_spec=grid_spec, out_shape=jax.ShapeDtypeStruct(shape, dtype),
        compiler_params=_params(("parallel",)),
    )(chip, w)


def _sibling_job(grads):
    n = len(grads)

    def copies(g_refs, l_refs, send_sems, recv_sems):
        x, y, c = _position()
        return [pltpu.make_async_remote_copy(
            src_ref=_half_rows(g_refs[wi], 1 - c, grads[wi].shape[1] // 2), dst_ref=l_refs[wi],
            send_sem=send_sems.at[wi], recv_sem=recv_sems.at[wi],
            device_id=(x, y, 1 - c), device_id_type=MESH) for wi in range(n)]

    def start(*refs):
        for cp in copies(*refs):
            cp.start()

    def finish(*refs):
        for cp in copies(*refs):
            cp.wait()

    outs = [jax.ShapeDtypeStruct((g.shape[0], g.shape[1] // 2, g.shape[2]), g.dtype) for g in grads]
    return _CommJob(grads, outs, {}, n, start, finish)


def _chip_sum(g, landed, core, name):
    _, R, C = g.shape
    hr = R // 2
    tr = _pick(hr, (512, 352, 256, 128))
    nr = hr // tr

    def body(c_ref, g_ref, l_ref, o_ref):
        o_ref[...] = (g_ref[...] + l_ref[...]).astype(BF16)

    grid_spec = pltpu.PrefetchScalarGridSpec(
        num_scalar_prefetch=1, grid=(N_CHIPS, nr),
        in_specs=[pl.BlockSpec((None, tr, C), lambda j, r, c_ref: (j, c_ref[0] * nr + r, 0)),
                  pl.BlockSpec((None, tr, C), lambda j, r, c_ref: (j, r, 0))],
        out_specs=pl.BlockSpec((None, tr, C), lambda j, r, c_ref: (j, r, 0)))
    return pl.pallas_call(
        body, name=name, grid_spec=grid_spec, out_shape=jax.ShapeDtypeStruct((N_CHIPS, hr, C), BF16),
        compiler_params=_params(("parallel", "parallel")),
    )(core, g, landed)


def _across_job(parts):
    n = len(parts)

    def copy(p_refs, l_refs, send_sems, recv_sems, wi, k, to):
        x, y, _ = _position()
        me = 2 * x + y
        return pltpu.make_async_remote_copy(
            src_ref=p_refs[wi].at[me ^ k], dst_ref=l_refs[wi].at[me],
            send_sem=send_sems.at[3 * wi + k - 1], recv_sem=recv_sems.at[3 * wi + k - 1],
            device_id=to, device_id_type=MESH)

    def start(p_refs, l_refs, send_sems, recv_sems):
        x, y, c = _position()
        for wi in range(n):
            for k in (1, 2, 3):
                copy(p_refs, l_refs, send_sems, recv_sems, wi, k, (*_chip_at(x, y, k), c)).start()

    def finish(p_refs, l_refs, send_sems, recv_sems):
        x, y, c = _position()
        me = 2 * x + y
        for wi in range(n):
            for k in (1, 2, 3):
                slot = l_refs[wi].at[me ^ k]
                pltpu.make_async_remote_copy(
                    src_ref=slot, dst_ref=slot, send_sem=send_sems.at[3 * wi + k - 1],
                    recv_sem=recv_sems.at[3 * wi + k - 1], device_id=(x, y, c), device_id_type=MESH).wait_recv()
        for wi in range(n):
            for k in (1, 2, 3):
                copy(p_refs, l_refs, send_sems, recv_sems, wi, k, (x, y, c)).wait_send()

    return _CommJob(parts, [jax.ShapeDtypeStruct(p.shape, p.dtype) for p in parts], {}, 3 * n, start, finish)


class _Reducer:
    def __init__(self, core):
        self.core, self.parts, self.across, self.pending = core, {}, {}, []

    def to_sibling(self, layer, big):
        names = list(big)
        flat = [big[k] for k in names]

        def sink(landed):
            for k, g, la in zip(names, flat, landed):
                self.parts[k, layer] = _chip_sum(g, la, self.core, f"chip_sum_{k}_{layer}")
                self.pending.append((k, layer))

        return _sibling_job(flat), sink

    def ready(self, layer, big):
        job, sink = self.to_sibling(layer, big)
        sink(_run_job(job, f"grads_to_sibling_{next(iter(big))}_{layer}"))

    def take(self):
        keys, self.pending = self.pending, []
        if not keys:
            return None, lambda results: None

        def sink(results):
            self.across.update(zip(keys, results))

        return _across_job([self.parts[key] for key in keys]), sink


def _sum_chips(parts, landed, where, layer, depth, prev, name):
    _, hr, C = landed.shape
    tr = _pick(hr, (512, 352, 256, 128))
    nr = hr // tr

    def body(*refs):
        own_ref, slots, o_ref = refs[1], refs[2:2 + N_CHIPS], refs[-1]
        chip = refs[0][0]
        total = None
        for q in range(N_CHIPS):
            term = jnp.where(chip == q, own_ref[...], slots[q][...]).astype(F32)
            total = term if total is None else total + term
        o_ref[...] = total

    def slot_spec(q):
        return pl.BlockSpec((None, tr, C), lambda r, w: (jnp.where(w[0] == q, (q + 1) % N_CHIPS, q), r, 0))

    in_specs = [pl.BlockSpec((None, tr, C), lambda r, w: (w[0], r, 0))] + [slot_spec(q) for q in range(N_CHIPS)]
    args = [where, parts] + [landed] * N_CHIPS
    aliases = {}
    if prev is not None:
        in_specs.append(ANY)
        args.append(prev)
        aliases = {len(args) - 1: 0}
    grid_spec = pltpu.PrefetchScalarGridSpec(
        num_scalar_prefetch=1, grid=(nr,), in_specs=in_specs,
        out_specs=pl.BlockSpec((None, tr, C), lambda r, w: (layer, w[1] * nr + r, 0)))
    return pl.pallas_call(
        body, name=name, grid_spec=grid_spec, out_shape=jax.ShapeDtypeStruct((depth, 2 * hr, C), F32),
        input_output_aliases=aliases, compiler_params=_params(("parallel",)),
    )(*args)


def _rs_join_halves(reduced):
    n = len(reduced)

    def body(*refs):
        o_refs = refs[n:2 * n]
        send_sems, recv_sems = refs[2 * n:]
        x, y, c = _position()
        sent = []
        for wi in range(n):
            hr = reduced[wi].shape[1] // 2
            mine = _half_rows(o_refs[wi], c, hr)
            cp = pltpu.make_async_remote_copy(
                src_ref=mine, dst_ref=mine, send_sem=send_sems.at[wi], recv_sem=recv_sems.at[wi],
                device_id=(x, y, 1 - c), device_id_type=MESH)
            cp.start()
            sent.append(cp)
        for wi in range(n):
            hr = reduced[wi].shape[1] // 2
            theirs = _half_rows(o_refs[wi], 1 - c, hr)
            pltpu.make_async_remote_copy(
                src_ref=theirs, dst_ref=theirs, send_sem=send_sems.at[wi], recv_sem=recv_sems.at[wi],
                device_id=(x, y, c), device_id_type=MESH).wait_recv()
        for cp in sent:
            cp.wait_send()

    return pl.pallas_call(
        body, name="grads_join_halves", out_shape=[jax.ShapeDtypeStruct(r.shape, r.dtype) for r in reduced],
        in_specs=[ANY] * n, out_specs=[ANY] * n, input_output_aliases={i: i for i in range(n)},
        scratch_shapes=[pltpu.SemaphoreType.DMA((n,)), pltpu.SemaphoreType.DMA((n,))],
    )(*reduced)


def _adam_math(w, g, m, v):
    m = ADAM_B1 * m + (1.0 - ADAM_B1) * g
    v = ADAM_B2 * v + (1.0 - ADAM_B2) * jnp.square(g)
    m_hat = m / (1.0 - ADAM_B1 ** ADAM_STEP)
    v_hat = v / (1.0 - ADAM_B2 ** ADAM_STEP)
    delta = -ADAM_LR * (m_hat / (jnp.sqrt(v_hat) + ADAM_EPS) + ADAM_WD * w)
    return delta, m, v


def _small_reduce_adam(vec_grads, dww_grads, final_grad, meta_grad, state):
    vec_names = list(vec_grads)
    depth = len(dww_grads)
    D = final_grad.shape[1]
    n_meta = meta_grad.shape[0]
    taps_pad, C = dww_grads[0].shape
    names = vec_names + ["final", "meta", "dww"]
    at, row0 = 0, {}
    for k in vec_names:
        row0[k] = at
        at += depth
    row0["final"] = at
    at = -(-(at + 1) // 8) * 8
    row0["meta"] = at
    at += -(-n_meta // 8) * 8
    row0["dww"] = at
    rows = at + depth * taps_pad
    lanes = max(D, C)
    n_g = len(vec_names) * depth + depth + 2

    def body(*refs):
        g_refs = refs[:n_g]
        st = refs[n_g:n_g + 3 * len(names)]
        outs = refs[n_g + 3 * len(names):n_g + 7 * len(names)]
        slab, land, send_sems, recv_sems = refs[n_g + 7 * len(names):]
        x, y, c = _position()
        me = 4 * x + 2 * y + c
        chip = 2 * x + y
        slab[...] = jnp.zeros_like(slab)
        it = iter(g_refs)
        for k in vec_names:
            for l in range(depth):
                g_ref = next(it)
                slab[row0[k] + l:row0[k] + l + 1, 0:g_ref.shape[1]] = g_ref[...]
        for l in range(depth):
            slab[row0["dww"] + l * taps_pad:row0["dww"] + (l + 1) * taps_pad, 0:C] = next(it)[...]
        slab[row0["final"]:row0["final"] + 1, 0:D] = next(it)[...]
        slab[row0["meta"]:row0["meta"] + n_meta, 0:D] = next(it)[...]
        sent = []
        for k in range(1, N_DEV):
            to = (1 - x if k & 4 else x, 1 - y if k & 2 else y, 1 - c if k & 1 else c)
            cp = pltpu.make_async_remote_copy(
                src_ref=slab, dst_ref=land.at[k], send_sem=send_sems.at[k - 1], recv_sem=recv_sems.at[k - 1],
                device_id=to, device_id_type=MESH)
            cp.start()
            sent.append(cp)
        land[0] = slab[...]
        for cp in sent:
            cp.wait_recv()
        for cp in sent:
            cp.wait_send()
        total = land[me]
        for e in range(1, N_DEV):
            total = total + land[me ^ e]
        slab[...] = total

        def mine(r0, n_rows, width):
            got = slab[r0:r0 + n_rows, 0:width]
            for j in range(1, N_CHIPS):
                got = jnp.where(chip == j, slab[r0:r0 + n_rows, j * width:(j + 1) * width], got)
            return got

        def update(i, g, index=()):
            w_ref, m_ref, v_ref = st[3 * i:3 * i + 3]
            o = outs[4 * i:4 * i + 4]
            at = index if index else Ellipsis
            res = (g,) + _adam_math(w_ref[at], g, m_ref[at], v_ref[at])
            for o_ref, val in zip(o, res):
                o_ref[at] = val

        for i, k in enumerate(vec_names):
            width = st[3 * i].shape[1]
            update(i, slab[row0[k]:row0[k] + depth, 0:width])
        base = len(vec_names)
        update(base, slab[row0["final"]:row0["final"] + 1, 0:D])
        update(base + 1, mine(row0["meta"], n_meta, D // N_CHIPS))
        taps = st[3 * (base + 2)].shape[1]
        for l in range(depth):
            update(base + 2, mine(row0["dww"] + l * taps_pad, taps, C // N_CHIPS), (l,))

    flat_g = [g for k in vec_names for g in vec_grads[k]] + list(dww_grads) + [final_grad, meta_grad]
    flat_state = [a for k in names for a in state[k]]
    vmem = pl.BlockSpec(memory_space=pltpu.VMEM)
    out_shape = [jax.ShapeDtypeStruct(state[k][0].shape, F32) for k in names for _ in range(4)]
    res = pl.pallas_call(
        body, name="small_reduce_adam", out_shape=out_shape,
        in_specs=[vmem] * (len(flat_g) + len(flat_state)), out_specs=[vmem] * len(out_shape),
        scratch_shapes=[pltpu.VMEM((rows, lanes), F32), pltpu.VMEM((N_DEV, rows, lanes), F32),
                        pltpu.SemaphoreType.DMA((N_DEV - 1,)), pltpu.SemaphoreType.DMA((N_DEV - 1,))],
    )(*flat_g, *flat_state)
    return {k: tuple(res[4 * i:4 * i + 4]) for i, k in enumerate(names)}


def _adam(w, g, m, v, name):
    def body(w_ref, g_ref, m_ref, v_ref, d_ref, nm_ref, nv_ref):
        d_ref[...], nm_ref[...], nv_ref[...] = _adam_math(w_ref[...], g_ref[...], m_ref[...], v_ref[...])

    lyr, R, C = w.shape
    tr = _pick(R, (512, 704, 256, 128))
    blk = pl.BlockSpec((None, tr, C), lambda l, r: (l, r, 0))
    return pl.pallas_call(
        body, name=name, grid=(lyr, R // tr), in_specs=[blk] * 4, out_specs=[blk] * 3,
        out_shape=[jax.ShapeDtypeStruct(w.shape, F32)] * 3, compiler_params=_params(("parallel", "parallel")),
    )(w, g, m, v)


def _rows(a, pad_to=8):
    r = a.reshape(-1, LANES)
    extra = (-r.shape[0]) % pad_to
    return jnp.pad(r, ((0, extra), (0, 0))) if extra else r


def _pack(arrays):
    return jnp.concatenate([_rows(a) for a in arrays], axis=0)


def _unpack(slab, shapes):
    out, at = [], 0
    for shp in shapes:
        nrow = math.prod(shp) // LANES
        out.append(slab[at:at + nrow].reshape(shp))
        at += nrow + (-nrow) % 8
    return out


BIG = ("w_in", "w_out", "w_gate_t", "w_up_t", "w_down")
BIG_COL_SHARDED = (True, False, False, False, False)
TRANSPOSED = {"w_gate_t": "w_gate", "w_up_t": "w_up"}


def kernel(x, meta_tokens, mix_norm_g, w_in, conv_dw_w, conv_dw_b, conv_ln_g, conv_ln_b, w_out, ffn_norm_g, w_gate, w_up, w_down, final_norm_g, loss_target, m_meta_tokens, m_mix_norm_g, m_w_in, m_conv_dw_w, m_conv_dw_b, m_conv_ln_g, m_conv_ln_b, m_w_out, m_ffn_norm_g, m_w_gate, m_w_up, m_w_down, m_final_norm_g, v_meta_tokens, v_mix_norm_g, v_w_in, v_conv_dw_w, v_conv_dw_b, v_conv_ln_g, v_conv_ln_b, v_w_out, v_ffn_norm_g, v_w_gate, v_w_up, v_w_down, v_final_norm_g):
    n_meta, seq = meta_tokens.shape[0], x.shape[1]
    D = x.shape[2]
    depth, taps, c_shard = conv_dw_w.shape
    C = conv_dw_b.shape[-1]
    chip = (2 * lax.axis_index("x") + lax.axis_index("y")).astype(jnp.int32)
    core = lax.axis_index("c").astype(jnp.int32).reshape(1)
    chip1 = chip.reshape(1)
    where = jnp.concatenate([chip1, core])
    big_w = dict(w_in=w_in, w_out=w_out, w_gate=w_gate, w_up=w_up, w_down=w_down)
    big_m = dict(w_in=m_w_in, w_out=m_w_out, w_gate=m_w_gate, w_up=m_w_up, w_down=m_w_down)
    big_v = dict(w_in=v_w_in, w_out=v_w_out, w_gate=v_w_gate, w_up=v_w_up, w_down=v_w_down)

    small_shard = _pack([conv_dw_w, meta_tokens])[None]
    to_send = {k: jnp.swapaxes(big_w[TRANSPOSED[k]], 1, 2) if k in TRANSPOSED else big_w[k] for k in BIG}
    col = dict(zip(BIG, BIG_COL_SHARDED))
    wts = {k: [_place_shard(to_send[k], l, chip1, col[k], BF16, f"place_{k}_{l}") for l in range(depth)] for k in BIG}
    small_placed = _place_shard(small_shard, 0, chip1, False, F32, "place_small")

    def gather_job(keys, extra=()):
        arrays = [wts[k][l] for k, l in keys] + list(extra)
        shapes = [(1,) + to_send[k].shape[1:] for k, _ in keys] + [(1,) + small_shard.shape[1:]] * len(extra)
        return _gather_job(arrays, shapes, [col[k] for k, _ in keys] + [False] * len(extra))

    first_keys = [("w_in", 0)]
    *first, small_full = _run_job(gather_job(first_keys, [small_placed]), "gather_first")
    for (k, l), arr in zip(first_keys, first):
        wts[k][l] = arr
    behind_keys = {("attn", 0): [("w_out", 0), ("w_gate_t", 0), ("w_up_t", 0)], ("conv", 0): [("w_down", 0)]}
    for l in range(1, depth):
        behind_keys["ffn", l - 1] = [("w_in", l), ("w_gate_t", l)]
        behind_keys["down", l - 1] = [("w_up_t", l)]
        behind_keys["attn", l] = [("w_out", l), ("w_down", l)]
    gather_behind = {host: (gather_job(keys), keys) for host, keys in behind_keys.items()}

    rows_shard = small_shard.shape[1]
    dw_full, meta_full = [], []
    for j in range(N_CHIPS):
        dwj, mj = _unpack(small_full[0, j * rows_shard:(j + 1) * rows_shard],
                          [conv_dw_w.shape, meta_tokens.shape])
        dw_full.append(dwj)
        meta_full.append(mj)
    dw_w_full = jnp.concatenate(dw_full, axis=2)
    meta = jnp.concatenate(meta_full, axis=1)

    L = n_meta + seq
    Lp = -(-L // QUERY_BLOCK) * QUERY_BLOCK
    h0 = jnp.concatenate([meta, x[0], jnp.zeros((Lp - L, D), F32)], axis=0)
    target = jnp.pad(loss_target[0], ((n_meta, Lp - L), (0, 0)))
    reducer = _Reducer(core)
    loss, dh0, grads = _local_step(h0, target, n_meta, seq, (mix_norm_g, ffn_norm_g),
                                   (dw_w_full, conv_dw_b, conv_ln_g, conv_ln_b), wts, final_norm_g,
                                   gather_behind, reducer)
    loss = lax.psum(loss[0, 0], ("x", "y", "c"))
    grad_x = dh0[n_meta:L][None]

    reduced = []
    for k in BIG:
        arr = None
        for l in range(depth):
            arr = _sum_chips(reducer.parts[k, l], reducer.across[k, l], where, l, depth, arr, f"sum_chips_{k}_{l}")
        reduced.append(arr)
    big_g = dict(zip(BIG, _rs_join_halves(reduced)))

    out_g, out_d, out_m, out_v = {}, {}, {}, {}
    for kk in BIG:
        k = TRANSPOSED.get(kk, kk)
        view = (lambda a: jnp.swapaxes(a, 1, 2)) if kk in TRANSPOSED else (lambda a: a)
        res = _adam(view(big_w[k]), big_g[kk], view(big_m[k]), view(big_v[k]), f"adam_{k}")
        out_g[k] = view(big_g[kk])
        out_d[k], out_m[k], out_v[k] = (view(a) for a in res)

    as_row = lambda a: a.reshape(1, -1)
    state = dict(mix_g=(mix_norm_g, m_mix_norm_g, v_mix_norm_g), ffn_g=(ffn_norm_g, m_ffn_norm_g, v_ffn_norm_g),
                 dw_b=(conv_dw_b, m_conv_dw_b, v_conv_dw_b), ln_g=(conv_ln_g, m_conv_ln_g, v_conv_ln_g),
                 ln_b=(conv_ln_b, m_conv_ln_b, v_conv_ln_b),
                 final=(as_row(final_norm_g), as_row(m_final_norm_g), as_row(v_final_norm_g)),
                 meta=(meta_tokens, m_meta_tokens, v_meta_tokens), dww=(conv_dw_w, m_conv_dw_w, v_conv_dw_w))
    vec_names = ("mix_g", "ffn_g", "dw_b", "ln_g", "ln_b")
    small = _small_reduce_adam({k: grads[k] for k in vec_names}, grads["dw_w"], grads["final_g"], dh0[:n_meta], state)
    out_name = dict(mix_g="mix_norm_g", ffn_g="ffn_norm_g", dw_b="conv_dw_b", ln_g="conv_ln_g", ln_b="conv_ln_b",
                    final="final_norm_g", meta="meta_tokens", dww="conv_dw_w")
    for k, res in small.items():
        if k == "final":
            res = tuple(a.reshape(-1) for a in res)
        out_g[out_name[k]], out_d[out_name[k]], out_m[out_name[k]], out_v[out_name[k]] = res

    order = ("meta_tokens", "mix_norm_g", "w_in", "conv_dw_w", "conv_dw_b", "conv_ln_g", "conv_ln_b", "w_out",
             "ffn_norm_g", "w_gate", "w_up", "w_down", "final_norm_g")
    return (loss, grad_x, *[out_g[k] for k in order], *[out_d[k] for k in order],
            *[out_m[k] for k in order], *[out_v[k] for k in order])
```

```python
import functools
import math

import jax
import jax.numpy as jnp
from jax import lax
from jax.experimental import pallas as pl
from jax.experimental.pallas import tpu as pltpu

F32 = jnp.float32
BF16 = jnp.bfloat16
MESH = pl.DeviceIdType.MESH

EPS = 1e-6
QUERY_BLOCK = 128
LANES = 128
HEAD_DIM = 64
LOG_STICK_FLOOR = -40.0
CONV_HALO = 32
N_CHIPS = 4
N_DEV = 8
VMEM_LIMIT = 56 * 1024 * 1024

ADAM_LR = 0.001
ADAM_B1 = 0.9
ADAM_B2 = 0.999
ADAM_EPS = 1e-08
ADAM_WD = 0.01
ADAM_STEP = 10


def _pick(n, prefs):
    for p in prefs:
        if n % p == 0:
            return p
    return n


def _params(sem=None):
    return pltpu.CompilerParams(dimension_semantics=sem, vmem_limit_bytes=VMEM_LIMIT)


def _sigmoid(x):
    return 1.0 / (1.0 + jnp.exp(-x))


def _loss_head(h, g, target, n_meta, seq, name):
    L, D = h.shape
    T = _pick(L, (384, 128))

    def body(h_ref, g_ref, t_ref, loss_ref, dh_ref, dg_ref):
        i = pl.program_id(0)
        x = h_ref[...]
        gv = g_ref[...]
        r = lax.rsqrt(jnp.mean(x * x, axis=-1, keepdims=True) + EPS)
        xh = x * r
        y = xh * gv
        rows = i * T + lax.broadcasted_iota(jnp.int32, (T, 1), 0)
        live = (rows >= n_meta) & (rows < n_meta + seq)
        diff = jnp.where(live, y - t_ref[...], 0.0)
        dyv = diff / D
        dxh = dyv * gv
        dh_ref[...] = r * (dxh - xh * jnp.mean(dxh * xh, axis=-1, keepdims=True))

        @pl.when(i == 0)
        def _():
            dg_ref[...] = jnp.zeros_like(dg_ref)
            loss_ref[...] = jnp.zeros_like(loss_ref)

        dg_ref[...] += jnp.sum(dyv * xh, axis=0, keepdims=True)
        per_row = jnp.mean(diff * diff, axis=-1, keepdims=True)
        loss_ref[...] += 0.5 * jnp.sum(per_row, axis=0, keepdims=True)

    row = pl.BlockSpec((T, D), lambda i: (i, 0))
    vec = pl.BlockSpec((1, D), lambda i: (0, 0))
    one = pl.BlockSpec((1, 1), lambda i: (0, 0))
    return pl.pallas_call(
        body, name=name, grid=(L // T,),
        in_specs=[row, vec, row], out_specs=[one, row, vec],
        out_shape=[jax.ShapeDtypeStruct((1, 1), F32), jax.ShapeDtypeStruct((L, D), F32),
                   jax.ShapeDtypeStruct((1, D), F32)],
        compiler_params=_params(("arbitrary",)),
    )(h, g, target)


def _ffn_tiles(M, F):
    return _pick(M, (352, 384, 128)), _pick(F, (1408, 512, 256, 128))


def _resident(shape):
    return pl.BlockSpec((None,) + tuple(shape[1:]), lambda *_: (0,) * len(shape), pipeline_mode=pl.Buffered(1))


def _norm_in_proj(h, g, w_in, n_qkv, name):
    M, D = h.shape
    N = w_in.shape[2]
    tm = _pick(M, (352, 384, 128))
    tn = _pick(math.gcd(n_qkv, N - n_qkv), (512, 256, 128))

    def body(h_ref, g_ref, w_ref, hn_ref, qkv_ref, rest_ref):
        x = h_ref[...]
        r = lax.rsqrt(jnp.mean(x * x, axis=-1, keepdims=True) + EPS)
        hv = (x * r * g_ref[...]).astype(BF16)
        hn_ref[...] = hv
        for c in range(0, N, tn):
            out = jnp.dot(hv, w_ref[:, c:c + tn], preferred_element_type=F32)
            if c < n_qkv:
                qkv_ref[:, c:c + tn] = out.astype(qkv_ref.dtype)
            else:
                rest_ref[:, c - n_qkv:c - n_qkv + tn] = out

    rows = pl.BlockSpec((tm, D), lambda i: (i, 0))
    return pl.pallas_call(
        body, name=name, grid=(M // tm,),
        in_specs=[rows, pl.BlockSpec((1, D), lambda i: (0, 0)), _resident(w_in.shape)],
        out_specs=[rows, pl.BlockSpec((tm, n_qkv), lambda i: (i, 0)), pl.BlockSpec((tm, N - n_qkv), lambda i: (i, 0))],
        out_shape=[jax.ShapeDtypeStruct((M, D), BF16), jax.ShapeDtypeStruct((M, n_qkv), BF16),
                   jax.ShapeDtypeStruct((M, N - n_qkv), F32)],
        compiler_params=_params(("parallel",)),
    )(h, g, w_in)


def _ffn_up(h, norm_g, w_gate_t, w_up_t, name, job=None):
    M, D = h.shape
    F = w_gate_t.shape[1]
    tm, tf = _ffn_tiles(M, F)
    nt = (((1,), (1,)), ((), ()))

    def body(h_ref, ng_ref, wg_ref, wu_ref, hn_ref, g_ref, u_ref, a_ref):
        x = h_ref[...]
        r = lax.rsqrt(jnp.mean(x * x, axis=-1, keepdims=True) + EPS)
        hv = (x * r * ng_ref[...]).astype(BF16)
        hn_ref[...] = hv
        for c in range(0, F, tf):
            gv = lax.dot_general(hv, wg_ref[c:c + tf, :], nt, preferred_element_type=F32)
            uv = lax.dot_general(hv, wu_ref[c:c + tf, :], nt, preferred_element_type=F32)
            g_ref[:, c:c + tf] = gv.astype(g_ref.dtype)
            u_ref[:, c:c + tf] = uv.astype(u_ref.dtype)
            a_ref[:, c:c + tf] = (gv * _sigmoid(gv) * uv).astype(a_ref.dtype)

    rows = pl.BlockSpec((tm, D), lambda i: (i, 0))
    wide = pl.BlockSpec((tm, F), lambda i: (i, 0))
    return _call_with_job(
        body, job, name=name, grid=(M // tm,),
        in_specs=[rows, pl.BlockSpec((1, D), lambda i: (0, 0)), _resident(w_gate_t.shape), _resident(w_up_t.shape)],
        out_specs=[rows, wide, wide, wide],
        out_shape=[jax.ShapeDtypeStruct((M, D), BF16), jax.ShapeDtypeStruct((M, F), BF16),
                   jax.ShapeDtypeStruct((M, F), BF16), jax.ShapeDtypeStruct((M, F), BF16)],
        scratch_shapes=[], args=(h, norm_g, w_gate_t, w_up_t))


def _ffn_down_bwd(dh, w_down, g, u, name):
    M, D = dh.shape
    F = g.shape[1]
    tm, tf = _ffn_tiles(M, F)

    def body(d_ref, w_ref, g_ref, u_ref, dg_ref, du_ref):
        dhv = d_ref[...].astype(BF16)
        for c in range(0, F, tf):
            dv = lax.dot_general(dhv, w_ref[c:c + tf, :], (((1,), (1,)), ((), ())), preferred_element_type=F32)
            gv = g_ref[:, c:c + tf].astype(F32)
            s = _sigmoid(gv)
            du_ref[:, c:c + tf] = (dv * (gv * s)).astype(du_ref.dtype)
            dg_ref[:, c:c + tf] = (dv * u_ref[:, c:c + tf].astype(F32)
                                   * (s * (1.0 + gv * (1.0 - s)))).astype(dg_ref.dtype)

    wide = pl.BlockSpec((tm, F), lambda i: (i, 0))
    return pl.pallas_call(
        body, name=name, grid=(M // tm,),
        in_specs=[pl.BlockSpec((tm, D), lambda i: (i, 0)), _resident(w_down.shape), wide, wide],
        out_specs=[wide, wide],
        out_shape=[jax.ShapeDtypeStruct((M, F), BF16), jax.ShapeDtypeStruct((M, F), BF16)],
        compiler_params=_params(("parallel",)),
    )(dh, w_down, g, u)


def _mm_rows(pairs, name, transposed=False, residual=None, norm_bwd=None, job=None):
    M = pairs[0][0].shape[0]
    N = pairs[0][1].shape[1 if transposed else 2]
    shallow = sum(p[0].shape[1] for p in pairs) <= 1024
    tm = _pick(M, ((1056,) if shallow else ()) + (352, 384, 128))
    tn = _pick(N, (512, 256, 128))
    n = len(pairs)

    def body(*refs):
        a_refs, w_refs, extra = refs[:n], refs[n:2 * n], refs[2 * n:]
        p_ref = extra[-1] if norm_bwd is not None else extra[-1 if residual is None else 1]
        lhs = [a_ref[...].astype(BF16) for a_ref in a_refs]
        for c in range(0, N, tn):
            acc = None
            for a, w_ref, (_, _, blk) in zip(lhs, w_refs, pairs):
                k = a.shape[1]
                if transposed:
                    d = lax.dot_general(a, w_ref[c:c + tn, blk * k:(blk + 1) * k], (((1,), (1,)), ((), ())),
                                        preferred_element_type=F32)
                else:
                    d = jnp.dot(a, w_ref[blk * k:(blk + 1) * k, c:c + tn], preferred_element_type=F32)
                acc = d if acc is None else acc + d
            if residual is not None:
                acc = acc + extra[0][:, c:c + tn]
            p_ref[:, c:c + tn] = acc
        if norm_bwd is not None:
            h_ref, g_ref, dhin_ref, dh_ref, dg_ref, _ = extra
            x = h_ref[...]
            dyv = p_ref[...]
            r = lax.rsqrt(jnp.mean(x * x, axis=-1, keepdims=True) + EPS)
            xh = x * r
            dxh = dyv * g_ref[...]
            dh_ref[...] = dhin_ref[...] + r * (dxh - xh * jnp.mean(dxh * xh, axis=-1, keepdims=True))

            @pl.when(pl.program_id(0) == 0)
            def _():
                dg_ref[...] = jnp.zeros_like(dg_ref)

            dg_ref[...] += jnp.sum(dyv * xh, axis=0, keepdims=True)

    in_specs = [pl.BlockSpec((tm, a.shape[1]), lambda i: (i, 0)) for a, _, _ in pairs]
    in_specs += [_resident(w.shape) for _, w, _ in pairs]
    args = [p[0] for p in pairs] + [p[1] for p in pairs]
    wide = pl.BlockSpec((tm, N), lambda i: (i, 0))
    vec = pl.BlockSpec((1, N), lambda i: (0, 0))
    out_specs, out_shape, scratch = [wide], [jax.ShapeDtypeStruct((M, N), F32)], []
    if residual is not None:
        in_specs.append(wide)
        args.append(residual)
    if norm_bwd is not None:
        in_specs += [wide, vec, wide]
        args += list(norm_bwd)
        out_specs.append(vec)
        out_shape.append(jax.ShapeDtypeStruct((1, N), F32))
        scratch = [pltpu.VMEM((tm, N), F32)]
    outs, arrived = _call_with_job(body, job, name=name, grid=(M // tm,), in_specs=in_specs, out_specs=out_specs,
                                   out_shape=out_shape, scratch_shapes=scratch, args=args)
    res = tuple(outs) if norm_bwd is not None else (outs[0],)
    res = res if job is None else res + (arrived,)
    return res[0] if len(res) == 1 else res


def _mm_tn(a, b, name, col_sharded, chips=N_CHIPS):
    M, K = a.shape
    N = b.shape[1]
    tm = _pick(M, (1056, 384, 128))
    tk = _pick(K, (1024, 1408, 512, 256, 128))
    tn = N // N_CHIPS if col_sharded else _pick(N, (1024, 512, 128))

    def body(a_ref, b_ref, o_ref):
        @pl.when(pl.program_id(2) == 0)
        def _():
            o_ref[...] = jnp.zeros_like(o_ref)

        o_ref[...] += lax.dot_general(a_ref[...].astype(BF16), b_ref[...].astype(BF16),
                                      (((0,), (0,)), ((), ())), preferred_element_type=F32)

    if col_sharded:
        out_shape = jax.ShapeDtypeStruct((N_CHIPS, K, tn), F32)
        out_spec = pl.BlockSpec((None, tk, tn), lambda k, j, m: (j, k, 0))
    else:
        out_shape = jax.ShapeDtypeStruct((K, N), F32)
        out_spec = pl.BlockSpec((tk, tn), lambda k, j, m: (k, j))
    out = pl.pallas_call(
        body, name=name, grid=(K // tk, N // tn, M // tm),
        in_specs=[pl.BlockSpec((tm, tk), lambda k, j, m: (m, k)), pl.BlockSpec((tm, tn), lambda k, j, m: (m, j))],
        out_specs=out_spec, out_shape=out_shape,
        compiler_params=_params(("parallel", "parallel", "arbitrary")),
    )(a, b)
    return out if col_sharded else out.reshape(chips, K // chips, N)


def _stack_heads(x, scale=None):
    lane = lax.broadcasted_iota(jnp.int32, x.shape, 1)
    zero = jnp.zeros_like(x)
    lo = jnp.where(lane < HEAD_DIM, x, zero)
    hi = jnp.where(lane < HEAD_DIM, zero, x)
    out = jnp.concatenate([lo, hi], axis=0)
    return out if scale is None else out * scale


def _unstack_heads(x2):
    qb = x2.shape[0] // 2
    lane = lax.broadcasted_iota(jnp.int32, (qb, LANES), 1)
    return jnp.where(lane < HEAD_DIM, x2[:qb], x2[qb:])


ATTN_CHUNK = 32


def _in_lockstep(staged):
    waiting, live = list(staged), []
    while waiting or live:
        if waiting:
            live.append(waiting.pop(0))
        for gen in list(live):
            if next(gen, StopIteration) is StopIteration:
                live.remove(gen)


def _row_chunks():
    return [slice(r, r + ATTN_CHUNK) for r in range(0, 2 * QUERY_BLOCK, ATTN_CHUNK)]


def _chunk_valid(rows, older):
    shape = (rows.stop - rows.start, older + QUERY_BLOCK)
    r = (rows.start + lax.broadcasted_iota(jnp.int32, shape, 0)) & (QUERY_BLOCK - 1)
    return lax.broadcasted_iota(jnp.int32, shape, 1) < r + older


def _split_to(x, hi_ref, lo_ref, rows):
    xh = x.astype(BF16)
    hi_ref[rows, :] = xh
    lo_ref[rows, :] = (x - xh.astype(F32)).astype(BF16)


def _triangle(keys, strict):
    r = lax.broadcasted_iota(jnp.int32, (keys, keys), 0)
    c = lax.broadcasted_iota(jnp.int32, (keys, keys), 1)
    return jnp.where((r > c) if strict else (r >= c), 1.0, 0.0).astype(BF16)


def _fill_attn_consts(tri_ref, mask_ref):
    n = 2 * QUERY_BLOCK
    tri_ref[0] = _triangle(n, True)
    tri_ref[1] = _triangle(n, False)
    mask_ref[...] = jnp.where(_chunk_valid(slice(0, n), QUERY_BLOCK), 1.0, 0.0)


def _valid(mask_ref, rows, older):
    cols = slice(0, 2 * QUERY_BLOCK) if older else slice(QUERY_BLOCK, 2 * QUERY_BLOCK)
    return mask_ref[rows, cols] > 0.5


def _row_total(first):
    lane = lax.broadcasted_iota(jnp.int32, first.shape, 1)
    total = jnp.sum(jnp.where(lane == 0, first, 0.0), axis=1, keepdims=True)
    return jnp.broadcast_to(total, first.shape)


def _pairs_per_step(n_pairs, most):
    return max(g for g in (1, 2, 4) if g <= most and n_pairs % g == 0)


def _lanes(g):
    return slice(g * LANES, (g + 1) * LANES)


def _sweep_older(i, step, carry_ref, first):
    def cond(state):
        n, live = state
        return jnp.logical_and(n < i, live)

    def older(state):
        n, _ = state
        step(i - 1 - n, False)
        return n + 1, jnp.max(carry_ref[...]) > LOG_STICK_FLOOR

    lax.while_loop(cond, older, (first, jnp.max(carry_ref[...]) > LOG_STICK_FLOOR))


class _CommJob:
    def __init__(self, inputs, out_shapes, aliases, n_sems, start, finish):
        self.inputs, self.out_shapes, self.aliases, self.n_sems = list(inputs), list(out_shapes), aliases, n_sems
        self.start, self.finish = start, finish


def _merge_jobs(jobs):
    jobs = [j for j in jobs if j is not None]
    if len(jobs) <= 1:
        return jobs[0] if jobs else None
    spans, aliases = [], {}
    i0 = o0 = s0 = 0
    for j in jobs:
        spans.append((i0, o0, s0))
        aliases.update({i0 + a: o0 + b for a, b in j.aliases.items()})
        i0, o0, s0 = i0 + len(j.inputs), o0 + len(j.out_shapes), s0 + j.n_sems

    def run(which):
        def go(ins, outs, send_sems, recv_sems):
            for j, (i, o, s) in zip(jobs, spans):
                getattr(j, which)(ins[i:i + len(j.inputs)], outs[o:o + len(j.out_shapes)],
                                  send_sems.at[pl.ds(s, j.n_sems)], recv_sems.at[pl.ds(s, j.n_sems)])
        return go

    return _CommJob([a for j in jobs for a in j.inputs], [s for j in jobs for s in j.out_shapes], aliases, s0,
                    run("start"), run("finish"))


def _call_with_job(core_body, job, *, name, grid, in_specs, out_specs, out_shape, scratch_shapes, args):
    sem = ("arbitrary",) * len(grid)
    if job is None:
        res = pl.pallas_call(core_body, name=name, grid=grid, in_specs=in_specs, out_specs=out_specs,
                             out_shape=out_shape, scratch_shapes=scratch_shapes, compiler_params=_params(sem))(*args)
        return list(res), []
    n_in, n_out, n_scr = len(in_specs), len(out_specs), len(scratch_shapes)
    m_in, m_out = len(job.inputs), len(job.out_shapes)

    def body(*refs):
        at = 0
        parts = []
        for count in (n_in, m_in, n_out, m_out, n_scr, 2):
            parts.append(refs[at:at + count])
            at += count
        ins, job_in, outs, job_outs, scratch, (send_sems, recv_sems) = parts
        first = functools.reduce(jnp.logical_and, [pl.program_id(a) == 0 for a in range(len(grid))])
        last = functools.reduce(jnp.logical_and, [pl.program_id(a) == grid[a] - 1 for a in range(len(grid))])

        @pl.when(first)
        def _():
            job.start(job_in, job_outs, send_sems, recv_sems)

        core_body(*ins, *outs, *scratch)

        @pl.when(last)
        def _():
            job.finish(job_in, job_outs, send_sems, recv_sems)

    res = pl.pallas_call(
        body, name=name, grid=grid, in_specs=list(in_specs) + [ANY] * m_in, out_specs=list(out_specs) + [ANY] * m_out,
        out_shape=list(out_shape) + job.out_shapes,
        input_output_aliases={n_in + a: n_out + b for a, b in job.aliases.items()},
        scratch_shapes=list(scratch_shapes) + [pltpu.SemaphoreType.DMA((job.n_sems,)), pltpu.SemaphoreType.DMA((job.n_sems,))],
        compiler_params=_params(sem),
    )(*args, *job.inputs)
    return list(res[:n_out]), list(res[n_out:])


def _run_job(job, name):
    m_in, m_out = len(job.inputs), len(job.out_shapes)

    def body(*refs):
        job_in, job_outs = refs[:m_in], refs[m_in:m_in + m_out]
        send_sems, recv_sems = refs[m_in + m_out:]
        job.start(job_in, job_outs, send_sems, recv_sems)
        job.finish(job_in, job_outs, send_sems, recv_sems)

    return list(pl.pallas_call(
        body, name=name, in_specs=[ANY] * m_in, out_specs=[ANY] * m_out, out_shape=job.out_shapes,
        input_output_aliases=dict(job.aliases),
        scratch_shapes=[pltpu.SemaphoreType.DMA((job.n_sems,)), pltpu.SemaphoreType.DMA((job.n_sems,))],
    )(*job.inputs))


def _attn_fwd(qkv, sb_width, name, job=None):
    L = qkv.shape[0]
    QB = QUERY_BLOCK
    nb = L // QB
    n_pairs = sb_width // LANES
    G = _pairs_per_step(n_pairs, 4)
    W = G * LANES
    scale = 1.0 / math.sqrt(HEAD_DIM)

    def body(q_ref, k_ref, v_ref, o_ref, acc_ref, carry_ref, f32_ref, bf16_ref, tri_ref, mask_ref):
        i = pl.program_id(1)

        @pl.when(i == 0)
        def _():
            _fill_attn_consts(tri_ref, mask_ref)

        q2 = [_stack_heads(q_ref[:, _lanes(g)], scale) for g in range(G)]

        def step(g, j, older, masked):
            n = older + QB
            start = pl.multiple_of(j * QB, QB)
            kb = k_ref[pl.ds(start, n), _lanes(g)]
            vb = v_ref[pl.ds(start, n), _lanes(g)]
            zs, as_, bs = (f32_ref.at[g, t, :, :n] for t in range(3))
            hi, lo = (bf16_ref.at[g, t, :, :n] for t in range(2))
            zs[...] = lax.dot_general(q2[g], kb, (((1,), (1,)), ((), ())), preferred_element_type=F32)
            yield
            for rows in _row_chunks():
                z = zs[rows, :]
                sp = jnp.log(1.0 + jnp.exp(-jnp.abs(z)))
                b = jnp.minimum(-z, 0.0) - sp
                if masked:
                    b = jnp.where(_valid(mask_ref, rows, older),b, 0.0)
                as_[rows, :] = jnp.minimum(z, 0.0) - sp
                bs[rows, :] = b
                _split_to(b, hi, lo, rows)
            yield
            tri = tri_ref[0, :n, :n]
            zs[...] = (jnp.dot(hi[...], tri, preferred_element_type=F32)
                       + jnp.dot(lo[...], tri, preferred_element_type=F32))
            yield
            for rows in _row_chunks():
                excl = zs[rows, :]
                total = _row_total(excl[:, :LANES] + bs[rows, :LANES])
                if masked:
                    carry_ref[g, rows, :] = total
                else:
                    excl = excl + jnp.tile(carry_ref[g, rows, :], (1, n // LANES))
                    carry_ref[g, rows, :] += total
                w = jnp.exp(as_[rows, :] + excl)
                if masked:
                    w = jnp.where(_valid(mask_ref, rows, older),w, 0.0)
                _split_to(w, hi, lo, rows)
            yield
            out = jnp.dot(hi[...], vb, preferred_element_type=F32) + jnp.dot(lo[...], vb, preferred_element_type=F32)
            acc_ref[g] = out if masked else acc_ref[g] + out

        @pl.when(i == 0)
        def _():
            _in_lockstep([step(g, 0, 0, True) for g in range(G)])

        @pl.when(i > 0)
        def _():
            _in_lockstep([step(g, i - 1, QB, True) for g in range(G)])
            for g in range(G):
                _sweep_older(i, lambda j, _, g=g: _in_lockstep([step(g, j, 0, False)]), carry_ref.at[g], 1)

        for g in range(G):
            o_ref[:, _lanes(g)] = _unstack_heads(acc_ref[g])

    n_steps = n_pairs // G
    (out,), job_out = _call_with_job(
        body, job, name=name, grid=(n_steps, nb),
        in_specs=[pl.BlockSpec((QB, W), lambda p, i: (i, p)),
                  pl.BlockSpec((L, W), lambda p, i: (0, n_steps + p)),
                  pl.BlockSpec((L, W), lambda p, i: (0, 2 * n_steps + p))],
        out_specs=[pl.BlockSpec((QB, W), lambda p, i: (i, p))],
        out_shape=[jax.ShapeDtypeStruct((L, sb_width), F32)],
        scratch_shapes=[pltpu.VMEM((G, 2 * QB, LANES), F32), pltpu.VMEM((G, 2 * QB, LANES), F32),
                        pltpu.VMEM((G, 3, 2 * QB, 2 * QB), F32), pltpu.VMEM((G, 2, 2 * QB, 2 * QB), BF16),
                        pltpu.VMEM((2, 2 * QB, 2 * QB), BF16), pltpu.VMEM((2 * QB, 2 * QB), F32)],
        args=(qkv, qkv, qkv))
    return out, job_out


def _attn_bwd(qkv, o, dmixed, sb_width, name, job=None):
    L = qkv.shape[0]
    QB = QUERY_BLOCK
    nb = L // QB
    n_pairs = sb_width // LANES
    G = _pairs_per_step(n_pairs, 2)
    W = G * LANES
    scale = 1.0 / math.sqrt(HEAD_DIM)

    def body(q_ref, k_ref, v_ref, o_ref, do_ref, dq_ref, dk_ref, dv_ref,
             dq_acc, dk_acc, dv_acc, ce_ref, cr_ref, dtot_ref, f32_ref, bf16_ref, tri_ref, mask_ref):
        i = pl.program_id(1)

        @pl.when(i == 0)
        def _():
            dk_acc[...] = jnp.zeros_like(dk_acc)
            dv_acc[...] = jnp.zeros_like(dv_acc)
            _fill_attn_consts(tri_ref, mask_ref)

        q2s = [_stack_heads(q_ref[:, _lanes(g)], scale) for g in range(G)]
        do2s = [_stack_heads(do_ref[:, _lanes(g)].astype(BF16)) for g in range(G)]
        for g in range(G):
            ov = o_ref[:, _lanes(g)]
            per_row = jnp.sum(do2s[g].astype(F32) * jnp.concatenate([ov, ov], axis=0), axis=1, keepdims=True)
            dtot_ref[g] = jnp.broadcast_to(per_row, (2 * QB, LANES))

        def step(g, j, older, masked):
            n = older + QB
            wide = n // LANES
            q2, do2 = q2s[g], do2s[g]
            start = pl.multiple_of(j * QB, QB)
            kb = k_ref[pl.ds(start, n), _lanes(g)]
            vb = v_ref[pl.ds(start, n), _lanes(g)]
            zs, as_, bs, betas, gs = (f32_ref.at[g, t, :, :n] for t in range(5))
            hi, lo, wb = (bf16_ref.at[g, t, :, :n] for t in range(3))
            nt = (((1,), (1,)), ((), ()))
            zs[...] = lax.dot_general(q2, kb, nt, preferred_element_type=F32)
            gs[...] = lax.dot_general(do2, vb, nt, preferred_element_type=F32)
            yield
            for rows in _row_chunks():
                z = zs[rows, :]
                e = jnp.exp(-jnp.abs(z))
                sp = jnp.log(1.0 + e)
                b = jnp.minimum(-z, 0.0) - sp
                if masked:
                    b = jnp.where(_valid(mask_ref, rows, older),b, 0.0)
                rinv = 1.0 / (1.0 + e)
                as_[rows, :] = jnp.minimum(z, 0.0) - sp
                bs[rows, :] = b
                betas[rows, :] = jnp.where(z >= 0.0, rinv, e * rinv)
                _split_to(b, hi, lo, rows)
            yield
            tri = tri_ref[0, :n, :n]
            zs[...] = (jnp.dot(hi[...], tri, preferred_element_type=F32)
                       + jnp.dot(lo[...], tri, preferred_element_type=F32))
            yield
            for rows in _row_chunks():
                excl = zs[rows, :]
                total = _row_total(excl[:, :LANES] + bs[rows, :LANES])
                if masked:
                    ce_ref[g, rows, :] = total
                else:
                    excl = excl + jnp.tile(ce_ref[g, rows, :], (1, wide))
                    ce_ref[g, rows, :] += total
                w = jnp.exp(as_[rows, :] + excl)
                if masked:
                    w = jnp.where(_valid(mask_ref, rows, older),w, 0.0)
                wb[rows, :] = w.astype(BF16)
                gw = w * gs[rows, :]
                gs[rows, :] = gw
                _split_to(gw, hi, lo, rows)
            yield
            tri = tri_ref[1, :n, :n]
            zs[...] = (jnp.dot(hi[...], tri, preferred_element_type=F32)
                       + jnp.dot(lo[...], tri, preferred_element_type=F32))
            yield
            for rows in _row_chunks():
                rinc = zs[rows, :]
                total = _row_total(rinc[:, :LANES])
                if masked:
                    cr_ref[g, rows, :] = total
                else:
                    rinc = rinc + jnp.tile(cr_ref[g, rows, :], (1, wide))
                    cr_ref[g, rows, :] += total
                beta = betas[rows, :]
                dz = gs[rows, :] * (1.0 - beta) - beta * (jnp.tile(dtot_ref[g, rows, :], (1, wide)) - rinc)
                if masked:
                    dz = jnp.where(_valid(mask_ref, rows, older),dz, 0.0)
                hi[rows, :] = dz.astype(BF16)
            yield
            dzb = hi[...]
            dq = jnp.dot(dzb, kb, preferred_element_type=F32)
            dq_acc[g] = dq if masked else dq_acc[g] + dq
            dk_acc[pl.ds(start, n), _lanes(g)] += lax.dot_general(
                dzb, q2, (((0,), (0,)), ((), ())), preferred_element_type=F32)
            dv_acc[pl.ds(start, n), _lanes(g)] += lax.dot_general(
                wb[...], do2, (((0,), (0,)), ((), ())), preferred_element_type=F32)

        @pl.when(i == 0)
        def _():
            _in_lockstep([step(g, 0, 0, True) for g in range(G)])

        @pl.when(i > 0)
        def _():
            _in_lockstep([step(g, i - 1, QB, True) for g in range(G)])
            for g in range(G):
                _sweep_older(i, lambda j, _, g=g: _in_lockstep([step(g, j, 0, False)]), ce_ref.at[g], 1)

        for g in range(G):
            dq_ref[:, _lanes(g)] = (_unstack_heads(dq_acc[g]) * scale).astype(dq_ref.dtype)

        @pl.when(i == nb - 1)
        def _():
            dk_ref[...] = dk_acc[...].astype(dk_ref.dtype)
            dv_ref[...] = dv_acc[...].astype(dv_ref.dtype)

    n_steps = n_pairs // G
    blk = pl.BlockSpec((QB, W), lambda p, i: (i, p))
    col = pl.BlockSpec((L, W), lambda p, i: (0, p))
    return _call_with_job(
        body, job, name=name, grid=(n_steps, nb),
        in_specs=[blk,
                  pl.BlockSpec((L, W), lambda p, i: (0, n_steps + p)),
                  pl.BlockSpec((L, W), lambda p, i: (0, 2 * n_steps + p)),
                  blk, blk],
        out_specs=[blk, col, col],
        out_shape=[jax.ShapeDtypeStruct((L, sb_width), BF16)] * 3,
        scratch_shapes=[pltpu.VMEM((G, 2 * QB, LANES), F32), pltpu.VMEM((L, W), F32),
                        pltpu.VMEM((L, W), F32), pltpu.VMEM((G, 2 * QB, LANES), F32),
                        pltpu.VMEM((G, 2 * QB, LANES), F32), pltpu.VMEM((G, 2 * QB, LANES), F32),
                        pltpu.VMEM((G, 5, 2 * QB, 2 * QB), F32), pltpu.VMEM((G, 3, 2 * QB, 2 * QB), BF16),
                        pltpu.VMEM((2, 2 * QB, 2 * QB), BF16), pltpu.VMEM((2 * QB, 2 * QB), F32)],
        args=(qkv, qkv, qkv, o, dmixed))


def _conv_tile(L):
    return _pick(L, (384, 128))


def _glu(x, C):
    return x[:, :C] * _sigmoid(x[:, C:])


CONV_CHUNK = 32
SHIFT_TAIL = 24


def _fill_shifted(src_ref, dst_ref):
    n = dst_ref.shape[1]
    for r in range(1, 8):
        dst_ref[r - 1] = src_ref[r:r + n, :]


def _rows_at(src_ref, shifted_ref, start, n):
    q, r = divmod(start, 8)
    if r == 0:
        return src_ref[start:start + n, :]
    return shifted_ref[r - 1, 8 * q:8 * q + n, :]


def _conv_fwd(cacg, dw_w, dw_b, ln_g, ln_b, name, job=None):
    L, C2 = cacg.shape
    C = C2 // 2
    T = _conv_tile(L)
    H = CONV_HALO
    K = dw_w.shape[0]
    CH = CONV_CHUNK

    def body(x_ref, prev_ref, w_ref, b_ref, g_ref, beta_ref, o_ref, y_ref, u_ref, us_ref):
        i = pl.program_id(0)
        u_ref[0:H, :] = jnp.where(i > 0, _glu(prev_ref[...], C), 0.0)
        u_ref[H:, :] = _glu(x_ref[...], C)
        _fill_shifted(u_ref, us_ref)
        for c0 in range(0, T, CH):
            y = jnp.broadcast_to(b_ref[...], (CH, C))
            for k in range(K):
                y = y + w_ref[k:k + 1, :] * _rows_at(u_ref, us_ref, c0 + H - (K - 1) + k, CH)
            y_ref[c0:c0 + CH, :] = y
            mu = jnp.mean(y, axis=-1, keepdims=True)
            yc = y - mu
            rstd = lax.rsqrt(jnp.mean(yc * yc, axis=-1, keepdims=True) + EPS)
            ln = yc * rstd * g_ref[...] + beta_ref[...]
            o_ref[c0:c0 + CH, :] = (ln * _sigmoid(ln)).astype(o_ref.dtype)

    vec = pl.BlockSpec((1, C), lambda i: (0, 0))
    tile = pl.BlockSpec((T, C), lambda i: (i, 0))
    (out, y), arrived = _call_with_job(
        body, job, name=name, grid=(L // T,),
        in_specs=[pl.BlockSpec((T, C2), lambda i: (i, 0)),
                  pl.BlockSpec((H, C2), lambda i: (jnp.maximum(i * (T // H) - 1, 0), 0)),
                  pl.BlockSpec((K, C), lambda i: (0, 0)), vec, vec, vec],
        out_specs=[tile, tile],
        out_shape=[jax.ShapeDtypeStruct((L, C), BF16), jax.ShapeDtypeStruct((L, C), F32)],
        scratch_shapes=[pltpu.VMEM((T + H, C), F32), pltpu.VMEM((7, T + SHIFT_TAIL, C), F32)],
        args=(cacg, cacg, dw_w, dw_b, ln_g, ln_b))
    return out, y, arrived


def _conv_bwd(cacg, y, dmixed, dw_w, ln_g, ln_b, name, job=None):
    L, C2 = cacg.shape
    C = C2 // 2
    T = _conv_tile(L)
    H = CONV_HALO
    K = dw_w.shape[0]
    nt = L // T
    TE = T + H

    CH = CONV_CHUNK

    def body(x_ref, prev_ref, y_ref, ynext_ref, d_ref, dnext_ref, w_ref, g_ref, beta_ref,
             dca_ref, dcg_ref, dwt_ref, db_ref, dg_ref, dbeta_ref, u_ref, us_ref, dy_ref, dys_ref):
        i = pl.program_id(0)
        last = i == nt - 1

        @pl.when(i == 0)
        def _():
            dwt_ref[...] = jnp.zeros_like(dwt_ref)
            db_ref[...] = jnp.zeros_like(db_ref)
            dg_ref[...] = jnp.zeros_like(dg_ref)
            dbeta_ref[...] = jnp.zeros_like(dbeta_ref)

        u_ref[0:H, :] = jnp.where(i > 0, _glu(prev_ref[...], C), 0.0)
        u_ref[H:, :] = _glu(x_ref[...], C)
        _fill_shifted(u_ref, us_ref)
        dg_acc = jnp.zeros((1, C), F32)
        dbeta_acc = jnp.zeros((1, C), F32)
        db_acc = jnp.zeros((1, C), F32)
        for c0 in range(0, TE, CH):
            y = y_ref[c0:c0 + CH, :] if c0 < T else ynext_ref[c0 - T:c0 - T + CH, :]
            mu = jnp.mean(y, axis=-1, keepdims=True)
            yc = y - mu
            rstd = lax.rsqrt(jnp.mean(yc * yc, axis=-1, keepdims=True) + EPS)
            yh = yc * rstd
            ln = yh * g_ref[...] + beta_ref[...]
            s = _sigmoid(ln)
            dout = d_ref[c0:c0 + CH, :] if c0 < T else jnp.where(last, 0.0, dnext_ref[c0 - T:c0 - T + CH, :])
            dln = dout * (s * (1.0 + ln * (1.0 - s)))
            dyh = dln * g_ref[...]
            dy = rstd * (dyh - jnp.mean(dyh, axis=-1, keepdims=True)
                         - yh * jnp.mean(dyh * yh, axis=-1, keepdims=True))
            dy_ref[c0:c0 + CH, :] = dy
            if c0 < T:
                dg_acc = dg_acc + jnp.sum(dln * yh, axis=0, keepdims=True)
                dbeta_acc = dbeta_acc + jnp.sum(dln, axis=0, keepdims=True)
                db_acc = db_acc + jnp.sum(dy, axis=0, keepdims=True)
        dg_ref[...] += dg_acc
        dbeta_ref[...] += dbeta_acc
        db_ref[...] += db_acc
        _fill_shifted(dy_ref, dys_ref)
        for k in range(K):
            dwt_ref[k:k + 1, :] += jnp.sum(
                dy_ref[0:T, :] * _rows_at(u_ref, us_ref, H - (K - 1) + k, T), axis=0, keepdims=True)
        for c0 in range(0, T, CH):
            du = jnp.zeros((CH, C), F32)
            for k in range(K):
                du = du + w_ref[k:k + 1, :] * _rows_at(dy_ref, dys_ref, c0 + (K - 1) - k, CH)
            x = x_ref[c0:c0 + CH, :]
            sg = _sigmoid(x[:, C:])
            dca_ref[c0:c0 + CH, :] = (du * sg).astype(dca_ref.dtype)
            dcg_ref[c0:c0 + CH, :] = (du * x[:, :C] * sg * (1.0 - sg)).astype(dcg_ref.dtype)

    nh = L // H
    vec = pl.BlockSpec((1, C), lambda i: (0, 0))
    row = pl.BlockSpec((T, C), lambda i: (i, 0))
    after = lambda i: jnp.minimum((i + 1) * (T // H), nh - 1)
    return _call_with_job(
        body, job, name=name, grid=(nt,),
        in_specs=[pl.BlockSpec((T, C2), lambda i: (i, 0)),
                  pl.BlockSpec((H, C2), lambda i: (jnp.maximum(i * (T // H) - 1, 0), 0)),
                  row, pl.BlockSpec((H, C), lambda i: (after(i), 0)),
                  pl.BlockSpec((T, C), lambda i: (i, 1)), pl.BlockSpec((H, C), lambda i: (after(i), 1)),
                  pl.BlockSpec((K, C), lambda i: (0, 0)), vec, vec],
        out_specs=[row, row, pl.BlockSpec((H, C), lambda i: (0, 0)), vec, vec, vec],
        out_shape=[jax.ShapeDtypeStruct((L, C), BF16), jax.ShapeDtypeStruct((L, C), BF16),
                   jax.ShapeDtypeStruct((H, C), F32), jax.ShapeDtypeStruct((1, C), F32),
                   jax.ShapeDtypeStruct((1, C), F32), jax.ShapeDtypeStruct((1, C), F32)],
        scratch_shapes=[pltpu.VMEM((T + H, C), F32), pltpu.VMEM((7, T + SHIFT_TAIL, C), F32),
                        pltpu.VMEM((TE, C), F32), pltpu.VMEM((7, T + SHIFT_TAIL, C), F32)],
        args=(cacg, cacg, y, y, dmixed, dmixed, dw_w, ln_g, ln_b))


def _local_step(h0, target, n_meta, seq, norms, conv_p, wts, final_g, gather_behind, reducer):
    mix_g, ffn_g = norms
    dw_w, dw_b, ln_g, ln_b = conv_p
    depth = mix_g.shape[0]
    C = dw_b.shape[-1]
    sbw = (wts["w_in"][0].shape[-1] - 2 * C) // 3
    assert sbw == C, "the mixer halves must have equal width"
    row = lambda a, i: a[i][None, :]

    h = h0
    saved = []
    for i in range(depth):
        hn, proj_qkv, cacg = _norm_in_proj(h, row(mix_g, i), wts["w_in"][i], 3 * sbw, f"in_proj_{i}")
        def hosting(kind):
            job, keys = gather_behind.get((kind, i), (None, ()))

            def sink(arrived):
                for (wname, wl), arr in zip(keys, arrived):
                    wts[wname][wl] = arr

            return job, sink

        job, sink = hosting("attn")
        attn, arrived = _attn_fwd(proj_qkv, sbw, f"attn_fwd_{i}", job)
        sink(arrived)
        job, sink = hosting("conv")
        conv, conv_y, arrived = _conv_fwd(cacg, dw_w[i], row(dw_b, i), row(ln_g, i), row(ln_b, i),
                                          f"conv_fwd_{i}", job)
        sink(arrived)
        h_mid = _mm_rows([(attn, wts["w_out"][i], 0), (conv, wts["w_out"][i], 1)], f"out_proj_{i}", residual=h)
        job, sink = hosting("ffn")
        (hn2, g, u, act), arrived = _ffn_up(h_mid, row(ffn_g, i), wts["w_gate_t"][i], wts["w_up_t"][i],
                                            f"ffn_up_{i}", job)
        sink(arrived)
        job, sink = hosting("down")
        h_out = _mm_rows([(act, wts["w_down"][i], 0)], f"down_{i}", residual=h_mid, job=job)
        if job is not None:
            h_out, arrived = h_out
            sink(arrived)
        saved.append((h, hn, proj_qkv, cacg, attn, conv, conv_y, h_mid, hn2, g, u, act))
        h = h_out

    loss, dh, d_final_g = _loss_head(h, final_g[None, :], target, n_meta, seq, "loss_head")

    grads = {k: [None] * depth for k in ("mix_g", "ffn_g", "dw_w", "dw_b", "ln_g", "ln_b")}
    for i in reversed(range(depth)):
        h_in, hn, proj_qkv, cacg, attn, conv, conv_y, h_mid, hn2, g, u, act = saved[i]
        big = {}
        dg, du = _ffn_down_bwd(dh, wts["w_down"][i], g, u, f"ffn_down_bwd_{i}")
        big["w_down"] = _mm_tn(act, dh, f"dw_down_{i}", col_sharded=False)
        big["w_gate_t"] = _mm_tn(dg, hn2, f"dw_gate_{i}", col_sharded=False)
        big["w_up_t"] = _mm_tn(du, hn2, f"dw_up_{i}", col_sharded=False)
        sib_job, sib_sink = reducer.to_sibling(i, big)
        dh, d_ffn, arrived = _mm_rows([(dg, wts["w_gate_t"][i], 0), (du, wts["w_up_t"][i], 0)], f"d_hn2_{i}",
                                      norm_bwd=(h_mid, row(ffn_g, i), dh), job=sib_job)
        sib_sink(arrived)
        dmixed = _mm_rows([(dh, wts["w_out"][i], 0)], f"d_mixed_{i}", transposed=True)
        dw_out = jnp.concatenate([_mm_tn(attn, dh, f"dw_out_attn_{i}", col_sharded=False, chips=2),
                                  _mm_tn(conv, dh, f"dw_out_conv_{i}", col_sharded=False, chips=2)], axis=0)
        sib_job, sib_sink = reducer.to_sibling(i, {"w_out": dw_out})
        x_job, x_sink = reducer.take()
        (dq, dk, dv), arrived = _attn_bwd(proj_qkv, attn, dmixed, sbw, f"attn_bwd_{i}",
                                          _merge_jobs([sib_job, x_job]))
        sib_sink(arrived[:len(sib_job.out_shapes)])
        x_sink(arrived[len(sib_job.out_shapes):])
        x_job, x_sink = reducer.take()
        (dca, dcg, d_dw, d_b, d_lg, d_lb), arrived = _conv_bwd(
            cacg, conv_y, dmixed, dw_w[i], row(ln_g, i), row(ln_b, i), f"conv_bwd_{i}", x_job)
        x_sink(arrived)
        grads["dw_w"][i] = d_dw
        grads["dw_b"][i], grads["ln_g"][i], grads["ln_b"][i] = d_b, d_lg, d_lb
        dproj = jnp.concatenate([dq, dk, dv, dca, dcg], axis=1)
        reducer.ready(i, {"w_in": _mm_tn(hn, dproj, f"dw_in_{i}", col_sharded=True)})
        x_job, x_sink = reducer.take() if i == 0 else (None, None)
        res = _mm_rows([(dproj, wts["w_in"][i], 0)], f"d_hn_{i}", transposed=True,
                       norm_bwd=(h_in, row(mix_g, i), dh), job=x_job)
        dh, d_mix = res[:2]
        if x_job is not None:
            x_sink(res[2])
        grads["mix_g"][i], grads["ffn_g"][i] = d_mix, d_ffn
    grads["final_g"] = d_final_g
    return loss, dh, grads


ANY = pl.BlockSpec(memory_space=pl.ANY)


def _position():
    return lax.axis_index("x"), lax.axis_index("y"), lax.axis_index("c")


def _chip_at(x, y, k):
    return (1 - x if k & 2 else x), (1 - y if k & 1 else y)


def _half_rows(ref, half, rows, base=0):
    start = pl.multiple_of(base + half * rows, 8)
    lead = (slice(None),) * (len(ref.shape) - 2)
    return ref.at[(*lead, pl.ds(start, rows), slice(None))]


def _gather_job(fulls, shard_shapes, col_sharded):
    n = len(fulls)

    def tools(f_refs, send_sems, recv_sems):
        def block(wi, chip, half):
            _, R, C = shard_shapes[wi]
            if col_sharded[wi]:
                cols = pl.ds(pl.multiple_of(chip * C, LANES), C)
                return f_refs[wi].at[:, pl.ds(pl.multiple_of(half * (R // 2), 8), R // 2), cols]
            return _half_rows(f_refs[wi], half, R // 2, base=chip * R)

        def copy(wi, slot, blk, to):
            return pltpu.make_async_remote_copy(
                src_ref=blk, dst_ref=blk, send_sem=send_sems.at[6 * wi + slot],
                recv_sem=recv_sems.at[6 * wi + slot], device_id=to, device_id_type=MESH)

        return block, copy

    def start(_, f_refs, send_sems, recv_sems):
        block, copy = tools(f_refs, send_sems, recv_sems)
        x, y, c = _position()
        me = 2 * x + y
        for wi in range(n):
            for k in (1, 2, 3):
                copy(wi, k - 1, block(wi, me, c), (*_chip_at(x, y, k), c)).start()

    def finish(_, f_refs, send_sems, recv_sems):
        block, copy = tools(f_refs, send_sems, recv_sems)
        x, y, c = _position()
        me = 2 * x + y
        for wi in range(n):
            for k in (1, 2, 3):
                landed = block(wi, me ^ k, c)
                copy(wi, k - 1, landed, (x, y, c)).wait_recv()
                copy(wi, 2 + k, landed, (x, y, 1 - c)).start()
        for wi in range(n):
            for k in (1, 2, 3):
                copy(wi, 2 + k, block(wi, me ^ k, 1 - c), (x, y, c)).wait_recv()
        for wi in range(n):
            for k in (1, 2, 3):
                copy(wi, k - 1, block(wi, me, c), (x, y, c)).wait_send()
                copy(wi, 2 + k, block(wi, me ^ k, c), (x, y, c)).wait_send()

    return _CommJob(fulls, [jax.ShapeDtypeStruct(f.shape, f.dtype) for f in fulls], {i: i for i in range(n)},
                    6 * n, start, finish)


def _place_shard(w, layer, chip, col_sharded, dtype, name):
    _, R, C = w.shape
    tr = _pick(R, (512, 704, 256, 128, 48))
    nr = R // tr

    def body(chip_ref, w_ref, o_ref):
        o_ref[...] = w_ref[...].astype(dtype)

    if col_sharded:
        shape = (1, R, N_CHIPS * C)
        out_spec = pl.BlockSpec((None, tr, C), lambda r, chip_ref: (0, r, chip_ref[0]))
    else:
        shape = (1, N_CHIPS * R, C)
        out_spec = pl.BlockSpec((None, tr, C), lambda r, chip_ref: (0, chip_ref[0] * nr + r, 0))
    grid_spec = pltpu.PrefetchScalarGridSpec(
        num_scalar_prefetch=1, grid=(nr,),
        in_specs=[pl.BlockSpec((None, tr, C), lambda r, chip_ref: (layer, r, 0))], out_specs=out_spec)
    return pl.pallas_call(
        body, name=name, grid_spec=grid_spec, out_shape=jax.ShapeDtypeStruct(shape, dtype),
        compiler_params=_params(("parallel",)),
    )(chip, w)


def _sibling_job(grads):
    n = len(grads)

    def copies(g_refs, l_refs, send_sems, recv_sems):
        x, y, c = _position()
        return [pltpu.make_async_remote_copy(
            src_ref=_half_rows(g_refs[wi], 1 - c, grads[wi].shape[1] // 2), dst_ref=l_refs[wi],
            send_sem=send_sems.at[wi], recv_sem=recv_sems.at[wi],
            device_id=(x, y, 1 - c), device_id_type=MESH) for wi in range(n)]

    def start(*refs):
        for cp in copies(*refs):
            cp.start()

    def finish(*refs):
        for cp in copies(*refs):
            cp.wait()

    outs = [jax.ShapeDtypeStruct((g.shape[0], g.shape[1] // 2, g.shape[2]), g.dtype) for g in grads]
    return _CommJob(grads, outs, {}, n, start, finish)


def _chip_sum(g, landed, core, name):
    _, R, C = g.shape
    hr = R // 2
    tr = _pick(hr, (512, 352, 256, 128))
    nr = hr // tr

    def body(c_ref, g_ref, l_ref, o_ref):
        o_ref[...] = (g_ref[...] + l_ref[...]).astype(BF16)

    grid_spec = pltpu.PrefetchScalarGridSpec(
        num_scalar_prefetch=1, grid=(N_CHIPS, nr),
        in_specs=[pl.BlockSpec((None, tr, C), lambda j, r, c_ref: (j, c_ref[0] * nr + r, 0)),
                  pl.BlockSpec((None, tr, C), lambda j, r, c_ref: (j, r, 0))],
        out_specs=pl.BlockSpec((None, tr, C), lambda j, r, c_ref: (j, r, 0)))
    return pl.pallas_call(
        body, name=name, grid_spec=grid_spec, out_shape=jax.ShapeDtypeStruct((N_CHIPS, hr, C), BF16),
        compiler_params=_params(("parallel", "parallel")),
    )(core, g, landed)


def _across_job(parts):
    n = len(parts)

    def copy(p_refs, l_refs, send_sems, recv_sems, wi, k, to):
        x, y, _ = _position()
        me = 2 * x + y
        return pltpu.make_async_remote_copy(
            src_ref=p_refs[wi].at[me ^ k], dst_ref=l_refs[wi].at[me],
            send_sem=send_sems.at[3 * wi + k - 1], recv_sem=recv_sems.at[3 * wi + k - 1],
            device_id=to, device_id_type=MESH)

    def start(p_refs, l_refs, send_sems, recv_sems):
        x, y, c = _position()
        for wi in range(n):
            for k in (1, 2, 3):
                copy(p_refs, l_refs, send_sems, recv_sems, wi, k, (*_chip_at(x, y, k), c)).start()

    def finish(p_refs, l_refs, send_sems, recv_sems):
        x, y, c = _position()
        me = 2 * x + y
        for wi in range(n):
            for k in (1, 2, 3):
                slot = l_refs[wi].at[me ^ k]
                pltpu.make_async_remote_copy(
                    src_ref=slot, dst_ref=slot, send_sem=send_sems.at[3 * wi + k - 1],
                    recv_sem=recv_sems.at[3 * wi + k - 1], device_id=(x, y, c), device_id_type=MESH).wait_recv()
        for wi in range(n):
            for k in (1, 2, 3):
                copy(p_refs, l_refs, send_sems, recv_sems, wi, k, (x, y, c)).wait_send()

    return _CommJob(parts, [jax.ShapeDtypeStruct(p.shape, p.dtype) for p in parts], {}, 3 * n, start, finish)


class _Reducer:
    def __init__(self, core):
        self.core, self.parts, self.across, self.pending = core, {}, {}, []

    def to_sibling(self, layer, big):
        names = list(big)
        flat = [big[k] for k in names]

        def sink(landed):
            for k, g, la in zip(names, flat, landed):
                self.parts[k, layer] = _chip_sum(g, la, self.core, f"chip_sum_{k}_{layer}")
                self.pending.append((k, layer))

        return _sibling_job(flat), sink

    def ready(self, layer, big):
        job, sink = self.to_sibling(layer, big)
        sink(_run_job(job, f"grads_to_sibling_{next(iter(big))}_{layer}"))

    def take(self):
        keys, self.pending = self.pending, []
        if not keys:
            return None, lambda results: None

        def sink(results):
            self.across.update(zip(keys, results))

        return _across_job([self.parts[key] for key in keys]), sink


def _sum_chips(parts, landed, where, layer, depth, prev, name):
    _, hr, C = landed.shape
    tr = _pick(hr, (512, 352, 256, 128))
    nr = hr // tr

    def body(*refs):
        own_ref, slots, o_ref = refs[1], refs[2:2 + N_CHIPS], refs[-1]
        chip = refs[0][0]
        total = None
        for q in range(N_CHIPS):
            term = jnp.where(chip == q, own_ref[...], slots[q][...]).astype(F32)
            total = term if total is None else total + term
        o_ref[...] = total

    def slot_spec(q):
        return pl.BlockSpec((None, tr, C), lambda r, w: (jnp.where(w[0] == q, (q + 1) % N_CHIPS, q), r, 0))

    in_specs = [pl.BlockSpec((None, tr, C), lambda r, w: (w[0], r, 0))] + [slot_spec(q) for q in range(N_CHIPS)]
    args = [where, parts] + [landed] * N_CHIPS
    aliases = {}
    if prev is not None:
        in_specs.append(ANY)
        args.append(prev)
        aliases = {len(args) - 1: 0}
    grid_spec = pltpu.PrefetchScalarGridSpec(
        num_scalar_prefetch=1, grid=(nr,), in_specs=in_specs,
        out_specs=pl.BlockSpec((None, tr, C), lambda r, w: (layer, w[1] * nr + r, 0)))
    return pl.pallas_call(
        body, name=name, grid_spec=grid_spec, out_shape=jax.ShapeDtypeStruct((depth, 2 * hr, C), F32),
        input_output_aliases=aliases, compiler_params=_params(("parallel",)),
    )(*args)


def _rs_join_halves(reduced):
    n = len(reduced)

    def body(*refs):
        o_refs = refs[n:2 * n]
        send_sems, recv_sems = refs[2 * n:]
        x, y, c = _position()
        sent = []
        for wi in range(n):
            hr = reduced[wi].shape[1] // 2
            mine = _half_rows(o_refs[wi], c, hr)
            cp = pltpu.make_async_remote_copy(
                src_ref=mine, dst_ref=mine, send_sem=send_sems.at[wi], recv_sem=recv_sems.at[wi],
                device_id=(x, y, 1 - c), device_id_type=MESH)
            cp.start()
            sent.append(cp)
        for wi in range(n):
            hr = reduced[wi].shape[1] // 2
            theirs = _half_rows(o_refs[wi], 1 - c, hr)
            pltpu.make_async_remote_copy(
                src_ref=theirs, dst_ref=theirs, send_sem=send_sems.at[wi], recv_sem=recv_sems.at[wi],
                device_id=(x, y, c), device_id_type=MESH).wait_recv()
        for cp in sent:
            cp.wait_send()

    return pl.pallas_call(
        body, name="grads_join_halves", out_shape=[jax.ShapeDtypeStruct(r.shape, r.dtype) for r in reduced],
        in_specs=[ANY] * n, out_specs=[ANY] * n, input_output_aliases={i: i for i in range(n)},
        scratch_shapes=[pltpu.SemaphoreType.DMA((n,)), pltpu.SemaphoreType.DMA((n,))],
    )(*reduced)


def _adam_math(w, g, m, v):
    m = ADAM_B1 * m + (1.0 - ADAM_B1) * g
    v = ADAM_B2 * v + (1.0 - ADAM_B2) * jnp.square(g)
    m_hat = m / (1.0 - ADAM_B1 ** ADAM_STEP)
    v_hat = v / (1.0 - ADAM_B2 ** ADAM_STEP)
    delta = -ADAM_LR * (m_hat / (jnp.sqrt(v_hat) + ADAM_EPS) + ADAM_WD * w)
    return delta, m, v


def _small_reduce_adam(vec_grads, dww_grads, final_grad, meta_grad, state):
    vec_names = list(vec_grads)
    depth = len(dww_grads)
    D = final_grad.shape[1]
    n_meta = meta_grad.shape[0]
    taps_pad, C = dww_grads[0].shape
    names = vec_names + ["final", "meta", "dww"]
    at, row0 = 0, {}
    for k in vec_names:
        row0[k] = at
        at += depth
    row0["final"] = at
    at = -(-(at + 1) // 8) * 8
    row0["meta"] = at
    at += -(-n_meta // 8) * 8
    row0["dww"] = at
    rows = at + depth * taps_pad
    lanes = max(D, C)
    n_g = len(vec_names) * depth + depth + 2

    def body(*refs):
        g_refs = refs[:n_g]
        st = refs[n_g:n_g + 3 * len(names)]
        outs = refs[n_g + 3 * len(names):n_g + 7 * len(names)]
        slab, land, send_sems, recv_sems = refs[n_g + 7 * len(names):]
        x, y, c = _position()
        me = 4 * x + 2 * y + c
        chip = 2 * x + y
        slab[...] = jnp.zeros_like(slab)
        it = iter(g_refs)
        for k in vec_names:
            for l in range(depth):
                g_ref = next(it)
                slab[row0[k] + l:row0[k] + l + 1, 0:g_ref.shape[1]] = g_ref[...]
        for l in range(depth):
            slab[row0["dww"] + l * taps_pad:row0["dww"] + (l + 1) * taps_pad, 0:C] = next(it)[...]
        slab[row0["final"]:row0["final"] + 1, 0:D] = next(it)[...]
        slab[row0["meta"]:row0["meta"] + n_meta, 0:D] = next(it)[...]
        sent = []
        for k in range(1, N_DEV):
            to = (1 - x if k & 4 else x, 1 - y if k & 2 else y, 1 - c if k & 1 else c)
            cp = pltpu.make_async_remote_copy(
                src_ref=slab, dst_ref=land.at[k], send_sem=send_sems.at[k - 1], recv_sem=recv_sems.at[k - 1],
                device_id=to, device_id_type=MESH)
            cp.start()
            sent.append(cp)
        land[0] = slab[...]
        for cp in sent:
            cp.wait_recv()
        for cp in sent:
            cp.wait_send()
        total = land[me]
        for e in range(1, N_DEV):
            total = total + land[me ^ e]
        slab[...] = total

        def mine(r0, n_rows, width):
            got = slab[r0:r0 + n_rows, 0:width]
            for j in range(1, N_CHIPS):
                got = jnp.where(chip == j, slab[r0:r0 + n_rows, j * width:(j + 1) * width], got)
            return got

        def update(i, g, index=()):
            w_ref, m_ref, v_ref = st[3 * i:3 * i + 3]
            o = outs[4 * i:4 * i + 4]
            at = index if index else Ellipsis
            res = (g,) + _adam_math(w_ref[at], g, m_ref[at], v_ref[at])
            for o_ref, val in zip(o, res):
                o_ref[at] = val

        for i, k in enumerate(vec_names):
            width = st[3 * i].shape[1]
            update(i, slab[row0[k]:row0[k] + depth, 0:width])
        base = len(vec_names)
        update(base, slab[row0["final"]:row0["final"] + 1, 0:D])
        update(base + 1, mine(row0["meta"], n_meta, D // N_CHIPS))
        taps = st[3 * (base + 2)].shape[1]
        for l in range(depth):
            update(base + 2, mine(row0["dww"] + l * taps_pad, taps, C // N_CHIPS), (l,))

    flat_g = [g for k in vec_names for g in vec_grads[k]] + list(dww_grads) + [final_grad, meta_grad]
    flat_state = [a for k in names for a in state[k]]
    vmem = pl.BlockSpec(memory_space=pltpu.VMEM)
    out_shape = [jax.ShapeDtypeStruct(state[k][0].shape, F32) for k in names for _ in range(4)]
    res = pl.pallas_call(
        body, name="small_reduce_adam", out_shape=out_shape,
        in_specs=[vmem] * (len(flat_g) + len(flat_state)), out_specs=[vmem] * len(out_shape),
        scratch_shapes=[pltpu.VMEM((rows, lanes), F32), pltpu.VMEM((N_DEV, rows, lanes), F32),
                        pltpu.SemaphoreType.DMA((N_DEV - 1,)), pltpu.SemaphoreType.DMA((N_DEV - 1,))],
    )(*flat_g, *flat_state)
    return {k: tuple(res[4 * i:4 * i + 4]) for i, k in enumerate(names)}


def _adam(w, g, m, v, name):
    def body(w_ref, g_ref, m_ref, v_ref, d_ref, nm_ref, nv_ref):
        d_ref[...], nm_ref[...], nv_ref[...] = _adam_math(w_ref[...], g_ref[...], m_ref[...], v_ref[...])

    lyr, R, C = w.shape
    tr = _pick(R, (512, 704, 256, 128))
    blk = pl.BlockSpec((None, tr, C), lambda l, r: (l, r, 0))
    return pl.pallas_call(
        body, name=name, grid=(lyr, R // tr), in_specs=[blk] * 4, out_specs=[blk] * 3,
        out_shape=[jax.ShapeDtypeStruct(w.shape, F32)] * 3, compiler_params=_params(("parallel", "parallel")),
    )(w, g, m, v)


def _rows(a, pad_to=8):
    r = a.reshape(-1, LANES)
    extra = (-r.shape[0]) % pad_to
    return jnp.pad(r, ((0, extra), (0, 0))) if extra else r


def _pack(arrays):
    return jnp.concatenate([_rows(a) for a in arrays], axis=0)


def _unpack(slab, shapes):
    out, at = [], 0
    for shp in shapes:
        nrow = math.prod(shp) // LANES
        out.append(slab[at:at + nrow].reshape(shp))
        at += nrow + (-nrow) % 8
    return out


BIG = ("w_in", "w_out", "w_gate_t", "w_up_t", "w_down")
BIG_COL_SHARDED = (True, False, False, False, False)
TRANSPOSED = {"w_gate_t": "w_gate", "w_up_t": "w_up"}


def kernel(x, meta_tokens, mix_norm_g, w_in, conv_dw_w, conv_dw_b, conv_ln_g, conv_ln_b, w_out, ffn_norm_g, w_gate, w_up, w_down, final_norm_g, loss_target, m_meta_tokens, m_mix_norm_g, m_w_in, m_conv_dw_w, m_conv_dw_b, m_conv_ln_g, m_conv_ln_b, m_w_out, m_ffn_norm_g, m_w_gate, m_w_up, m_w_down, m_final_norm_g, v_meta_tokens, v_mix_norm_g, v_w_in, v_conv_dw_w, v_conv_dw_b, v_conv_ln_g, v_conv_ln_b, v_w_out, v_ffn_norm_g, v_w_gate, v_w_up, v_w_down, v_final_norm_g):
    n_meta, seq = meta_tokens.shape[0], x.shape[1]
    D = x.shape[2]
    depth, taps, c_shard = conv_dw_w.shape
    C = conv_dw_b.shape[-1]
    chip = (2 * lax.axis_index("x") + lax.axis_index("y")).astype(jnp.int32)
    core = lax.axis_index("c").astype(jnp.int32).reshape(1)
    chip1 = chip.reshape(1)
    where = jnp.concatenate([chip1, core])
    big_w = dict(w_in=w_in, w_out=w_out, w_gate=w_gate, w_up=w_up, w_down=w_down)
    big_m = dict(w_in=m_w_in, w_out=m_w_out, w_gate=m_w_gate, w_up=m_w_up, w_down=m_w_down)
    big_v = dict(w_in=v_w_in, w_out=v_w_out, w_gate=v_w_gate, w_up=v_w_up, w_down=v_w_down)

    small_shard = _pack([conv_dw_w, meta_tokens])[None]
    to_send = {k: jnp.swapaxes(big_w[TRANSPOSED[k]], 1, 2) if k in TRANSPOSED else big_w[k] for k in BIG}
    col = dict(zip(BIG, BIG_COL_SHARDED))
    wts = {k: [_place_shard(to_send[k], l, chip1, col[k], BF16, f"place_{k}_{l}") for l in range(depth)] for k in BIG}
    small_placed = _place_shard(small_shard, 0, chip1, False, F32, "place_small")

    def gather_job(keys, extra=()):
        arrays = [wts[k][l] for k, l in keys] + list(extra)
        shapes = [(1,) + to_send[k].shape[1:] for k, _ in keys] + [(1,) + small_shard.shape[1:]] * len(extra)
        return _gather_job(arrays, shapes, [col[k] for k, _ in keys] + [False] * len(extra))

    first_keys = [("w_in", 0)]
    *first, small_full = _run_job(gather_job(first_keys, [small_placed]), "gather_first")
    for (k, l), arr in zip(first_keys, first):
        wts[k][l] = arr
    behind_keys = {("attn", 0): [("w_out", 0), ("w_gate_t", 0), ("w_up_t", 0)], ("conv", 0): [("w_down", 0)]}
    for l in range(1, depth):
        behind_keys["ffn", l - 1] = [("w_in", l), ("w_gate_t", l)]
        behind_keys["down", l - 1] = [("w_up_t", l)]
        behind_keys["attn", l] = [("w_out", l), ("w_down", l)]
    gather_behind = {host: (gather_job(keys), keys) for host, keys in behind_keys.items()}

    rows_shard = small_shard.shape[1]
    dw_full, meta_full = [], []
    for j in range(N_CHIPS):
        dwj, mj = _unpack(small_full[0, j * rows_shard:(j + 1) * rows_shard],
                          [conv_dw_w.shape, meta_tokens.shape])
        dw_full.append(dwj)
        meta_full.append(mj)
    dw_w_full = jnp.concatenate(dw_full, axis=2)
    meta = jnp.concatenate(meta_full, axis=1)

    L = n_meta + seq
    Lp = -(-L // QUERY_BLOCK) * QUERY_BLOCK
    h0 = jnp.concatenate([meta, x[0], jnp.zeros((Lp - L, D), F32)], axis=0)
    target = jnp.pad(loss_target[0], ((n_meta, Lp - L), (0, 0)))
    reducer = _Reducer(core)
    loss, dh0, grads = _local_step(h0, target, n_meta, seq, (mix_norm_g, ffn_norm_g),
                                   (dw_w_full, conv_dw_b, conv_ln_g, conv_ln_b), wts, final_norm_g,
                                   gather_behind, reducer)
    loss = lax.psum(loss[0, 0], ("x", "y", "c"))
    grad_x = dh0[n_meta:L][None]

    reduced = []
    for k in BIG:
        arr = None
        for l in range(depth):
            arr = _sum_chips(reducer.parts[k, l], reducer.across[k, l], where, l, depth, arr, f"sum_chips_{k}_{l}")
        reduced.append(arr)
    big_g = dict(zip(BIG, _rs_join_halves(reduced)))

    out_g, out_d, out_m, out_v = {}, {}, {}, {}
    for kk in BIG:
        k = TRANSPOSED.get(kk, kk)
        view = (lambda a: jnp.swapaxes(a, 1, 2)) if kk in TRANSPOSED else (lambda a: a)
        res = _adam(view(big_w[k]), big_g[kk], view(big_m[k]), view(big_v[k]), f"adam_{k}")
        out_g[k] = view(big_g[kk])
        out_d[k], out_m[k], out_v[k] = (view(a) for a in res)

    as_row = lambda a: a.reshape(1, -1)
    state = dict(mix_g=(mix_norm_g, m_mix_norm_g, v_mix_norm_g), ffn_g=(ffn_norm_g, m_ffn_norm_g, v_ffn_norm_g),
                 dw_b=(conv_dw_b, m_conv_dw_b, v_conv_dw_b), ln_g=(conv_ln_g, m_conv_ln_g, v_conv_ln_g),
                 ln_b=(conv_ln_b, m_conv_ln_b, v_conv_ln_b),
                 final=(as_row(final_norm_g), as_row(m_final_norm_g), as_row(v_final_norm_g)),
                 meta=(meta_tokens, m_meta_tokens, v_meta_tokens), dww=(conv_dw_w, m_conv_dw_w, v_conv_dw_w))
    vec_names = ("mix_g", "ffn_g", "dw_b", "ln_g", "ln_b")
    small = _small_reduce_adam({k: grads[k] for k in vec_names}, grads["dw_w"], grads["final_g"], dh0[:n_meta], state)
    out_name = dict(mix_g="mix_norm_g", ffn_g="ffn_norm_g", dw_b="conv_dw_b", ln_g="conv_ln_g", ln_b="conv_ln_b",
                    final="final_norm_g", meta="meta_tokens", dww="conv_dw_w")
    for k, res in small.items():
        if k == "final":
            res = tuple(a.reshape(-1) for a in res)
        out_g[out_name[k]], out_d[out_name[k]], out_m[out_name[k]], out_v[out_name[k]] = res

    order = ("meta_tokens", "mix_norm_g", "w_in", "conv_dw_w", "conv_dw_b", "conv_ln_g", "conv_ln_b", "w_out",
             "ffn_norm_g", "w_gate", "w_up", "w_down", "final_norm_g")
    return (loss, grad_x, *[out_g[k] for k in order], *[out_d[k] for k in order],
            *[out_m[k] for k in order], *[out_v[k] for k in order])
```

```python
import functools
import math

import jax
import jax.numpy as jnp
from jax import lax
from jax.experimental import pallas as pl
from jax.experimental.pallas import tpu as pltpu

F32 = jnp.float32
BF16 = jnp.bfloat16
MESH = pl.DeviceIdType.MESH

EPS = 1e-6
QUERY_BLOCK = 128
LANES = 128
HEAD_DIM = 64
LOG_STICK_FLOOR = -40.0
CONV_HALO = 32
N_CHIPS = 4
N_DEV = 8
VMEM_LIMIT = 56 * 1024 * 1024

ADAM_LR = 0.001
ADAM_B1 = 0.9
ADAM_B2 = 0.999
ADAM_EPS = 1e-08
ADAM_WD = 0.01
ADAM_STEP = 10


def _pick(n, prefs):
    for p in prefs:
        if n % p == 0:
            return p
    return n


def _params(sem=None):
    return pltpu.CompilerParams(dimension_semantics=sem, vmem_limit_bytes=VMEM_LIMIT)


def _sigmoid(x):
    return 1.0 / (1.0 + jnp.exp(-x))


def _loss_head(h, g, target, n_meta, seq, name):
    L, D = h.shape
    T = _pick(L, (384, 128))

    def body(h_ref, g_ref, t_ref, loss_ref, dh_ref, dg_ref):
        i = pl.program_id(0)
        x = h_ref[...]
        gv = g_ref[...]
        r = lax.rsqrt(jnp.mean(x * x, axis=-1, keepdims=True) + EPS)
        xh = x * r
        y = xh * gv
        rows = i * T + lax.broadcasted_iota(jnp.int32, (T, 1), 0)
        live = (rows >= n_meta) & (rows < n_meta + seq)
        diff = jnp.where(live, y - t_ref[...], 0.0)
        dyv = diff / D
        dxh = dyv * gv
        dh_ref[...] = r * (dxh - xh * jnp.mean(dxh * xh, axis=-1, keepdims=True))

        @pl.when(i == 0)
        def _():
            dg_ref[...] = jnp.zeros_like(dg_ref)
            loss_ref[...] = jnp.zeros_like(loss_ref)

        dg_ref[...] += jnp.sum(dyv * xh, axis=0, keepdims=True)
        per_row = jnp.mean(diff * diff, axis=-1, keepdims=True)
        loss_ref[...] += 0.5 * jnp.sum(per_row, axis=0, keepdims=True)

    row = pl.BlockSpec((T, D), lambda i: (i, 0))
    vec = pl.BlockSpec((1, D), lambda i: (0, 0))
    one = pl.BlockSpec((1, 1), lambda i: (0, 0))
    return pl.pallas_call(
        body, name=name, grid=(L // T,),
        in_specs=[row, vec, row], out_specs=[one, row, vec],
        out_shape=[jax.ShapeDtypeStruct((1, 1), F32), jax.ShapeDtypeStruct((L, D), F32),
                   jax.ShapeDtypeStruct((1, D), F32)],
        compiler_params=_params(("arbitrary",)),
    )(h, g, target)


def _ffn_tiles(M, F):
    return _pick(M, (352, 384, 128)), _pick(F, (1408, 512, 256, 128))


def _resident(shape):
    return pl.BlockSpec((None,) + tuple(shape[1:]), lambda *_: (0,) * len(shape), pipeline_mode=pl.Buffered(1))


def _norm_in_proj(h, g, w_in, n_qkv, name):
    M, D = h.shape
    N = w_in.shape[2]
    tm = _pick(M, (352, 384, 128))
    tn = _pick(math.gcd(n_qkv, N - n_qkv), (512, 256, 128))

    def body(h_ref, g_ref, w_ref, hn_ref, qkv_ref, rest_ref):
        x = h_ref[...]
        r = lax.rsqrt(jnp.mean(x * x, axis=-1, keepdims=True) + EPS)
        hv = (x * r * g_ref[...]).astype(BF16)
        hn_ref[...] = hv
        for c in range(0, N, tn):
            out = jnp.dot(hv, w_ref[:, c:c + tn], preferred_element_type=F32)
            if c < n_qkv:
                qkv_ref[:, c:c + tn] = out.astype(qkv_ref.dtype)
            else:
                rest_ref[:, c - n_qkv:c - n_qkv + tn] = out

    rows = pl.BlockSpec((tm, D), lambda i: (i, 0))
    return pl.pallas_call(
        body, name=name, grid=(M // tm,),
        in_specs=[rows, pl.BlockSpec((1, D), lambda i: (0, 0)), _resident(w_in.shape)],
        out_specs=[rows, pl.BlockSpec((tm, n_qkv), lambda i: (i, 0)), pl.BlockSpec((tm, N - n_qkv), lambda i: (i, 0))],
        out_shape=[jax.ShapeDtypeStruct((M, D), BF16), jax.ShapeDtypeStruct((M, n_qkv), BF16),
                   jax.ShapeDtypeStruct((M, N - n_qkv), F32)],
        compiler_params=_params(("parallel",)),
    )(h, g, w_in)


def _ffn_up(h, norm_g, w_gate_t, w_up_t, name, job=None):
    M, D = h.shape
    F = w_gate_t.shape[1]
    tm, tf = _ffn_tiles(M, F)
    nt = (((1,), (1,)), ((), ()))

    def body(h_ref, ng_ref, wg_ref, wu_ref, hn_ref, g_ref, u_ref, a_ref):
        x = h_ref[...]
        r = lax.rsqrt(jnp.mean(x * x, axis=-1, keepdims=True) + EPS)
        hv = (x * r * ng_ref[...]).astype(BF16)
        hn_ref[...] = hv
        for c in range(0, F, tf):
            gv = lax.dot_general(hv, wg_ref[c:c + tf, :], nt, preferred_element_type=F32)
            uv = lax.dot_general(hv, wu_ref[c:c + tf, :], nt, preferred_element_type=F32)
            g_ref[:, c:c + tf] = gv.astype(g_ref.dtype)
            u_ref[:, c:c + tf] = uv.astype(u_ref.dtype)
            a_ref[:, c:c + tf] = (gv * _sigmoid(gv) * uv).astype(a_ref.dtype)

    rows = pl.BlockSpec((tm, D), lambda i: (i, 0))
    wide = pl.BlockSpec((tm, F), lambda i: (i, 0))
    return _call_with_job(
        body, job, name=name, grid=(M // tm,),
        in_specs=[rows, pl.BlockSpec((1, D), lambda i: (0, 0)), _resident(w_gate_t.shape), _resident(w_up_t.shape)],
        out_specs=[rows, wide, wide, wide],
        out_shape=[jax.ShapeDtypeStruct((M, D), BF16), jax.ShapeDtypeStruct((M, F), BF16),
                   jax.ShapeDtypeStruct((M, F), BF16), jax.ShapeDtypeStruct((M, F), BF16)],
        scratch_shapes=[], args=(h, norm_g, w_gate_t, w_up_t))


def _ffn_down_bwd(dh, w_down, g, u, name):
    M, D = dh.shape
    F = g.shape[1]
    tm, tf = _ffn_tiles(M, F)

    def body(d_ref, w_ref, g_ref, u_ref, dg_ref, du_ref):
        dhv = d_ref[...].astype(BF16)
        for c in range(0, F, tf):
            dv = lax.dot_general(dhv, w_ref[c:c + tf, :], (((1,), (1,)), ((), ())), preferred_element_type=F32)
            gv = g_ref[:, c:c + tf].astype(F32)
            s = _sigmoid(gv)
            du_ref[:, c:c + tf] = (dv * (gv * s)).astype(du_ref.dtype)
            dg_ref[:, c:c + tf] = (dv * u_ref[:, c:c + tf].astype(F32)
                                   * (s * (1.0 + gv * (1.0 - s)))).astype(dg_ref.dtype)

    wide = pl.BlockSpec((tm, F), lambda i: (i, 0))
    return pl.pallas_call(
        body, name=name, grid=(M // tm,),
        in_specs=[pl.BlockSpec((tm, D), lambda i: (i, 0)), _resident(w_down.shape), wide, wide],
        out_specs=[wide, wide],
        out_shape=[jax.ShapeDtypeStruct((M, F), BF16), jax.ShapeDtypeStruct((M, F), BF16)],
        compiler_params=_params(("parallel",)),
    )(dh, w_down, g, u)


def _mm_rows(pairs, name, transposed=False, residual=None, norm_bwd=None, job=None):
    M = pairs[0][0].shape[0]
    N = pairs[0][1].shape[1 if transposed else 2]
    shallow = sum(p[0].shape[1] for p in pairs) <= 1024
    tm = _pick(M, ((1056,) if shallow else ()) + (352, 384, 128))
    tn = _pick(N, (512, 256, 128))
    n = len(pairs)

    def body(*refs):
        a_refs, w_refs, extra = refs[:n], refs[n:2 * n], refs[2 * n:]
        p_ref = extra[-1] if norm_bwd is not None else extra[-1 if residual is None else 1]
        lhs = [a_ref[...].astype(BF16) for a_ref in a_refs]
        for c in range(0, N, tn):
            acc = None
            for a, w_ref, (_, _, blk) in zip(lhs, w_refs, pairs):
                k = a.shape[1]
                if transposed:
                    d = lax.dot_general(a, w_ref[c:c + tn, blk * k:(blk + 1) * k], (((1,), (1,)), ((), ())),
                                        preferred_element_type=F32)
                else:
                    d = jnp.dot(a, w_ref[blk * k:(blk + 1) * k, c:c + tn], preferred_element_type=F32)
                acc = d if acc is None else acc + d
            if residual is not None:
                acc = acc + extra[0][:, c:c + tn]
            p_ref[:, c:c + tn] = acc
        if norm_bwd is not None:
            h_ref, g_ref, dhin_ref, dh_ref, dg_ref, _ = extra
            x = h_ref[...]
            dyv = p_ref[...]
            r = lax.rsqrt(jnp.mean(x * x, axis=-1, keepdims=True) + EPS)
            xh = x * r
            dxh = dyv * g_ref[...]
            dh_ref[...] = dhin_ref[...] + r * (dxh - xh * jnp.mean(dxh * xh, axis=-1, keepdims=True))

            @pl.when(pl.program_id(0) == 0)
            def _():
                dg_ref[...] = jnp.zeros_like(dg_ref)

            dg_ref[...] += jnp.sum(dyv * xh, axis=0, keepdims=True)

    in_specs = [pl.BlockSpec((tm, a.shape[1]), lambda i: (i, 0)) for a, _, _ in pairs]
    in_specs += [_resident(w.shape) for _, w, _ in pairs]
    args = [p[0] for p in pairs] + [p[1] for p in pairs]
    wide = pl.BlockSpec((tm, N), lambda i: (i, 0))
    vec = pl.BlockSpec((1, N), lambda i: (0, 0))
    out_specs, out_shape, scratch = [wide], [jax.ShapeDtypeStruct((M, N), F32)], []
    if residual is not None:
        in_specs.append(wide)
        args.append(residual)
    if norm_bwd is not None:
        in_specs += [wide, vec, wide]
        args += list(norm_bwd)
        out_specs.append(vec)
        out_shape.append(jax.ShapeDtypeStruct((1, N), F32))
        scratch = [pltpu.VMEM((tm, N), F32)]
    outs, arrived = _call_with_job(body, job, name=name, grid=(M // tm,), in_specs=in_specs, out_specs=out_specs,
                                   out_shape=out_shape, scratch_shapes=scratch, args=args)
    res = tuple(outs) if norm_bwd is not None else (outs[0],)
    res = res if job is None else res + (arrived,)
    return res[0] if len(res) == 1 else res


def _mm_tn_stacked(a0, a1, b, name):
    M, K0 = a0.shape
    assert a1.shape == (M, K0)
    N = b.shape[1]
    tm = _pick(M, (1056, 384, 128))
    tn = _pick(N, (1024, 512, 128))

    def body(a0_ref, a1_ref, b_ref, o_ref):
        k = pl.program_id(0)

        @pl.when(pl.program_id(2) == 0)
        def _():
            o_ref[...] = jnp.zeros_like(o_ref)

        bv = b_ref[...].astype(BF16)
        tn_dims = (((0,), (0,)), ((), ()))

        @pl.when(k == 0)
        def _():
            o_ref[...] += lax.dot_general(a0_ref[...].astype(BF16), bv, tn_dims, preferred_element_type=F32)

        @pl.when(k == 1)
        def _():
            o_ref[...] += lax.dot_general(a1_ref[...].astype(BF16), bv, tn_dims, preferred_element_type=F32)

    a_spec = pl.BlockSpec((tm, K0), lambda k, j, m: (m, 0))
    out = pl.pallas_call(
        body, name=name, grid=(2, N // tn, M // tm),
        in_specs=[a_spec, a_spec, pl.BlockSpec((tm, tn), lambda k, j, m: (m, j))],
        out_specs=pl.BlockSpec((K0, tn), lambda k, j, m: (k, j)),
        out_shape=jax.ShapeDtypeStruct((2 * K0, N), F32),
        compiler_params=_params(("parallel", "parallel", "arbitrary")),
    )(a0, a1, b)
    return out.reshape(N_CHIPS, 2 * K0 // N_CHIPS, N)


def _mm_tn(a, b, name, col_sharded, chips=N_CHIPS):
    M, K = a.shape
    N = b.shape[1]
    tm = _pick(M, (1056, 384, 128))
    tk = _pick(K, (1024, 1408, 512, 256, 128))
    tn = N // N_CHIPS if col_sharded else _pick(N, (1024, 512, 128))

    def body(a_ref, b_ref, o_ref):
        @pl.when(pl.program_id(2) == 0)
        def _():
            o_ref[...] = jnp.zeros_like(o_ref)

        o_ref[...] += lax.dot_general(a_ref[...].astype(BF16), b_ref[...].astype(BF16),
                                      (((0,), (0,)), ((), ())), preferred_element_type=F32)

    if col_sharded:
        out_shape = jax.ShapeDtypeStruct((N_CHIPS, K, tn), F32)
        out_spec = pl.BlockSpec((None, tk, tn), lambda k, j, m: (j, k, 0))
    else:
        out_shape = jax.ShapeDtypeStruct((K, N), F32)
        out_spec = pl.BlockSpec((tk, tn), lambda k, j, m: (k, j))
    out = pl.pallas_call(
        body, name=name, grid=(K // tk, N // tn, M // tm),
        in_specs=[pl.BlockSpec((tm, tk), lambda k, j, m: (m, k)), pl.BlockSpec((tm, tn), lambda k, j, m: (m, j))],
        out_specs=out_spec, out_shape=out_shape,
        compiler_params=_params(("parallel", "parallel", "arbitrary")),
    )(a, b)
    return out if col_sharded else out.reshape(chips, K // chips, N)


def _stack_heads(x, scale=None):
    lane = lax.broadcasted_iota(jnp.int32, x.shape, 1)
    zero = jnp.zeros_like(x)
    lo = jnp.where(lane < HEAD_DIM, x, zero)
    hi = jnp.where(lane < HEAD_DIM, zero, x)
    out = jnp.concatenate([lo, hi], axis=0)
    return out if scale is None else out * scale


def _unstack_heads(x2):
    qb = x2.shape[0] // 2
    lane = lax.broadcasted_iota(jnp.int32, (qb, LANES), 1)
    return jnp.where(lane < HEAD_DIM, x2[:qb], x2[qb:])


ATTN_CHUNK = 32


def _in_lockstep(staged):
    waiting, live = list(staged), []
    while waiting or live:
        if waiting:
            live.append(waiting.pop(0))
        for gen in list(live):
            if next(gen, StopIteration) is StopIteration:
                live.remove(gen)


def _row_chunks():
    return [slice(r, r + ATTN_CHUNK) for r in range(0, 2 * QUERY_BLOCK, ATTN_CHUNK)]


def _chunk_valid(rows, older):
    shape = (rows.stop - rows.start, older + QUERY_BLOCK)
    r = (rows.start + lax.broadcasted_iota(jnp.int32, shape, 0)) & (QUERY_BLOCK - 1)
    return lax.broadcasted_iota(jnp.int32, shape, 1) < r + older


def _split_to(x, hi_ref, lo_ref, rows):
    xh = x.astype(BF16)
    hi_ref[rows, :] = xh
    lo_ref[rows, :] = (x - xh.astype(F32)).astype(BF16)


def _triangle(keys, strict):
    r = lax.broadcasted_iota(jnp.int32, (keys, keys), 0)
    c = lax.broadcasted_iota(jnp.int32, (keys, keys), 1)
    return jnp.where((r > c) if strict else (r >= c), 1.0, 0.0).astype(BF16)


def _fill_attn_consts(tri_ref, mask_ref):
    n = 2 * QUERY_BLOCK
    tri_ref[0] = _triangle(n, True)
    tri_ref[1] = _triangle(n, False)
    mask_ref[...] = jnp.where(_chunk_valid(slice(0, n), QUERY_BLOCK), 1.0, 0.0)


def _valid(mask_ref, rows, older):
    cols = slice(0, 2 * QUERY_BLOCK) if older else slice(QUERY_BLOCK, 2 * QUERY_BLOCK)
    return mask_ref[rows, cols] > 0.5


def _row_total(first):
    lane = lax.broadcasted_iota(jnp.int32, first.shape, 1)
    total = jnp.sum(jnp.where(lane == 0, first, 0.0), axis=1, keepdims=True)
    return jnp.broadcast_to(total, first.shape)


def _pairs_per_step(n_pairs, most):
    return max(g for g in (1, 2, 4) if g <= most and n_pairs % g == 0)


def _lanes(g):
    return slice(g * LANES, (g + 1) * LANES)


def _sweep_older(i, step, carry_ref, first):
    def cond(state):
        n, live = state
        return jnp.logical_and(n < i, live)

    def older(state):
        n, _ = state
        step(i - 1 - n, False)
        return n + 1, jnp.max(carry_ref[...]) > LOG_STICK_FLOOR

    lax.while_loop(cond, older, (first, jnp.max(carry_ref[...]) > LOG_STICK_FLOOR))


class _CommJob:
    def __init__(self, inputs, out_shapes, aliases, n_sems, start, finish):
        self.inputs, self.out_shapes, self.aliases, self.n_sems = list(inputs), list(out_shapes), aliases, n_sems
        self.start, self.finish = start, finish


def _merge_jobs(jobs):
    jobs = [j for j in jobs if j is not None]
    if len(jobs) <= 1:
        return jobs[0] if jobs else None
    spans, aliases = [], {}
    i0 = o0 = s0 = 0
    for j in jobs:
        spans.append((i0, o0, s0))
        aliases.update({i0 + a: o0 + b for a, b in j.aliases.items()})
        i0, o0, s0 = i0 + len(j.inputs), o0 + len(j.out_shapes), s0 + j.n_sems

    def run(which):
        def go(ins, outs, send_sems, recv_sems):
            for j, (i, o, s) in zip(jobs, spans):
                getattr(j, which)(ins[i:i + len(j.inputs)], outs[o:o + len(j.out_shapes)],
                                  send_sems.at[pl.ds(s, j.n_sems)], recv_sems.at[pl.ds(s, j.n_sems)])
        return go

    return _CommJob([a for j in jobs for a in j.inputs], [s for j in jobs for s in j.out_shapes], aliases, s0,
                    run("start"), run("finish"))


def _call_with_job(core_body, job, *, name, grid, in_specs, out_specs, out_shape, scratch_shapes, args):
    sem = ("arbitrary",) * len(grid)
    if job is None:
        res = pl.pallas_call(core_body, name=name, grid=grid, in_specs=in_specs, out_specs=out_specs,
                             out_shape=out_shape, scratch_shapes=scratch_shapes, compiler_params=_params(sem))(*args)
        return list(res), []
    n_in, n_out, n_scr = len(in_specs), len(out_specs), len(scratch_shapes)
    m_in, m_out = len(job.inputs), len(job.out_shapes)

    def body(*refs):
        at = 0
        parts = []
        for count in (n_in, m_in, n_out, m_out, n_scr, 2):
            parts.append(refs[at:at + count])
            at += count
        ins, job_in, outs, job_outs, scratch, (send_sems, recv_sems) = parts
        first = functools.reduce(jnp.logical_and, [pl.program_id(a) == 0 for a in range(len(grid))])
        last = functools.reduce(jnp.logical_and, [pl.program_id(a) == grid[a] - 1 for a in range(len(grid))])

        @pl.when(first)
        def _():
            job.start(job_in, job_outs, send_sems, recv_sems)

        core_body(*ins, *outs, *scratch)

        @pl.when(last)
        def _():
            job.finish(job_in, job_outs, send_sems, recv_sems)

    res = pl.pallas_call(
        body, name=name, grid=grid, in_specs=list(in_specs) + [ANY] * m_in, out_specs=list(out_specs) + [ANY] * m_out,
        out_shape=list(out_shape) + job.out_shapes,
        input_output_aliases={n_in + a: n_out + b for a, b in job.aliases.items()},
        scratch_shapes=list(scratch_shapes) + [pltpu.SemaphoreType.DMA((job.n_sems,)), pltpu.SemaphoreType.DMA((job.n_sems,))],
        compiler_params=_params(sem),
    )(*args, *job.inputs)
    return list(res[:n_out]), list(res[n_out:])


def _run_job(job, name):
    m_in, m_out = len(job.inputs), len(job.out_shapes)

    def body(*refs):
        job_in, job_outs = refs[:m_in], refs[m_in:m_in + m_out]
        send_sems, recv_sems = refs[m_in + m_out:]
        job.start(job_in, job_outs, send_sems, recv_sems)
        job.finish(job_in, job_outs, send_sems, recv_sems)

    return list(pl.pallas_call(
        body, name=name, in_specs=[ANY] * m_in, out_specs=[ANY] * m_out, out_shape=job.out_shapes,
        input_output_aliases=dict(job.aliases),
        scratch_shapes=[pltpu.SemaphoreType.DMA((job.n_sems,)), pltpu.SemaphoreType.DMA((job.n_sems,))],
    )(*job.inputs))


def _attn_fwd(qkv, sb_width, name, job=None):
    L = qkv.shape[0]
    QB = QUERY_BLOCK
    nb = L // QB
    n_pairs = sb_width // LANES
    G = _pairs_per_step(n_pairs, 4)
    W = G * LANES
    scale = 1.0 / math.sqrt(HEAD_DIM)

    def body(q_ref, k_ref, v_ref, o_ref, acc_ref, carry_ref, f32_ref, bf16_ref, tri_ref, mask_ref):
        i = pl.program_id(1)

        @pl.when(i == 0)
        def _():
            _fill_attn_consts(tri_ref, mask_ref)

        q2 = [_stack_heads(q_ref[:, _lanes(g)], scale) for g in range(G)]

        def step(g, j, older, masked):
            n = older + QB
            start = pl.multiple_of(j * QB, QB)
            kb = k_ref[pl.ds(start, n), _lanes(g)]
            vb = v_ref[pl.ds(start, n), _lanes(g)]
            zs, as_, bs = (f32_ref.at[g, t, :, :n] for t in range(3))
            hi, lo = (bf16_ref.at[g, t, :, :n] for t in range(2))
            zs[...] = lax.dot_general(q2[g], kb, (((1,), (1,)), ((), ())), preferred_element_type=F32)
            yield
            for rows in _row_chunks():
                z = zs[rows, :]
                sp = jnp.log(1.0 + jnp.exp(-jnp.abs(z)))
                b = jnp.minimum(-z, 0.0) - sp
                if masked:
                    b = jnp.where(_valid(mask_ref, rows, older),b, 0.0)
                as_[rows, :] = jnp.minimum(z, 0.0) - sp
                bs[rows, :] = b
                _split_to(b, hi, lo, rows)
            yield
            tri = tri_ref[0, :n, :n]
            zs[...] = (jnp.dot(hi[...], tri, preferred_element_type=F32)
                       + jnp.dot(lo[...], tri, preferred_element_type=F32))
            yield
            for rows in _row_chunks():
                excl = zs[rows, :]
                total = _row_total(excl[:, :LANES] + bs[rows, :LANES])
                if masked:
                    carry_ref[g, rows, :] = total
                else:
                    excl = excl + jnp.tile(carry_ref[g, rows, :], (1, n // LANES))
                    carry_ref[g, rows, :] += total
                w = jnp.exp(as_[rows, :] + excl)
                if masked:
                    w = jnp.where(_valid(mask_ref, rows, older),w, 0.0)
                _split_to(w, hi, lo, rows)
            yield
            out = jnp.dot(hi[...], vb, preferred_element_type=F32) + jnp.dot(lo[...], vb, preferred_element_type=F32)
            acc_ref[g] = out if masked else acc_ref[g] + out

        @pl.when(i == 0)
        def _():
            _in_lockstep([step(g, 0, 0, True) for g in range(G)])

        @pl.when(i > 0)
        def _():
            _in_lockstep([step(g, i - 1, QB, True) for g in range(G)])
            for g in range(G):
                _sweep_older(i, lambda j, _, g=g: _in_lockstep([step(g, j, 0, False)]), carry_ref.at[g], 1)

        for g in range(G):
            o_ref[:, _lanes(g)] = _unstack_heads(acc_ref[g])

    n_steps = n_pairs // G
    (out,), job_out = _call_with_job(
        body, job, name=name, grid=(n_steps, nb),
        in_specs=[pl.BlockSpec((QB, W), lambda p, i: (i, p)),
                  pl.BlockSpec((L, W), lambda p, i: (0, n_steps + p)),
                  pl.BlockSpec((L, W), lambda p, i: (0, 2 * n_steps + p))],
        out_specs=[pl.BlockSpec((QB, W), lambda p, i: (i, p))],
        out_shape=[jax.ShapeDtypeStruct((L, sb_width), F32)],
        scratch_shapes=[pltpu.VMEM((G, 2 * QB, LANES), F32), pltpu.VMEM((G, 2 * QB, LANES), F32),
                        pltpu.VMEM((G, 3, 2 * QB, 2 * QB), F32), pltpu.VMEM((G, 2, 2 * QB, 2 * QB), BF16),
                        pltpu.VMEM((2, 2 * QB, 2 * QB), BF16), pltpu.VMEM((2 * QB, 2 * QB), F32)],
        args=(qkv, qkv, qkv))
    return out, job_out


def _attn_bwd(qkv, o, dmixed, sb_width, name, job=None):
    L = qkv.shape[0]
    QB = QUERY_BLOCK
    nb = L // QB
    n_pairs = sb_width // LANES
    G = _pairs_per_step(n_pairs, 2)
    W = G * LANES
    scale = 1.0 / math.sqrt(HEAD_DIM)

    def body(q_ref, k_ref, v_ref, o_ref, do_ref, dq_ref, dk_ref, dv_ref,
             dq_acc, dk_acc, dv_acc, ce_ref, cr_ref, dtot_ref, f32_ref, bf16_ref, tri_ref, mask_ref):
        i = pl.program_id(1)

        @pl.when(i == 0)
        def _():
            dk_acc[...] = jnp.zeros_like(dk_acc)
            dv_acc[...] = jnp.zeros_like(dv_acc)
            _fill_attn_consts(tri_ref, mask_ref)

        q2s = [_stack_heads(q_ref[:, _lanes(g)], scale) for g in range(G)]
        do2s = [_stack_heads(do_ref[:, _lanes(g)].astype(BF16)) for g in range(G)]
        for g in range(G):
            ov = o_ref[:, _lanes(g)]
            per_row = jnp.sum(do2s[g].astype(F32) * jnp.concatenate([ov, ov], axis=0), axis=1, keepdims=True)
            dtot_ref[g] = jnp.broadcast_to(per_row, (2 * QB, LANES))

        def step(g, j, older, masked):
            n = older + QB
            wide = n // LANES
            q2, do2 = q2s[g], do2s[g]
            start = pl.multiple_of(j * QB, QB)
            kb = k_ref[pl.ds(start, n), _lanes(g)]
            vb = v_ref[pl.ds(start, n), _lanes(g)]
            zs, as_, bs, betas, gs = (f32_ref.at[g, t, :, :n] for t in range(5))
            hi, lo, wb = (bf16_ref.at[g, t, :, :n] for t in range(3))
            nt = (((1,), (1,)), ((), ()))
            zs[...] = lax.dot_general(q2, kb, nt, preferred_element_type=F32)
            gs[...] = lax.dot_general(do2, vb, nt, preferred_element_type=F32)
            yield
            for rows in _row_chunks():
                z = zs[rows, :]
                e = jnp.exp(-jnp.abs(z))
                sp = jnp.log(1.0 + e)
                b = jnp.minimum(-z, 0.0) - sp
                if masked:
                    b = jnp.where(_valid(mask_ref, rows, older),b, 0.0)
                rinv = 1.0 / (1.0 + e)
                as_[rows, :] = jnp.minimum(z, 0.0) - sp
                bs[rows, :] = b
                betas[rows, :] = jnp.where(z >= 0.0, rinv, e * rinv)
                _split_to(b, hi, lo, rows)
            yield
            tri = tri_ref[0, :n, :n]
            zs[...] = (jnp.dot(hi[...], tri, preferred_element_type=F32)
                       + jnp.dot(lo[...], tri, preferred_element_type=F32))
            yield
            for rows in _row_chunks():
                excl = zs[rows, :]
                total = _row_total(excl[:, :LANES] + bs[rows, :LANES])
                if masked:
                    ce_ref[g, rows, :] = total
                else:
                    excl = excl + jnp.tile(ce_ref[g, rows, :], (1, wide))
                    ce_ref[g, rows, :] += total
                w = jnp.exp(as_[rows, :] + excl)
                if masked:
                    w = jnp.where(_valid(mask_ref, rows, older),w, 0.0)
                wb[rows, :] = w.astype(BF16)
                gw = w * gs[rows, :]
                gs[rows, :] = gw
                _split_to(gw, hi, lo, rows)
            yield
            tri = tri_ref[1, :n, :n]
            zs[...] = (jnp.dot(hi[...], tri, preferred_element_type=F32)
                       + jnp.dot(lo[...], tri, preferred_element_type=F32))
            yield
            for rows in _row_chunks():
                rinc = zs[rows, :]
                total = _row_total(rinc[:, :LANES])
                if masked:
                    cr_ref[g, rows, :] = total
                else:
                    rinc = rinc + jnp.tile(cr_ref[g, rows, :], (1, wide))
                    cr_ref[g, rows, :] += total
                beta = betas[rows, :]
                dz = gs[rows, :] * (1.0 - beta) - beta * (jnp.tile(dtot_ref[g, rows, :], (1, wide)) - rinc)
                if masked:
                    dz = jnp.where(_valid(mask_ref, rows, older),dz, 0.0)
                hi[rows, :] = dz.astype(BF16)
            yield
            dzb = hi[...]
            dq = jnp.dot(dzb, kb, preferred_element_type=F32)
            dq_acc[g] = dq if masked else dq_acc[g] + dq
            dk_acc[pl.ds(start, n), _lanes(g)] += lax.dot_general(
                dzb, q2, (((0,), (0,)), ((), ())), preferred_element_type=F32)
            dv_acc[pl.ds(start, n), _lanes(g)] += lax.dot_general(
                wb[...], do2, (((0,), (0,)), ((), ())), preferred_element_type=F32)

        @pl.when(i == 0)
        def _():
            _in_lockstep([step(g, 0, 0, True) for g in range(G)])

        @pl.when(i > 0)
        def _():
            _in_lockstep([step(g, i - 1, QB, True) for g in range(G)])
            for g in range(G):
                _sweep_older(i, lambda j, _, g=g: _in_lockstep([step(g, j, 0, False)]), ce_ref.at[g], 1)

        for g in range(G):
            dq_ref[:, _lanes(g)] = (_unstack_heads(dq_acc[g]) * scale).astype(dq_ref.dtype)

        @pl.when(i == nb - 1)
        def _():
            dk_ref[...] = dk_acc[...].astype(dk_ref.dtype)
            dv_ref[...] = dv_acc[...].astype(dv_ref.dtype)

    n_steps = n_pairs // G
    blk = pl.BlockSpec((QB, W), lambda p, i: (i, p))
    col = pl.BlockSpec((L, W), lambda p, i: (0, p))
    return _call_with_job(
        body, job, name=name, grid=(n_steps, nb),
        in_specs=[blk,
                  pl.BlockSpec((L, W), lambda p, i: (0, n_steps + p)),
                  pl.BlockSpec((L, W), lambda p, i: (0, 2 * n_steps + p)),
                  blk, blk],
        out_specs=[blk, col, col],
        out_shape=[jax.ShapeDtypeStruct((L, sb_width), BF16)] * 3,
        scratch_shapes=[pltpu.VMEM((G, 2 * QB, LANES), F32), pltpu.VMEM((L, W), F32),
                        pltpu.VMEM((L, W), F32), pltpu.VMEM((G, 2 * QB, LANES), F32),
                        pltpu.VMEM((G, 2 * QB, LANES), F32), pltpu.VMEM((G, 2 * QB, LANES), F32),
                        pltpu.VMEM((G, 5, 2 * QB, 2 * QB), F32), pltpu.VMEM((G, 3, 2 * QB, 2 * QB), BF16),
                        pltpu.VMEM((2, 2 * QB, 2 * QB), BF16), pltpu.VMEM((2 * QB, 2 * QB), F32)],
        args=(qkv, qkv, qkv, o, dmixed))


def _conv_tile(L):
    return _pick(L, (384, 128))


def _glu(x, C):
    return x[:, :C] * _sigmoid(x[:, C:])


CONV_CHUNK = 32
SHIFT_TAIL = 24


def _fill_shifted(src_ref, dst_ref):
    n = dst_ref.shape[1]
    for r in range(1, 8):
        dst_ref[r - 1] = src_ref[r:r + n, :]


def _rows_at(src_ref, shifted_ref, start, n):
    q, r = divmod(start, 8)
    if r == 0:
        return src_ref[start:start + n, :]
    return shifted_ref[r - 1, 8 * q:8 * q + n, :]


def _conv_fwd(cacg, dw_w, dw_b, ln_g, ln_b, name, job=None):
    L, C2 = cacg.shape
    C = C2 // 2
    T = _conv_tile(L)
    H = CONV_HALO
    K = dw_w.shape[0]
    CH = CONV_CHUNK

    def body(x_ref, prev_ref, w_ref, b_ref, g_ref, beta_ref, o_ref, y_ref, u_ref, us_ref):
        i = pl.program_id(0)
        u_ref[0:H, :] = jnp.where(i > 0, _glu(prev_ref[...], C), 0.0)
        u_ref[H:, :] = _glu(x_ref[...], C)
        _fill_shifted(u_ref, us_ref)
        for c0 in range(0, T, CH):
            y = jnp.broadcast_to(b_ref[...], (CH, C))
            for k in range(K):
                y = y + w_ref[k:k + 1, :] * _rows_at(u_ref, us_ref, c0 + H - (K - 1) + k, CH)
            y_ref[c0:c0 + CH, :] = y
            mu = jnp.mean(y, axis=-1, keepdims=True)
            yc = y - mu
            rstd = lax.rsqrt(jnp.mean(yc * yc, axis=-1, keepdims=True) + EPS)
            ln = yc * rstd * g_ref[...] + beta_ref[...]
            o_ref[c0:c0 + CH, :] = (ln * _sigmoid(ln)).astype(o_ref.dtype)

    vec = pl.BlockSpec((1, C), lambda i: (0, 0))
    tile = pl.BlockSpec((T, C), lambda i: (i, 0))
    (out, y), arrived = _call_with_job(
        body, job, name=name, grid=(L // T,),
        in_specs=[pl.BlockSpec((T, C2), lambda i: (i, 0)),
                  pl.BlockSpec((H, C2), lambda i: (jnp.maximum(i * (T // H) - 1, 0), 0)),
                  pl.BlockSpec((K, C), lambda i: (0, 0)), vec, vec, vec],
        out_specs=[tile, tile],
        out_shape=[jax.ShapeDtypeStruct((L, C), BF16), jax.ShapeDtypeStruct((L, C), F32)],
        scratch_shapes=[pltpu.VMEM((T + H, C), F32), pltpu.VMEM((7, T + SHIFT_TAIL, C), F32)],
        args=(cacg, cacg, dw_w, dw_b, ln_g, ln_b))
    return out, y, arrived


def _conv_bwd(cacg, y, dmixed, dw_w, ln_g, ln_b, name, job=None):
    L, C2 = cacg.shape
    C = C2 // 2
    T = _conv_tile(L)
    H = CONV_HALO
    K = dw_w.shape[0]
    nt = L // T
    TE = T + H

    CH = CONV_CHUNK

    def body(x_ref, prev_ref, y_ref, ynext_ref, d_ref, dnext_ref, w_ref, g_ref, beta_ref,
             dca_ref, dcg_ref, dwt_ref, db_ref, dg_ref, dbeta_ref, u_ref, us_ref, dy_ref, dys_ref):
        i = pl.program_id(0)
        last = i == nt - 1

        @pl.when(i == 0)
        def _():
            dwt_ref[...] = jnp.zeros_like(dwt_ref)
            db_ref[...] = jnp.zeros_like(db_ref)
            dg_ref[...] = jnp.zeros_like(dg_ref)
            dbeta_ref[...] = jnp.zeros_like(dbeta_ref)

        u_ref[0:H, :] = jnp.where(i > 0, _glu(prev_ref[...], C), 0.0)
        u_ref[H:, :] = _glu(x_ref[...], C)
        _fill_shifted(u_ref, us_ref)
        dg_acc = jnp.zeros((1, C), F32)
        dbeta_acc = jnp.zeros((1, C), F32)
        db_acc = jnp.zeros((1, C), F32)
        for c0 in range(0, TE, CH):
            y = y_ref[c0:c0 + CH, :] if c0 < T else ynext_ref[c0 - T:c0 - T + CH, :]
            mu = jnp.mean(y, axis=-1, keepdims=True)
            yc = y - mu
            rstd = lax.rsqrt(jnp.mean(yc * yc, axis=-1, keepdims=True) + EPS)
            yh = yc * rstd
            ln = yh * g_ref[...] + beta_ref[...]
            s = _sigmoid(ln)
            dout = d_ref[c0:c0 + CH, :] if c0 < T else jnp.where(last, 0.0, dnext_ref[c0 - T:c0 - T + CH, :])
            dln = dout * (s * (1.0 + ln * (1.0 - s)))
            dyh = dln * g_ref[...]
            dy = rstd * (dyh - jnp.mean(dyh, axis=-1, keepdims=True)
                         - yh * jnp.mean(dyh * yh, axis=-1, keepdims=True))
            dy_ref[c0:c0 + CH, :] = dy
            if c0 < T:
                dg_acc = dg_acc + jnp.sum(dln * yh, axis=0, keepdims=True)
                dbeta_acc = dbeta_acc + jnp.sum(dln, axis=0, keepdims=True)
                db_acc = db_acc + jnp.sum(dy, axis=0, keepdims=True)
        dg_ref[...] += dg_acc
        dbeta_ref[...] += dbeta_acc
        db_ref[...] += db_acc
        _fill_shifted(dy_ref, dys_ref)
        for k in range(K):
            dwt_ref[k:k + 1, :] += jnp.sum(
                dy_ref[0:T, :] * _rows_at(u_ref, us_ref, H - (K - 1) + k, T), axis=0, keepdims=True)
        for c0 in range(0, T, CH):
            du = jnp.zeros((CH, C), F32)
            for k in range(K):
                du = du + w_ref[k:k + 1, :] * _rows_at(dy_ref, dys_ref, c0 + (K - 1) - k, CH)
            x = x_ref[c0:c0 + CH, :]
            sg = _sigmoid(x[:, C:])
            dca_ref[c0:c0 + CH, :] = (du * sg).astype(dca_ref.dtype)
            dcg_ref[c0:c0 + CH, :] = (du * x[:, :C] * sg * (1.0 - sg)).astype(dcg_ref.dtype)

    nh = L // H
    vec = pl.BlockSpec((1, C), lambda i: (0, 0))
    row = pl.BlockSpec((T, C), lambda i: (i, 0))
    after = lambda i: jnp.minimum((i + 1) * (T // H), nh - 1)
    return _call_with_job(
        body, job, name=name, grid=(nt,),
        in_specs=[pl.BlockSpec((T, C2), lambda i: (i, 0)),
                  pl.BlockSpec((H, C2), lambda i: (jnp.maximum(i * (T // H) - 1, 0), 0)),
                  row, pl.BlockSpec((H, C), lambda i: (after(i), 0)),
                  pl.BlockSpec((T, C), lambda i: (i, 1)), pl.BlockSpec((H, C), lambda i: (after(i), 1)),
                  pl.BlockSpec((K, C), lambda i: (0, 0)), vec, vec],
        out_specs=[row, row, pl.BlockSpec((H, C), lambda i: (0, 0)), vec, vec, vec],
        out_shape=[jax.ShapeDtypeStruct((L, C), BF16), jax.ShapeDtypeStruct((L, C), BF16),
                   jax.ShapeDtypeStruct((H, C), F32), jax.ShapeDtypeStruct((1, C), F32),
                   jax.ShapeDtypeStruct((1, C), F32), jax.ShapeDtypeStruct((1, C), F32)],
        scratch_shapes=[pltpu.VMEM((T + H, C), F32), pltpu.VMEM((7, T + SHIFT_TAIL, C), F32),
                        pltpu.VMEM((TE, C), F32), pltpu.VMEM((7, T + SHIFT_TAIL, C), F32)],
        args=(cacg, cacg, y, y, dmixed, dmixed, dw_w, ln_g, ln_b))


def _local_step(h0, target, n_meta, seq, norms, conv_p, wts, final_g, gather_behind, reducer):
    mix_g, ffn_g = norms
    dw_w, dw_b, ln_g, ln_b = conv_p
    depth = mix_g.shape[0]
    C = dw_b.shape[-1]
    sbw = (wts["w_in"][0].shape[-1] - 2 * C) // 3
    assert sbw == C, "the mixer halves must have equal width"
    row = lambda a, i: a[i][None, :]

    h = h0
    saved = []
    for i in range(depth):
        hn, proj_qkv, cacg = _norm_in_proj(h, row(mix_g, i), wts["w_in"][i], 3 * sbw, f"in_proj_{i}")
        def hosting(kind):
            job, keys = gather_behind.get((kind, i), (None, ()))

            def sink(arrived):
                for (wname, wl), arr in zip(keys, arrived):
                    wts[wname][wl] = arr

            return job, sink

        job, sink = hosting("attn")
        attn, arrived = _attn_fwd(proj_qkv, sbw, f"attn_fwd_{i}", job)
        sink(arrived)
        job, sink = hosting("conv")
        conv, conv_y, arrived = _conv_fwd(cacg, dw_w[i], row(dw_b, i), row(ln_g, i), row(ln_b, i),
                                          f"conv_fwd_{i}", job)
        sink(arrived)
        h_mid = _mm_rows([(attn, wts["w_out"][i], 0), (conv, wts["w_out"][i], 1)], f"out_proj_{i}", residual=h)
        job, sink = hosting("ffn")
        (hn2, g, u, act), arrived = _ffn_up(h_mid, row(ffn_g, i), wts["w_gate_t"][i], wts["w_up_t"][i],
                                            f"ffn_up_{i}", job)
        sink(arrived)
        job, sink = hosting("down")
        h_out = _mm_rows([(act, wts["w_down"][i], 0)], f"down_{i}", residual=h_mid, job=job)
        if job is not None:
            h_out, arrived = h_out
            sink(arrived)
        saved.append((h, hn, proj_qkv, cacg, attn, conv, conv_y, h_mid, hn2, g, u, act))
        h = h_out

    loss, dh, d_final_g = _loss_head(h, final_g[None, :], target, n_meta, seq, "loss_head")

    grads = {k: [None] * depth for k in ("mix_g", "ffn_g", "dw_w", "dw_b", "ln_g", "ln_b")}
    for i in reversed(range(depth)):
        h_in, hn, proj_qkv, cacg, attn, conv, conv_y, h_mid, hn2, g, u, act = saved[i]
        big = {}
        dg, du = _ffn_down_bwd(dh, wts["w_down"][i], g, u, f"ffn_down_bwd_{i}")
        big["w_down"] = _mm_tn(act, dh, f"dw_down_{i}", col_sharded=False)
        big["w_gate_t"] = _mm_tn(dg, hn2, f"dw_gate_{i}", col_sharded=False)
        big["w_up_t"] = _mm_tn(du, hn2, f"dw_up_{i}", col_sharded=False)
        sib_job, sib_sink = reducer.to_sibling(i, big)
        dh, d_ffn, arrived = _mm_rows([(dg, wts["w_gate_t"][i], 0), (du, wts["w_up_t"][i], 0)], f"d_hn2_{i}",
                                      norm_bwd=(h_mid, row(ffn_g, i), dh), job=sib_job)
        sib_sink(arrived)
        dmixed = _mm_rows([(dh, wts["w_out"][i], 0)], f"d_mixed_{i}", transposed=True)
        dw_out = _mm_tn_stacked(attn, conv, dh, f"dw_out_{i}")
        sib_job, sib_sink = reducer.to_sibling(i, {"w_out": dw_out})
        x_job, x_sink = reducer.take()
        (dq, dk, dv), arrived = _attn_bwd(proj_qkv, attn, dmixed, sbw, f"attn_bwd_{i}",
                                          _merge_jobs([sib_job, x_job]))
        sib_sink(arrived[:len(sib_job.out_shapes)])
        x_sink(arrived[len(sib_job.out_shapes):])
        x_job, x_sink = reducer.take()
        (dca, dcg, d_dw, d_b, d_lg, d_lb), arrived = _conv_bwd(
            cacg, conv_y, dmixed, dw_w[i], row(ln_g, i), row(ln_b, i), f"conv_bwd_{i}", x_job)
        x_sink(arrived)
        grads["dw_w"][i] = d_dw
        grads["dw_b"][i], grads["ln_g"][i], grads["ln_b"][i] = d_b, d_lg, d_lb
        dproj = jnp.concatenate([dq, dk, dv, dca, dcg], axis=1)
        reducer.ready(i, {"w_in": _mm_tn(hn, dproj, f"dw_in_{i}", col_sharded=True)})
        x_job, x_sink = reducer.take() if i == 0 else (None, None)
        res = _mm_rows([(dproj, wts["w_in"][i], 0)], f"d_hn_{i}", transposed=True,
                       norm_bwd=(h_in, row(mix_g, i), dh), job=x_job)
        dh, d_mix = res[:2]
        if x_job is not None:
            x_sink(res[2])
        grads["mix_g"][i], grads["ffn_g"][i] = d_mix, d_ffn
    grads["final_g"] = d_final_g
    return loss, dh, grads


ANY = pl.BlockSpec(memory_space=pl.ANY)


def _position():
    return lax.axis_index("x"), lax.axis_index("y"), lax.axis_index("c")


def _chip_at(x, y, k):
    return (1 - x if k & 2 else x), (1 - y if k & 1 else y)


def _half_rows(ref, half, rows, base=0):
    start = pl.multiple_of(base + half * rows, 8)
    lead = (slice(None),) * (len(ref.shape) - 2)
    return ref.at[(*lead, pl.ds(start, rows), slice(None))]


def _gather_job(fulls, shard_shapes, col_sharded):
    n = len(fulls)

    def tools(f_refs, send_sems, recv_sems):
        def block(wi, chip, half):
            _, R, C = shard_shapes[wi]
            if col_sharded[wi]:
                cols = pl.ds(pl.multiple_of(chip * C, LANES), C)
                return f_refs[wi].at[:, pl.ds(pl.multiple_of(half * (R // 2), 8), R // 2), cols]
            return _half_rows(f_refs[wi], half, R // 2, base=chip * R)

        def copy(wi, slot, blk, to):
            return pltpu.make_async_remote_copy(
                src_ref=blk, dst_ref=blk, send_sem=send_sems.at[6 * wi + slot],
                recv_sem=recv_sems.at[6 * wi + slot], device_id=to, device_id_type=MESH)

        return block, copy

    def start(_, f_refs, send_sems, recv_sems):
        block, copy = tools(f_refs, send_sems, recv_sems)
        x, y, c = _position()
        me = 2 * x + y
        for wi in range(n):
            for k in (1, 2, 3):
                copy(wi, k - 1, block(wi, me, c), (*_chip_at(x, y, k), c)).start()

    def finish(_, f_refs, send_sems, recv_sems):
        block, copy = tools(f_refs, send_sems, recv_sems)
        x, y, c = _position()
        me = 2 * x + y
        for wi in range(n):
            for k in (1, 2, 3):
                landed = block(wi, me ^ k, c)
                copy(wi, k - 1, landed, (x, y, c)).wait_recv()
                copy(wi, 2 + k, landed, (x, y, 1 - c)).start()
        for wi in range(n):
            for k in (1, 2, 3):
                copy(wi, 2 + k, block(wi, me ^ k, 1 - c), (x, y, c)).wait_recv()
        for wi in range(n):
            for k in (1, 2, 3):
                copy(wi, k - 1, block(wi, me, c), (x, y, c)).wait_send()
                copy(wi, 2 + k, block(wi, me ^ k, c), (x, y, c)).wait_send()

    return _CommJob(fulls, [jax.ShapeDtypeStruct(f.shape, f.dtype) for f in fulls], {i: i for i in range(n)},
                    6 * n, start, finish)


def _place_shard(w, layer, chip, col_sharded, dtype, name):
    _, R, C = w.shape
    tr = _pick(R, (512, 704, 256, 128, 48))
    nr = R // tr

    def body(chip_ref, w_ref, o_ref):
        o_ref[...] = w_ref[...].astype(dtype)

    if col_sharded:
        shape = (1, R, N_CHIPS * C)
        out_spec = pl.BlockSpec((None, tr, C), lambda r, chip_ref: (0, r, chip_ref[0]))
    else:
        shape = (1, N_CHIPS * R, C)
        out_spec = pl.BlockSpec((None, tr, C), lambda r, chip_ref: (0, chip_ref[0] * nr + r, 0))
    grid_spec = pltpu.PrefetchScalarGridSpec(
        num_scalar_prefetch=1, grid=(nr,),
        in_specs=[pl.BlockSpec((None, tr, C), lambda r, chip_ref: (layer, r, 0))], out_specs=out_spec)
    return pl.pallas_call(
        body, name=name, grid_spec=grid_spec, out_shape=jax.ShapeDtypeStruct(shape, dtype),
        compiler_params=_params(("parallel",)),
    )(chip, w)


def _sibling_job(grads):
    n = len(grads)

    def copies(g_refs, l_refs, send_sems, recv_sems):
        x, y, c = _position()
        return [pltpu.make_async_remote_copy(
            src_ref=_half_rows(g_refs[wi], 1 - c, grads[wi].shape[1] // 2), dst_ref=l_refs[wi],
            send_sem=send_sems.at[wi], recv_sem=recv_sems.at[wi],
            device_id=(x, y, 1 - c), device_id_type=MESH) for wi in range(n)]

    def start(*refs):
        for cp in copies(*refs):
            cp.start()

    def finish(*refs):
        for cp in copies(*refs):
            cp.wait()

    outs = [jax.ShapeDtypeStruct((g.shape[0], g.shape[1] // 2, g.shape[2]), g.dtype) for g in grads]
    return _CommJob(grads, outs, {}, n, start, finish)


def _chip_sum(g, landed, core, name):
    _, R, C = g.shape
    hr = R // 2
    tr = _pick(hr, (512, 352, 256, 128))
    nr = hr // tr

    def body(c_ref, g_ref, l_ref, o_ref):
        o_ref[...] = (g_ref[...] + l_ref[...]).astype(BF16)

    grid_spec = pltpu.PrefetchScalarGridSpec(
        num_scalar_prefetch=1, grid=(N_CHIPS, nr),
        in_specs=[pl.BlockSpec((None, tr, C), lambda j, r, c_ref: (j, c_ref[0] * nr + r, 0)),
                  pl.BlockSpec((None, tr, C), lambda j, r, c_ref: (j, r, 0))],
        out_specs=pl.BlockSpec((None, tr, C), lambda j, r, c_ref: (j, r, 0)))
    return pl.pallas_call(
        body, name=name, grid_spec=grid_spec, out_shape=jax.ShapeDtypeStruct((N_CHIPS, hr, C), BF16),
        compiler_params=_params(("parallel", "parallel")),
    )(core, g, landed)


def _across_job(parts):
    n = len(parts)

    def copy(p_refs, l_refs, send_sems, recv_sems, wi, k, to):
        x, y, _ = _position()
        me = 2 * x + y
        return pltpu.make_async_remote_copy(
            src_ref=p_refs[wi].at[me ^ k], dst_ref=l_refs[wi].at[me],
            send_sem=send_sems.at[3 * wi + k - 1], recv_sem=recv_sems.at[3 * wi + k - 1],
            device_id=to, device_id_type=MESH)

    def start(p_refs, l_refs, send_sems, recv_sems):
        x, y, c = _position()
        for wi in range(n):
            for k in (1, 2, 3):
                copy(p_refs, l_refs, send_sems, recv_sems, wi, k, (*_chip_at(x, y, k), c)).start()

    def finish(p_refs, l_refs, send_sems, recv_sems):
        x, y, c = _position()
        me = 2 * x + y
        for wi in range(n):
            for k in (1, 2, 3):
                slot = l_refs[wi].at[me ^ k]
                pltpu.make_async_remote_copy(
                    src_ref=slot, dst_ref=slot, send_sem=send_sems.at[3 * wi + k - 1],
                    recv_sem=recv_sems.at[3 * wi + k - 1], device_id=(x, y, c), device_id_type=MESH).wait_recv()
        for wi in range(n):
            for k in (1, 2, 3):
                copy(p_refs, l_refs, send_sems, recv_sems, wi, k, (x, y, c)).wait_send()

    return _CommJob(parts, [jax.ShapeDtypeStruct(p.shape, p.dtype) for p in parts], {}, 3 * n, start, finish)


class _Reducer:
    def __init__(self, core):
        self.core, self.parts, self.across, self.pending = core, {}, {}, []

    def to_sibling(self, layer, big):
        names = list(big)
        flat = [big[k] for k in names]

        def sink(landed):
            for k, g, la in zip(names, flat, landed):
                self.parts[k, layer] = _chip_sum(g, la, self.core, f"chip_sum_{k}_{layer}")
                self.pending.append((k, layer))

        return _sibling_job(flat), sink

    def ready(self, layer, big):
        job, sink = self.to_sibling(layer, big)
        sink(_run_job(job, f"grads_to_sibling_{next(iter(big))}_{layer}"))

    def take(self):
        keys, self.pending = self.pending, []
        if not keys:
            return None, lambda results: None

        def sink(results):
            self.across.update(zip(keys, results))

        return _across_job([self.parts[key] for key in keys]), sink


def _sum_chips(parts, landed, where, layer, depth, prev, name):
    _, hr, C = landed.shape
    tr = _pick(hr, (512, 352, 256, 128))
    nr = hr // tr

    def body(*refs):
        own_ref, slots, o_ref = refs[1], refs[2:2 + N_CHIPS], refs[-1]
        chip = refs[0][0]
        total = None
        for q in range(N_CHIPS):
            term = jnp.where(chip == q, own_ref[...], slots[q][...]).astype(F32)
            total = term if total is None else total + term
        o_ref[...] = total

    def slot_spec(q):
        return pl.BlockSpec((None, tr, C), lambda r, w: (jnp.where(w[0] == q, (q + 1) % N_CHIPS, q), r, 0))

    in_specs = [pl.BlockSpec((None, tr, C), lambda r, w: (w[0], r, 0))] + [slot_spec(q) for q in range(N_CHIPS)]
    args = [where, parts] + [landed] * N_CHIPS
    aliases = {}
    if prev is not None:
        in_specs.append(ANY)
        args.append(prev)
        aliases = {len(args) - 1: 0}
    grid_spec = pltpu.PrefetchScalarGridSpec(
        num_scalar_prefetch=1, grid=(nr,), in_specs=in_specs,
        out_specs=pl.BlockSpec((None, tr, C), lambda r, w: (layer, w[1] * nr + r, 0)))
    return pl.pallas_call(
        body, name=name, grid_spec=grid_spec, out_shape=jax.ShapeDtypeStruct((depth, 2 * hr, C), F32),
        input_output_aliases=aliases, compiler_params=_params(("parallel",)),
    )(*args)


def _rs_join_halves(reduced):
    n = len(reduced)

    def body(*refs):
        o_refs = refs[n:2 * n]
        send_sems, recv_sems = refs[2 * n:]
        x, y, c = _position()
        sent = []
        for wi in range(n):
            hr = reduced[wi].shape[1] // 2
            mine = _half_rows(o_refs[wi], c, hr)
            cp = pltpu.make_async_remote_copy(
                src_ref=mine, dst_ref=mine, send_sem=send_sems.at[wi], recv_sem=recv_sems.at[wi],
                device_id=(x, y, 1 - c), device_id_type=MESH)
            cp.start()
            sent.append(cp)
        for wi in range(n):
            hr = reduced[wi].shape[1] // 2
            theirs = _half_rows(o_refs[wi], 1 - c, hr)
            pltpu.make_async_remote_copy(
                src_ref=theirs, dst_ref=theirs, send_sem=send_sems.at[wi], recv_sem=recv_sems.at[wi],
                device_id=(x, y, c), device_id_type=MESH).wait_recv()
        for cp in sent:
            cp.wait_send()

    return pl.pallas_call(
        body, name="grads_join_halves", out_shape=[jax.ShapeDtypeStruct(r.shape, r.dtype) for r in reduced],
        in_specs=[ANY] * n, out_specs=[ANY] * n, input_output_aliases={i: i for i in range(n)},
        scratch_shapes=[pltpu.SemaphoreType.DMA((n,)), pltpu.SemaphoreType.DMA((n,))],
    )(*reduced)


def _adam_math(w, g, m, v):
    m = ADAM_B1 * m + (1.0 - ADAM_B1) * g
    v = ADAM_B2 * v + (1.0 - ADAM_B2) * jnp.square(g)
    m_hat = m / (1.0 - ADAM_B1 ** ADAM_STEP)
    v_hat = v / (1.0 - ADAM_B2 ** ADAM_STEP)
    delta = -ADAM_LR * (m_hat / (jnp.sqrt(v_hat) + ADAM_EPS) + ADAM_WD * w)
    return delta, m, v


def _small_reduce_adam(vec_grads, dww_grads, final_grad, meta_grad, state):
    vec_names = list(vec_grads)
    depth = len(dww_grads)
    D = final_grad.shape[1]
    n_meta = meta_grad.shape[0]
    taps_pad, C = dww_grads[0].shape
    names = vec_names + ["final", "meta", "dww"]
    at, row0 = 0, {}
    for k in vec_names:
        row0[k] = at
        at += depth
    row0["final"] = at
    at = -(-(at + 1) // 8) * 8
    row0["meta"] = at
    at += -(-n_meta // 8) * 8
    row0["dww"] = at
    rows = at + depth * taps_pad
    lanes = max(D, C)
    n_g = len(vec_names) * depth + depth + 2

    def body(*refs):
        g_refs = refs[:n_g]
        st = refs[n_g:n_g + 3 * len(names)]
        outs = refs[n_g + 3 * len(names):n_g + 7 * len(names)]
        slab, land, send_sems, recv_sems = refs[n_g + 7 * len(names):]
        x, y, c = _position()
        me = 4 * x + 2 * y + c
        chip = 2 * x + y
        slab[...] = jnp.zeros_like(slab)
        it = iter(g_refs)
        for k in vec_names:
            for l in range(depth):
                g_ref = next(it)
                slab[row0[k] + l:row0[k] + l + 1, 0:g_ref.shape[1]] = g_ref[...]
        for l in range(depth):
            slab[row0["dww"] + l * taps_pad:row0["dww"] + (l + 1) * taps_pad, 0:C] = next(it)[...]
        slab[row0["final"]:row0["final"] + 1, 0:D] = next(it)[...]
        slab[row0["meta"]:row0["meta"] + n_meta, 0:D] = next(it)[...]
        sent = []
        for k in range(1, N_DEV):
            to = (1 - x if k & 4 else x, 1 - y if k & 2 else y, 1 - c if k & 1 else c)
            cp = pltpu.make_async_remote_copy(
                src_ref=slab, dst_ref=land.at[k], send_sem=send_sems.at[k - 1], recv_sem=recv_sems.at[k - 1],
                device_id=to, device_id_type=MESH)
            cp.start()
            sent.append(cp)
        land[0] = slab[...]
        for cp in sent:
            cp.wait_recv()
        for cp in sent:
            cp.wait_send()
        total = land[me]
        for e in range(1, N_DEV):
            total = total + land[me ^ e]
        slab[...] = total

        def mine(r0, n_rows, width):
            got = slab[r0:r0 + n_rows, 0:width]
            for j in range(1, N_CHIPS):
                got = jnp.where(chip == j, slab[r0:r0 + n_rows, j * width:(j + 1) * width], got)
            return got

        def update(i, g, index=()):
            w_ref, m_ref, v_ref = st[3 * i:3 * i + 3]
            o = outs[4 * i:4 * i + 4]
            at = index if index else Ellipsis
            res = (g,) + _adam_math(w_ref[at], g, m_ref[at], v_ref[at])
            for o_ref, val in zip(o, res):
                o_ref[at] = val

        for i, k in enumerate(vec_names):
            width = st[3 * i].shape[1]
            update(i, slab[row0[k]:row0[k] + depth, 0:width])
        base = len(vec_names)
        update(base, slab[row0["final"]:row0["final"] + 1, 0:D])
        update(base + 1, mine(row0["meta"], n_meta, D // N_CHIPS))
        taps = st[3 * (base + 2)].shape[1]
        for l in range(depth):
            update(base + 2, mine(row0["dww"] + l * taps_pad, taps, C // N_CHIPS), (l,))

    flat_g = [g for k in vec_names for g in vec_grads[k]] + list(dww_grads) + [final_grad, meta_grad]
    flat_state = [a for k in names for a in state[k]]
    vmem = pl.BlockSpec(memory_space=pltpu.VMEM)
    out_shape = [jax.ShapeDtypeStruct(state[k][0].shape, F32) for k in names for _ in range(4)]
    res = pl.pallas_call(
        body, name="small_reduce_adam", out_shape=out_shape,
        in_specs=[vmem] * (len(flat_g) + len(flat_state)), out_specs=[vmem] * len(out_shape),
        scratch_shapes=[pltpu.VMEM((rows, lanes), F32), pltpu.VMEM((N_DEV, rows, lanes), F32),
                        pltpu.SemaphoreType.DMA((N_DEV - 1,)), pltpu.SemaphoreType.DMA((N_DEV - 1,))],
    )(*flat_g, *flat_state)
    return {k: tuple(res[4 * i:4 * i + 4]) for i, k in enumerate(names)}


def _adam(w, g, m, v, name):
    def body(w_ref, g_ref, m_ref, v_ref, d_ref, nm_ref, nv_ref):
        d_ref[...], nm_ref[...], nv_ref[...] = _adam_math(w_ref[...], g_ref[...], m_ref[...], v_ref[...])

    lyr, R, C = w.shape
    tr = _pick(R, (512, 704, 256, 128))
    blk = pl.BlockSpec((None, tr, C), lambda l, r: (l, r, 0))
    return pl.pallas_call(
        body, name=name, grid=(lyr, R // tr), in_specs=[blk] * 4, out_specs=[blk] * 3,
        out_shape=[jax.ShapeDtypeStruct(w.shape, F32)] * 3, compiler_params=_params(("parallel", "parallel")),
    )(w, g, m, v)


def _rows(a, pad_to=8):
    r = a.reshape(-1, LANES)
    extra = (-r.shape[0]) % pad_to
    return jnp.pad(r, ((0, extra), (0, 0))) if extra else r


def _pack(arrays):
    return jnp.concatenate([_rows(a) for a in arrays], axis=0)


def _unpack(slab, shapes):
    out, at = [], 0
    for shp in shapes:
        nrow = math.prod(shp) // LANES
        out.append(slab[at:at + nrow].reshape(shp))
        at += nrow + (-nrow) % 8
    return out


BIG = ("w_in", "w_out", "w_gate_t", "w_up_t", "w_down")
BIG_COL_SHARDED = (True, False, False, False, False)
TRANSPOSED = {"w_gate_t": "w_gate", "w_up_t": "w_up"}


def kernel(x, meta_tokens, mix_norm_g, w_in, conv_dw_w, conv_dw_b, conv_ln_g, conv_ln_b, w_out, ffn_norm_g, w_gate, w_up, w_down, final_norm_g, loss_target, m_meta_tokens, m_mix_norm_g, m_w_in, m_conv_dw_w, m_conv_dw_b, m_conv_ln_g, m_conv_ln_b, m_w_out, m_ffn_norm_g, m_w_gate, m_w_up, m_w_down, m_final_norm_g, v_meta_tokens, v_mix_norm_g, v_w_in, v_conv_dw_w, v_conv_dw_b, v_conv_ln_g, v_conv_ln_b, v_w_out, v_ffn_norm_g, v_w_gate, v_w_up, v_w_down, v_final_norm_g):
    n_meta, seq = meta_tokens.shape[0], x.shape[1]
    D = x.shape[2]
    depth, taps, c_shard = conv_dw_w.shape
    C = conv_dw_b.shape[-1]
    chip = (2 * lax.axis_index("x") + lax.axis_index("y")).astype(jnp.int32)
    core = lax.axis_index("c").astype(jnp.int32).reshape(1)
    chip1 = chip.reshape(1)
    where = jnp.concatenate([chip1, core])
    big_w = dict(w_in=w_in, w_out=w_out, w_gate=w_gate, w_up=w_up, w_down=w_down)
    big_m = dict(w_in=m_w_in, w_out=m_w_out, w_gate=m_w_gate, w_up=m_w_up, w_down=m_w_down)
    big_v = dict(w_in=v_w_in, w_out=v_w_out, w_gate=v_w_gate, w_up=v_w_up, w_down=v_w_down)

    small_shard = _pack([conv_dw_w, meta_tokens])[None]
    to_send = {k: jnp.swapaxes(big_w[TRANSPOSED[k]], 1, 2) if k in TRANSPOSED else big_w[k] for k in BIG}
    col = dict(zip(BIG, BIG_COL_SHARDED))
    wts = {k: [_place_shard(to_send[k], l, chip1, col[k], BF16, f"place_{k}_{l}") for l in range(depth)] for k in BIG}
    small_placed = _place_shard(small_shard, 0, chip1, False, F32, "place_small")

    def gather_job(keys, extra=()):
        arrays = [wts[k][l] for k, l in keys] + list(extra)
        shapes = [(1,) + to_send[k].shape[1:] for k, _ in keys] + [(1,) + small_shard.shape[1:]] * len(extra)
        return _gather_job(arrays, shapes, [col[k] for k, _ in keys] + [False] * len(extra))

    first_keys = [("w_in", 0)]
    *first, small_full = _run_job(gather_job(first_keys, [small_placed]), "gather_first")
    for (k, l), arr in zip(first_keys, first):
        wts[k][l] = arr
    behind_keys = {("attn", 0): [("w_out", 0), ("w_gate_t", 0), ("w_up_t", 0)], ("conv", 0): [("w_down", 0)]}
    for l in range(1, depth):
        behind_keys["ffn", l - 1] = [("w_in", l), ("w_gate_t", l)]
        behind_keys["down", l - 1] = [("w_up_t", l)]
        behind_keys["attn", l] = [("w_out", l), ("w_down", l)]
    gather_behind = {host: (gather_job(keys), keys) for host, keys in behind_keys.items()}

    rows_shard = small_shard.shape[1]
    dw_full, meta_full = [], []
    for j in range(N_CHIPS):
        dwj, mj = _unpack(small_full[0, j * rows_shard:(j + 1) * rows_shard],
                          [conv_dw_w.shape, meta_tokens.shape])
        dw_full.append(dwj)
        meta_full.append(mj)
    dw_w_full = jnp.concatenate(dw_full, axis=2)
    meta = jnp.concatenate(meta_full, axis=1)

    L = n_meta + seq
    Lp = -(-L // QUERY_BLOCK) * QUERY_BLOCK
    h0 = jnp.concatenate([meta, x[0], jnp.zeros((Lp - L, D), F32)], axis=0)
    target = jnp.pad(loss_target[0], ((n_meta, Lp - L), (0, 0)))
    reducer = _Reducer(core)
    loss, dh0, grads = _local_step(h0, target, n_meta, seq, (mix_norm_g, ffn_norm_g),
                                   (dw_w_full, conv_dw_b, conv_ln_g, conv_ln_b), wts, final_norm_g,
                                   gather_behind, reducer)
    loss = lax.psum(loss[0, 0], ("x", "y", "c"))
    grad_x = dh0[n_meta:L][None]

    reduced = []
    for k in BIG:
        arr = None
        for l in range(depth):
            arr = _sum_chips(reducer.parts[k, l], reducer.across[k, l], where, l, depth, arr, f"sum_chips_{k}_{l}")
        reduced.append(arr)
    big_g = dict(zip(BIG, _rs_join_halves(reduced)))

    out_g, out_d, out_m, out_v = {}, {}, {}, {}
    for kk in BIG:
        k = TRANSPOSED.get(kk, kk)
        view = (lambda a: jnp.swapaxes(a, 1, 2)) if kk in TRANSPOSED else (lambda a: a)
        res = _adam(view(big_w[k]), big_g[kk], view(big_m[k]), view(big_v[k]), f"adam_{k}")
        out_g[k] = view(big_g[kk])
        out_d[k], out_m[k], out_v[k] = (view(a) for a in res)

    as_row = lambda a: a.reshape(1, -1)
    state = dict(mix_g=(mix_norm_g, m_mix_norm_g, v_mix_norm_g), ffn_g=(ffn_norm_g, m_ffn_norm_g, v_ffn_norm_g),
                 dw_b=(conv_dw_b, m_conv_dw_b, v_conv_dw_b), ln_g=(conv_ln_g, m_conv_ln_g, v_conv_ln_g),
                 ln_b=(conv_ln_b, m_conv_ln_b, v_conv_ln_b),
                 final=(as_row(final_norm_g), as_row(m_final_norm_g), as_row(v_final_norm_g)),
                 meta=(meta_tokens, m_meta_tokens, v_meta_tokens), dww=(conv_dw_w, m_conv_dw_w, v_conv_dw_w))
    vec_names = ("mix_g", "ffn_g", "dw_b", "ln_g", "ln_b")
    small = _small_reduce_adam({k: grads[k] for k in vec_names}, grads["dw_w"], grads["final_g"], dh0[:n_meta], state)
    out_name = dict(mix_g="mix_norm_g", ffn_g="ffn_norm_g", dw_b="conv_dw_b", ln_g="conv_ln_g", ln_b="conv_ln_b",
                    final="final_norm_g", meta="meta_tokens", dww="conv_dw_w")
    for k, res in small.items():
        if k == "final":
            res = tuple(a.reshape(-1) for a in res)
        out_g[out_name[k]], out_d[out_name[k]], out_m[out_name[k]], out_v[out_name[k]] = res

    order = ("meta_tokens", "mix_norm_g", "w_in", "conv_dw_w", "conv_dw_b", "conv_ln_g", "conv_ln_b", "w_out",
             "ffn_norm_g", "w_gate", "w_up", "w_down", "final_norm_g")
    return (loss, grad_x, *[out_g[k] for k in order], *[out_d[k] for k in order],
            *[out_m[k] for k in order], *[out_v[k] for k in order])
```
